```python
import jax, jax.numpy as jnp
from jax import lax
import numpy as np

D_MODEL = 1024
BATCH = 8
SEQ = 4096
DEPTH = 2

N_MEM = 256
HEAD_DIM = 64
N_Q_HEADS = 8
N_KV_HEADS = 2
ATTN_WIDTH = N_Q_HEADS * HEAD_DIM
KV_WIDTH = N_KV_HEADS * HEAD_DIM
CONV_CH = D_MODEL - ATTN_WIDTH
MIX_WIDTH = ATTN_WIDTH + CONV_CH
IN_COLS = ATTN_WIDTH + 2 * KV_WIDTH + 2 * CONV_CH
CONV_K = 31
WINDOW = 128
BLOCK = 128
N_X_HEADS = 4
X_HEAD_DIM = D_MODEL // N_X_HEADS
D_FF = (((8 * D_MODEL + 2) // 3 + 255) // 256) * 256
EPS = 1e-6
NEG = -1e30

kernel_name = "hybrid_swa_sink_conformer_memxattn"


def rmsnorm(x, g):
    xf = x.astype(jnp.float32)
    y = xf * lax.rsqrt(jnp.mean(xf * xf, axis=-1, keepdims=True) + EPS)
    return (y * g.astype(jnp.float32)).astype(x.dtype)


def layernorm(x, g, b):
    xf = x.astype(jnp.float32)
    mu = jnp.mean(xf, axis=-1, keepdims=True)
    xc = xf - mu
    y = xc * lax.rsqrt(jnp.mean(xc * xc, axis=-1, keepdims=True) + EPS)
    return (y * g.astype(jnp.float32) + b.astype(jnp.float32)).astype(x.dtype)


def alibi_slopes(n):
    return 2.0 ** (-8.0 * (jnp.arange(n, dtype=jnp.float32) + 1.0) / n)


def sliding_window_attention(q, k, v, sinks):
    B, S, H, hd = q.shape
    G = H // N_KV_HEADS
    nb = S // BLOCK
    qb = q.reshape(B, nb, BLOCK, N_KV_HEADS, G, hd)
    kb = k.reshape(B, nb, BLOCK, N_KV_HEADS, hd)
    vb = v.reshape(B, nb, BLOCK, N_KV_HEADS, hd)
    pad = ((0, 0), (1, 0), (0, 0), (0, 0), (0, 0))
    kk = jnp.concatenate([jnp.pad(kb[:, :-1], pad), kb], axis=2)
    vv = jnp.concatenate([jnp.pad(vb[:, :-1], pad), vb], axis=2)
    scores = jnp.einsum('bnqkgd,bnskd->bnkgqs', qb, kk).astype(jnp.float32) * (hd ** -0.5)
    dist = (jnp.arange(BLOCK)[:, None] + BLOCK - jnp.arange(2 * BLOCK)[None, :])
    valid = (dist >= 0) & (dist < WINDOW)
    exists = (jnp.arange(nb)[:, None, None] > 0) | (jnp.arange(2 * BLOCK)[None, None, :] >= BLOCK)
    valid = valid[None] & exists
    slopes = alibi_slopes(H).reshape(N_KV_HEADS, G)
    bias = -slopes[:, :, None, None] * dist.astype(jnp.float32)
    scores = jnp.where(valid[None, :, None, None], scores + bias, NEG)
    sink = sinks.astype(jnp.float32).reshape(N_KV_HEADS, G)[None, None, :, :, None, None]
    m = jnp.maximum(jnp.max(scores, axis=-1, keepdims=True), sink)
    p = jnp.exp(scores - m)
    probs = p / (jnp.sum(p, axis=-1, keepdims=True) + jnp.exp(sink - m))
    out = jnp.einsum('bnkgqs,bnskd->bnqkgd', probs.astype(v.dtype), vv)
    return out.reshape(B, S, H * hd)


def conformer_conv(c, conv_w, conv_b, ln_g, ln_b):
    val, gate = jnp.split(c, 2, axis=-1)
    g = val * jax.nn.sigmoid(gate)
    y = lax.conv_general_dilated(
        g, conv_w[:, None, :].astype(g.dtype), window_strides=(1,),
        padding=[(CONV_K - 1, 0)], dimension_numbers=('NWC', 'WIO', 'NWC'),
        feature_group_count=CONV_CH) + conv_b
    y = layernorm(y, ln_g, ln_b)
    return jax.nn.silu(y)


def memory_cross_attention(h, mem_n, wq, wkv, qg, kg, wo):
    B, S, _ = h.shape
    q = jnp.einsum('bsd,de->bse', h, wq).reshape(B, S, N_X_HEADS, X_HEAD_DIM)
    kv = jnp.einsum('bmd,de->bme', mem_n, wkv)
    k, v = jnp.split(kv, 2, axis=-1)
    k = k.reshape(B, -1, N_X_HEADS, X_HEAD_DIM)
    v = v.reshape(B, -1, N_X_HEADS, X_HEAD_DIM)
    q = rmsnorm(q, qg)
    k = rmsnorm(k, kg)
    s = jnp.einsum('bshd,bmhd->bhsm', q, k).astype(jnp.float32) * (X_HEAD_DIM ** -0.5)
    p = jax.nn.softmax(s, axis=-1).astype(v.dtype)
    o = jnp.einsum('bhsm,bmhd->bshd', p, v).reshape(B, S, D_MODEL)
    return jnp.einsum('bse,ed->bsd', o, wo)


def swiglu(h, w_gate_up, w_down):
    gu = jnp.einsum('bsd,df->bsf', h, w_gate_up)
    g, u = jnp.split(gu, 2, axis=-1)
    return jnp.einsum('bsf,fd->bsd', jax.nn.silu(g) * u, w_down)


def _fwd_setup_inputs(seed: int = 0) -> dict:
    key = jax.random.key(seed)
    ks = jax.random.split(key, 24)
    f = jnp.float32
    nrm = lambda k, shape, scale: (jax.random.normal(k, shape, f) * scale).astype(f)
    gain = lambda k, shape: (1.0 + 0.02 * jax.random.normal(k, shape, f)).astype(f)
    L = DEPTH
    return {
        "x": jax.random.normal(ks[0], (BATCH, SEQ, D_MODEL), f),
        "mem": jax.random.normal(ks[1], (BATCH, N_MEM, D_MODEL), f),
        "norm_mix_g": gain(ks[2], (L, D_MODEL)),
        "w_in": nrm(ks[3], (L, D_MODEL, IN_COLS), D_MODEL ** -0.5),
        "q_norm_g": gain(ks[4], (L, HEAD_DIM)),
        "k_norm_g": gain(ks[5], (L, HEAD_DIM)),
        "sinks": nrm(ks[6], (L, N_Q_HEADS), 0.5),
        "conv_w": nrm(ks[7], (L, CONV_K, CONV_CH), CONV_K ** -0.5),
        "conv_b": nrm(ks[8], (L, CONV_CH), 0.01),
        "conv_ln_g": gain(ks[9], (L, CONV_CH)),
        "conv_ln_b": nrm(ks[10], (L, CONV_CH), 0.01),
        "w_out": nrm(ks[11], (L, MIX_WIDTH, D_MODEL), MIX_WIDTH ** -0.5),
        "norm_x_g": gain(ks[12], (L, D_MODEL)),
        "norm_mem_g": gain(ks[13], (L, D_MODEL)),
        "wq_x": nrm(ks[14], (L, D_MODEL, D_MODEL), D_MODEL ** -0.5),
        "wkv_x": nrm(ks[15], (L, D_MODEL, 2 * D_MODEL), D_MODEL ** -0.5),
        "xq_norm_g": gain(ks[16], (L, X_HEAD_DIM)),
        "xk_norm_g": gain(ks[17], (L, X_HEAD_DIM)),
        "wo_x": nrm(ks[18], (L, D_MODEL, D_MODEL), D_MODEL ** -0.5),
        "norm_ffn_g": gain(ks[19], (L, D_MODEL)),
        "w_gate_up": nrm(ks[20], (L, D_MODEL, 2 * D_FF), D_MODEL ** -0.5),
        "w_down": nrm(ks[21], (L, D_FF, D_MODEL), D_FF ** -0.5),
    }


def _fwd_reference(x, mem, norm_mix_g, w_in, q_norm_g, k_norm_g, sinks, conv_w, conv_b,
              conv_ln_g, conv_ln_b, w_out, norm_x_g, norm_mem_g, wq_x, wkv_x,
              xq_norm_g, xk_norm_g, wo_x, norm_ffn_g, w_gate_up, w_down):
    B, S, _ = x.shape
    for l in range(DEPTH):
        h = rmsnorm(x, norm_mix_g[l])
        u = jnp.einsum('bsd,dp->bsp', h, w_in[l])
        q = u[..., :ATTN_WIDTH].reshape(B, S, N_Q_HEADS, HEAD_DIM)
        k = u[..., ATTN_WIDTH:ATTN_WIDTH + KV_WIDTH].reshape(B, S, N_KV_HEADS, HEAD_DIM)
        v = u[..., ATTN_WIDTH + KV_WIDTH:ATTN_WIDTH + 2 * KV_WIDTH].reshape(B, S, N_KV_HEADS, HEAD_DIM)
        c = u[..., ATTN_WIDTH + 2 * KV_WIDTH:]
        q = rmsnorm(q, q_norm_g[l])
        k = rmsnorm(k, k_norm_g[l])
        attn = sliding_window_attention(q, k, v, sinks[l])
        conv = conformer_conv(c, conv_w[l], conv_b[l], conv_ln_g[l], conv_ln_b[l])
        mixed = jnp.concatenate([attn, conv], axis=-1)
        x = x + jnp.einsum('bse,ed->bsd', mixed, w_out[l])
        hx = rmsnorm(x, norm_x_g[l])
        mem_n = rmsnorm(mem, norm_mem_g[l])
        x = x + memory_cross_attention(hx, mem_n, wq_x[l], wkv_x[l], xq_norm_g[l], xk_norm_g[l], wo_x[l])
        hf = rmsnorm(x, norm_ffn_g[l])
        x = x + swiglu(hf, w_gate_up[l], w_down[l])
    return x


import jax as _jax
import jax.numpy as _jnp

TWIN_FORMAT = 'train_step'
FWD_PARAMS = ['x', 'mem', 'norm_mix_g', 'w_in', 'q_norm_g', 'k_norm_g', 'sinks', 'conv_w', 'conv_b', 'conv_ln_g', 'conv_ln_b', 'w_out', 'norm_x_g', 'norm_mem_g', 'wq_x', 'wkv_x', 'xq_norm_g', 'xk_norm_g', 'wo_x', 'norm_ffn_g', 'w_gate_up', 'w_down']
TWIN_WEIGHTS = ['norm_mix_g', 'w_in', 'q_norm_g', 'k_norm_g', 'sinks', 'conv_w', 'conv_b', 'conv_ln_g', 'conv_ln_b', 'w_out', 'norm_x_g', 'norm_mem_g', 'wq_x', 'wkv_x', 'xq_norm_g', 'xk_norm_g', 'wo_x', 'norm_ffn_g', 'w_gate_up', 'w_down']
TWIN_DIFF_INPUT = 'x'
TWIN_INPUTS = ['x', 'mem', 'norm_mix_g', 'w_in', 'q_norm_g', 'k_norm_g', 'sinks', 'conv_w', 'conv_b', 'conv_ln_g', 'conv_ln_b', 'w_out', 'norm_x_g', 'norm_mem_g', 'wq_x', 'wkv_x', 'xq_norm_g', 'xk_norm_g', 'wo_x', 'norm_ffn_g', 'w_gate_up', 'w_down', 'loss_target', 'm_norm_mix_g', 'm_w_in', 'm_q_norm_g', 'm_k_norm_g', 'm_sinks', 'm_conv_w', 'm_conv_b', 'm_conv_ln_g', 'm_conv_ln_b', 'm_w_out', 'm_norm_x_g', 'm_norm_mem_g', 'm_wq_x', 'm_wkv_x', 'm_xq_norm_g', 'm_xk_norm_g', 'm_wo_x', 'm_norm_ffn_g', 'm_w_gate_up', 'm_w_down', 'v_norm_mix_g', 'v_w_in', 'v_q_norm_g', 'v_k_norm_g', 'v_sinks', 'v_conv_w', 'v_conv_b', 'v_conv_ln_g', 'v_conv_ln_b', 'v_w_out', 'v_norm_x_g', 'v_norm_mem_g', 'v_wq_x', 'v_wkv_x', 'v_xq_norm_g', 'v_xk_norm_g', 'v_wo_x', 'v_norm_ffn_g', 'v_w_gate_up', 'v_w_down']
TWIN_OUTPUTS = ['loss', 'grad_x', 'grad_norm_mix_g', 'grad_w_in', 'grad_q_norm_g', 'grad_k_norm_g', 'grad_sinks', 'grad_conv_w', 'grad_conv_b', 'grad_conv_ln_g', 'grad_conv_ln_b', 'grad_w_out', 'grad_norm_x_g', 'grad_norm_mem_g', 'grad_wq_x', 'grad_wkv_x', 'grad_xq_norm_g', 'grad_xk_norm_g', 'grad_wo_x', 'grad_norm_ffn_g', 'grad_w_gate_up', 'grad_w_down', 'delta_norm_mix_g', 'delta_w_in', 'delta_q_norm_g', 'delta_k_norm_g', 'delta_sinks', 'delta_conv_w', 'delta_conv_b', 'delta_conv_ln_g', 'delta_conv_ln_b', 'delta_w_out', 'delta_norm_x_g', 'delta_norm_mem_g', 'delta_wq_x', 'delta_wkv_x', 'delta_xq_norm_g', 'delta_xk_norm_g', 'delta_wo_x', 'delta_norm_ffn_g', 'delta_w_gate_up', 'delta_w_down', 'new_m_norm_mix_g', 'new_m_w_in', 'new_m_q_norm_g', 'new_m_k_norm_g', 'new_m_sinks', 'new_m_conv_w', 'new_m_conv_b', 'new_m_conv_ln_g', 'new_m_conv_ln_b', 'new_m_w_out', 'new_m_norm_x_g', 'new_m_norm_mem_g', 'new_m_wq_x', 'new_m_wkv_x', 'new_m_xq_norm_g', 'new_m_xk_norm_g', 'new_m_wo_x', 'new_m_norm_ffn_g', 'new_m_w_gate_up', 'new_m_w_down', 'new_v_norm_mix_g', 'new_v_w_in', 'new_v_q_norm_g', 'new_v_k_norm_g', 'new_v_sinks', 'new_v_conv_w', 'new_v_conv_b', 'new_v_conv_ln_g', 'new_v_conv_ln_b', 'new_v_w_out', 'new_v_norm_x_g', 'new_v_norm_mem_g', 'new_v_wq_x', 'new_v_wkv_x', 'new_v_xq_norm_g', 'new_v_xk_norm_g', 'new_v_wo_x', 'new_v_norm_ffn_g', 'new_v_w_gate_up', 'new_v_w_down']
TWIN_LEAF_KINDS = {'loss': 'loss', 'grad_x': 'grad_x', 'grad_norm_mix_g': 'grad_w', 'grad_w_in': 'grad_w', 'grad_q_norm_g': 'grad_w', 'grad_k_norm_g': 'grad_w', 'grad_sinks': 'grad_w', 'grad_conv_w': 'grad_w', 'grad_conv_b': 'grad_w', 'grad_conv_ln_g': 'grad_w', 'grad_conv_ln_b': 'grad_w', 'grad_w_out': 'grad_w', 'grad_norm_x_g': 'grad_w', 'grad_norm_mem_g': 'grad_w', 'grad_wq_x': 'grad_w', 'grad_wkv_x': 'grad_w', 'grad_xq_norm_g': 'grad_w', 'grad_xk_norm_g': 'grad_w', 'grad_wo_x': 'grad_w', 'grad_norm_ffn_g': 'grad_w', 'grad_w_gate_up': 'grad_w', 'grad_w_down': 'grad_w', 'delta_norm_mix_g': 'delta_w', 'delta_w_in': 'delta_w', 'delta_q_norm_g': 'delta_w', 'delta_k_norm_g': 'delta_w', 'delta_sinks': 'delta_w', 'delta_conv_w': 'delta_w', 'delta_conv_b': 'delta_w', 'delta_conv_ln_g': 'delta_w', 'delta_conv_ln_b': 'delta_w', 'delta_w_out': 'delta_w', 'delta_norm_x_g': 'delta_w', 'delta_norm_mem_g': 'delta_w', 'delta_wq_x': 'delta_w', 'delta_wkv_x': 'delta_w', 'delta_xq_norm_g': 'delta_w', 'delta_xk_norm_g': 'delta_w', 'delta_wo_x': 'delta_w', 'delta_norm_ffn_g': 'delta_w', 'delta_w_gate_up': 'delta_w', 'delta_w_down': 'delta_w', 'new_m_norm_mix_g': 'new_m', 'new_m_w_in': 'new_m', 'new_m_q_norm_g': 'new_m', 'new_m_k_norm_g': 'new_m', 'new_m_sinks': 'new_m', 'new_m_conv_w': 'new_m', 'new_m_conv_b': 'new_m', 'new_m_conv_ln_g': 'new_m', 'new_m_conv_ln_b': 'new_m', 'new_m_w_out': 'new_m', 'new_m_norm_x_g': 'new_m', 'new_m_norm_mem_g': 'new_m', 'new_m_wq_x': 'new_m', 'new_m_wkv_x': 'new_m', 'new_m_xq_norm_g': 'new_m', 'new_m_xk_norm_g': 'new_m', 'new_m_wo_x': 'new_m', 'new_m_norm_ffn_g': 'new_m', 'new_m_w_gate_up': 'new_m', 'new_m_w_down': 'new_m', 'new_v_norm_mix_g': 'new_v', 'new_v_w_in': 'new_v', 'new_v_q_norm_g': 'new_v', 'new_v_k_norm_g': 'new_v', 'new_v_sinks': 'new_v', 'new_v_conv_w': 'new_v', 'new_v_conv_b': 'new_v', 'new_v_conv_ln_g': 'new_v', 'new_v_conv_ln_b': 'new_v', 'new_v_w_out': 'new_v', 'new_v_norm_x_g': 'new_v', 'new_v_norm_mem_g': 'new_v', 'new_v_wq_x': 'new_v', 'new_v_wkv_x': 'new_v', 'new_v_xq_norm_g': 'new_v', 'new_v_xk_norm_g': 'new_v', 'new_v_wo_x': 'new_v', 'new_v_norm_ffn_g': 'new_v', 'new_v_w_gate_up': 'new_v', 'new_v_w_down': 'new_v'}


def _forward(args):
    return _fwd_reference(*[args[k] for k in FWD_PARAMS])


def _output_shape():
    out = _jax.eval_shape(lambda: _forward(_fwd_setup_inputs(0)))
    return out.shape, out.dtype

N_MICROBATCH = 1
ADAM_LR = 0.001
ADAM_B1 = 0.9
ADAM_B2 = 0.999
ADAM_EPS = 1e-08
ADAM_WD = 0.01
ADAM_STEP = 10
PER_EXAMPLE_BATCH_AXIS = {'x': 0, 'mem': 0, 'loss_target': 0}
SHARED_INPUTS = []
_WEIGHT_DTYPES = {'norm_mix_g': _jnp.float32, 'w_in': _jnp.float32, 'q_norm_g': _jnp.float32, 'k_norm_g': _jnp.float32, 'sinks': _jnp.float32, 'conv_w': _jnp.float32, 'conv_b': _jnp.float32, 'conv_ln_g': _jnp.float32, 'conv_ln_b': _jnp.float32, 'w_out': _jnp.float32, 'norm_x_g': _jnp.float32, 'norm_mem_g': _jnp.float32, 'wq_x': _jnp.float32, 'wkv_x': _jnp.float32, 'xq_norm_g': _jnp.float32, 'xk_norm_g': _jnp.float32, 'wo_x': _jnp.float32, 'norm_ffn_g': _jnp.float32, 'w_gate_up': _jnp.float32, 'w_down': _jnp.float32}
MOMENT_SCALE = {'norm_mix_g': 1.298569e+00, 'w_in': 4.435036e-01, 'q_norm_g': 9.444192e+00, 'k_norm_g': 9.435457e+00, 'sinks': 3.174968e+01, 'conv_w': 6.679092e-01, 'conv_b': 1.001087e+01, 'conv_ln_g': 1.513049e+01, 'conv_ln_b': 1.114334e+01, 'w_out': 1.526294e+00, 'norm_x_g': 8.102285e-02, 'norm_mem_g': 5.002516e-01, 'wq_x': 7.659745e-02, 'wkv_x': 2.120500e-01, 'xq_norm_g': 1.198585e+00, 'xk_norm_g': 1.194892e+00, 'wo_x': 3.070644e-01, 'norm_ffn_g': 2.447943e+01, 'w_gate_up': 3.425892e-01, 'w_down': 3.948165e-01}


def _to_microbatches(a, axis):
    t = _jnp.moveaxis(a, axis, 0)
    t = t.reshape((N_MICROBATCH, t.shape[0] // N_MICROBATCH) + t.shape[1:])
    return _jnp.moveaxis(t, 1, axis + 1)


def setup_inputs(seed: int = 0) -> dict:
    inp = _fwd_setup_inputs(seed)
    key = _jax.random.fold_in(_jax.random.key(seed), 7919)
    shape, _ = _output_shape()
    out = dict(inp)
    out["loss_target"] = _jax.random.normal(_jax.random.fold_in(key, 0), shape, _jnp.float32)
    for i, name in enumerate(TWIN_WEIGHTS):
        w = inp[name].astype(_jnp.float32)
        if MOMENT_SCALE is None:
            s = _jnp.sqrt(_jnp.mean(_jnp.square(w)) + 1e-30)
        else:
            s = MOMENT_SCALE[name]
        km, kv = _jax.random.split(_jax.random.fold_in(key, i + 1))
        out[name] = w
        out["m_" + name] = s * _jax.random.normal(km, w.shape, _jnp.float32)
        out["v_" + name] = (s * s) * _jax.random.uniform(kv, w.shape, _jnp.float32, 0.5, 1.5)
    if N_MICROBATCH > 1:
        for name, axis in PER_EXAMPLE_BATCH_AXIS.items():
            out[name] = _to_microbatches(out[name], axis)
    return {'x': out['x'], 'mem': out['mem'], 'norm_mix_g': out['norm_mix_g'], 'w_in': out['w_in'], 'q_norm_g': out['q_norm_g'], 'k_norm_g': out['k_norm_g'], 'sinks': out['sinks'], 'conv_w': out['conv_w'], 'conv_b': out['conv_b'], 'conv_ln_g': out['conv_ln_g'], 'conv_ln_b': out['conv_ln_b'], 'w_out': out['w_out'], 'norm_x_g': out['norm_x_g'], 'norm_mem_g': out['norm_mem_g'], 'wq_x': out['wq_x'], 'wkv_x': out['wkv_x'], 'xq_norm_g': out['xq_norm_g'], 'xk_norm_g': out['xk_norm_g'], 'wo_x': out['wo_x'], 'norm_ffn_g': out['norm_ffn_g'], 'w_gate_up': out['w_gate_up'], 'w_down': out['w_down'], 'loss_target': out['loss_target'], 'm_norm_mix_g': out['m_norm_mix_g'], 'm_w_in': out['m_w_in'], 'm_q_norm_g': out['m_q_norm_g'], 'm_k_norm_g': out['m_k_norm_g'], 'm_sinks': out['m_sinks'], 'm_conv_w': out['m_conv_w'], 'm_conv_b': out['m_conv_b'], 'm_conv_ln_g': out['m_conv_ln_g'], 'm_conv_ln_b': out['m_conv_ln_b'], 'm_w_out': out['m_w_out'], 'm_norm_x_g': out['m_norm_x_g'], 'm_norm_mem_g': out['m_norm_mem_g'], 'm_wq_x': out['m_wq_x'], 'm_wkv_x': out['m_wkv_x'], 'm_xq_norm_g': out['m_xq_norm_g'], 'm_xk_norm_g': out['m_xk_norm_g'], 'm_wo_x': out['m_wo_x'], 'm_norm_ffn_g': out['m_norm_ffn_g'], 'm_w_gate_up': out['m_w_gate_up'], 'm_w_down': out['m_w_down'], 'v_norm_mix_g': out['v_norm_mix_g'], 'v_w_in': out['v_w_in'], 'v_q_norm_g': out['v_q_norm_g'], 'v_k_norm_g': out['v_k_norm_g'], 'v_sinks': out['v_sinks'], 'v_conv_w': out['v_conv_w'], 'v_conv_b': out['v_conv_b'], 'v_conv_ln_g': out['v_conv_ln_g'], 'v_conv_ln_b': out['v_conv_ln_b'], 'v_w_out': out['v_w_out'], 'v_norm_x_g': out['v_norm_x_g'], 'v_norm_mem_g': out['v_norm_mem_g'], 'v_wq_x': out['v_wq_x'], 'v_wkv_x': out['v_wkv_x'], 'v_xq_norm_g': out['v_xq_norm_g'], 'v_xk_norm_g': out['v_xk_norm_g'], 'v_wo_x': out['v_wo_x'], 'v_norm_ffn_g': out['v_norm_ffn_g'], 'v_w_gate_up': out['v_w_gate_up'], 'v_w_down': out['v_w_down']}


def _loss(weights, diff, rest, loss_target):
    with _jax.named_scope("forward"):
        args = {**rest, TWIN_DIFF_INPUT: diff, **{k: w.astype(_WEIGHT_DTYPES[k]) for k, w in weights.items()}}
        y = _forward(args)
    with _jax.named_scope("loss_head"):
        err = _jnp.square(y.astype(_jnp.float32) - loss_target)
        return 0.5 * _jnp.sum(_jnp.mean(err, axis=-1)) if err.ndim else 0.5 * err


def _adamw(w, g, m, v):
    m = ADAM_B1 * m + (1.0 - ADAM_B1) * g
    v = ADAM_B2 * v + (1.0 - ADAM_B2) * _jnp.square(g)
    m_hat = m / (1.0 - ADAM_B1 ** ADAM_STEP)
    v_hat = v / (1.0 - ADAM_B2 ** ADAM_STEP)
    delta = -ADAM_LR * (m_hat / (_jnp.sqrt(v_hat) + ADAM_EPS) + ADAM_WD * w)
    return delta, m, v


def reference(x, mem, norm_mix_g, w_in, q_norm_g, k_norm_g, sinks, conv_w, conv_b, conv_ln_g, conv_ln_b, w_out, norm_x_g, norm_mem_g, wq_x, wkv_x, xq_norm_g, xk_norm_g, wo_x, norm_ffn_g, w_gate_up, w_down, loss_target, m_norm_mix_g, m_w_in, m_q_norm_g, m_k_norm_g, m_sinks, m_conv_w, m_conv_b, m_conv_ln_g, m_conv_ln_b, m_w_out, m_norm_x_g, m_norm_mem_g, m_wq_x, m_wkv_x, m_xq_norm_g, m_xk_norm_g, m_wo_x, m_norm_ffn_g, m_w_gate_up, m_w_down, v_norm_mix_g, v_w_in, v_q_norm_g, v_k_norm_g, v_sinks, v_conv_w, v_conv_b, v_conv_ln_g, v_conv_ln_b, v_w_out, v_norm_x_g, v_norm_mem_g, v_wq_x, v_wkv_x, v_xq_norm_g, v_xk_norm_g, v_wo_x, v_norm_ffn_g, v_w_gate_up, v_w_down):
    given = dict(x=x, mem=mem, norm_mix_g=norm_mix_g, w_in=w_in, q_norm_g=q_norm_g, k_norm_g=k_norm_g, sinks=sinks, conv_w=conv_w, conv_b=conv_b, conv_ln_g=conv_ln_g, conv_ln_b=conv_ln_b, w_out=w_out, norm_x_g=norm_x_g, norm_mem_g=norm_mem_g, wq_x=wq_x, wkv_x=wkv_x, xq_norm_g=xq_norm_g, xk_norm_g=xk_norm_g, wo_x=wo_x, norm_ffn_g=norm_ffn_g, w_gate_up=w_gate_up, w_down=w_down, loss_target=loss_target, m_norm_mix_g=m_norm_mix_g, m_w_in=m_w_in, m_q_norm_g=m_q_norm_g, m_k_norm_g=m_k_norm_g, m_sinks=m_sinks, m_conv_w=m_conv_w, m_conv_b=m_conv_b, m_conv_ln_g=m_conv_ln_g, m_conv_ln_b=m_conv_ln_b, m_w_out=m_w_out, m_norm_x_g=m_norm_x_g, m_norm_mem_g=m_norm_mem_g, m_wq_x=m_wq_x, m_wkv_x=m_wkv_x, m_xq_norm_g=m_xq_norm_g, m_xk_norm_g=m_xk_norm_g, m_wo_x=m_wo_x, m_norm_ffn_g=m_norm_ffn_g, m_w_gate_up=m_w_gate_up, m_w_down=m_w_down, v_norm_mix_g=v_norm_mix_g, v_w_in=v_w_in, v_q_norm_g=v_q_norm_g, v_k_norm_g=v_k_norm_g, v_sinks=v_sinks, v_conv_w=v_conv_w, v_conv_b=v_conv_b, v_conv_ln_g=v_conv_ln_g, v_conv_ln_b=v_conv_ln_b, v_w_out=v_w_out, v_norm_x_g=v_norm_x_g, v_norm_mem_g=v_norm_mem_g, v_wq_x=v_wq_x, v_wkv_x=v_wkv_x, v_xq_norm_g=v_xq_norm_g, v_xk_norm_g=v_xk_norm_g, v_wo_x=v_wo_x, v_norm_ffn_g=v_norm_ffn_g, v_w_gate_up=v_w_gate_up, v_w_down=v_w_down)
    weights = {n: given[n] for n in TWIN_WEIGHTS}
    shared = {n: given[n] for n in SHARED_INPUTS}
    per_example = {n: given[n] for n in ['x', 'mem']}
    grad_fn = _jax.value_and_grad(_loss, argnums=(0, 1))

    def one_microbatch(ex, loss_target):
        ex = dict(ex)
        diff = ex.pop(TWIN_DIFF_INPUT)
        return grad_fn(weights, diff, {**shared, **ex}, loss_target)

    if N_MICROBATCH == 1:
        loss, (grad_w, grad_x) = one_microbatch(per_example, given["loss_target"])
    else:
        def body(carry, xs):
            loss_sum, grad_sum = carry
            l_k, (gw_k, gx_k) = one_microbatch(xs[0], xs[1])
            with _jax.named_scope("update"):
                return (loss_sum + l_k, _jax.tree.map(_jnp.add, grad_sum, gw_k)), gx_k

        init = (_jnp.zeros((), _jnp.float32), _jax.tree.map(_jnp.zeros_like, weights))
        (loss, grad_w), grad_x = _jax.lax.scan(body, init, (per_example, given["loss_target"]))
    with _jax.named_scope("update"):
        delta_w, new_m, new_v = {}, {}, {}
        for n in TWIN_WEIGHTS:
            delta_w[n], new_m[n], new_v[n] = _adamw(weights[n], grad_w[n], given["m_" + n], given["v_" + n])
    return (loss, grad_x, *[grad_w[n] for n in TWIN_WEIGHTS], *[delta_w[n] for n in TWIN_WEIGHTS],
            *[new_m[n] for n in TWIN_WEIGHTS], *[new_v[n] for n in TWIN_WEIGHTS])
```

```python
import functools

import jax
import jax.numpy as jnp
from jax import lax
from jax.experimental import pallas as pl
from jax.experimental.pallas import tpu as pltpu

F32 = jnp.float32
BF16 = jnp.bfloat16

D_MODEL = 1024
HEAD_DIM = 64
N_Q_HEADS = 8
N_KV_HEADS = 2
GROUP = N_Q_HEADS // N_KV_HEADS
ATTN_WIDTH = N_Q_HEADS * HEAD_DIM
KV_WIDTH = N_KV_HEADS * HEAD_DIM
QKV_WIDTH = ATTN_WIDTH + 2 * KV_WIDTH
CONV_CH = 512
IN_COLS = QKV_WIDTH + 2 * CONV_CH
CONV_K = 31
CONV_HALO = 32
BLOCK = 128
N_X_HEADS = 4
X_HEAD_DIM = 256
D_FF = 2816
EPS = 1e-6
NEG = -1e30
DEPTH = 2
N_DEV = 8

ADAM_LR = 0.001
ADAM_B1 = 0.9
ADAM_B2 = 0.999
ADAM_EPS = 1e-08
ADAM_WD = 0.01
ADAM_STEP = 10

V7X_VMEM_LIMIT = 56 * 1024 * 1024
LANES = 128

MESH = pl.DeviceIdType.MESH


def _cp(**kw):
    return pltpu.CompilerParams(vmem_limit_bytes=V7X_VMEM_LIMIT, **kw)


def _dot(a, b, dims):
    return lax.dot_general(a.astype(BF16), b.astype(BF16), (dims, ((), ())), preferred_element_type=F32)


def _dot_nn(a, b):
    return _dot(a, b, ((1,), (0,)))


def _dot_nt(a, b):
    return _dot(a, b, ((1,), (1,)))


def _dot_tn(a, b):
    return _dot(a, b, ((0,), (0,)))


def _sigmoid(x):
    return jax.nn.sigmoid(x)


def _rms(x):
    r = lax.rsqrt(jnp.mean(x * x, axis=-1, keepdims=True) + EPS)
    return x * r, r


def _rms_bwd(dy, xhat, r, g):
    dxh = dy * g
    return r * (dxh - xhat * jnp.mean(dxh * xhat, axis=-1, keepdims=True))


def rms_fwd(x, g, *, tm=512):
    m, d = x.shape
    tm = min(tm, m)

    def body(x_ref, g_ref, o_ref):
        xh, _ = _rms(x_ref[...])
        o_ref[...] = (xh * g_ref[...]).astype(o_ref.dtype)

    return pl.pallas_call(
        body, name="rms_fwd", grid=(m // tm,),
        in_specs=[pl.BlockSpec((tm, d), lambda i: (i, 0)), pl.BlockSpec((1, d), lambda i: (0, 0))],
        out_specs=pl.BlockSpec((tm, d), lambda i: (i, 0)),
        out_shape=jax.ShapeDtypeStruct((m, d), BF16), compiler_params=_cp(),
    )(x, g.reshape(1, d))


def rms_bwd(dh, x, g, dres, *, tm=512):
    m, d = x.shape
    tm = min(tm, m)
    has_res = dres is not None

    def body(*refs):
        if has_res:
            dh_ref, x_ref, g_ref, r_ref, dx_ref, dg_ref = refs
        else:
            dh_ref, x_ref, g_ref, dx_ref, dg_ref = refs
        xh, r = _rms(x_ref[...])
        dy = dh_ref[...].astype(F32)

        @pl.when(pl.program_id(0) == 0)
        def _():
            dg_ref[...] = jnp.zeros_like(dg_ref)

        dg_ref[...] += jnp.sum(dy * xh, axis=0, keepdims=True)
        dx = _rms_bwd(dy, xh, r, g_ref[...])
        if has_res:
            dx = dx + r_ref[...]
        dx_ref[...] = dx

    row = pl.BlockSpec((tm, d), lambda i: (i, 0))
    vec = pl.BlockSpec((1, d), lambda i: (0, 0))
    ins = [dh, x, g.reshape(1, d)] + ([dres] if has_res else [])
    return pl.pallas_call(
        body, name="rms_bwd" + ("_res" if has_res else ""), grid=(m // tm,),
        in_specs=[row, row, vec] + ([row] if has_res else []),
        out_specs=[row, vec],
        out_shape=[jax.ShapeDtypeStruct((m, d), F32), jax.ShapeDtypeStruct((1, d), F32)],
        compiler_params=_cp(),
    )(*ins)


def swiglu_fwd(gu, *, tm=512):
    m, f2 = gu.shape
    f = f2 // 2

    def body(g_ref, u_ref, o_ref):
        g = g_ref[...]
        o_ref[...] = (g * _sigmoid(g) * u_ref[...]).astype(o_ref.dtype)

    return pl.pallas_call(
        body, name="swiglu_fwd", grid=(m // tm,),
        in_specs=[pl.BlockSpec((tm, f), lambda i: (i, 0)), pl.BlockSpec((tm, f), lambda i: (i, 1))],
        out_specs=pl.BlockSpec((tm, f), lambda i: (i, 0)),
        out_shape=jax.ShapeDtypeStruct((m, f), BF16), compiler_params=_cp(),
    )(gu, gu)


def swiglu_bwd(gu, da, *, tm=256):
    m, f2 = gu.shape
    f = f2 // 2

    def body(gu_ref, da_ref, o_ref):
        g = gu_ref[:, :f]
        u = gu_ref[:, f:]
        da_v = da_ref[...]
        sg = _sigmoid(g)
        o_ref[:, :f] = (da_v * u * (sg * (1.0 + g * (1.0 - sg)))).astype(o_ref.dtype)
        o_ref[:, f:] = (da_v * (g * sg)).astype(o_ref.dtype)

    return pl.pallas_call(
        body, name="swiglu_bwd", grid=(m // tm,),
        in_specs=[pl.BlockSpec((tm, f2), lambda i: (i, 0)), pl.BlockSpec((tm, f), lambda i: (i, 0))],
        out_specs=pl.BlockSpec((tm, f2), lambda i: (i, 0)),
        out_shape=jax.ShapeDtypeStruct((m, f2), BF16), compiler_params=_cp(),
    )(gu, da)


def loss_head(y, target, *, tm=512):
    m, d = y.shape

    def body(y_ref, t_ref, dy_ref, l_ref):
        err = y_ref[...] - t_ref[...]
        dy_ref[...] = err * (1.0 / d)

        @pl.when(pl.program_id(0) == 0)
        def _():
            l_ref[...] = jnp.zeros_like(l_ref)

        part = jnp.sum(jnp.sum(err * err, axis=-1, keepdims=True), axis=0, keepdims=True)
        l_ref[...] += jnp.broadcast_to(part * (0.5 / d), l_ref.shape)

    row = pl.BlockSpec((tm, d), lambda i: (i, 0))
    return pl.pallas_call(
        body, name="loss_head", grid=(m // tm,),
        in_specs=[row, row],
        out_specs=[row, pl.BlockSpec((1, LANES), lambda i: (0, 0))],
        out_shape=[jax.ShapeDtypeStruct((m, d), F32), jax.ShapeDtypeStruct((1, LANES), F32)],
        compiler_params=_cp(),
    )(y, target)


def _tile(n, cap):
    if n <= cap:
        return n
    best = None
    for t in range(LANES, cap + 1, LANES):
        if n % t == 0:
            best = t
    assert best is not None, (n, cap)
    return best


def mm(a, b, *, trans_b, out_dtype, res=None, tm=1024, tn_cap=1536, name):
    m, k = a.shape
    n = b.shape[0] if trans_b else b.shape[1]
    assert (b.shape[1] if trans_b else b.shape[0]) == k
    tm = min(tm, m)
    tn = _tile(n, tn_cap)
    has_res = res is not None

    def body(*refs):
        if has_res:
            a_ref, b_ref, r_ref, o_ref = refs
        else:
            a_ref, b_ref, o_ref = refs
        acc = _dot_nt(a_ref[...], b_ref[...]) if trans_b else _dot_nn(a_ref[...], b_ref[...])
        if has_res:
            acc = acc + r_ref[...]
        o_ref[...] = acc.astype(o_ref.dtype)

    b_spec = pl.BlockSpec((tn, k), lambda i, j: (j, 0)) if trans_b else pl.BlockSpec((k, tn), lambda i, j: (0, j))
    o_spec = pl.BlockSpec((tm, tn), lambda i, j: (i, j))
    return pl.pallas_call(
        body, name=name, grid=(m // tm, n // tn),
        in_specs=[pl.BlockSpec((tm, k), lambda i, j: (i, 0)), b_spec] + ([o_spec] if has_res else []),
        out_specs=o_spec,
        out_shape=jax.ShapeDtypeStruct((m, n), out_dtype), compiler_params=_cp(),
    )(*([a, b] + ([res] if has_res else [])))


def mm_tn(a, b, *, name, ta_cap=1536, tb_cap=1024, tk=1024):
    m, ka = a.shape
    nb = b.shape[1]
    assert b.shape[0] == m
    tk = min(tk, m)
    ta = _tile(ka, ta_cap)
    tb = _tile(nb, tb_cap)

    def body(a_ref, b_ref, o_ref):
        @pl.when(pl.program_id(2) == 0)
        def _():
            o_ref[...] = jnp.zeros_like(o_ref)

        o_ref[...] += _dot_tn(a_ref[...], b_ref[...])

    return pl.pallas_call(
        body, name=name, grid=(ka // ta, nb // tb, m // tk),
        in_specs=[pl.BlockSpec((tk, ta), lambda i, j, kk: (kk, i)), pl.BlockSpec((tk, tb), lambda i, j, kk: (kk, j))],
        out_specs=pl.BlockSpec((ta, tb), lambda i, j, kk: (i, j)),
        out_shape=jax.ShapeDtypeStruct((ka, nb), F32), compiler_params=_cp(),
    )(a, b)


SWA_TILE = 512
SWA_SUB = SWA_TILE // BLOCK


def _swa_mask():
    rows = GROUP * BLOCK
    r = lax.broadcasted_iota(jnp.int32, (rows, 2 * BLOCK), 0)
    j = lax.broadcasted_iota(jnp.int32, (rows, 2 * BLOCK), 1)
    dist = (r & (BLOCK - 1)) + BLOCK - j
    return dist.astype(F32), (dist >= 0) & (dist < BLOCK), j >= BLOCK


def _slope_col(kv):
    return jnp.concatenate([jnp.full((BLOCK, 1), 2.0 ** -(kv * GROUP + g + 1), F32) for g in range(GROUP)], axis=0)


def _sink_col(sinks_ref, kv):
    return jnp.concatenate([jnp.full((BLOCK, 1), sinks_ref[kv * GROUP + g], F32) for g in range(GROUP)], axis=0)


def _stack_heads(ref, rows, kv):
    return jnp.concatenate(
        [ref[rows, (kv * GROUP + g) * HEAD_DIM:(kv * GROUP + g + 1) * HEAD_DIM] for g in range(GROUP)], axis=0)


def _swa_keys(cur_ref, prev_ref, b, col):
    cols = slice(col, col + HEAD_DIM)
    if b == 0:
        return jnp.concatenate([prev_ref[:, cols], cur_ref[0:BLOCK, cols]], axis=0)
    return cur_ref[(b - 1) * BLOCK:(b + 1) * BLOCK, cols]


def _swa_probs(qn, kn, bias, valid, sink):
    s = _dot_nt(qn, kn) * (HEAD_DIM ** -0.5)
    s = jnp.where(valid, s + bias, NEG)
    mx = jnp.maximum(jnp.max(s, axis=-1, keepdims=True), sink)
    e = jnp.exp(s - mx)
    es = jnp.exp(sink - mx)
    den = jnp.sum(e, axis=-1, keepdims=True) + es
    return e / den, es / den


def swa_fwd(u, qg, kg, sinks):
    t = u.shape[0]
    nt = t // SWA_TILE

    def body(sinks_ref, cur_ref, prev_ref, qg_ref, kg_ref, o_ref):
        i = pl.program_id(0)
        qg_v = qg_ref[...]
        kg_v = kg_ref[...]
        dist, window, own_block = _swa_mask()
        valid_first = window & (own_block | (i > 0))
        for kv in range(N_KV_HEADS):
            sink = _sink_col(sinks_ref, kv)
            bias = -_slope_col(kv) * dist
            for b in range(SWA_SUB):
                rows = slice(b * BLOCK, (b + 1) * BLOCK)
                valid = valid_first if b == 0 else window
                qn = _rms(_stack_heads(cur_ref, rows, kv))[0] * qg_v
                kn = _rms(_swa_keys(cur_ref, prev_ref, b, ATTN_WIDTH + kv * HEAD_DIM))[0] * kg_v
                vv = _swa_keys(cur_ref, prev_ref, b, ATTN_WIDTH + KV_WIDTH + kv * HEAD_DIM)
                p, _ = _swa_probs(qn, kn, bias, valid, sink)
                o4 = _dot_nn(p, vv)
                for g in range(GROUP):
                    h = kv * GROUP + g
                    o_ref[rows, h * HEAD_DIM:(h + 1) * HEAD_DIM] = o4[g * BLOCK:(g + 1) * BLOCK].astype(o_ref.dtype)

    vec = pl.BlockSpec((1, HEAD_DIM), lambda i: (0, 0))
    return pl.pallas_call(
        body, name="swa_fwd", grid=(nt,),
        in_specs=[
            pl.BlockSpec(memory_space=pltpu.SMEM),
            pl.BlockSpec((SWA_TILE, QKV_WIDTH), lambda i: (i, 0)),
            pl.BlockSpec((BLOCK, QKV_WIDTH), lambda i: (jnp.maximum(i * SWA_SUB - 1, 0), 0)),
            vec, vec,
        ],
        out_specs=pl.BlockSpec((SWA_TILE, ATTN_WIDTH), lambda i: (i, 0)),
        out_shape=jax.ShapeDtypeStruct((t, 2 * ATTN_WIDTH), BF16), compiler_params=_cp(),
    )(sinks, u, u, qg.reshape(1, HEAD_DIM), kg.reshape(1, HEAD_DIM))


def swa_bwd(u, dmixed, qg, kg, sinks):
    t = u.shape[0]
    nt = t // SWA_TILE
    kcol = lambda kv: slice(kv * HEAD_DIM, (kv + 1) * HEAD_DIM)
    vcol = lambda kv: slice(KV_WIDTH + kv * HEAD_DIM, KV_WIDTH + (kv + 1) * HEAD_DIM)

    def body(sinks_ref, cur_ref, prev_ref, do_ref, qg_ref, kg_ref, du_ref, dqg_ref, dkg_ref, dsk_ref, acc_ref, carry_ref):
        step = pl.program_id(0)
        i = nt - 1 - step
        qg_v = qg_ref[...]
        kg_v = kg_ref[...]

        @pl.when(step == 0)
        def _():
            carry_ref[...] = jnp.zeros_like(carry_ref)
            dqg_ref[...] = jnp.zeros_like(dqg_ref)
            dkg_ref[...] = jnp.zeros_like(dkg_ref)
            dsk_ref[...] = jnp.zeros_like(dsk_ref)

        acc_ref[0:SWA_TILE, :] = jnp.zeros((SWA_TILE, 2 * KV_WIDTH), F32)
        acc_ref[SWA_TILE:SWA_TILE + BLOCK, :] = carry_ref[...]

        lane = lax.broadcasted_iota(jnp.int32, (1, LANES), 1)
        dqg_acc = jnp.zeros((1, HEAD_DIM), F32)
        dsk_acc = jnp.zeros((1, LANES), F32)
        dist, window, own_block = _swa_mask()
        valid_first = window & (own_block | (i > 0))
        for kv in range(N_KV_HEADS):
            sink = _sink_col(sinks_ref, kv)
            bias = -_slope_col(kv) * dist
            for b in range(SWA_SUB):
                rows = slice(b * BLOCK, (b + 1) * BLOCK)
                valid = valid_first if b == 0 else window
                qh, rq = _rms(_stack_heads(cur_ref, rows, kv))
                qn = qh * qg_v
                kn = _rms(_swa_keys(cur_ref, prev_ref, b, ATTN_WIDTH + kv * HEAD_DIM))[0] * kg_v
                vv = _swa_keys(cur_ref, prev_ref, b, ATTN_WIDTH + KV_WIDTH + kv * HEAD_DIM)
                p, ps = _swa_probs(qn, kn, bias, valid, sink)
                do4 = _stack_heads(do_ref, rows, kv)
                dp = _dot_nt(do4, vv)
                delta = jnp.sum(p * dp, axis=-1, keepdims=True)
                ds = p * (dp - delta)
                dsink = -ps * delta
                for g in range(GROUP):
                    part = jnp.sum(dsink[g * BLOCK:(g + 1) * BLOCK], axis=0, keepdims=True)
                    dsk_acc = dsk_acc + jnp.where(lane == kv * GROUP + g, part, 0.0)
                dvv = _dot_tn(p, do4)
                dqn = _dot_nn(ds, kn) * (HEAD_DIM ** -0.5)
                dkn = _dot_tn(ds, qn) * (HEAD_DIM ** -0.5)
                dqg_acc = dqg_acc + jnp.sum(dqn * qh, axis=0, keepdims=True)
                dq = _rms_bwd(dqn, qh, rq, qg_v)
                for g in range(GROUP):
                    h = kv * GROUP + g
                    du_ref[rows, h * HEAD_DIM:(h + 1) * HEAD_DIM] = dq[g * BLOCK:(g + 1) * BLOCK].astype(du_ref.dtype)
                keys = slice(b * BLOCK, (b + 2) * BLOCK)
                acc_ref[keys, kcol(kv)] += dkn
                acc_ref[keys, vcol(kv)] += dvv
        dqg_ref[...] += dqg_acc
        dsk_ref[...] += dsk_acc

        own = slice(BLOCK, BLOCK + SWA_TILE)
        dkg_acc = jnp.zeros((1, HEAD_DIM), F32)
        for kv in range(N_KV_HEADS):
            kh, rk = _rms(cur_ref[:, ATTN_WIDTH + kv * HEAD_DIM:ATTN_WIDTH + (kv + 1) * HEAD_DIM])
            dkn = acc_ref[own, kcol(kv)]
            dkg_acc = dkg_acc + jnp.sum(dkn * kh, axis=0, keepdims=True)
            dk = _rms_bwd(dkn, kh, rk, kg_v)
            du_ref[:, ATTN_WIDTH + kv * HEAD_DIM:ATTN_WIDTH + (kv + 1) * HEAD_DIM] = dk.astype(du_ref.dtype)
            vc = ATTN_WIDTH + KV_WIDTH + kv * HEAD_DIM
            du_ref[:, vc:vc + HEAD_DIM] = acc_ref[own, vcol(kv)].astype(du_ref.dtype)
        dkg_ref[...] += dkg_acc
        carry_ref[...] = acc_ref[0:BLOCK, :]

    vec = pl.BlockSpec((1, HEAD_DIM), lambda s: (0, 0))
    return pl.pallas_call(
        body, name="swa_bwd", grid=(nt,),
        in_specs=[
            pl.BlockSpec(memory_space=pltpu.SMEM),
            pl.BlockSpec((SWA_TILE, QKV_WIDTH), lambda s: (nt - 1 - s, 0)),
            pl.BlockSpec((BLOCK, QKV_WIDTH), lambda s: (jnp.maximum((nt - 1 - s) * SWA_SUB - 1, 0), 0)),
            pl.BlockSpec((SWA_TILE, ATTN_WIDTH), lambda s: (nt - 1 - s, 0)),
            vec, vec,
        ],
        out_specs=[
            pl.BlockSpec((SWA_TILE, QKV_WIDTH), lambda s: (nt - 1 - s, 0)),
            vec, vec, pl.BlockSpec((1, LANES), lambda s: (0, 0)),
        ],
        out_shape=[
            jax.ShapeDtypeStruct((t, IN_COLS), BF16),
            jax.ShapeDtypeStruct((1, HEAD_DIM), F32), jax.ShapeDtypeStruct((1, HEAD_DIM), F32),
            jax.ShapeDtypeStruct((1, LANES), F32),
        ],
        scratch_shapes=[pltpu.VMEM((SWA_TILE + BLOCK, 2 * KV_WIDTH), F32), pltpu.VMEM((BLOCK, 2 * KV_WIDTH), F32)],
        compiler_params=_cp(),
    )(sinks, u, u, dmixed, qg.reshape(1, HEAD_DIM), kg.reshape(1, HEAD_DIM))


CONV_TILE = 512
CONV_CHUNK = 64
VAL0 = QKV_WIDTH
GATE0 = QKV_WIDTH + CONV_CH


def _glu(ref):
    return ref[:, VAL0:GATE0] * _sigmoid(ref[:, GATE0:GATE0 + CONV_CH])


def _conv_rows(gl_ref, w_ref, bias, first, count):
    out = []
    for c0 in range(0, count, CONV_CHUNK):
        n = min(CONV_CHUNK, count - c0)
        acc = jnp.broadcast_to(bias, (n, CONV_CH))
        for k in range(CONV_K):
            acc = acc + w_ref[k:k + 1, :] * gl_ref[pl.ds(first + c0 + 2 + k, n), :]
        out.append(acc)
    return jnp.concatenate(out, axis=0) if len(out) > 1 else out[0]


def _layernorm_stats(y):
    mu = jnp.mean(y, axis=-1, keepdims=True)
    yc = y - mu
    rstd = lax.rsqrt(jnp.mean(yc * yc, axis=-1, keepdims=True) + EPS)
    return yc * rstd, rstd


def conv_fwd(u, mixed, conv_w, conv_b, ln_g, ln_b):
    t = u.shape[0]
    nt = t // CONV_TILE
    per = CONV_TILE // CONV_HALO

    def body(cur_ref, prev_ref, mixed_ref, w_ref, b_ref, g_ref, b2_ref, o_ref, gl_ref):
        del mixed_ref
        i = pl.program_id(0)
        gl_ref[0:CONV_HALO, :] = jnp.where(i > 0, _glu(prev_ref), 0.0)
        gl_ref[CONV_HALO:CONV_HALO + CONV_TILE, :] = _glu(cur_ref)
        y = _conv_rows(gl_ref, w_ref, b_ref[...], 0, CONV_TILE)
        yh, _ = _layernorm_stats(y)
        yln = yh * g_ref[...] + b2_ref[...]
        o_ref[...] = (yln * _sigmoid(yln)).astype(o_ref.dtype)

    vec = pl.BlockSpec((1, CONV_CH), lambda i: (0, 0))
    return pl.pallas_call(
        body, name="conv_fwd", grid=(nt,),
        in_specs=[
            pl.BlockSpec((CONV_TILE, IN_COLS), lambda i: (i, 0)),
            pl.BlockSpec((CONV_HALO, IN_COLS), lambda i: (jnp.maximum(i * per - 1, 0), 0)),
            pl.BlockSpec(memory_space=pl.ANY),
            pl.BlockSpec((CONV_HALO, CONV_CH), lambda i: (0, 0)),
            vec, vec, vec,
        ],
        out_specs=pl.BlockSpec((CONV_TILE, CONV_CH), lambda i: (i, 1)),
        out_shape=jax.ShapeDtypeStruct(mixed.shape, mixed.dtype),
        scratch_shapes=[pltpu.VMEM((CONV_HALO + CONV_TILE, CONV_CH), F32)],
        input_output_aliases={2: 0}, compiler_params=_cp(),
    )(u, u, mixed, conv_w, conv_b.reshape(1, CONV_CH), ln_g.reshape(1, CONV_CH), ln_b.reshape(1, CONV_CH))


def conv_bwd(u, dmixed, du, conv_w, conv_b, ln_g, ln_b):
    t = u.shape[0]
    nt = t // CONV_TILE
    per = CONV_TILE // CONV_HALO
    ext = CONV_TILE + CONV_HALO

    def body(cur_ref, prev_ref, next_ref, do_ref, don_ref, du_in_ref, w_ref, b_ref, g_ref, b2_ref,
             du_ref, dw_ref, dvec_ref, gl_ref, dy_ref):
        i = pl.program_id(0)
        last = i == nt - 1

        @pl.when(i == 0)
        def _():
            dw_ref[...] = jnp.zeros_like(dw_ref)
            dvec_ref[...] = jnp.zeros_like(dvec_ref)

        gl_ref[0:CONV_HALO, :] = jnp.where(i > 0, _glu(prev_ref), 0.0)
        gl_ref[CONV_HALO:CONV_HALO + CONV_TILE, :] = _glu(cur_ref)
        gl_ref[CONV_HALO + CONV_TILE:, :] = _glu(next_ref)

        y = _conv_rows(gl_ref, w_ref, b_ref[...], 0, ext)
        yh, rstd = _layernorm_stats(y)
        g = g_ref[...]
        yln = yh * g + b2_ref[...]
        sg = _sigmoid(yln)
        dout = jnp.concatenate([do_ref[...], jnp.where(last, 0.0, don_ref[...])], axis=0)
        dyln = dout * (sg * (1.0 + yln * (1.0 - sg)))
        dyh = dyln * g
        dy = rstd * (dyh - jnp.mean(dyh, axis=-1, keepdims=True) - yh * jnp.mean(dyh * yh, axis=-1, keepdims=True))
        dy_ref[...] = dy

        own = slice(0, CONV_TILE)
        dvec_ref[0:1, :] += jnp.sum(dy[own], axis=0, keepdims=True)
        dvec_ref[1:2, :] += jnp.sum(dyln[own] * yh[own], axis=0, keepdims=True)
        dvec_ref[2:3, :] += jnp.sum(dyln[own], axis=0, keepdims=True)
        for k in range(CONV_K):
            dw_ref[k:k + 1, :] += jnp.sum(dy[own] * gl_ref[pl.ds(2 + k, CONV_TILE), :], axis=0, keepdims=True)

        for c0 in range(0, CONV_TILE, CONV_CHUNK):
            acc = jnp.zeros((CONV_CHUNK, CONV_CH), F32)
            for k in range(CONV_K):
                acc = acc + w_ref[k:k + 1, :] * dy_ref[pl.ds(c0 + CONV_K - 1 - k, CONV_CHUNK), :]
            rows = slice(c0, c0 + CONV_CHUNK)
            val = cur_ref[rows, VAL0:GATE0]
            sgate = _sigmoid(cur_ref[rows, GATE0:GATE0 + CONV_CH])
            du_ref[rows, VAL0:GATE0] = (acc * sgate).astype(du_ref.dtype)
            du_ref[rows, GATE0:GATE0 + CONV_CH] = (acc * val * sgate * (1.0 - sgate)).astype(du_ref.dtype)
        du_ref[:, 0:QKV_WIDTH] = du_in_ref[:, 0:QKV_WIDTH]

    vec = pl.BlockSpec((1, CONV_CH), lambda i: (0, 0))
    n_halo = t // CONV_HALO
    return pl.pallas_call(
        body, name="conv_bwd", grid=(nt,),
        in_specs=[
            pl.BlockSpec((CONV_TILE, IN_COLS), lambda i: (i, 0)),
            pl.BlockSpec((CONV_HALO, IN_COLS), lambda i: (jnp.maximum(i * per - 1, 0), 0)),
            pl.BlockSpec((CONV_HALO, IN_COLS), lambda i: (jnp.minimum((i + 1) * per, n_halo - 1), 0)),
            pl.BlockSpec((CONV_TILE, CONV_CH), lambda i: (i, 1)),
            pl.BlockSpec((CONV_HALO, CONV_CH), lambda i: (jnp.minimum((i + 1) * per, n_halo - 1), 1)),
            pl.BlockSpec((CONV_TILE, IN_COLS), lambda i: (i, 0)),
            pl.BlockSpec((CONV_HALO, CONV_CH), lambda i: (0, 0)),
            vec, vec, vec,
        ],
        out_specs=[
            pl.BlockSpec((CONV_TILE, IN_COLS), lambda i: (i, 0)),
            pl.BlockSpec((CONV_HALO, CONV_CH), lambda i: (0, 0)),
            pl.BlockSpec((8, CONV_CH), lambda i: (0, 0)),
        ],
        out_shape=[
            jax.ShapeDtypeStruct(du.shape, du.dtype),
            jax.ShapeDtypeStruct((CONV_HALO, CONV_CH), F32),
            jax.ShapeDtypeStruct((8, CONV_CH), F32),
        ],
        scratch_shapes=[pltpu.VMEM((CONV_TILE + 2 * CONV_HALO, CONV_CH), F32), pltpu.VMEM((ext, CONV_CH), F32)],
        input_output_aliases={5: 0}, compiler_params=_cp(),
    )(u, u, u, dmixed, dmixed, du, conv_w, conv_b.reshape(1, CONV_CH), ln_g.reshape(1, CONV_CH), ln_b.reshape(1, CONV_CH))


XATTN_TILE = 512


def _xattn_probs(qn, kn):
    s = _dot_nt(qn, kn) * (X_HEAD_DIM ** -0.5)
    e = jnp.exp(s - jnp.max(s, axis=-1, keepdims=True))
    return e / jnp.sum(e, axis=-1, keepdims=True)


def xattn_fwd(q, kv, qg, kg):
    t, d = q.shape
    n_mem = kv.shape[0]

    def body(q_ref, kv_ref, qg_ref, kg_ref, o_ref):
        for h in range(N_X_HEADS):
            cols = slice(h * X_HEAD_DIM, (h + 1) * X_HEAD_DIM)
            qn = _rms(q_ref[:, cols])[0] * qg_ref[...]
            kn = _rms(kv_ref[:, cols])[0] * kg_ref[...]
            p = _xattn_probs(qn, kn)
            o_ref[:, cols] = _dot_nn(p, kv_ref[:, d + h * X_HEAD_DIM:d + (h + 1) * X_HEAD_DIM]).astype(o_ref.dtype)

    vec = pl.BlockSpec((1, X_HEAD_DIM), lambda i: (0, 0))
    return pl.pallas_call(
        body, name="xattn_fwd", grid=(t // XATTN_TILE,),
        in_specs=[pl.BlockSpec((XATTN_TILE, d), lambda i: (i, 0)), pl.BlockSpec((n_mem, 2 * d), lambda i: (0, 0)), vec, vec],
        out_specs=pl.BlockSpec((XATTN_TILE, d), lambda i: (i, 0)),
        out_shape=jax.ShapeDtypeStruct((t, d), BF16), compiler_params=_cp(),
    )(q, kv, qg.reshape(1, X_HEAD_DIM), kg.reshape(1, X_HEAD_DIM))


def xattn_bwd(q, kv, do, qg, kg):
    t, d = q.shape
    n_mem = kv.shape[0]
    nt = t // XATTN_TILE

    def body(q_ref, kv_ref, do_ref, qg_ref, kg_ref, dq_ref, dkv_ref, dqg_ref, dkg_ref):
        i = pl.program_id(0)

        @pl.when(i == 0)
        def _():
            dkv_ref[...] = jnp.zeros_like(dkv_ref)
            dqg_ref[...] = jnp.zeros_like(dqg_ref)
            dkg_ref[...] = jnp.zeros_like(dkg_ref)

        qg_v = qg_ref[...]
        kg_v = kg_ref[...]
        dqg_acc = jnp.zeros((1, X_HEAD_DIM), F32)
        for h in range(N_X_HEADS):
            cols = slice(h * X_HEAD_DIM, (h + 1) * X_HEAD_DIM)
            vcols = slice(d + h * X_HEAD_DIM, d + (h + 1) * X_HEAD_DIM)
            qh, rq = _rms(q_ref[:, cols])
            qn = qh * qg_v
            kn = _rms(kv_ref[:, cols])[0] * kg_v
            v = kv_ref[:, vcols]
            do_h = do_ref[:, cols]
            p = _xattn_probs(qn, kn)
            dp = _dot_nt(do_h, v)
            ds = p * (dp - jnp.sum(p * dp, axis=-1, keepdims=True))
            dkv_ref[:, vcols] += _dot_tn(p, do_h)
            dqn = _dot_nn(ds, kn) * (X_HEAD_DIM ** -0.5)
            dkv_ref[:, cols] += _dot_tn(ds, qn) * (X_HEAD_DIM ** -0.5)
            dqg_acc = dqg_acc + jnp.sum(dqn * qh, axis=0, keepdims=True)
            dq_ref[:, cols] = _rms_bwd(dqn, qh, rq, qg_v).astype(dq_ref.dtype)
        dqg_ref[...] += dqg_acc

        @pl.when(i == nt - 1)
        def _():
            dkg_acc = jnp.zeros((1, X_HEAD_DIM), F32)
            for h in range(N_X_HEADS):
                cols = slice(h * X_HEAD_DIM, (h + 1) * X_HEAD_DIM)
                kh, rk = _rms(kv_ref[:, cols])
                dkn = dkv_ref[:, cols]
                dkg_acc = dkg_acc + jnp.sum(dkn * kh, axis=0, keepdims=True)
                dkv_ref[:, cols] = _rms_bwd(dkn, kh, rk, kg_v)
            dkg_ref[...] = dkg_acc

    vec = pl.BlockSpec((1, X_HEAD_DIM), lambda i: (0, 0))
    row = pl.BlockSpec((XATTN_TILE, d), lambda i: (i, 0))
    full = pl.BlockSpec((n_mem, 2 * d), lambda i: (0, 0))
    return pl.pallas_call(
        body, name="xattn_bwd", grid=(nt,),
        in_specs=[row, full, row, vec, vec],
        out_specs=[row, full, vec, vec],
        out_shape=[
            jax.ShapeDtypeStruct((t, d), BF16), jax.ShapeDtypeStruct((n_mem, 2 * d), F32),
            jax.ShapeDtypeStruct((1, X_HEAD_DIM), F32), jax.ShapeDtypeStruct((1, X_HEAD_DIM), F32),
        ],
        compiler_params=_cp(),
    )(q, kv, do, qg.reshape(1, X_HEAD_DIM), kg.reshape(1, X_HEAD_DIM))


def adamw(w, g, m, v, *, name):
    r, c = w.shape
    tr = r
    for cand in (512, 256, 128, 64, 32, 16, 8):
        if r % cand == 0 and r > cand:
            tr = cand
            break

    def body(w_ref, g_ref, m_ref, v_ref, d_ref, nm_ref, nv_ref):
        g_v = g_ref[...]
        m2 = ADAM_B1 * m_ref[...] + (1.0 - ADAM_B1) * g_v
        v2 = ADAM_B2 * v_ref[...] + (1.0 - ADAM_B2) * jnp.square(g_v)
        m_hat = m2 / (1.0 - ADAM_B1 ** ADAM_STEP)
        v_hat = v2 / (1.0 - ADAM_B2 ** ADAM_STEP)
        d_ref[...] = -ADAM_LR * (m_hat / (jnp.sqrt(v_hat) + ADAM_EPS) + ADAM_WD * w_ref[...])
        nm_ref[...] = m2
        nv_ref[...] = v2

    spec = pl.BlockSpec((tr, c), lambda i: (i, 0))
    shape = jax.ShapeDtypeStruct((r, c), F32)
    return pl.pallas_call(
        body, name=name, grid=(r // tr,), in_specs=[spec] * 4, out_specs=[spec] * 3,
        out_shape=[shape] * 3, compiler_params=_cp(),
    )(w, g, m, v)


def _position():
    return lax.axis_index("x"), lax.axis_index("y"), lax.axis_index("c")


def all_gather(shard, *, name, in_vmem):
    r, c_ = shard.shape

    def body(x_ref, out_ref, send_sems, recv_sems, local_sem):
        x, y, c = _position()
        me, sibling = (x, y, c), (x, y, 1 - c)
        chips = [(1 - x, y), (x, 1 - y), (1 - x, 1 - y)]

        def rows(px, py, pc):
            return out_ref.at[4 * px + 2 * py + pc]

        def copy(k, block, to, src=None):
            return pltpu.make_async_remote_copy(
                src_ref=rows(*block) if src is None else src, dst_ref=rows(*block),
                send_sem=send_sems.at[k], recv_sem=recv_sems.at[k], device_id=to, device_id_type=MESH)

        mine = pltpu.make_async_copy(x_ref, rows(*me), local_sem)
        mine.start()
        first = [copy(0, me, sibling, src=x_ref)]
        first += [copy(1 + j, me, (*chip, c), src=x_ref) for j, chip in enumerate(chips)]
        for cp in first:
            cp.start()
        passed = [copy(4 + j, (*chip, c), sibling) for j, chip in enumerate(chips)]
        for j, chip in enumerate(chips):
            copy(1 + j, (*chip, c), me).wait_recv()
            passed[j].start()
        copy(0, sibling, me).wait_recv()
        for j, chip in enumerate(chips):
            copy(4 + j, (*chip, 1 - c), me).wait_recv()
        for cp in first + passed:
            cp.wait_send()
        mine.wait()

    space = pltpu.VMEM if in_vmem else pltpu.HBM
    return pl.pallas_call(
        body, name=name,
        out_shape=jax.ShapeDtypeStruct((N_DEV, r, c_), shard.dtype),
        in_specs=[pl.BlockSpec(memory_space=space)], out_specs=pl.BlockSpec(memory_space=space),
        scratch_shapes=[pltpu.SemaphoreType.DMA((7,)), pltpu.SemaphoreType.DMA((7,)), pltpu.SemaphoreType.DMA],
        compiler_params=_cp(),
    )(shard)


def exchange_sibling(parts):
    _, r, c_ = parts.shape

    def body(p_ref, out_ref, send_sems, recv_sems):
        x, y, c = _position()

        def copy(k):
            return pltpu.make_async_remote_copy(
                src_ref=p_ref.at[2 * k + (1 - c)], dst_ref=out_ref.at[k],
                send_sem=send_sems.at[k], recv_sem=recv_sems.at[k], device_id=(x, y, 1 - c), device_id_type=MESH)

        for k in range(4):
            copy(k).start()
        for k in range(4):
            copy(k).wait_recv()
        for k in range(4):
            copy(k).wait_send()

    hbm = pl.BlockSpec(memory_space=pltpu.HBM)
    return pl.pallas_call(
        body, name="rs_exchange_sibling",
        out_shape=jax.ShapeDtypeStruct((4, r, c_), parts.dtype),
        in_specs=[hbm], out_specs=hbm,
        scratch_shapes=[pltpu.SemaphoreType.DMA((4,)), pltpu.SemaphoreType.DMA((4,))],
        compiler_params=_cp(),
    )(parts)


def sum_for_chips(parts, from_sibling, c_idx, *, tr=480):
    _, r, c_ = parts.shape
    assert r % tr == 0

    def body(c_ref, p_ref, s_ref, o_ref):
        del c_ref
        o_ref[...] = (p_ref[...] + s_ref[...]).astype(o_ref.dtype)

    return pl.pallas_call(
        body, name="rs_sum_for_chips",
        grid_spec=pltpu.PrefetchScalarGridSpec(
            num_scalar_prefetch=1, grid=(4, r // tr),
            in_specs=[pl.BlockSpec((None, tr, c_), lambda k, i, c_ref: (2 * k + c_ref[0], i, 0)),
                      pl.BlockSpec((None, tr, c_), lambda k, i, c_ref: (k, i, 0))],
            out_specs=pl.BlockSpec((None, tr, c_), lambda k, i, c_ref: (k, i, 0))),
        out_shape=jax.ShapeDtypeStruct((4, r, c_), BF16), compiler_params=_cp(),
    )(c_idx, parts, from_sibling)


def exchange_chips(sums):
    _, r, c_ = sums.shape

    def body(s_ref, out_ref, send_sems, recv_sems):
        x, y, c = _position()
        chips = [(1 - x, y), (x, 1 - y), (1 - x, 1 - y)]

        def copy(j):
            px, py = chips[j]
            return pltpu.make_async_remote_copy(
                src_ref=s_ref.at[2 * px + py], dst_ref=out_ref.at[j],
                send_sem=send_sems.at[j], recv_sem=recv_sems.at[j], device_id=(px, py, c), device_id_type=MESH)

        for j in range(3):
            copy(j).start()
        for j in range(3):
            copy(j).wait_recv()
        for j in range(3):
            copy(j).wait_send()

    hbm = pl.BlockSpec(memory_space=pltpu.HBM)
    return pl.pallas_call(
        body, name="rs_exchange_chips",
        out_shape=jax.ShapeDtypeStruct((3, r, c_), sums.dtype),
        in_specs=[hbm], out_specs=hbm,
        scratch_shapes=[pltpu.SemaphoreType.DMA((3,)), pltpu.SemaphoreType.DMA((3,))],
        compiler_params=_cp(),
    )(sums)


def sum_final(parts, from_sibling, from_chips, kc_idx, *, tr=480):
    _, r, c_ = parts.shape

    def body(kc_ref, p_ref, s_ref, a_ref, b_ref, d_ref, o_ref):
        del kc_ref
        o_ref[...] = (((p_ref[...] + s_ref[...]) + a_ref[...].astype(F32)) + b_ref[...].astype(F32)) + d_ref[...].astype(F32)

    def chip_spec(j):
        return pl.BlockSpec((None, tr, c_), lambda i, kc: (j, i, 0))

    return pl.pallas_call(
        body, name="rs_sum_final",
        grid_spec=pltpu.PrefetchScalarGridSpec(
            num_scalar_prefetch=1, grid=(r // tr,),
            in_specs=[pl.BlockSpec((None, tr, c_), lambda i, kc: (2 * kc[0] + kc[1], i, 0)),
                      pl.BlockSpec((None, tr, c_), lambda i, kc: (kc[0], i, 0)),
                      chip_spec(0), chip_spec(1), chip_spec(2)],
            out_specs=pl.BlockSpec((tr, c_), lambda i, kc: (i, 0))),
        out_shape=jax.ShapeDtypeStruct((r, c_), F32), compiler_params=_cp(),
    )(kc_idx, parts, from_sibling, from_chips, from_chips, from_chips)


def sum_devices(gathered):
    n, r, c_ = gathered.shape

    def body(g_ref, o_ref):
        acc = g_ref[0]
        for k in range(1, n):
            acc = acc + g_ref[k]
        o_ref[...] = acc

    return pl.pallas_call(
        body, name="sum_devices", out_shape=jax.ShapeDtypeStruct((r, c_), F32), compiler_params=_cp(),
    )(gathered)


BIG = (
    ("w_in", IN_COLS, True), ("w_out", D_MODEL, False), ("wq_x", D_MODEL, False), ("wkv_x", 2 * D_MODEL, True),
    ("wo_x", D_MODEL, False), ("w_gate_up", 2 * D_FF, True), ("w_down", D_FF, False),
)
SHARD_ROWS = sum(rows // N_DEV for _, rows, _ in BIG)

SMALL = ("norm_mix_g", "q_norm_g", "k_norm_g", "sinks", "conv_b", "conv_ln_g", "conv_ln_b",
         "norm_x_g", "norm_mem_g", "xq_norm_g", "xk_norm_g", "norm_ffn_g")


def _pack_rows(vectors, width=LANES, row_multiple=8):
    flat = jnp.concatenate([v.reshape(-1) for v in vectors])
    per = width * row_multiple
    padded = -(-flat.shape[0] // per) * per
    return jnp.pad(flat, (0, padded - flat.shape[0])).reshape(-1, width)


def _unpack_rows(packed, shapes):
    flat = packed.reshape(-1)
    out, at = [], 0
    for s in shapes:
        n = 1
        for dim in s:
            n *= dim
        out.append(flat[at:at + n].reshape(s))
        at += n
    return out


def _layer_fwd(x0, mem, w, s):
    h0 = rms_fwd(x0, s["norm_mix_g"])
    u = mm(h0, w["w_in"], trans_b=True, out_dtype=F32, name="mm_u")
    mixed = swa_fwd(u, s["q_norm_g"], s["k_norm_g"], s["sinks"])
    mixed = conv_fwd(u, mixed, s["conv_w"], s["conv_b"], s["conv_ln_g"], s["conv_ln_b"])
    x1 = mm(mixed, w["w_out"], trans_b=False, out_dtype=F32, res=x0, name="mm_x1")
    h1 = rms_fwd(x1, s["norm_x_g"])
    qx = mm(h1, w["wq_x"], trans_b=False, out_dtype=F32, name="mm_qx")
    memn = rms_fwd(mem, s["norm_mem_g"])
    kv = mm(memn, w["wkv_x"], trans_b=True, out_dtype=F32, name="mm_kv")
    o = xattn_fwd(qx, kv, s["xq_norm_g"], s["xk_norm_g"])
    x2 = mm(o, w["wo_x"], trans_b=False, out_dtype=F32, res=x1, name="mm_x2")
    h2 = rms_fwd(x2, s["norm_ffn_g"])
    gu = mm(h2, w["w_gate_up"], trans_b=True, out_dtype=F32, name="mm_gu")
    a = swiglu_fwd(gu)
    x3 = mm(a, w["w_down"], trans_b=False, out_dtype=F32, res=x2, tm=512, name="mm_x3")
    saved = dict(x0=x0, h0=h0, u=u, mixed=mixed, x1=x1, h1=h1, qx=qx, memn=memn, kv=kv, o=o, x2=x2, h2=h2, gu=gu, a=a)
    return x3, saved


def _layer_bwd(dx3, mem, w, s, sv):
    gb, gs = {}, {}
    da = mm(dx3, w["w_down"], trans_b=True, out_dtype=F32, name="mm_da")
    gb["w_down"] = mm_tn(sv["a"], dx3, name="mm_dw_down")
    dgu = swiglu_bwd(sv["gu"], da)
    dh2 = mm(dgu, w["w_gate_up"], trans_b=False, out_dtype=F32, tm=512, name="mm_dh2")
    gb["w_gate_up"] = mm_tn(dgu, sv["h2"], name="mm_dw_gate_up")
    dx2, dg = rms_bwd(dh2, sv["x2"], s["norm_ffn_g"], dx3)
    gs["norm_ffn_g"] = dg

    do = mm(dx2, w["wo_x"], trans_b=True, out_dtype=F32, name="mm_do")
    gb["wo_x"] = mm_tn(sv["o"], dx2, name="mm_dwo")
    dq, dkv, dqg, dkg = xattn_bwd(sv["qx"], sv["kv"], do, s["xq_norm_g"], s["xk_norm_g"])
    gs["xq_norm_g"], gs["xk_norm_g"] = dqg, dkg
    dh1 = mm(dq, w["wq_x"], trans_b=True, out_dtype=F32, name="mm_dh1")
    gb["wq_x"] = mm_tn(sv["h1"], dq, name="mm_dwq")
    dx1, dg = rms_bwd(dh1, sv["x1"], s["norm_x_g"], dx2)
    gs["norm_x_g"] = dg
    dmemn = mm(dkv, w["wkv_x"], trans_b=False, out_dtype=F32, name="mm_dmemn")
    gb["wkv_x"] = mm_tn(dkv, sv["memn"], name="mm_dwkv")
    _, dg = rms_bwd(dmemn, mem, s["norm_mem_g"], None)
    gs["norm_mem_g"] = dg

    dmixed = mm(dx1, w["w_out"], trans_b=True, out_dtype=F32, name="mm_dmixed")
    gb["w_out"] = mm_tn(sv["mixed"], dx1, name="mm_dw_out")
    du, dqg, dkg, dsinks = swa_bwd(sv["u"], dmixed, s["q_norm_g"], s["k_norm_g"], s["sinks"])
    gs["q_norm_g"], gs["k_norm_g"], gs["sinks"] = dqg, dkg, dsinks[0, :N_Q_HEADS]
    du, dconv_w, dvec = conv_bwd(sv["u"], dmixed, du, s["conv_w"], s["conv_b"], s["conv_ln_g"], s["conv_ln_b"])
    gs["conv_w"] = dconv_w[:CONV_K]
    gs["conv_b"], gs["conv_ln_g"], gs["conv_ln_b"] = dvec[0], dvec[1], dvec[2]
    dh0 = mm(du, w["w_in"], trans_b=False, out_dtype=F32, name="mm_dh0")
    gb["w_in"] = mm_tn(du, sv["h0"], name="mm_dw_in")
    dx0, dg = rms_bwd(dh0, sv["x0"], s["norm_mix_g"], dx1)
    gs["norm_mix_g"] = dg
    return dx0, gb, gs


def _local_step(x, mem, target, weights, smalls):
    saved = []
    h = x
    for l in range(DEPTH):
        h, sv = _layer_fwd(h, mem, weights[l], smalls[l])
        saved.append(sv)
    dx, loss_part = loss_head(h, target)
    gbs, gss = [None] * DEPTH, [None] * DEPTH
    for l in reversed(range(DEPTH)):
        dx, gbs[l], gss[l] = _layer_bwd(dx, mem, weights[l], smalls[l], saved[l])
    return loss_part[0, 0], dx, gbs, gss


def kernel(x, mem, norm_mix_g, w_in, q_norm_g, k_norm_g, sinks, conv_w, conv_b, conv_ln_g, conv_ln_b, w_out, norm_x_g, norm_mem_g, wq_x, wkv_x, xq_norm_g, xk_norm_g, wo_x, norm_ffn_g, w_gate_up, w_down, loss_target, m_norm_mix_g, m_w_in, m_q_norm_g, m_k_norm_g, m_sinks, m_conv_w, m_conv_b, m_conv_ln_g, m_conv_ln_b, m_w_out, m_norm_x_g, m_norm_mem_g, m_wq_x, m_wkv_x, m_xq_norm_g, m_xk_norm_g, m_wo_x, m_norm_ffn_g, m_w_gate_up, m_w_down, v_norm_mix_g, v_w_in, v_q_norm_g, v_k_norm_g, v_sinks, v_conv_w, v_conv_b, v_conv_ln_g, v_conv_ln_b, v_w_out, v_norm_x_g, v_norm_mem_g, v_wq_x, v_wkv_x, v_xq_norm_g, v_xk_norm_g, v_wo_x, v_norm_ffn_g, v_w_gate_up, v_w_down):
    P = dict(norm_mix_g=norm_mix_g, w_in=w_in, q_norm_g=q_norm_g, k_norm_g=k_norm_g, sinks=sinks, conv_w=conv_w, conv_b=conv_b,
             conv_ln_g=conv_ln_g, conv_ln_b=conv_ln_b, w_out=w_out, norm_x_g=norm_x_g, norm_mem_g=norm_mem_g, wq_x=wq_x,
             wkv_x=wkv_x, xq_norm_g=xq_norm_g, xk_norm_g=xk_norm_g, wo_x=wo_x, norm_ffn_g=norm_ffn_g, w_gate_up=w_gate_up,
             w_down=w_down)
    M = dict(norm_mix_g=m_norm_mix_g, w_in=m_w_in, q_norm_g=m_q_norm_g, k_norm_g=m_k_norm_g, sinks=m_sinks, conv_w=m_conv_w,
             conv_b=m_conv_b, conv_ln_g=m_conv_ln_g, conv_ln_b=m_conv_ln_b, w_out=m_w_out, norm_x_g=m_norm_x_g,
             norm_mem_g=m_norm_mem_g, wq_x=m_wq_x, wkv_x=m_wkv_x, xq_norm_g=m_xq_norm_g, xk_norm_g=m_xk_norm_g, wo_x=m_wo_x,
             norm_ffn_g=m_norm_ffn_g, w_gate_up=m_w_gate_up, w_down=m_w_down)
    V = dict(norm_mix_g=v_norm_mix_g, w_in=v_w_in, q_norm_g=v_q_norm_g, k_norm_g=v_k_norm_g, sinks=v_sinks, conv_w=v_conv_w,
             conv_b=v_conv_b, conv_ln_g=v_conv_ln_g, conv_ln_b=v_conv_ln_b, w_out=v_w_out, norm_x_g=v_norm_x_g,
             norm_mem_g=v_norm_mem_g, wq_x=v_wq_x, wkv_x=v_wkv_x, xq_norm_g=v_xq_norm_g, xk_norm_g=v_xk_norm_g, wo_x=v_wo_x,
             norm_ffn_g=v_norm_ffn_g, w_gate_up=v_w_gate_up, w_down=v_w_down)
    order = ["norm_mix_g", "w_in", "q_norm_g", "k_norm_g", "sinks", "conv_w", "conv_b", "conv_ln_g", "conv_ln_b", "w_out",
             "norm_x_g", "norm_mem_g", "wq_x", "wkv_x", "xq_norm_g", "xk_norm_g", "wo_x", "norm_ffn_g", "w_gate_up", "w_down"]
    xi, yi, ci = _position()
    dev = 4 * xi + 2 * yi + ci
    x2d, mem2d, tgt2d = x[0], mem[0], loss_target[0]

    def travelling(name, l, transposed):
        a = P[name][l]
        return (a.T if transposed else a).astype(BF16)

    packed = jnp.concatenate([travelling(n, l, tr) for l in range(DEPTH) for n, _, tr in BIG], axis=0)
    gathered = all_gather(packed, name="ag_weights", in_vmem=False)
    weights = []
    for l in range(DEPTH):
        at, wl = l * SHARD_ROWS, {}
        for n, rows, _ in BIG:
            wl[n] = gathered[:, at:at + rows // N_DEV, :].reshape(rows, D_MODEL)
            at += rows // N_DEV
        weights.append(wl)
    cw = jnp.pad(conv_w.reshape(DEPTH * CONV_K, CONV_CH // N_DEV), ((0, 2), (0, LANES - CONV_CH // N_DEV)))
    cw_all = all_gather(cw, name="ag_conv_w", in_vmem=True)
    cw_full = cw_all[:, :DEPTH * CONV_K, :CONV_CH // N_DEV].reshape(N_DEV, DEPTH, CONV_K, CONV_CH // N_DEV)
    cw_full = jnp.transpose(cw_full, (1, 2, 0, 3)).reshape(DEPTH, CONV_K, CONV_CH)
    smalls = []
    for l in range(DEPTH):
        sl = {n: P[n][l] for n in SMALL}
        sl["conv_w"] = jnp.pad(cw_full[l], ((0, CONV_HALO - CONV_K), (0, 0)))
        smalls.append(sl)

    loss_part, grad_x, gbs, gss = _local_step(x2d, mem2d, tgt2d, weights, smalls)
    loss = lax.psum(loss_part, ("x", "y", "c"))

    parts = jnp.concatenate(
        [gbs[l][n].reshape(N_DEV, rows // N_DEV, D_MODEL) for l in range(DEPTH) for n, rows, _ in BIG], axis=1)
    c_idx = jnp.reshape(ci, (1,)).astype(jnp.int32)
    kc_idx = jnp.stack([2 * xi + yi, ci]).astype(jnp.int32)
    from_sibling = exchange_sibling(parts)
    chip_sums = sum_for_chips(parts, from_sibling, c_idx)
    from_chips = exchange_chips(chip_sums)
    reduced = sum_final(parts, from_sibling, from_chips, kc_idx)
    grads, at = {}, 0
    for n, rows, transposed in BIG:
        per_layer = [reduced[l * SHARD_ROWS + at:l * SHARD_ROWS + at + rows // N_DEV] for l in range(DEPTH)]
        grads[n] = jnp.stack([g.T if transposed else g for g in per_layer])
        at += rows // N_DEV

    small_names = SMALL + ("conv_w",)
    small_shapes = [(DEPTH,) + ((CONV_K, CONV_CH) if n == "conv_w" else P[n].shape[1:]) for n in small_names]
    small_parts = _pack_rows([jnp.stack([gss[l][n].reshape(sh[1:]) for l in range(DEPTH)])
                              for n, sh in zip(small_names, small_shapes)])
    small_sum = sum_devices(all_gather(small_parts, name="ag_small_grads", in_vmem=True))
    for n, g in zip(small_names, _unpack_rows(small_sum, small_shapes)):
        if n == "conv_w":
            g = lax.dynamic_slice_in_dim(g, dev * (CONV_CH // N_DEV), CONV_CH // N_DEV, axis=2)
        grads[n] = g

    delta, new_m, new_v = {}, {}, {}
    for n, _, _ in BIG:
        shape = P[n].shape
        two_d = lambda a: a.reshape(shape[0] * shape[1], shape[2])
        d_, m_, v_ = adamw(two_d(P[n]), two_d(grads[n]), two_d(M[n]), two_d(V[n]), name="adamw_" + n)
        delta[n], new_m[n], new_v[n] = d_.reshape(shape), m_.reshape(shape), v_.reshape(shape)
    shapes = [P[n].shape for n in small_names]
    d_, m_, v_ = adamw(_pack_rows([P[n] for n in small_names]), _pack_rows([grads[n] for n in small_names]),
                       _pack_rows([M[n] for n in small_names]), _pack_rows([V[n] for n in small_names]), name="adamw_small")
    for n, dd, mm_, vv in zip(small_names, _unpack_rows(d_, shapes), _unpack_rows(m_, shapes), _unpack_rows(v_, shapes)):
        delta[n], new_m[n], new_v[n] = dd, mm_, vv

    return (loss, grad_x[None], *[grads[n] for n in order], *[delta[n] for n in order],
            *[new_m[n] for n in order], *[new_v[n] for n in order])
```

```python
import functools

import jax
import jax.numpy as jnp
from jax import lax
from jax.experimental import pallas as pl
from jax.experimental.pallas import tpu as pltpu

F32 = jnp.float32
BF16 = jnp.bfloat16

D_MODEL = 1024
HEAD_DIM = 64
N_Q_HEADS = 8
N_KV_HEADS = 2
GROUP = N_Q_HEADS // N_KV_HEADS
ATTN_WIDTH = N_Q_HEADS * HEAD_DIM
KV_WIDTH = N_KV_HEADS * HEAD_DIM
QKV_WIDTH = ATTN_WIDTH + 2 * KV_WIDTH
CONV_CH = 512
IN_COLS = QKV_WIDTH + 2 * CONV_CH
CONV_K = 31
CONV_HALO = 32
BLOCK = 128
N_X_HEADS = 4
X_HEAD_DIM = 256
D_FF = 2816
EPS = 1e-6
NEG = -1e30
DEPTH = 2
N_DEV = 8

ADAM_LR = 0.001
ADAM_B1 = 0.9
ADAM_B2 = 0.999
ADAM_EPS = 1e-08
ADAM_WD = 0.01
ADAM_STEP = 10

V7X_VMEM_LIMIT = 56 * 1024 * 1024
LANES = 128

MESH = pl.DeviceIdType.MESH


def _cp(**kw):
    return pltpu.CompilerParams(vmem_limit_bytes=V7X_VMEM_LIMIT, **kw)


def _dot(a, b, dims):
    return lax.dot_general(a.astype(BF16), b.astype(BF16), (dims, ((), ())), preferred_element_type=F32)


def _dot_nn(a, b):
    return _dot(a, b, ((1,), (0,)))


def _dot_nt(a, b):
    return _dot(a, b, ((1,), (1,)))


def _dot_tn(a, b):
    return _dot(a, b, ((0,), (0,)))


def _sigmoid(x):
    return jax.nn.sigmoid(x)


def _rms(x):
    r = lax.rsqrt(jnp.mean(x * x, axis=-1, keepdims=True) + EPS)
    return x * r, r


def _rms_bwd(dy, xhat, r, g):
    dxh = dy * g
    return r * (dxh - xhat * jnp.mean(dxh * xhat, axis=-1, keepdims=True))


def rms_fwd(x, g, *, tm=512):
    m, d = x.shape
    tm = min(tm, m)

    def body(x_ref, g_ref, o_ref):
        xh, _ = _rms(x_ref[...])
        o_ref[...] = (xh * g_ref[...]).astype(o_ref.dtype)

    return pl.pallas_call(
        body, name="rms_fwd", grid=(m // tm,),
        in_specs=[pl.BlockSpec((tm, d), lambda i: (i, 0)), pl.BlockSpec((1, d), lambda i: (0, 0))],
        out_specs=pl.BlockSpec((tm, d), lambda i: (i, 0)),
        out_shape=jax.ShapeDtypeStruct((m, d), BF16), compiler_params=_cp(),
    )(x, g.reshape(1, d))


def rms_bwd(dh, x, g, dres, *, tm=512):
    m, d = x.shape
    tm = min(tm, m)
    has_res = dres is not None

    def body(*refs):
        if has_res:
            dh_ref, x_ref, g_ref, r_ref, dx_ref, dg_ref = refs
        else:
            dh_ref, x_ref, g_ref, dx_ref, dg_ref = refs
        xh, r = _rms(x_ref[...])
        dy = dh_ref[...].astype(F32)

        @pl.when(pl.program_id(0) == 0)
        def _():
            dg_ref[...] = jnp.zeros_like(dg_ref)

        dg_ref[...] += jnp.sum(dy * xh, axis=0, keepdims=True)
        dx = _rms_bwd(dy, xh, r, g_ref[...])
        if has_res:
            dx = dx + r_ref[...]
        dx_ref[...] = dx

    row = pl.BlockSpec((tm, d), lambda i: (i, 0))
    vec = pl.BlockSpec((1, d), lambda i: (0, 0))
    ins = [dh, x, g.reshape(1, d)] + ([dres] if has_res else [])
    return pl.pallas_call(
        body, name="rms_bwd" + ("_res" if has_res else ""), grid=(m // tm,),
        in_specs=[row, row, vec] + ([row] if has_res else []),
        out_specs=[row, vec],
        out_shape=[jax.ShapeDtypeStruct((m, d), F32), jax.ShapeDtypeStruct((1, d), F32)],
        compiler_params=_cp(),
    )(*ins)


def swiglu_fwd(gu, *, tm=512):
    m, f2 = gu.shape
    f = f2 // 2

    def body(g_ref, u_ref, o_ref):
        g = g_ref[...]
        o_ref[...] = (g * _sigmoid(g) * u_ref[...]).astype(o_ref.dtype)

    return pl.pallas_call(
        body, name="swiglu_fwd", grid=(m // tm,),
        in_specs=[pl.BlockSpec((tm, f), lambda i: (i, 0)), pl.BlockSpec((tm, f), lambda i: (i, 1))],
        out_specs=pl.BlockSpec((tm, f), lambda i: (i, 0)),
        out_shape=jax.ShapeDtypeStruct((m, f), BF16), compiler_params=_cp(),
    )(gu, gu)


def swiglu_bwd(gu, da, *, tm=256):
    m, f2 = gu.shape
    f = f2 // 2

    def body(gu_ref, da_ref, o_ref):
        g = gu_ref[:, :f]
        u = gu_ref[:, f:]
        da_v = da_ref[...]
        sg = _sigmoid(g)
        o_ref[:, :f] = (da_v * u * (sg * (1.0 + g * (1.0 - sg)))).astype(o_ref.dtype)
        o_ref[:, f:] = (da_v * (g * sg)).astype(o_ref.dtype)

    return pl.pallas_call(
        body, name="swiglu_bwd", grid=(m // tm,),
        in_specs=[pl.BlockSpec((tm, f2), lambda i: (i, 0)), pl.BlockSpec((tm, f), lambda i: (i, 0))],
        out_specs=pl.BlockSpec((tm, f2), lambda i: (i, 0)),
        out_shape=jax.ShapeDtypeStruct((m, f2), BF16), compiler_params=_cp(),
    )(gu, da)


def loss_head(y, target, *, tm=512):
    m, d = y.shape

    def body(y_ref, t_ref, dy_ref, l_ref):
        err = y_ref[...] - t_ref[...]
        dy_ref[...] = err * (1.0 / d)

        @pl.when(pl.program_id(0) == 0)
        def _():
            l_ref[...] = jnp.zeros_like(l_ref)

        part = jnp.sum(jnp.sum(err * err, axis=-1, keepdims=True), axis=0, keepdims=True)
        l_ref[...] += jnp.broadcast_to(part * (0.5 / d), l_ref.shape)

    row = pl.BlockSpec((tm, d), lambda i: (i, 0))
    return pl.pallas_call(
        body, name="loss_head", grid=(m // tm,),
        in_specs=[row, row],
        out_specs=[row, pl.BlockSpec((1, LANES), lambda i: (0, 0))],
        out_shape=[jax.ShapeDtypeStruct((m, d), F32), jax.ShapeDtypeStruct((1, LANES), F32)],
        compiler_params=_cp(),
    )(y, target)


def _tile(n, cap):
    if n <= cap:
        return n
    best = None
    for t in range(LANES, cap + 1, LANES):
        if n % t == 0:
            best = t
    assert best is not None, (n, cap)
    return best


def mm(a, b, *, trans_b, out_dtype, res=None, tm=1024, tn_cap=1536, name):
    m, k = a.shape
    n = b.shape[0] if trans_b else b.shape[1]
    assert (b.shape[1] if trans_b else b.shape[0]) == k
    tm = min(tm, m)
    tn = _tile(n, tn_cap)
    has_res = res is not None

    def body(*refs):
        if has_res:
            a_ref, b_ref, r_ref, o_ref = refs
        else:
            a_ref, b_ref, o_ref = refs
        acc = _dot_nt(a_ref[...], b_ref[...]) if trans_b else _dot_nn(a_ref[...], b_ref[...])
        if has_res:
            acc = acc + r_ref[...]
        o_ref[...] = acc.astype(o_ref.dtype)

    b_spec = pl.BlockSpec((tn, k), lambda i, j: (j, 0)) if trans_b else pl.BlockSpec((k, tn), lambda i, j: (0, j))
    o_spec = pl.BlockSpec((tm, tn), lambda i, j: (i, j))
    return pl.pallas_call(
        body, name=name, grid=(m // tm, n // tn),
        in_specs=[pl.BlockSpec((tm, k), lambda i, j: (i, 0)), b_spec] + ([o_spec] if has_res else []),
        out_specs=o_spec,
        out_shape=jax.ShapeDtypeStruct((m, n), out_dtype), compiler_params=_cp(),
    )(*([a, b] + ([res] if has_res else [])))


def mm_tn(a, b, *, name, ta_cap=1536, tb_cap=1024, tk=1024):
    m, ka = a.shape
    nb = b.shape[1]
    assert b.shape[0] == m
    tk = min(tk, m)
    ta = _tile(ka, ta_cap)
    tb = _tile(nb, tb_cap)

    def body(a_ref, b_ref, o_ref):
        @pl.when(pl.program_id(2) == 0)
        def _():
            o_ref[...] = jnp.zeros_like(o_ref)

        o_ref[...] += _dot_tn(a_ref[...], b_ref[...])

    return pl.pallas_call(
        body, name=name, grid=(ka // ta, nb // tb, m // tk),
        in_specs=[pl.BlockSpec((tk, ta), lambda i, j, kk: (kk, i)), pl.BlockSpec((tk, tb), lambda i, j, kk: (kk, j))],
        out_specs=pl.BlockSpec((ta, tb), lambda i, j, kk: (i, j)),
        out_shape=jax.ShapeDtypeStruct((ka, nb), F32), compiler_params=_cp(),
    )(a, b)


def _whole(shape):
    return pl.BlockSpec(shape, lambda i: (0,) * len(shape), pipeline_mode=pl.Buffered(1))


def _rows(tm, n):
    return pl.BlockSpec((tm, n), lambda i: (i, 0))


def _vec(n):
    return pl.BlockSpec((1, n), lambda i: (0, 0))


def _chunks(n, cap=1408):
    size = _tile(n, cap)
    return [(s, size) for s in range(0, n, size)]


def _zero_at_first_step(*refs):
    @pl.when(pl.program_id(0) == 0)
    def _():
        for r in refs:
            r[...] = jnp.zeros_like(r)


def norm_proj(x, g, wt, *, tm=512):
    m, d = x.shape
    n = wt.shape[0]

    def body(x_ref, g_ref, wt_ref, h_ref, u_ref):
        h = (_rms(x_ref[...])[0] * g_ref[...]).astype(BF16)
        h_ref[...] = h
        for s, sz in _chunks(n):
            u_ref[:, s:s + sz] = _dot_nt(h, wt_ref[s:s + sz, :])

    return pl.pallas_call(
        body, name="norm_proj", grid=(m // tm,),
        in_specs=[_rows(tm, d), _vec(d), _whole((n, d))],
        out_specs=[_rows(tm, d), _rows(tm, n)],
        out_shape=[jax.ShapeDtypeStruct((m, d), BF16), jax.ShapeDtypeStruct((m, n), F32)],
        compiler_params=_cp(),
    )(x, g.reshape(1, d), wt)


def _xattn_heads(q_ref, kv_ref, qg_v, kg_v, d):
    out = []
    for h in range(N_X_HEADS):
        cols = slice(h * X_HEAD_DIM, (h + 1) * X_HEAD_DIM)
        qh, rq = _rms(q_ref[:, cols])
        qn = qh * qg_v
        kn = _rms(kv_ref[:, cols])[0] * kg_v
        v = kv_ref[:, d + h * X_HEAD_DIM:d + (h + 1) * X_HEAD_DIM]
        out.append((qh, rq, qn, kn, v, _xattn_probs(qn, kn)))
    return out


def mid_fwd(mixed, x0, w_out, g_x, wq, kv, xqg, xkg, wo, g_f, *, tm=512):
    m, d = x0.shape
    n_mem = kv.shape[0]

    def body(mixed_ref, x0_ref, w_out_ref, g_x_ref, wq_ref, kv_ref, xqg_ref, xkg_ref, wo_ref, g_f_ref,
             x1_ref, h1_ref, qx_ref, o_ref, x2_ref, h2_ref):
        x1 = x0_ref[...] + _dot_nn(mixed_ref[...], w_out_ref[...])
        x1_ref[...] = x1
        h1 = (_rms(x1)[0] * g_x_ref[...]).astype(BF16)
        h1_ref[...] = h1
        qx_ref[...] = _dot_nn(h1, wq_ref[...])
        for h, (_, _, _, _, v, p) in enumerate(_xattn_heads(qx_ref, kv_ref, xqg_ref[...], xkg_ref[...], d)):
            o_ref[:, h * X_HEAD_DIM:(h + 1) * X_HEAD_DIM] = _dot_nn(p, v).astype(o_ref.dtype)
        x2 = x1 + _dot_nn(o_ref[...], wo_ref[...])
        x2_ref[...] = x2
        h2_ref[...] = (_rms(x2)[0] * g_f_ref[...]).astype(BF16)

    sq = _whole((d, d))
    f32_rows, bf_rows = jax.ShapeDtypeStruct((m, d), F32), jax.ShapeDtypeStruct((m, d), BF16)
    return pl.pallas_call(
        body, name="mid_fwd", grid=(m // tm,),
        in_specs=[_rows(tm, d), _rows(tm, d), sq, _vec(d), sq, _whole((n_mem, 2 * d)), _vec(X_HEAD_DIM), _vec(X_HEAD_DIM),
                  sq, _vec(d)],
        out_specs=[_rows(tm, d)] * 6,
        out_shape=[f32_rows, bf_rows, f32_rows, bf_rows, f32_rows, bf_rows],
        compiler_params=_cp(),
    )(mixed, x0, w_out, g_x.reshape(1, d), wq, kv, xqg.reshape(1, X_HEAD_DIM), xkg.reshape(1, X_HEAD_DIM), wo,
      g_f.reshape(1, d))


def ffn_fwd(h2, x2, wt_gu, w_down, *, tm=256):
    m, d = x2.shape
    f = w_down.shape[0]

    def body(h2_ref, x2_ref, wt_gu_ref, w_down_ref, gu_ref, a_ref, x3_ref):
        h = h2_ref[...]
        for s, sz in _chunks(2 * f):
            gu_ref[:, s:s + sz] = _dot_nt(h, wt_gu_ref[s:s + sz, :])
        for s, sz in _chunks(f):
            g = gu_ref[:, s:s + sz]
            a_ref[:, s:s + sz] = (g * _sigmoid(g) * gu_ref[:, f + s:f + s + sz]).astype(a_ref.dtype)
        x3_ref[...] = x2_ref[...] + _dot_nn(a_ref[...], w_down_ref[...])

    return pl.pallas_call(
        body, name="ffn_fwd", grid=(m // tm,),
        in_specs=[_rows(tm, d), _rows(tm, d), _whole((2 * f, d)), _whole((f, d))],
        out_specs=[_rows(tm, 2 * f), _rows(tm, f), _rows(tm, d)],
        out_shape=[jax.ShapeDtypeStruct((m, 2 * f), F32), jax.ShapeDtypeStruct((m, f), BF16),
                   jax.ShapeDtypeStruct((m, d), F32)],
        compiler_params=_cp(),
    )(h2, x2, wt_gu, w_down)


def ffn_bwd(dx3, gu, x2, g_f, w_down, wt_gu, *, tm=256):
    m, d = x2.shape
    f = w_down.shape[0]

    def body(dx3_ref, gu_ref, x2_ref, g_ref, w_down_ref, wt_gu_ref, dgu_ref, dx2_ref, dg_ref):
        _zero_at_first_step(dg_ref)
        dx3 = dx3_ref[...]
        dx3_b = dx3.astype(BF16)
        for s, sz in _chunks(f):
            da = _dot_nt(dx3_b, w_down_ref[s:s + sz, :])
            g = gu_ref[:, s:s + sz]
            u = gu_ref[:, f + s:f + s + sz]
            sg = _sigmoid(g)
            dgu_ref[:, s:s + sz] = (da * u * (sg * (1.0 + g * (1.0 - sg)))).astype(dgu_ref.dtype)
            dgu_ref[:, f + s:f + s + sz] = (da * (g * sg)).astype(dgu_ref.dtype)
        dh2 = _dot_nn(dgu_ref[...], wt_gu_ref[...])
        xh, r = _rms(x2_ref[...])
        dg_ref[...] += jnp.sum(dh2 * xh, axis=0, keepdims=True)
        dx2_ref[...] = dx3 + _rms_bwd(dh2, xh, r, g_ref[...])

    return pl.pallas_call(
        body, name="ffn_bwd", grid=(m // tm,),
        in_specs=[_rows(tm, d), _rows(tm, 2 * f), _rows(tm, d), _vec(d), _whole((f, d)), _whole((2 * f, d))],
        out_specs=[_rows(tm, 2 * f), _rows(tm, d), _vec(d)],
        out_shape=[jax.ShapeDtypeStruct((m, 2 * f), BF16), jax.ShapeDtypeStruct((m, d), F32),
                   jax.ShapeDtypeStruct((1, d), F32)],
        compiler_params=_cp(),
    )(dx3, gu, x2, g_f.reshape(1, d), w_down, wt_gu)


def mid_bwd(dx2, qx, kv, xqg, xkg, x1, g_x, wo, wq, w_out, *, tm=512):
    m, d = x1.shape
    n_mem = kv.shape[0]
    nt = m // tm

    def body(dx2_ref, qx_ref, kv_ref, xqg_ref, xkg_ref, x1_ref, g_x_ref, wo_ref, wq_ref, w_out_ref,
             dq_ref, dx1_ref, dmixed_ref, dkv_ref, dqg_ref, dkg_ref, dg_ref):
        i = pl.program_id(0)
        _zero_at_first_step(dkv_ref, dqg_ref, dkg_ref, dg_ref)
        qg_v, kg_v = xqg_ref[...], xkg_ref[...]
        dx2 = dx2_ref[...]
        do = _dot_nt(dx2, wo_ref[...])
        dqg_acc = jnp.zeros((1, X_HEAD_DIM), F32)
        for h, (qh, rq, qn, kn, v, p) in enumerate(_xattn_heads(qx_ref, kv_ref, qg_v, kg_v, d)):
            cols = slice(h * X_HEAD_DIM, (h + 1) * X_HEAD_DIM)
            vcols = slice(d + h * X_HEAD_DIM, d + (h + 1) * X_HEAD_DIM)
            do_h = do[:, cols]
            dp = _dot_nt(do_h, v)
            ds = p * (dp - jnp.sum(p * dp, axis=-1, keepdims=True))
            dkv_ref[:, vcols] += _dot_tn(p, do_h)
            dqn = _dot_nn(ds, kn) * (X_HEAD_DIM ** -0.5)
            dkv_ref[:, cols] += _dot_tn(ds, qn) * (X_HEAD_DIM ** -0.5)
            dqg_acc = dqg_acc + jnp.sum(dqn * qh, axis=0, keepdims=True)
            dq_ref[:, cols] = _rms_bwd(dqn, qh, rq, qg_v).astype(dq_ref.dtype)
        dqg_ref[...] += dqg_acc
        dh1 = _dot_nt(dq_ref[...], wq_ref[...])
        xh, r = _rms(x1_ref[...])
        dg_ref[...] += jnp.sum(dh1 * xh, axis=0, keepdims=True)
        dx1 = dx2 + _rms_bwd(dh1, xh, r, g_x_ref[...])
        dx1_ref[...] = dx1
        dmixed_ref[...] = _dot_nt(dx1, w_out_ref[...])

        @pl.when(i == nt - 1)
        def _():
            dkg_acc = jnp.zeros((1, X_HEAD_DIM), F32)
            for h in range(N_X_HEADS):
                cols = slice(h * X_HEAD_DIM, (h + 1) * X_HEAD_DIM)
                kh, rk = _rms(kv_ref[:, cols])
                dkn = dkv_ref[:, cols]
                dkg_acc = dkg_acc + jnp.sum(dkn * kh, axis=0, keepdims=True)
                dkv_ref[:, cols] = _rms_bwd(dkn, kh, rk, kg_v)
            dkg_ref[...] = dkg_acc

    sq = _whole((d, d))
    full = pl.BlockSpec((n_mem, 2 * d), lambda i: (0, 0))
    return pl.pallas_call(
        body, name="mid_bwd", grid=(nt,),
        in_specs=[_rows(tm, d), _rows(tm, d), _whole((n_mem, 2 * d)), _vec(X_HEAD_DIM), _vec(X_HEAD_DIM), _rows(tm, d),
                  _vec(d), sq, sq, sq],
        out_specs=[_rows(tm, d), _rows(tm, d), _rows(tm, d), full, _vec(X_HEAD_DIM), _vec(X_HEAD_DIM), _vec(d)],
        out_shape=[jax.ShapeDtypeStruct((m, d), BF16), jax.ShapeDtypeStruct((m, d), F32), jax.ShapeDtypeStruct((m, d), F32),
                   jax.ShapeDtypeStruct((n_mem, 2 * d), F32), jax.ShapeDtypeStruct((1, X_HEAD_DIM), F32),
                   jax.ShapeDtypeStruct((1, X_HEAD_DIM), F32), jax.ShapeDtypeStruct((1, d), F32)],
        compiler_params=_cp(),
    )(dx2, qx, kv, xqg.reshape(1, X_HEAD_DIM), xkg.reshape(1, X_HEAD_DIM), x1, g_x.reshape(1, d), wo, wq, w_out)


def in_bwd(du, wt_in, x0, g_mix, dx1, *, tm=512):
    m, d = x0.shape
    n = wt_in.shape[0]

    def body(du_ref, wt_ref, x0_ref, g_ref, dx1_ref, dx0_ref, dg_ref):
        _zero_at_first_step(dg_ref)
        dh0 = _dot_nn(du_ref[...], wt_ref[...])
        xh, r = _rms(x0_ref[...])
        dg_ref[...] += jnp.sum(dh0 * xh, axis=0, keepdims=True)
        dx0_ref[...] = dx1_ref[...] + _rms_bwd(dh0, xh, r, g_ref[...])

    return pl.pallas_call(
        body, name="in_bwd", grid=(m // tm,),
        in_specs=[_rows(tm, n), _whole((n, d)), _rows(tm, d), _vec(d), _rows(tm, d)],
        out_specs=[_rows(tm, d), _vec(d)],
        out_shape=[jax.ShapeDtypeStruct((m, d), F32), jax.ShapeDtypeStruct((1, d), F32)],
        compiler_params=_cp(),
    )(du, wt_in, x0, g_mix.reshape(1, d), dx1)


SWA_TILE = 512
SWA_SUB = SWA_TILE // BLOCK


def _swa_mask():
    rows = GROUP * BLOCK
    r = lax.broadcasted_iota(jnp.int32, (rows, 2 * BLOCK), 0)
    j = lax.broadcasted_iota(jnp.int32, (rows, 2 * BLOCK), 1)
    dist = (r & (BLOCK - 1)) + BLOCK - j
    return dist.astype(F32), (dist >= 0) & (dist < BLOCK), j >= BLOCK


def _slope_col(kv):
    return jnp.concatenate([jnp.full((BLOCK, 1), 2.0 ** -(kv * GROUP + g + 1), F32) for g in range(GROUP)], axis=0)


def _sink_col(sinks_ref, kv):
    return jnp.concatenate([jnp.full((BLOCK, 1), sinks_ref[kv * GROUP + g], F32) for g in range(GROUP)], axis=0)


def _stack_heads(ref, rows, kv):
    return jnp.concatenate(
        [ref[rows, (kv * GROUP + g) * HEAD_DIM:(kv * GROUP + g + 1) * HEAD_DIM] for g in range(GROUP)], axis=0)


def _swa_keys(cur_ref, prev_ref, b, col):
    cols = slice(col, col + HEAD_DIM)
    if b == 0:
        return jnp.concatenate([prev_ref[:, cols], cur_ref[0:BLOCK, cols]], axis=0)
    return cur_ref[(b - 1) * BLOCK:(b + 1) * BLOCK, cols]


def _swa_probs(qn, kn, bias, valid, sink):
    s = _dot_nt(qn, kn) * (HEAD_DIM ** -0.5)
    s = jnp.where(valid, s + bias, NEG)
    mx = jnp.maximum(jnp.max(s, axis=-1, keepdims=True), sink)
    e = jnp.exp(s - mx)
    es = jnp.exp(sink - mx)
    den = jnp.sum(e, axis=-1, keepdims=True) + es
    return e / den, es / den


def swa_fwd(u, qg, kg, sinks):
    t = u.shape[0]
    nt = t // SWA_TILE

    def body(sinks_ref, cur_ref, prev_ref, qg_ref, kg_ref, o_ref):
        i = pl.program_id(0)
        qg_v = qg_ref[...]
        kg_v = kg_ref[...]
        dist, window, own_block = _swa_mask()
        valid_first = window & (own_block | (i > 0))
        for kv in range(N_KV_HEADS):
            sink = _sink_col(sinks_ref, kv)
            bias = -_slope_col(kv) * dist
            for b in range(SWA_SUB):
                rows = slice(b * BLOCK, (b + 1) * BLOCK)
                valid = valid_first if b == 0 else window
                qn = _rms(_stack_heads(cur_ref, rows, kv))[0] * qg_v
                kn = _rms(_swa_keys(cur_ref, prev_ref, b, ATTN_WIDTH + kv * HEAD_DIM))[0] * kg_v
                vv = _swa_keys(cur_ref, prev_ref, b, ATTN_WIDTH + KV_WIDTH + kv * HEAD_DIM)
                p, _ = _swa_probs(qn, kn, bias, valid, sink)
                o4 = _dot_nn(p, vv)
                for g in range(GROUP):
                    h = kv * GROUP + g
                    o_ref[rows, h * HEAD_DIM:(h + 1) * HEAD_DIM] = o4[g * BLOCK:(g + 1) * BLOCK].astype(o_ref.dtype)

    vec = pl.BlockSpec((1, HEAD_DIM), lambda i: (0, 0))
    return pl.pallas_call(
        body, name="swa_fwd", grid=(nt,),
        in_specs=[
            pl.BlockSpec(memory_space=pltpu.SMEM),
            pl.BlockSpec((SWA_TILE, QKV_WIDTH), lambda i: (i, 0)),
            pl.BlockSpec((BLOCK, QKV_WIDTH), lambda i: (jnp.maximum(i * SWA_SUB - 1, 0), 0)),
            vec, vec,
        ],
        out_specs=pl.BlockSpec((SWA_TILE, ATTN_WIDTH), lambda i: (i, 0)),
        out_shape=jax.ShapeDtypeStruct((t, 2 * ATTN_WIDTH), BF16), compiler_params=_cp(),
    )(sinks, u, u, qg.reshape(1, HEAD_DIM), kg.reshape(1, HEAD_DIM))


def swa_bwd(u, dmixed, qg, kg, sinks):
    t = u.shape[0]
    nt = t // SWA_TILE
    kcol = lambda kv: slice(kv * HEAD_DIM, (kv + 1) * HEAD_DIM)
    vcol = lambda kv: slice(KV_WIDTH + kv * HEAD_DIM, KV_WIDTH + (kv + 1) * HEAD_DIM)

    def body(sinks_ref, cur_ref, prev_ref, do_ref, qg_ref, kg_ref, du_ref, dqg_ref, dkg_ref, dsk_ref, acc_ref, carry_ref):
        step = pl.program_id(0)
        i = nt - 1 - step
        qg_v = qg_ref[...]
        kg_v = kg_ref[...]

        @pl.when(step == 0)
        def _():
            carry_ref[...] = jnp.zeros_like(carry_ref)
            dqg_ref[...] = jnp.zeros_like(dqg_ref)
            dkg_ref[...] = jnp.zeros_like(dkg_ref)
            dsk_ref[...] = jnp.zeros_like(dsk_ref)

        acc_ref[0:SWA_TILE, :] = jnp.zeros((SWA_TILE, 2 * KV_WIDTH), F32)
        acc_ref[SWA_TILE:SWA_TILE + BLOCK, :] = carry_ref[...]

        lane = lax.broadcasted_iota(jnp.int32, (1, LANES), 1)
        dqg_acc = jnp.zeros((1, HEAD_DIM), F32)
        dsk_acc = jnp.zeros((1, LANES), F32)
        dist, window, own_block = _swa_mask()
        valid_first = window & (own_block | (i > 0))
        for kv in range(N_KV_HEADS):
            sink = _sink_col(sinks_ref, kv)
            bias = -_slope_col(kv) * dist
            for b in range(SWA_SUB):
                rows = slice(b * BLOCK, (b + 1) * BLOCK)
                valid = valid_first if b == 0 else window
                qh, rq = _rms(_stack_heads(cur_ref, rows, kv))
                qn = qh * qg_v
                kn = _rms(_swa_keys(cur_ref, prev_ref, b, ATTN_WIDTH + kv * HEAD_DIM))[0] * kg_v
                vv = _swa_keys(cur_ref, prev_ref, b, ATTN_WIDTH + KV_WIDTH + kv * HEAD_DIM)
                p, ps = _swa_probs(qn, kn, bias, valid, sink)
                do4 = _stack_heads(do_ref, rows, kv)
                dp = _dot_nt(do4, vv)
                delta = jnp.sum(p * dp, axis=-1, keepdims=True)
                ds = p * (dp - delta)
                dsink = -ps * delta
                for g in range(GROUP):
                    part = jnp.sum(dsink[g * BLOCK:(g + 1) * BLOCK], axis=0, keepdims=True)
                    dsk_acc = dsk_acc + jnp.where(lane == kv * GROUP + g, part, 0.0)
                dvv = _dot_tn(p, do4)
                dqn = _dot_nn(ds, kn) * (HEAD_DIM ** -0.5)
                dkn = _dot_tn(ds, qn) * (HEAD_DIM ** -0.5)
                dqg_acc = dqg_acc + jnp.sum(dqn * qh, axis=0, keepdims=True)
                dq = _rms_bwd(dqn, qh, rq, qg_v)
                for g in range(GROUP):
                    h = kv * GROUP + g
                    du_ref[rows, h * HEAD_DIM:(h + 1) * HEAD_DIM] = dq[g * BLOCK:(g + 1) * BLOCK].astype(du_ref.dtype)
                keys = slice(b * BLOCK, (b + 2) * BLOCK)
                acc_ref[keys, kcol(kv)] += dkn
                acc_ref[keys, vcol(kv)] += dvv
        dqg_ref[...] += dqg_acc
        dsk_ref[...] += dsk_acc

        own = slice(BLOCK, BLOCK + SWA_TILE)
        dkg_acc = jnp.zeros((1, HEAD_DIM), F32)
        for kv in range(N_KV_HEADS):
            kh, rk = _rms(cur_ref[:, ATTN_WIDTH + kv * HEAD_DIM:ATTN_WIDTH + (kv + 1) * HEAD_DIM])
            dkn = acc_ref[own, kcol(kv)]
            dkg_acc = dkg_acc + jnp.sum(dkn * kh, axis=0, keepdims=True)
            dk = _rms_bwd(dkn, kh, rk, kg_v)
            du_ref[:, ATTN_WIDTH + kv * HEAD_DIM:ATTN_WIDTH + (kv + 1) * HEAD_DIM] = dk.astype(du_ref.dtype)
            vc = ATTN_WIDTH + KV_WIDTH + kv * HEAD_DIM
            du_ref[:, vc:vc + HEAD_DIM] = acc_ref[own, vcol(kv)].astype(du_ref.dtype)
        dkg_ref[...] += dkg_acc
        carry_ref[...] = acc_ref[0:BLOCK, :]

    vec = pl.BlockSpec((1, HEAD_DIM), lambda s: (0, 0))
    return pl.pallas_call(
        body, name="swa_bwd", grid=(nt,),
        in_specs=[
            pl.BlockSpec(memory_space=pltpu.SMEM),
            pl.BlockSpec((SWA_TILE, QKV_WIDTH), lambda s: (nt - 1 - s, 0)),
            pl.BlockSpec((BLOCK, QKV_WIDTH), lambda s: (jnp.maximum((nt - 1 - s) * SWA_SUB - 1, 0), 0)),
            pl.BlockSpec((SWA_TILE, ATTN_WIDTH), lambda s: (nt - 1 - s, 0)),
            vec, vec,
        ],
        out_specs=[
            pl.BlockSpec((SWA_TILE, QKV_WIDTH), lambda s: (nt - 1 - s, 0)),
            vec, vec, pl.BlockSpec((1, LANES), lambda s: (0, 0)),
        ],
        out_shape=[
            jax.ShapeDtypeStruct((t, IN_COLS), BF16),
            jax.ShapeDtypeStruct((1, HEAD_DIM), F32), jax.ShapeDtypeStruct((1, HEAD_DIM), F32),
            jax.ShapeDtypeStruct((1, LANES), F32),
        ],
        scratch_shapes=[pltpu.VMEM((SWA_TILE + BLOCK, 2 * KV_WIDTH), F32), pltpu.VMEM((BLOCK, 2 * KV_WIDTH), F32)],
        compiler_params=_cp(),
    )(sinks, u, u, dmixed, qg.reshape(1, HEAD_DIM), kg.reshape(1, HEAD_DIM))


CONV_TILE = 512
CONV_CHUNK = 64
VAL0 = QKV_WIDTH
GATE0 = QKV_WIDTH + CONV_CH


def _glu(ref):
    return ref[:, VAL0:GATE0] * _sigmoid(ref[:, GATE0:GATE0 + CONV_CH])


SUBLANES = 8
CONV_BUF = CONV_HALO + CONV_TILE + SUBLANES
CONV_EXT = CONV_HALO + CONV_TILE


def _fill_shifted(sh_ref):
    for r in range(1, SUBLANES):
        sh_ref[r, 0:CONV_EXT, :] = sh_ref[0, pl.ds(r, CONV_EXT), :]


def _shifted(sh_ref, start, offset, n):
    return sh_ref[offset % SUBLANES, pl.ds(start + offset - offset % SUBLANES, n), :]


def _layernorm_stats(y):
    mu = jnp.mean(y, axis=-1, keepdims=True)
    yc = y - mu
    rstd = lax.rsqrt(jnp.mean(yc * yc, axis=-1, keepdims=True) + EPS)
    return yc * rstd, rstd


def conv_fwd(u, mixed, conv_w, conv_b, ln_g, ln_b):
    t = u.shape[0]
    nt = t // CONV_TILE
    per = CONV_TILE // CONV_HALO

    def body(cur_ref, prev_ref, mixed_ref, w_ref, b_ref, g_ref, b2_ref, o_ref, y_ref, gl_ref):
        del mixed_ref
        i = pl.program_id(0)
        gl_ref[0, 0:CONV_HALO, :] = jnp.where(i > 0, _glu(prev_ref), 0.0)
        gl_ref[0, CONV_HALO:CONV_EXT, :] = _glu(cur_ref)
        gl_ref[0, CONV_EXT:CONV_BUF, :] = jnp.zeros((SUBLANES, CONV_CH), F32)
        _fill_shifted(gl_ref)
        for c0 in range(0, CONV_TILE, CONV_CHUNK):
            acc = jnp.broadcast_to(b_ref[...], (CONV_CHUNK, CONV_CH))
            for k in range(CONV_K):
                acc = acc + w_ref[k:k + 1, :] * _shifted(gl_ref, c0, 2 + k, CONV_CHUNK)
            y_ref[c0:c0 + CONV_CHUNK, :] = acc
        yh, _ = _layernorm_stats(y_ref[...])
        yln = yh * g_ref[...] + b2_ref[...]
        o_ref[...] = (yln * _sigmoid(yln)).astype(o_ref.dtype)

    vec = pl.BlockSpec((1, CONV_CH), lambda i: (0, 0))
    return pl.pallas_call(
        body, name="conv_fwd", grid=(nt,),
        in_specs=[
            pl.BlockSpec((CONV_TILE, IN_COLS), lambda i: (i, 0)),
            pl.BlockSpec((CONV_HALO, IN_COLS), lambda i: (jnp.maximum(i * per - 1, 0), 0)),
            pl.BlockSpec(memory_space=pl.ANY),
            pl.BlockSpec((CONV_HALO, CONV_CH), lambda i: (0, 0)),
            vec, vec, vec,
        ],
        out_specs=[pl.BlockSpec((CONV_TILE, CONV_CH), lambda i: (i, 1)), pl.BlockSpec((CONV_TILE, CONV_CH), lambda i: (i, 0))],
        out_shape=[jax.ShapeDtypeStruct(mixed.shape, mixed.dtype), jax.ShapeDtypeStruct((t, CONV_CH), F32)],
        scratch_shapes=[pltpu.VMEM((SUBLANES, CONV_BUF, CONV_CH), F32)],
        input_output_aliases={2: 0}, compiler_params=_cp(),
    )(u, u, mixed, conv_w, conv_b.reshape(1, CONV_CH), ln_g.reshape(1, CONV_CH), ln_b.reshape(1, CONV_CH))


def conv_bwd(u, y, dmixed, du, conv_w, ln_g, ln_b):
    t = u.shape[0]
    nt = t // CONV_TILE
    per = CONV_TILE // CONV_HALO

    def body(cur_ref, prev_ref, y_ref, yn_ref, do_ref, don_ref, du_in_ref, w_ref, g_ref, b2_ref,
             du_ref, dw_ref, dvec_ref, gl_ref, dy_ref):
        i = pl.program_id(0)
        last = i == nt - 1
        _zero_at_first_step(dw_ref, dvec_ref)

        gl_ref[0, 0:CONV_HALO, :] = jnp.where(i > 0, _glu(prev_ref), 0.0)
        gl_ref[0, CONV_HALO:CONV_EXT, :] = _glu(cur_ref)
        gl_ref[0, CONV_EXT:CONV_BUF, :] = jnp.zeros((SUBLANES, CONV_CH), F32)
        _fill_shifted(gl_ref)

        yh, rstd = _layernorm_stats(jnp.concatenate([y_ref[...], yn_ref[...]], axis=0))
        g = g_ref[...]
        yln = yh * g + b2_ref[...]
        sg = _sigmoid(yln)
        dout = jnp.concatenate([do_ref[...], jnp.where(last, 0.0, don_ref[...])], axis=0)
        dyln = dout * (sg * (1.0 + yln * (1.0 - sg)))
        dyh = dyln * g
        dy = rstd * (dyh - jnp.mean(dyh, axis=-1, keepdims=True) - yh * jnp.mean(dyh * yh, axis=-1, keepdims=True))
        dy_ref[0, 0:CONV_EXT, :] = dy
        dy_ref[0, CONV_EXT:CONV_BUF, :] = jnp.zeros((SUBLANES, CONV_CH), F32)
        _fill_shifted(dy_ref)

        own = slice(0, CONV_TILE)
        dvec_ref[0:1, :] += jnp.sum(dy[own], axis=0, keepdims=True)
        dvec_ref[1:2, :] += jnp.sum(dyln[own] * yh[own], axis=0, keepdims=True)
        dvec_ref[2:3, :] += jnp.sum(dyln[own], axis=0, keepdims=True)
        for k in range(CONV_K):
            dw_ref[k:k + 1, :] += jnp.sum(dy[own] * _shifted(gl_ref, 0, 2 + k, CONV_TILE), axis=0, keepdims=True)

        for c0 in range(0, CONV_TILE, CONV_CHUNK):
            acc = jnp.zeros((CONV_CHUNK, CONV_CH), F32)
            for k in range(CONV_K):
                acc = acc + w_ref[k:k + 1, :] * _shifted(dy_ref, c0, CONV_K - 1 - k, CONV_CHUNK)
            rows = slice(c0, c0 + CONV_CHUNK)
            val = cur_ref[rows, VAL0:GATE0]
            sgate = _sigmoid(cur_ref[rows, GATE0:GATE0 + CONV_CH])
            du_ref[rows, VAL0:GATE0] = (acc * sgate).astype(du_ref.dtype)
            du_ref[rows, GATE0:GATE0 + CONV_CH] = (acc * val * sgate * (1.0 - sgate)).astype(du_ref.dtype)
        du_ref[:, 0:QKV_WIDTH] = du_in_ref[:, 0:QKV_WIDTH]

    vec = pl.BlockSpec((1, CONV_CH), lambda i: (0, 0))
    n_halo = t // CONV_HALO
    return pl.pallas_call(
        body, name="conv_bwd", grid=(nt,),
        in_specs=[
            pl.BlockSpec((CONV_TILE, IN_COLS), lambda i: (i, 0)),
            pl.BlockSpec((CONV_HALO, IN_COLS), lambda i: (jnp.maximum(i * per - 1, 0), 0)),
            pl.BlockSpec((CONV_TILE, CONV_CH), lambda i: (i, 0)),
            pl.BlockSpec((CONV_HALO, CONV_CH), lambda i: (jnp.minimum((i + 1) * per, n_halo - 1), 0)),
            pl.BlockSpec((CONV_TILE, CONV_CH), lambda i: (i, 1)),
            pl.BlockSpec((CONV_HALO, CONV_CH), lambda i: (jnp.minimum((i + 1) * per, n_halo - 1), 1)),
            pl.BlockSpec((CONV_TILE, IN_COLS), lambda i: (i, 0)),
            pl.BlockSpec((CONV_HALO, CONV_CH), lambda i: (0, 0)),
            vec, vec,
        ],
        out_specs=[
            pl.BlockSpec((CONV_TILE, IN_COLS), lambda i: (i, 0)),
            pl.BlockSpec((CONV_HALO, CONV_CH), lambda i: (0, 0)),
            pl.BlockSpec((8, CONV_CH), lambda i: (0, 0)),
        ],
        out_shape=[
            jax.ShapeDtypeStruct(du.shape, du.dtype),
            jax.ShapeDtypeStruct((CONV_HALO, CONV_CH), F32),
            jax.ShapeDtypeStruct((8, CONV_CH), F32),
        ],
        scratch_shapes=[pltpu.VMEM((SUBLANES, CONV_BUF, CONV_CH), F32), pltpu.VMEM((SUBLANES, CONV_BUF, CONV_CH), F32)],
        input_output_aliases={6: 0}, compiler_params=_cp(),
    )(u, u, y, y, dmixed, dmixed, du, conv_w, ln_g.reshape(1, CONV_CH), ln_b.reshape(1, CONV_CH))


XATTN_TILE = 512


def _xattn_probs(qn, kn):
    s = _dot_nt(qn, kn) * (X_HEAD_DIM ** -0.5)
    e = jnp.exp(s - jnp.max(s, axis=-1, keepdims=True))
    return e / jnp.sum(e, axis=-1, keepdims=True)


def xattn_fwd(q, kv, qg, kg):
    t, d = q.shape
    n_mem = kv.shape[0]

    def body(q_ref, kv_ref, qg_ref, kg_ref, o_ref):
        for h in range(N_X_HEADS):
            cols = slice(h * X_HEAD_DIM, (h + 1) * X_HEAD_DIM)
            qn = _rms(q_ref[:, cols])[0] * qg_ref[...]
            kn = _rms(kv_ref[:, cols])[0] * kg_ref[...]
            p = _xattn_probs(qn, kn)
            o_ref[:, cols] = _dot_nn(p, kv_ref[:, d + h * X_HEAD_DIM:d + (h + 1) * X_HEAD_DIM]).astype(o_ref.dtype)

    vec = pl.BlockSpec((1, X_HEAD_DIM), lambda i: (0, 0))
    return pl.pallas_call(
        body, name="xattn_fwd", grid=(t // XATTN_TILE,),
        in_specs=[pl.BlockSpec((XATTN_TILE, d), lambda i: (i, 0)), pl.BlockSpec((n_mem, 2 * d), lambda i: (0, 0)), vec, vec],
        out_specs=pl.BlockSpec((XATTN_TILE, d), lambda i: (i, 0)),
        out_shape=jax.ShapeDtypeStruct((t, d), BF16), compiler_params=_cp(),
    )(q, kv, qg.reshape(1, X_HEAD_DIM), kg.reshape(1, X_HEAD_DIM))


def xattn_bwd(q, kv, do, qg, kg):
    t, d = q.shape
    n_mem = kv.shape[0]
    nt = t // XATTN_TILE

    def body(q_ref, kv_ref, do_ref, qg_ref, kg_ref, dq_ref, dkv_ref, dqg_ref, dkg_ref):
        i = pl.program_id(0)

        @pl.when(i == 0)
        def _():
            dkv_ref[...] = jnp.zeros_like(dkv_ref)
            dqg_ref[...] = jnp.zeros_like(dqg_ref)
            dkg_ref[...] = jnp.zeros_like(dkg_ref)

        qg_v = qg_ref[...]
        kg_v = kg_ref[...]
        dqg_acc = jnp.zeros((1, X_HEAD_DIM), F32)
        for h in range(N_X_HEADS):
            cols = slice(h * X_HEAD_DIM, (h + 1) * X_HEAD_DIM)
            vcols = slice(d + h * X_HEAD_DIM, d + (h + 1) * X_HEAD_DIM)
            qh, rq = _rms(q_ref[:, cols])
            qn = qh * qg_v
            kn = _rms(kv_ref[:, cols])[0] * kg_v
            v = kv_ref[:, vcols]
            do_h = do_ref[:, cols]
            p = _xattn_probs(qn, kn)
            dp = _dot_nt(do_h, v)
            ds = p * (dp - jnp.sum(p * dp, axis=-1, keepdims=True))
            dkv_ref[:, vcols] += _dot_tn(p, do_h)
            dqn = _dot_nn(ds, kn) * (X_HEAD_DIM ** -0.5)
            dkv_ref[:, cols] += _dot_tn(ds, qn) * (X_HEAD_DIM ** -0.5)
            dqg_acc = dqg_acc + jnp.sum(dqn * qh, axis=0, keepdims=True)
            dq_ref[:, cols] = _rms_bwd(dqn, qh, rq, qg_v).astype(dq_ref.dtype)
        dqg_ref[...] += dqg_acc

        @pl.when(i == nt - 1)
        def _():
            dkg_acc = jnp.zeros((1, X_HEAD_DIM), F32)
            for h in range(N_X_HEADS):
                cols = slice(h * X_HEAD_DIM, (h + 1) * X_HEAD_DIM)
                kh, rk = _rms(kv_ref[:, cols])
                dkn = dkv_ref[:, cols]
                dkg_acc = dkg_acc + jnp.sum(dkn * kh, axis=0, keepdims=True)
                dkv_ref[:, cols] = _rms_bwd(dkn, kh, rk, kg_v)
            dkg_ref[...] = dkg_acc

    vec = pl.BlockSpec((1, X_HEAD_DIM), lambda i: (0, 0))
    row = pl.BlockSpec((XATTN_TILE, d), lambda i: (i, 0))
    full = pl.BlockSpec((n_mem, 2 * d), lambda i: (0, 0))
    return pl.pallas_call(
        body, name="xattn_bwd", grid=(nt,),
        in_specs=[row, full, row, vec, vec],
        out_specs=[row, full, vec, vec],
        out_shape=[
            jax.ShapeDtypeStruct((t, d), BF16), jax.ShapeDtypeStruct((n_mem, 2 * d), F32),
            jax.ShapeDtypeStruct((1, X_HEAD_DIM), F32), jax.ShapeDtypeStruct((1, X_HEAD_DIM), F32),
        ],
        compiler_params=_cp(),
    )(q, kv, do, qg.reshape(1, X_HEAD_DIM), kg.reshape(1, X_HEAD_DIM))


def adamw(w, g, m, v, *, name):
    r, c = w.shape
    tr = r
    for cand in (512, 256, 128, 64, 32, 16, 8):
        if r % cand == 0 and r > cand:
            tr = cand
            break

    def body(w_ref, g_ref, m_ref, v_ref, d_ref, nm_ref, nv_ref):
        g_v = g_ref[...]
        m2 = ADAM_B1 * m_ref[...] + (1.0 - ADAM_B1) * g_v
        v2 = ADAM_B2 * v_ref[...] + (1.0 - ADAM_B2) * jnp.square(g_v)
        m_hat = m2 / (1.0 - ADAM_B1 ** ADAM_STEP)
        v_hat = v2 / (1.0 - ADAM_B2 ** ADAM_STEP)
        d_ref[...] = -ADAM_LR * (m_hat / (jnp.sqrt(v_hat) + ADAM_EPS) + ADAM_WD * w_ref[...])
        nm_ref[...] = m2
        nv_ref[...] = v2

    spec = pl.BlockSpec((tr, c), lambda i: (i, 0))
    shape = jax.ShapeDtypeStruct((r, c), F32)
    return pl.pallas_call(
        body, name=name, grid=(r // tr,), in_specs=[spec] * 4, out_specs=[spec] * 3,
        out_shape=[shape] * 3, compiler_params=_cp(),
    )(w, g, m, v)


def _position():
    return lax.axis_index("x"), lax.axis_index("y"), lax.axis_index("c")


def all_gather(shard, *, name, in_vmem):
    r, c_ = shard.shape

    def body(x_ref, out_ref, send_sems, recv_sems, local_sem):
        x, y, c = _position()
        me, sibling = (x, y, c), (x, y, 1 - c)
        chips = [(1 - x, y), (x, 1 - y), (1 - x, 1 - y)]

        def rows(px, py, pc):
            return out_ref.at[4 * px + 2 * py + pc]

        def copy(k, block, to, src=None):
            return pltpu.make_async_remote_copy(
                src_ref=rows(*block) if src is None else src, dst_ref=rows(*block),
                send_sem=send_sems.at[k], recv_sem=recv_sems.at[k], device_id=to, device_id_type=MESH)

        mine = pltpu.make_async_copy(x_ref, rows(*me), local_sem)
        mine.start()
        first = [copy(0, me, sibling, src=x_ref)]
        first += [copy(1 + j, me, (*chip, c), src=x_ref) for j, chip in enumerate(chips)]
        for cp in first:
            cp.start()
        passed = [copy(4 + j, (*chip, c), sibling) for j, chip in enumerate(chips)]
        for j, chip in enumerate(chips):
            copy(1 + j, (*chip, c), me).wait_recv()
            passed[j].start()
        copy(0, sibling, me).wait_recv()
        for j, chip in enumerate(chips):
            copy(4 + j, (*chip, 1 - c), me).wait_recv()
        for cp in first + passed:
            cp.wait_send()
        mine.wait()

    space = pltpu.VMEM if in_vmem else pltpu.HBM
    return pl.pallas_call(
        body, name=name,
        out_shape=jax.ShapeDtypeStruct((N_DEV, r, c_), shard.dtype),
        in_specs=[pl.BlockSpec(memory_space=space)], out_specs=pl.BlockSpec(memory_space=space),
        scratch_shapes=[pltpu.SemaphoreType.DMA((7,)), pltpu.SemaphoreType.DMA((7,)), pltpu.SemaphoreType.DMA],
        compiler_params=_cp(),
    )(shard)


def exchange_sibling(parts):
    _, r, c_ = parts.shape

    def body(p_ref, out_ref, send_sems, recv_sems):
        x, y, c = _position()

        def copy(k):
            return pltpu.make_async_remote_copy(
                src_ref=p_ref.at[2 * k + (1 - c)], dst_ref=out_ref.at[k],
                send_sem=send_sems.at[k], recv_sem=recv_sems.at[k], device_id=(x, y, 1 - c), device_id_type=MESH)

        for k in range(4):
            copy(k).start()
        for k in range(4):
            copy(k).wait_recv()
        for k in range(4):
            copy(k).wait_send()

    hbm = pl.BlockSpec(memory_space=pltpu.HBM)
    return pl.pallas_call(
        body, name="rs_exchange_sibling",
        out_shape=jax.ShapeDtypeStruct((4, r, c_), parts.dtype),
        in_specs=[hbm], out_specs=hbm,
        scratch_shapes=[pltpu.SemaphoreType.DMA((4,)), pltpu.SemaphoreType.DMA((4,))],
        compiler_params=_cp(),
    )(parts)


def sum_for_chips(parts, from_sibling, c_idx, *, tr=480):
    _, r, c_ = parts.shape
    assert r % tr == 0

    def body(c_ref, p_ref, s_ref, o_ref):
        del c_ref
        o_ref[...] = (p_ref[...] + s_ref[...]).astype(o_ref.dtype)

    return pl.pallas_call(
        body, name="rs_sum_for_chips",
        grid_spec=pltpu.PrefetchScalarGridSpec(
            num_scalar_prefetch=1, grid=(4, r // tr),
            in_specs=[pl.BlockSpec((None, tr, c_), lambda k, i, c_ref: (2 * k + c_ref[0], i, 0)),
                      pl.BlockSpec((None, tr, c_), lambda k, i, c_ref: (k, i, 0))],
            out_specs=pl.BlockSpec((None, tr, c_), lambda k, i, c_ref: (k, i, 0))),
        out_shape=jax.ShapeDtypeStruct((4, r, c_), BF16), compiler_params=_cp(),
    )(c_idx, parts, from_sibling)


def exchange_chips(sums):
    _, r, c_ = sums.shape

    def body(s_ref, out_ref, send_sems, recv_sems):
        x, y, c = _position()
        chips = [(1 - x, y), (x, 1 - y), (1 - x, 1 - y)]

        def copy(j):
            px, py = chips[j]
            return pltpu.make_async_remote_copy(
                src_ref=s_ref.at[2 * px + py], dst_ref=out_ref.at[j],
                send_sem=send_sems.at[j], recv_sem=recv_sems.at[j], device_id=(px, py, c), device_id_type=MESH)

        for j in range(3):
            copy(j).start()
        for j in range(3):
            copy(j).wait_recv()
        for j in range(3):
            copy(j).wait_send()

    hbm = pl.BlockSpec(memory_space=pltpu.HBM)
    return pl.pallas_call(
        body, name="rs_exchange_chips",
        out_shape=jax.ShapeDtypeStruct((3, r, c_), sums.dtype),
        in_specs=[hbm], out_specs=hbm,
        scratch_shapes=[pltpu.SemaphoreType.DMA((3,)), pltpu.SemaphoreType.DMA((3,))],
        compiler_params=_cp(),
    )(sums)


def sum_final(parts, from_sibling, from_chips, kc_idx, *, tr=480):
    _, r, c_ = parts.shape

    def body(kc_ref, p_ref, s_ref, a_ref, b_ref, d_ref, o_ref):
        del kc_ref
        o_ref[...] = (((p_ref[...] + s_ref[...]) + a_ref[...].astype(F32)) + b_ref[...].astype(F32)) + d_ref[...].astype(F32)

    def chip_spec(j):
        return pl.BlockSpec((None, tr, c_), lambda i, kc: (j, i, 0))

    return pl.pallas_call(
        body, name="rs_sum_final",
        grid_spec=pltpu.PrefetchScalarGridSpec(
            num_scalar_prefetch=1, grid=(r // tr,),
            in_specs=[pl.BlockSpec((None, tr, c_), lambda i, kc: (2 * kc[0] + kc[1], i, 0)),
                      pl.BlockSpec((None, tr, c_), lambda i, kc: (kc[0], i, 0)),
                      chip_spec(0), chip_spec(1), chip_spec(2)],
            out_specs=pl.BlockSpec((tr, c_), lambda i, kc: (i, 0))),
        out_shape=jax.ShapeDtypeStruct((r, c_), F32), compiler_params=_cp(),
    )(kc_idx, parts, from_sibling, from_chips, from_chips, from_chips)


def sum_devices(gathered):
    n, r, c_ = gathered.shape

    def body(g_ref, o_ref):
        acc = g_ref[0]
        for k in range(1, n):
            acc = acc + g_ref[k]
        o_ref[...] = acc

    return pl.pallas_call(
        body, name="sum_devices", out_shape=jax.ShapeDtypeStruct((r, c_), F32), compiler_params=_cp(),
    )(gathered)


BIG = (
    ("w_in", IN_COLS, True), ("w_out", D_MODEL, False), ("wq_x", D_MODEL, False), ("wkv_x", 2 * D_MODEL, True),
    ("wo_x", D_MODEL, False), ("w_gate_up", 2 * D_FF, True), ("w_down", D_FF, False),
)
SHARD_ROWS = sum(rows // N_DEV for _, rows, _ in BIG)

SMALL = ("norm_mix_g", "q_norm_g", "k_norm_g", "sinks", "conv_b", "conv_ln_g", "conv_ln_b",
         "norm_x_g", "norm_mem_g", "xq_norm_g", "xk_norm_g", "norm_ffn_g")


def _pack_rows(vectors, width=LANES, row_multiple=8):
    flat = jnp.concatenate([v.reshape(-1) for v in vectors])
    per = width * row_multiple
    padded = -(-flat.shape[0] // per) * per
    return jnp.pad(flat, (0, padded - flat.shape[0])).reshape(-1, width)


def _unpack_rows(packed, shapes):
    flat = packed.reshape(-1)
    out, at = [], 0
    for s in shapes:
        n = 1
        for dim in s:
            n *= dim
        out.append(flat[at:at + n].reshape(s))
        at += n
    return out


def _layer_fwd(x0, mem, w, s):
    h0, u = norm_proj(x0, s["norm_mix_g"], w["w_in"])
    mixed = swa_fwd(u, s["q_norm_g"], s["k_norm_g"], s["sinks"])
    mixed, conv_y = conv_fwd(u, mixed, s["conv_w"], s["conv_b"], s["conv_ln_g"], s["conv_ln_b"])
    memn = rms_fwd(mem, s["norm_mem_g"])
    kv = mm(memn, w["wkv_x"], trans_b=True, out_dtype=F32, name="mm_kv")
    x1, h1, qx, o, x2, h2 = mid_fwd(mixed, x0, w["w_out"], s["norm_x_g"], w["wq_x"], kv, s["xq_norm_g"], s["xk_norm_g"],
                                    w["wo_x"], s["norm_ffn_g"])
    gu, a, x3 = ffn_fwd(h2, x2, w["w_gate_up"], w["w_down"])
    saved = dict(x0=x0, h0=h0, u=u, conv_y=conv_y, mixed=mixed, x1=x1, h1=h1, qx=qx, memn=memn, kv=kv, o=o, x2=x2, h2=h2,
                 gu=gu, a=a)
    return x3, saved


def _layer_bwd(dx3, mem, w, s, sv):
    gb, gs = {}, {}
    dgu, dx2, dg = ffn_bwd(dx3, sv["gu"], sv["x2"], s["norm_ffn_g"], w["w_down"], w["w_gate_up"])
    gs["norm_ffn_g"] = dg
    gb["w_down"] = mm_tn(sv["a"], dx3, name="mm_dw_down")
    gb["w_gate_up"] = mm_tn(dgu, sv["h2"], name="mm_dw_gate_up")

    dq, dx1, dmixed, dkv, dqg, dkg, dg = mid_bwd(dx2, sv["qx"], sv["kv"], s["xq_norm_g"], s["xk_norm_g"], sv["x1"],
                                                 s["norm_x_g"], w["wo_x"], w["wq_x"], w["w_out"])
    gs["xq_norm_g"], gs["xk_norm_g"], gs["norm_x_g"] = dqg, dkg, dg
    gb["wo_x"] = mm_tn(sv["o"], dx2, name="mm_dwo")
    gb["wq_x"] = mm_tn(sv["h1"], dq, name="mm_dwq")
    dmemn = mm(dkv, w["wkv_x"], trans_b=False, out_dtype=F32, name="mm_dmemn")
    gb["wkv_x"] = mm_tn(dkv, sv["memn"], name="mm_dwkv")
    _, dg = rms_bwd(dmemn, mem, s["norm_mem_g"], None)
    gs["norm_mem_g"] = dg
    gb["w_out"] = mm_tn(sv["mixed"], dx1, name="mm_dw_out")
    du, dqg, dkg, dsinks = swa_bwd(sv["u"], dmixed, s["q_norm_g"], s["k_norm_g"], s["sinks"])
    gs["q_norm_g"], gs["k_norm_g"], gs["sinks"] = dqg, dkg, dsinks[0, :N_Q_HEADS]
    du, dconv_w, dvec = conv_bwd(sv["u"], sv["conv_y"], dmixed, du, s["conv_w"], s["conv_ln_g"], s["conv_ln_b"])
    gs["conv_w"] = dconv_w[:CONV_K]
    gs["conv_b"], gs["conv_ln_g"], gs["conv_ln_b"] = dvec[0], dvec[1], dvec[2]
    gb["w_in"] = mm_tn(du, sv["h0"], name="mm_dw_in")
    dx0, dg = in_bwd(du, w["w_in"], sv["x0"], s["norm_mix_g"], dx1)
    gs["norm_mix_g"] = dg
    return dx0, gb, gs


def _local_step(x, mem, target, weights, smalls):
    saved = []
    h = x
    for l in range(DEPTH):
        h, sv = _layer_fwd(h, mem, weights[l], smalls[l])
        saved.append(sv)
    dx, loss_part = loss_head(h, target)
    gbs, gss = [None] * DEPTH, [None] * DEPTH
    for l in reversed(range(DEPTH)):
        dx, gbs[l], gss[l] = _layer_bwd(dx, mem, weights[l], smalls[l], saved[l])
    return loss_part[0, 0], dx, gbs, gss


def kernel(x, mem, norm_mix_g, w_in, q_norm_g, k_norm_g, sinks, conv_w, conv_b, conv_ln_g, conv_ln_b, w_out, norm_x_g, norm_mem_g, wq_x, wkv_x, xq_norm_g, xk_norm_g, wo_x, norm_ffn_g, w_gate_up, w_down, loss_target, m_norm_mix_g, m_w_in, m_q_norm_g, m_k_norm_g, m_sinks, m_conv_w, m_conv_b, m_conv_ln_g, m_conv_ln_b, m_w_out, m_norm_x_g, m_norm_mem_g, m_wq_x, m_wkv_x, m_xq_norm_g, m_xk_norm_g, m_wo_x, m_norm_ffn_g, m_w_gate_up, m_w_down, v_norm_mix_g, v_w_in, v_q_norm_g, v_k_norm_g, v_sinks, v_conv_w, v_conv_b, v_conv_ln_g, v_conv_ln_b, v_w_out, v_norm_x_g, v_norm_mem_g, v_wq_x, v_wkv_x, v_xq_norm_g, v_xk_norm_g, v_wo_x, v_norm_ffn_g, v_w_gate_up, v_w_down):
    P = dict(norm_mix_g=norm_mix_g, w_in=w_in, q_norm_g=q_norm_g, k_norm_g=k_norm_g, sinks=sinks, conv_w=conv_w, conv_b=conv_b,
             conv_ln_g=conv_ln_g, conv_ln_b=conv_ln_b, w_out=w_out, norm_x_g=norm_x_g, norm_mem_g=norm_mem_g, wq_x=wq_x,
             wkv_x=wkv_x, xq_norm_g=xq_norm_g, xk_norm_g=xk_norm_g, wo_x=wo_x, norm_ffn_g=norm_ffn_g, w_gate_up=w_gate_up,
             w_down=w_down)
    M = dict(norm_mix_g=m_norm_mix_g, w_in=m_w_in, q_norm_g=m_q_norm_g, k_norm_g=m_k_norm_g, sinks=m_sinks, conv_w=m_conv_w,
             conv_b=m_conv_b, conv_ln_g=m_conv_ln_g, conv_ln_b=m_conv_ln_b, w_out=m_w_out, norm_x_g=m_norm_x_g,
             norm_mem_g=m_norm_mem_g, wq_x=m_wq_x, wkv_x=m_wkv_x, xq_norm_g=m_xq_norm_g, xk_norm_g=m_xk_norm_g, wo_x=m_wo_x,
             norm_ffn_g=m_norm_ffn_g, w_gate_up=m_w_gate_up, w_down=m_w_down)
    V = dict(norm_mix_g=v_norm_mix_g, w_in=v_w_in, q_norm_g=v_q_norm_g, k_norm_g=v_k_norm_g, sinks=v_sinks, conv_w=v_conv_w,
             conv_b=v_conv_b, conv_ln_g=v_conv_ln_g, conv_ln_b=v_conv_ln_b, w_out=v_w_out, norm_x_g=v_norm_x_g,
             norm_mem_g=v_norm_mem_g, wq_x=v_wq_x, wkv_x=v_wkv_x, xq_norm_g=v_xq_norm_g, xk_norm_g=v_xk_norm_g, wo_x=v_wo_x,
             norm_ffn_g=v_norm_ffn_g, w_gate_up=v_w_gate_up, w_down=v_w_down)
    order = ["norm_mix_g", "w_in", "q_norm_g", "k_norm_g", "sinks", "conv_w", "conv_b", "conv_ln_g", "conv_ln_b", "w_out",
             "norm_x_g", "norm_mem_g", "wq_x", "wkv_x", "xq_norm_g", "xk_norm_g", "wo_x", "norm_ffn_g", "w_gate_up", "w_down"]
    xi, yi, ci = _position()
    dev = 4 * xi + 2 * yi + ci
    x2d, mem2d, tgt2d = x[0], mem[0], loss_target[0]

    def travelling(name, l, transposed):
        a = P[name][l]
        return (a.T if transposed else a).astype(BF16)

    packed = jnp.concatenate([travelling(n, l, tr) for l in range(DEPTH) for n, _, tr in BIG], axis=0)
    gathered = all_gather(packed, name="ag_weights", in_vmem=False)
    weights = []
    for l in range(DEPTH):
        at, wl = l * SHARD_ROWS, {}
        for n, rows, _ in BIG:
            wl[n] = gathered[:, at:at + rows // N_DEV, :].reshape(rows, D_MODEL)
            at += rows // N_DEV
        weights.append(wl)
    cw = jnp.pad(conv_w.reshape(DEPTH * CONV_K, CONV_CH // N_DEV), ((0, 2), (0, LANES - CONV_CH // N_DEV)))
    cw_all = all_gather(cw, name="ag_conv_w", in_vmem=True)
    cw_full = cw_all[:, :DEPTH * CONV_K, :CONV_CH // N_DEV].reshape(N_DEV, DEPTH, CONV_K, CONV_CH // N_DEV)
    cw_full = jnp.transpose(cw_full, (1, 2, 0, 3)).reshape(DEPTH, CONV_K, CONV_CH)
    smalls = []
    for l in range(DEPTH):
        sl = {n: P[n][l] for n in SMALL}
        sl["conv_w"] = jnp.pad(cw_full[l], ((0, CONV_HALO - CONV_K), (0, 0)))
        smalls.append(sl)

    loss_part, grad_x, gbs, gss = _local_step(x2d, mem2d, tgt2d, weights, smalls)
    loss = lax.psum(loss_part, ("x", "y", "c"))

    parts = jnp.concatenate(
        [gbs[l][n].reshape(N_DEV, rows // N_DEV, D_MODEL) for l in range(DEPTH) for n, rows, _ in BIG], axis=1)
    c_idx = jnp.reshape(ci, (1,)).astype(jnp.int32)
    kc_idx = jnp.stack([2 * xi + yi, ci]).astype(jnp.int32)
    from_sibling = exchange_sibling(parts)
    chip_sums = sum_for_chips(parts, from_sibling, c_idx)
    from_chips = exchange_chips(chip_sums)
    reduced = sum_final(parts, from_sibling, from_chips, kc_idx)
    grads, at = {}, 0
    for n, rows, transposed in BIG:
        per_layer = [reduced[l * SHARD_ROWS + at:l * SHARD_ROWS + at + rows // N_DEV] for l in range(DEPTH)]
        grads[n] = jnp.stack([g.T if transposed else g for g in per_layer])
        at += rows // N_DEV

    small_names = SMALL + ("conv_w",)
    small_shapes = [(DEPTH,) + ((CONV_K, CONV_CH) if n == "conv_w" else P[n].shape[1:]) for n in small_names]
    small_parts = _pack_rows([jnp.stack([gss[l][n].reshape(sh[1:]) for l in range(DEPTH)])
                              for n, sh in zip(small_names, small_shapes)])
    small_sum = sum_devices(all_gather(small_parts, name="ag_small_grads", in_vmem=True))
    for n, g in zip(small_names, _unpack_rows(small_sum, small_shapes)):
        if n == "conv_w":
            g = lax.dynamic_slice_in_dim(g, dev * (CONV_CH // N_DEV), CONV_CH // N_DEV, axis=2)
        grads[n] = g

    delta, new_m, new_v = {}, {}, {}
    for n, _, _ in BIG:
        shape = P[n].shape
        two_d = lambda a: a.reshape(shape[0] * shape[1], shape[2])
        d_, m_, v_ = adamw(two_d(P[n]), two_d(grads[n]), two_d(M[n]), two_d(V[n]), name="adamw_" + n)
        delta[n], new_m[n], new_v[n] = d_.reshape(shape), m_.reshape(shape), v_.reshape(shape)
    shapes = [P[n].shape for n in small_names]
    d_, m_, v_ = adamw(_pack_rows([P[n] for n in small_names]), _pack_rows([grads[n] for n in small_names]),
                       _pack_rows([M[n] for n in small_names]), _pack_rows([V[n] for n in small_names]), name="adamw_small")
    for n, dd, mm_, vv in zip(small_names, _unpack_rows(d_, shapes), _unpack_rows(m_, shapes), _unpack_rows(v_, shapes)):
        delta[n], new_m[n], new_v[n] = dd, mm_, vv

    return (loss, grad_x[None], *[grads[n] for n in order], *[delta[n] for n in order],
            *[new_m[n] for n in order], *[new_v[n] for n in order])
```

```python
import functools

import jax
import jax.numpy as jnp
from jax import lax
from jax.experimental import pallas as pl
from jax.experimental.pallas import tpu as pltpu

F32 = jnp.float32
BF16 = jnp.bfloat16

D_MODEL = 1024
HEAD_DIM = 64
N_Q_HEADS = 8
N_KV_HEADS = 2
GROUP = N_Q_HEADS // N_KV_HEADS
ATTN_WIDTH = N_Q_HEADS * HEAD_DIM
KV_WIDTH = N_KV_HEADS * HEAD_DIM
QKV_WIDTH = ATTN_WIDTH + 2 * KV_WIDTH
CONV_CH = 512
IN_COLS = QKV_WIDTH + 2 * CONV_CH
CONV_K = 31
CONV_HALO = 32
BLOCK = 128
N_X_HEADS = 4
X_HEAD_DIM = 256
D_FF = 2816
EPS = 1e-6
NEG = -1e30
DEPTH = 2
N_DEV = 8

ADAM_LR = 0.001
ADAM_B1 = 0.9
ADAM_B2 = 0.999
ADAM_EPS = 1e-08
ADAM_WD = 0.01
ADAM_STEP = 10

V7X_VMEM_LIMIT = 56 * 1024 * 1024
LANES = 128

MESH = pl.DeviceIdType.MESH


def _cp(**kw):
    return pltpu.CompilerParams(vmem_limit_bytes=V7X_VMEM_LIMIT, **kw)


def _dot(a, b, dims):
    return lax.dot_general(a.astype(BF16), b.astype(BF16), (dims, ((), ())), preferred_element_type=F32)


def _dot_nn(a, b):
    return _dot(a, b, ((1,), (0,)))


def _dot_nt(a, b):
    return _dot(a, b, ((1,), (1,)))


def _dot_tn(a, b):
    return _dot(a, b, ((0,), (0,)))


def _sigmoid(x):
    return jax.nn.sigmoid(x)


def _rms(x):
    r = lax.rsqrt(jnp.mean(x * x, axis=-1, keepdims=True) + EPS)
    return x * r, r


def _rms_bwd(dy, xhat, r, g):
    dxh = dy * g
    return r * (dxh - xhat * jnp.mean(dxh * xhat, axis=-1, keepdims=True))


def rms_fwd(x, g, *, tm=512):
    m, d = x.shape
    tm = min(tm, m)

    def body(x_ref, g_ref, o_ref):
        xh, _ = _rms(x_ref[...])
        o_ref[...] = (xh * g_ref[...]).astype(o_ref.dtype)

    return pl.pallas_call(
        body, name="rms_fwd", grid=(m // tm,),
        in_specs=[pl.BlockSpec((tm, d), lambda i: (i, 0)), pl.BlockSpec((1, d), lambda i: (0, 0))],
        out_specs=pl.BlockSpec((tm, d), lambda i: (i, 0)),
        out_shape=jax.ShapeDtypeStruct((m, d), BF16), compiler_params=_cp(),
    )(x, g.reshape(1, d))


def rms_bwd(dh, x, g, dres, *, tm=512):
    m, d = x.shape
    tm = min(tm, m)
    has_res = dres is not None

    def body(*refs):
        if has_res:
            dh_ref, x_ref, g_ref, r_ref, dx_ref, dg_ref = refs
        else:
            dh_ref, x_ref, g_ref, dx_ref, dg_ref = refs
        xh, r = _rms(x_ref[...])
        dy = dh_ref[...].astype(F32)

        @pl.when(pl.program_id(0) == 0)
        def _():
            dg_ref[...] = jnp.zeros_like(dg_ref)

        dg_ref[...] += jnp.sum(dy * xh, axis=0, keepdims=True)
        dx = _rms_bwd(dy, xh, r, g_ref[...])
        if has_res:
            dx = dx + r_ref[...]
        dx_ref[...] = dx

    row = pl.BlockSpec((tm, d), lambda i: (i, 0))
    vec = pl.BlockSpec((1, d), lambda i: (0, 0))
    ins = [dh, x, g.reshape(1, d)] + ([dres] if has_res else [])
    return pl.pallas_call(
        body, name="rms_bwd" + ("_res" if has_res else ""), grid=(m // tm,),
        in_specs=[row, row, vec] + ([row] if has_res else []),
        out_specs=[row, vec],
        out_shape=[jax.ShapeDtypeStruct((m, d), F32), jax.ShapeDtypeStruct((1, d), F32)],
        compiler_params=_cp(),
    )(*ins)


def swiglu_fwd(gu, *, tm=512):
    m, f2 = gu.shape
    f = f2 // 2

    def body(g_ref, u_ref, o_ref):
        g = g_ref[...]
        o_ref[...] = (g * _sigmoid(g) * u_ref[...]).astype(o_ref.dtype)

    return pl.pallas_call(
        body, name="swiglu_fwd", grid=(m // tm,),
        in_specs=[pl.BlockSpec((tm, f), lambda i: (i, 0)), pl.BlockSpec((tm, f), lambda i: (i, 1))],
        out_specs=pl.BlockSpec((tm, f), lambda i: (i, 0)),
        out_shape=jax.ShapeDtypeStruct((m, f), BF16), compiler_params=_cp(),
    )(gu, gu)


def swiglu_bwd(gu, da, *, tm=256):
    m, f2 = gu.shape
    f = f2 // 2

    def body(gu_ref, da_ref, o_ref):
        g = gu_ref[:, :f]
        u = gu_ref[:, f:]
        da_v = da_ref[...]
        sg = _sigmoid(g)
        o_ref[:, :f] = (da_v * u * (sg * (1.0 + g * (1.0 - sg)))).astype(o_ref.dtype)
        o_ref[:, f:] = (da_v * (g * sg)).astype(o_ref.dtype)

    return pl.pallas_call(
        body, name="swiglu_bwd", grid=(m // tm,),
        in_specs=[pl.BlockSpec((tm, f2), lambda i: (i, 0)), pl.BlockSpec((tm, f), lambda i: (i, 0))],
        out_specs=pl.BlockSpec((tm, f2), lambda i: (i, 0)),
        out_shape=jax.ShapeDtypeStruct((m, f2), BF16), compiler_params=_cp(),
    )(gu, da)


def loss_head(y, target, *, tm=512):
    m, d = y.shape

    def body(y_ref, t_ref, dy_ref, l_ref):
        err = y_ref[...] - t_ref[...]
        dy_ref[...] = err * (1.0 / d)

        @pl.when(pl.program_id(0) == 0)
        def _():
            l_ref[...] = jnp.zeros_like(l_ref)

        part = jnp.sum(jnp.sum(err * err, axis=-1, keepdims=True), axis=0, keepdims=True)
        l_ref[...] += jnp.broadcast_to(part * (0.5 / d), l_ref.shape)

    row = pl.BlockSpec((tm, d), lambda i: (i, 0))
    return pl.pallas_call(
        body, name="loss_head", grid=(m // tm,),
        in_specs=[row, row],
        out_specs=[row, pl.BlockSpec((1, LANES), lambda i: (0, 0))],
        out_shape=[jax.ShapeDtypeStruct((m, d), F32), jax.ShapeDtypeStruct((1, LANES), F32)],
        compiler_params=_cp(),
    )(y, target)


def _tile(n, cap):
    if n <= cap:
        return n
    best = None
    for t in range(LANES, cap + 1, LANES):
        if n % t == 0:
            best = t
    assert best is not None, (n, cap)
    return best


def mm(a, b, *, trans_b, out_dtype, res=None, tm=1024, tn_cap=1536, name):
    m, k = a.shape
    n = b.shape[0] if trans_b else b.shape[1]
    assert (b.shape[1] if trans_b else b.shape[0]) == k
    tm = min(tm, m)
    tn = _tile(n, tn_cap)
    has_res = res is not None

    def body(*refs):
        if has_res:
            a_ref, b_ref, r_ref, o_ref = refs
        else:
            a_ref, b_ref, o_ref = refs
        acc = _dot_nt(a_ref[...], b_ref[...]) if trans_b else _dot_nn(a_ref[...], b_ref[...])
        if has_res:
            acc = acc + r_ref[...]
        o_ref[...] = acc.astype(o_ref.dtype)

    b_spec = pl.BlockSpec((tn, k), lambda i, j: (j, 0)) if trans_b else pl.BlockSpec((k, tn), lambda i, j: (0, j))
    o_spec = pl.BlockSpec((tm, tn), lambda i, j: (i, j))
    return pl.pallas_call(
        body, name=name, grid=(m // tm, n // tn),
        in_specs=[pl.BlockSpec((tm, k), lambda i, j: (i, 0)), b_spec] + ([o_spec] if has_res else []),
        out_specs=o_spec,
        out_shape=jax.ShapeDtypeStruct((m, n), out_dtype), compiler_params=_cp(),
    )(*([a, b] + ([res] if has_res else [])))


def mm_tn(a, b, *, name, ta_cap=1536, tb_cap=1024, tk=1024):
    m, ka = a.shape
    nb = b.shape[1]
    assert b.shape[0] == m
    tk = min(tk, m)
    ta = _tile(ka, ta_cap)
    tb = _tile(nb, tb_cap)

    def body(a_ref, b_ref, o_ref):
        @pl.when(pl.program_id(2) == 0)
        def _():
            o_ref[...] = jnp.zeros_like(o_ref)

        o_ref[...] += _dot_tn(a_ref[...], b_ref[...])

    return pl.pallas_call(
        body, name=name, grid=(ka // ta, nb // tb, m // tk),
        in_specs=[pl.BlockSpec((tk, ta), lambda i, j, kk: (kk, i)), pl.BlockSpec((tk, tb), lambda i, j, kk: (kk, j))],
        out_specs=pl.BlockSpec((ta, tb), lambda i, j, kk: (i, j)),
        out_shape=jax.ShapeDtypeStruct((ka, nb), F32), compiler_params=_cp(),
    )(a, b)


def _whole(shape):
    return pl.BlockSpec(shape, lambda i: (0,) * len(shape), pipeline_mode=pl.Buffered(1))


def _rows(tm, n):
    return pl.BlockSpec((tm, n), lambda i: (i, 0))


def _vec(n):
    return pl.BlockSpec((1, n), lambda i: (0, 0))


def _chunks(n, cap=1408):
    size = _tile(n, cap)
    return [(s, size) for s in range(0, n, size)]


def _zero_at_first_step(*refs):
    @pl.when(pl.program_id(0) == 0)
    def _():
        for r in refs:
            r[...] = jnp.zeros_like(r)


def norm_proj(x, g, wt, *, tm=512):
    m, d = x.shape
    n = wt.shape[0]

    def body(x_ref, g_ref, wt_ref, h_ref, u_ref):
        h = (_rms(x_ref[...])[0] * g_ref[...]).astype(BF16)
        h_ref[...] = h
        for s, sz in _chunks(n):
            u_ref[:, s:s + sz] = _dot_nt(h, wt_ref[s:s + sz, :])

    return pl.pallas_call(
        body, name="norm_proj", grid=(m // tm,),
        in_specs=[_rows(tm, d), _vec(d), _whole((n, d))],
        out_specs=[_rows(tm, d), _rows(tm, n)],
        out_shape=[jax.ShapeDtypeStruct((m, d), BF16), jax.ShapeDtypeStruct((m, n), F32)],
        compiler_params=_cp(),
    )(x, g.reshape(1, d), wt)


def _xattn_heads(q_ref, kv_ref, qg_v, kg_v, d):
    out = []
    for h in range(N_X_HEADS):
        cols = slice(h * X_HEAD_DIM, (h + 1) * X_HEAD_DIM)
        qh, rq = _rms(q_ref[:, cols])
        qn = qh * qg_v
        kn = _rms(kv_ref[:, cols])[0] * kg_v
        v = kv_ref[:, d + h * X_HEAD_DIM:d + (h + 1) * X_HEAD_DIM]
        out.append((qh, rq, qn, kn, v, _xattn_probs(qn, kn)))
    return out


def mid_fwd(mixed, x0, w_out, g_x, wq, kv, xqg, xkg, wo, g_f, *, tm=512):
    m, d = x0.shape
    n_mem = kv.shape[0]

    def body(mixed_ref, x0_ref, w_out_ref, g_x_ref, wq_ref, kv_ref, xqg_ref, xkg_ref, wo_ref, g_f_ref,
             x1_ref, h1_ref, qx_ref, o_ref, x2_ref, h2_ref):
        x1 = x0_ref[...] + _dot_nn(mixed_ref[...], w_out_ref[...])
        x1_ref[...] = x1
        h1 = (_rms(x1)[0] * g_x_ref[...]).astype(BF16)
        h1_ref[...] = h1
        qx_ref[...] = _dot_nn(h1, wq_ref[...])
        for h, (_, _, _, _, v, p) in enumerate(_xattn_heads(qx_ref, kv_ref, xqg_ref[...], xkg_ref[...], d)):
            o_ref[:, h * X_HEAD_DIM:(h + 1) * X_HEAD_DIM] = _dot_nn(p, v).astype(o_ref.dtype)
        x2 = x1 + _dot_nn(o_ref[...], wo_ref[...])
        x2_ref[...] = x2
        h2_ref[...] = (_rms(x2)[0] * g_f_ref[...]).astype(BF16)

    sq = _whole((d, d))
    f32_rows, bf_rows = jax.ShapeDtypeStruct((m, d), F32), jax.ShapeDtypeStruct((m, d), BF16)
    return pl.pallas_call(
        body, name="mid_fwd", grid=(m // tm,),
        in_specs=[_rows(tm, d), _rows(tm, d), sq, _vec(d), sq, _whole((n_mem, 2 * d)), _vec(X_HEAD_DIM), _vec(X_HEAD_DIM),
                  sq, _vec(d)],
        out_specs=[_rows(tm, d)] * 6,
        out_shape=[f32_rows, bf_rows, f32_rows, bf_rows, f32_rows, bf_rows],
        compiler_params=_cp(),
    )(mixed, x0, w_out, g_x.reshape(1, d), wq, kv, xqg.reshape(1, X_HEAD_DIM), xkg.reshape(1, X_HEAD_DIM), wo,
      g_f.reshape(1, d))


def ffn_fwd(h2, x2, wt_gu, w_down, *, tm=256):
    m, d = x2.shape
    f = w_down.shape[0]

    def body(h2_ref, x2_ref, wt_gu_ref, w_down_ref, gu_ref, a_ref, x3_ref):
        h = h2_ref[...]
        for s, sz in _chunks(2 * f):
            gu_ref[:, s:s + sz] = _dot_nt(h, wt_gu_ref[s:s + sz, :])
        for s, sz in _chunks(f):
            g = gu_ref[:, s:s + sz]
            a_ref[:, s:s + sz] = (g * _sigmoid(g) * gu_ref[:, f + s:f + s + sz]).astype(a_ref.dtype)
        x3_ref[...] = x2_ref[...] + _dot_nn(a_ref[...], w_down_ref[...])

    return pl.pallas_call(
        body, name="ffn_fwd", grid=(m // tm,),
        in_specs=[_rows(tm, d), _rows(tm, d), _whole((2 * f, d)), _whole((f, d))],
        out_specs=[_rows(tm, 2 * f), _rows(tm, f), _rows(tm, d)],
        out_shape=[jax.ShapeDtypeStruct((m, 2 * f), F32), jax.ShapeDtypeStruct((m, f), BF16),
                   jax.ShapeDtypeStruct((m, d), F32)],
        compiler_params=_cp(),
    )(h2, x2, wt_gu, w_down)


def ffn_bwd(dx3, gu, x2, g_f, w_down, wt_gu, *, tm=256):
    m, d = x2.shape
    f = w_down.shape[0]

    def body(dx3_ref, gu_ref, x2_ref, g_ref, w_down_ref, wt_gu_ref, dgu_ref, dx2_ref, dg_ref):
        _zero_at_first_step(dg_ref)
        dx3 = dx3_ref[...]
        dx3_b = dx3.astype(BF16)
        for s, sz in _chunks(f):
            da = _dot_nt(dx3_b, w_down_ref[s:s + sz, :])
            g = gu_ref[:, s:s + sz]
            u = gu_ref[:, f + s:f + s + sz]
            sg = _sigmoid(g)
            dgu_ref[:, s:s + sz] = (da * u * (sg * (1.0 + g * (1.0 - sg)))).astype(dgu_ref.dtype)
            dgu_ref[:, f + s:f + s + sz] = (da * (g * sg)).astype(dgu_ref.dtype)
        dh2 = _dot_nn(dgu_ref[...], wt_gu_ref[...])
        xh, r = _rms(x2_ref[...])
        dg_ref[...] += jnp.sum(dh2 * xh, axis=0, keepdims=True)
        dx2_ref[...] = dx3 + _rms_bwd(dh2, xh, r, g_ref[...])

    return pl.pallas_call(
        body, name="ffn_bwd", grid=(m // tm,),
        in_specs=[_rows(tm, d), _rows(tm, 2 * f), _rows(tm, d), _vec(d), _whole((f, d)), _whole((2 * f, d))],
        out_specs=[_rows(tm, 2 * f), _rows(tm, d), _vec(d)],
        out_shape=[jax.ShapeDtypeStruct((m, 2 * f), BF16), jax.ShapeDtypeStruct((m, d), F32),
                   jax.ShapeDtypeStruct((1, d), F32)],
        compiler_params=_cp(),
    )(dx3, gu, x2, g_f.reshape(1, d), w_down, wt_gu)


def mid_bwd(dx2, qx, kv, xqg, xkg, x1, g_x, wo, wq, w_out, *, tm=512):
    m, d = x1.shape
    n_mem = kv.shape[0]
    nt = m // tm

    def body(dx2_ref, qx_ref, kv_ref, xqg_ref, xkg_ref, x1_ref, g_x_ref, wo_ref, wq_ref, w_out_ref,
             dq_ref, dx1_ref, dmixed_ref, dkv_ref, dqg_ref, dkg_ref, dg_ref):
        i = pl.program_id(0)
        _zero_at_first_step(dkv_ref, dqg_ref, dkg_ref, dg_ref)
        qg_v, kg_v = xqg_ref[...], xkg_ref[...]
        dx2 = dx2_ref[...]
        do = _dot_nt(dx2, wo_ref[...])
        dqg_acc = jnp.zeros((1, X_HEAD_DIM), F32)
        for h, (qh, rq, qn, kn, v, p) in enumerate(_xattn_heads(qx_ref, kv_ref, qg_v, kg_v, d)):
            cols = slice(h * X_HEAD_DIM, (h + 1) * X_HEAD_DIM)
            vcols = slice(d + h * X_HEAD_DIM, d + (h + 1) * X_HEAD_DIM)
            do_h = do[:, cols]
            dp = _dot_nt(do_h, v)
            ds = p * (dp - jnp.sum(p * dp, axis=-1, keepdims=True))
            dkv_ref[:, vcols] += _dot_tn(p, do_h)
            dqn = _dot_nn(ds, kn) * (X_HEAD_DIM ** -0.5)
            dkv_ref[:, cols] += _dot_tn(ds, qn) * (X_HEAD_DIM ** -0.5)
            dqg_acc = dqg_acc + jnp.sum(dqn * qh, axis=0, keepdims=True)
            dq_ref[:, cols] = _rms_bwd(dqn, qh, rq, qg_v).astype(dq_ref.dtype)
        dqg_ref[...] += dqg_acc
        dh1 = _dot_nt(dq_ref[...], wq_ref[...])
        xh, r = _rms(x1_ref[...])
        dg_ref[...] += jnp.sum(dh1 * xh, axis=0, keepdims=True)
        dx1 = dx2 + _rms_bwd(dh1, xh, r, g_x_ref[...])
        dx1_ref[...] = dx1
        dmixed_ref[...] = _dot_nt(dx1, w_out_ref[...])

        @pl.when(i == nt - 1)
        def _():
            dkg_acc = jnp.zeros((1, X_HEAD_DIM), F32)
            for h in range(N_X_HEADS):
                cols = slice(h * X_HEAD_DIM, (h + 1) * X_HEAD_DIM)
                kh, rk = _rms(kv_ref[:, cols])
                dkn = dkv_ref[:, cols]
                dkg_acc = dkg_acc + jnp.sum(dkn * kh, axis=0, keepdims=True)
                dkv_ref[:, cols] = _rms_bwd(dkn, kh, rk, kg_v)
            dkg_ref[...] = dkg_acc

    sq = _whole((d, d))
    full = pl.BlockSpec((n_mem, 2 * d), lambda i: (0, 0))
    return pl.pallas_call(
        body, name="mid_bwd", grid=(nt,),
        in_specs=[_rows(tm, d), _rows(tm, d), _whole((n_mem, 2 * d)), _vec(X_HEAD_DIM), _vec(X_HEAD_DIM), _rows(tm, d),
                  _vec(d), sq, sq, sq],
        out_specs=[_rows(tm, d), _rows(tm, d), _rows(tm, d), full, _vec(X_HEAD_DIM), _vec(X_HEAD_DIM), _vec(d)],
        out_shape=[jax.ShapeDtypeStruct((m, d), BF16), jax.ShapeDtypeStruct((m, d), F32), jax.ShapeDtypeStruct((m, d), F32),
                   jax.ShapeDtypeStruct((n_mem, 2 * d), F32), jax.ShapeDtypeStruct((1, X_HEAD_DIM), F32),
                   jax.ShapeDtypeStruct((1, X_HEAD_DIM), F32), jax.ShapeDtypeStruct((1, d), F32)],
        compiler_params=_cp(),
    )(dx2, qx, kv, xqg.reshape(1, X_HEAD_DIM), xkg.reshape(1, X_HEAD_DIM), x1, g_x.reshape(1, d), wo, wq, w_out)


def in_bwd(du, wt_in, x0, g_mix, dx1, *, tm=512):
    m, d = x0.shape
    n = wt_in.shape[0]

    def body(du_ref, wt_ref, x0_ref, g_ref, dx1_ref, dx0_ref, dg_ref):
        _zero_at_first_step(dg_ref)
        dh0 = _dot_nn(du_ref[...], wt_ref[...])
        xh, r = _rms(x0_ref[...])
        dg_ref[...] += jnp.sum(dh0 * xh, axis=0, keepdims=True)
        dx0_ref[...] = dx1_ref[...] + _rms_bwd(dh0, xh, r, g_ref[...])

    return pl.pallas_call(
        body, name="in_bwd", grid=(m // tm,),
        in_specs=[_rows(tm, n), _whole((n, d)), _rows(tm, d), _vec(d), _rows(tm, d)],
        out_specs=[_rows(tm, d), _vec(d)],
        out_shape=[jax.ShapeDtypeStruct((m, d), F32), jax.ShapeDtypeStruct((1, d), F32)],
        compiler_params=_cp(),
    )(du, wt_in, x0, g_mix.reshape(1, d), dx1)


SWA_TILE = 512
SWA_SUB = SWA_TILE // BLOCK


def _swa_mask():
    rows = GROUP * BLOCK
    r = lax.broadcasted_iota(jnp.int32, (rows, 2 * BLOCK), 0)
    j = lax.broadcasted_iota(jnp.int32, (rows, 2 * BLOCK), 1)
    dist = (r & (BLOCK - 1)) + BLOCK - j
    return dist.astype(F32), (dist >= 0) & (dist < BLOCK), j >= BLOCK


def _slope_col(kv):
    return jnp.concatenate([jnp.full((BLOCK, 1), 2.0 ** -(kv * GROUP + g + 1), F32) for g in range(GROUP)], axis=0)


def _sink_col(sinks_ref, kv):
    return jnp.concatenate([jnp.full((BLOCK, 1), sinks_ref[kv * GROUP + g], F32) for g in range(GROUP)], axis=0)


def _stack_heads(ref, rows, kv):
    return jnp.concatenate(
        [ref[rows, (kv * GROUP + g) * HEAD_DIM:(kv * GROUP + g + 1) * HEAD_DIM] for g in range(GROUP)], axis=0)


def _swa_keys(cur_ref, prev_ref, b, col):
    cols = slice(col, col + HEAD_DIM)
    if b == 0:
        return jnp.concatenate([prev_ref[:, cols], cur_ref[0:BLOCK, cols]], axis=0)
    return cur_ref[(b - 1) * BLOCK:(b + 1) * BLOCK, cols]


def _swa_probs(qn, kn, bias, valid, sink):
    s = _dot_nt(qn, kn) * (HEAD_DIM ** -0.5)
    s = jnp.where(valid, s + bias, NEG)
    mx = jnp.maximum(jnp.max(s, axis=-1, keepdims=True), sink)
    e = jnp.exp(s - mx)
    es = jnp.exp(sink - mx)
    den = jnp.sum(e, axis=-1, keepdims=True) + es
    return e / den, es / den


def swa_fwd(u, qg, kg, sinks):
    t = u.shape[0]
    nt = t // SWA_TILE

    def body(sinks_ref, cur_ref, prev_ref, qg_ref, kg_ref, o_ref):
        i = pl.program_id(0)
        qg_v = qg_ref[...]
        kg_v = kg_ref[...]
        dist, window, own_block = _swa_mask()
        valid_first = window & (own_block | (i > 0))
        for kv in range(N_KV_HEADS):
            sink = _sink_col(sinks_ref, kv)
            bias = -_slope_col(kv) * dist
            for b in range(SWA_SUB):
                rows = slice(b * BLOCK, (b + 1) * BLOCK)
                valid = valid_first if b == 0 else window
                qn = _rms(_stack_heads(cur_ref, rows, kv))[0] * qg_v
                kn = _rms(_swa_keys(cur_ref, prev_ref, b, ATTN_WIDTH + kv * HEAD_DIM))[0] * kg_v
                vv = _swa_keys(cur_ref, prev_ref, b, ATTN_WIDTH + KV_WIDTH + kv * HEAD_DIM)
                p, _ = _swa_probs(qn, kn, bias, valid, sink)
                o4 = _dot_nn(p, vv)
                for g in range(GROUP):
                    h = kv * GROUP + g
                    o_ref[rows, h * HEAD_DIM:(h + 1) * HEAD_DIM] = o4[g * BLOCK:(g + 1) * BLOCK].astype(o_ref.dtype)

    vec = pl.BlockSpec((1, HEAD_DIM), lambda i: (0, 0))
    return pl.pallas_call(
        body, name="swa_fwd", grid=(nt,),
        in_specs=[
            pl.BlockSpec(memory_space=pltpu.SMEM),
            pl.BlockSpec((SWA_TILE, QKV_WIDTH), lambda i: (i, 0)),
            pl.BlockSpec((BLOCK, QKV_WIDTH), lambda i: (jnp.maximum(i * SWA_SUB - 1, 0), 0)),
            vec, vec,
        ],
        out_specs=pl.BlockSpec((SWA_TILE, ATTN_WIDTH), lambda i: (i, 0)),
        out_shape=jax.ShapeDtypeStruct((t, 2 * ATTN_WIDTH), BF16), compiler_params=_cp(),
    )(sinks, u, u, qg.reshape(1, HEAD_DIM), kg.reshape(1, HEAD_DIM))


def swa_bwd(u, dmixed, qg, kg, sinks):
    t = u.shape[0]
    nt = t // SWA_TILE
    kcol = lambda kv: slice(kv * HEAD_DIM, (kv + 1) * HEAD_DIM)
    vcol = lambda kv: slice(KV_WIDTH + kv * HEAD_DIM, KV_WIDTH + (kv + 1) * HEAD_DIM)

    def body(sinks_ref, cur_ref, prev_ref, do_ref, qg_ref, kg_ref, du_ref, dqg_ref, dkg_ref, dsk_ref, acc_ref, carry_ref):
        step = pl.program_id(0)
        i = nt - 1 - step
        qg_v = qg_ref[...]
        kg_v = kg_ref[...]

        @pl.when(step == 0)
        def _():
            carry_ref[...] = jnp.zeros_like(carry_ref)
            dqg_ref[...] = jnp.zeros_like(dqg_ref)
            dkg_ref[...] = jnp.zeros_like(dkg_ref)
            dsk_ref[...] = jnp.zeros_like(dsk_ref)

        acc_ref[0:SWA_TILE, :] = jnp.zeros((SWA_TILE, 2 * KV_WIDTH), F32)
        acc_ref[SWA_TILE:SWA_TILE + BLOCK, :] = carry_ref[...]

        lane = lax.broadcasted_iota(jnp.int32, (1, LANES), 1)
        dqg_acc = jnp.zeros((1, HEAD_DIM), F32)
        dsk_acc = jnp.zeros((1, LANES), F32)
        dist, window, own_block = _swa_mask()
        valid_first = window & (own_block | (i > 0))
        for kv in range(N_KV_HEADS):
            sink = _sink_col(sinks_ref, kv)
            bias = -_slope_col(kv) * dist
            for b in range(SWA_SUB):
                rows = slice(b * BLOCK, (b + 1) * BLOCK)
                valid = valid_first if b == 0 else window
                qh, rq = _rms(_stack_heads(cur_ref, rows, kv))
                qn = qh * qg_v
                kn = _rms(_swa_keys(cur_ref, prev_ref, b, ATTN_WIDTH + kv * HEAD_DIM))[0] * kg_v
                vv = _swa_keys(cur_ref, prev_ref, b, ATTN_WIDTH + KV_WIDTH + kv * HEAD_DIM)
                p, ps = _swa_probs(qn, kn, bias, valid, sink)
                do4 = _stack_heads(do_ref, rows, kv)
                dp = _dot_nt(do4, vv)
                delta = jnp.sum(p * dp, axis=-1, keepdims=True)
                ds = p * (dp - delta)
                dsink = -ps * delta
                for g in range(GROUP):
                    part = jnp.sum(dsink[g * BLOCK:(g + 1) * BLOCK], axis=0, keepdims=True)
                    dsk_acc = dsk_acc + jnp.where(lane == kv * GROUP + g, part, 0.0)
                dvv = _dot_tn(p, do4)
                dqn = _dot_nn(ds, kn) * (HEAD_DIM ** -0.5)
                dkn = _dot_tn(ds, qn) * (HEAD_DIM ** -0.5)
                dqg_acc = dqg_acc + jnp.sum(dqn * qh, axis=0, keepdims=True)
                dq = _rms_bwd(dqn, qh, rq, qg_v)
                for g in range(GROUP):
                    h = kv * GROUP + g
                    du_ref[rows, h * HEAD_DIM:(h + 1) * HEAD_DIM] = dq[g * BLOCK:(g + 1) * BLOCK].astype(du_ref.dtype)
                keys = slice(b * BLOCK, (b + 2) * BLOCK)
                acc_ref[keys, kcol(kv)] += dkn
                acc_ref[keys, vcol(kv)] += dvv
        dqg_ref[...] += dqg_acc
        dsk_ref[...] += dsk_acc

        own = slice(BLOCK, BLOCK + SWA_TILE)
        dkg_acc = jnp.zeros((1, HEAD_DIM), F32)
        for kv in range(N_KV_HEADS):
            kh, rk = _rms(cur_ref[:, ATTN_WIDTH + kv * HEAD_DIM:ATTN_WIDTH + (kv + 1) * HEAD_DIM])
            dkn = acc_ref[own, kcol(kv)]
            dkg_acc = dkg_acc + jnp.sum(dkn * kh, axis=0, keepdims=True)
            dk = _rms_bwd(dkn, kh, rk, kg_v)
            du_ref[:, ATTN_WIDTH + kv * HEAD_DIM:ATTN_WIDTH + (kv + 1) * HEAD_DIM] = dk.astype(du_ref.dtype)
            vc = ATTN_WIDTH + KV_WIDTH + kv * HEAD_DIM
            du_ref[:, vc:vc + HEAD_DIM] = acc_ref[own, vcol(kv)].astype(du_ref.dtype)
        dkg_ref[...] += dkg_acc
        carry_ref[...] = acc_ref[0:BLOCK, :]

    vec = pl.BlockSpec((1, HEAD_DIM), lambda s: (0, 0))
    return pl.pallas_call(
        body, name="swa_bwd", grid=(nt,),
        in_specs=[
            pl.BlockSpec(memory_space=pltpu.SMEM),
            pl.BlockSpec((SWA_TILE, QKV_WIDTH), lambda s: (nt - 1 - s, 0)),
            pl.BlockSpec((BLOCK, QKV_WIDTH), lambda s: (jnp.maximum((nt - 1 - s) * SWA_SUB - 1, 0), 0)),
            pl.BlockSpec((SWA_TILE, ATTN_WIDTH), lambda s: (nt - 1 - s, 0)),
            vec, vec,
        ],
        out_specs=[
            pl.BlockSpec((SWA_TILE, QKV_WIDTH), lambda s: (nt - 1 - s, 0)),
            vec, vec, pl.BlockSpec((1, LANES), lambda s: (0, 0)),
        ],
        out_shape=[
            jax.ShapeDtypeStruct((t, IN_COLS), BF16),
            jax.ShapeDtypeStruct((1, HEAD_DIM), F32), jax.ShapeDtypeStruct((1, HEAD_DIM), F32),
            jax.ShapeDtypeStruct((1, LANES), F32),
        ],
        scratch_shapes=[pltpu.VMEM((SWA_TILE + BLOCK, 2 * KV_WIDTH), F32), pltpu.VMEM((BLOCK, 2 * KV_WIDTH), F32)],
        compiler_params=_cp(),
    )(sinks, u, u, dmixed, qg.reshape(1, HEAD_DIM), kg.reshape(1, HEAD_DIM))


CONV_TILE = 512
CONV_CHUNK = 64
VAL0 = QKV_WIDTH
GATE0 = QKV_WIDTH + CONV_CH


def _glu(ref):
    return ref[:, VAL0:GATE0] * _sigmoid(ref[:, GATE0:GATE0 + CONV_CH])


SUBLANES = 8
CONV_BUF = CONV_HALO + CONV_TILE + SUBLANES
CONV_EXT = CONV_HALO + CONV_TILE


def _fill_shifted(sh_ref):
    for r in range(1, SUBLANES):
        sh_ref[r, 0:CONV_EXT, :] = sh_ref[0, pl.ds(r, CONV_EXT), :]


def _shifted(sh_ref, start, offset, n):
    return sh_ref[offset % SUBLANES, pl.ds(start + offset - offset % SUBLANES, n), :]


def _layernorm_stats(y):
    mu = jnp.mean(y, axis=-1, keepdims=True)
    yc = y - mu
    rstd = lax.rsqrt(jnp.mean(yc * yc, axis=-1, keepdims=True) + EPS)
    return yc * rstd, rstd


def conv_fwd(u, mixed, conv_w, conv_b, ln_g, ln_b):
    t = u.shape[0]
    nt = t // CONV_TILE
    per = CONV_TILE // CONV_HALO

    def body(cur_ref, prev_ref, mixed_ref, w_ref, b_ref, g_ref, b2_ref, o_ref, y_ref, gl_ref):
        del mixed_ref
        i = pl.program_id(0)
        gl_ref[0, 0:CONV_HALO, :] = jnp.where(i > 0, _glu(prev_ref), 0.0)
        gl_ref[0, CONV_HALO:CONV_EXT, :] = _glu(cur_ref)
        gl_ref[0, CONV_EXT:CONV_BUF, :] = jnp.zeros((SUBLANES, CONV_CH), F32)
        _fill_shifted(gl_ref)
        for c0 in range(0, CONV_TILE, CONV_CHUNK):
            acc = jnp.broadcast_to(b_ref[...], (CONV_CHUNK, CONV_CH))
            for k in range(CONV_K):
                acc = acc + w_ref[k:k + 1, :] * _shifted(gl_ref, c0, 2 + k, CONV_CHUNK)
            y_ref[c0:c0 + CONV_CHUNK, :] = acc
        yh, _ = _layernorm_stats(y_ref[...])
        yln = yh * g_ref[...] + b2_ref[...]
        o_ref[...] = (yln * _sigmoid(yln)).astype(o_ref.dtype)

    vec = pl.BlockSpec((1, CONV_CH), lambda i: (0, 0))
    return pl.pallas_call(
        body, name="conv_fwd", grid=(nt,),
        in_specs=[
            pl.BlockSpec((CONV_TILE, IN_COLS), lambda i: (i, 0)),
            pl.BlockSpec((CONV_HALO, IN_COLS), lambda i: (jnp.maximum(i * per - 1, 0), 0)),
            pl.BlockSpec(memory_space=pl.ANY),
            pl.BlockSpec((CONV_HALO, CONV_CH), lambda i: (0, 0)),
            vec, vec, vec,
        ],
        out_specs=[pl.BlockSpec((CONV_TILE, CONV_CH), lambda i: (i, 1)), pl.BlockSpec((CONV_TILE, CONV_CH), lambda i: (i, 0))],
        out_shape=[jax.ShapeDtypeStruct(mixed.shape, mixed.dtype), jax.ShapeDtypeStruct((t, CONV_CH), F32)],
        scratch_shapes=[pltpu.VMEM((SUBLANES, CONV_BUF, CONV_CH), F32)],
        input_output_aliases={2: 0}, compiler_params=_cp(),
    )(u, u, mixed, conv_w, conv_b.reshape(1, CONV_CH), ln_g.reshape(1, CONV_CH), ln_b.reshape(1, CONV_CH))


def conv_bwd(u, y, dmixed, du, conv_w, ln_g, ln_b):
    t = u.shape[0]
    nt = t // CONV_TILE
    per = CONV_TILE // CONV_HALO

    def body(cur_ref, prev_ref, y_ref, yn_ref, do_ref, don_ref, du_in_ref, w_ref, g_ref, b2_ref,
             du_ref, dw_ref, dvec_ref, gl_ref, dy_ref):
        i = pl.program_id(0)
        last = i == nt - 1
        _zero_at_first_step(dw_ref, dvec_ref)

        gl_ref[0, 0:CONV_HALO, :] = jnp.where(i > 0, _glu(prev_ref), 0.0)
        gl_ref[0, CONV_HALO:CONV_EXT, :] = _glu(cur_ref)
        gl_ref[0, CONV_EXT:CONV_BUF, :] = jnp.zeros((SUBLANES, CONV_CH), F32)
        _fill_shifted(gl_ref)

        yh, rstd = _layernorm_stats(jnp.concatenate([y_ref[...], yn_ref[...]], axis=0))
        g = g_ref[...]
        yln = yh * g + b2_ref[...]
        sg = _sigmoid(yln)
        dout = jnp.concatenate([do_ref[...], jnp.where(last, 0.0, don_ref[...])], axis=0)
        dyln = dout * (sg * (1.0 + yln * (1.0 - sg)))
        dyh = dyln * g
        dy = rstd * (dyh - jnp.mean(dyh, axis=-1, keepdims=True) - yh * jnp.mean(dyh * yh, axis=-1, keepdims=True))
        dy_ref[0, 0:CONV_EXT, :] = dy
        dy_ref[0, CONV_EXT:CONV_BUF, :] = jnp.zeros((SUBLANES, CONV_CH), F32)
        _fill_shifted(dy_ref)

        own = slice(0, CONV_TILE)
        dvec_ref[0:1, :] += jnp.sum(dy[own], axis=0, keepdims=True)
        dvec_ref[1:2, :] += jnp.sum(dyln[own] * yh[own], axis=0, keepdims=True)
        dvec_ref[2:3, :] += jnp.sum(dyln[own], axis=0, keepdims=True)
        for k in range(CONV_K):
            dw_ref[k:k + 1, :] += jnp.sum(dy[own] * _shifted(gl_ref, 0, 2 + k, CONV_TILE), axis=0, keepdims=True)

        for c0 in range(0, CONV_TILE, CONV_CHUNK):
            acc = jnp.zeros((CONV_CHUNK, CONV_CH), F32)
            for k in range(CONV_K):
                acc = acc + w_ref[k:k + 1, :] * _shifted(dy_ref, c0, CONV_K - 1 - k, CONV_CHUNK)
            rows = slice(c0, c0 + CONV_CHUNK)
            val = cur_ref[rows, VAL0:GATE0]
            sgate = _sigmoid(cur_ref[rows, GATE0:GATE0 + CONV_CH])
            du_ref[rows, VAL0:GATE0] = (acc * sgate).astype(du_ref.dtype)
            du_ref[rows, GATE0:GATE0 + CONV_CH] = (acc * val * sgate * (1.0 - sgate)).astype(du_ref.dtype)
        du_ref[:, 0:QKV_WIDTH] = du_in_ref[:, 0:QKV_WIDTH]

    vec = pl.BlockSpec((1, CONV_CH), lambda i: (0, 0))
    n_halo = t // CONV_HALO
    return pl.pallas_call(
        body, name="conv_bwd", grid=(nt,),
        in_specs=[
            pl.BlockSpec((CONV_TILE, IN_COLS), lambda i: (i, 0)),
            pl.BlockSpec((CONV_HALO, IN_COLS), lambda i: (jnp.maximum(i * per - 1, 0), 0)),
            pl.BlockSpec((CONV_TILE, CONV_CH), lambda i: (i, 0)),
            pl.BlockSpec((CONV_HALO, CONV_CH), lambda i: (jnp.minimum((i + 1) * per, n_halo - 1), 0)),
            pl.BlockSpec((CONV_TILE, CONV_CH), lambda i: (i, 1)),
            pl.BlockSpec((CONV_HALO, CONV_CH), lambda i: (jnp.minimum((i + 1) * per, n_halo - 1), 1)),
            pl.BlockSpec((CONV_TILE, IN_COLS), lambda i: (i, 0)),
            pl.BlockSpec((CONV_HALO, CONV_CH), lambda i: (0, 0)),
            vec, vec,
        ],
        out_specs=[
            pl.BlockSpec((CONV_TILE, IN_COLS), lambda i: (i, 0)),
            pl.BlockSpec((CONV_HALO, CONV_CH), lambda i: (0, 0)),
            pl.BlockSpec((8, CONV_CH), lambda i: (0, 0)),
        ],
        out_shape=[
            jax.ShapeDtypeStruct(du.shape, du.dtype),
            jax.ShapeDtypeStruct((CONV_HALO, CONV_CH), F32),
            jax.ShapeDtypeStruct((8, CONV_CH), F32),
        ],
        scratch_shapes=[pltpu.VMEM((SUBLANES, CONV_BUF, CONV_CH), F32), pltpu.VMEM((SUBLANES, CONV_BUF, CONV_CH), F32)],
        input_output_aliases={6: 0}, compiler_params=_cp(),
    )(u, u, y, y, dmixed, dmixed, du, conv_w, ln_g.reshape(1, CONV_CH), ln_b.reshape(1, CONV_CH))


XATTN_TILE = 512


def _xattn_probs(qn, kn):
    s = _dot_nt(qn, kn) * (X_HEAD_DIM ** -0.5)
    e = jnp.exp(s - jnp.max(s, axis=-1, keepdims=True))
    return e / jnp.sum(e, axis=-1, keepdims=True)


def xattn_fwd(q, kv, qg, kg):
    t, d = q.shape
    n_mem = kv.shape[0]

    def body(q_ref, kv_ref, qg_ref, kg_ref, o_ref):
        for h in range(N_X_HEADS):
            cols = slice(h * X_HEAD_DIM, (h + 1) * X_HEAD_DIM)
            qn = _rms(q_ref[:, cols])[0] * qg_ref[...]
            kn = _rms(kv_ref[:, cols])[0] * kg_ref[...]
            p = _xattn_probs(qn, kn)
            o_ref[:, cols] = _dot_nn(p, kv_ref[:, d + h * X_HEAD_DIM:d + (h + 1) * X_HEAD_DIM]).astype(o_ref.dtype)

    vec = pl.BlockSpec((1, X_HEAD_DIM), lambda i: (0, 0))
    return pl.pallas_call(
        body, name="xattn_fwd", grid=(t // XATTN_TILE,),
        in_specs=[pl.BlockSpec((XATTN_TILE, d), lambda i: (i, 0)), pl.BlockSpec((n_mem, 2 * d), lambda i: (0, 0)), vec, vec],
        out_specs=pl.BlockSpec((XATTN_TILE, d), lambda i: (i, 0)),
        out_shape=jax.ShapeDtypeStruct((t, d), BF16), compiler_params=_cp(),
    )(q, kv, qg.reshape(1, X_HEAD_DIM), kg.reshape(1, X_HEAD_DIM))


def xattn_bwd(q, kv, do, qg, kg):
    t, d = q.shape
    n_mem = kv.shape[0]
    nt = t // XATTN_TILE

    def body(q_ref, kv_ref, do_ref, qg_ref, kg_ref, dq_ref, dkv_ref, dqg_ref, dkg_ref):
        i = pl.program_id(0)

        @pl.when(i == 0)
        def _():
            dkv_ref[...] = jnp.zeros_like(dkv_ref)
            dqg_ref[...] = jnp.zeros_like(dqg_ref)
            dkg_ref[...] = jnp.zeros_like(dkg_ref)

        qg_v = qg_ref[...]
        kg_v = kg_ref[...]
        dqg_acc = jnp.zeros((1, X_HEAD_DIM), F32)
        for h in range(N_X_HEADS):
            cols = slice(h * X_HEAD_DIM, (h + 1) * X_HEAD_DIM)
            vcols = slice(d + h * X_HEAD_DIM, d + (h + 1) * X_HEAD_DIM)
            qh, rq = _rms(q_ref[:, cols])
            qn = qh * qg_v
            kn = _rms(kv_ref[:, cols])[0] * kg_v
            v = kv_ref[:, vcols]
            do_h = do_ref[:, cols]
            p = _xattn_probs(qn, kn)
            dp = _dot_nt(do_h, v)
            ds = p * (dp - jnp.sum(p * dp, axis=-1, keepdims=True))
            dkv_ref[:, vcols] += _dot_tn(p, do_h)
            dqn = _dot_nn(ds, kn) * (X_HEAD_DIM ** -0.5)
            dkv_ref[:, cols] += _dot_tn(ds, qn) * (X_HEAD_DIM ** -0.5)
            dqg_acc = dqg_acc + jnp.sum(dqn * qh, axis=0, keepdims=True)
            dq_ref[:, cols] = _rms_bwd(dqn, qh, rq, qg_v).astype(dq_ref.dtype)
        dqg_ref[...] += dqg_acc

        @pl.when(i == nt - 1)
        def _():
            dkg_acc = jnp.zeros((1, X_HEAD_DIM), F32)
            for h in range(N_X_HEADS):
                cols = slice(h * X_HEAD_DIM, (h + 1) * X_HEAD_DIM)
                kh, rk = _rms(kv_ref[:, cols])
                dkn = dkv_ref[:, cols]
                dkg_acc = dkg_acc + jnp.sum(dkn * kh, axis=0, keepdims=True)
                dkv_ref[:, cols] = _rms_bwd(dkn, kh, rk, kg_v)
            dkg_ref[...] = dkg_acc

    vec = pl.BlockSpec((1, X_HEAD_DIM), lambda i: (0, 0))
    row = pl.BlockSpec((XATTN_TILE, d), lambda i: (i, 0))
    full = pl.BlockSpec((n_mem, 2 * d), lambda i: (0, 0))
    return pl.pallas_call(
        body, name="xattn_bwd", grid=(nt,),
        in_specs=[row, full, row, vec, vec],
        out_specs=[row, full, vec, vec],
        out_shape=[
            jax.ShapeDtypeStruct((t, d), BF16), jax.ShapeDtypeStruct((n_mem, 2 * d), F32),
            jax.ShapeDtypeStruct((1, X_HEAD_DIM), F32), jax.ShapeDtypeStruct((1, X_HEAD_DIM), F32),
        ],
        compiler_params=_cp(),
    )(q, kv, do, qg.reshape(1, X_HEAD_DIM), kg.reshape(1, X_HEAD_DIM))


def adamw(w, g, m, v, *, name):
    r, c = w.shape
    tr = r
    for cand in (512, 256, 128, 64, 32, 16, 8):
        if r % cand == 0 and r > cand:
            tr = cand
            break

    def body(w_ref, g_ref, m_ref, v_ref, d_ref, nm_ref, nv_ref):
        g_v = g_ref[...]
        m2 = ADAM_B1 * m_ref[...] + (1.0 - ADAM_B1) * g_v
        v2 = ADAM_B2 * v_ref[...] + (1.0 - ADAM_B2) * jnp.square(g_v)
        m_hat = m2 / (1.0 - ADAM_B1 ** ADAM_STEP)
        v_hat = v2 / (1.0 - ADAM_B2 ** ADAM_STEP)
        d_ref[...] = -ADAM_LR * (m_hat / (jnp.sqrt(v_hat) + ADAM_EPS) + ADAM_WD * w_ref[...])
        nm_ref[...] = m2
        nv_ref[...] = v2

    spec = pl.BlockSpec((tr, c), lambda i: (i, 0))
    shape = jax.ShapeDtypeStruct((r, c), F32)
    return pl.pallas_call(
        body, name=name, grid=(r // tr,), in_specs=[spec] * 4, out_specs=[spec] * 3,
        out_shape=[shape] * 3, compiler_params=_cp(),
    )(w, g, m, v)


def _position():
    return lax.axis_index("x"), lax.axis_index("y"), lax.axis_index("c")


def all_gather(shard, *, name, in_vmem):
    r, c_ = shard.shape

    def body(x_ref, out_ref, send_sems, recv_sems, local_sem):
        x, y, c = _position()
        me, sibling = (x, y, c), (x, y, 1 - c)
        chips = [(1 - x, y), (x, 1 - y), (1 - x, 1 - y)]

        def rows(px, py, pc):
            return out_ref.at[4 * px + 2 * py + pc]

        def copy(k, block, to, src=None):
            return pltpu.make_async_remote_copy(
                src_ref=rows(*block) if src is None else src, dst_ref=rows(*block),
                send_sem=send_sems.at[k], recv_sem=recv_sems.at[k], device_id=to, device_id_type=MESH)

        mine = pltpu.make_async_copy(x_ref, rows(*me), local_sem)
        mine.start()
        first = [copy(0, me, sibling, src=x_ref)]
        first += [copy(1 + j, me, (*chip, c), src=x_ref) for j, chip in enumerate(chips)]
        for cp in first:
            cp.start()
        passed = [copy(4 + j, (*chip, c), sibling) for j, chip in enumerate(chips)]
        for j, chip in enumerate(chips):
            copy(1 + j, (*chip, c), me).wait_recv()
            passed[j].start()
        copy(0, sibling, me).wait_recv()
        for j, chip in enumerate(chips):
            copy(4 + j, (*chip, 1 - c), me).wait_recv()
        for cp in first + passed:
            cp.wait_send()
        mine.wait()

    space = pltpu.VMEM if in_vmem else pltpu.HBM
    return pl.pallas_call(
        body, name=name,
        out_shape=jax.ShapeDtypeStruct((N_DEV, r, c_), shard.dtype),
        in_specs=[pl.BlockSpec(memory_space=space)], out_specs=pl.BlockSpec(memory_space=space),
        scratch_shapes=[pltpu.SemaphoreType.DMA((7,)), pltpu.SemaphoreType.DMA((7,)), pltpu.SemaphoreType.DMA],
        compiler_params=_cp(),
    )(shard)


def exchange_sibling(parts):
    _, r, c_ = parts.shape

    def body(p_ref, out_ref, send_sems, recv_sems):
        x, y, c = _position()

        def copy(k):
            return pltpu.make_async_remote_copy(
                src_ref=p_ref.at[2 * k + (1 - c)], dst_ref=out_ref.at[k],
                send_sem=send_sems.at[k], recv_sem=recv_sems.at[k], device_id=(x, y, 1 - c), device_id_type=MESH)

        for k in range(4):
            copy(k).start()
        for k in range(4):
            copy(k).wait_recv()
        for k in range(4):
            copy(k).wait_send()

    hbm = pl.BlockSpec(memory_space=pltpu.HBM)
    return pl.pallas_call(
        body, name="rs_exchange_sibling",
        out_shape=jax.ShapeDtypeStruct((4, r, c_), parts.dtype),
        in_specs=[hbm], out_specs=hbm,
        scratch_shapes=[pltpu.SemaphoreType.DMA((4,)), pltpu.SemaphoreType.DMA((4,))],
        compiler_params=_cp(),
    )(parts)


_HBM = pl.BlockSpec(memory_space=pltpu.HBM)
_SEM = pl.BlockSpec(memory_space=pltpu.SEMAPHORE)
_EFFECT = pltpu.SideEffectType.DATAFLOW_SIDE_EFFECTING


def split_start(src, land, plan, n, *, name):
    def body(src_ref, land_ref, send_sems, recv_sems, src_thru, land_thru, token):
        del src_thru, land_thru
        for k in range(n):
            s, d, to = plan(src_ref, land_ref, k)
            pltpu.make_async_remote_copy(src_ref=s, dst_ref=d, send_sem=send_sems.at[k], recv_sem=recv_sems.at[k],
                                         device_id=to, device_id_type=MESH).start()
        token[...] = jnp.zeros_like(token)

    return pl.pallas_call(
        body, name=name,
        out_shape=(pltpu.SemaphoreType.DMA((n,)), pltpu.SemaphoreType.DMA((n,)), pltpu.HBM(src.shape, src.dtype),
                   pltpu.HBM(land.shape, land.dtype), jax.ShapeDtypeStruct((8, LANES), F32)),
        in_specs=(_HBM, _HBM), out_specs=(_SEM, _SEM, _HBM, _HBM, pl.BlockSpec(memory_space=pltpu.VMEM)),
        input_output_aliases={0: 2, 1: 3},
        compiler_params=pltpu.CompilerParams(has_side_effects=_EFFECT),
    )(pltpu.with_memory_space_constraint(src, pltpu.HBM), pltpu.with_memory_space_constraint(land, pltpu.HBM))


def split_wait(started, after, plan, n, *, name):
    send_sems, recv_sems, src_thru, land_thru, _ = started

    def body(src_ref, land_ref, send_sems, recv_sems, after_ref, src_out, land_out):
        del after_ref, src_out, land_out
        for k in range(n):
            s, d, to = plan(src_ref, land_ref, k)
            cp = pltpu.make_async_remote_copy(src_ref=s, dst_ref=d, send_sem=send_sems.at[k], recv_sem=recv_sems.at[k],
                                              device_id=to, device_id_type=MESH)
            cp.wait_send()
            cp.wait_recv()

    return pl.pallas_call(
        body, name=name,
        out_shape=(pltpu.HBM(src_thru.shape, src_thru.dtype), pltpu.HBM(land_thru.shape, land_thru.dtype)),
        in_specs=(_HBM, _HBM, _SEM, _SEM, pl.BlockSpec(memory_space=pl.ANY)), out_specs=(_HBM, _HBM),
        input_output_aliases={0: 0, 1: 1},
        compiler_params=pltpu.CompilerParams(has_side_effects=_EFFECT),
    )(src_thru, land_thru, send_sems, recv_sems, after)


def _gather_plan(src_ref, land_ref, k):
    x, y, c = _position()
    bits = k + 1
    peer = ((1 - x) if bits & 4 else x, (1 - y) if bits & 2 else y, (1 - c) if bits & 1 else c)
    return src_ref, land_ref.at[4 * x + 2 * y + c], peer


def _sibling_plan(src_ref, land_ref, k):
    x, y, c = _position()
    return src_ref.at[2 * k + (1 - c)], land_ref.at[k], (x, y, 1 - c)


def _chips_plan(src_ref, land_ref, j):
    x, y, c = _position()
    px, py = [(1 - x, y), (x, 1 - y), (1 - x, 1 - y)][j]
    return src_ref.at[2 * px + py], land_ref.at[j], (px, py, c)


def _sum_rows(r):
    return max(t for t in range(16, 513, 16) if r % t == 0)


def sum_for_chips(parts, from_sibling, c_idx):
    _, r, c_ = parts.shape
    tr = _sum_rows(r)

    def body(c_ref, p_ref, s_ref, o_ref):
        del c_ref
        o_ref[...] = (p_ref[...] + s_ref[...]).astype(o_ref.dtype)

    return pl.pallas_call(
        body, name="rs_sum_for_chips",
        grid_spec=pltpu.PrefetchScalarGridSpec(
            num_scalar_prefetch=1, grid=(4, r // tr),
            in_specs=[pl.BlockSpec((None, tr, c_), lambda k, i, c_ref: (2 * k + c_ref[0], i, 0)),
                      pl.BlockSpec((None, tr, c_), lambda k, i, c_ref: (k, i, 0))],
            out_specs=pl.BlockSpec((None, tr, c_), lambda k, i, c_ref: (k, i, 0))),
        out_shape=jax.ShapeDtypeStruct((4, r, c_), BF16), compiler_params=_cp(),
    )(c_idx, parts, from_sibling)


def exchange_chips(sums):
    _, r, c_ = sums.shape

    def body(s_ref, out_ref, send_sems, recv_sems):
        x, y, c = _position()
        chips = [(1 - x, y), (x, 1 - y), (1 - x, 1 - y)]

        def copy(j):
            px, py = chips[j]
            return pltpu.make_async_remote_copy(
                src_ref=s_ref.at[2 * px + py], dst_ref=out_ref.at[j],
                send_sem=send_sems.at[j], recv_sem=recv_sems.at[j], device_id=(px, py, c), device_id_type=MESH)

        for j in range(3):
            copy(j).start()
        for j in range(3):
            copy(j).wait_recv()
        for j in range(3):
            copy(j).wait_send()

    hbm = pl.BlockSpec(memory_space=pltpu.HBM)
    return pl.pallas_call(
        body, name="rs_exchange_chips",
        out_shape=jax.ShapeDtypeStruct((3, r, c_), sums.dtype),
        in_specs=[hbm], out_specs=hbm,
        scratch_shapes=[pltpu.SemaphoreType.DMA((3,)), pltpu.SemaphoreType.DMA((3,))],
        compiler_params=_cp(),
    )(sums)


def sum_final(parts, from_sibling, from_chips, kc_idx):
    _, r, c_ = parts.shape
    tr = _sum_rows(r)

    def body(kc_ref, p_ref, s_ref, a_ref, b_ref, d_ref, o_ref):
        del kc_ref
        o_ref[...] = (((p_ref[...] + s_ref[...]) + a_ref[...].astype(F32)) + b_ref[...].astype(F32)) + d_ref[...].astype(F32)

    def chip_spec(j):
        return pl.BlockSpec((None, tr, c_), lambda i, kc: (j, i, 0))

    return pl.pallas_call(
        body, name="rs_sum_final",
        grid_spec=pltpu.PrefetchScalarGridSpec(
            num_scalar_prefetch=1, grid=(r // tr,),
            in_specs=[pl.BlockSpec((None, tr, c_), lambda i, kc: (2 * kc[0] + kc[1], i, 0)),
                      pl.BlockSpec((None, tr, c_), lambda i, kc: (kc[0], i, 0)),
                      chip_spec(0), chip_spec(1), chip_spec(2)],
            out_specs=pl.BlockSpec((tr, c_), lambda i, kc: (i, 0))),
        out_shape=jax.ShapeDtypeStruct((r, c_), F32), compiler_params=_cp(),
    )(kc_idx, parts, from_sibling, from_chips, from_chips, from_chips)


def sum_devices(gathered):
    n, r, c_ = gathered.shape

    def body(g_ref, o_ref):
        acc = g_ref[0]
        for k in range(1, n):
            acc = acc + g_ref[k]
        o_ref[...] = acc

    return pl.pallas_call(
        body, name="sum_devices", out_shape=jax.ShapeDtypeStruct((r, c_), F32), compiler_params=_cp(),
    )(gathered)


BIG = (
    ("w_in", IN_COLS, True), ("w_out", D_MODEL, False), ("wq_x", D_MODEL, False), ("wkv_x", 2 * D_MODEL, True),
    ("wo_x", D_MODEL, False), ("w_gate_up", 2 * D_FF, True), ("w_down", D_FF, False),
)
SHARD_ROWS = sum(rows // N_DEV for _, rows, _ in BIG)

SMALL = ("norm_mix_g", "q_norm_g", "k_norm_g", "sinks", "conv_b", "conv_ln_g", "conv_ln_b",
         "norm_x_g", "norm_mem_g", "xq_norm_g", "xk_norm_g", "norm_ffn_g")


def _pack_rows(vectors, width=LANES, row_multiple=8):
    flat = jnp.concatenate([v.reshape(-1) for v in vectors])
    per = width * row_multiple
    padded = -(-flat.shape[0] // per) * per
    return jnp.pad(flat, (0, padded - flat.shape[0])).reshape(-1, width)


def _unpack_rows(packed, shapes):
    flat = packed.reshape(-1)
    out, at = [], 0
    for s in shapes:
        n = 1
        for dim in s:
            n *= dim
        out.append(flat[at:at + n].reshape(s))
        at += n
    return out


def _layer_fwd(x0, mem, w, s):
    h0, u = norm_proj(x0, s["norm_mix_g"], w["w_in"])
    mixed = swa_fwd(u, s["q_norm_g"], s["k_norm_g"], s["sinks"])
    mixed, conv_y = conv_fwd(u, mixed, s["conv_w"], s["conv_b"], s["conv_ln_g"], s["conv_ln_b"])
    memn = rms_fwd(mem, s["norm_mem_g"])
    kv = mm(memn, w["wkv_x"], trans_b=True, out_dtype=F32, name="mm_kv")
    x1, h1, qx, o, x2, h2 = mid_fwd(mixed, x0, w["w_out"], s["norm_x_g"], w["wq_x"], kv, s["xq_norm_g"], s["xk_norm_g"],
                                    w["wo_x"], s["norm_ffn_g"])
    gu, a, x3 = ffn_fwd(h2, x2, w["w_gate_up"], w["w_down"])
    saved = dict(x0=x0, h0=h0, u=u, conv_y=conv_y, mixed=mixed, x1=x1, h1=h1, qx=qx, memn=memn, kv=kv, o=o, x2=x2, h2=h2,
                 gu=gu, a=a)
    return x3, saved


def _ordered_after(a, token):
    return a if token is None else a + token[0, 0]


def _layer_bwd(dx3, mem, w, s, sv, token, stage_done):
    gs = {}
    dgu, dx2, dg = ffn_bwd(dx3, sv["gu"], sv["x2"], _ordered_after(s["norm_ffn_g"], token), w["w_down"], w["w_gate_up"])
    gs["norm_ffn_g"] = dg
    gb = {"w_down": mm_tn(sv["a"], dx3, name="mm_dw_down")}
    gb["w_gate_up"] = mm_tn(dgu, sv["h2"], name="mm_dw_gate_up")
    token = stage_done("ffn", gb, gb["w_gate_up"])

    gb = {}
    dq, dx1, dmixed, dkv, dqg, dkg, dg = mid_bwd(dx2, sv["qx"], sv["kv"], s["xq_norm_g"], s["xk_norm_g"], sv["x1"],
                                                 _ordered_after(s["norm_x_g"], token), w["wo_x"], w["wq_x"], w["w_out"])
    gs["xq_norm_g"], gs["xk_norm_g"], gs["norm_x_g"] = dqg, dkg, dg
    gb["wo_x"] = mm_tn(sv["o"], dx2, name="mm_dwo")
    gb["wq_x"] = mm_tn(sv["h1"], dq, name="mm_dwq")
    dmemn = mm(dkv, w["wkv_x"], trans_b=False, out_dtype=F32, name="mm_dmemn")
    gb["wkv_x"] = mm_tn(dkv, sv["memn"], name="mm_dwkv")
    _, dg = rms_bwd(dmemn, mem, s["norm_mem_g"], None)
    gs["norm_mem_g"] = dg
    gb["w_out"] = mm_tn(sv["mixed"], dx1, name="mm_dw_out")
    token = stage_done("mid", gb, gb["w_out"])

    du, dqg, dkg, dsinks = swa_bwd(sv["u"], dmixed, _ordered_after(s["q_norm_g"], token), s["k_norm_g"], s["sinks"])
    gs["q_norm_g"], gs["k_norm_g"], gs["sinks"] = dqg, dkg, dsinks[0, :N_Q_HEADS]
    du, dconv_w, dvec = conv_bwd(sv["u"], sv["conv_y"], dmixed, du, s["conv_w"], s["conv_ln_g"], s["conv_ln_b"])
    gs["conv_w"] = dconv_w[:CONV_K]
    gs["conv_b"], gs["conv_ln_g"], gs["conv_ln_b"] = dvec[0], dvec[1], dvec[2]
    dx0, dg = in_bwd(du, w["w_in"], sv["x0"], s["norm_mix_g"], dx1)
    gs["norm_mix_g"] = dg
    token = stage_done("mix", {"w_in": mm_tn(du, sv["h0"], name="mm_dw_in")}, dx0)
    return dx0, gs, token


def _local_step(x, mem, target, weights_of, smalls, stage_done):
    saved, weights = [], []
    h = x
    for l in range(DEPTH):
        weights.append(weights_of(l, h))
        h, sv = _layer_fwd(h, mem, weights[l], smalls[l])
        saved.append(sv)
    dx, loss_part = loss_head(h, target)
    gss, token = [None] * DEPTH, None
    for l in reversed(range(DEPTH)):
        dx, gss[l], token = _layer_bwd(dx, mem, weights[l], smalls[l], saved[l], token,
                                       functools.partial(stage_done, l))
    return loss_part[0, 0], dx, gss


def kernel(x, mem, norm_mix_g, w_in, q_norm_g, k_norm_g, sinks, conv_w, conv_b, conv_ln_g, conv_ln_b, w_out, norm_x_g, norm_mem_g, wq_x, wkv_x, xq_norm_g, xk_norm_g, wo_x, norm_ffn_g, w_gate_up, w_down, loss_target, m_norm_mix_g, m_w_in, m_q_norm_g, m_k_norm_g, m_sinks, m_conv_w, m_conv_b, m_conv_ln_g, m_conv_ln_b, m_w_out, m_norm_x_g, m_norm_mem_g, m_wq_x, m_wkv_x, m_xq_norm_g, m_xk_norm_g, m_wo_x, m_norm_ffn_g, m_w_gate_up, m_w_down, v_norm_mix_g, v_w_in, v_q_norm_g, v_k_norm_g, v_sinks, v_conv_w, v_conv_b, v_conv_ln_g, v_conv_ln_b, v_w_out, v_norm_x_g, v_norm_mem_g, v_wq_x, v_wkv_x, v_xq_norm_g, v_xk_norm_g, v_wo_x, v_norm_ffn_g, v_w_gate_up, v_w_down):
    P = dict(norm_mix_g=norm_mix_g, w_in=w_in, q_norm_g=q_norm_g, k_norm_g=k_norm_g, sinks=sinks, conv_w=conv_w, conv_b=conv_b,
             conv_ln_g=conv_ln_g, conv_ln_b=conv_ln_b, w_out=w_out, norm_x_g=norm_x_g, norm_mem_g=norm_mem_g, wq_x=wq_x,
             wkv_x=wkv_x, xq_norm_g=xq_norm_g, xk_norm_g=xk_norm_g, wo_x=wo_x, norm_ffn_g=norm_ffn_g, w_gate_up=w_gate_up,
             w_down=w_down)
    M = dict(norm_mix_g=m_norm_mix_g, w_in=m_w_in, q_norm_g=m_q_norm_g, k_norm_g=m_k_norm_g, sinks=m_sinks, conv_w=m_conv_w,
             conv_b=m_conv_b, conv_ln_g=m_conv_ln_g, conv_ln_b=m_conv_ln_b, w_out=m_w_out, norm_x_g=m_norm_x_g,
             norm_mem_g=m_norm_mem_g, wq_x=m_wq_x, wkv_x=m_wkv_x, xq_norm_g=m_xq_norm_g, xk_norm_g=m_xk_norm_g, wo_x=m_wo_x,
             norm_ffn_g=m_norm_ffn_g, w_gate_up=m_w_gate_up, w_down=m_w_down)
    V = dict(norm_mix_g=v_norm_mix_g, w_in=v_w_in, q_norm_g=v_q_norm_g, k_norm_g=v_k_norm_g, sinks=v_sinks, conv_w=v_conv_w,
             conv_b=v_conv_b, conv_ln_g=v_conv_ln_g, conv_ln_b=v_conv_ln_b, w_out=v_w_out, norm_x_g=v_norm_x_g,
             norm_mem_g=v_norm_mem_g, wq_x=v_wq_x, wkv_x=v_wkv_x, xq_norm_g=v_xq_norm_g, xk_norm_g=v_xk_norm_g, wo_x=v_wo_x,
             norm_ffn_g=v_norm_ffn_g, w_gate_up=v_w_gate_up, w_down=v_w_down)
    order = ["norm_mix_g", "w_in", "q_norm_g", "k_norm_g", "sinks", "conv_w", "conv_b", "conv_ln_g", "conv_ln_b", "w_out",
             "norm_x_g", "norm_mem_g", "wq_x", "wkv_x", "xq_norm_g", "xk_norm_g", "wo_x", "norm_ffn_g", "w_gate_up", "w_down"]
    xi, yi, ci = _position()
    dev = 4 * xi + 2 * yi + ci
    x2d, mem2d, tgt2d = x[0], mem[0], loss_target[0]

    def travelling(name, l, transposed):
        a = P[name][l]
        return (a.T if transposed else a).astype(BF16)

    def unpack(gathered):
        at, wl = 0, {}
        for n, rows, _ in BIG:
            wl[n] = gathered[:, at:at + rows // N_DEV, :].reshape(rows, D_MODEL)
            at += rows // N_DEV
        return wl

    packed = [jnp.concatenate([travelling(n, l, tr) for n, _, tr in BIG], axis=0) for l in range(DEPTH)]
    weights0 = unpack(all_gather(packed[0], name="ag_weights0", in_vmem=False))
    land = lax.dynamic_update_slice(lax.empty((N_DEV, SHARD_ROWS, D_MODEL), BF16), packed[1][None], (dev, 0, 0))
    gather1 = split_start(packed[1], land, _gather_plan, N_DEV - 1, name="ag_weights1_start")

    def weights_of(l, marker):
        if l == 0:
            return weights0
        return unpack(split_wait(gather1, marker, _gather_plan, N_DEV - 1, name="ag_weights1_wait")[1])

    cw = jnp.pad(conv_w.reshape(DEPTH * CONV_K, CONV_CH // N_DEV), ((0, 2), (0, LANES - CONV_CH // N_DEV)))
    cw_all = all_gather(cw, name="ag_conv_w", in_vmem=True)
    cw_full = cw_all[:, :DEPTH * CONV_K, :CONV_CH // N_DEV].reshape(N_DEV, DEPTH, CONV_K, CONV_CH // N_DEV)
    cw_full = jnp.transpose(cw_full, (1, 2, 0, 3)).reshape(DEPTH, CONV_K, CONV_CH)
    smalls = []
    for l in range(DEPTH):
        sl = {n: P[n][l] for n in SMALL}
        sl["conv_w"] = jnp.pad(cw_full[l], ((0, CONV_HALO - CONV_K), (0, 0)))
        smalls.append(sl)
    smalls[0]["norm_mix_g"] = _ordered_after(smalls[0]["norm_mix_g"], gather1[4])

    c_idx = jnp.reshape(ci, (1,)).astype(jnp.int32)
    kc_idx = jnp.stack([2 * xi + yi, ci]).astype(jnp.int32)
    ffn_names = ("w_gate_up", "w_down")
    groups = {1: [n for n, _, _ in BIG], "ffn": list(ffn_names), "rest": [n for n, _, _ in BIG if n not in ffn_names]}
    rows_of = {n: rows for n, rows, _ in BIG}
    got, flight, reduced = {}, {}, {}

    def as_parts(names, gb):
        return jnp.concatenate([gb[n].reshape(N_DEV, rows_of[n] // N_DEV, D_MODEL) for n in names], axis=1)

    def land_like(parts, blocks, dtype):
        return lax.empty((blocks,) + parts.shape[1:], dtype)

    def to_sibling(key, parts):
        flight[key] = split_start(parts, land_like(parts, 4, F32), _sibling_plan, 4, name=f"rs_sibling_{key}_start")
        return flight[key][4]

    def to_chips(key, marker):
        parts, from_sibling = split_wait(flight[key], marker, _sibling_plan, 4, name=f"rs_sibling_{key}_wait")
        chip_sums = sum_for_chips(parts, from_sibling, c_idx)
        started = split_start(chip_sums, land_like(parts, 3, BF16), _chips_plan, 3, name=f"rs_chips_{key}_start")
        flight[key] = (parts, from_sibling, started)
        return started[4]

    def finish(key, marker):
        parts, from_sibling, started = flight[key]
        _, from_chips = split_wait(started, marker, _chips_plan, 3, name=f"rs_chips_{key}_wait")
        reduced[key] = sum_final(parts, from_sibling, from_chips, kc_idx)

    def stage_done(l, stage, gb, marker):
        got.update({(l, n): g for n, g in gb.items()})
        if l == 1:
            if stage != "mix":
                return None
            return to_sibling(1, as_parts(groups[1], {n: got[(1, n)] for n in groups[1]}))
        if stage == "ffn":
            t1 = to_chips(1, marker)
            t2 = to_sibling("ffn", as_parts(groups["ffn"], gb))
            return t1 + t2
        if stage == "mid":
            return to_chips("ffn", marker)
        finish(1, marker)
        finish("ffn", marker)
        parts = as_parts(groups["rest"], {n: got[(0, n)] for n in groups["rest"]})
        from_sibling = exchange_sibling(parts)
        from_chips = exchange_chips(sum_for_chips(parts, from_sibling, c_idx))
        reduced["rest"] = sum_final(parts, from_sibling, from_chips, kc_idx)
        return None

    loss_part, grad_x, gss = _local_step(x2d, mem2d, tgt2d, weights_of, smalls, stage_done)
    loss = lax.psum(loss_part, ("x", "y", "c"))

    grads = {}
    for n, rows, transposed in BIG:
        per_layer = []
        for l, key in ((0, "ffn" if n in ffn_names else "rest"), (1, 1)):
            at = sum(rows_of[m] // N_DEV for m in groups[key][:groups[key].index(n)])
            g = reduced[key][at:at + rows // N_DEV]
            per_layer.append(g.T if transposed else g)
        grads[n] = jnp.stack(per_layer)

    small_names = SMALL + ("conv_w",)
    small_shapes = [(DEPTH,) + ((CONV_K, CONV_CH) if n == "conv_w" else P[n].shape[1:]) for n in small_names]
    small_parts = _pack_rows([jnp.stack([gss[l][n].reshape(sh[1:]) for l in range(DEPTH)])
                              for n, sh in zip(small_names, small_shapes)])
    small_sum = sum_devices(all_gather(small_parts, name="ag_small_grads", in_vmem=True))
    for n, g in zip(small_names, _unpack_rows(small_sum, small_shapes)):
        if n == "conv_w":
            g = lax.dynamic_slice_in_dim(g, dev * (CONV_CH // N_DEV), CONV_CH // N_DEV, axis=2)
        grads[n] = g

    delta, new_m, new_v = {}, {}, {}
    for n, _, _ in BIG:
        shape = P[n].shape
        two_d = lambda a: a.reshape(shape[0] * shape[1], shape[2])
        d_, m_, v_ = adamw(two_d(P[n]), two_d(grads[n]), two_d(M[n]), two_d(V[n]), name="adamw_" + n)
        delta[n], new_m[n], new_v[n] = d_.reshape(shape), m_.reshape(shape), v_.reshape(shape)
    shapes = [P[n].shape for n in small_names]
    d_, m_, v_ = adamw(_pack_rows([P[n] for n in small_names]), _pack_rows([grads[n] for n in small_names]),
                       _pack_rows([M[n] for n in small_names]), _pack_rows([V[n] for n in small_names]), name="adamw_small")
    for n, dd, mm_, vv in zip(small_names, _unpack_rows(d_, shapes), _unpack_rows(m_, shapes), _unpack_rows(v_, shapes)):
        delta[n], new_m[n], new_v[n] = dd, mm_, vv

    return (loss, grad_x[None], *[grads[n] for n in order], *[delta[n] for n in order],
            *[new_m[n] for n in order], *[new_v[n] for n in order])
```

```python
import functools

import jax
import jax.numpy as jnp
from jax import lax
from jax.experimental import pallas as pl
from jax.experimental.pallas import tpu as pltpu

F32 = jnp.float32
BF16 = jnp.bfloat16

D_MODEL = 1024
HEAD_DIM = 64
N_Q_HEADS = 8
N_KV_HEADS = 2
GROUP = N_Q_HEADS // N_KV_HEADS
ATTN_WIDTH = N_Q_HEADS * HEAD_DIM
KV_WIDTH = N_KV_HEADS * HEAD_DIM
QKV_WIDTH = ATTN_WIDTH + 2 * KV_WIDTH
CONV_CH = 512
IN_COLS = QKV_WIDTH + 2 * CONV_CH
CONV_K = 31
CONV_HALO = 32
BLOCK = 128
N_X_HEADS = 4
X_HEAD_DIM = 256
D_FF = 2816
EPS = 1e-6
NEG = -1e30
DEPTH = 2
N_DEV = 8

ADAM_LR = 0.001
ADAM_B1 = 0.9
ADAM_B2 = 0.999
ADAM_EPS = 1e-08
ADAM_WD = 0.01
ADAM_STEP = 10

V7X_VMEM_LIMIT = 56 * 1024 * 1024
LANES = 128

MESH = pl.DeviceIdType.MESH


def _cp(**kw):
    return pltpu.CompilerParams(vmem_limit_bytes=V7X_VMEM_LIMIT, **kw)


def _dot(a, b, dims):
    return lax.dot_general(a.astype(BF16), b.astype(BF16), (dims, ((), ())), preferred_element_type=F32)


def _dot_nn(a, b):
    return _dot(a, b, ((1,), (0,)))


def _dot_nt(a, b):
    return _dot(a, b, ((1,), (1,)))


def _dot_tn(a, b):
    return _dot(a, b, ((0,), (0,)))


def _sigmoid(x):
    return jax.nn.sigmoid(x)


def _rms(x):
    r = lax.rsqrt(jnp.mean(x * x, axis=-1, keepdims=True) + EPS)
    return x * r, r


def _rms_bwd(dy, xhat, r, g):
    dxh = dy * g
    return r * (dxh - xhat * jnp.mean(dxh * xhat, axis=-1, keepdims=True))


def rms_fwd(x, g, *, tm=512):
    m, d = x.shape
    tm = min(tm, m)

    def body(x_ref, g_ref, o_ref):
        xh, _ = _rms(x_ref[...])
        o_ref[...] = (xh * g_ref[...]).astype(o_ref.dtype)

    return pl.pallas_call(
        body, name="rms_fwd", grid=(m // tm,),
        in_specs=[pl.BlockSpec((tm, d), lambda i: (i, 0)), pl.BlockSpec((1, d), lambda i: (0, 0))],
        out_specs=pl.BlockSpec((tm, d), lambda i: (i, 0)),
        out_shape=jax.ShapeDtypeStruct((m, d), BF16), compiler_params=_cp(),
    )(x, g.reshape(1, d))


def rms_bwd(dh, x, g, dres, *, tm=512):
    m, d = x.shape
    tm = min(tm, m)
    has_res = dres is not None

    def body(*refs):
        if has_res:
            dh_ref, x_ref, g_ref, r_ref, dx_ref, dg_ref = refs
        else:
            dh_ref, x_ref, g_ref, dx_ref, dg_ref = refs
        xh, r = _rms(x_ref[...])
        dy = dh_ref[...].astype(F32)

        @pl.when(pl.program_id(0) == 0)
        def _():
            dg_ref[...] = jnp.zeros_like(dg_ref)

        dg_ref[...] += jnp.sum(dy * xh, axis=0, keepdims=True)
        dx = _rms_bwd(dy, xh, r, g_ref[...])
        if has_res:
            dx = dx + r_ref[...]
        dx_ref[...] = dx

    row = pl.BlockSpec((tm, d), lambda i: (i, 0))
    vec = pl.BlockSpec((1, d), lambda i: (0, 0))
    ins = [dh, x, g.reshape(1, d)] + ([dres] if has_res else [])
    return pl.pallas_call(
        body, name="rms_bwd" + ("_res" if has_res else ""), grid=(m // tm,),
        in_specs=[row, row, vec] + ([row] if has_res else []),
        out_specs=[row, vec],
        out_shape=[jax.ShapeDtypeStruct((m, d), F32), jax.ShapeDtypeStruct((1, d), F32)],
        compiler_params=_cp(),
    )(*ins)


def swiglu_fwd(gu, *, tm=512):
    m, f2 = gu.shape
    f = f2 // 2

    def body(g_ref, u_ref, o_ref):
        g = g_ref[...]
        o_ref[...] = (g * _sigmoid(g) * u_ref[...]).astype(o_ref.dtype)

    return pl.pallas_call(
        body, name="swiglu_fwd", grid=(m // tm,),
        in_specs=[pl.BlockSpec((tm, f), lambda i: (i, 0)), pl.BlockSpec((tm, f), lambda i: (i, 1))],
        out_specs=pl.BlockSpec((tm, f), lambda i: (i, 0)),
        out_shape=jax.ShapeDtypeStruct((m, f), BF16), compiler_params=_cp(),
    )(gu, gu)


def swiglu_bwd(gu, da, *, tm=256):
    m, f2 = gu.shape
    f = f2 // 2

    def body(gu_ref, da_ref, o_ref):
        g = gu_ref[:, :f]
        u = gu_ref[:, f:]
        da_v = da_ref[...]
        sg = _sigmoid(g)
        o_ref[:, :f] = (da_v * u * (sg * (1.0 + g * (1.0 - sg)))).astype(o_ref.dtype)
        o_ref[:, f:] = (da_v * (g * sg)).astype(o_ref.dtype)

    return pl.pallas_call(
        body, name="swiglu_bwd", grid=(m // tm,),
        in_specs=[pl.BlockSpec((tm, f2), lambda i: (i, 0)), pl.BlockSpec((tm, f), lambda i: (i, 0))],
        out_specs=pl.BlockSpec((tm, f2), lambda i: (i, 0)),
        out_shape=jax.ShapeDtypeStruct((m, f2), BF16), compiler_params=_cp(),
    )(gu, da)


def loss_head(y, target, *, tm=512):
    m, d = y.shape

    def body(y_ref, t_ref, dy_ref, l_ref):
        err = y_ref[...] - t_ref[...]
        dy_ref[...] = err * (1.0 / d)

        @pl.when(pl.program_id(0) == 0)
        def _():
            l_ref[...] = jnp.zeros_like(l_ref)

        part = jnp.sum(jnp.sum(err * err, axis=-1, keepdims=True), axis=0, keepdims=True)
        l_ref[...] += jnp.broadcast_to(part * (0.5 / d), l_ref.shape)

    row = pl.BlockSpec((tm, d), lambda i: (i, 0))
    return pl.pallas_call(
        body, name="loss_head", grid=(m // tm,),
        in_specs=[row, row],
        out_specs=[row, pl.BlockSpec((1, LANES), lambda i: (0, 0))],
        out_shape=[jax.ShapeDtypeStruct((m, d), F32), jax.ShapeDtypeStruct((1, LANES), F32)],
        compiler_params=_cp(),
    )(y, target)


def _tile(n, cap):
    if n <= cap:
        return n
    best = None
    for t in range(LANES, cap + 1, LANES):
        if n % t == 0:
            best = t
    assert best is not None, (n, cap)
    return best


def mm(a, b, *, trans_b, out_dtype, res=None, tm=1024, tn_cap=1536, name):
    m, k = a.shape
    n = b.shape[0] if trans_b else b.shape[1]
    assert (b.shape[1] if trans_b else b.shape[0]) == k
    tm = min(tm, m)
    tn = _tile(n, tn_cap)
    has_res = res is not None

    def body(*refs):
        if has_res:
            a_ref, b_ref, r_ref, o_ref = refs
        else:
            a_ref, b_ref, o_ref = refs
        acc = _dot_nt(a_ref[...], b_ref[...]) if trans_b else _dot_nn(a_ref[...], b_ref[...])
        if has_res:
            acc = acc + r_ref[...]
        o_ref[...] = acc.astype(o_ref.dtype)

    b_spec = pl.BlockSpec((tn, k), lambda i, j: (j, 0)) if trans_b else pl.BlockSpec((k, tn), lambda i, j: (0, j))
    o_spec = pl.BlockSpec((tm, tn), lambda i, j: (i, j))
    return pl.pallas_call(
        body, name=name, grid=(m // tm, n // tn),
        in_specs=[pl.BlockSpec((tm, k), lambda i, j: (i, 0)), b_spec] + ([o_spec] if has_res else []),
        out_specs=o_spec,
        out_shape=jax.ShapeDtypeStruct((m, n), out_dtype), compiler_params=_cp(),
    )(*([a, b] + ([res] if has_res else [])))


def mm_tn(a, b, *, name, ta_cap=1536, tb_cap=1024, tk=1024):
    m, ka = a.shape
    nb = b.shape[1]
    assert b.shape[0] == m
    tk = min(tk, m)
    ta = _tile(ka, ta_cap)
    tb = _tile(nb, tb_cap)

    def body(a_ref, b_ref, o_ref):
        @pl.when(pl.program_id(2) == 0)
        def _():
            o_ref[...] = jnp.zeros_like(o_ref)

        o_ref[...] += _dot_tn(a_ref[...], b_ref[...])

    return pl.pallas_call(
        body, name=name, grid=(ka // ta, nb // tb, m // tk),
        in_specs=[pl.BlockSpec((tk, ta), lambda i, j, kk: (kk, i)), pl.BlockSpec((tk, tb), lambda i, j, kk: (kk, j))],
        out_specs=pl.BlockSpec((ta, tb), lambda i, j, kk: (i, j)),
        out_shape=jax.ShapeDtypeStruct((ka, nb), F32), compiler_params=_cp(),
    )(a, b)


def _whole(shape):
    return pl.BlockSpec(shape, lambda i: (0,) * len(shape), pipeline_mode=pl.Buffered(1))


def _rows(tm, n):
    return pl.BlockSpec((tm, n), lambda i: (i, 0))


def _vec(n):
    return pl.BlockSpec((1, n), lambda i: (0, 0))


def _chunks(n, cap=1408):
    size = _tile(n, cap)
    return [(s, size) for s in range(0, n, size)]


def _zero_at_first_step(*refs):
    @pl.when(pl.program_id(0) == 0)
    def _():
        for r in refs:
            r[...] = jnp.zeros_like(r)


def norm_proj(x, g, wt, *, tm=512):
    m, d = x.shape
    n = wt.shape[0]

    def body(x_ref, g_ref, wt_ref, h_ref, u_ref):
        h = (_rms(x_ref[...])[0] * g_ref[...]).astype(BF16)
        h_ref[...] = h
        for s, sz in _chunks(n):
            u_ref[:, s:s + sz] = _dot_nt(h, wt_ref[s:s + sz, :])

    return pl.pallas_call(
        body, name="norm_proj", grid=(m // tm,),
        in_specs=[_rows(tm, d), _vec(d), _whole((n, d))],
        out_specs=[_rows(tm, d), _rows(tm, n)],
        out_shape=[jax.ShapeDtypeStruct((m, d), BF16), jax.ShapeDtypeStruct((m, n), F32)],
        compiler_params=_cp(),
    )(x, g.reshape(1, d), wt)


def _xattn_heads(q_ref, kv_ref, qg_v, kg_v, d):
    out = []
    for h in range(N_X_HEADS):
        cols = slice(h * X_HEAD_DIM, (h + 1) * X_HEAD_DIM)
        qh, rq = _rms(q_ref[:, cols])
        qn = qh * qg_v
        kn = _rms(kv_ref[:, cols])[0] * kg_v
        v = kv_ref[:, d + h * X_HEAD_DIM:d + (h + 1) * X_HEAD_DIM]
        out.append((qh, rq, qn, kn, v, _xattn_probs(qn, kn)))
    return out


def mid_fwd(mixed, x0, w_out, g_x, wq, kv, xqg, xkg, wo, g_f, *, tm=512):
    m, d = x0.shape
    n_mem = kv.shape[0]

    def body(mixed_ref, x0_ref, w_out_ref, g_x_ref, wq_ref, kv_ref, xqg_ref, xkg_ref, wo_ref, g_f_ref,
             x1_ref, h1_ref, qx_ref, o_ref, x2_ref, h2_ref):
        x1 = x0_ref[...] + _dot_nn(mixed_ref[...], w_out_ref[...])
        x1_ref[...] = x1
        h1 = (_rms(x1)[0] * g_x_ref[...]).astype(BF16)
        h1_ref[...] = h1
        qx_ref[...] = _dot_nn(h1, wq_ref[...])
        for h, (_, _, _, _, v, p) in enumerate(_xattn_heads(qx_ref, kv_ref, xqg_ref[...], xkg_ref[...], d)):
            o_ref[:, h * X_HEAD_DIM:(h + 1) * X_HEAD_DIM] = _dot_nn(p, v).astype(o_ref.dtype)
        x2 = x1 + _dot_nn(o_ref[...], wo_ref[...])
        x2_ref[...] = x2
        h2_ref[...] = (_rms(x2)[0] * g_f_ref[...]).astype(BF16)

    sq = _whole((d, d))
    f32_rows, bf_rows = jax.ShapeDtypeStruct((m, d), F32), jax.ShapeDtypeStruct((m, d), BF16)
    return pl.pallas_call(
        body, name="mid_fwd", grid=(m // tm,),
        in_specs=[_rows(tm, d), _rows(tm, d), sq, _vec(d), sq, _whole((n_mem, 2 * d)), _vec(X_HEAD_DIM), _vec(X_HEAD_DIM),
                  sq, _vec(d)],
        out_specs=[_rows(tm, d)] * 6,
        out_shape=[f32_rows, bf_rows, f32_rows, bf_rows, f32_rows, bf_rows],
        compiler_params=_cp(),
    )(mixed, x0, w_out, g_x.reshape(1, d), wq, kv, xqg.reshape(1, X_HEAD_DIM), xkg.reshape(1, X_HEAD_DIM), wo,
      g_f.reshape(1, d))


def ffn_fwd(h2, x2, wt_gu, w_down, *, tm=256):
    m, d = x2.shape
    f = w_down.shape[0]

    def body(h2_ref, x2_ref, wt_gu_ref, w_down_ref, gu_ref, a_ref, x3_ref):
        h = h2_ref[...]
        for s, sz in _chunks(2 * f):
            gu_ref[:, s:s + sz] = _dot_nt(h, wt_gu_ref[s:s + sz, :])
        for s, sz in _chunks(f):
            g = gu_ref[:, s:s + sz]
            a_ref[:, s:s + sz] = (g * _sigmoid(g) * gu_ref[:, f + s:f + s + sz]).astype(a_ref.dtype)
        x3_ref[...] = x2_ref[...] + _dot_nn(a_ref[...], w_down_ref[...])

    return pl.pallas_call(
        body, name="ffn_fwd", grid=(m // tm,),
        in_specs=[_rows(tm, d), _rows(tm, d), _whole((2 * f, d)), _whole((f, d))],
        out_specs=[_rows(tm, 2 * f), _rows(tm, f), _rows(tm, d)],
        out_shape=[jax.ShapeDtypeStruct((m, 2 * f), F32), jax.ShapeDtypeStruct((m, f), BF16),
                   jax.ShapeDtypeStruct((m, d), F32)],
        compiler_params=_cp(),
    )(h2, x2, wt_gu, w_down)


def ffn_bwd(dx3, gu, x2, g_f, w_down, wt_gu, *, tm=256):
    m, d = x2.shape
    f = w_down.shape[0]

    def body(dx3_ref, gu_ref, x2_ref, g_ref, w_down_ref, wt_gu_ref, dgu_ref, dx2_ref, dg_ref):
        _zero_at_first_step(dg_ref)
        dx3 = dx3_ref[...]
        dx3_b = dx3.astype(BF16)
        for s, sz in _chunks(f):
            da = _dot_nt(dx3_b, w_down_ref[s:s + sz, :])
            g = gu_ref[:, s:s + sz]
            u = gu_ref[:, f + s:f + s + sz]
            sg = _sigmoid(g)
            dgu_ref[:, s:s + sz] = (da * u * (sg * (1.0 + g * (1.0 - sg)))).astype(dgu_ref.dtype)
            dgu_ref[:, f + s:f + s + sz] = (da * (g * sg)).astype(dgu_ref.dtype)
        dh2 = _dot_nn(dgu_ref[...], wt_gu_ref[...])
        xh, r = _rms(x2_ref[...])
        dg_ref[...] += jnp.sum(dh2 * xh, axis=0, keepdims=True)
        dx2_ref[...] = dx3 + _rms_bwd(dh2, xh, r, g_ref[...])

    return pl.pallas_call(
        body, name="ffn_bwd", grid=(m // tm,),
        in_specs=[_rows(tm, d), _rows(tm, 2 * f), _rows(tm, d), _vec(d), _whole((f, d)), _whole((2 * f, d))],
        out_specs=[_rows(tm, 2 * f), _rows(tm, d), _vec(d)],
        out_shape=[jax.ShapeDtypeStruct((m, 2 * f), BF16), jax.ShapeDtypeStruct((m, d), F32),
                   jax.ShapeDtypeStruct((1, d), F32)],
        compiler_params=_cp(),
    )(dx3, gu, x2, g_f.reshape(1, d), w_down, wt_gu)


def mid_bwd(dx2, qx, kv, xqg, xkg, x1, g_x, wo, wq, w_out, *, tm=512):
    m, d = x1.shape
    n_mem = kv.shape[0]
    nt = m // tm

    def body(dx2_ref, qx_ref, kv_ref, xqg_ref, xkg_ref, x1_ref, g_x_ref, wo_ref, wq_ref, w_out_ref,
             dq_ref, dx1_ref, dmixed_ref, dkv_ref, dqg_ref, dkg_ref, dg_ref):
        i = pl.program_id(0)
        _zero_at_first_step(dkv_ref, dqg_ref, dkg_ref, dg_ref)
        qg_v, kg_v = xqg_ref[...], xkg_ref[...]
        dx2 = dx2_ref[...]
        do = _dot_nt(dx2, wo_ref[...])
        dqg_acc = jnp.zeros((1, X_HEAD_DIM), F32)
        for h, (qh, rq, qn, kn, v, p) in enumerate(_xattn_heads(qx_ref, kv_ref, qg_v, kg_v, d)):
            cols = slice(h * X_HEAD_DIM, (h + 1) * X_HEAD_DIM)
            vcols = slice(d + h * X_HEAD_DIM, d + (h + 1) * X_HEAD_DIM)
            do_h = do[:, cols]
            dp = _dot_nt(do_h, v)
            ds = p * (dp - jnp.sum(p * dp, axis=-1, keepdims=True))
            dkv_ref[:, vcols] += _dot_tn(p, do_h)
            dqn = _dot_nn(ds, kn) * (X_HEAD_DIM ** -0.5)
            dkv_ref[:, cols] += _dot_tn(ds, qn) * (X_HEAD_DIM ** -0.5)
            dqg_acc = dqg_acc + jnp.sum(dqn * qh, axis=0, keepdims=True)
            dq_ref[:, cols] = _rms_bwd(dqn, qh, rq, qg_v).astype(dq_ref.dtype)
        dqg_ref[...] += dqg_acc
        dh1 = _dot_nt(dq_ref[...], wq_ref[...])
        xh, r = _rms(x1_ref[...])
        dg_ref[...] += jnp.sum(dh1 * xh, axis=0, keepdims=True)
        dx1 = dx2 + _rms_bwd(dh1, xh, r, g_x_ref[...])
        dx1_ref[...] = dx1
        dmixed_ref[...] = _dot_nt(dx1, w_out_ref[...])

        @pl.when(i == nt - 1)
        def _():
            dkg_acc = jnp.zeros((1, X_HEAD_DIM), F32)
            for h in range(N_X_HEADS):
                cols = slice(h * X_HEAD_DIM, (h + 1) * X_HEAD_DIM)
                kh, rk = _rms(kv_ref[:, cols])
                dkn = dkv_ref[:, cols]
                dkg_acc = dkg_acc + jnp.sum(dkn * kh, axis=0, keepdims=True)
                dkv_ref[:, cols] = _rms_bwd(dkn, kh, rk, kg_v)
            dkg_ref[...] = dkg_acc

    sq = _whole((d, d))
    full = pl.BlockSpec((n_mem, 2 * d), lambda i: (0, 0))
    return pl.pallas_call(
        body, name="mid_bwd", grid=(nt,),
        in_specs=[_rows(tm, d), _rows(tm, d), _whole((n_mem, 2 * d)), _vec(X_HEAD_DIM), _vec(X_HEAD_DIM), _rows(tm, d),
                  _vec(d), sq, sq, sq],
        out_specs=[_rows(tm, d), _rows(tm, d), _rows(tm, d), full, _vec(X_HEAD_DIM), _vec(X_HEAD_DIM), _vec(d)],
        out_shape=[jax.ShapeDtypeStruct((m, d), BF16), jax.ShapeDtypeStruct((m, d), F32), jax.ShapeDtypeStruct((m, d), F32),
                   jax.ShapeDtypeStruct((n_mem, 2 * d), F32), jax.ShapeDtypeStruct((1, X_HEAD_DIM), F32),
                   jax.ShapeDtypeStruct((1, X_HEAD_DIM), F32), jax.ShapeDtypeStruct((1, d), F32)],
        compiler_params=_cp(),
    )(dx2, qx, kv, xqg.reshape(1, X_HEAD_DIM), xkg.reshape(1, X_HEAD_DIM), x1, g_x.reshape(1, d), wo, wq, w_out)


def in_bwd(du, wt_in, x0, g_mix, dx1, *, tm=512):
    m, d = x0.shape
    n = wt_in.shape[0]

    def body(du_ref, wt_ref, x0_ref, g_ref, dx1_ref, dx0_ref, dg_ref):
        _zero_at_first_step(dg_ref)
        dh0 = _dot_nn(du_ref[...], wt_ref[...])
        xh, r = _rms(x0_ref[...])
        dg_ref[...] += jnp.sum(dh0 * xh, axis=0, keepdims=True)
        dx0_ref[...] = dx1_ref[...] + _rms_bwd(dh0, xh, r, g_ref[...])

    return pl.pallas_call(
        body, name="in_bwd", grid=(m // tm,),
        in_specs=[_rows(tm, n), _whole((n, d)), _rows(tm, d), _vec(d), _rows(tm, d)],
        out_specs=[_rows(tm, d), _vec(d)],
        out_shape=[jax.ShapeDtypeStruct((m, d), F32), jax.ShapeDtypeStruct((1, d), F32)],
        compiler_params=_cp(),
    )(du, wt_in, x0, g_mix.reshape(1, d), dx1)


SWA_TILE = 512
SWA_SUB = SWA_TILE // BLOCK


def _swa_mask():
    rows = GROUP * BLOCK
    r = lax.broadcasted_iota(jnp.int32, (rows, 2 * BLOCK), 0)
    j = lax.broadcasted_iota(jnp.int32, (rows, 2 * BLOCK), 1)
    dist = (r & (BLOCK - 1)) + BLOCK - j
    return dist.astype(F32), (dist >= 0) & (dist < BLOCK), j >= BLOCK


def _slope_col(kv):
    return jnp.concatenate([jnp.full((BLOCK, 1), 2.0 ** -(kv * GROUP + g + 1), F32) for g in range(GROUP)], axis=0)


def _sink_col(sinks_ref, kv):
    return jnp.concatenate([jnp.full((BLOCK, 1), sinks_ref[kv * GROUP + g], F32) for g in range(GROUP)], axis=0)


def _stack_heads(ref, rows, kv):
    return jnp.concatenate(
        [ref[rows, (kv * GROUP + g) * HEAD_DIM:(kv * GROUP + g + 1) * HEAD_DIM] for g in range(GROUP)], axis=0)


def _swa_keys(cur_ref, prev_ref, b, col):
    cols = slice(col, col + HEAD_DIM)
    if b == 0:
        return jnp.concatenate([prev_ref[:, cols], cur_ref[0:BLOCK, cols]], axis=0)
    return cur_ref[(b - 1) * BLOCK:(b + 1) * BLOCK, cols]


def _swa_probs(qn, kn, bias, valid, sink):
    s = _dot_nt(qn, kn) * (HEAD_DIM ** -0.5)
    s = jnp.where(valid, s + bias, NEG)
    mx = jnp.maximum(jnp.max(s, axis=-1, keepdims=True), sink)
    e = jnp.exp(s - mx)
    es = jnp.exp(sink - mx)
    den = jnp.sum(e, axis=-1, keepdims=True) + es
    return e / den, es / den


def swa_fwd(u, qg, kg, sinks):
    t = u.shape[0]
    nt = t // SWA_TILE

    def body(sinks_ref, cur_ref, prev_ref, qg_ref, kg_ref, o_ref):
        i = pl.program_id(0)
        qg_v = qg_ref[...]
        kg_v = kg_ref[...]
        dist, window, own_block = _swa_mask()
        valid_first = window & (own_block | (i > 0))
        for kv in range(N_KV_HEADS):
            sink = _sink_col(sinks_ref, kv)
            bias = -_slope_col(kv) * dist
            for b in range(SWA_SUB):
                rows = slice(b * BLOCK, (b + 1) * BLOCK)
                valid = valid_first if b == 0 else window
                qn = _rms(_stack_heads(cur_ref, rows, kv))[0] * qg_v
                kn = _rms(_swa_keys(cur_ref, prev_ref, b, ATTN_WIDTH + kv * HEAD_DIM))[0] * kg_v
                vv = _swa_keys(cur_ref, prev_ref, b, ATTN_WIDTH + KV_WIDTH + kv * HEAD_DIM)
                p, _ = _swa_probs(qn, kn, bias, valid, sink)
                o4 = _dot_nn(p, vv)
                for g in range(GROUP):
                    h = kv * GROUP + g
                    o_ref[rows, h * HEAD_DIM:(h + 1) * HEAD_DIM] = o4[g * BLOCK:(g + 1) * BLOCK].astype(o_ref.dtype)

    vec = pl.BlockSpec((1, HEAD_DIM), lambda i: (0, 0))
    return pl.pallas_call(
        body, name="swa_fwd", grid=(nt,),
        in_specs=[
            pl.BlockSpec(memory_space=pltpu.SMEM),
            pl.BlockSpec((SWA_TILE, QKV_WIDTH), lambda i: (i, 0)),
            pl.BlockSpec((BLOCK, QKV_WIDTH), lambda i: (jnp.maximum(i * SWA_SUB - 1, 0), 0)),
            vec, vec,
        ],
        out_specs=pl.BlockSpec((SWA_TILE, ATTN_WIDTH), lambda i: (i, 0)),
        out_shape=jax.ShapeDtypeStruct((t, 2 * ATTN_WIDTH), BF16), compiler_params=_cp(),
    )(sinks, u, u, qg.reshape(1, HEAD_DIM), kg.reshape(1, HEAD_DIM))


def swa_bwd(u, dmixed, qg, kg, sinks):
    t = u.shape[0]
    nt = t // SWA_TILE
    kcol = lambda kv: slice(kv * HEAD_DIM, (kv + 1) * HEAD_DIM)
    vcol = lambda kv: slice(KV_WIDTH + kv * HEAD_DIM, KV_WIDTH + (kv + 1) * HEAD_DIM)

    def body(sinks_ref, cur_ref, prev_ref, do_ref, qg_ref, kg_ref, du_ref, dqg_ref, dkg_ref, dsk_ref, acc_ref, carry_ref):
        step = pl.program_id(0)
        i = nt - 1 - step
        qg_v = qg_ref[...]
        kg_v = kg_ref[...]

        @pl.when(step == 0)
        def _():
            carry_ref[...] = jnp.zeros_like(carry_ref)
            dqg_ref[...] = jnp.zeros_like(dqg_ref)
            dkg_ref[...] = jnp.zeros_like(dkg_ref)
            dsk_ref[...] = jnp.zeros_like(dsk_ref)

        acc_ref[0:SWA_TILE, :] = jnp.zeros((SWA_TILE, 2 * KV_WIDTH), F32)
        acc_ref[SWA_TILE:SWA_TILE + BLOCK, :] = carry_ref[...]

        lane = lax.broadcasted_iota(jnp.int32, (1, LANES), 1)
        dqg_acc = jnp.zeros((1, HEAD_DIM), F32)
        dsk_acc = jnp.zeros((1, LANES), F32)
        dist, window, own_block = _swa_mask()
        valid_first = window & (own_block | (i > 0))
        for kv in range(N_KV_HEADS):
            sink = _sink_col(sinks_ref, kv)
            bias = -_slope_col(kv) * dist
            for b in range(SWA_SUB):
                rows = slice(b * BLOCK, (b + 1) * BLOCK)
                valid = valid_first if b == 0 else window
                qh, rq = _rms(_stack_heads(cur_ref, rows, kv))
                qn = qh * qg_v
                kn = _rms(_swa_keys(cur_ref, prev_ref, b, ATTN_WIDTH + kv * HEAD_DIM))[0] * kg_v
                vv = _swa_keys(cur_ref, prev_ref, b, ATTN_WIDTH + KV_WIDTH + kv * HEAD_DIM)
                p, ps = _swa_probs(qn, kn, bias, valid, sink)
                do4 = _stack_heads(do_ref, rows, kv)
                dp = _dot_nt(do4, vv)
                delta = jnp.sum(p * dp, axis=-1, keepdims=True)
                ds = p * (dp - delta)
                dsink = -ps * delta
                for g in range(GROUP):
                    part = jnp.sum(dsink[g * BLOCK:(g + 1) * BLOCK], axis=0, keepdims=True)
                    dsk_acc = dsk_acc + jnp.where(lane == kv * GROUP + g, part, 0.0)
                dvv = _dot_tn(p, do4)
                dqn = _dot_nn(ds, kn) * (HEAD_DIM ** -0.5)
                dkn = _dot_tn(ds, qn) * (HEAD_DIM ** -0.5)
                dqg_acc = dqg_acc + jnp.sum(dqn * qh, axis=0, keepdims=True)
                dq = _rms_bwd(dqn, qh, rq, qg_v)
                for g in range(GROUP):
                    h = kv * GROUP + g
                    du_ref[rows, h * HEAD_DIM:(h + 1) * HEAD_DIM] = dq[g * BLOCK:(g + 1) * BLOCK].astype(du_ref.dtype)
                keys = slice(b * BLOCK, (b + 2) * BLOCK)
                acc_ref[keys, kcol(kv)] += dkn
                acc_ref[keys, vcol(kv)] += dvv
        dqg_ref[...] += dqg_acc
        dsk_ref[...] += dsk_acc

        own = slice(BLOCK, BLOCK + SWA_TILE)
        dkg_acc = jnp.zeros((1, HEAD_DIM), F32)
        for kv in range(N_KV_HEADS):
            kh, rk = _rms(cur_ref[:, ATTN_WIDTH + kv * HEAD_DIM:ATTN_WIDTH + (kv + 1) * HEAD_DIM])
            dkn = acc_ref[own, kcol(kv)]
            dkg_acc = dkg_acc + jnp.sum(dkn * kh, axis=0, keepdims=True)
            dk = _rms_bwd(dkn, kh, rk, kg_v)
            du_ref[:, ATTN_WIDTH + kv * HEAD_DIM:ATTN_WIDTH + (kv + 1) * HEAD_DIM] = dk.astype(du_ref.dtype)
            vc = ATTN_WIDTH + KV_WIDTH + kv * HEAD_DIM
            du_ref[:, vc:vc + HEAD_DIM] = acc_ref[own, vcol(kv)].astype(du_ref.dtype)
        dkg_ref[...] += dkg_acc
        carry_ref[...] = acc_ref[0:BLOCK, :]

    vec = pl.BlockSpec((1, HEAD_DIM), lambda s: (0, 0))
    return pl.pallas_call(
        body, name="swa_bwd", grid=(nt,),
        in_specs=[
            pl.BlockSpec(memory_space=pltpu.SMEM),
            pl.BlockSpec((SWA_TILE, QKV_WIDTH), lambda s: (nt - 1 - s, 0)),
            pl.BlockSpec((BLOCK, QKV_WIDTH), lambda s: (jnp.maximum((nt - 1 - s) * SWA_SUB - 1, 0), 0)),
            pl.BlockSpec((SWA_TILE, ATTN_WIDTH), lambda s: (nt - 1 - s, 0)),
            vec, vec,
        ],
        out_specs=[
            pl.BlockSpec((SWA_TILE, QKV_WIDTH), lambda s: (nt - 1 - s, 0)),
            vec, vec, pl.BlockSpec((1, LANES), lambda s: (0, 0)),
        ],
        out_shape=[
            jax.ShapeDtypeStruct((t, IN_COLS), BF16),
            jax.ShapeDtypeStruct((1, HEAD_DIM), F32), jax.ShapeDtypeStruct((1, HEAD_DIM), F32),
            jax.ShapeDtypeStruct((1, LANES), F32),
        ],
        scratch_shapes=[pltpu.VMEM((SWA_TILE + BLOCK, 2 * KV_WIDTH), F32), pltpu.VMEM((BLOCK, 2 * KV_WIDTH), F32)],
        compiler_params=_cp(),
    )(sinks, u, u, dmixed, qg.reshape(1, HEAD_DIM), kg.reshape(1, HEAD_DIM))


CONV_TILE = 512
CONV_CHUNK = 64
VAL0 = QKV_WIDTH
GATE0 = QKV_WIDTH + CONV_CH


def _glu(ref):
    return ref[:, VAL0:GATE0] * _sigmoid(ref[:, GATE0:GATE0 + CONV_CH])


SUBLANES = 8
CONV_BUF = CONV_HALO + CONV_TILE + SUBLANES
CONV_EXT = CONV_HALO + CONV_TILE


def _fill_shifted(sh_ref):
    for r in range(1, SUBLANES):
        sh_ref[r, 0:CONV_EXT, :] = sh_ref[0, pl.ds(r, CONV_EXT), :]


def _shifted(sh_ref, start, offset, n):
    return sh_ref[offset % SUBLANES, pl.ds(start + offset - offset % SUBLANES, n), :]


def _layernorm_stats(y):
    mu = jnp.mean(y, axis=-1, keepdims=True)
    yc = y - mu
    rstd = lax.rsqrt(jnp.mean(yc * yc, axis=-1, keepdims=True) + EPS)
    return yc * rstd, rstd


def conv_fwd(u, mixed, conv_w, conv_b, ln_g, ln_b):
    t = u.shape[0]
    nt = t // CONV_TILE
    per = CONV_TILE // CONV_HALO

    def body(cur_ref, prev_ref, mixed_ref, w_ref, b_ref, g_ref, b2_ref, o_ref, y_ref, gl_ref):
        del mixed_ref
        i = pl.program_id(0)
        gl_ref[0, 0:CONV_HALO, :] = jnp.where(i > 0, _glu(prev_ref), 0.0)
        gl_ref[0, CONV_HALO:CONV_EXT, :] = _glu(cur_ref)
        gl_ref[0, CONV_EXT:CONV_BUF, :] = jnp.zeros((SUBLANES, CONV_CH), F32)
        _fill_shifted(gl_ref)
        for c0 in range(0, CONV_TILE, CONV_CHUNK):
            acc = jnp.broadcast_to(b_ref[...], (CONV_CHUNK, CONV_CH))
            for k in range(CONV_K):
                acc = acc + w_ref[k:k + 1, :] * _shifted(gl_ref, c0, 2 + k, CONV_CHUNK)
            y_ref[c0:c0 + CONV_CHUNK, :] = acc
        yh, _ = _layernorm_stats(y_ref[...])
        yln = yh * g_ref[...] + b2_ref[...]
        o_ref[...] = (yln * _sigmoid(yln)).astype(o_ref.dtype)

    vec = pl.BlockSpec((1, CONV_CH), lambda i: (0, 0))
    return pl.pallas_call(
        body, name="conv_fwd", grid=(nt,),
        in_specs=[
            pl.BlockSpec((CONV_TILE, IN_COLS), lambda i: (i, 0)),
            pl.BlockSpec((CONV_HALO, IN_COLS), lambda i: (jnp.maximum(i * per - 1, 0), 0)),
            pl.BlockSpec(memory_space=pl.ANY),
            pl.BlockSpec((CONV_HALO, CONV_CH), lambda i: (0, 0)),
            vec, vec, vec,
        ],
        out_specs=[pl.BlockSpec((CONV_TILE, CONV_CH), lambda i: (i, 1)), pl.BlockSpec((CONV_TILE, CONV_CH), lambda i: (i, 0))],
        out_shape=[jax.ShapeDtypeStruct(mixed.shape, mixed.dtype), jax.ShapeDtypeStruct((t, CONV_CH), F32)],
        scratch_shapes=[pltpu.VMEM((SUBLANES, CONV_BUF, CONV_CH), F32)],
        input_output_aliases={2: 0}, compiler_params=_cp(),
    )(u, u, mixed, conv_w, conv_b.reshape(1, CONV_CH), ln_g.reshape(1, CONV_CH), ln_b.reshape(1, CONV_CH))


def conv_bwd(u, y, dmixed, du, conv_w, ln_g, ln_b):
    t = u.shape[0]
    nt = t // CONV_TILE
    per = CONV_TILE // CONV_HALO

    def body(cur_ref, prev_ref, y_ref, yn_ref, do_ref, don_ref, du_in_ref, w_ref, g_ref, b2_ref,
             du_ref, dw_ref, dvec_ref, gl_ref, dy_ref):
        i = pl.program_id(0)
        last = i == nt - 1
        _zero_at_first_step(dw_ref, dvec_ref)

        gl_ref[0, 0:CONV_HALO, :] = jnp.where(i > 0, _glu(prev_ref), 0.0)
        gl_ref[0, CONV_HALO:CONV_EXT, :] = _glu(cur_ref)
        gl_ref[0, CONV_EXT:CONV_BUF, :] = jnp.zeros((SUBLANES, CONV_CH), F32)
        _fill_shifted(gl_ref)

        yh, rstd = _layernorm_stats(jnp.concatenate([y_ref[...], yn_ref[...]], axis=0))
        g = g_ref[...]
        yln = yh * g + b2_ref[...]
        sg = _sigmoid(yln)
        dout = jnp.concatenate([do_ref[...], jnp.where(last, 0.0, don_ref[...])], axis=0)
        dyln = dout * (sg * (1.0 + yln * (1.0 - sg)))
        dyh = dyln * g
        dy = rstd * (dyh - jnp.mean(dyh, axis=-1, keepdims=True) - yh * jnp.mean(dyh * yh, axis=-1, keepdims=True))
        dy_ref[0, 0:CONV_EXT, :] = dy
        dy_ref[0, CONV_EXT:CONV_BUF, :] = jnp.zeros((SUBLANES, CONV_CH), F32)
        _fill_shifted(dy_ref)

        own = slice(0, CONV_TILE)
        dvec_ref[0:1, :] += jnp.sum(dy[own], axis=0, keepdims=True)
        dvec_ref[1:2, :] += jnp.sum(dyln[own] * yh[own], axis=0, keepdims=True)
        dvec_ref[2:3, :] += jnp.sum(dyln[own], axis=0, keepdims=True)
        for k in range(CONV_K):
            dw_ref[k:k + 1, :] += jnp.sum(dy[own] * _shifted(gl_ref, 0, 2 + k, CONV_TILE), axis=0, keepdims=True)

        for c0 in range(0, CONV_TILE, CONV_CHUNK):
            acc = jnp.zeros((CONV_CHUNK, CONV_CH), F32)
            for k in range(CONV_K):
                acc = acc + w_ref[k:k + 1, :] * _shifted(dy_ref, c0, CONV_K - 1 - k, CONV_CHUNK)
            rows = slice(c0, c0 + CONV_CHUNK)
            val = cur_ref[rows, VAL0:GATE0]
            sgate = _sigmoid(cur_ref[rows, GATE0:GATE0 + CONV_CH])
            du_ref[rows, VAL0:GATE0] = (acc * sgate).astype(du_ref.dtype)
            du_ref[rows, GATE0:GATE0 + CONV_CH] = (acc * val * sgate * (1.0 - sgate)).astype(du_ref.dtype)
        du_ref[:, 0:QKV_WIDTH] = du_in_ref[:, 0:QKV_WIDTH]

    vec = pl.BlockSpec((1, CONV_CH), lambda i: (0, 0))
    n_halo = t // CONV_HALO
    return pl.pallas_call(
        body, name="conv_bwd", grid=(nt,),
        in_specs=[
            pl.BlockSpec((CONV_TILE, IN_COLS), lambda i: (i, 0)),
            pl.BlockSpec((CONV_HALO, IN_COLS), lambda i: (jnp.maximum(i * per - 1, 0), 0)),
            pl.BlockSpec((CONV_TILE, CONV_CH), lambda i: (i, 0)),
            pl.BlockSpec((CONV_HALO, CONV_CH), lambda i: (jnp.minimum((i + 1) * per, n_halo - 1), 0)),
            pl.BlockSpec((CONV_TILE, CONV_CH), lambda i: (i, 1)),
            pl.BlockSpec((CONV_HALO, CONV_CH), lambda i: (jnp.minimum((i + 1) * per, n_halo - 1), 1)),
            pl.BlockSpec((CONV_TILE, IN_COLS), lambda i: (i, 0)),
            pl.BlockSpec((CONV_HALO, CONV_CH), lambda i: (0, 0)),
            vec, vec,
        ],
        out_specs=[
            pl.BlockSpec((CONV_TILE, IN_COLS), lambda i: (i, 0)),
            pl.BlockSpec((CONV_HALO, CONV_CH), lambda i: (0, 0)),
            pl.BlockSpec((8, CONV_CH), lambda i: (0, 0)),
        ],
        out_shape=[
            jax.ShapeDtypeStruct(du.shape, du.dtype),
            jax.ShapeDtypeStruct((CONV_HALO, CONV_CH), F32),
            jax.ShapeDtypeStruct((8, CONV_CH), F32),
        ],
        scratch_shapes=[pltpu.VMEM((SUBLANES, CONV_BUF, CONV_CH), F32), pltpu.VMEM((SUBLANES, CONV_BUF, CONV_CH), F32)],
        input_output_aliases={6: 0}, compiler_params=_cp(),
    )(u, u, y, y, dmixed, dmixed, du, conv_w, ln_g.reshape(1, CONV_CH), ln_b.reshape(1, CONV_CH))


XATTN_TILE = 512


def _xattn_probs(qn, kn):
    s = _dot_nt(qn, kn) * (X_HEAD_DIM ** -0.5)
    e = jnp.exp(s - jnp.max(s, axis=-1, keepdims=True))
    return e / jnp.sum(e, axis=-1, keepdims=True)


def xattn_fwd(q, kv, qg, kg):
    t, d = q.shape
    n_mem = kv.shape[0]

    def body(q_ref, kv_ref, qg_ref, kg_ref, o_ref):
        for h in range(N_X_HEADS):
            cols = slice(h * X_HEAD_DIM, (h + 1) * X_HEAD_DIM)
            qn = _rms(q_ref[:, cols])[0] * qg_ref[...]
            kn = _rms(kv_ref[:, cols])[0] * kg_ref[...]
            p = _xattn_probs(qn, kn)
            o_ref[:, cols] = _dot_nn(p, kv_ref[:, d + h * X_HEAD_DIM:d + (h + 1) * X_HEAD_DIM]).astype(o_ref.dtype)

    vec = pl.BlockSpec((1, X_HEAD_DIM), lambda i: (0, 0))
    return pl.pallas_call(
        body, name="xattn_fwd", grid=(t // XATTN_TILE,),
        in_specs=[pl.BlockSpec((XATTN_TILE, d), lambda i: (i, 0)), pl.BlockSpec((n_mem, 2 * d), lambda i: (0, 0)), vec, vec],
        out_specs=pl.BlockSpec((XATTN_TILE, d), lambda i: (i, 0)),
        out_shape=jax.ShapeDtypeStruct((t, d), BF16), compiler_params=_cp(),
    )(q, kv, qg.reshape(1, X_HEAD_DIM), kg.reshape(1, X_HEAD_DIM))


def xattn_bwd(q, kv, do, qg, kg):
    t, d = q.shape
    n_mem = kv.shape[0]
    nt = t // XATTN_TILE

    def body(q_ref, kv_ref, do_ref, qg_ref, kg_ref, dq_ref, dkv_ref, dqg_ref, dkg_ref):
        i = pl.program_id(0)

        @pl.when(i == 0)
        def _():
            dkv_ref[...] = jnp.zeros_like(dkv_ref)
            dqg_ref[...] = jnp.zeros_like(dqg_ref)
            dkg_ref[...] = jnp.zeros_like(dkg_ref)

        qg_v = qg_ref[...]
        kg_v = kg_ref[...]
        dqg_acc = jnp.zeros((1, X_HEAD_DIM), F32)
        for h in range(N_X_HEADS):
            cols = slice(h * X_HEAD_DIM, (h + 1) * X_HEAD_DIM)
            vcols = slice(d + h * X_HEAD_DIM, d + (h + 1) * X_HEAD_DIM)
            qh, rq = _rms(q_ref[:, cols])
            qn = qh * qg_v
            kn = _rms(kv_ref[:, cols])[0] * kg_v
            v = kv_ref[:, vcols]
            do_h = do_ref[:, cols]
            p = _xattn_probs(qn, kn)
            dp = _dot_nt(do_h, v)
            ds = p * (dp - jnp.sum(p * dp, axis=-1, keepdims=True))
            dkv_ref[:, vcols] += _dot_tn(p, do_h)
            dqn = _dot_nn(ds, kn) * (X_HEAD_DIM ** -0.5)
            dkv_ref[:, cols] += _dot_tn(ds, qn) * (X_HEAD_DIM ** -0.5)
            dqg_acc = dqg_acc + jnp.sum(dqn * qh, axis=0, keepdims=True)
            dq_ref[:, cols] = _rms_bwd(dqn, qh, rq, qg_v).astype(dq_ref.dtype)
        dqg_ref[...] += dqg_acc

        @pl.when(i == nt - 1)
        def _():
            dkg_acc = jnp.zeros((1, X_HEAD_DIM), F32)
            for h in range(N_X_HEADS):
                cols = slice(h * X_HEAD_DIM, (h + 1) * X_HEAD_DIM)
                kh, rk = _rms(kv_ref[:, cols])
                dkn = dkv_ref[:, cols]
                dkg_acc = dkg_acc + jnp.sum(dkn * kh, axis=0, keepdims=True)
                dkv_ref[:, cols] = _rms_bwd(dkn, kh, rk, kg_v)
            dkg_ref[...] = dkg_acc

    vec = pl.BlockSpec((1, X_HEAD_DIM), lambda i: (0, 0))
    row = pl.BlockSpec((XATTN_TILE, d), lambda i: (i, 0))
    full = pl.BlockSpec((n_mem, 2 * d), lambda i: (0, 0))
    return pl.pallas_call(
        body, name="xattn_bwd", grid=(nt,),
        in_specs=[row, full, row, vec, vec],
        out_specs=[row, full, vec, vec],
        out_shape=[
            jax.ShapeDtypeStruct((t, d), BF16), jax.ShapeDtypeStruct((n_mem, 2 * d), F32),
            jax.ShapeDtypeStruct((1, X_HEAD_DIM), F32), jax.ShapeDtypeStruct((1, X_HEAD_DIM), F32),
        ],
        compiler_params=_cp(),
    )(q, kv, do, qg.reshape(1, X_HEAD_DIM), kg.reshape(1, X_HEAD_DIM))


def adamw(w, g, m, v, *, name):
    r, c = w.shape
    tr = r
    for cand in (512, 256, 128, 64, 32, 16, 8):
        if r % cand == 0 and r > cand:
            tr = cand
            break

    def body(w_ref, g_ref, m_ref, v_ref, d_ref, nm_ref, nv_ref):
        g_v = g_ref[...]
        m2 = ADAM_B1 * m_ref[...] + (1.0 - ADAM_B1) * g_v
        v2 = ADAM_B2 * v_ref[...] + (1.0 - ADAM_B2) * jnp.square(g_v)
        m_hat = m2 / (1.0 - ADAM_B1 ** ADAM_STEP)
        v_hat = v2 / (1.0 - ADAM_B2 ** ADAM_STEP)
        d_ref[...] = -ADAM_LR * (m_hat / (jnp.sqrt(v_hat) + ADAM_EPS) + ADAM_WD * w_ref[...])
        nm_ref[...] = m2
        nv_ref[...] = v2

    spec = pl.BlockSpec((tr, c), lambda i: (i, 0))
    shape = jax.ShapeDtypeStruct((r, c), F32)
    return pl.pallas_call(
        body, name=name, grid=(r // tr,), in_specs=[spec] * 4, out_specs=[spec] * 3,
        out_shape=[shape] * 3, compiler_params=_cp(),
    )(w, g, m, v)


def _position():
    return lax.axis_index("x"), lax.axis_index("y"), lax.axis_index("c")


def all_gather(shard, *, name, in_vmem):
    r, c_ = shard.shape

    def body(x_ref, out_ref, token_ref, send_sems, recv_sems, local_sem):
        x, y, c = _position()
        me, sibling = (x, y, c), (x, y, 1 - c)
        chips = [(1 - x, y), (x, 1 - y), (1 - x, 1 - y)]
        token_ref[...] = jnp.zeros_like(token_ref)

        def rows(px, py, pc):
            return out_ref.at[4 * px + 2 * py + pc]

        def copy(k, block, to, src=None):
            return pltpu.make_async_remote_copy(
                src_ref=rows(*block) if src is None else src, dst_ref=rows(*block),
                send_sem=send_sems.at[k], recv_sem=recv_sems.at[k], device_id=to, device_id_type=MESH)

        mine = pltpu.make_async_copy(x_ref, rows(*me), local_sem)
        mine.start()
        first = [copy(0, me, sibling, src=x_ref)]
        first += [copy(1 + j, me, (*chip, c), src=x_ref) for j, chip in enumerate(chips)]
        for cp in first:
            cp.start()
        passed = [copy(4 + j, (*chip, c), sibling) for j, chip in enumerate(chips)]
        for j, chip in enumerate(chips):
            copy(1 + j, (*chip, c), me).wait_recv()
            passed[j].start()
        copy(0, sibling, me).wait_recv()
        for j, chip in enumerate(chips):
            copy(4 + j, (*chip, 1 - c), me).wait_recv()
        for cp in first + passed:
            cp.wait_send()
        mine.wait()

    space = pltpu.VMEM if in_vmem else pltpu.HBM
    return pl.pallas_call(
        body, name=name,
        out_shape=[jax.ShapeDtypeStruct((N_DEV, r, c_), shard.dtype), jax.ShapeDtypeStruct((8, LANES), F32)],
        in_specs=[pl.BlockSpec(memory_space=space)],
        out_specs=[pl.BlockSpec(memory_space=space), pl.BlockSpec(memory_space=pltpu.VMEM)],
        scratch_shapes=[pltpu.SemaphoreType.DMA((7,)), pltpu.SemaphoreType.DMA((7,)), pltpu.SemaphoreType.DMA],
        compiler_params=_cp(),
    )(shard)


def exchange_sibling(parts):
    _, r, c_ = parts.shape

    def body(p_ref, out_ref, send_sems, recv_sems):
        x, y, c = _position()

        def copy(k):
            return pltpu.make_async_remote_copy(
                src_ref=p_ref.at[2 * k + (1 - c)], dst_ref=out_ref.at[k],
                send_sem=send_sems.at[k], recv_sem=recv_sems.at[k], device_id=(x, y, 1 - c), device_id_type=MESH)

        for k in range(4):
            copy(k).start()
        for k in range(4):
            copy(k).wait_recv()
        for k in range(4):
            copy(k).wait_send()

    hbm = pl.BlockSpec(memory_space=pltpu.HBM)
    return pl.pallas_call(
        body, name="rs_exchange_sibling",
        out_shape=jax.ShapeDtypeStruct((4, r, c_), parts.dtype),
        in_specs=[hbm], out_specs=hbm,
        scratch_shapes=[pltpu.SemaphoreType.DMA((4,)), pltpu.SemaphoreType.DMA((4,))],
        compiler_params=_cp(),
    )(parts)


_HBM = pl.BlockSpec(memory_space=pltpu.HBM)
_SEM = pl.BlockSpec(memory_space=pltpu.SEMAPHORE)
_EFFECT = pltpu.SideEffectType.DATAFLOW_SIDE_EFFECTING


def split_start(src, land, plan, n, *, name):
    def body(src_ref, land_ref, send_sems, recv_sems, src_thru, land_thru, token):
        del src_thru, land_thru
        for k in range(n):
            s, d, to = plan(src_ref, land_ref, k)
            pltpu.make_async_remote_copy(src_ref=s, dst_ref=d, send_sem=send_sems.at[k], recv_sem=recv_sems.at[k],
                                         device_id=to, device_id_type=MESH).start()
        token[...] = jnp.zeros_like(token)

    return pl.pallas_call(
        body, name=name,
        out_shape=(pltpu.SemaphoreType.DMA((n,)), pltpu.SemaphoreType.DMA((n,)), pltpu.HBM(src.shape, src.dtype),
                   pltpu.HBM(land.shape, land.dtype), jax.ShapeDtypeStruct((8, LANES), F32)),
        in_specs=(_HBM, _HBM), out_specs=(_SEM, _SEM, _HBM, _HBM, pl.BlockSpec(memory_space=pltpu.VMEM)),
        input_output_aliases={0: 2, 1: 3},
        compiler_params=pltpu.CompilerParams(has_side_effects=_EFFECT),
    )(pltpu.with_memory_space_constraint(src, pltpu.HBM), pltpu.with_memory_space_constraint(land, pltpu.HBM))


def split_wait(started, after, plan, n, *, name):
    send_sems, recv_sems, src_thru, land_thru, _ = started

    def body(src_ref, land_ref, send_sems, recv_sems, after_ref, src_out, land_out):
        del after_ref, src_out, land_out
        for k in range(n):
            s, d, to = plan(src_ref, land_ref, k)
            cp = pltpu.make_async_remote_copy(src_ref=s, dst_ref=d, send_sem=send_sems.at[k], recv_sem=recv_sems.at[k],
                                              device_id=to, device_id_type=MESH)
            cp.wait_send()
            cp.wait_recv()

    return pl.pallas_call(
        body, name=name,
        out_shape=(pltpu.HBM(src_thru.shape, src_thru.dtype), pltpu.HBM(land_thru.shape, land_thru.dtype)),
        in_specs=(_HBM, _HBM, _SEM, _SEM, pl.BlockSpec(memory_space=pl.ANY)), out_specs=(_HBM, _HBM),
        input_output_aliases={0: 0, 1: 1},
        compiler_params=pltpu.CompilerParams(has_side_effects=_EFFECT),
    )(src_thru, land_thru, send_sems, recv_sems, after)


def _gather_plan(src_ref, land_ref, k):
    x, y, c = _position()
    bits = k + 1
    peer = ((1 - x) if bits & 4 else x, (1 - y) if bits & 2 else y, (1 - c) if bits & 1 else c)
    return src_ref, land_ref.at[4 * x + 2 * y + c], peer


def _sibling_plan(src_ref, land_ref, k):
    x, y, c = _position()
    return src_ref.at[2 * k + (1 - c)], land_ref.at[k], (x, y, 1 - c)


def _chips_plan(src_ref, land_ref, j):
    x, y, c = _position()
    px, py = [(1 - x, y), (x, 1 - y), (1 - x, 1 - y)][j]
    return src_ref.at[2 * px + py], land_ref.at[j], (px, py, c)


def _sum_rows(r):
    return max(t for t in range(16, 513, 16) if r % t == 0)


def sum_for_chips(parts, from_sibling, c_idx):
    _, r, c_ = parts.shape
    tr = _sum_rows(r)

    def body(c_ref, p_ref, s_ref, o_ref):
        del c_ref
        o_ref[...] = (p_ref[...] + s_ref[...]).astype(o_ref.dtype)

    return pl.pallas_call(
        body, name="rs_sum_for_chips",
        grid_spec=pltpu.PrefetchScalarGridSpec(
            num_scalar_prefetch=1, grid=(4, r // tr),
            in_specs=[pl.BlockSpec((None, tr, c_), lambda k, i, c_ref: (2 * k + c_ref[0], i, 0)),
                      pl.BlockSpec((None, tr, c_), lambda k, i, c_ref: (k, i, 0))],
            out_specs=pl.BlockSpec((None, tr, c_), lambda k, i, c_ref: (k, i, 0))),
        out_shape=jax.ShapeDtypeStruct((4, r, c_), BF16), compiler_params=_cp(),
    )(c_idx, parts, from_sibling)


def exchange_chips(sums):
    _, r, c_ = sums.shape

    def body(s_ref, out_ref, send_sems, recv_sems):
        x, y, c = _position()
        chips = [(1 - x, y), (x, 1 - y), (1 - x, 1 - y)]

        def copy(j):
            px, py = chips[j]
            return pltpu.make_async_remote_copy(
                src_ref=s_ref.at[2 * px + py], dst_ref=out_ref.at[j],
                send_sem=send_sems.at[j], recv_sem=recv_sems.at[j], device_id=(px, py, c), device_id_type=MESH)

        for j in range(3):
            copy(j).start()
        for j in range(3):
            copy(j).wait_recv()
        for j in range(3):
            copy(j).wait_send()

    hbm = pl.BlockSpec(memory_space=pltpu.HBM)
    return pl.pallas_call(
        body, name="rs_exchange_chips",
        out_shape=jax.ShapeDtypeStruct((3, r, c_), sums.dtype),
        in_specs=[hbm], out_specs=hbm,
        scratch_shapes=[pltpu.SemaphoreType.DMA((3,)), pltpu.SemaphoreType.DMA((3,))],
        compiler_params=_cp(),
    )(sums)


def sum_final(parts, from_sibling, from_chips, kc_idx):
    _, r, c_ = parts.shape
    tr = _sum_rows(r)

    def body(kc_ref, p_ref, s_ref, a_ref, b_ref, d_ref, o_ref):
        del kc_ref
        o_ref[...] = (((p_ref[...] + s_ref[...]) + a_ref[...].astype(F32)) + b_ref[...].astype(F32)) + d_ref[...].astype(F32)

    def chip_spec(j):
        return pl.BlockSpec((None, tr, c_), lambda i, kc: (j, i, 0))

    return pl.pallas_call(
        body, name="rs_sum_final",
        grid_spec=pltpu.PrefetchScalarGridSpec(
            num_scalar_prefetch=1, grid=(r // tr,),
            in_specs=[pl.BlockSpec((None, tr, c_), lambda i, kc: (2 * kc[0] + kc[1], i, 0)),
                      pl.BlockSpec((None, tr, c_), lambda i, kc: (kc[0], i, 0)),
                      chip_spec(0), chip_spec(1), chip_spec(2)],
            out_specs=pl.BlockSpec((tr, c_), lambda i, kc: (i, 0))),
        out_shape=jax.ShapeDtypeStruct((r, c_), F32), compiler_params=_cp(),
    )(kc_idx, parts, from_sibling, from_chips, from_chips, from_chips)


def sum_devices(gathered):
    n, r, c_ = gathered.shape

    def body(g_ref, o_ref):
        acc = g_ref[0]
        for k in range(1, n):
            acc = acc + g_ref[k]
        o_ref[...] = acc

    return pl.pallas_call(
        body, name="sum_devices", out_shape=jax.ShapeDtypeStruct((r, c_), F32), compiler_params=_cp(),
    )(gathered)


BIG = (
    ("w_in", IN_COLS, True), ("w_out", D_MODEL, False), ("wq_x", D_MODEL, False), ("wkv_x", 2 * D_MODEL, True),
    ("wo_x", D_MODEL, False), ("w_gate_up", 2 * D_FF, True), ("w_down", D_FF, False),
)
SHARD_ROWS = sum(rows // N_DEV for _, rows, _ in BIG)

SMALL = ("norm_mix_g", "q_norm_g", "k_norm_g", "sinks", "conv_b", "conv_ln_g", "conv_ln_b",
         "norm_x_g", "norm_mem_g", "xq_norm_g", "xk_norm_g", "norm_ffn_g")


def _pack_rows(vectors, width=LANES, row_multiple=8):
    flat = jnp.concatenate([v.reshape(-1) for v in vectors])
    per = width * row_multiple
    padded = -(-flat.shape[0] // per) * per
    return jnp.pad(flat, (0, padded - flat.shape[0])).reshape(-1, width)


def _unpack_rows(packed, shapes):
    flat = packed.reshape(-1)
    out, at = [], 0
    for s in shapes:
        n = 1
        for dim in s:
            n *= dim
        out.append(flat[at:at + n].reshape(s))
        at += n
    return out


def _layer_fwd(x0, mem, w, s):
    h0, u = norm_proj(x0, s["norm_mix_g"], w["w_in"])
    mixed = swa_fwd(u, s["q_norm_g"], s["k_norm_g"], s["sinks"])
    mixed, conv_y = conv_fwd(u, mixed, s["conv_w"], s["conv_b"], s["conv_ln_g"], s["conv_ln_b"])
    memn = rms_fwd(mem, s["norm_mem_g"])
    kv = mm(memn, w["wkv_x"], trans_b=True, out_dtype=F32, name="mm_kv")
    x1, h1, qx, o, x2, h2 = mid_fwd(mixed, x0, w["w_out"], s["norm_x_g"], w["wq_x"], kv, s["xq_norm_g"], s["xk_norm_g"],
                                    w["wo_x"], s["norm_ffn_g"])
    gu, a, x3 = ffn_fwd(h2, x2, w["w_gate_up"], w["w_down"])
    saved = dict(x0=x0, h0=h0, u=u, conv_y=conv_y, mixed=mixed, x1=x1, h1=h1, qx=qx, memn=memn, kv=kv, o=o, x2=x2, h2=h2,
                 gu=gu, a=a)
    return x3, saved


def _ordered_after(a, token):
    return a if token is None else a + token[0, 0]


def _layer_bwd(dx3, mem, w, s, sv, token, stage_done):
    gs = {}
    dgu, dx2, dg = ffn_bwd(dx3, sv["gu"], sv["x2"], _ordered_after(s["norm_ffn_g"], token), w["w_down"], w["w_gate_up"])
    gs["norm_ffn_g"] = dg
    gb = {"w_down": mm_tn(sv["a"], dx3, name="mm_dw_down")}
    gb["w_gate_up"] = mm_tn(dgu, sv["h2"], name="mm_dw_gate_up")
    token = stage_done("ffn", gb, gb["w_gate_up"])

    gb = {}
    dq, dx1, dmixed, dkv, dqg, dkg, dg = mid_bwd(dx2, sv["qx"], sv["kv"], s["xq_norm_g"], s["xk_norm_g"], sv["x1"],
                                                 _ordered_after(s["norm_x_g"], token), w["wo_x"], w["wq_x"], w["w_out"])
    gs["xq_norm_g"], gs["xk_norm_g"], gs["norm_x_g"] = dqg, dkg, dg
    gb["wo_x"] = mm_tn(sv["o"], dx2, name="mm_dwo")
    gb["wq_x"] = mm_tn(sv["h1"], dq, name="mm_dwq")
    dmemn = mm(dkv, w["wkv_x"], trans_b=False, out_dtype=F32, name="mm_dmemn")
    gb["wkv_x"] = mm_tn(dkv, sv["memn"], name="mm_dwkv")
    _, dg = rms_bwd(dmemn, mem, s["norm_mem_g"], None)
    gs["norm_mem_g"] = dg
    gb["w_out"] = mm_tn(sv["mixed"], dx1, name="mm_dw_out")
    token = stage_done("mid", gb, gb["w_out"])

    du, dqg, dkg, dsinks = swa_bwd(sv["u"], dmixed, _ordered_after(s["q_norm_g"], token), s["k_norm_g"], s["sinks"])
    gs["q_norm_g"], gs["k_norm_g"], gs["sinks"] = dqg, dkg, dsinks[0, :N_Q_HEADS]
    token = stage_done("attn", {}, dqg)
    du, dconv_w, dvec = conv_bwd(sv["u"], sv["conv_y"], dmixed, du, s["conv_w"], _ordered_after(s["conv_ln_g"], token),
                                 s["conv_ln_b"])
    gs["conv_w"] = dconv_w[:CONV_K]
    gs["conv_b"], gs["conv_ln_g"], gs["conv_ln_b"] = dvec[0], dvec[1], dvec[2]
    dx0, dg = in_bwd(du, w["w_in"], sv["x0"], s["norm_mix_g"], dx1)
    gs["norm_mix_g"] = dg
    token = stage_done("mix", {"w_in": mm_tn(du, sv["h0"], name="mm_dw_in")}, dx0)
    return dx0, gs, token


def _local_step(x, mem, target, weights_of, smalls, stage_done):
    saved, weights = [], []
    h = x
    for l in range(DEPTH):
        weights.append(weights_of(l, h))
        h, sv = _layer_fwd(h, mem, weights[l], smalls[l])
        saved.append(sv)
    dx, loss_part = loss_head(h, target)
    gss, token = [None] * DEPTH, None
    for l in reversed(range(DEPTH)):
        dx, gss[l], token = _layer_bwd(dx, mem, weights[l], smalls[l], saved[l], token,
                                       functools.partial(stage_done, l))
    return loss_part[0, 0], dx, gss


def kernel(x, mem, norm_mix_g, w_in, q_norm_g, k_norm_g, sinks, conv_w, conv_b, conv_ln_g, conv_ln_b, w_out, norm_x_g, norm_mem_g, wq_x, wkv_x, xq_norm_g, xk_norm_g, wo_x, norm_ffn_g, w_gate_up, w_down, loss_target, m_norm_mix_g, m_w_in, m_q_norm_g, m_k_norm_g, m_sinks, m_conv_w, m_conv_b, m_conv_ln_g, m_conv_ln_b, m_w_out, m_norm_x_g, m_norm_mem_g, m_wq_x, m_wkv_x, m_xq_norm_g, m_xk_norm_g, m_wo_x, m_norm_ffn_g, m_w_gate_up, m_w_down, v_norm_mix_g, v_w_in, v_q_norm_g, v_k_norm_g, v_sinks, v_conv_w, v_conv_b, v_conv_ln_g, v_conv_ln_b, v_w_out, v_norm_x_g, v_norm_mem_g, v_wq_x, v_wkv_x, v_xq_norm_g, v_xk_norm_g, v_wo_x, v_norm_ffn_g, v_w_gate_up, v_w_down):
    P = dict(norm_mix_g=norm_mix_g, w_in=w_in, q_norm_g=q_norm_g, k_norm_g=k_norm_g, sinks=sinks, conv_w=conv_w, conv_b=conv_b,
             conv_ln_g=conv_ln_g, conv_ln_b=conv_ln_b, w_out=w_out, norm_x_g=norm_x_g, norm_mem_g=norm_mem_g, wq_x=wq_x,
             wkv_x=wkv_x, xq_norm_g=xq_norm_g, xk_norm_g=xk_norm_g, wo_x=wo_x, norm_ffn_g=norm_ffn_g, w_gate_up=w_gate_up,
             w_down=w_down)
    M = dict(norm_mix_g=m_norm_mix_g, w_in=m_w_in, q_norm_g=m_q_norm_g, k_norm_g=m_k_norm_g, sinks=m_sinks, conv_w=m_conv_w,
             conv_b=m_conv_b, conv_ln_g=m_conv_ln_g, conv_ln_b=m_conv_ln_b, w_out=m_w_out, norm_x_g=m_norm_x_g,
             norm_mem_g=m_norm_mem_g, wq_x=m_wq_x, wkv_x=m_wkv_x, xq_norm_g=m_xq_norm_g, xk_norm_g=m_xk_norm_g, wo_x=m_wo_x,
             norm_ffn_g=m_norm_ffn_g, w_gate_up=m_w_gate_up, w_down=m_w_down)
    V = dict(norm_mix_g=v_norm_mix_g, w_in=v_w_in, q_norm_g=v_q_norm_g, k_norm_g=v_k_norm_g, sinks=v_sinks, conv_w=v_conv_w,
             conv_b=v_conv_b, conv_ln_g=v_conv_ln_g, conv_ln_b=v_conv_ln_b, w_out=v_w_out, norm_x_g=v_norm_x_g,
             norm_mem_g=v_norm_mem_g, wq_x=v_wq_x, wkv_x=v_wkv_x, xq_norm_g=v_xq_norm_g, xk_norm_g=v_xk_norm_g, wo_x=v_wo_x,
             norm_ffn_g=v_norm_ffn_g, w_gate_up=v_w_gate_up, w_down=v_w_down)
    order = ["norm_mix_g", "w_in", "q_norm_g", "k_norm_g", "sinks", "conv_w", "conv_b", "conv_ln_g", "conv_ln_b", "w_out",
             "norm_x_g", "norm_mem_g", "wq_x", "wkv_x", "xq_norm_g", "xk_norm_g", "wo_x", "norm_ffn_g", "w_gate_up", "w_down"]
    xi, yi, ci = _position()
    dev = 4 * xi + 2 * yi + ci
    x2d, mem2d, tgt2d = x[0], mem[0], loss_target[0]

    def travelling(name, l, transposed):
        a = P[name][l]
        return (a.T if transposed else a).astype(BF16)

    def unpack(gathered):
        at, wl = 0, {}
        for n, rows, _ in BIG:
            wl[n] = gathered[:, at:at + rows // N_DEV, :].reshape(rows, D_MODEL)
            at += rows // N_DEV
        return wl

    cw = jnp.pad(conv_w.reshape(DEPTH * CONV_K, CONV_CH // N_DEV), ((0, 2), (0, LANES - CONV_CH // N_DEV)))
    cw_all, cw_token = all_gather(cw, name="ag_conv_w", in_vmem=True)
    packed = [jnp.concatenate([travelling(n, l, tr) for n, _, tr in BIG], axis=0) for l in range(DEPTH)]
    gathered0, token0 = all_gather(_ordered_after(packed[0], cw_token.astype(BF16)), name="ag_weights0", in_vmem=False)
    weights0 = unpack(gathered0)
    packed1 = _ordered_after(packed[1], token0.astype(BF16))
    land = lax.dynamic_update_slice(lax.empty((N_DEV, SHARD_ROWS, D_MODEL), BF16), packed1[None], (dev, 0, 0))
    gather1 = split_start(packed1, land, _gather_plan, N_DEV - 1, name="ag_weights1_start")

    def weights_of(l, marker):
        if l == 0:
            return weights0
        return unpack(split_wait(gather1, marker, _gather_plan, N_DEV - 1, name="ag_weights1_wait")[1])

    cw_full = cw_all[:, :DEPTH * CONV_K, :CONV_CH // N_DEV].reshape(N_DEV, DEPTH, CONV_K, CONV_CH // N_DEV)
    cw_full = jnp.transpose(cw_full, (1, 2, 0, 3)).reshape(DEPTH, CONV_K, CONV_CH)
    smalls = []
    for l in range(DEPTH):
        sl = {n: P[n][l] for n in SMALL}
        sl["conv_w"] = jnp.pad(cw_full[l], ((0, CONV_HALO - CONV_K), (0, 0)))
        smalls.append(sl)
    smalls[0]["norm_mix_g"] = _ordered_after(smalls[0]["norm_mix_g"], gather1[4])

    c_idx = jnp.reshape(ci, (1,)).astype(jnp.int32)
    kc_idx = jnp.stack([2 * xi + yi, ci]).astype(jnp.int32)
    groups = {1: [n for n, _, _ in BIG], "ffn": ["w_gate_up", "w_down"], "mid": ["w_out", "wq_x", "wkv_x", "wo_x"],
              "in": ["w_in"]}
    group_of = {n: key for key in ("ffn", "mid", "in") for n in groups[key]}
    rows_of = {n: rows for n, rows, _ in BIG}
    got, flight, reduced = {}, {}, {}

    def as_parts(names, gb):
        return jnp.concatenate([gb[n].reshape(N_DEV, rows_of[n] // N_DEV, D_MODEL) for n in names], axis=1)

    def land_like(parts, blocks, dtype):
        return lax.empty((blocks,) + parts.shape[1:], dtype)

    def to_sibling(key, parts):
        flight[key] = split_start(parts, land_like(parts, 4, F32), _sibling_plan, 4, name=f"rs_sibling_{key}_start")
        return flight[key][4]

    def to_chips(key, marker):
        parts, from_sibling = split_wait(flight[key], marker, _sibling_plan, 4, name=f"rs_sibling_{key}_wait")
        chip_sums = sum_for_chips(parts, from_sibling, c_idx)
        started = split_start(chip_sums, land_like(parts, 3, BF16), _chips_plan, 3, name=f"rs_chips_{key}_start")
        flight[key] = (parts, from_sibling, started)
        return started[4]

    def finish(key, marker):
        parts, from_sibling, started = flight[key]
        _, from_chips = split_wait(started, marker, _chips_plan, 3, name=f"rs_chips_{key}_wait")
        reduced[key] = sum_final(parts, from_sibling, from_chips, kc_idx)

    def stage_done(l, stage, gb, marker):
        got.update({(l, n): g for n, g in gb.items()})
        if l == 1:
            if stage != "mix":
                return None
            return to_sibling(1, as_parts(groups[1], {n: got[(1, n)] for n in groups[1]}))
        if stage == "ffn":
            t1 = to_chips(1, marker)
            t2 = to_sibling("ffn", as_parts(groups["ffn"], gb))
            return t1 + t2
        if stage == "mid":
            t1 = to_chips("ffn", marker)
            t2 = to_sibling("mid", as_parts(groups["mid"], gb))
            return t1 + t2
        if stage == "attn":
            return to_chips("mid", marker)
        finish(1, marker)
        finish("ffn", marker)
        finish("mid", marker)
        parts = as_parts(groups["in"], gb)
        from_sibling = exchange_sibling(parts)
        from_chips = exchange_chips(sum_for_chips(parts, from_sibling, c_idx))
        reduced["in"] = sum_final(parts, from_sibling, from_chips, kc_idx)
        return None

    loss_part, grad_x, gss = _local_step(x2d, mem2d, tgt2d, weights_of, smalls, stage_done)
    loss = lax.psum(loss_part, ("x", "y", "c"))

    grads = {}
    for n, rows, transposed in BIG:
        per_layer = []
        for key in (group_of[n], 1):
            at = sum(rows_of[m] // N_DEV for m in groups[key][:groups[key].index(n)])
            g = reduced[key][at:at + rows // N_DEV]
            per_layer.append(g.T if transposed else g)
        grads[n] = jnp.stack(per_layer)

    small_names = SMALL + ("conv_w",)
    small_shapes = [(DEPTH,) + ((CONV_K, CONV_CH) if n == "conv_w" else P[n].shape[1:]) for n in small_names]
    small_parts = _pack_rows([jnp.stack([gss[l][n].reshape(sh[1:]) for l in range(DEPTH)])
                              for n, sh in zip(small_names, small_shapes)])
    small_sum = sum_devices(all_gather(small_parts, name="ag_small_grads", in_vmem=True)[0])
    for n, g in zip(small_names, _unpack_rows(small_sum, small_shapes)):
        if n == "conv_w":
            g = lax.dynamic_slice_in_dim(g, dev * (CONV_CH // N_DEV), CONV_CH // N_DEV, axis=2)
        grads[n] = g

    delta, new_m, new_v = {}, {}, {}
    for n, _, _ in BIG:
        shape = P[n].shape
        two_d = lambda a: a.reshape(shape[0] * shape[1], shape[2])
        d_, m_, v_ = adamw(two_d(P[n]), two_d(grads[n]), two_d(M[n]), two_d(V[n]), name="adamw_" + n)
        delta[n], new_m[n], new_v[n] = d_.reshape(shape), m_.reshape(shape), v_.reshape(shape)
    shapes = [P[n].shape for n in small_names]
    d_, m_, v_ = adamw(_pack_rows([P[n] for n in small_names]), _pack_rows([grads[n] for n in small_names]),
                       _pack_rows([M[n] for n in small_names]), _pack_rows([V[n] for n in small_names]), name="adamw_small")
    for n, dd, mm_, vv in zip(small_names, _unpack_rows(d_, shapes), _unpack_rows(m_, shapes), _unpack_rows(v_, shapes)):
        delta[n], new_m[n], new_v[n] = dd, mm_, vv

    return (loss, grad_x[None], *[grads[n] for n in order], *[delta[n] for n in order],
            *[new_m[n] for n in order], *[new_v[n] for n in order])
```

```python
import functools

import jax
import jax.numpy as jnp
from jax import lax
from jax.experimental import pallas as pl
from jax.experimental.pallas import tpu as pltpu

F32 = jnp.float32
BF16 = jnp.bfloat16

D_MODEL = 1024
HEAD_DIM = 64
N_Q_HEADS = 8
N_KV_HEADS = 2
GROUP = N_Q_HEADS // N_KV_HEADS
ATTN_WIDTH = N_Q_HEADS * HEAD_DIM
KV_WIDTH = N_KV_HEADS * HEAD_DIM
QKV_WIDTH = ATTN_WIDTH + 2 * KV_WIDTH
CONV_CH = 512
IN_COLS = QKV_WIDTH + 2 * CONV_CH
CONV_K = 31
CONV_HALO = 32
BLOCK = 128
N_X_HEADS = 4
X_HEAD_DIM = 256
D_FF = 2816
EPS = 1e-6
NEG = -1e30
DEPTH = 2
N_DEV = 8

ADAM_LR = 0.001
ADAM_B1 = 0.9
ADAM_B2 = 0.999
ADAM_EPS = 1e-08
ADAM_WD = 0.01
ADAM_STEP = 10

V7X_VMEM_LIMIT = 56 * 1024 * 1024
LANES = 128

MESH = pl.DeviceIdType.MESH


def _cp(**kw):
    return pltpu.CompilerParams(vmem_limit_bytes=V7X_VMEM_LIMIT, **kw)


def _dot(a, b, dims):
    return lax.dot_general(a.astype(BF16), b.astype(BF16), (dims, ((), ())), preferred_element_type=F32)


def _dot_nn(a, b):
    return _dot(a, b, ((1,), (0,)))


def _dot_nt(a, b):
    return _dot(a, b, ((1,), (1,)))


def _dot_tn(a, b):
    return _dot(a, b, ((0,), (0,)))


def _sigmoid(x):
    return jax.nn.sigmoid(x)


def _rms(x):
    r = lax.rsqrt(jnp.mean(x * x, axis=-1, keepdims=True) + EPS)
    return x * r, r


def _rms_bwd(dy, xhat, r, g):
    dxh = dy * g
    return r * (dxh - xhat * jnp.mean(dxh * xhat, axis=-1, keepdims=True))


def rms_fwd(x, g, *, tm=512):
    m, d = x.shape
    tm = min(tm, m)

    def body(x_ref, g_ref, o_ref):
        xh, _ = _rms(x_ref[...])
        o_ref[...] = (xh * g_ref[...]).astype(o_ref.dtype)

    return pl.pallas_call(
        body, name="rms_fwd", grid=(m // tm,),
        in_specs=[pl.BlockSpec((tm, d), lambda i: (i, 0)), pl.BlockSpec((1, d), lambda i: (0, 0))],
        out_specs=pl.BlockSpec((tm, d), lambda i: (i, 0)),
        out_shape=jax.ShapeDtypeStruct((m, d), BF16), compiler_params=_cp(),
    )(x, g.reshape(1, d))


def rms_bwd(dh, x, g, dres, *, tm=512):
    m, d = x.shape
    tm = min(tm, m)
    has_res = dres is not None

    def body(*refs):
        if has_res:
            dh_ref, x_ref, g_ref, r_ref, dx_ref, dg_ref = refs
        else:
            dh_ref, x_ref, g_ref, dx_ref, dg_ref = refs
        xh, r = _rms(x_ref[...])
        dy = dh_ref[...].astype(F32)

        @pl.when(pl.program_id(0) == 0)
        def _():
            dg_ref[...] = jnp.zeros_like(dg_ref)

        dg_ref[...] += jnp.sum(dy * xh, axis=0, keepdims=True)
        dx = _rms_bwd(dy, xh, r, g_ref[...])
        if has_res:
            dx = dx + r_ref[...]
        dx_ref[...] = dx

    row = pl.BlockSpec((tm, d), lambda i: (i, 0))
    vec = pl.BlockSpec((1, d), lambda i: (0, 0))
    ins = [dh, x, g.reshape(1, d)] + ([dres] if has_res else [])
    return pl.pallas_call(
        body, name="rms_bwd" + ("_res" if has_res else ""), grid=(m // tm,),
        in_specs=[row, row, vec] + ([row] if has_res else []),
        out_specs=[row, vec],
        out_shape=[jax.ShapeDtypeStruct((m, d), F32), jax.ShapeDtypeStruct((1, d), F32)],
        compiler_params=_cp(),
    )(*ins)


def swiglu_fwd(gu, *, tm=512):
    m, f2 = gu.shape
    f = f2 // 2

    def body(g_ref, u_ref, o_ref):
        g = g_ref[...]
        o_ref[...] = (g * _sigmoid(g) * u_ref[...]).astype(o_ref.dtype)

    return pl.pallas_call(
        body, name="swiglu_fwd", grid=(m // tm,),
        in_specs=[pl.BlockSpec((tm, f), lambda i: (i, 0)), pl.BlockSpec((tm, f), lambda i: (i, 1))],
        out_specs=pl.BlockSpec((tm, f), lambda i: (i, 0)),
        out_shape=jax.ShapeDtypeStruct((m, f), BF16), compiler_params=_cp(),
    )(gu, gu)


def swiglu_bwd(gu, da, *, tm=256):
    m, f2 = gu.shape
    f = f2 // 2

    def body(gu_ref, da_ref, o_ref):
        g = gu_ref[:, :f]
        u = gu_ref[:, f:]
        da_v = da_ref[...]
        sg = _sigmoid(g)
        o_ref[:, :f] = (da_v * u * (sg * (1.0 + g * (1.0 - sg)))).astype(o_ref.dtype)
        o_ref[:, f:] = (da_v * (g * sg)).astype(o_ref.dtype)

    return pl.pallas_call(
        body, name="swiglu_bwd", grid=(m // tm,),
        in_specs=[pl.BlockSpec((tm, f2), lambda i: (i, 0)), pl.BlockSpec((tm, f), lambda i: (i, 0))],
        out_specs=pl.BlockSpec((tm, f2), lambda i: (i, 0)),
        out_shape=jax.ShapeDtypeStruct((m, f2), BF16), compiler_params=_cp(),
    )(gu, da)


def loss_head(y, target, *, tm=512):
    m, d = y.shape

    def body(y_ref, t_ref, dy_ref, l_ref):
        err = y_ref[...] - t_ref[...]
        dy_ref[...] = err * (1.0 / d)

        @pl.when(pl.program_id(0) == 0)
        def _():
            l_ref[...] = jnp.zeros_like(l_ref)

        part = jnp.sum(jnp.sum(err * err, axis=-1, keepdims=True), axis=0, keepdims=True)
        l_ref[...] += jnp.broadcast_to(part * (0.5 / d), l_ref.shape)

    row = pl.BlockSpec((tm, d), lambda i: (i, 0))
    return pl.pallas_call(
        body, name="loss_head", grid=(m // tm,),
        in_specs=[row, row],
        out_specs=[row, pl.BlockSpec((1, LANES), lambda i: (0, 0))],
        out_shape=[jax.ShapeDtypeStruct((m, d), F32), jax.ShapeDtypeStruct((1, LANES), F32)],
        compiler_params=_cp(),
    )(y, target)


def _tile(n, cap):
    if n <= cap:
        return n
    best = None
    for t in range(LANES, cap + 1, LANES):
        if n % t == 0:
            best = t
    assert best is not None, (n, cap)
    return best


def mm(a, b, *, trans_b, out_dtype, res=None, tm=1024, tn_cap=1536, name):
    m, k = a.shape
    n = b.shape[0] if trans_b else b.shape[1]
    assert (b.shape[1] if trans_b else b.shape[0]) == k
    tm = min(tm, m)
    tn = _tile(n, tn_cap)
    has_res = res is not None

    def body(*refs):
        if has_res:
            a_ref, b_ref, r_ref, o_ref = refs
        else:
            a_ref, b_ref, o_ref = refs
        acc = _dot_nt(a_ref[...], b_ref[...]) if trans_b else _dot_nn(a_ref[...], b_ref[...])
        if has_res:
            acc = acc + r_ref[...]
        o_ref[...] = acc.astype(o_ref.dtype)

    b_spec = pl.BlockSpec((tn, k), lambda i, j: (j, 0)) if trans_b else pl.BlockSpec((k, tn), lambda i, j: (0, j))
    o_spec = pl.BlockSpec((tm, tn), lambda i, j: (i, j))
    return pl.pallas_call(
        body, name=name, grid=(m // tm, n // tn),
        in_specs=[pl.BlockSpec((tm, k), lambda i, j: (i, 0)), b_spec] + ([o_spec] if has_res else []),
        out_specs=o_spec,
        out_shape=jax.ShapeDtypeStruct((m, n), out_dtype), compiler_params=_cp(),
    )(*([a, b] + ([res] if has_res else [])))


def mm_tn(a, b, *, name, ta_cap=1536, tb_cap=1024, tk=1024):
    m, ka = a.shape
    nb = b.shape[1]
    assert b.shape[0] == m
    tk = min(tk, m)
    ta = _tile(ka, ta_cap)
    tb = _tile(nb, tb_cap)

    def body(a_ref, b_ref, o_ref):
        @pl.when(pl.program_id(2) == 0)
        def _():
            o_ref[...] = jnp.zeros_like(o_ref)

        o_ref[...] += _dot_tn(a_ref[...], b_ref[...])

    return pl.pallas_call(
        body, name=name, grid=(ka // ta, nb // tb, m // tk),
        in_specs=[pl.BlockSpec((tk, ta), lambda i, j, kk: (kk, i)), pl.BlockSpec((tk, tb), lambda i, j, kk: (kk, j))],
        out_specs=pl.BlockSpec((ta, tb), lambda i, j, kk: (i, j)),
        out_shape=jax.ShapeDtypeStruct((ka, nb), F32), compiler_params=_cp(),
    )(a, b)


def _whole(shape):
    return pl.BlockSpec(shape, lambda i: (0,) * len(shape), pipeline_mode=pl.Buffered(1))


def _rows(tm, n):
    return pl.BlockSpec((tm, n), lambda i: (i, 0))


def _vec(n):
    return pl.BlockSpec((1, n), lambda i: (0, 0))


def _chunks(n, cap=1408):
    size = _tile(n, cap)
    return [(s, size) for s in range(0, n, size)]


def _zero_at_first_step(*refs):
    @pl.when(pl.program_id(0) == 0)
    def _():
        for r in refs:
            r[...] = jnp.zeros_like(r)


def norm_proj(x, g, wt, *, tm=512):
    m, d = x.shape
    n = wt.shape[0]

    def body(x_ref, g_ref, wt_ref, h_ref, u_ref):
        h = (_rms(x_ref[...])[0] * g_ref[...]).astype(BF16)
        h_ref[...] = h
        for s, sz in _chunks(n):
            u_ref[:, s:s + sz] = _dot_nt(h, wt_ref[s:s + sz, :])

    return pl.pallas_call(
        body, name="norm_proj", grid=(m // tm,),
        in_specs=[_rows(tm, d), _vec(d), _whole((n, d))],
        out_specs=[_rows(tm, d), _rows(tm, n)],
        out_shape=[jax.ShapeDtypeStruct((m, d), BF16), jax.ShapeDtypeStruct((m, n), F32)],
        compiler_params=_cp(),
    )(x, g.reshape(1, d), wt)


def _xattn_heads(q_ref, kv_ref, qg_v, kg_v, d):
    out = []
    for h in range(N_X_HEADS):
        cols = slice(h * X_HEAD_DIM, (h + 1) * X_HEAD_DIM)
        qh, rq = _rms(q_ref[:, cols])
        qn = qh * qg_v
        kn = _rms(kv_ref[:, cols])[0] * kg_v
        v = kv_ref[:, d + h * X_HEAD_DIM:d + (h + 1) * X_HEAD_DIM]
        out.append((qh, rq, qn, kn, v, _xattn_probs(qn, kn)))
    return out


def mid_fwd(mixed, x0, w_out, g_x, wq, kv, xqg, xkg, wo, g_f, *, tm=512):
    m, d = x0.shape
    n_mem = kv.shape[0]

    def body(mixed_ref, x0_ref, w_out_ref, g_x_ref, wq_ref, kv_ref, xqg_ref, xkg_ref, wo_ref, g_f_ref,
             x1_ref, h1_ref, qx_ref, o_ref, x2_ref, h2_ref):
        x1 = x0_ref[...] + _dot_nn(mixed_ref[...], w_out_ref[...])
        x1_ref[...] = x1
        h1 = (_rms(x1)[0] * g_x_ref[...]).astype(BF16)
        h1_ref[...] = h1
        qx_ref[...] = _dot_nn(h1, wq_ref[...])
        for h, (_, _, _, _, v, p) in enumerate(_xattn_heads(qx_ref, kv_ref, xqg_ref[...], xkg_ref[...], d)):
            o_ref[:, h * X_HEAD_DIM:(h + 1) * X_HEAD_DIM] = _dot_nn(p, v).astype(o_ref.dtype)
        x2 = x1 + _dot_nn(o_ref[...], wo_ref[...])
        x2_ref[...] = x2
        h2_ref[...] = (_rms(x2)[0] * g_f_ref[...]).astype(BF16)

    sq = _whole((d, d))
    f32_rows, bf_rows = jax.ShapeDtypeStruct((m, d), F32), jax.ShapeDtypeStruct((m, d), BF16)
    return pl.pallas_call(
        body, name="mid_fwd", grid=(m // tm,),
        in_specs=[_rows(tm, d), _rows(tm, d), sq, _vec(d), sq, _whole((n_mem, 2 * d)), _vec(X_HEAD_DIM), _vec(X_HEAD_DIM),
                  sq, _vec(d)],
        out_specs=[_rows(tm, d)] * 6,
        out_shape=[f32_rows, bf_rows, f32_rows, bf_rows, f32_rows, bf_rows],
        compiler_params=_cp(),
    )(mixed, x0, w_out, g_x.reshape(1, d), wq, kv, xqg.reshape(1, X_HEAD_DIM), xkg.reshape(1, X_HEAD_DIM), wo,
      g_f.reshape(1, d))


def ffn_fwd(h2, x2, wt_gu, w_down, *, tm=256):
    m, d = x2.shape
    f = w_down.shape[0]

    def body(h2_ref, x2_ref, wt_gu_ref, w_down_ref, gu_ref, a_ref, x3_ref):
        h = h2_ref[...]
        for s, sz in _chunks(2 * f):
            gu_ref[:, s:s + sz] = _dot_nt(h, wt_gu_ref[s:s + sz, :])
        for s, sz in _chunks(f):
            g = gu_ref[:, s:s + sz]
            a_ref[:, s:s + sz] = (g * _sigmoid(g) * gu_ref[:, f + s:f + s + sz]).astype(a_ref.dtype)
        x3_ref[...] = x2_ref[...] + _dot_nn(a_ref[...], w_down_ref[...])

    return pl.pallas_call(
        body, name="ffn_fwd", grid=(m // tm,),
        in_specs=[_rows(tm, d), _rows(tm, d), _whole((2 * f, d)), _whole((f, d))],
        out_specs=[_rows(tm, 2 * f), _rows(tm, f), _rows(tm, d)],
        out_shape=[jax.ShapeDtypeStruct((m, 2 * f), F32), jax.ShapeDtypeStruct((m, f), BF16),
                   jax.ShapeDtypeStruct((m, d), F32)],
        compiler_params=_cp(),
    )(h2, x2, wt_gu, w_down)


def ffn_bwd(dx3, gu, x2, g_f, w_down, wt_gu, *, tm=256):
    m, d = x2.shape
    f = w_down.shape[0]

    def body(dx3_ref, gu_ref, x2_ref, g_ref, w_down_ref, wt_gu_ref, dgu_ref, dx2_ref, dg_ref):
        _zero_at_first_step(dg_ref)
        dx3 = dx3_ref[...]
        dx3_b = dx3.astype(BF16)
        for s, sz in _chunks(f):
            da = _dot_nt(dx3_b, w_down_ref[s:s + sz, :])
            g = gu_ref[:, s:s + sz]
            u = gu_ref[:, f + s:f + s + sz]
            sg = _sigmoid(g)
            dgu_ref[:, s:s + sz] = (da * u * (sg * (1.0 + g * (1.0 - sg)))).astype(dgu_ref.dtype)
            dgu_ref[:, f + s:f + s + sz] = (da * (g * sg)).astype(dgu_ref.dtype)
        dh2 = _dot_nn(dgu_ref[...], wt_gu_ref[...])
        xh, r = _rms(x2_ref[...])
        dg_ref[...] += jnp.sum(dh2 * xh, axis=0, keepdims=True)
        dx2_ref[...] = dx3 + _rms_bwd(dh2, xh, r, g_ref[...])

    return pl.pallas_call(
        body, name="ffn_bwd", grid=(m // tm,),
        in_specs=[_rows(tm, d), _rows(tm, 2 * f), _rows(tm, d), _vec(d), _whole((f, d)), _whole((2 * f, d))],
        out_specs=[_rows(tm, 2 * f), _rows(tm, d), _vec(d)],
        out_shape=[jax.ShapeDtypeStruct((m, 2 * f), BF16), jax.ShapeDtypeStruct((m, d), F32),
                   jax.ShapeDtypeStruct((1, d), F32)],
        compiler_params=_cp(),
    )(dx3, gu, x2, g_f.reshape(1, d), w_down, wt_gu)


def mid_bwd(dx2, qx, kv, xqg, xkg, x1, g_x, wo, wq, w_out, *, tm=512):
    m, d = x1.shape
    n_mem = kv.shape[0]
    nt = m // tm

    def body(dx2_ref, qx_ref, kv_ref, xqg_ref, xkg_ref, x1_ref, g_x_ref, wo_ref, wq_ref, w_out_ref,
             dq_ref, dx1_ref, dmixed_ref, dkv_ref, dqg_ref, dkg_ref, dg_ref):
        i = pl.program_id(0)
        _zero_at_first_step(dkv_ref, dqg_ref, dkg_ref, dg_ref)
        qg_v, kg_v = xqg_ref[...], xkg_ref[...]
        dx2 = dx2_ref[...]
        do = _dot_nt(dx2, wo_ref[...])
        dqg_acc = jnp.zeros((1, X_HEAD_DIM), F32)
        for h, (qh, rq, qn, kn, v, p) in enumerate(_xattn_heads(qx_ref, kv_ref, qg_v, kg_v, d)):
            cols = slice(h * X_HEAD_DIM, (h + 1) * X_HEAD_DIM)
            vcols = slice(d + h * X_HEAD_DIM, d + (h + 1) * X_HEAD_DIM)
            do_h = do[:, cols]
            dp = _dot_nt(do_h, v)
            ds = p * (dp - jnp.sum(p * dp, axis=-1, keepdims=True))
            dkv_ref[:, vcols] += _dot_tn(p, do_h)
            dqn = _dot_nn(ds, kn) * (X_HEAD_DIM ** -0.5)
            dkv_ref[:, cols] += _dot_tn(ds, qn) * (X_HEAD_DIM ** -0.5)
            dqg_acc = dqg_acc + jnp.sum(dqn * qh, axis=0, keepdims=True)
            dq_ref[:, cols] = _rms_bwd(dqn, qh, rq, qg_v).astype(dq_ref.dtype)
        dqg_ref[...] += dqg_acc
        dh1 = _dot_nt(dq_ref[...], wq_ref[...])
        xh, r = _rms(x1_ref[...])
        dg_ref[...] += jnp.sum(dh1 * xh, axis=0, keepdims=True)
        dx1 = dx2 + _rms_bwd(dh1, xh, r, g_x_ref[...])
        dx1_ref[...] = dx1
        dmixed_ref[...] = _dot_nt(dx1, w_out_ref[...])

        @pl.when(i == nt - 1)
        def _():
            dkg_acc = jnp.zeros((1, X_HEAD_DIM), F32)
            for h in range(N_X_HEADS):
                cols = slice(h * X_HEAD_DIM, (h + 1) * X_HEAD_DIM)
                kh, rk = _rms(kv_ref[:, cols])
                dkn = dkv_ref[:, cols]
                dkg_acc = dkg_acc + jnp.sum(dkn * kh, axis=0, keepdims=True)
                dkv_ref[:, cols] = _rms_bwd(dkn, kh, rk, kg_v)
            dkg_ref[...] = dkg_acc

    sq = _whole((d, d))
    full = pl.BlockSpec((n_mem, 2 * d), lambda i: (0, 0))
    return pl.pallas_call(
        body, name="mid_bwd", grid=(nt,),
        in_specs=[_rows(tm, d), _rows(tm, d), _whole((n_mem, 2 * d)), _vec(X_HEAD_DIM), _vec(X_HEAD_DIM), _rows(tm, d),
                  _vec(d), sq, sq, sq],
        out_specs=[_rows(tm, d), _rows(tm, d), _rows(tm, d), full, _vec(X_HEAD_DIM), _vec(X_HEAD_DIM), _vec(d)],
        out_shape=[jax.ShapeDtypeStruct((m, d), BF16), jax.ShapeDtypeStruct((m, d), F32), jax.ShapeDtypeStruct((m, d), F32),
                   jax.ShapeDtypeStruct((n_mem, 2 * d), F32), jax.ShapeDtypeStruct((1, X_HEAD_DIM), F32),
                   jax.ShapeDtypeStruct((1, X_HEAD_DIM), F32), jax.ShapeDtypeStruct((1, d), F32)],
        compiler_params=_cp(),
    )(dx2, qx, kv, xqg.reshape(1, X_HEAD_DIM), xkg.reshape(1, X_HEAD_DIM), x1, g_x.reshape(1, d), wo, wq, w_out)


def in_bwd(du, wt_in, x0, g_mix, dx1, *, tm=512):
    m, d = x0.shape
    n = wt_in.shape[0]

    def body(du_ref, wt_ref, x0_ref, g_ref, dx1_ref, dx0_ref, dg_ref):
        _zero_at_first_step(dg_ref)
        dh0 = _dot_nn(du_ref[...], wt_ref[...])
        xh, r = _rms(x0_ref[...])
        dg_ref[...] += jnp.sum(dh0 * xh, axis=0, keepdims=True)
        dx0_ref[...] = dx1_ref[...] + _rms_bwd(dh0, xh, r, g_ref[...])

    return pl.pallas_call(
        body, name="in_bwd", grid=(m // tm,),
        in_specs=[_rows(tm, n), _whole((n, d)), _rows(tm, d), _vec(d), _rows(tm, d)],
        out_specs=[_rows(tm, d), _vec(d)],
        out_shape=[jax.ShapeDtypeStruct((m, d), F32), jax.ShapeDtypeStruct((1, d), F32)],
        compiler_params=_cp(),
    )(du, wt_in, x0, g_mix.reshape(1, d), dx1)


SWA_TILE = 512
SWA_SUB = SWA_TILE // BLOCK


def _swa_mask():
    rows = GROUP * BLOCK
    r = lax.broadcasted_iota(jnp.int32, (rows, 2 * BLOCK), 0)
    j = lax.broadcasted_iota(jnp.int32, (rows, 2 * BLOCK), 1)
    dist = (r & (BLOCK - 1)) + BLOCK - j
    return dist.astype(F32), (dist >= 0) & (dist < BLOCK), j >= BLOCK


def _slope_col(kv):
    return jnp.concatenate([jnp.full((BLOCK, 1), 2.0 ** -(kv * GROUP + g + 1), F32) for g in range(GROUP)], axis=0)


def _sink_col(sinks_ref, kv):
    return jnp.concatenate([jnp.full((BLOCK, 1), sinks_ref[kv * GROUP + g], F32) for g in range(GROUP)], axis=0)


def _stack_heads(ref, rows, kv):
    return jnp.concatenate(
        [ref[rows, (kv * GROUP + g) * HEAD_DIM:(kv * GROUP + g + 1) * HEAD_DIM] for g in range(GROUP)], axis=0)


def _swa_keys(cur_ref, prev_ref, b, col):
    cols = slice(col, col + HEAD_DIM)
    if b == 0:
        return jnp.concatenate([prev_ref[:, cols], cur_ref[0:BLOCK, cols]], axis=0)
    return cur_ref[(b - 1) * BLOCK:(b + 1) * BLOCK, cols]


def _swa_probs(qn, kn, bias, valid, sink):
    s = _dot_nt(qn, kn) * (HEAD_DIM ** -0.5)
    s = jnp.where(valid, s + bias, NEG)
    mx = jnp.maximum(jnp.max(s, axis=-1, keepdims=True), sink)
    e = jnp.exp(s - mx)
    es = jnp.exp(sink - mx)
    den = jnp.sum(e, axis=-1, keepdims=True) + es
    return e / den, es / den


def swa_fwd(u, qg, kg, sinks):
    t = u.shape[0]
    nt = t // SWA_TILE

    def body(sinks_ref, cur_ref, prev_ref, qg_ref, kg_ref, o_ref):
        i = pl.program_id(0)
        qg_v = qg_ref[...]
        kg_v = kg_ref[...]
        dist, window, own_block = _swa_mask()
        valid_first = window & (own_block | (i > 0))
        for kv in range(N_KV_HEADS):
            sink = _sink_col(sinks_ref, kv)
            bias = -_slope_col(kv) * dist
            for b in range(SWA_SUB):
                rows = slice(b * BLOCK, (b + 1) * BLOCK)
                valid = valid_first if b == 0 else window
                qn = _rms(_stack_heads(cur_ref, rows, kv))[0] * qg_v
                kn = _rms(_swa_keys(cur_ref, prev_ref, b, ATTN_WIDTH + kv * HEAD_DIM))[0] * kg_v
                vv = _swa_keys(cur_ref, prev_ref, b, ATTN_WIDTH + KV_WIDTH + kv * HEAD_DIM)
                p, _ = _swa_probs(qn, kn, bias, valid, sink)
                o4 = _dot_nn(p, vv)
                for g in range(GROUP):
                    h = kv * GROUP + g
                    o_ref[rows, h * HEAD_DIM:(h + 1) * HEAD_DIM] = o4[g * BLOCK:(g + 1) * BLOCK].astype(o_ref.dtype)

    vec = pl.BlockSpec((1, HEAD_DIM), lambda i: (0, 0))
    return pl.pallas_call(
        body, name="swa_fwd", grid=(nt,),
        in_specs=[
            pl.BlockSpec(memory_space=pltpu.SMEM),
            pl.BlockSpec((SWA_TILE, QKV_WIDTH), lambda i: (i, 0)),
            pl.BlockSpec((BLOCK, QKV_WIDTH), lambda i: (jnp.maximum(i * SWA_SUB - 1, 0), 0)),
            vec, vec,
        ],
        out_specs=pl.BlockSpec((SWA_TILE, ATTN_WIDTH), lambda i: (i, 0)),
        out_shape=jax.ShapeDtypeStruct((t, 2 * ATTN_WIDTH), BF16), compiler_params=_cp(),
    )(sinks, u, u, qg.reshape(1, HEAD_DIM), kg.reshape(1, HEAD_DIM))


def swa_bwd(u, dmixed, qg, kg, sinks):
    t = u.shape[0]
    nt = t // SWA_TILE
    kcol = lambda kv: slice(kv * HEAD_DIM, (kv + 1) * HEAD_DIM)
    vcol = lambda kv: slice(KV_WIDTH + kv * HEAD_DIM, KV_WIDTH + (kv + 1) * HEAD_DIM)

    def body(sinks_ref, cur_ref, prev_ref, do_ref, qg_ref, kg_ref, du_ref, dqg_ref, dkg_ref, dsk_ref, acc_ref, carry_ref):
        step = pl.program_id(0)
        i = nt - 1 - step
        qg_v = qg_ref[...]
        kg_v = kg_ref[...]

        @pl.when(step == 0)
        def _():
            carry_ref[...] = jnp.zeros_like(carry_ref)
            dqg_ref[...] = jnp.zeros_like(dqg_ref)
            dkg_ref[...] = jnp.zeros_like(dkg_ref)
            dsk_ref[...] = jnp.zeros_like(dsk_ref)

        acc_ref[0:SWA_TILE, :] = jnp.zeros((SWA_TILE, 2 * KV_WIDTH), F32)
        acc_ref[SWA_TILE:SWA_TILE + BLOCK, :] = carry_ref[...]

        lane = lax.broadcasted_iota(jnp.int32, (1, LANES), 1)
        dqg_acc = jnp.zeros((1, HEAD_DIM), F32)
        dsk_acc = jnp.zeros((1, LANES), F32)
        dist, window, own_block = _swa_mask()
        valid_first = window & (own_block | (i > 0))
        for kv in range(N_KV_HEADS):
            sink = _sink_col(sinks_ref, kv)
            bias = -_slope_col(kv) * dist
            for b in range(SWA_SUB):
                rows = slice(b * BLOCK, (b + 1) * BLOCK)
                valid = valid_first if b == 0 else window
                qh, rq = _rms(_stack_heads(cur_ref, rows, kv))
                qn = qh * qg_v
                kn = _rms(_swa_keys(cur_ref, prev_ref, b, ATTN_WIDTH + kv * HEAD_DIM))[0] * kg_v
                vv = _swa_keys(cur_ref, prev_ref, b, ATTN_WIDTH + KV_WIDTH + kv * HEAD_DIM)
                p, ps = _swa_probs(qn, kn, bias, valid, sink)
                do4 = _stack_heads(do_ref, rows, kv)
                dp = _dot_nt(do4, vv)
                delta = jnp.sum(p * dp, axis=-1, keepdims=True)
                ds = p * (dp - delta)
                dsink = -ps * delta
                for g in range(GROUP):
                    part = jnp.sum(dsink[g * BLOCK:(g + 1) * BLOCK], axis=0, keepdims=True)
                    dsk_acc = dsk_acc + jnp.where(lane == kv * GROUP + g, part, 0.0)
                dvv = _dot_tn(p, do4)
                dqn = _dot_nn(ds, kn) * (HEAD_DIM ** -0.5)
                dkn = _dot_tn(ds, qn) * (HEAD_DIM ** -0.5)
                dqg_acc = dqg_acc + jnp.sum(dqn * qh, axis=0, keepdims=True)
                dq = _rms_bwd(dqn, qh, rq, qg_v)
                for g in range(GROUP):
                    h = kv * GROUP + g
                    du_ref[rows, h * HEAD_DIM:(h + 1) * HEAD_DIM] = dq[g * BLOCK:(g + 1) * BLOCK].astype(du_ref.dtype)
                keys = slice(b * BLOCK, (b + 2) * BLOCK)
                acc_ref[keys, kcol(kv)] += dkn
                acc_ref[keys, vcol(kv)] += dvv
        dqg_ref[...] += dqg_acc
        dsk_ref[...] += dsk_acc

        own = slice(BLOCK, BLOCK + SWA_TILE)
        dkg_acc = jnp.zeros((1, HEAD_DIM), F32)
        for kv in range(N_KV_HEADS):
            kh, rk = _rms(cur_ref[:, ATTN_WIDTH + kv * HEAD_DIM:ATTN_WIDTH + (kv + 1) * HEAD_DIM])
            dkn = acc_ref[own, kcol(kv)]
            dkg_acc = dkg_acc + jnp.sum(dkn * kh, axis=0, keepdims=True)
            dk = _rms_bwd(dkn, kh, rk, kg_v)
            du_ref[:, ATTN_WIDTH + kv * HEAD_DIM:ATTN_WIDTH + (kv + 1) * HEAD_DIM] = dk.astype(du_ref.dtype)
            vc = ATTN_WIDTH + KV_WIDTH + kv * HEAD_DIM
            du_ref[:, vc:vc + HEAD_DIM] = acc_ref[own, vcol(kv)].astype(du_ref.dtype)
        dkg_ref[...] += dkg_acc
        carry_ref[...] = acc_ref[0:BLOCK, :]

    vec = pl.BlockSpec((1, HEAD_DIM), lambda s: (0, 0))
    return pl.pallas_call(
        body, name="swa_bwd", grid=(nt,),
        in_specs=[
            pl.BlockSpec(memory_space=pltpu.SMEM),
            pl.BlockSpec((SWA_TILE, QKV_WIDTH), lambda s: (nt - 1 - s, 0)),
            pl.BlockSpec((BLOCK, QKV_WIDTH), lambda s: (jnp.maximum((nt - 1 - s) * SWA_SUB - 1, 0), 0)),
            pl.BlockSpec((SWA_TILE, ATTN_WIDTH), lambda s: (nt - 1 - s, 0)),
            vec, vec,
        ],
        out_specs=[
            pl.BlockSpec((SWA_TILE, QKV_WIDTH), lambda s: (nt - 1 - s, 0)),
            vec, vec, pl.BlockSpec((1, LANES), lambda s: (0, 0)),
        ],
        out_shape=[
            jax.ShapeDtypeStruct((t, IN_COLS), BF16),
            jax.ShapeDtypeStruct((1, HEAD_DIM), F32), jax.ShapeDtypeStruct((1, HEAD_DIM), F32),
            jax.ShapeDtypeStruct((1, LANES), F32),
        ],
        scratch_shapes=[pltpu.VMEM((SWA_TILE + BLOCK, 2 * KV_WIDTH), F32), pltpu.VMEM((BLOCK, 2 * KV_WIDTH), F32)],
        compiler_params=_cp(),
    )(sinks, u, u, dmixed, qg.reshape(1, HEAD_DIM), kg.reshape(1, HEAD_DIM))


CONV_TILE = 512
CONV_CHUNK = 64
VAL0 = QKV_WIDTH
GATE0 = QKV_WIDTH + CONV_CH


def _glu(ref):
    return ref[:, VAL0:GATE0] * _sigmoid(ref[:, GATE0:GATE0 + CONV_CH])


SUBLANES = 8
CONV_BUF = CONV_HALO + CONV_TILE + SUBLANES
CONV_EXT = CONV_HALO + CONV_TILE


def _fill_shifted(sh_ref):
    for r in range(1, SUBLANES):
        sh_ref[r, 0:CONV_EXT, :] = sh_ref[0, pl.ds(r, CONV_EXT), :]


def _shifted(sh_ref, start, offset, n):
    return sh_ref[offset % SUBLANES, pl.ds(start + offset - offset % SUBLANES, n), :]


def _layernorm_stats(y):
    mu = jnp.mean(y, axis=-1, keepdims=True)
    yc = y - mu
    rstd = lax.rsqrt(jnp.mean(yc * yc, axis=-1, keepdims=True) + EPS)
    return yc * rstd, rstd


def conv_fwd(u, mixed, conv_w, conv_b, ln_g, ln_b):
    t = u.shape[0]
    nt = t // CONV_TILE
    per = CONV_TILE // CONV_HALO

    def body(cur_ref, prev_ref, mixed_ref, w_ref, b_ref, g_ref, b2_ref, o_ref, y_ref, gl_ref):
        del mixed_ref
        i = pl.program_id(0)
        gl_ref[0, 0:CONV_HALO, :] = jnp.where(i > 0, _glu(prev_ref), 0.0)
        gl_ref[0, CONV_HALO:CONV_EXT, :] = _glu(cur_ref)
        gl_ref[0, CONV_EXT:CONV_BUF, :] = jnp.zeros((SUBLANES, CONV_CH), F32)
        _fill_shifted(gl_ref)
        for c0 in range(0, CONV_TILE, CONV_CHUNK):
            acc = jnp.broadcast_to(b_ref[...], (CONV_CHUNK, CONV_CH))
            for k in range(CONV_K):
                acc = acc + w_ref[k:k + 1, :] * _shifted(gl_ref, c0, 2 + k, CONV_CHUNK)
            y_ref[c0:c0 + CONV_CHUNK, :] = acc
        yh, _ = _layernorm_stats(y_ref[...])
        yln = yh * g_ref[...] + b2_ref[...]
        o_ref[...] = (yln * _sigmoid(yln)).astype(o_ref.dtype)

    vec = pl.BlockSpec((1, CONV_CH), lambda i: (0, 0))
    return pl.pallas_call(
        body, name="conv_fwd", grid=(nt,),
        in_specs=[
            pl.BlockSpec((CONV_TILE, IN_COLS), lambda i: (i, 0)),
            pl.BlockSpec((CONV_HALO, IN_COLS), lambda i: (jnp.maximum(i * per - 1, 0), 0)),
            pl.BlockSpec(memory_space=pl.ANY),
            pl.BlockSpec((CONV_HALO, CONV_CH), lambda i: (0, 0)),
            vec, vec, vec,
        ],
        out_specs=[pl.BlockSpec((CONV_TILE, CONV_CH), lambda i: (i, 1)), pl.BlockSpec((CONV_TILE, CONV_CH), lambda i: (i, 0))],
        out_shape=[jax.ShapeDtypeStruct(mixed.shape, mixed.dtype), jax.ShapeDtypeStruct((t, CONV_CH), F32)],
        scratch_shapes=[pltpu.VMEM((SUBLANES, CONV_BUF, CONV_CH), F32)],
        input_output_aliases={2: 0}, compiler_params=_cp(),
    )(u, u, mixed, conv_w, conv_b.reshape(1, CONV_CH), ln_g.reshape(1, CONV_CH), ln_b.reshape(1, CONV_CH))


def conv_bwd(u, y, dmixed, du, conv_w, ln_g, ln_b):
    t = u.shape[0]
    nt = t // CONV_TILE
    per = CONV_TILE // CONV_HALO

    def body(cur_ref, prev_ref, y_ref, yn_ref, do_ref, don_ref, du_in_ref, w_ref, g_ref, b2_ref,
             du_ref, dw_ref, dvec_ref, gl_ref, dy_ref):
        i = pl.program_id(0)
        last = i == nt - 1
        _zero_at_first_step(dw_ref, dvec_ref)

        gl_ref[0, 0:CONV_HALO, :] = jnp.where(i > 0, _glu(prev_ref), 0.0)
        gl_ref[0, CONV_HALO:CONV_EXT, :] = _glu(cur_ref)
        gl_ref[0, CONV_EXT:CONV_BUF, :] = jnp.zeros((SUBLANES, CONV_CH), F32)
        _fill_shifted(gl_ref)

        yh, rstd = _layernorm_stats(jnp.concatenate([y_ref[...], yn_ref[...]], axis=0))
        g = g_ref[...]
        yln = yh * g + b2_ref[...]
        sg = _sigmoid(yln)
        dout = jnp.concatenate([do_ref[...], jnp.where(last, 0.0, don_ref[...])], axis=0)
        dyln = dout * (sg * (1.0 + yln * (1.0 - sg)))
        dyh = dyln * g
        dy = rstd * (dyh - jnp.mean(dyh, axis=-1, keepdims=True) - yh * jnp.mean(dyh * yh, axis=-1, keepdims=True))
        dy_ref[0, 0:CONV_EXT, :] = dy
        dy_ref[0, CONV_EXT:CONV_BUF, :] = jnp.zeros((SUBLANES, CONV_CH), F32)
        _fill_shifted(dy_ref)

        own = slice(0, CONV_TILE)
        dvec_ref[0:1, :] += jnp.sum(dy[own], axis=0, keepdims=True)
        dvec_ref[1:2, :] += jnp.sum(dyln[own] * yh[own], axis=0, keepdims=True)
        dvec_ref[2:3, :] += jnp.sum(dyln[own], axis=0, keepdims=True)
        for k in range(CONV_K):
            dw_ref[k:k + 1, :] += jnp.sum(dy[own] * _shifted(gl_ref, 0, 2 + k, CONV_TILE), axis=0, keepdims=True)

        for c0 in range(0, CONV_TILE, CONV_CHUNK):
            acc = jnp.zeros((CONV_CHUNK, CONV_CH), F32)
            for k in range(CONV_K):
                acc = acc + w_ref[k:k + 1, :] * _shifted(dy_ref, c0, CONV_K - 1 - k, CONV_CHUNK)
            rows = slice(c0, c0 + CONV_CHUNK)
            val = cur_ref[rows, VAL0:GATE0]
            sgate = _sigmoid(cur_ref[rows, GATE0:GATE0 + CONV_CH])
            du_ref[rows, VAL0:GATE0] = (acc * sgate).astype(du_ref.dtype)
            du_ref[rows, GATE0:GATE0 + CONV_CH] = (acc * val * sgate * (1.0 - sgate)).astype(du_ref.dtype)
        du_ref[:, 0:QKV_WIDTH] = du_in_ref[:, 0:QKV_WIDTH]

    vec = pl.BlockSpec((1, CONV_CH), lambda i: (0, 0))
    n_halo = t // CONV_HALO
    return pl.pallas_call(
        body, name="conv_bwd", grid=(nt,),
        in_specs=[
            pl.BlockSpec((CONV_TILE, IN_COLS), lambda i: (i, 0)),
            pl.BlockSpec((CONV_HALO, IN_COLS), lambda i: (jnp.maximum(i * per - 1, 0), 0)),
            pl.BlockSpec((CONV_TILE, CONV_CH), lambda i: (i, 0)),
            pl.BlockSpec((CONV_HALO, CONV_CH), lambda i: (jnp.minimum((i + 1) * per, n_halo - 1), 0)),
            pl.BlockSpec((CONV_TILE, CONV_CH), lambda i: (i, 1)),
            pl.BlockSpec((CONV_HALO, CONV_CH), lambda i: (jnp.minimum((i + 1) * per, n_halo - 1), 1)),
            pl.BlockSpec((CONV_TILE, IN_COLS), lambda i: (i, 0)),
            pl.BlockSpec((CONV_HALO, CONV_CH), lambda i: (0, 0)),
            vec, vec,
        ],
        out_specs=[
            pl.BlockSpec((CONV_TILE, IN_COLS), lambda i: (i, 0)),
            pl.BlockSpec((CONV_HALO, CONV_CH), lambda i: (0, 0)),
            pl.BlockSpec((8, CONV_CH), lambda i: (0, 0)),
        ],
        out_shape=[
            jax.ShapeDtypeStruct(du.shape, du.dtype),
            jax.ShapeDtypeStruct((CONV_HALO, CONV_CH), F32),
            jax.ShapeDtypeStruct((8, CONV_CH), F32),
        ],
        scratch_shapes=[pltpu.VMEM((SUBLANES, CONV_BUF, CONV_CH), F32), pltpu.VMEM((SUBLANES, CONV_BUF, CONV_CH), F32)],
        input_output_aliases={6: 0}, compiler_params=_cp(),
    )(u, u, y, y, dmixed, dmixed, du, conv_w, ln_g.reshape(1, CONV_CH), ln_b.reshape(1, CONV_CH))


XATTN_TILE = 512


def _xattn_probs(qn, kn):
    s = _dot_nt(qn, kn) * (X_HEAD_DIM ** -0.5)
    e = jnp.exp(s - jnp.max(s, axis=-1, keepdims=True))
    return e / jnp.sum(e, axis=-1, keepdims=True)


def xattn_fwd(q, kv, qg, kg):
    t, d = q.shape
    n_mem = kv.shape[0]

    def body(q_ref, kv_ref, qg_ref, kg_ref, o_ref):
        for h in range(N_X_HEADS):
            cols = slice(h * X_HEAD_DIM, (h + 1) * X_HEAD_DIM)
            qn = _rms(q_ref[:, cols])[0] * qg_ref[...]
            kn = _rms(kv_ref[:, cols])[0] * kg_ref[...]
            p = _xattn_probs(qn, kn)
            o_ref[:, cols] = _dot_nn(p, kv_ref[:, d + h * X_HEAD_DIM:d + (h + 1) * X_HEAD_DIM]).astype(o_ref.dtype)

    vec = pl.BlockSpec((1, X_HEAD_DIM), lambda i: (0, 0))
    return pl.pallas_call(
        body, name="xattn_fwd", grid=(t // XATTN_TILE,),
        in_specs=[pl.BlockSpec((XATTN_TILE, d), lambda i: (i, 0)), pl.BlockSpec((n_mem, 2 * d), lambda i: (0, 0)), vec, vec],
        out_specs=pl.BlockSpec((XATTN_TILE, d), lambda i: (i, 0)),
        out_shape=jax.ShapeDtypeStruct((t, d), BF16), compiler_params=_cp(),
    )(q, kv, qg.reshape(1, X_HEAD_DIM), kg.reshape(1, X_HEAD_DIM))


def xattn_bwd(q, kv, do, qg, kg):
    t, d = q.shape
    n_mem = kv.shape[0]
    nt = t // XATTN_TILE

    def body(q_ref, kv_ref, do_ref, qg_ref, kg_ref, dq_ref, dkv_ref, dqg_ref, dkg_ref):
        i = pl.program_id(0)

        @pl.when(i == 0)
        def _():
            dkv_ref[...] = jnp.zeros_like(dkv_ref)
            dqg_ref[...] = jnp.zeros_like(dqg_ref)
            dkg_ref[...] = jnp.zeros_like(dkg_ref)

        qg_v = qg_ref[...]
        kg_v = kg_ref[...]
        dqg_acc = jnp.zeros((1, X_HEAD_DIM), F32)
        for h in range(N_X_HEADS):
            cols = slice(h * X_HEAD_DIM, (h + 1) * X_HEAD_DIM)
            vcols = slice(d + h * X_HEAD_DIM, d + (h + 1) * X_HEAD_DIM)
            qh, rq = _rms(q_ref[:, cols])
            qn = qh * qg_v
            kn = _rms(kv_ref[:, cols])[0] * kg_v
            v = kv_ref[:, vcols]
            do_h = do_ref[:, cols]
            p = _xattn_probs(qn, kn)
            dp = _dot_nt(do_h, v)
            ds = p * (dp - jnp.sum(p * dp, axis=-1, keepdims=True))
            dkv_ref[:, vcols] += _dot_tn(p, do_h)
            dqn = _dot_nn(ds, kn) * (X_HEAD_DIM ** -0.5)
            dkv_ref[:, cols] += _dot_tn(ds, qn) * (X_HEAD_DIM ** -0.5)
            dqg_acc = dqg_acc + jnp.sum(dqn * qh, axis=0, keepdims=True)
            dq_ref[:, cols] = _rms_bwd(dqn, qh, rq, qg_v).astype(dq_ref.dtype)
        dqg_ref[...] += dqg_acc

        @pl.when(i == nt - 1)
        def _():
            dkg_acc = jnp.zeros((1, X_HEAD_DIM), F32)
            for h in range(N_X_HEADS):
                cols = slice(h * X_HEAD_DIM, (h + 1) * X_HEAD_DIM)
                kh, rk = _rms(kv_ref[:, cols])
                dkn = dkv_ref[:, cols]
                dkg_acc = dkg_acc + jnp.sum(dkn * kh, axis=0, keepdims=True)
                dkv_ref[:, cols] = _rms_bwd(dkn, kh, rk, kg_v)
            dkg_ref[...] = dkg_acc

    vec = pl.BlockSpec((1, X_HEAD_DIM), lambda i: (0, 0))
    row = pl.BlockSpec((XATTN_TILE, d), lambda i: (i, 0))
    full = pl.BlockSpec((n_mem, 2 * d), lambda i: (0, 0))
    return pl.pallas_call(
        body, name="xattn_bwd", grid=(nt,),
        in_specs=[row, full, row, vec, vec],
        out_specs=[row, full, vec, vec],
        out_shape=[
            jax.ShapeDtypeStruct((t, d), BF16), jax.ShapeDtypeStruct((n_mem, 2 * d), F32),
            jax.ShapeDtypeStruct((1, X_HEAD_DIM), F32), jax.ShapeDtypeStruct((1, X_HEAD_DIM), F32),
        ],
        compiler_params=_cp(),
    )(q, kv, do, qg.reshape(1, X_HEAD_DIM), kg.reshape(1, X_HEAD_DIM))


def adamw(w, g, m, v, *, name):
    r, c = w.shape
    tr = r
    for cand in (512, 256, 128, 64, 32, 16, 8):
        if r % cand == 0 and r > cand:
            tr = cand
            break

    def body(w_ref, g_ref, m_ref, v_ref, d_ref, nm_ref, nv_ref):
        g_v = g_ref[...]
        m2 = ADAM_B1 * m_ref[...] + (1.0 - ADAM_B1) * g_v
        v2 = ADAM_B2 * v_ref[...] + (1.0 - ADAM_B2) * jnp.square(g_v)
        m_hat = m2 / (1.0 - ADAM_B1 ** ADAM_STEP)
        v_hat = v2 / (1.0 - ADAM_B2 ** ADAM_STEP)
        d_ref[...] = -ADAM_LR * (m_hat / (jnp.sqrt(v_hat) + ADAM_EPS) + ADAM_WD * w_ref[...])
        nm_ref[...] = m2
        nv_ref[...] = v2

    spec = pl.BlockSpec((tr, c), lambda i: (i, 0))
    shape = jax.ShapeDtypeStruct((r, c), F32)
    return pl.pallas_call(
        body, name=name, grid=(r // tr,), in_specs=[spec] * 4, out_specs=[spec] * 3,
        out_shape=[shape] * 3, compiler_params=_cp(),
    )(w, g, m, v)


def _position():
    return lax.axis_index("x"), lax.axis_index("y"), lax.axis_index("c")


def all_gather(shard, *, name, in_vmem):
    r, c_ = shard.shape

    def body(x_ref, out_ref, token_ref, send_sems, recv_sems, local_sem):
        x, y, c = _position()
        me, sibling = (x, y, c), (x, y, 1 - c)
        chips = [(1 - x, y), (x, 1 - y), (1 - x, 1 - y)]
        token_ref[...] = jnp.zeros_like(token_ref)

        def rows(px, py, pc):
            return out_ref.at[4 * px + 2 * py + pc]

        def copy(k, block, to, src=None):
            return pltpu.make_async_remote_copy(
                src_ref=rows(*block) if src is None else src, dst_ref=rows(*block),
                send_sem=send_sems.at[k], recv_sem=recv_sems.at[k], device_id=to, device_id_type=MESH)

        mine = pltpu.make_async_copy(x_ref, rows(*me), local_sem)
        mine.start()
        first = [copy(0, me, sibling, src=x_ref)]
        first += [copy(1 + j, me, (*chip, c), src=x_ref) for j, chip in enumerate(chips)]
        for cp in first:
            cp.start()
        passed = [copy(4 + j, (*chip, c), sibling) for j, chip in enumerate(chips)]
        for j, chip in enumerate(chips):
            copy(1 + j, (*chip, c), me).wait_recv()
            passed[j].start()
        copy(0, sibling, me).wait_recv()
        for j, chip in enumerate(chips):
            copy(4 + j, (*chip, 1 - c), me).wait_recv()
        for cp in first + passed:
            cp.wait_send()
        mine.wait()

    space = pltpu.VMEM if in_vmem else pltpu.HBM
    return pl.pallas_call(
        body, name=name,
        out_shape=[jax.ShapeDtypeStruct((N_DEV, r, c_), shard.dtype), jax.ShapeDtypeStruct((8, LANES), F32)],
        in_specs=[pl.BlockSpec(memory_space=space)],
        out_specs=[pl.BlockSpec(memory_space=space), pl.BlockSpec(memory_space=pltpu.VMEM)],
        scratch_shapes=[pltpu.SemaphoreType.DMA((7,)), pltpu.SemaphoreType.DMA((7,)), pltpu.SemaphoreType.DMA],
        compiler_params=_cp(),
    )(shard)


def all_gather_many(shards, *, name):
    n = len(shards)

    def body(*refs):
        x_refs, out_refs, token_ref = refs[:n], refs[n:2 * n], refs[2 * n]
        send_sems, recv_sems, local_sems = refs[2 * n + 1:]
        x, y, c = _position()
        me, sibling = (x, y, c), (x, y, 1 - c)
        chips = [(1 - x, y), (x, 1 - y), (1 - x, 1 - y)]
        token_ref[...] = jnp.zeros_like(token_ref)

        def rows(t, px, py, pc):
            return out_refs[t].at[4 * px + 2 * py + pc]

        def copy(t, k, block, to, src=None):
            return pltpu.make_async_remote_copy(
                src_ref=rows(t, *block) if src is None else src, dst_ref=rows(t, *block),
                send_sem=send_sems.at[7 * t + k], recv_sem=recv_sems.at[7 * t + k], device_id=to, device_id_type=MESH)

        mine = [pltpu.make_async_copy(x_refs[t], rows(t, *me), local_sems.at[t]) for t in range(n)]
        for cp in mine:
            cp.start()
        first = []
        for t in range(n):
            first.append(copy(t, 0, me, sibling, src=x_refs[t]))
            first += [copy(t, 1 + j, me, (*chip, c), src=x_refs[t]) for j, chip in enumerate(chips)]
        for cp in first:
            cp.start()
        passed = []
        for t in range(n):
            for j, chip in enumerate(chips):
                copy(t, 1 + j, (*chip, c), me).wait_recv()
                passed.append(copy(t, 4 + j, (*chip, c), sibling))
                passed[-1].start()
        for t in range(n):
            copy(t, 0, sibling, me).wait_recv()
            for j, chip in enumerate(chips):
                copy(t, 4 + j, (*chip, 1 - c), me).wait_recv()
        for cp in first + passed:
            cp.wait_send()
        for cp in mine:
            cp.wait()

    hbm = pl.BlockSpec(memory_space=pltpu.HBM)
    out = pl.pallas_call(
        body, name=name,
        out_shape=[jax.ShapeDtypeStruct((N_DEV,) + s.shape, s.dtype) for s in shards] + [jax.ShapeDtypeStruct((8, LANES), F32)],
        in_specs=[hbm] * n, out_specs=[hbm] * n + [pl.BlockSpec(memory_space=pltpu.VMEM)],
        scratch_shapes=[pltpu.SemaphoreType.DMA((7 * n,)), pltpu.SemaphoreType.DMA((7 * n,)), pltpu.SemaphoreType.DMA((n,))],
        compiler_params=_cp(),
    )(*shards)
    return out[:n], out[n]


def exchange(srcs, plan, n_copies, slots, *, name):
    n = len(srcs)

    def body(*refs):
        src_refs, out_refs, (send_sems, recv_sems) = refs[:n], refs[n:2 * n], refs[2 * n:]
        copies = []
        for t in range(n):
            for k in range(n_copies):
                s, d, to = plan(src_refs[t], out_refs[t], k)
                copies.append(pltpu.make_async_remote_copy(
                    src_ref=s, dst_ref=d, send_sem=send_sems.at[n_copies * t + k], recv_sem=recv_sems.at[n_copies * t + k],
                    device_id=to, device_id_type=MESH))
        for cp in copies:
            cp.start()
        for cp in copies:
            cp.wait_recv()
        for cp in copies:
            cp.wait_send()

    hbm = pl.BlockSpec(memory_space=pltpu.HBM)
    return pl.pallas_call(
        body, name=name,
        out_shape=[jax.ShapeDtypeStruct((slots,) + s.shape[1:], s.dtype) for s in srcs],
        in_specs=[hbm] * n, out_specs=[hbm] * n,
        scratch_shapes=[pltpu.SemaphoreType.DMA((n_copies * n,)), pltpu.SemaphoreType.DMA((n_copies * n,))],
        compiler_params=_cp(),
    )(*srcs)


_HBM = pl.BlockSpec(memory_space=pltpu.HBM)
_SEM = pl.BlockSpec(memory_space=pltpu.SEMAPHORE)
_EFFECT = pltpu.SideEffectType.DATAFLOW_SIDE_EFFECTING


def _split_copies(src_refs, land_refs, send_sems, recv_sems, plan, n_copies):
    copies = []
    for t, (src_ref, land_ref) in enumerate(zip(src_refs, land_refs)):
        for k in range(n_copies):
            s, d, to = plan(src_ref, land_ref, k)
            copies.append(pltpu.make_async_remote_copy(
                src_ref=s, dst_ref=d, send_sem=send_sems.at[n_copies * t + k], recv_sem=recv_sems.at[n_copies * t + k],
                device_id=to, device_id_type=MESH))
    return copies


def split_start(srcs, lands, plan, n_copies, *, name):
    n = len(srcs)

    def body(*refs):
        src_refs, land_refs, send_sems, recv_sems, token = refs[:n], refs[n:2 * n], refs[2 * n], refs[2 * n + 1], refs[-1]
        for cp in _split_copies(src_refs, land_refs, send_sems, recv_sems, plan, n_copies):
            cp.start()
        token[...] = jnp.zeros_like(token)

    both = list(srcs) + list(lands)
    out = pl.pallas_call(
        body, name=name,
        out_shape=(pltpu.SemaphoreType.DMA((n_copies * n,)), pltpu.SemaphoreType.DMA((n_copies * n,)),
                   *[pltpu.HBM(a.shape, a.dtype) for a in both], jax.ShapeDtypeStruct((8, LANES), F32)),
        in_specs=(_HBM,) * (2 * n), out_specs=(_SEM, _SEM) + (_HBM,) * (2 * n) + (pl.BlockSpec(memory_space=pltpu.VMEM),),
        input_output_aliases={i: 2 + i for i in range(2 * n)},
        compiler_params=pltpu.CompilerParams(has_side_effects=_EFFECT),
    )(*[pltpu.with_memory_space_constraint(a, pltpu.HBM) for a in both])
    return out[0], out[1], list(out[2:2 + n]), list(out[2 + n:2 + 2 * n]), out[-1]


def split_wait(started, after, plan, n_copies, *, name):
    send_sems, recv_sems, srcs, lands, _ = started
    n = len(srcs)

    def body(*refs):
        src_refs, land_refs, send_sems, recv_sems = refs[:n], refs[n:2 * n], refs[2 * n], refs[2 * n + 1]
        for cp in _split_copies(src_refs, land_refs, send_sems, recv_sems, plan, n_copies):
            cp.wait_send()
            cp.wait_recv()

    both = list(srcs) + list(lands)
    out = pl.pallas_call(
        body, name=name,
        out_shape=tuple(pltpu.HBM(a.shape, a.dtype) for a in both),
        in_specs=(_HBM,) * (2 * n) + (_SEM, _SEM, pl.BlockSpec(memory_space=pl.ANY)), out_specs=(_HBM,) * (2 * n),
        input_output_aliases={i: i for i in range(2 * n)},
        compiler_params=pltpu.CompilerParams(has_side_effects=_EFFECT),
    )(*both, send_sems, recv_sems, after)
    return list(out[:n]), list(out[n:])


def _gather_plan(src_ref, land_ref, k):
    x, y, c = _position()
    bits = k + 1
    peer = ((1 - x) if bits & 4 else x, (1 - y) if bits & 2 else y, (1 - c) if bits & 1 else c)
    return src_ref, land_ref.at[4 * x + 2 * y + c], peer


def _sibling_plan(src_ref, land_ref, k):
    x, y, c = _position()
    return src_ref.at[2 * k + (1 - c)], land_ref.at[k], (x, y, 1 - c)


def _chips_plan(src_ref, land_ref, j):
    x, y, c = _position()
    px, py = [(1 - x, y), (x, 1 - y), (1 - x, 1 - y)][j]
    return src_ref.at[2 * px + py], land_ref.at[j], (px, py, c)


SUM_STEPS = 2


def sum_for_chips(parts, from_sibling, c_idx, *, name):
    n = len(parts)

    def body(c_ref, *refs):
        del c_ref
        for t in range(n):
            refs[2 * n + t][...] = (refs[t][...] + refs[n + t][...]).astype(BF16)

    def blk(a):
        return (None, a.shape[1] // SUM_STEPS, a.shape[2])

    return pl.pallas_call(
        body, name=name,
        grid_spec=pltpu.PrefetchScalarGridSpec(
            num_scalar_prefetch=1, grid=(4, SUM_STEPS),
            in_specs=[pl.BlockSpec(blk(a), lambda k, i, c_ref: (2 * k + c_ref[0], i, 0)) for a in parts]
            + [pl.BlockSpec(blk(a), lambda k, i, c_ref: (k, i, 0)) for a in from_sibling],
            out_specs=[pl.BlockSpec(blk(a), lambda k, i, c_ref: (k, i, 0)) for a in from_sibling]),
        out_shape=[jax.ShapeDtypeStruct(a.shape, BF16) for a in from_sibling], compiler_params=_cp(),
    )(c_idx, *parts, *from_sibling)


def sum_final(parts, from_sibling, from_chips, kc_idx, *, name):
    n = len(parts)

    def body(kc_ref, *refs):
        del kc_ref
        for t in range(n):
            p, s, a, b, d = (refs[j * n + t] for j in range(5))
            refs[5 * n + t][...] = (((p[...] + s[...]) + a[...].astype(F32)) + b[...].astype(F32)) + d[...].astype(F32)

    def blk(a):
        return (None, a.shape[1] // SUM_STEPS, a.shape[2])

    def chip_specs(j):
        return [pl.BlockSpec(blk(a), lambda i, kc: (j, i, 0)) for a in from_chips]

    return pl.pallas_call(
        body, name=name,
        grid_spec=pltpu.PrefetchScalarGridSpec(
            num_scalar_prefetch=1, grid=(SUM_STEPS,),
            in_specs=[pl.BlockSpec(blk(a), lambda i, kc: (2 * kc[0] + kc[1], i, 0)) for a in parts]
            + [pl.BlockSpec(blk(a), lambda i, kc: (kc[0], i, 0)) for a in from_sibling]
            + chip_specs(0) + chip_specs(1) + chip_specs(2),
            out_specs=[pl.BlockSpec(blk(a)[1:], lambda i, kc: (i, 0)) for a in parts]),
        out_shape=[jax.ShapeDtypeStruct(a.shape[1:], F32) for a in parts], compiler_params=_cp(),
    )(kc_idx, *parts, *from_sibling, *from_chips, *from_chips, *from_chips)


def sum_devices(gathered):
    n, r, c_ = gathered.shape

    def body(g_ref, o_ref):
        acc = g_ref[0]
        for k in range(1, n):
            acc = acc + g_ref[k]
        o_ref[...] = acc

    return pl.pallas_call(
        body, name="sum_devices", out_shape=jax.ShapeDtypeStruct((r, c_), F32), compiler_params=_cp(),
    )(gathered)


BIG = (
    ("w_in", IN_COLS, True), ("w_out", D_MODEL, False), ("wq_x", D_MODEL, False), ("wkv_x", 2 * D_MODEL, True),
    ("wo_x", D_MODEL, False), ("w_gate_up", 2 * D_FF, True), ("w_down", D_FF, False),
)
SHARD_ROWS = sum(rows // N_DEV for _, rows, _ in BIG)

SMALL = ("norm_mix_g", "q_norm_g", "k_norm_g", "sinks", "conv_b", "conv_ln_g", "conv_ln_b",
         "norm_x_g", "norm_mem_g", "xq_norm_g", "xk_norm_g", "norm_ffn_g")


def _pack_rows(vectors, width=LANES, row_multiple=8):
    flat = jnp.concatenate([v.reshape(-1) for v in vectors])
    per = width * row_multiple
    padded = -(-flat.shape[0] // per) * per
    return jnp.pad(flat, (0, padded - flat.shape[0])).reshape(-1, width)


def _unpack_rows(packed, shapes):
    flat = packed.reshape(-1)
    out, at = [], 0
    for s in shapes:
        n = 1
        for dim in s:
            n *= dim
        out.append(flat[at:at + n].reshape(s))
        at += n
    return out


def _layer_fwd(x0, mem, w, s):
    h0, u = norm_proj(x0, s["norm_mix_g"], w["w_in"])
    mixed = swa_fwd(u, s["q_norm_g"], s["k_norm_g"], s["sinks"])
    mixed, conv_y = conv_fwd(u, mixed, s["conv_w"], s["conv_b"], s["conv_ln_g"], s["conv_ln_b"])
    memn = rms_fwd(mem, s["norm_mem_g"])
    kv = mm(memn, w["wkv_x"], trans_b=True, out_dtype=F32, name="mm_kv")
    x1, h1, qx, o, x2, h2 = mid_fwd(mixed, x0, w["w_out"], s["norm_x_g"], w["wq_x"], kv, s["xq_norm_g"], s["xk_norm_g"],
                                    w["wo_x"], s["norm_ffn_g"])
    gu, a, x3 = ffn_fwd(h2, x2, w["w_gate_up"], w["w_down"])
    saved = dict(x0=x0, h0=h0, u=u, conv_y=conv_y, mixed=mixed, x1=x1, h1=h1, qx=qx, memn=memn, kv=kv, o=o, x2=x2, h2=h2,
                 gu=gu, a=a)
    return x3, saved


def _ordered_after(a, token):
    return a if token is None else a + token[0, 0]


def _layer_bwd(dx3, mem, w, s, sv, token, stage_done):
    gs = {}
    dgu, dx2, dg = ffn_bwd(dx3, sv["gu"], sv["x2"], _ordered_after(s["norm_ffn_g"], token), w["w_down"], w["w_gate_up"])
    gs["norm_ffn_g"] = dg
    gb = {"w_down": mm_tn(sv["a"], dx3, name="mm_dw_down")}
    gb["w_gate_up"] = mm_tn(dgu, sv["h2"], name="mm_dw_gate_up")
    token = stage_done("ffn", gb, gb["w_gate_up"])

    gb = {}
    dq, dx1, dmixed, dkv, dqg, dkg, dg = mid_bwd(dx2, sv["qx"], sv["kv"], s["xq_norm_g"], s["xk_norm_g"], sv["x1"],
                                                 _ordered_after(s["norm_x_g"], token), w["wo_x"], w["wq_x"], w["w_out"])
    gs["xq_norm_g"], gs["xk_norm_g"], gs["norm_x_g"] = dqg, dkg, dg
    gb["wo_x"] = mm_tn(sv["o"], dx2, name="mm_dwo")
    gb["wq_x"] = mm_tn(sv["h1"], dq, name="mm_dwq")
    dmemn = mm(dkv, w["wkv_x"], trans_b=False, out_dtype=F32, name="mm_dmemn")
    gb["wkv_x"] = mm_tn(dkv, sv["memn"], name="mm_dwkv")
    _, dg = rms_bwd(dmemn, mem, s["norm_mem_g"], None)
    gs["norm_mem_g"] = dg
    gb["w_out"] = mm_tn(sv["mixed"], dx1, name="mm_dw_out")
    token = stage_done("mid", gb, gb["w_out"])

    du, dqg, dkg, dsinks = swa_bwd(sv["u"], dmixed, _ordered_after(s["q_norm_g"], token), s["k_norm_g"], s["sinks"])
    gs["q_norm_g"], gs["k_norm_g"], gs["sinks"] = dqg, dkg, dsinks[0, :N_Q_HEADS]
    token = stage_done("attn", {}, dqg)
    du, dconv_w, dvec = conv_bwd(sv["u"], sv["conv_y"], dmixed, du, s["conv_w"], _ordered_after(s["conv_ln_g"], token),
                                 s["conv_ln_b"])
    gs["conv_w"] = dconv_w[:CONV_K]
    gs["conv_b"], gs["conv_ln_g"], gs["conv_ln_b"] = dvec[0], dvec[1], dvec[2]
    dx0, dg = in_bwd(du, w["w_in"], sv["x0"], s["norm_mix_g"], dx1)
    gs["norm_mix_g"] = dg
    token = stage_done("mix", {"w_in": mm_tn(du, sv["h0"], name="mm_dw_in")}, dx0)
    return dx0, gs, token


def _local_step(x, mem, target, weights_of, smalls, stage_done):
    saved, weights = [], []
    h = x
    for l in range(DEPTH):
        weights.append(weights_of(l, h))
        h, sv = _layer_fwd(h, mem, weights[l], smalls[l])
        saved.append(sv)
    dx, loss_part = loss_head(h, target)
    gss, token = [None] * DEPTH, None
    for l in reversed(range(DEPTH)):
        dx, gss[l], token = _layer_bwd(dx, mem, weights[l], smalls[l], saved[l], token,
                                       functools.partial(stage_done, l))
    return loss_part[0, 0], dx, gss


def kernel(x, mem, norm_mix_g, w_in, q_norm_g, k_norm_g, sinks, conv_w, conv_b, conv_ln_g, conv_ln_b, w_out, norm_x_g, norm_mem_g, wq_x, wkv_x, xq_norm_g, xk_norm_g, wo_x, norm_ffn_g, w_gate_up, w_down, loss_target, m_norm_mix_g, m_w_in, m_q_norm_g, m_k_norm_g, m_sinks, m_conv_w, m_conv_b, m_conv_ln_g, m_conv_ln_b, m_w_out, m_norm_x_g, m_norm_mem_g, m_wq_x, m_wkv_x, m_xq_norm_g, m_xk_norm_g, m_wo_x, m_norm_ffn_g, m_w_gate_up, m_w_down, v_norm_mix_g, v_w_in, v_q_norm_g, v_k_norm_g, v_sinks, v_conv_w, v_conv_b, v_conv_ln_g, v_conv_ln_b, v_w_out, v_norm_x_g, v_norm_mem_g, v_wq_x, v_wkv_x, v_xq_norm_g, v_xk_norm_g, v_wo_x, v_norm_ffn_g, v_w_gate_up, v_w_down):
    P = dict(norm_mix_g=norm_mix_g, w_in=w_in, q_norm_g=q_norm_g, k_norm_g=k_norm_g, sinks=sinks, conv_w=conv_w, conv_b=conv_b,
             conv_ln_g=conv_ln_g, conv_ln_b=conv_ln_b, w_out=w_out, norm_x_g=norm_x_g, norm_mem_g=norm_mem_g, wq_x=wq_x,
             wkv_x=wkv_x, xq_norm_g=xq_norm_g, xk_norm_g=xk_norm_g, wo_x=wo_x, norm_ffn_g=norm_ffn_g, w_gate_up=w_gate_up,
             w_down=w_down)
    M = dict(norm_mix_g=m_norm_mix_g, w_in=m_w_in, q_norm_g=m_q_norm_g, k_norm_g=m_k_norm_g, sinks=m_sinks, conv_w=m_conv_w,
             conv_b=m_conv_b, conv_ln_g=m_conv_ln_g, conv_ln_b=m_conv_ln_b, w_out=m_w_out, norm_x_g=m_norm_x_g,
             norm_mem_g=m_norm_mem_g, wq_x=m_wq_x, wkv_x=m_wkv_x, xq_norm_g=m_xq_norm_g, xk_norm_g=m_xk_norm_g, wo_x=m_wo_x,
             norm_ffn_g=m_norm_ffn_g, w_gate_up=m_w_gate_up, w_down=m_w_down)
    V = dict(norm_mix_g=v_norm_mix_g, w_in=v_w_in, q_norm_g=v_q_norm_g, k_norm_g=v_k_norm_g, sinks=v_sinks, conv_w=v_conv_w,
             conv_b=v_conv_b, conv_ln_g=v_conv_ln_g, conv_ln_b=v_conv_ln_b, w_out=v_w_out, norm_x_g=v_norm_x_g,
             norm_mem_g=v_norm_mem_g, wq_x=v_wq_x, wkv_x=v_wkv_x, xq_norm_g=v_xq_norm_g, xk_norm_g=v_xk_norm_g, wo_x=v_wo_x,
             norm_ffn_g=v_norm_ffn_g, w_gate_up=v_w_gate_up, w_down=v_w_down)
    order = ["norm_mix_g", "w_in", "q_norm_g", "k_norm_g", "sinks", "conv_w", "conv_b", "conv_ln_g", "conv_ln_b", "w_out",
             "norm_x_g", "norm_mem_g", "wq_x", "wkv_x", "xq_norm_g", "xk_norm_g", "wo_x", "norm_ffn_g", "w_gate_up", "w_down"]
    xi, yi, ci = _position()
    dev = 4 * xi + 2 * yi + ci
    x2d, mem2d, tgt2d = x[0], mem[0], loss_target[0]

    def travelling(name, l, transposed):
        a = P[name][l]
        return (a.T if transposed else a).astype(BF16)

    names = [n for n, _, _ in BIG]
    rows_of = {n: rows for n, rows, _ in BIG}

    def whole(gathered):
        return {n: g.reshape(rows_of[n], D_MODEL) for n, g in zip(names, gathered)}

    cw = jnp.pad(conv_w.reshape(DEPTH * CONV_K, CONV_CH // N_DEV), ((0, 2), (0, LANES - CONV_CH // N_DEV)))
    cw_all, cw_token = all_gather(cw, name="ag_conv_w", in_vmem=True)
    shards0 = [_ordered_after(travelling(n, 0, tr), cw_token.astype(BF16)) for n, _, tr in BIG]
    gathered0, token0 = all_gather_many(shards0, name="ag_weights0")
    weights0 = whole(gathered0)
    shards1 = [_ordered_after(travelling(n, 1, tr), token0.astype(BF16)) for n, _, tr in BIG]
    lands1 = [lax.dynamic_update_slice(lax.empty((N_DEV,) + s.shape, BF16), s[None], (dev, 0, 0)) for s in shards1]
    gather1 = split_start(shards1, lands1, _gather_plan, N_DEV - 1, name="ag_weights1_start")

    def weights_of(l, marker):
        if l == 0:
            return weights0
        return whole(split_wait(gather1, marker, _gather_plan, N_DEV - 1, name="ag_weights1_wait")[1])

    cw_full = cw_all[:, :DEPTH * CONV_K, :CONV_CH // N_DEV].reshape(N_DEV, DEPTH, CONV_K, CONV_CH // N_DEV)
    cw_full = jnp.transpose(cw_full, (1, 2, 0, 3)).reshape(DEPTH, CONV_K, CONV_CH)
    smalls = []
    for l in range(DEPTH):
        sl = {n: P[n][l] for n in SMALL}
        sl["conv_w"] = jnp.pad(cw_full[l], ((0, CONV_HALO - CONV_K), (0, 0)))
        smalls.append(sl)
    smalls[0]["norm_mix_g"] = _ordered_after(smalls[0]["norm_mix_g"], gather1[4])

    c_idx = jnp.reshape(ci, (1,)).astype(jnp.int32)
    kc_idx = jnp.stack([2 * xi + yi, ci]).astype(jnp.int32)
    got, flight, reduced = {}, {}, {}

    def as_parts(gb):
        keys = sorted(gb)
        return keys, [gb[k].reshape(N_DEV, rows_of[k[1]] // N_DEV, D_MODEL) for k in keys]

    def lands_like(parts, blocks, dtype):
        return [lax.empty((blocks,) + p.shape[1:], dtype) for p in parts]

    def to_sibling(group, gb):
        keys, parts = as_parts(gb)
        flight[group] = (keys, split_start(parts, lands_like(parts, 4, F32), _sibling_plan, 4,
                                           name=f"rs_sibling_{group}_start"))
        return flight[group][1][4]

    def to_chips(group, marker):
        keys, started = flight[group]
        parts, from_sibling = split_wait(started, marker, _sibling_plan, 4, name=f"rs_sibling_{group}_wait")
        chip_sums = sum_for_chips(parts, from_sibling, c_idx, name=f"rs_sum_for_chips_{group}")
        started = split_start(chip_sums, lands_like(parts, 3, BF16), _chips_plan, 3, name=f"rs_chips_{group}_start")
        flight[group] = (keys, parts, from_sibling, started)
        return started[4]

    def finish(group, marker):
        keys, parts, from_sibling, started = flight[group]
        _, from_chips = split_wait(started, marker, _chips_plan, 3, name=f"rs_chips_{group}_wait")
        reduced.update(zip(keys, sum_final(parts, from_sibling, from_chips, kc_idx, name=f"rs_sum_final_{group}")))

    def stage_done(l, stage, gb, marker):
        gb = {(l, n): g for n, g in gb.items()}
        if l == 1:
            got.update(gb)
            return to_sibling("l1", got) if stage == "mix" else None
        if stage == "ffn":
            return to_chips("l1", marker) + to_sibling("ffn", gb)
        if stage == "mid":
            return to_chips("ffn", marker) + to_sibling("mid", gb)
        if stage == "attn":
            return to_chips("mid", marker)
        finish("l1", marker)
        finish("ffn", marker)
        finish("mid", marker)
        keys, parts = as_parts(gb)
        from_sibling = exchange(parts, _sibling_plan, 4, 4, name="rs_sibling_in")
        chip_sums = sum_for_chips(parts, from_sibling, c_idx, name="rs_sum_for_chips_in")
        from_chips = exchange(chip_sums, _chips_plan, 3, 3, name="rs_chips_in")
        reduced.update(zip(keys, sum_final(parts, from_sibling, from_chips, kc_idx, name="rs_sum_final_in")))
        return None

    loss_part, grad_x, gss = _local_step(x2d, mem2d, tgt2d, weights_of, smalls, stage_done)
    loss = lax.psum(loss_part, ("x", "y", "c"))

    grads = {n: jnp.stack([reduced[(l, n)].T if transposed else reduced[(l, n)] for l in range(DEPTH)])
             for n, _, transposed in BIG}

    small_names = SMALL + ("conv_w",)
    small_shapes = [(DEPTH,) + ((CONV_K, CONV_CH) if n == "conv_w" else P[n].shape[1:]) for n in small_names]
    small_parts = _pack_rows([jnp.stack([gss[l][n].reshape(sh[1:]) for l in range(DEPTH)])
                              for n, sh in zip(small_names, small_shapes)])
    small_sum = sum_devices(all_gather(small_parts, name="ag_small_grads", in_vmem=True)[0])
    for n, g in zip(small_names, _unpack_rows(small_sum, small_shapes)):
        if n == "conv_w":
            g = lax.dynamic_slice_in_dim(g, dev * (CONV_CH // N_DEV), CONV_CH // N_DEV, axis=2)
        grads[n] = g

    delta, new_m, new_v = {}, {}, {}
    for n, _, _ in BIG:
        shape = P[n].shape
        two_d = lambda a: a.reshape(shape[0] * shape[1], shape[2])
        d_, m_, v_ = adamw(two_d(P[n]), two_d(grads[n]), two_d(M[n]), two_d(V[n]), name="adamw_" + n)
        delta[n], new_m[n], new_v[n] = d_.reshape(shape), m_.reshape(shape), v_.reshape(shape)
    shapes = [P[n].shape for n in small_names]
    d_, m_, v_ = adamw(_pack_rows([P[n] for n in small_names]), _pack_rows([grads[n] for n in small_names]),
                       _pack_rows([M[n] for n in small_names]), _pack_rows([V[n] for n in small_names]), name="adamw_small")
    for n, dd, mm_, vv in zip(small_names, _unpack_rows(d_, shapes), _unpack_rows(m_, shapes), _unpack_rows(v_, shapes)):
        delta[n], new_m[n], new_v[n] = dd, mm_, vv

    return (loss, grad_x[None], *[grads[n] for n in order], *[delta[n] for n in order],
            *[new_m[n] for n in order], *[new_v[n] for n in order])
```

```python
import functools

import jax
import jax.numpy as jnp
import numpy as np
from jax import lax
from jax.experimental import pallas as pl
from jax.experimental.pallas import tpu as pltpu

F32 = jnp.float32
BF16 = jnp.bfloat16

D_MODEL = 1024
HEAD_DIM = 64
N_Q_HEADS = 8
N_KV_HEADS = 2
GROUP = N_Q_HEADS // N_KV_HEADS
ATTN_WIDTH = N_Q_HEADS * HEAD_DIM
KV_WIDTH = N_KV_HEADS * HEAD_DIM
QKV_WIDTH = ATTN_WIDTH + 2 * KV_WIDTH
CONV_CH = 512
IN_COLS = QKV_WIDTH + 2 * CONV_CH
CONV_K = 31
CONV_HALO = 32
BLOCK = 128
N_X_HEADS = 4
X_HEAD_DIM = 256
D_FF = 2816
EPS = 1e-6
NEG = -1e30
DEPTH = 2
N_DEV = 8

ADAM_LR = 0.001
ADAM_B1 = 0.9
ADAM_B2 = 0.999
ADAM_EPS = 1e-08
ADAM_WD = 0.01
ADAM_STEP = 10

V7X_VMEM_LIMIT = 56 * 1024 * 1024
LANES = 128

MESH = pl.DeviceIdType.MESH


def _cp(**kw):
    return pltpu.CompilerParams(vmem_limit_bytes=V7X_VMEM_LIMIT, **kw)


def _dot(a, b, dims):
    return lax.dot_general(a.astype(BF16), b.astype(BF16), (dims, ((), ())), preferred_element_type=F32)


def _dot_nn(a, b):
    return _dot(a, b, ((1,), (0,)))


def _dot_nt(a, b):
    return _dot(a, b, ((1,), (1,)))


def _dot_tn(a, b):
    return _dot(a, b, ((0,), (0,)))


def _sigmoid(x):
    return jax.nn.sigmoid(x)


def _rms(x):
    r = lax.rsqrt(jnp.mean(x * x, axis=-1, keepdims=True) + EPS)
    return x * r, r


def _rms_bwd(dy, xhat, r, g):
    dxh = dy * g
    return r * (dxh - xhat * jnp.mean(dxh * xhat, axis=-1, keepdims=True))


def rms_fwd(x, g, *, tm=512):
    m, d = x.shape
    tm = min(tm, m)

    def body(x_ref, g_ref, o_ref):
        xh, _ = _rms(x_ref[...])
        o_ref[...] = (xh * g_ref[...]).astype(o_ref.dtype)

    return pl.pallas_call(
        body, name="rms_fwd", grid=(m // tm,),
        in_specs=[pl.BlockSpec((tm, d), lambda i: (i, 0)), pl.BlockSpec((1, d), lambda i: (0, 0))],
        out_specs=pl.BlockSpec((tm, d), lambda i: (i, 0)),
        out_shape=jax.ShapeDtypeStruct((m, d), BF16), compiler_params=_cp(),
    )(x, g.reshape(1, d))


def rms_bwd(dh, x, g, dres, *, tm=512):
    m, d = x.shape
    tm = min(tm, m)
    has_res = dres is not None

    def body(*refs):
        if has_res:
            dh_ref, x_ref, g_ref, r_ref, dx_ref, dg_ref = refs
        else:
            dh_ref, x_ref, g_ref, dx_ref, dg_ref = refs
        xh, r = _rms(x_ref[...])
        dy = dh_ref[...].astype(F32)

        @pl.when(pl.program_id(0) == 0)
        def _():
            dg_ref[...] = jnp.zeros_like(dg_ref)

        dg_ref[...] += jnp.sum(dy * xh, axis=0, keepdims=True)
        dx = _rms_bwd(dy, xh, r, g_ref[...])
        if has_res:
            dx = dx + r_ref[...]
        dx_ref[...] = dx

    row = pl.BlockSpec((tm, d), lambda i: (i, 0))
    vec = pl.BlockSpec((1, d), lambda i: (0, 0))
    ins = [dh, x, g.reshape(1, d)] + ([dres] if has_res else [])
    return pl.pallas_call(
        body, name="rms_bwd" + ("_res" if has_res else ""), grid=(m // tm,),
        in_specs=[row, row, vec] + ([row] if has_res else []),
        out_specs=[row, vec],
        out_shape=[jax.ShapeDtypeStruct((m, d), F32), jax.ShapeDtypeStruct((1, d), F32)],
        compiler_params=_cp(),
    )(*ins)


def loss_head(y, target, *, tm=512):
    m, d = y.shape

    def body(y_ref, t_ref, dy_ref, l_ref):
        err = y_ref[...] - t_ref[...]
        dy_ref[...] = err * (1.0 / d)

        @pl.when(pl.program_id(0) == 0)
        def _():
            l_ref[...] = jnp.zeros_like(l_ref)

        part = jnp.sum(jnp.sum(err * err, axis=-1, keepdims=True), axis=0, keepdims=True)
        l_ref[...] += jnp.broadcast_to(part * (0.5 / d), l_ref.shape)

    row = pl.BlockSpec((tm, d), lambda i: (i, 0))
    return pl.pallas_call(
        body, name="loss_head", grid=(m // tm,),
        in_specs=[row, row],
        out_specs=[row, pl.BlockSpec((1, LANES), lambda i: (0, 0))],
        out_shape=[jax.ShapeDtypeStruct((m, d), F32), jax.ShapeDtypeStruct((1, LANES), F32)],
        compiler_params=_cp(),
    )(y, target)


def _tile(n, cap):
    if n <= cap:
        return n
    best = None
    for t in range(LANES, cap + 1, LANES):
        if n % t == 0:
            best = t
    assert best is not None, (n, cap)
    return best


def mm(a, b, *, trans_b, out_dtype, res=None, tm=1024, tn_cap=1536, name):
    m, k = a.shape
    n = b.shape[0] if trans_b else b.shape[1]
    assert (b.shape[1] if trans_b else b.shape[0]) == k
    tm = min(tm, m)
    tn = _tile(n, tn_cap)
    has_res = res is not None

    def body(*refs):
        if has_res:
            a_ref, b_ref, r_ref, o_ref = refs
        else:
            a_ref, b_ref, o_ref = refs
        acc = _dot_nt(a_ref[...], b_ref[...]) if trans_b else _dot_nn(a_ref[...], b_ref[...])
        if has_res:
            acc = acc + r_ref[...]
        o_ref[...] = acc.astype(o_ref.dtype)

    b_spec = pl.BlockSpec((tn, k), lambda i, j: (j, 0)) if trans_b else pl.BlockSpec((k, tn), lambda i, j: (0, j))
    o_spec = pl.BlockSpec((tm, tn), lambda i, j: (i, j))
    return pl.pallas_call(
        body, name=name, grid=(m // tm, n // tn),
        in_specs=[pl.BlockSpec((tm, k), lambda i, j: (i, 0)), b_spec] + ([o_spec] if has_res else []),
        out_specs=o_spec,
        out_shape=jax.ShapeDtypeStruct((m, n), out_dtype), compiler_params=_cp(),
    )(*([a, b] + ([res] if has_res else [])))


def mm_tn(a, b, *, name, ta_cap=1536, tb_cap=1024, tk=1024):
    m, ka = a.shape
    nb = b.shape[1]
    assert b.shape[0] == m
    tk = min(tk, m)
    ta = _tile(ka, ta_cap)
    tb = _tile(nb, tb_cap)

    def body(a_ref, b_ref, o_ref):
        @pl.when(pl.program_id(2) == 0)
        def _():
            o_ref[...] = jnp.zeros_like(o_ref)

        o_ref[...] += _dot_tn(a_ref[...], b_ref[...])

    return pl.pallas_call(
        body, name=name, grid=(ka // ta, nb // tb, m // tk),
        in_specs=[pl.BlockSpec((tk, ta), lambda i, j, kk: (kk, i)), pl.BlockSpec((tk, tb), lambda i, j, kk: (kk, j))],
        out_specs=pl.BlockSpec((ta, tb), lambda i, j, kk: (i, j)),
        out_shape=jax.ShapeDtypeStruct((ka, nb), F32), compiler_params=_cp(),
    )(a, b)


def _whole(shape):
    return pl.BlockSpec(shape, lambda i: (0,) * len(shape), pipeline_mode=pl.Buffered(1))


def _rows(tm, n):
    return pl.BlockSpec((tm, n), lambda i: (i, 0))


def _vec(n):
    return pl.BlockSpec((1, n), lambda i: (0, 0))


def _chunks(n, cap=1408):
    size = _tile(n, cap)
    return [(s, size) for s in range(0, n, size)]


def _zero_at_first_step(*refs):
    @pl.when(pl.program_id(0) == 0)
    def _():
        for r in refs:
            r[...] = jnp.zeros_like(r)


def norm_proj(x, g, wt, *, tm=512):
    m, d = x.shape
    n = wt.shape[0]

    def body(x_ref, g_ref, wt_ref, h_ref, u_ref):
        h = (_rms(x_ref[...])[0] * g_ref[...]).astype(BF16)
        h_ref[...] = h
        for s, sz in _chunks(n):
            u_ref[:, s:s + sz] = _dot_nt(h, wt_ref[s:s + sz, :])

    return pl.pallas_call(
        body, name="norm_proj", grid=(m // tm,),
        in_specs=[_rows(tm, d), _vec(d), _whole((n, d))],
        out_specs=[_rows(tm, d), _rows(tm, n)],
        out_shape=[jax.ShapeDtypeStruct((m, d), BF16), jax.ShapeDtypeStruct((m, n), F32)],
        compiler_params=_cp(),
    )(x, g.reshape(1, d), wt)


def _xattn_heads(q_ref, kv_ref, qg_v, kg_v, d):
    out = []
    for h in range(N_X_HEADS):
        cols = slice(h * X_HEAD_DIM, (h + 1) * X_HEAD_DIM)
        qh, rq = _rms(q_ref[:, cols])
        qn = qh * qg_v
        kn = _rms(kv_ref[:, cols])[0] * kg_v
        v = kv_ref[:, d + h * X_HEAD_DIM:d + (h + 1) * X_HEAD_DIM]
        out.append((qh, rq, qn, kn, v, _xattn_probs(qn, kn)))
    return out


def mid_fwd(mixed, x0, w_out, g_x, wq, kv, xqg, xkg, wo, g_f, *, tm=512):
    m, d = x0.shape
    n_mem = kv.shape[0]

    def body(mixed_ref, x0_ref, w_out_ref, g_x_ref, wq_ref, kv_ref, xqg_ref, xkg_ref, wo_ref, g_f_ref,
             x1_ref, h1_ref, qx_ref, o_ref, x2_ref, h2_ref):
        x1 = x0_ref[...] + _dot_nn(mixed_ref[...], w_out_ref[...])
        x1_ref[...] = x1
        h1 = (_rms(x1)[0] * g_x_ref[...]).astype(BF16)
        h1_ref[...] = h1
        qx_ref[...] = _dot_nn(h1, wq_ref[...])
        for h, (_, _, _, _, v, p) in enumerate(_xattn_heads(qx_ref, kv_ref, xqg_ref[...], xkg_ref[...], d)):
            o_ref[:, h * X_HEAD_DIM:(h + 1) * X_HEAD_DIM] = _dot_nn(p, v).astype(o_ref.dtype)
        x2 = x1 + _dot_nn(o_ref[...], wo_ref[...])
        x2_ref[...] = x2
        h2_ref[...] = (_rms(x2)[0] * g_f_ref[...]).astype(BF16)

    sq = _whole((d, d))
    f32_rows, bf_rows = jax.ShapeDtypeStruct((m, d), F32), jax.ShapeDtypeStruct((m, d), BF16)
    return pl.pallas_call(
        body, name="mid_fwd", grid=(m // tm,),
        in_specs=[_rows(tm, d), _rows(tm, d), sq, _vec(d), sq, _whole((n_mem, 2 * d)), _vec(X_HEAD_DIM), _vec(X_HEAD_DIM),
                  sq, _vec(d)],
        out_specs=[_rows(tm, d)] * 6,
        out_shape=[f32_rows, bf_rows, f32_rows, bf_rows, f32_rows, bf_rows],
        compiler_params=_cp(),
    )(mixed, x0, w_out, g_x.reshape(1, d), wq, kv, xqg.reshape(1, X_HEAD_DIM), xkg.reshape(1, X_HEAD_DIM), wo,
      g_f.reshape(1, d))


def ffn_fwd(h2, x2, wt_gu, w_down, *, tm=256):
    m, d = x2.shape
    f = w_down.shape[0]

    def body(h2_ref, x2_ref, wt_gu_ref, w_down_ref, gu_ref, a_ref, x3_ref):
        h = h2_ref[...]
        for s, sz in _chunks(2 * f):
            gu_ref[:, s:s + sz] = _dot_nt(h, wt_gu_ref[s:s + sz, :])
        for s, sz in _chunks(f):
            g = gu_ref[:, s:s + sz]
            a_ref[:, s:s + sz] = (g * _sigmoid(g) * gu_ref[:, f + s:f + s + sz]).astype(a_ref.dtype)
        x3_ref[...] = x2_ref[...] + _dot_nn(a_ref[...], w_down_ref[...])

    return pl.pallas_call(
        body, name="ffn_fwd", grid=(m // tm,),
        in_specs=[_rows(tm, d), _rows(tm, d), _whole((2 * f, d)), _whole((f, d))],
        out_specs=[_rows(tm, 2 * f), _rows(tm, f), _rows(tm, d)],
        out_shape=[jax.ShapeDtypeStruct((m, 2 * f), F32), jax.ShapeDtypeStruct((m, f), BF16),
                   jax.ShapeDtypeStruct((m, d), F32)],
        compiler_params=_cp(),
    )(h2, x2, wt_gu, w_down)


def ffn_bwd(dx3, gu, x2, g_f, w_down, wt_gu, *, tm=256):
    m, d = x2.shape
    f = w_down.shape[0]

    def body(dx3_ref, gu_ref, x2_ref, g_ref, w_down_ref, wt_gu_ref, dgu_ref, dx2_ref, dg_ref):
        _zero_at_first_step(dg_ref)
        dx3 = dx3_ref[...]
        dx3_b = dx3.astype(BF16)
        for s, sz in _chunks(f):
            da = _dot_nt(dx3_b, w_down_ref[s:s + sz, :])
            g = gu_ref[:, s:s + sz]
            u = gu_ref[:, f + s:f + s + sz]
            sg = _sigmoid(g)
            dgu_ref[:, s:s + sz] = (da * u * (sg * (1.0 + g * (1.0 - sg)))).astype(dgu_ref.dtype)
            dgu_ref[:, f + s:f + s + sz] = (da * (g * sg)).astype(dgu_ref.dtype)
        dh2 = _dot_nn(dgu_ref[...], wt_gu_ref[...])
        xh, r = _rms(x2_ref[...])
        dg_ref[...] += jnp.sum(dh2 * xh, axis=0, keepdims=True)
        dx2_ref[...] = dx3 + _rms_bwd(dh2, xh, r, g_ref[...])

    return pl.pallas_call(
        body, name="ffn_bwd", grid=(m // tm,),
        in_specs=[_rows(tm, d), _rows(tm, 2 * f), _rows(tm, d), _vec(d), _whole((f, d)), _whole((2 * f, d))],
        out_specs=[_rows(tm, 2 * f), _rows(tm, d), _vec(d)],
        out_shape=[jax.ShapeDtypeStruct((m, 2 * f), BF16), jax.ShapeDtypeStruct((m, d), F32),
                   jax.ShapeDtypeStruct((1, d), F32)],
        compiler_params=_cp(),
    )(dx3, gu, x2, g_f.reshape(1, d), w_down, wt_gu)


def mid_bwd(dx2, qx, kv, xqg, xkg, x1, g_x, wo, wq, w_out, *, tm=512):
    m, d = x1.shape
    n_mem = kv.shape[0]
    nt = m // tm

    def body(dx2_ref, qx_ref, kv_ref, xqg_ref, xkg_ref, x1_ref, g_x_ref, wo_ref, wq_ref, w_out_ref,
             dq_ref, dx1_ref, dmixed_ref, dkv_ref, dqg_ref, dkg_ref, dg_ref):
        i = pl.program_id(0)
        _zero_at_first_step(dkv_ref, dqg_ref, dkg_ref, dg_ref)
        qg_v, kg_v = xqg_ref[...], xkg_ref[...]
        dx2 = dx2_ref[...]
        do = _dot_nt(dx2, wo_ref[...])
        dqg_acc = jnp.zeros((1, X_HEAD_DIM), F32)
        for h, (qh, rq, qn, kn, v, p) in enumerate(_xattn_heads(qx_ref, kv_ref, qg_v, kg_v, d)):
            cols = slice(h * X_HEAD_DIM, (h + 1) * X_HEAD_DIM)
            vcols = slice(d + h * X_HEAD_DIM, d + (h + 1) * X_HEAD_DIM)
            do_h = do[:, cols]
            dp = _dot_nt(do_h, v)
            ds = p * (dp - jnp.sum(p * dp, axis=-1, keepdims=True))
            dkv_ref[:, vcols] += _dot_tn(p, do_h)
            dqn = _dot_nn(ds, kn) * (X_HEAD_DIM ** -0.5)
            dkv_ref[:, cols] += _dot_tn(ds, qn) * (X_HEAD_DIM ** -0.5)
            dqg_acc = dqg_acc + jnp.sum(dqn * qh, axis=0, keepdims=True)
            dq_ref[:, cols] = _rms_bwd(dqn, qh, rq, qg_v).astype(dq_ref.dtype)
        dqg_ref[...] += dqg_acc
        dh1 = _dot_nt(dq_ref[...], wq_ref[...])
        xh, r = _rms(x1_ref[...])
        dg_ref[...] += jnp.sum(dh1 * xh, axis=0, keepdims=True)
        dx1 = dx2 + _rms_bwd(dh1, xh, r, g_x_ref[...])
        dx1_ref[...] = dx1
        dmixed_ref[...] = _dot_nt(dx1, w_out_ref[...])

        @pl.when(i == nt - 1)
        def _():
            dkg_acc = jnp.zeros((1, X_HEAD_DIM), F32)
            for h in range(N_X_HEADS):
                cols = slice(h * X_HEAD_DIM, (h + 1) * X_HEAD_DIM)
                kh, rk = _rms(kv_ref[:, cols])
                dkn = dkv_ref[:, cols]
                dkg_acc = dkg_acc + jnp.sum(dkn * kh, axis=0, keepdims=True)
                dkv_ref[:, cols] = _rms_bwd(dkn, kh, rk, kg_v)
            dkg_ref[...] = dkg_acc

    sq = _whole((d, d))
    full = pl.BlockSpec((n_mem, 2 * d), lambda i: (0, 0))
    return pl.pallas_call(
        body, name="mid_bwd", grid=(nt,),
        in_specs=[_rows(tm, d), _rows(tm, d), _whole((n_mem, 2 * d)), _vec(X_HEAD_DIM), _vec(X_HEAD_DIM), _rows(tm, d),
                  _vec(d), sq, sq, sq],
        out_specs=[_rows(tm, d), _rows(tm, d), _rows(tm, d), full, _vec(X_HEAD_DIM), _vec(X_HEAD_DIM), _vec(d)],
        out_shape=[jax.ShapeDtypeStruct((m, d), BF16), jax.ShapeDtypeStruct((m, d), F32), jax.ShapeDtypeStruct((m, d), F32),
                   jax.ShapeDtypeStruct((n_mem, 2 * d), F32), jax.ShapeDtypeStruct((1, X_HEAD_DIM), F32),
                   jax.ShapeDtypeStruct((1, X_HEAD_DIM), F32), jax.ShapeDtypeStruct((1, d), F32)],
        compiler_params=_cp(),
    )(dx2, qx, kv, xqg.reshape(1, X_HEAD_DIM), xkg.reshape(1, X_HEAD_DIM), x1, g_x.reshape(1, d), wo, wq, w_out)


def in_bwd(du, wt_in, x0, g_mix, dx1, *, tm=512):
    m, d = x0.shape
    n = wt_in.shape[0]

    def body(du_ref, wt_ref, x0_ref, g_ref, dx1_ref, dx0_ref, dg_ref):
        _zero_at_first_step(dg_ref)
        dh0 = _dot_nn(du_ref[...], wt_ref[...])
        xh, r = _rms(x0_ref[...])
        dg_ref[...] += jnp.sum(dh0 * xh, axis=0, keepdims=True)
        dx0_ref[...] = dx1_ref[...] + _rms_bwd(dh0, xh, r, g_ref[...])

    return pl.pallas_call(
        body, name="in_bwd", grid=(m // tm,),
        in_specs=[_rows(tm, n), _whole((n, d)), _rows(tm, d), _vec(d), _rows(tm, d)],
        out_specs=[_rows(tm, d), _vec(d)],
        out_shape=[jax.ShapeDtypeStruct((m, d), F32), jax.ShapeDtypeStruct((1, d), F32)],
        compiler_params=_cp(),
    )(du, wt_in, x0, g_mix.reshape(1, d), dx1)


SWA_TILE = 512
SWA_SUB = SWA_TILE // BLOCK
SWA_KEYS = SWA_TILE + BLOCK
PAIR = 2 * HEAD_DIM
KCOL = ATTN_WIDTH
VCOL = ATTN_WIDTH + KV_WIDTH


def _swa_constants():
    r = np.arange(2 * BLOCK)[:, None]
    j = np.arange(4 * BLOCK)[None, :]
    dist = (r % BLOCK) + BLOCK - (j % (2 * BLOCK))
    valid = (dist >= 0) & (dist < BLOCK)
    first_valid = valid & ((j % (2 * BLOCK)) >= BLOCK)
    bias, bias_first = [], []
    for kv in range(N_KV_HEADS):
        head = kv * GROUP + 2 * (r // BLOCK) + j // (2 * BLOCK)
        b = -(2.0 ** -(head + 1.0)) * dist
        bias.append(np.where(valid, b, NEG))
        bias_first.append(np.where(first_valid, b, NEG))
    lane = np.arange(LANES)
    seg = (lane[:, None] // HEAD_DIM == lane[None, :] // HEAD_DIM) / HEAD_DIM
    row = np.arange(4 * BLOCK)[:, None]
    ones = (row // (2 * BLOCK)) == (lane[None, :] // HEAD_DIM)
    return (jnp.asarray(np.stack(bias), F32), jnp.asarray(np.stack(bias_first), F32), jnp.asarray(seg, BF16),
            jnp.asarray(ones, BF16))


def _segmean(x, seg_ref):
    hi = x.astype(BF16)
    lo = (x - hi.astype(F32)).astype(BF16)
    return _dot_nn(hi, seg_ref[...]) + _dot_nn(lo, seg_ref[...])


def _two_heads(x, kv):
    lane = lax.broadcasted_iota(jnp.int32, (1, LANES), 1)
    mine = (lane < HEAD_DIM) if kv == 0 else (lane >= HEAD_DIM)
    base = jnp.where(mine, x, 0.0)
    other = pltpu.roll(base, HEAD_DIM, 1)
    return jnp.concatenate([base, other] if kv == 0 else [other, base], axis=0)


def _from_two_heads(y, kv):
    rows = y.shape[0] // 2
    lane = lax.broadcasted_iota(jnp.int32, (1, LANES), 1)
    top, bot = y[:rows], y[rows:]
    if kv == 0:
        return jnp.where(lane < HEAD_DIM, top + pltpu.roll(bot, HEAD_DIM, 1), 0.0)
    return jnp.where(lane >= HEAD_DIM, pltpu.roll(top, HEAD_DIM, 1) + bot, 0.0)


def _pair_rows(ref, rows, kv):
    c = kv * 2 * PAIR
    return jnp.concatenate([ref[rows, c:c + PAIR], ref[rows, c + PAIR:c + 2 * PAIR]], axis=0)


def _head_cols(fn, kv):
    return [jnp.concatenate([fn(kv * GROUP + half), fn(kv * GROUP + 2 + half)], axis=0) for half in range(2)]


def _swa_prologue(cur_ref, prev_ref, qg_ref, kg_ref, seg_ref, qg_s, kn_s, v_s):
    qg_s[...] = (cur_ref[:, 0:ATTN_WIDTH] * qg_ref[...]).astype(BF16)
    k = jnp.concatenate([prev_ref[:, KCOL:KCOL + KV_WIDTH], cur_ref[:, KCOL:KCOL + KV_WIDTH]], axis=0)
    kn_s[...] = k * lax.rsqrt(_segmean(k * k, seg_ref) + EPS) * kg_ref[...]
    v_s[0:BLOCK, :] = prev_ref[:, VCOL:VCOL + KV_WIDTH]
    v_s[BLOCK:SWA_KEYS, :] = cur_ref[:, VCOL:VCOL + KV_WIDTH]


def _swa_scores(cur_ref, sinks_ref, qg_s, kn_s, bias, rows, keys, kv):
    q2 = _pair_rows(qg_s, rows, kv)
    k2 = _two_heads(kn_s[keys, :], kv)
    t = _dot_nt(q2, k2)

    def rq(h):
        x = cur_ref[rows, h * HEAD_DIM:(h + 1) * HEAD_DIM]
        return lax.rsqrt(jnp.mean(x * x, axis=-1, keepdims=True) + EPS)

    scale = _head_cols(lambda h: rq(h) * (HEAD_DIM ** -0.5), kv)
    sink = _head_cols(lambda h: jnp.full((BLOCK, 1), sinks_ref[h], F32), kv)
    halves = []
    for half in range(2):
        cols = slice(half * 2 * BLOCK, (half + 1) * 2 * BLOCK)
        s = t[:, cols] * scale[half] + bias[:, cols]
        mx = jnp.maximum(jnp.max(s, axis=-1, keepdims=True), sink[half])
        halves.append((scale[half], jnp.exp(s - mx), jnp.exp(sink[half] - mx)))
    return q2, k2, t, halves


def swa_fwd(u, qg, kg, sinks):
    t_rows = u.shape[0]
    nt = t_rows // SWA_TILE
    bias_c, bias_first_c, seg_c, ones_c = _swa_constants()

    def body(sinks_ref, cur_ref, prev_ref, qg_ref, kg_ref, seg_ref, bias_ref, biasf_ref, ones_ref, o_ref, qg_s, kn_s, v_s):
        i = pl.program_id(0)
        _swa_prologue(cur_ref, prev_ref, qg_ref, kg_ref, seg_ref, qg_s, kn_s, v_s)
        lane = lax.broadcasted_iota(jnp.int32, (1, LANES), 1)
        for b in range(SWA_SUB):
            rows = slice(b * BLOCK, (b + 1) * BLOCK)
            keys = slice(b * BLOCK, (b + 2) * BLOCK)
            for kv in range(N_KV_HEADS):
                bias = jnp.where(i == 0, biasf_ref[kv], bias_ref[kv]) if b == 0 else bias_ref[kv]
                _, _, _, halves = _swa_scores(cur_ref, sinks_ref, qg_s, kn_s, bias, rows, keys, kv)
                e = jnp.concatenate([halves[0][1], halves[1][1]], axis=1)
                v2 = jnp.concatenate([_two_heads(v_s[keys, :], kv).astype(BF16), ones_ref[...]], axis=1)
                ox = _dot_nn(e, v2)
                den = ox[:, LANES:] + jnp.where(lane < HEAD_DIM, halves[0][2], halves[1][2])
                out = (ox[:, :LANES] / den).astype(o_ref.dtype)
                c = kv * 2 * PAIR
                o_ref[rows, c:c + PAIR] = out[:BLOCK]
                o_ref[rows, c + PAIR:c + 2 * PAIR] = out[BLOCK:]

    const3 = pl.BlockSpec((N_KV_HEADS, 2 * BLOCK, 4 * BLOCK), lambda i: (0, 0, 0))
    return pl.pallas_call(
        body, name="swa_fwd", grid=(nt,),
        in_specs=[
            pl.BlockSpec(memory_space=pltpu.SMEM),
            pl.BlockSpec((SWA_TILE, QKV_WIDTH), lambda i: (i, 0)),
            pl.BlockSpec((BLOCK, QKV_WIDTH), lambda i: (jnp.maximum(i * SWA_SUB - 1, 0), 0)),
            pl.BlockSpec((1, ATTN_WIDTH), lambda i: (0, 0)), pl.BlockSpec((1, KV_WIDTH), lambda i: (0, 0)),
            pl.BlockSpec((LANES, LANES), lambda i: (0, 0)), const3, const3,
            pl.BlockSpec((4 * BLOCK, LANES), lambda i: (0, 0)),
        ],
        out_specs=pl.BlockSpec((SWA_TILE, ATTN_WIDTH), lambda i: (i, 0)),
        out_shape=jax.ShapeDtypeStruct((t_rows, 2 * ATTN_WIDTH), BF16),
        scratch_shapes=[pltpu.VMEM((SWA_TILE, ATTN_WIDTH), BF16), pltpu.VMEM((SWA_KEYS, KV_WIDTH), F32),
                        pltpu.VMEM((SWA_KEYS, KV_WIDTH), F32)],
        compiler_params=_cp(),
    )(sinks, u, u, jnp.tile(qg, N_Q_HEADS).reshape(1, ATTN_WIDTH), jnp.tile(kg, N_KV_HEADS).reshape(1, KV_WIDTH),
      seg_c, bias_c, bias_first_c, ones_c)


def swa_bwd(u, dmixed, qg, kg, sinks):
    t_rows = u.shape[0]
    nt = t_rows // SWA_TILE
    bias_c, bias_first_c, seg_c, _ = _swa_constants()

    def body(sinks_ref, cur_ref, prev_ref, do_ref, qg_ref, kg_ref, seg_ref, bias_ref, biasf_ref,
             du_ref, dqg_ref, dkg_ref, dsk_ref, qg_s, kn_s, v_s, acck_s, accv_s, carryk_s, carryv_s):
        step = pl.program_id(0)
        i = nt - 1 - step

        @pl.when(step == 0)
        def _():
            for r in (carryk_s, carryv_s, dqg_ref, dkg_ref, dsk_ref):
                r[...] = jnp.zeros_like(r)

        _swa_prologue(cur_ref, prev_ref, qg_ref, kg_ref, seg_ref, qg_s, kn_s, v_s)
        for acc, carry in ((acck_s, carryk_s), (accv_s, carryv_s)):
            acc[0:SWA_TILE, :] = jnp.zeros((SWA_TILE, KV_WIDTH), F32)
            acc[SWA_TILE:SWA_KEYS, :] = carry[...]

        lane = lax.broadcasted_iota(jnp.int32, (1, LANES), 1)
        g_pair = qg_ref[:, 0:PAIR]
        dqg_acc = jnp.zeros((1, PAIR), F32)
        dsk_acc = jnp.zeros((1, LANES), F32)
        for b in range(SWA_SUB):
            rows = slice(b * BLOCK, (b + 1) * BLOCK)
            keys = slice(b * BLOCK, (b + 2) * BLOCK)
            for kv in range(N_KV_HEADS):
                bias = jnp.where(i == 0, biasf_ref[kv], bias_ref[kv]) if b == 0 else bias_ref[kv]
                q2, k2, t, halves = _swa_scores(cur_ref, sinks_ref, qg_s, kn_s, bias, rows, keys, kv)
                v2 = _two_heads(v_s[keys, :], kv)
                do2 = _pair_rows(do_ref, rows, kv)
                dp = _dot_nt(do2, v2)
                p_parts, dt_parts, coef = [], [], []
                for half, (scale, e, es) in enumerate(halves):
                    cols = slice(half * 2 * BLOCK, (half + 1) * 2 * BLOCK)
                    rden = 1.0 / (jnp.sum(e, axis=-1, keepdims=True) + es)
                    p = e * rden
                    dp_h = dp[:, cols]
                    delta = jnp.sum(p * dp_h, axis=-1, keepdims=True)
                    ds = p * (dp_h - delta)
                    dsink = -(es * rden) * delta
                    for pair in range(2):
                        part = jnp.sum(dsink[pair * BLOCK:(pair + 1) * BLOCK], axis=0, keepdims=True)
                        dsk_acc = dsk_acc + jnp.where(lane == kv * GROUP + 2 * pair + half, part, 0.0)
                    dscale = jnp.sum(ds * t[:, cols], axis=-1, keepdims=True)
                    coef.append(-dscale * scale * scale * scale)
                    p_parts.append(p)
                    dt_parts.append(ds * scale)
                p2 = jnp.concatenate(p_parts, axis=1)
                dt = jnp.concatenate(dt_parts, axis=1)
                dqg2 = _dot_nn(dt, k2)
                q_raw = _pair_rows(cur_ref, rows, kv)
                dq = dqg2 * g_pair + jnp.where(lane < HEAD_DIM, coef[0], coef[1]) * q_raw
                dqg_acc = dqg_acc + jnp.sum(dqg2 * q_raw, axis=0, keepdims=True)
                c = kv * 2 * PAIR
                du_ref[rows, c:c + PAIR] = dq[:BLOCK].astype(du_ref.dtype)
                du_ref[rows, c + PAIR:c + 2 * PAIR] = dq[BLOCK:].astype(du_ref.dtype)
                acck_s[keys, :] += _from_two_heads(_dot_tn(dt, q2), kv)
                accv_s[keys, :] += _from_two_heads(_dot_tn(p2, do2), kv)
        dqg_ref[...] += dqg_acc + pltpu.roll(dqg_acc, HEAD_DIM, 1)
        dsk_ref[...] += dsk_acc

        own = slice(BLOCK, SWA_KEYS)
        k = cur_ref[:, KCOL:KCOL + KV_WIDTH]
        rk = lax.rsqrt(_segmean(k * k, seg_ref) + EPS)
        kh = k * rk
        dkn = acck_s[own, :]
        dkh = dkn * kg_ref[...]
        du_ref[:, KCOL:KCOL + KV_WIDTH] = (rk * (dkh - kh * _segmean(dkh * kh, seg_ref))).astype(du_ref.dtype)
        du_ref[:, VCOL:VCOL + KV_WIDTH] = accv_s[own, :].astype(du_ref.dtype)
        dkg_part = jnp.sum(dkn * kh, axis=0, keepdims=True)
        dkg_ref[...] += dkg_part + pltpu.roll(dkg_part, HEAD_DIM, 1)
        carryk_s[...] = acck_s[0:BLOCK, :]
        carryv_s[...] = accv_s[0:BLOCK, :]

    const3 = pl.BlockSpec((N_KV_HEADS, 2 * BLOCK, 4 * BLOCK), lambda s: (0, 0, 0))
    vec = pl.BlockSpec((1, LANES), lambda s: (0, 0))
    return pl.pallas_call(
        body, name="swa_bwd", grid=(nt,),
        in_specs=[
            pl.BlockSpec(memory_space=pltpu.SMEM),
            pl.BlockSpec((SWA_TILE, QKV_WIDTH), lambda s: (nt - 1 - s, 0)),
            pl.BlockSpec((BLOCK, QKV_WIDTH), lambda s: (jnp.maximum((nt - 1 - s) * SWA_SUB - 1, 0), 0)),
            pl.BlockSpec((SWA_TILE, ATTN_WIDTH), lambda s: (nt - 1 - s, 0)),
            pl.BlockSpec((1, ATTN_WIDTH), lambda s: (0, 0)), vec,
            pl.BlockSpec((LANES, LANES), lambda s: (0, 0)), const3, const3,
        ],
        out_specs=[pl.BlockSpec((SWA_TILE, QKV_WIDTH), lambda s: (nt - 1 - s, 0)), vec, vec, vec],
        out_shape=[jax.ShapeDtypeStruct((t_rows, IN_COLS), BF16)] + [jax.ShapeDtypeStruct((1, LANES), F32)] * 3,
        scratch_shapes=[pltpu.VMEM((SWA_TILE, ATTN_WIDTH), BF16)] + [pltpu.VMEM((SWA_KEYS, KV_WIDTH), F32)] * 4
        + [pltpu.VMEM((BLOCK, KV_WIDTH), F32)] * 2,
        compiler_params=_cp(),
    )(sinks, u, u, dmixed, jnp.tile(qg, N_Q_HEADS).reshape(1, ATTN_WIDTH), jnp.tile(kg, N_KV_HEADS).reshape(1, KV_WIDTH),
      seg_c, bias_c, bias_first_c)


CONV_TILE = 512
CONV_CHUNK = 64
VAL0 = QKV_WIDTH
GATE0 = QKV_WIDTH + CONV_CH


def _glu(ref):
    return ref[:, VAL0:GATE0] * _sigmoid(ref[:, GATE0:GATE0 + CONV_CH])


SUBLANES = 8
CONV_BUF = CONV_HALO + CONV_TILE + SUBLANES
CONV_EXT = CONV_HALO + CONV_TILE


def _fill_shifted(sh_ref):
    for r in range(1, SUBLANES):
        sh_ref[r, 0:CONV_EXT, :] = sh_ref[0, pl.ds(r, CONV_EXT), :]


def _shifted(sh_ref, start, offset, n):
    return sh_ref[offset % SUBLANES, pl.ds(start + offset - offset % SUBLANES, n), :]


def _layernorm_stats(y):
    mu = jnp.mean(y, axis=-1, keepdims=True)
    yc = y - mu
    rstd = lax.rsqrt(jnp.mean(yc * yc, axis=-1, keepdims=True) + EPS)
    return yc * rstd, rstd


def conv_fwd(u, mixed, conv_w, conv_b, ln_g, ln_b):
    t = u.shape[0]
    nt = t // CONV_TILE
    per = CONV_TILE // CONV_HALO

    def body(cur_ref, prev_ref, mixed_ref, w_ref, b_ref, g_ref, b2_ref, o_ref, y_ref, gl_ref):
        del mixed_ref
        i = pl.program_id(0)
        gl_ref[0, 0:CONV_HALO, :] = jnp.where(i > 0, _glu(prev_ref), 0.0)
        gl_ref[0, CONV_HALO:CONV_EXT, :] = _glu(cur_ref)
        gl_ref[0, CONV_EXT:CONV_BUF, :] = jnp.zeros((SUBLANES, CONV_CH), F32)
        _fill_shifted(gl_ref)
        for c0 in range(0, CONV_TILE, CONV_CHUNK):
            acc = jnp.broadcast_to(b_ref[...], (CONV_CHUNK, CONV_CH))
            for k in range(CONV_K):
                acc = acc + w_ref[k:k + 1, :] * _shifted(gl_ref, c0, 2 + k, CONV_CHUNK)
            y_ref[c0:c0 + CONV_CHUNK, :] = acc
        yh, _ = _layernorm_stats(y_ref[...])
        yln = yh * g_ref[...] + b2_ref[...]
        o_ref[...] = (yln * _sigmoid(yln)).astype(o_ref.dtype)

    vec = pl.BlockSpec((1, CONV_CH), lambda i: (0, 0))
    return pl.pallas_call(
        body, name="conv_fwd", grid=(nt,),
        in_specs=[
            pl.BlockSpec((CONV_TILE, IN_COLS), lambda i: (i, 0)),
            pl.BlockSpec((CONV_HALO, IN_COLS), lambda i: (jnp.maximum(i * per - 1, 0), 0)),
            pl.BlockSpec(memory_space=pl.ANY),
            pl.BlockSpec((CONV_HALO, CONV_CH), lambda i: (0, 0)),
            vec, vec, vec,
        ],
        out_specs=[pl.BlockSpec((CONV_TILE, CONV_CH), lambda i: (i, 1)), pl.BlockSpec((CONV_TILE, CONV_CH), lambda i: (i, 0))],
        out_shape=[jax.ShapeDtypeStruct(mixed.shape, mixed.dtype), jax.ShapeDtypeStruct((t, CONV_CH), F32)],
        scratch_shapes=[pltpu.VMEM((SUBLANES, CONV_BUF, CONV_CH), F32)],
        input_output_aliases={2: 0}, compiler_params=_cp(),
    )(u, u, mixed, conv_w, conv_b.reshape(1, CONV_CH), ln_g.reshape(1, CONV_CH), ln_b.reshape(1, CONV_CH))


def conv_bwd(u, y, dmixed, du, conv_w, ln_g, ln_b):
    t = u.shape[0]
    nt = t // CONV_TILE
    per = CONV_TILE // CONV_HALO

    def body(cur_ref, prev_ref, y_ref, yn_ref, do_ref, don_ref, du_in_ref, w_ref, g_ref, b2_ref,
             du_ref, dw_ref, dvec_ref, gl_ref, dy_ref):
        i = pl.program_id(0)
        last = i == nt - 1
        _zero_at_first_step(dw_ref, dvec_ref)

        gl_ref[0, 0:CONV_HALO, :] = jnp.where(i > 0, _glu(prev_ref), 0.0)
        gl_ref[0, CONV_HALO:CONV_EXT, :] = _glu(cur_ref)
        gl_ref[0, CONV_EXT:CONV_BUF, :] = jnp.zeros((SUBLANES, CONV_CH), F32)
        _fill_shifted(gl_ref)

        yh, rstd = _layernorm_stats(jnp.concatenate([y_ref[...], yn_ref[...]], axis=0))
        g = g_ref[...]
        yln = yh * g + b2_ref[...]
        sg = _sigmoid(yln)
        dout = jnp.concatenate([do_ref[...], jnp.where(last, 0.0, don_ref[...])], axis=0)
        dyln = dout * (sg * (1.0 + yln * (1.0 - sg)))
        dyh = dyln * g
        dy = rstd * (dyh - jnp.mean(dyh, axis=-1, keepdims=True) - yh * jnp.mean(dyh * yh, axis=-1, keepdims=True))
        dy_ref[0, 0:CONV_EXT, :] = dy
        dy_ref[0, CONV_EXT:CONV_BUF, :] = jnp.zeros((SUBLANES, CONV_CH), F32)
        _fill_shifted(dy_ref)

        own = slice(0, CONV_TILE)
        dvec_ref[0:1, :] += jnp.sum(dy[own], axis=0, keepdims=True)
        dvec_ref[1:2, :] += jnp.sum(dyln[own] * yh[own], axis=0, keepdims=True)
        dvec_ref[2:3, :] += jnp.sum(dyln[own], axis=0, keepdims=True)
        for k in range(CONV_K):
            dw_ref[k:k + 1, :] += jnp.sum(dy[own] * _shifted(gl_ref, 0, 2 + k, CONV_TILE), axis=0, keepdims=True)

        for c0 in range(0, CONV_TILE, CONV_CHUNK):
            acc = jnp.zeros((CONV_CHUNK, CONV_CH), F32)
            for k in range(CONV_K):
                acc = acc + w_ref[k:k + 1, :] * _shifted(dy_ref, c0, CONV_K - 1 - k, CONV_CHUNK)
            rows = slice(c0, c0 + CONV_CHUNK)
            val = cur_ref[rows, VAL0:GATE0]
            sgate = _sigmoid(cur_ref[rows, GATE0:GATE0 + CONV_CH])
            du_ref[rows, VAL0:GATE0] = (acc * sgate).astype(du_ref.dtype)
            du_ref[rows, GATE0:GATE0 + CONV_CH] = (acc * val * sgate * (1.0 - sgate)).astype(du_ref.dtype)
        du_ref[:, 0:QKV_WIDTH] = du_in_ref[:, 0:QKV_WIDTH]

    vec = pl.BlockSpec((1, CONV_CH), lambda i: (0, 0))
    n_halo = t // CONV_HALO
    return pl.pallas_call(
        body, name="conv_bwd", grid=(nt,),
        in_specs=[
            pl.BlockSpec((CONV_TILE, IN_COLS), lambda i: (i, 0)),
            pl.BlockSpec((CONV_HALO, IN_COLS), lambda i: (jnp.maximum(i * per - 1, 0), 0)),
            pl.BlockSpec((CONV_TILE, CONV_CH), lambda i: (i, 0)),
            pl.BlockSpec((CONV_HALO, CONV_CH), lambda i: (jnp.minimum((i + 1) * per, n_halo - 1), 0)),
            pl.BlockSpec((CONV_TILE, CONV_CH), lambda i: (i, 1)),
            pl.BlockSpec((CONV_HALO, CONV_CH), lambda i: (jnp.minimum((i + 1) * per, n_halo - 1), 1)),
            pl.BlockSpec((CONV_TILE, IN_COLS), lambda i: (i, 0)),
            pl.BlockSpec((CONV_HALO, CONV_CH), lambda i: (0, 0)),
            vec, vec,
        ],
        out_specs=[
            pl.BlockSpec((CONV_TILE, IN_COLS), lambda i: (i, 0)),
            pl.BlockSpec((CONV_HALO, CONV_CH), lambda i: (0, 0)),
            pl.BlockSpec((8, CONV_CH), lambda i: (0, 0)),
        ],
        out_shape=[
            jax.ShapeDtypeStruct(du.shape, du.dtype),
            jax.ShapeDtypeStruct((CONV_HALO, CONV_CH), F32),
            jax.ShapeDtypeStruct((8, CONV_CH), F32),
        ],
        scratch_shapes=[pltpu.VMEM((SUBLANES, CONV_BUF, CONV_CH), F32), pltpu.VMEM((SUBLANES, CONV_BUF, CONV_CH), F32)],
        input_output_aliases={6: 0}, compiler_params=_cp(),
    )(u, u, y, y, dmixed, dmixed, du, conv_w, ln_g.reshape(1, CONV_CH), ln_b.reshape(1, CONV_CH))


def _xattn_probs(qn, kn):
    s = _dot_nt(qn, kn) * (X_HEAD_DIM ** -0.5)
    e = jnp.exp(s - jnp.max(s, axis=-1, keepdims=True))
    return e / jnp.sum(e, axis=-1, keepdims=True)


def adamw(w, g, m, v, *, name):
    r, c = w.shape
    tr = r
    for cand in (512, 256, 128, 64, 32, 16, 8):
        if r % cand == 0 and r > cand:
            tr = cand
            break

    def body(w_ref, g_ref, m_ref, v_ref, d_ref, nm_ref, nv_ref):
        g_v = g_ref[...]
        m2 = ADAM_B1 * m_ref[...] + (1.0 - ADAM_B1) * g_v
        v2 = ADAM_B2 * v_ref[...] + (1.0 - ADAM_B2) * jnp.square(g_v)
        m_hat = m2 / (1.0 - ADAM_B1 ** ADAM_STEP)
        v_hat = v2 / (1.0 - ADAM_B2 ** ADAM_STEP)
        d_ref[...] = -ADAM_LR * (m_hat / (jnp.sqrt(v_hat) + ADAM_EPS) + ADAM_WD * w_ref[...])
        nm_ref[...] = m2
        nv_ref[...] = v2

    spec = pl.BlockSpec((tr, c), lambda i: (i, 0))
    shape = jax.ShapeDtypeStruct((r, c), F32)
    return pl.pallas_call(
        body, name=name, grid=(r // tr,), in_specs=[spec] * 4, out_specs=[spec] * 3,
        out_shape=[shape] * 3, compiler_params=_cp(),
    )(w, g, m, v)


def _position():
    return lax.axis_index("x"), lax.axis_index("y"), lax.axis_index("c")


def all_gather(shard, *, name, in_vmem):
    r, c_ = shard.shape

    def body(x_ref, out_ref, token_ref, send_sems, recv_sems, local_sem):
        x, y, c = _position()
        me, sibling = (x, y, c), (x, y, 1 - c)
        chips = [(1 - x, y), (x, 1 - y), (1 - x, 1 - y)]
        token_ref[...] = jnp.zeros_like(token_ref)

        def rows(px, py, pc):
            return out_ref.at[4 * px + 2 * py + pc]

        def copy(k, block, to, src=None):
            return pltpu.make_async_remote_copy(
                src_ref=rows(*block) if src is None else src, dst_ref=rows(*block),
                send_sem=send_sems.at[k], recv_sem=recv_sems.at[k], device_id=to, device_id_type=MESH)

        mine = pltpu.make_async_copy(x_ref, rows(*me), local_sem)
        mine.start()
        first = [copy(0, me, sibling, src=x_ref)]
        first += [copy(1 + j, me, (*chip, c), src=x_ref) for j, chip in enumerate(chips)]
        for cp in first:
            cp.start()
        passed = [copy(4 + j, (*chip, c), sibling) for j, chip in enumerate(chips)]
        for j, chip in enumerate(chips):
            copy(1 + j, (*chip, c), me).wait_recv()
            passed[j].start()
        copy(0, sibling, me).wait_recv()
        for j, chip in enumerate(chips):
            copy(4 + j, (*chip, 1 - c), me).wait_recv()
        for cp in first + passed:
            cp.wait_send()
        mine.wait()

    space = pltpu.VMEM if in_vmem else pltpu.HBM
    return pl.pallas_call(
        body, name=name,
        out_shape=[jax.ShapeDtypeStruct((N_DEV, r, c_), shard.dtype), jax.ShapeDtypeStruct((8, LANES), F32)],
        in_specs=[pl.BlockSpec(memory_space=space)],
        out_specs=[pl.BlockSpec(memory_space=space), pl.BlockSpec(memory_space=pltpu.VMEM)],
        scratch_shapes=[pltpu.SemaphoreType.DMA((7,)), pltpu.SemaphoreType.DMA((7,)), pltpu.SemaphoreType.DMA],
        compiler_params=_cp(),
    )(shard)


def all_gather_many(shards, *, name):
    n = len(shards)

    def body(*refs):
        x_refs, out_refs, token_ref = refs[:n], refs[n:2 * n], refs[2 * n]
        send_sems, recv_sems, local_sems = refs[2 * n + 1:]
        x, y, c = _position()
        me, sibling = (x, y, c), (x, y, 1 - c)
        chips = [(1 - x, y), (x, 1 - y), (1 - x, 1 - y)]
        token_ref[...] = jnp.zeros_like(token_ref)

        def rows(t, px, py, pc):
            return out_refs[t].at[4 * px + 2 * py + pc]

        def copy(t, k, block, to, src=None):
            return pltpu.make_async_remote_copy(
                src_ref=rows(t, *block) if src is None else src, dst_ref=rows(t, *block),
                send_sem=send_sems.at[7 * t + k], recv_sem=recv_sems.at[7 * t + k], device_id=to, device_id_type=MESH)

        mine = [pltpu.make_async_copy(x_refs[t], rows(t, *me), local_sems.at[t]) for t in range(n)]
        for cp in mine:
            cp.start()
        first = []
        for t in range(n):
            first.append(copy(t, 0, me, sibling, src=x_refs[t]))
            first += [copy(t, 1 + j, me, (*chip, c), src=x_refs[t]) for j, chip in enumerate(chips)]
        for cp in first:
            cp.start()
        passed = []
        for t in range(n):
            for j, chip in enumerate(chips):
                copy(t, 1 + j, (*chip, c), me).wait_recv()
                passed.append(copy(t, 4 + j, (*chip, c), sibling))
                passed[-1].start()
        for t in range(n):
            copy(t, 0, sibling, me).wait_recv()
            for j, chip in enumerate(chips):
                copy(t, 4 + j, (*chip, 1 - c), me).wait_recv()
        for cp in first + passed:
            cp.wait_send()
        for cp in mine:
            cp.wait()

    hbm = pl.BlockSpec(memory_space=pltpu.HBM)
    out = pl.pallas_call(
        body, name=name,
        out_shape=[jax.ShapeDtypeStruct((N_DEV,) + s.shape, s.dtype) for s in shards] + [jax.ShapeDtypeStruct((8, LANES), F32)],
        in_specs=[hbm] * n, out_specs=[hbm] * n + [pl.BlockSpec(memory_space=pltpu.VMEM)],
        scratch_shapes=[pltpu.SemaphoreType.DMA((7 * n,)), pltpu.SemaphoreType.DMA((7 * n,)), pltpu.SemaphoreType.DMA((n,))],
        compiler_params=_cp(),
    )(*shards)
    return out[:n], out[n]


def exchange(srcs, plan, n_copies, slots, *, name):
    n = len(srcs)

    def body(*refs):
        src_refs, out_refs, (send_sems, recv_sems) = refs[:n], refs[n:2 * n], refs[2 * n:]
        copies = []
        for t in range(n):
            for k in range(n_copies):
                s, d, to = plan(src_refs[t], out_refs[t], k)
                copies.append(pltpu.make_async_remote_copy(
                    src_ref=s, dst_ref=d, send_sem=send_sems.at[n_copies * t + k], recv_sem=recv_sems.at[n_copies * t + k],
                    device_id=to, device_id_type=MESH))
        for cp in copies:
            cp.start()
        for cp in copies:
            cp.wait_recv()
        for cp in copies:
            cp.wait_send()

    hbm = pl.BlockSpec(memory_space=pltpu.HBM)
    return pl.pallas_call(
        body, name=name,
        out_shape=[jax.ShapeDtypeStruct((slots,) + s.shape[1:], s.dtype) for s in srcs],
        in_specs=[hbm] * n, out_specs=[hbm] * n,
        scratch_shapes=[pltpu.SemaphoreType.DMA((n_copies * n,)), pltpu.SemaphoreType.DMA((n_copies * n,))],
        compiler_params=_cp(),
    )(*srcs)


_HBM = pl.BlockSpec(memory_space=pltpu.HBM)
_SEM = pl.BlockSpec(memory_space=pltpu.SEMAPHORE)
_EFFECT = pltpu.SideEffectType.DATAFLOW_SIDE_EFFECTING


def _split_copies(src_refs, land_refs, send_sems, recv_sems, plan, n_copies):
    copies = []
    for t, (src_ref, land_ref) in enumerate(zip(src_refs, land_refs)):
        for k in range(n_copies):
            s, d, to = plan(src_ref, land_ref, k)
            copies.append(pltpu.make_async_remote_copy(
                src_ref=s, dst_ref=d, send_sem=send_sems.at[n_copies * t + k], recv_sem=recv_sems.at[n_copies * t + k],
                device_id=to, device_id_type=MESH))
    return copies


def split_start(srcs, lands, plan, n_copies, *, name):
    n = len(srcs)

    def body(*refs):
        src_refs, land_refs, send_sems, recv_sems, token = refs[:n], refs[n:2 * n], refs[2 * n], refs[2 * n + 1], refs[-1]
        for cp in _split_copies(src_refs, land_refs, send_sems, recv_sems, plan, n_copies):
            cp.start()
        token[...] = jnp.zeros_like(token)

    both = list(srcs) + list(lands)
    out = pl.pallas_call(
        body, name=name,
        out_shape=(pltpu.SemaphoreType.DMA((n_copies * n,)), pltpu.SemaphoreType.DMA((n_copies * n,)),
                   *[pltpu.HBM(a.shape, a.dtype) for a in both], jax.ShapeDtypeStruct((8, LANES), F32)),
        in_specs=(_HBM,) * (2 * n), out_specs=(_SEM, _SEM) + (_HBM,) * (2 * n) + (pl.BlockSpec(memory_space=pltpu.VMEM),),
        input_output_aliases={i: 2 + i for i in range(2 * n)},
        compiler_params=pltpu.CompilerParams(has_side_effects=_EFFECT),
    )(*[pltpu.with_memory_space_constraint(a, pltpu.HBM) for a in both])
    return out[0], out[1], list(out[2:2 + n]), list(out[2 + n:2 + 2 * n]), out[-1]


def split_wait(started, after, plan, n_copies, *, name):
    send_sems, recv_sems, srcs, lands, _ = started
    n = len(srcs)

    def body(*refs):
        src_refs, land_refs, send_sems, recv_sems = refs[:n], refs[n:2 * n], refs[2 * n], refs[2 * n + 1]
        for cp in _split_copies(src_refs, land_refs, send_sems, recv_sems, plan, n_copies):
            cp.wait_send()
            cp.wait_recv()

    both = list(srcs) + list(lands)
    out = pl.pallas_call(
        body, name=name,
        out_shape=tuple(pltpu.HBM(a.shape, a.dtype) for a in both),
        in_specs=(_HBM,) * (2 * n) + (_SEM, _SEM, pl.BlockSpec(memory_space=pl.ANY)), out_specs=(_HBM,) * (2 * n),
        input_output_aliases={i: i for i in range(2 * n)},
        compiler_params=pltpu.CompilerParams(has_side_effects=_EFFECT),
    )(*both, send_sems, recv_sems, after)
    return list(out[:n]), list(out[n:])


def _gather_plan(src_ref, land_ref, k):
    x, y, c = _position()
    bits = k + 1
    peer = ((1 - x) if bits & 4 else x, (1 - y) if bits & 2 else y, (1 - c) if bits & 1 else c)
    return src_ref, land_ref.at[4 * x + 2 * y + c], peer


def _sibling_plan(src_ref, land_ref, k):
    x, y, c = _position()
    return src_ref.at[2 * k + (1 - c)], land_ref.at[k], (x, y, 1 - c)


def _chips_plan(src_ref, land_ref, j):
    x, y, c = _position()
    px, py = [(1 - x, y), (x, 1 - y), (1 - x, 1 - y)][j]
    return src_ref.at[2 * px + py], land_ref.at[j], (px, py, c)


SUM_STEPS = 2


def sum_for_chips(parts, from_sibling, c_idx, *, name):
    n = len(parts)

    def body(c_ref, *refs):
        del c_ref
        for t in range(n):
            refs[2 * n + t][...] = (refs[t][...] + refs[n + t][...]).astype(BF16)

    def blk(a):
        return (None, a.shape[1] // SUM_STEPS, a.shape[2])

    return pl.pallas_call(
        body, name=name,
        grid_spec=pltpu.PrefetchScalarGridSpec(
            num_scalar_prefetch=1, grid=(4, SUM_STEPS),
            in_specs=[pl.BlockSpec(blk(a), lambda k, i, c_ref: (2 * k + c_ref[0], i, 0)) for a in parts]
            + [pl.BlockSpec(blk(a), lambda k, i, c_ref: (k, i, 0)) for a in from_sibling],
            out_specs=[pl.BlockSpec(blk(a), lambda k, i, c_ref: (k, i, 0)) for a in from_sibling]),
        out_shape=[jax.ShapeDtypeStruct(a.shape, BF16) for a in from_sibling], compiler_params=_cp(),
    )(c_idx, *parts, *from_sibling)


def sum_final(parts, from_sibling, from_chips, kc_idx, *, name):
    n = len(parts)

    def body(kc_ref, *refs):
        del kc_ref
        for t in range(n):
            p, s, a, b, d = (refs[j * n + t] for j in range(5))
            refs[5 * n + t][...] = (((p[...] + s[...]) + a[...].astype(F32)) + b[...].astype(F32)) + d[...].astype(F32)

    def blk(a):
        return (None, a.shape[1] // SUM_STEPS, a.shape[2])

    def chip_specs(j):
        return [pl.BlockSpec(blk(a), lambda i, kc: (j, i, 0)) for a in from_chips]

    return pl.pallas_call(
        body, name=name,
        grid_spec=pltpu.PrefetchScalarGridSpec(
            num_scalar_prefetch=1, grid=(SUM_STEPS,),
            in_specs=[pl.BlockSpec(blk(a), lambda i, kc: (2 * kc[0] + kc[1], i, 0)) for a in parts]
            + [pl.BlockSpec(blk(a), lambda i, kc: (kc[0], i, 0)) for a in from_sibling]
            + chip_specs(0) + chip_specs(1) + chip_specs(2),
            out_specs=[pl.BlockSpec(blk(a)[1:], lambda i, kc: (i, 0)) for a in parts]),
        out_shape=[jax.ShapeDtypeStruct(a.shape[1:], F32) for a in parts], compiler_params=_cp(),
    )(kc_idx, *parts, *from_sibling, *from_chips, *from_chips, *from_chips)


def sum_devices(gathered):
    n, r, c_ = gathered.shape

    def body(g_ref, o_ref):
        acc = g_ref[0]
        for k in range(1, n):
            acc = acc + g_ref[k]
        o_ref[...] = acc

    return pl.pallas_call(
        body, name="sum_devices", out_shape=jax.ShapeDtypeStruct((r, c_), F32), compiler_params=_cp(),
    )(gathered)


BIG = (
    ("w_in", IN_COLS, True), ("w_out", D_MODEL, False), ("wq_x", D_MODEL, False), ("wkv_x", 2 * D_MODEL, True),
    ("wo_x", D_MODEL, False), ("w_gate_up", 2 * D_FF, True), ("w_down", D_FF, False),
)
SHARD_ROWS = sum(rows // N_DEV for _, rows, _ in BIG)

SMALL = ("norm_mix_g", "q_norm_g", "k_norm_g", "sinks", "conv_b", "conv_ln_g", "conv_ln_b",
         "norm_x_g", "norm_mem_g", "xq_norm_g", "xk_norm_g", "norm_ffn_g")


def _pack_rows(vectors, width=LANES, row_multiple=8):
    flat = jnp.concatenate([v.reshape(-1) for v in vectors])
    per = width * row_multiple
    padded = -(-flat.shape[0] // per) * per
    return jnp.pad(flat, (0, padded - flat.shape[0])).reshape(-1, width)


def _unpack_rows(packed, shapes):
    flat = packed.reshape(-1)
    out, at = [], 0
    for s in shapes:
        n = 1
        for dim in s:
            n *= dim
        out.append(flat[at:at + n].reshape(s))
        at += n
    return out


def _layer_fwd(x0, mem, w, s):
    h0, u = norm_proj(x0, s["norm_mix_g"], w["w_in"])
    mixed = swa_fwd(u, s["q_norm_g"], s["k_norm_g"], s["sinks"])
    mixed, conv_y = conv_fwd(u, mixed, s["conv_w"], s["conv_b"], s["conv_ln_g"], s["conv_ln_b"])
    memn = rms_fwd(mem, s["norm_mem_g"])
    kv = mm(memn, w["wkv_x"], trans_b=True, out_dtype=F32, name="mm_kv")
    x1, h1, qx, o, x2, h2 = mid_fwd(mixed, x0, w["w_out"], s["norm_x_g"], w["wq_x"], kv, s["xq_norm_g"], s["xk_norm_g"],
                                    w["wo_x"], s["norm_ffn_g"])
    gu, a, x3 = ffn_fwd(h2, x2, w["w_gate_up"], w["w_down"])
    saved = dict(x0=x0, h0=h0, u=u, conv_y=conv_y, mixed=mixed, x1=x1, h1=h1, qx=qx, memn=memn, kv=kv, o=o, x2=x2, h2=h2,
                 gu=gu, a=a)
    return x3, saved


def _ordered_after(a, token):
    return a if token is None else a + token[0, 0]


def _layer_bwd(dx3, mem, w, s, sv, token, stage_done):
    gs = {}
    dgu, dx2, dg = ffn_bwd(dx3, sv["gu"], sv["x2"], _ordered_after(s["norm_ffn_g"], token), w["w_down"], w["w_gate_up"])
    gs["norm_ffn_g"] = dg
    gb = {"w_down": mm_tn(sv["a"], dx3, name="mm_dw_down")}
    gb["w_gate_up"] = mm_tn(dgu, sv["h2"], name="mm_dw_gate_up")
    token = stage_done("ffn", gb, gb["w_gate_up"])

    gb = {}
    dq, dx1, dmixed, dkv, dqg, dkg, dg = mid_bwd(dx2, sv["qx"], sv["kv"], s["xq_norm_g"], s["xk_norm_g"], sv["x1"],
                                                 _ordered_after(s["norm_x_g"], token), w["wo_x"], w["wq_x"], w["w_out"])
    gs["xq_norm_g"], gs["xk_norm_g"], gs["norm_x_g"] = dqg, dkg, dg
    gb["wo_x"] = mm_tn(sv["o"], dx2, name="mm_dwo")
    gb["wq_x"] = mm_tn(sv["h1"], dq, name="mm_dwq")
    dmemn = mm(dkv, w["wkv_x"], trans_b=False, out_dtype=F32, name="mm_dmemn")
    gb["wkv_x"] = mm_tn(dkv, sv["memn"], name="mm_dwkv")
    _, dg = rms_bwd(dmemn, mem, s["norm_mem_g"], None)
    gs["norm_mem_g"] = dg
    gb["w_out"] = mm_tn(sv["mixed"], dx1, name="mm_dw_out")
    token = stage_done("mid", gb, gb["w_out"])

    du, dqg, dkg, dsinks = swa_bwd(sv["u"], dmixed, _ordered_after(s["q_norm_g"], token), s["k_norm_g"], s["sinks"])
    gs["q_norm_g"], gs["k_norm_g"], gs["sinks"] = dqg[0, :HEAD_DIM], dkg[0, :HEAD_DIM], dsinks[0, :N_Q_HEADS]
    token = stage_done("attn", {}, dqg)
    du, dconv_w, dvec = conv_bwd(sv["u"], sv["conv_y"], dmixed, du, s["conv_w"], _ordered_after(s["conv_ln_g"], token),
                                 s["conv_ln_b"])
    gs["conv_w"] = dconv_w[:CONV_K]
    gs["conv_b"], gs["conv_ln_g"], gs["conv_ln_b"] = dvec[0], dvec[1], dvec[2]
    dx0, dg = in_bwd(du, w["w_in"], sv["x0"], s["norm_mix_g"], dx1)
    gs["norm_mix_g"] = dg
    token = stage_done("mix", {"w_in": mm_tn(du, sv["h0"], name="mm_dw_in")}, dx0)
    return dx0, gs, token


def _local_step(x, mem, target, weights_of, smalls, stage_done):
    saved, weights = [], []
    h = x
    for l in range(DEPTH):
        weights.append(weights_of(l, h))
        h, sv = _layer_fwd(h, mem, weights[l], smalls[l])
        saved.append(sv)
    dx, loss_part = loss_head(h, target)
    gss, token = [None] * DEPTH, None
    for l in reversed(range(DEPTH)):
        dx, gss[l], token = _layer_bwd(dx, mem, weights[l], smalls[l], saved[l], token,
                                       functools.partial(stage_done, l))
    return loss_part[0, 0], dx, gss


def kernel(x, mem, norm_mix_g, w_in, q_norm_g, k_norm_g, sinks, conv_w, conv_b, conv_ln_g, conv_ln_b, w_out, norm_x_g, norm_mem_g, wq_x, wkv_x, xq_norm_g, xk_norm_g, wo_x, norm_ffn_g, w_gate_up, w_down, loss_target, m_norm_mix_g, m_w_in, m_q_norm_g, m_k_norm_g, m_sinks, m_conv_w, m_conv_b, m_conv_ln_g, m_conv_ln_b, m_w_out, m_norm_x_g, m_norm_mem_g, m_wq_x, m_wkv_x, m_xq_norm_g, m_xk_norm_g, m_wo_x, m_norm_ffn_g, m_w_gate_up, m_w_down, v_norm_mix_g, v_w_in, v_q_norm_g, v_k_norm_g, v_sinks, v_conv_w, v_conv_b, v_conv_ln_g, v_conv_ln_b, v_w_out, v_norm_x_g, v_norm_mem_g, v_wq_x, v_wkv_x, v_xq_norm_g, v_xk_norm_g, v_wo_x, v_norm_ffn_g, v_w_gate_up, v_w_down):
    P = dict(norm_mix_g=norm_mix_g, w_in=w_in, q_norm_g=q_norm_g, k_norm_g=k_norm_g, sinks=sinks, conv_w=conv_w, conv_b=conv_b,
             conv_ln_g=conv_ln_g, conv_ln_b=conv_ln_b, w_out=w_out, norm_x_g=norm_x_g, norm_mem_g=norm_mem_g, wq_x=wq_x,
             wkv_x=wkv_x, xq_norm_g=xq_norm_g, xk_norm_g=xk_norm_g, wo_x=wo_x, norm_ffn_g=norm_ffn_g, w_gate_up=w_gate_up,
             w_down=w_down)
    M = dict(norm_mix_g=m_norm_mix_g, w_in=m_w_in, q_norm_g=m_q_norm_g, k_norm_g=m_k_norm_g, sinks=m_sinks, conv_w=m_conv_w,
             conv_b=m_conv_b, conv_ln_g=m_conv_ln_g, conv_ln_b=m_conv_ln_b, w_out=m_w_out, norm_x_g=m_norm_x_g,
             norm_mem_g=m_norm_mem_g, wq_x=m_wq_x, wkv_x=m_wkv_x, xq_norm_g=m_xq_norm_g, xk_norm_g=m_xk_norm_g, wo_x=m_wo_x,
             norm_ffn_g=m_norm_ffn_g, w_gate_up=m_w_gate_up, w_down=m_w_down)
    V = dict(norm_mix_g=v_norm_mix_g, w_in=v_w_in, q_norm_g=v_q_norm_g, k_norm_g=v_k_norm_g, sinks=v_sinks, conv_w=v_conv_w,
             conv_b=v_conv_b, conv_ln_g=v_conv_ln_g, conv_ln_b=v_conv_ln_b, w_out=v_w_out, norm_x_g=v_norm_x_g,
             norm_mem_g=v_norm_mem_g, wq_x=v_wq_x, wkv_x=v_wkv_x, xq_norm_g=v_xq_norm_g, xk_norm_g=v_xk_norm_g, wo_x=v_wo_x,
             norm_ffn_g=v_norm_ffn_g, w_gate_up=v_w_gate_up, w_down=v_w_down)
    order = ["norm_mix_g", "w_in", "q_norm_g", "k_norm_g", "sinks", "conv_w", "conv_b", "conv_ln_g", "conv_ln_b", "w_out",
             "norm_x_g", "norm_mem_g", "wq_x", "wkv_x", "xq_norm_g", "xk_norm_g", "wo_x", "norm_ffn_g", "w_gate_up", "w_down"]
    xi, yi, ci = _position()
    dev = 4 * xi + 2 * yi + ci
    x2d, mem2d, tgt2d = x[0], mem[0], loss_target[0]

    def travelling(name, l, transposed):
        a = P[name][l]
        return (a.T if transposed else a).astype(BF16)

    names = [n for n, _, _ in BIG]
    rows_of = {n: rows for n, rows, _ in BIG}

    def whole(gathered):
        return {n: g.reshape(rows_of[n], D_MODEL) for n, g in zip(names, gathered)}

    cw = jnp.pad(conv_w.reshape(DEPTH * CONV_K, CONV_CH // N_DEV), ((0, 2), (0, LANES - CONV_CH // N_DEV)))
    cw_all, cw_token = all_gather(cw, name="ag_conv_w", in_vmem=True)
    shards0 = [_ordered_after(travelling(n, 0, tr), cw_token.astype(BF16)) for n, _, tr in BIG]
    gathered0, token0 = all_gather_many(shards0, name="ag_weights0")
    weights0 = whole(gathered0)
    shards1 = [_ordered_after(travelling(n, 1, tr), token0.astype(BF16)) for n, _, tr in BIG]
    lands1 = [lax.dynamic_update_slice(lax.empty((N_DEV,) + s.shape, BF16), s[None], (dev, 0, 0)) for s in shards1]
    gather1 = split_start(shards1, lands1, _gather_plan, N_DEV - 1, name="ag_weights1_start")

    def weights_of(l, marker):
        if l == 0:
            return weights0
        return whole(split_wait(gather1, marker, _gather_plan, N_DEV - 1, name="ag_weights1_wait")[1])

    cw_full = cw_all[:, :DEPTH * CONV_K, :CONV_CH // N_DEV].reshape(N_DEV, DEPTH, CONV_K, CONV_CH // N_DEV)
    cw_full = jnp.transpose(cw_full, (1, 2, 0, 3)).reshape(DEPTH, CONV_K, CONV_CH)
    smalls = []
    for l in range(DEPTH):
        sl = {n: P[n][l] for n in SMALL}
        sl["conv_w"] = jnp.pad(cw_full[l], ((0, CONV_HALO - CONV_K), (0, 0)))
        smalls.append(sl)
    smalls[0]["norm_mix_g"] = _ordered_after(smalls[0]["norm_mix_g"], gather1[4])

    c_idx = jnp.reshape(ci, (1,)).astype(jnp.int32)
    kc_idx = jnp.stack([2 * xi + yi, ci]).astype(jnp.int32)
    got, flight, reduced = {}, {}, {}

    def as_parts(gb):
        keys = sorted(gb)
        return keys, [gb[k].reshape(N_DEV, rows_of[k[1]] // N_DEV, D_MODEL) for k in keys]

    def lands_like(parts, blocks, dtype):
        return [lax.empty((blocks,) + p.shape[1:], dtype) for p in parts]

    def to_sibling(group, gb):
        keys, parts = as_parts(gb)
        flight[group] = (keys, split_start(parts, lands_like(parts, 4, F32), _sibling_plan, 4,
                                           name=f"rs_sibling_{group}_start"))
        return flight[group][1][4]

    def to_chips(group, marker):
        keys, started = flight[group]
        parts, from_sibling = split_wait(started, marker, _sibling_plan, 4, name=f"rs_sibling_{group}_wait")
        chip_sums = sum_for_chips(parts, from_sibling, c_idx, name=f"rs_sum_for_chips_{group}")
        started = split_start(chip_sums, lands_like(parts, 3, BF16), _chips_plan, 3, name=f"rs_chips_{group}_start")
        flight[group] = (keys, parts, from_sibling, started)
        return started[4]

    def finish(group, marker):
        keys, parts, from_sibling, started = flight[group]
        _, from_chips = split_wait(started, marker, _chips_plan, 3, name=f"rs_chips_{group}_wait")
        reduced.update(zip(keys, sum_final(parts, from_sibling, from_chips, kc_idx, name=f"rs_sum_final_{group}")))

    def stage_done(l, stage, gb, marker):
        gb = {(l, n): g for n, g in gb.items()}
        if l == 1:
            got.update(gb)
            return to_sibling("l1", got) if stage == "mix" else None
        if stage == "ffn":
            return to_chips("l1", marker) + to_sibling("ffn", gb)
        if stage == "mid":
            return to_chips("ffn", marker) + to_sibling("mid", gb)
        if stage == "attn":
            return to_chips("mid", marker)
        finish("l1", marker)
        finish("ffn", marker)
        finish("mid", marker)
        keys, parts = as_parts(gb)
        from_sibling = exchange(parts, _sibling_plan, 4, 4, name="rs_sibling_in")
        chip_sums = sum_for_chips(parts, from_sibling, c_idx, name="rs_sum_for_chips_in")
        from_chips = exchange(chip_sums, _chips_plan, 3, 3, name="rs_chips_in")
        reduced.update(zip(keys, sum_final(parts, from_sibling, from_chips, kc_idx, name="rs_sum_final_in")))
        return None

    loss_part, grad_x, gss = _local_step(x2d, mem2d, tgt2d, weights_of, smalls, stage_done)
    loss = lax.psum(loss_part, ("x", "y", "c"))

    grads = {n: jnp.stack([reduced[(l, n)].T if transposed else reduced[(l, n)] for l in range(DEPTH)])
             for n, _, transposed in BIG}

    small_names = SMALL + ("conv_w",)
    small_shapes = [(DEPTH,) + ((CONV_K, CONV_CH) if n == "conv_w" else P[n].shape[1:]) for n in small_names]
    small_parts = _pack_rows([jnp.stack([gss[l][n].reshape(sh[1:]) for l in range(DEPTH)])
                              for n, sh in zip(small_names, small_shapes)])
    small_sum = sum_devices(all_gather(small_parts, name="ag_small_grads", in_vmem=True)[0])
    for n, g in zip(small_names, _unpack_rows(small_sum, small_shapes)):
        if n == "conv_w":
            g = lax.dynamic_slice_in_dim(g, dev * (CONV_CH // N_DEV), CONV_CH // N_DEV, axis=2)
        grads[n] = g

    delta, new_m, new_v = {}, {}, {}
    for n, _, _ in BIG:
        shape = P[n].shape
        two_d = lambda a: a.reshape(shape[0] * shape[1], shape[2])
        d_, m_, v_ = adamw(two_d(P[n]), two_d(grads[n]), two_d(M[n]), two_d(V[n]), name="adamw_" + n)
        delta[n], new_m[n], new_v[n] = d_.reshape(shape), m_.reshape(shape), v_.reshape(shape)
    shapes = [P[n].shape for n in small_names]
    d_, m_, v_ = adamw(_pack_rows([P[n] for n in small_names]), _pack_rows([grads[n] for n in small_names]),
                       _pack_rows([M[n] for n in small_names]), _pack_rows([V[n] for n in small_names]), name="adamw_small")
    for n, dd, mm_, vv in zip(small_names, _unpack_rows(d_, shapes), _unpack_rows(m_, shapes), _unpack_rows(v_, shapes)):
        delta[n], new_m[n], new_v[n] = dd, mm_, vv

    return (loss, grad_x[None], *[grads[n] for n in order], *[delta[n] for n in order],
            *[new_m[n] for n in order], *[new_v[n] for n in order])
```

```python
import functools

import jax
import jax.numpy as jnp
import numpy as np
from jax import lax
from jax.experimental import pallas as pl
from jax.experimental.pallas import tpu as pltpu

F32 = jnp.float32
BF16 = jnp.bfloat16

D_MODEL = 1024
HEAD_DIM = 64
N_Q_HEADS = 8
N_KV_HEADS = 2
GROUP = N_Q_HEADS // N_KV_HEADS
ATTN_WIDTH = N_Q_HEADS * HEAD_DIM
KV_WIDTH = N_KV_HEADS * HEAD_DIM
QKV_WIDTH = ATTN_WIDTH + 2 * KV_WIDTH
CONV_CH = 512
IN_COLS = QKV_WIDTH + 2 * CONV_CH
CONV_K = 31
CONV_HALO = 32
BLOCK = 128
N_X_HEADS = 4
X_HEAD_DIM = 256
D_FF = 2816
EPS = 1e-6
NEG = -1e30
DEPTH = 2
N_DEV = 8

ADAM_LR = 0.001
ADAM_B1 = 0.9
ADAM_B2 = 0.999
ADAM_EPS = 1e-08
ADAM_WD = 0.01
ADAM_STEP = 10

V7X_VMEM_LIMIT = 56 * 1024 * 1024
LANES = 128

MESH = pl.DeviceIdType.MESH


def _cp(**kw):
    return pltpu.CompilerParams(vmem_limit_bytes=V7X_VMEM_LIMIT, **kw)


def _dot(a, b, dims):
    return lax.dot_general(a.astype(BF16), b.astype(BF16), (dims, ((), ())), preferred_element_type=F32)


def _dot_nn(a, b):
    return _dot(a, b, ((1,), (0,)))


def _dot_nt(a, b):
    return _dot(a, b, ((1,), (1,)))


def _dot_tn(a, b):
    return _dot(a, b, ((0,), (0,)))


def _sigmoid(x):
    return jax.nn.sigmoid(x)


def _rms(x):
    r = lax.rsqrt(jnp.mean(x * x, axis=-1, keepdims=True) + EPS)
    return x * r, r


def _rms_bwd(dy, xhat, r, g):
    dxh = dy * g
    return r * (dxh - xhat * jnp.mean(dxh * xhat, axis=-1, keepdims=True))


def rms_fwd(x, g, *, tm=512):
    m, d = x.shape
    tm = min(tm, m)

    def body(x_ref, g_ref, o_ref):
        xh, _ = _rms(x_ref[...])
        o_ref[...] = (xh * g_ref[...]).astype(o_ref.dtype)

    return pl.pallas_call(
        body, name="rms_fwd", grid=(m // tm,),
        in_specs=[pl.BlockSpec((tm, d), lambda i: (i, 0)), pl.BlockSpec((1, d), lambda i: (0, 0))],
        out_specs=pl.BlockSpec((tm, d), lambda i: (i, 0)),
        out_shape=jax.ShapeDtypeStruct((m, d), BF16), compiler_params=_cp(),
    )(x, g.reshape(1, d))


def rms_bwd(dh, x, g, dres, *, tm=512):
    m, d = x.shape
    tm = min(tm, m)
    has_res = dres is not None

    def body(*refs):
        if has_res:
            dh_ref, x_ref, g_ref, r_ref, dx_ref, dg_ref = refs
        else:
            dh_ref, x_ref, g_ref, dx_ref, dg_ref = refs
        xh, r = _rms(x_ref[...])
        dy = dh_ref[...].astype(F32)

        @pl.when(pl.program_id(0) == 0)
        def _():
            dg_ref[...] = jnp.zeros_like(dg_ref)

        dg_ref[...] += jnp.sum(dy * xh, axis=0, keepdims=True)
        dx = _rms_bwd(dy, xh, r, g_ref[...])
        if has_res:
            dx = dx + r_ref[...]
        dx_ref[...] = dx

    row = pl.BlockSpec((tm, d), lambda i: (i, 0))
    vec = pl.BlockSpec((1, d), lambda i: (0, 0))
    ins = [dh, x, g.reshape(1, d)] + ([dres] if has_res else [])
    return pl.pallas_call(
        body, name="rms_bwd" + ("_res" if has_res else ""), grid=(m // tm,),
        in_specs=[row, row, vec] + ([row] if has_res else []),
        out_specs=[row, vec],
        out_shape=[jax.ShapeDtypeStruct((m, d), F32), jax.ShapeDtypeStruct((1, d), F32)],
        compiler_params=_cp(),
    )(*ins)


def loss_head(y, target, *, tm=512):
    m, d = y.shape

    def body(y_ref, t_ref, dy_ref, l_ref):
        err = y_ref[...] - t_ref[...]
        dy_ref[...] = err * (1.0 / d)

        @pl.when(pl.program_id(0) == 0)
        def _():
            l_ref[...] = jnp.zeros_like(l_ref)

        part = jnp.sum(jnp.sum(err * err, axis=-1, keepdims=True), axis=0, keepdims=True)
        l_ref[...] += jnp.broadcast_to(part * (0.5 / d), l_ref.shape)

    row = pl.BlockSpec((tm, d), lambda i: (i, 0))
    return pl.pallas_call(
        body, name="loss_head", grid=(m // tm,),
        in_specs=[row, row],
        out_specs=[row, pl.BlockSpec((1, LANES), lambda i: (0, 0))],
        out_shape=[jax.ShapeDtypeStruct((m, d), F32), jax.ShapeDtypeStruct((1, LANES), F32)],
        compiler_params=_cp(),
    )(y, target)


def _tile(n, cap):
    if n <= cap:
        return n
    best = None
    for t in range(LANES, cap + 1, LANES):
        if n % t == 0:
            best = t
    assert best is not None, (n, cap)
    return best


def mm(a, b, *, trans_b, out_dtype, res=None, tm=1024, tn_cap=1536, name):
    m, k = a.shape
    n = b.shape[0] if trans_b else b.shape[1]
    assert (b.shape[1] if trans_b else b.shape[0]) == k
    tm = min(tm, m)
    tn = _tile(n, tn_cap)
    has_res = res is not None

    def body(*refs):
        if has_res:
            a_ref, b_ref, r_ref, o_ref = refs
        else:
            a_ref, b_ref, o_ref = refs
        acc = _dot_nt(a_ref[...], b_ref[...]) if trans_b else _dot_nn(a_ref[...], b_ref[...])
        if has_res:
            acc = acc + r_ref[...]
        o_ref[...] = acc.astype(o_ref.dtype)

    b_spec = pl.BlockSpec((tn, k), lambda i, j: (j, 0)) if trans_b else pl.BlockSpec((k, tn), lambda i, j: (0, j))
    o_spec = pl.BlockSpec((tm, tn), lambda i, j: (i, j))
    return pl.pallas_call(
        body, name=name, grid=(m // tm, n // tn),
        in_specs=[pl.BlockSpec((tm, k), lambda i, j: (i, 0)), b_spec] + ([o_spec] if has_res else []),
        out_specs=o_spec,
        out_shape=jax.ShapeDtypeStruct((m, n), out_dtype), compiler_params=_cp(),
    )(*([a, b] + ([res] if has_res else [])))


def mm_tn(a, b, *, name, ta_cap=1536, tb_cap=1024, tk=1024):
    m, ka = a.shape
    nb = b.shape[1]
    assert b.shape[0] == m
    tk = min(tk, m)
    ta = _tile(ka, ta_cap)
    tb = _tile(nb, tb_cap)

    def body(a_ref, b_ref, o_ref):
        @pl.when(pl.program_id(2) == 0)
        def _():
            o_ref[...] = jnp.zeros_like(o_ref)

        o_ref[...] += _dot_tn(a_ref[...], b_ref[...])

    return pl.pallas_call(
        body, name=name, grid=(ka // ta, nb // tb, m // tk),
        in_specs=[pl.BlockSpec((tk, ta), lambda i, j, kk: (kk, i)), pl.BlockSpec((tk, tb), lambda i, j, kk: (kk, j))],
        out_specs=pl.BlockSpec((ta, tb), lambda i, j, kk: (i, j)),
        out_shape=jax.ShapeDtypeStruct((ka, nb), F32), compiler_params=_cp(),
    )(a, b)


def _whole(shape):
    return pl.BlockSpec(shape, lambda i: (0,) * len(shape), pipeline_mode=pl.Buffered(1))


def _rows(tm, n):
    return pl.BlockSpec((tm, n), lambda i: (i, 0))


def _vec(n):
    return pl.BlockSpec((1, n), lambda i: (0, 0))


def _chunks(n, cap=1408):
    size = _tile(n, cap)
    return [(s, size) for s in range(0, n, size)]


def _zero_at_first_step(*refs):
    @pl.when(pl.program_id(0) == 0)
    def _():
        for r in refs:
            r[...] = jnp.zeros_like(r)


def norm_proj(x, g, wt, *, tm=512):
    m, d = x.shape
    n = wt.shape[0]

    def body(x_ref, g_ref, wt_ref, h_ref, u_ref):
        h = (_rms(x_ref[...])[0] * g_ref[...]).astype(BF16)
        h_ref[...] = h
        for s, sz in _chunks(n):
            u_ref[:, s:s + sz] = _dot_nt(h, wt_ref[s:s + sz, :])

    return pl.pallas_call(
        body, name="norm_proj", grid=(m // tm,),
        in_specs=[_rows(tm, d), _vec(d), _whole((n, d))],
        out_specs=[_rows(tm, d), _rows(tm, n)],
        out_shape=[jax.ShapeDtypeStruct((m, d), BF16), jax.ShapeDtypeStruct((m, n), F32)],
        compiler_params=_cp(),
    )(x, g.reshape(1, d), wt)


def _xattn_heads(q_ref, kv_ref, qg_v, kg_v, d):
    out = []
    for h in range(N_X_HEADS):
        cols = slice(h * X_HEAD_DIM, (h + 1) * X_HEAD_DIM)
        qh, rq = _rms(q_ref[:, cols])
        qn = qh * qg_v
        kn = _rms(kv_ref[:, cols])[0] * kg_v
        v = kv_ref[:, d + h * X_HEAD_DIM:d + (h + 1) * X_HEAD_DIM]
        out.append((qh, rq, qn, kn, v, _xattn_probs(qn, kn)))
    return out


def mid_fwd(mixed, x0, w_out, g_x, wq, kv, xqg, xkg, wo, g_f, *, tm=512):
    m, d = x0.shape
    n_mem = kv.shape[0]

    def body(mixed_ref, x0_ref, w_out_ref, g_x_ref, wq_ref, kv_ref, xqg_ref, xkg_ref, wo_ref, g_f_ref,
             x1_ref, h1_ref, qx_ref, o_ref, x2_ref, h2_ref):
        x1 = x0_ref[...] + _dot_nn(mixed_ref[...], w_out_ref[...])
        x1_ref[...] = x1
        h1 = (_rms(x1)[0] * g_x_ref[...]).astype(BF16)
        h1_ref[...] = h1
        qx_ref[...] = _dot_nn(h1, wq_ref[...])
        for h, (_, _, _, _, v, p) in enumerate(_xattn_heads(qx_ref, kv_ref, xqg_ref[...], xkg_ref[...], d)):
            o_ref[:, h * X_HEAD_DIM:(h + 1) * X_HEAD_DIM] = _dot_nn(p, v).astype(o_ref.dtype)
        x2 = x1 + _dot_nn(o_ref[...], wo_ref[...])
        x2_ref[...] = x2
        h2_ref[...] = (_rms(x2)[0] * g_f_ref[...]).astype(BF16)

    sq = _whole((d, d))
    f32_rows, bf_rows = jax.ShapeDtypeStruct((m, d), F32), jax.ShapeDtypeStruct((m, d), BF16)
    return pl.pallas_call(
        body, name="mid_fwd", grid=(m // tm,),
        in_specs=[_rows(tm, d), _rows(tm, d), sq, _vec(d), sq, _whole((n_mem, 2 * d)), _vec(X_HEAD_DIM), _vec(X_HEAD_DIM),
                  sq, _vec(d)],
        out_specs=[_rows(tm, d)] * 6,
        out_shape=[f32_rows, bf_rows, f32_rows, bf_rows, f32_rows, bf_rows],
        compiler_params=_cp(),
    )(mixed, x0, w_out, g_x.reshape(1, d), wq, kv, xqg.reshape(1, X_HEAD_DIM), xkg.reshape(1, X_HEAD_DIM), wo,
      g_f.reshape(1, d))


def ffn_fwd(h2, x2, wt_gu, w_down, *, tm=256):
    m, d = x2.shape
    f = w_down.shape[0]

    def body(h2_ref, x2_ref, wt_gu_ref, w_down_ref, gu_ref, a_ref, x3_ref):
        h = h2_ref[...]
        for s, sz in _chunks(2 * f):
            gu_ref[:, s:s + sz] = _dot_nt(h, wt_gu_ref[s:s + sz, :])
        for s, sz in _chunks(f):
            g = gu_ref[:, s:s + sz]
            a_ref[:, s:s + sz] = (g * _sigmoid(g) * gu_ref[:, f + s:f + s + sz]).astype(a_ref.dtype)
        x3_ref[...] = x2_ref[...] + _dot_nn(a_ref[...], w_down_ref[...])

    return pl.pallas_call(
        body, name="ffn_fwd", grid=(m // tm,),
        in_specs=[_rows(tm, d), _rows(tm, d), _whole((2 * f, d)), _whole((f, d))],
        out_specs=[_rows(tm, 2 * f), _rows(tm, f), _rows(tm, d)],
        out_shape=[jax.ShapeDtypeStruct((m, 2 * f), F32), jax.ShapeDtypeStruct((m, f), BF16),
                   jax.ShapeDtypeStruct((m, d), F32)],
        compiler_params=_cp(),
    )(h2, x2, wt_gu, w_down)


def ffn_bwd(dx3, gu, x2, g_f, w_down, wt_gu, *, tm=256):
    m, d = x2.shape
    f = w_down.shape[0]

    def body(dx3_ref, gu_ref, x2_ref, g_ref, w_down_ref, wt_gu_ref, dgu_ref, dx2_ref, dg_ref):
        _zero_at_first_step(dg_ref)
        dx3 = dx3_ref[...]
        dx3_b = dx3.astype(BF16)
        for s, sz in _chunks(f):
            da = _dot_nt(dx3_b, w_down_ref[s:s + sz, :])
            g = gu_ref[:, s:s + sz]
            u = gu_ref[:, f + s:f + s + sz]
            sg = _sigmoid(g)
            dgu_ref[:, s:s + sz] = (da * u * (sg * (1.0 + g * (1.0 - sg)))).astype(dgu_ref.dtype)
            dgu_ref[:, f + s:f + s + sz] = (da * (g * sg)).astype(dgu_ref.dtype)
        dh2 = _dot_nn(dgu_ref[...], wt_gu_ref[...])
        xh, r = _rms(x2_ref[...])
        dg_ref[...] += jnp.sum(dh2 * xh, axis=0, keepdims=True)
        dx2_ref[...] = dx3 + _rms_bwd(dh2, xh, r, g_ref[...])

    return pl.pallas_call(
        body, name="ffn_bwd", grid=(m // tm,),
        in_specs=[_rows(tm, d), _rows(tm, 2 * f), _rows(tm, d), _vec(d), _whole((f, d)), _whole((2 * f, d))],
        out_specs=[_rows(tm, 2 * f), _rows(tm, d), _vec(d)],
        out_shape=[jax.ShapeDtypeStruct((m, 2 * f), BF16), jax.ShapeDtypeStruct((m, d), F32),
                   jax.ShapeDtypeStruct((1, d), F32)],
        compiler_params=_cp(),
    )(dx3, gu, x2, g_f.reshape(1, d), w_down, wt_gu)


def mid_bwd(dx2, qx, kv, xqg, xkg, x1, g_x, wo, wq, w_out, *, tm=512):
    m, d = x1.shape
    n_mem = kv.shape[0]
    nt = m // tm

    def body(dx2_ref, qx_ref, kv_ref, xqg_ref, xkg_ref, x1_ref, g_x_ref, wo_ref, wq_ref, w_out_ref,
             dq_ref, dx1_ref, dmixed_ref, dkv_ref, dqg_ref, dkg_ref, dg_ref):
        i = pl.program_id(0)
        _zero_at_first_step(dkv_ref, dqg_ref, dkg_ref, dg_ref)
        qg_v, kg_v = xqg_ref[...], xkg_ref[...]
        dx2 = dx2_ref[...]
        do = _dot_nt(dx2, wo_ref[...])
        dqg_acc = jnp.zeros((1, X_HEAD_DIM), F32)
        for h, (qh, rq, qn, kn, v, p) in enumerate(_xattn_heads(qx_ref, kv_ref, qg_v, kg_v, d)):
            cols = slice(h * X_HEAD_DIM, (h + 1) * X_HEAD_DIM)
            vcols = slice(d + h * X_HEAD_DIM, d + (h + 1) * X_HEAD_DIM)
            do_h = do[:, cols]
            dp = _dot_nt(do_h, v)
            ds = p * (dp - jnp.sum(p * dp, axis=-1, keepdims=True))
            dkv_ref[:, vcols] += _dot_tn(p, do_h)
            dqn = _dot_nn(ds, kn) * (X_HEAD_DIM ** -0.5)
            dkv_ref[:, cols] += _dot_tn(ds, qn) * (X_HEAD_DIM ** -0.5)
            dqg_acc = dqg_acc + jnp.sum(dqn * qh, axis=0, keepdims=True)
            dq_ref[:, cols] = _rms_bwd(dqn, qh, rq, qg_v).astype(dq_ref.dtype)
        dqg_ref[...] += dqg_acc
        dh1 = _dot_nt(dq_ref[...], wq_ref[...])
        xh, r = _rms(x1_ref[...])
        dg_ref[...] += jnp.sum(dh1 * xh, axis=0, keepdims=True)
        dx1 = dx2 + _rms_bwd(dh1, xh, r, g_x_ref[...])
        dx1_ref[...] = dx1
        dmixed_ref[...] = _dot_nt(dx1, w_out_ref[...])

        @pl.when(i == nt - 1)
        def _():
            dkg_acc = jnp.zeros((1, X_HEAD_DIM), F32)
            for h in range(N_X_HEADS):
                cols = slice(h * X_HEAD_DIM, (h + 1) * X_HEAD_DIM)
                kh, rk = _rms(kv_ref[:, cols])
                dkn = dkv_ref[:, cols]
                dkg_acc = dkg_acc + jnp.sum(dkn * kh, axis=0, keepdims=True)
                dkv_ref[:, cols] = _rms_bwd(dkn, kh, rk, kg_v)
            dkg_ref[...] = dkg_acc

    sq = _whole((d, d))
    full = pl.BlockSpec((n_mem, 2 * d), lambda i: (0, 0))
    return pl.pallas_call(
        body, name="mid_bwd", grid=(nt,),
        in_specs=[_rows(tm, d), _rows(tm, d), _whole((n_mem, 2 * d)), _vec(X_HEAD_DIM), _vec(X_HEAD_DIM), _rows(tm, d),
                  _vec(d), sq, sq, sq],
        out_specs=[_rows(tm, d), _rows(tm, d), _rows(tm, d), full, _vec(X_HEAD_DIM), _vec(X_HEAD_DIM), _vec(d)],
        out_shape=[jax.ShapeDtypeStruct((m, d), BF16), jax.ShapeDtypeStruct((m, d), F32), jax.ShapeDtypeStruct((m, d), F32),
                   jax.ShapeDtypeStruct((n_mem, 2 * d), F32), jax.ShapeDtypeStruct((1, X_HEAD_DIM), F32),
                   jax.ShapeDtypeStruct((1, X_HEAD_DIM), F32), jax.ShapeDtypeStruct((1, d), F32)],
        compiler_params=_cp(),
    )(dx2, qx, kv, xqg.reshape(1, X_HEAD_DIM), xkg.reshape(1, X_HEAD_DIM), x1, g_x.reshape(1, d), wo, wq, w_out)


def in_bwd(du, wt_in, x0, g_mix, dx1, *, tm=512):
    m, d = x0.shape
    n = wt_in.shape[0]

    def body(du_ref, wt_ref, x0_ref, g_ref, dx1_ref, dx0_ref, dg_ref):
        _zero_at_first_step(dg_ref)
        dh0 = _dot_nn(du_ref[...], wt_ref[...])
        xh, r = _rms(x0_ref[...])
        dg_ref[...] += jnp.sum(dh0 * xh, axis=0, keepdims=True)
        dx0_ref[...] = dx1_ref[...] + _rms_bwd(dh0, xh, r, g_ref[...])

    return pl.pallas_call(
        body, name="in_bwd", grid=(m // tm,),
        in_specs=[_rows(tm, n), _whole((n, d)), _rows(tm, d), _vec(d), _rows(tm, d)],
        out_specs=[_rows(tm, d), _vec(d)],
        out_shape=[jax.ShapeDtypeStruct((m, d), F32), jax.ShapeDtypeStruct((1, d), F32)],
        compiler_params=_cp(),
    )(du, wt_in, x0, g_mix.reshape(1, d), dx1)


SWA_TILE = 512
SWA_SUB = SWA_TILE // BLOCK
SWA_KEYS = SWA_TILE + BLOCK
PAIR = 2 * HEAD_DIM
KCOL = ATTN_WIDTH
VCOL = ATTN_WIDTH + KV_WIDTH


def _swa_constants():
    r = np.arange(2 * BLOCK)[:, None]
    j = np.arange(4 * BLOCK)[None, :]
    dist = (r % BLOCK) + BLOCK - (j % (2 * BLOCK))
    valid = (dist >= 0) & (dist < BLOCK)
    first_valid = valid & ((j % (2 * BLOCK)) >= BLOCK)
    bias, bias_first = [], []
    for kv in range(N_KV_HEADS):
        head = kv * GROUP + 2 * (r // BLOCK) + j // (2 * BLOCK)
        b = -(2.0 ** -(head + 1.0)) * dist
        bias.append(np.where(valid, b, NEG))
        bias_first.append(np.where(first_valid, b, NEG))
    lane = np.arange(LANES)
    seg = (lane[:, None] // HEAD_DIM == lane[None, :] // HEAD_DIM) / HEAD_DIM
    row = np.arange(4 * BLOCK)[:, None]
    ones = (row // (2 * BLOCK)) == (lane[None, :] // HEAD_DIM)
    return (jnp.asarray(np.stack(bias), F32), jnp.asarray(np.stack(bias_first), F32), jnp.asarray(seg, BF16),
            jnp.asarray(ones, BF16))


def _segmean(x, seg_ref):
    hi = x.astype(BF16)
    lo = (x - hi.astype(F32)).astype(BF16)
    return _dot_nn(hi, seg_ref[...]) + _dot_nn(lo, seg_ref[...])


def _two_heads(x, kv):
    lane = lax.broadcasted_iota(jnp.int32, (1, LANES), 1)
    mine = (lane < HEAD_DIM) if kv == 0 else (lane >= HEAD_DIM)
    base = jnp.where(mine, x, 0.0)
    other = pltpu.roll(base, HEAD_DIM, 1)
    return jnp.concatenate([base, other] if kv == 0 else [other, base], axis=0)


def _from_two_heads(y, kv):
    rows = y.shape[0] // 2
    lane = lax.broadcasted_iota(jnp.int32, (1, LANES), 1)
    top, bot = y[:rows], y[rows:]
    if kv == 0:
        return jnp.where(lane < HEAD_DIM, top + pltpu.roll(bot, HEAD_DIM, 1), 0.0)
    return jnp.where(lane >= HEAD_DIM, pltpu.roll(top, HEAD_DIM, 1) + bot, 0.0)


def _pair_rows(ref, rows, kv):
    c = kv * 2 * PAIR
    return jnp.concatenate([ref[rows, c:c + PAIR], ref[rows, c + PAIR:c + 2 * PAIR]], axis=0)


def _head_cols(fn, kv):
    return [jnp.concatenate([fn(kv * GROUP + half), fn(kv * GROUP + 2 + half)], axis=0) for half in range(2)]


def _swa_prologue(cur_ref, prev_ref, qg_ref, kg_ref, seg_ref, qg_s, kn_s, v_s):
    qg_s[...] = (cur_ref[:, 0:ATTN_WIDTH] * qg_ref[...]).astype(BF16)
    k = jnp.concatenate([prev_ref[:, KCOL:KCOL + KV_WIDTH], cur_ref[:, KCOL:KCOL + KV_WIDTH]], axis=0)
    kn_s[...] = k * lax.rsqrt(_segmean(k * k, seg_ref) + EPS) * kg_ref[...]
    v_s[0:BLOCK, :] = prev_ref[:, VCOL:VCOL + KV_WIDTH]
    v_s[BLOCK:SWA_KEYS, :] = cur_ref[:, VCOL:VCOL + KV_WIDTH]


def _swa_scores(cur_ref, sinks_ref, qg_s, kn_s, bias, rows, keys, kv):
    q2 = _pair_rows(qg_s, rows, kv)
    k2 = _two_heads(kn_s[keys, :], kv)
    t = _dot_nt(q2, k2)

    def rq(h):
        x = cur_ref[rows, h * HEAD_DIM:(h + 1) * HEAD_DIM]
        return lax.rsqrt(jnp.mean(x * x, axis=-1, keepdims=True) + EPS)

    scale = _head_cols(lambda h: rq(h) * (HEAD_DIM ** -0.5), kv)
    sink = _head_cols(lambda h: jnp.full((BLOCK, 1), sinks_ref[h], F32), kv)
    halves = []
    for half in range(2):
        cols = slice(half * 2 * BLOCK, (half + 1) * 2 * BLOCK)
        s = t[:, cols] * scale[half] + bias[:, cols]
        mx = jnp.maximum(jnp.max(s, axis=-1, keepdims=True), sink[half])
        halves.append((scale[half], jnp.exp(s - mx), jnp.exp(sink[half] - mx)))
    return q2, k2, t, halves


def swa_fwd(u, qg, kg, sinks):
    t_rows = u.shape[0]
    nt = t_rows // SWA_TILE
    bias_c, bias_first_c, seg_c, ones_c = _swa_constants()

    def body(sinks_ref, cur_ref, prev_ref, qg_ref, kg_ref, seg_ref, bias_ref, biasf_ref, ones_ref, o_ref, qg_s, kn_s, v_s):
        i = pl.program_id(0)
        _swa_prologue(cur_ref, prev_ref, qg_ref, kg_ref, seg_ref, qg_s, kn_s, v_s)
        lane = lax.broadcasted_iota(jnp.int32, (1, LANES), 1)
        for b in range(SWA_SUB):
            rows = slice(b * BLOCK, (b + 1) * BLOCK)
            keys = slice(b * BLOCK, (b + 2) * BLOCK)
            for kv in range(N_KV_HEADS):
                bias = jnp.where(i == 0, biasf_ref[kv], bias_ref[kv]) if b == 0 else bias_ref[kv]
                _, _, _, halves = _swa_scores(cur_ref, sinks_ref, qg_s, kn_s, bias, rows, keys, kv)
                e = jnp.concatenate([halves[0][1], halves[1][1]], axis=1)
                v2 = jnp.concatenate([_two_heads(v_s[keys, :], kv).astype(BF16), ones_ref[...]], axis=1)
                ox = _dot_nn(e, v2)
                den = ox[:, LANES:] + jnp.where(lane < HEAD_DIM, halves[0][2], halves[1][2])
                out = (ox[:, :LANES] / den).astype(o_ref.dtype)
                c = kv * 2 * PAIR
                o_ref[rows, c:c + PAIR] = out[:BLOCK]
                o_ref[rows, c + PAIR:c + 2 * PAIR] = out[BLOCK:]

    const3 = pl.BlockSpec((N_KV_HEADS, 2 * BLOCK, 4 * BLOCK), lambda i: (0, 0, 0))
    return pl.pallas_call(
        body, name="swa_fwd", grid=(nt,),
        in_specs=[
            pl.BlockSpec(memory_space=pltpu.SMEM),
            pl.BlockSpec((SWA_TILE, QKV_WIDTH), lambda i: (i, 0)),
            pl.BlockSpec((BLOCK, QKV_WIDTH), lambda i: (jnp.maximum(i * SWA_SUB - 1, 0), 0)),
            pl.BlockSpec((1, ATTN_WIDTH), lambda i: (0, 0)), pl.BlockSpec((1, KV_WIDTH), lambda i: (0, 0)),
            pl.BlockSpec((LANES, LANES), lambda i: (0, 0)), const3, const3,
            pl.BlockSpec((4 * BLOCK, LANES), lambda i: (0, 0)),
        ],
        out_specs=pl.BlockSpec((SWA_TILE, ATTN_WIDTH), lambda i: (i, 0)),
        out_shape=jax.ShapeDtypeStruct((t_rows, 2 * ATTN_WIDTH), BF16),
        scratch_shapes=[pltpu.VMEM((SWA_TILE, ATTN_WIDTH), BF16), pltpu.VMEM((SWA_KEYS, KV_WIDTH), F32),
                        pltpu.VMEM((SWA_KEYS, KV_WIDTH), F32)],
        compiler_params=_cp(),
    )(sinks, u, u, jnp.tile(qg, N_Q_HEADS).reshape(1, ATTN_WIDTH), jnp.tile(kg, N_KV_HEADS).reshape(1, KV_WIDTH),
      seg_c, bias_c, bias_first_c, ones_c)


def swa_bwd(u, dmixed, qg, kg, sinks):
    t_rows = u.shape[0]
    nt = t_rows // SWA_TILE
    bias_c, bias_first_c, seg_c, _ = _swa_constants()

    def body(sinks_ref, cur_ref, prev_ref, do_ref, qg_ref, kg_ref, seg_ref, bias_ref, biasf_ref,
             du_ref, dqg_ref, dkg_ref, dsk_ref, qg_s, kn_s, v_s, acck_s, accv_s, carryk_s, carryv_s):
        step = pl.program_id(0)
        i = nt - 1 - step

        @pl.when(step == 0)
        def _():
            for r in (carryk_s, carryv_s, dqg_ref, dkg_ref, dsk_ref):
                r[...] = jnp.zeros_like(r)

        _swa_prologue(cur_ref, prev_ref, qg_ref, kg_ref, seg_ref, qg_s, kn_s, v_s)
        for acc, carry in ((acck_s, carryk_s), (accv_s, carryv_s)):
            acc[0:SWA_TILE, :] = jnp.zeros((SWA_TILE, KV_WIDTH), F32)
            acc[SWA_TILE:SWA_KEYS, :] = carry[...]

        lane = lax.broadcasted_iota(jnp.int32, (1, LANES), 1)
        g_pair = qg_ref[:, 0:PAIR]
        dqg_acc = jnp.zeros((1, PAIR), F32)
        dsk_acc = jnp.zeros((1, LANES), F32)
        for b in range(SWA_SUB):
            rows = slice(b * BLOCK, (b + 1) * BLOCK)
            keys = slice(b * BLOCK, (b + 2) * BLOCK)
            for kv in range(N_KV_HEADS):
                bias = jnp.where(i == 0, biasf_ref[kv], bias_ref[kv]) if b == 0 else bias_ref[kv]
                q2, k2, t, halves = _swa_scores(cur_ref, sinks_ref, qg_s, kn_s, bias, rows, keys, kv)
                v2 = _two_heads(v_s[keys, :], kv)
                do2 = _pair_rows(do_ref, rows, kv)
                dp = _dot_nt(do2, v2)
                p_parts, dt_parts, coef = [], [], []
                for half, (scale, e, es) in enumerate(halves):
                    cols = slice(half * 2 * BLOCK, (half + 1) * 2 * BLOCK)
                    rden = 1.0 / (jnp.sum(e, axis=-1, keepdims=True) + es)
                    p = e * rden
                    dp_h = dp[:, cols]
                    delta = jnp.sum(p * dp_h, axis=-1, keepdims=True)
                    ds = p * (dp_h - delta)
                    dsink = -(es * rden) * delta
                    for pair in range(2):
                        part = jnp.sum(dsink[pair * BLOCK:(pair + 1) * BLOCK], axis=0, keepdims=True)
                        dsk_acc = dsk_acc + jnp.where(lane == kv * GROUP + 2 * pair + half, part, 0.0)
                    dscale = jnp.sum(ds * t[:, cols], axis=-1, keepdims=True)
                    coef.append(-dscale * scale * scale * scale)
                    p_parts.append(p)
                    dt_parts.append(ds * scale)
                p2 = jnp.concatenate(p_parts, axis=1)
                dt = jnp.concatenate(dt_parts, axis=1)
                dqg2 = _dot_nn(dt, k2)
                q_raw = _pair_rows(cur_ref, rows, kv)
                dq = dqg2 * g_pair + jnp.where(lane < HEAD_DIM, coef[0], coef[1]) * q_raw
                dqg_acc = dqg_acc + jnp.sum(dqg2 * q_raw, axis=0, keepdims=True)
                c = kv * 2 * PAIR
                du_ref[rows, c:c + PAIR] = dq[:BLOCK].astype(du_ref.dtype)
                du_ref[rows, c + PAIR:c + 2 * PAIR] = dq[BLOCK:].astype(du_ref.dtype)
                acck_s[keys, :] += _from_two_heads(_dot_tn(dt, q2), kv)
                accv_s[keys, :] += _from_two_heads(_dot_tn(p2, do2), kv)
        dqg_ref[...] += dqg_acc + pltpu.roll(dqg_acc, HEAD_DIM, 1)
        dsk_ref[...] += dsk_acc

        own = slice(BLOCK, SWA_KEYS)
        k = cur_ref[:, KCOL:KCOL + KV_WIDTH]
        rk = lax.rsqrt(_segmean(k * k, seg_ref) + EPS)
        kh = k * rk
        dkn = acck_s[own, :]
        dkh = dkn * kg_ref[...]
        du_ref[:, KCOL:KCOL + KV_WIDTH] = (rk * (dkh - kh * _segmean(dkh * kh, seg_ref))).astype(du_ref.dtype)
        du_ref[:, VCOL:VCOL + KV_WIDTH] = accv_s[own, :].astype(du_ref.dtype)
        dkg_part = jnp.sum(dkn * kh, axis=0, keepdims=True)
        dkg_ref[...] += dkg_part + pltpu.roll(dkg_part, HEAD_DIM, 1)
        carryk_s[...] = acck_s[0:BLOCK, :]
        carryv_s[...] = accv_s[0:BLOCK, :]

    const3 = pl.BlockSpec((N_KV_HEADS, 2 * BLOCK, 4 * BLOCK), lambda s: (0, 0, 0))
    vec = pl.BlockSpec((1, LANES), lambda s: (0, 0))
    return pl.pallas_call(
        body, name="swa_bwd", grid=(nt,),
        in_specs=[
            pl.BlockSpec(memory_space=pltpu.SMEM),
            pl.BlockSpec((SWA_TILE, QKV_WIDTH), lambda s: (nt - 1 - s, 0)),
            pl.BlockSpec((BLOCK, QKV_WIDTH), lambda s: (jnp.maximum((nt - 1 - s) * SWA_SUB - 1, 0), 0)),
            pl.BlockSpec((SWA_TILE, ATTN_WIDTH), lambda s: (nt - 1 - s, 0)),
            pl.BlockSpec((1, ATTN_WIDTH), lambda s: (0, 0)), vec,
            pl.BlockSpec((LANES, LANES), lambda s: (0, 0)), const3, const3,
        ],
        out_specs=[pl.BlockSpec((SWA_TILE, QKV_WIDTH), lambda s: (nt - 1 - s, 0)), vec, vec, vec],
        out_shape=[jax.ShapeDtypeStruct((t_rows, IN_COLS), BF16)] + [jax.ShapeDtypeStruct((1, LANES), F32)] * 3,
        scratch_shapes=[pltpu.VMEM((SWA_TILE, ATTN_WIDTH), BF16)] + [pltpu.VMEM((SWA_KEYS, KV_WIDTH), F32)] * 4
        + [pltpu.VMEM((BLOCK, KV_WIDTH), F32)] * 2,
        compiler_params=_cp(),
    )(sinks, u, u, dmixed, jnp.tile(qg, N_Q_HEADS).reshape(1, ATTN_WIDTH), jnp.tile(kg, N_KV_HEADS).reshape(1, KV_WIDTH),
      seg_c, bias_c, bias_first_c)


CONV_TILE = 512
CONV_CHUNK = 64
VAL0 = QKV_WIDTH
GATE0 = QKV_WIDTH + CONV_CH


def _glu(ref):
    return ref[:, VAL0:GATE0] * _sigmoid(ref[:, GATE0:GATE0 + CONV_CH])


SUBLANES = 8
CONV_BUF = CONV_HALO + CONV_TILE + SUBLANES
CONV_EXT = CONV_HALO + CONV_TILE


def _fill_shifted(sh_ref):
    for r in range(1, SUBLANES):
        sh_ref[r, 0:CONV_EXT, :] = sh_ref[0, pl.ds(r, CONV_EXT), :]


def _shifted(sh_ref, start, offset, n):
    return sh_ref[offset % SUBLANES, pl.ds(start + offset - offset % SUBLANES, n), :]


def _layernorm_stats(y):
    mu = jnp.mean(y, axis=-1, keepdims=True)
    yc = y - mu
    rstd = lax.rsqrt(jnp.mean(yc * yc, axis=-1, keepdims=True) + EPS)
    return yc * rstd, rstd


def conv_fwd(u, mixed, conv_w, conv_b, ln_g, ln_b):
    t = u.shape[0]
    nt = t // CONV_TILE
    per = CONV_TILE // CONV_HALO

    def body(cur_ref, prev_ref, mixed_ref, w_ref, b_ref, g_ref, b2_ref, o_ref, y_ref, gl_ref):
        del mixed_ref
        i = pl.program_id(0)
        gl_ref[0, 0:CONV_HALO, :] = jnp.where(i > 0, _glu(prev_ref), 0.0)
        gl_ref[0, CONV_HALO:CONV_EXT, :] = _glu(cur_ref)
        gl_ref[0, CONV_EXT:CONV_BUF, :] = jnp.zeros((SUBLANES, CONV_CH), F32)
        _fill_shifted(gl_ref)
        for c0 in range(0, CONV_TILE, CONV_CHUNK):
            acc = jnp.broadcast_to(b_ref[...], (CONV_CHUNK, CONV_CH))
            for k in range(CONV_K):
                acc = acc + w_ref[k:k + 1, :] * _shifted(gl_ref, c0, 2 + k, CONV_CHUNK)
            y_ref[c0:c0 + CONV_CHUNK, :] = acc
        yh, _ = _layernorm_stats(y_ref[...])
        yln = yh * g_ref[...] + b2_ref[...]
        o_ref[...] = (yln * _sigmoid(yln)).astype(o_ref.dtype)

    vec = pl.BlockSpec((1, CONV_CH), lambda i: (0, 0))
    return pl.pallas_call(
        body, name="conv_fwd", grid=(nt,),
        in_specs=[
            pl.BlockSpec((CONV_TILE, IN_COLS), lambda i: (i, 0)),
            pl.BlockSpec((CONV_HALO, IN_COLS), lambda i: (jnp.maximum(i * per - 1, 0), 0)),
            pl.BlockSpec(memory_space=pl.ANY),
            pl.BlockSpec((CONV_HALO, CONV_CH), lambda i: (0, 0)),
            vec, vec, vec,
        ],
        out_specs=[pl.BlockSpec((CONV_TILE, CONV_CH), lambda i: (i, 1)), pl.BlockSpec((CONV_TILE, CONV_CH), lambda i: (i, 0))],
        out_shape=[jax.ShapeDtypeStruct(mixed.shape, mixed.dtype), jax.ShapeDtypeStruct((t, CONV_CH), F32)],
        scratch_shapes=[pltpu.VMEM((SUBLANES, CONV_BUF, CONV_CH), F32)],
        input_output_aliases={2: 0}, compiler_params=_cp(),
    )(u, u, mixed, conv_w, conv_b.reshape(1, CONV_CH), ln_g.reshape(1, CONV_CH), ln_b.reshape(1, CONV_CH))


def conv_bwd(u, y, dmixed, du, conv_w, ln_g, ln_b):
    t = u.shape[0]
    nt = t // CONV_TILE
    per = CONV_TILE // CONV_HALO

    def body(cur_ref, prev_ref, y_ref, yn_ref, do_ref, don_ref, du_in_ref, w_ref, g_ref, b2_ref,
             du_ref, dw_ref, dvec_ref, gl_ref, dy_ref):
        i = pl.program_id(0)
        last = i == nt - 1
        _zero_at_first_step(dw_ref, dvec_ref)

        gl_ref[0, 0:CONV_HALO, :] = jnp.where(i > 0, _glu(prev_ref), 0.0)
        gl_ref[0, CONV_HALO:CONV_EXT, :] = _glu(cur_ref)
        gl_ref[0, CONV_EXT:CONV_BUF, :] = jnp.zeros((SUBLANES, CONV_CH), F32)
        _fill_shifted(gl_ref)

        yh, rstd = _layernorm_stats(jnp.concatenate([y_ref[...], yn_ref[...]], axis=0))
        g = g_ref[...]
        yln = yh * g + b2_ref[...]
        sg = _sigmoid(yln)
        dout = jnp.concatenate([do_ref[...], jnp.where(last, 0.0, don_ref[...])], axis=0)
        dyln = dout * (sg * (1.0 + yln * (1.0 - sg)))
        dyh = dyln * g
        dy = rstd * (dyh - jnp.mean(dyh, axis=-1, keepdims=True) - yh * jnp.mean(dyh * yh, axis=-1, keepdims=True))
        dy_ref[0, 0:CONV_EXT, :] = dy
        dy_ref[0, CONV_EXT:CONV_BUF, :] = jnp.zeros((SUBLANES, CONV_CH), F32)
        _fill_shifted(dy_ref)

        own = slice(0, CONV_TILE)
        dvec_ref[0:1, :] += jnp.sum(dy[own], axis=0, keepdims=True)
        dvec_ref[1:2, :] += jnp.sum(dyln[own] * yh[own], axis=0, keepdims=True)
        dvec_ref[2:3, :] += jnp.sum(dyln[own], axis=0, keepdims=True)
        for k in range(CONV_K):
            dw_ref[k:k + 1, :] += jnp.sum(dy[own] * _shifted(gl_ref, 0, 2 + k, CONV_TILE), axis=0, keepdims=True)

        for c0 in range(0, CONV_TILE, CONV_CHUNK):
            acc = jnp.zeros((CONV_CHUNK, CONV_CH), F32)
            for k in range(CONV_K):
                acc = acc + w_ref[k:k + 1, :] * _shifted(dy_ref, c0, CONV_K - 1 - k, CONV_CHUNK)
            rows = slice(c0, c0 + CONV_CHUNK)
            val = cur_ref[rows, VAL0:GATE0]
            sgate = _sigmoid(cur_ref[rows, GATE0:GATE0 + CONV_CH])
            du_ref[rows, VAL0:GATE0] = (acc * sgate).astype(du_ref.dtype)
            du_ref[rows, GATE0:GATE0 + CONV_CH] = (acc * val * sgate * (1.0 - sgate)).astype(du_ref.dtype)
        du_ref[:, 0:QKV_WIDTH] = du_in_ref[:, 0:QKV_WIDTH]

    vec = pl.BlockSpec((1, CONV_CH), lambda i: (0, 0))
    n_halo = t // CONV_HALO
    return pl.pallas_call(
        body, name="conv_bwd", grid=(nt,),
        in_specs=[
            pl.BlockSpec((CONV_TILE, IN_COLS), lambda i: (i, 0)),
            pl.BlockSpec((CONV_HALO, IN_COLS), lambda i: (jnp.maximum(i * per - 1, 0), 0)),
            pl.BlockSpec((CONV_TILE, CONV_CH), lambda i: (i, 0)),
            pl.BlockSpec((CONV_HALO, CONV_CH), lambda i: (jnp.minimum((i + 1) * per, n_halo - 1), 0)),
            pl.BlockSpec((CONV_TILE, CONV_CH), lambda i: (i, 1)),
            pl.BlockSpec((CONV_HALO, CONV_CH), lambda i: (jnp.minimum((i + 1) * per, n_halo - 1), 1)),
            pl.BlockSpec((CONV_TILE, IN_COLS), lambda i: (i, 0)),
            pl.BlockSpec((CONV_HALO, CONV_CH), lambda i: (0, 0)),
            vec, vec,
        ],
        out_specs=[
            pl.BlockSpec((CONV_TILE, IN_COLS), lambda i: (i, 0)),
            pl.BlockSpec((CONV_HALO, CONV_CH), lambda i: (0, 0)),
            pl.BlockSpec((8, CONV_CH), lambda i: (0, 0)),
        ],
        out_shape=[
            jax.ShapeDtypeStruct(du.shape, du.dtype),
            jax.ShapeDtypeStruct((CONV_HALO, CONV_CH), F32),
            jax.ShapeDtypeStruct((8, CONV_CH), F32),
        ],
        scratch_shapes=[pltpu.VMEM((SUBLANES, CONV_BUF, CONV_CH), F32), pltpu.VMEM((SUBLANES, CONV_BUF, CONV_CH), F32)],
        input_output_aliases={6: 0}, compiler_params=_cp(),
    )(u, u, y, y, dmixed, dmixed, du, conv_w, ln_g.reshape(1, CONV_CH), ln_b.reshape(1, CONV_CH))


def _xattn_probs(qn, kn):
    s = _dot_nt(qn, kn) * (X_HEAD_DIM ** -0.5)
    e = jnp.exp(s - jnp.max(s, axis=-1, keepdims=True))
    return e / jnp.sum(e, axis=-1, keepdims=True)


def adamw(w, g, m, v, *, name):
    r, c = w.shape
    tr = r
    for cand in (512, 256, 128, 64, 32, 16, 8):
        if r % cand == 0 and r > cand:
            tr = cand
            break

    def body(w_ref, g_ref, m_ref, v_ref, d_ref, nm_ref, nv_ref):
        g_v = g_ref[...]
        m2 = ADAM_B1 * m_ref[...] + (1.0 - ADAM_B1) * g_v
        v2 = ADAM_B2 * v_ref[...] + (1.0 - ADAM_B2) * jnp.square(g_v)
        m_hat = m2 / (1.0 - ADAM_B1 ** ADAM_STEP)
        v_hat = v2 / (1.0 - ADAM_B2 ** ADAM_STEP)
        d_ref[...] = -ADAM_LR * (m_hat / (jnp.sqrt(v_hat) + ADAM_EPS) + ADAM_WD * w_ref[...])
        nm_ref[...] = m2
        nv_ref[...] = v2

    spec = pl.BlockSpec((tr, c), lambda i: (i, 0))
    shape = jax.ShapeDtypeStruct((r, c), F32)
    return pl.pallas_call(
        body, name=name, grid=(r // tr,), in_specs=[spec] * 4, out_specs=[spec] * 3,
        out_shape=[shape] * 3, compiler_params=_cp(),
    )(w, g, m, v)


def _position():
    return lax.axis_index("x"), lax.axis_index("y"), lax.axis_index("c")


def all_gather(shard, *, name, in_vmem):
    r, c_ = shard.shape

    def body(x_ref, out_ref, token_ref, send_sems, recv_sems, local_sem):
        x, y, c = _position()
        me, sibling = (x, y, c), (x, y, 1 - c)
        chips = [(1 - x, y), (x, 1 - y), (1 - x, 1 - y)]
        token_ref[...] = jnp.zeros_like(token_ref)

        def rows(px, py, pc):
            return out_ref.at[4 * px + 2 * py + pc]

        def copy(k, block, to, src=None):
            return pltpu.make_async_remote_copy(
                src_ref=rows(*block) if src is None else src, dst_ref=rows(*block),
                send_sem=send_sems.at[k], recv_sem=recv_sems.at[k], device_id=to, device_id_type=MESH)

        mine = pltpu.make_async_copy(x_ref, rows(*me), local_sem)
        mine.start()
        first = [copy(0, me, sibling, src=x_ref)]
        first += [copy(1 + j, me, (*chip, c), src=x_ref) for j, chip in enumerate(chips)]
        for cp in first:
            cp.start()
        passed = [copy(4 + j, (*chip, c), sibling) for j, chip in enumerate(chips)]
        for j, chip in enumerate(chips):
            copy(1 + j, (*chip, c), me).wait_recv()
            passed[j].start()
        copy(0, sibling, me).wait_recv()
        for j, chip in enumerate(chips):
            copy(4 + j, (*chip, 1 - c), me).wait_recv()
        for cp in first + passed:
            cp.wait_send()
        mine.wait()

    space = pltpu.VMEM if in_vmem else pltpu.HBM
    return pl.pallas_call(
        body, name=name,
        out_shape=[jax.ShapeDtypeStruct((N_DEV, r, c_), shard.dtype), jax.ShapeDtypeStruct((8, LANES), F32)],
        in_specs=[pl.BlockSpec(memory_space=space)],
        out_specs=[pl.BlockSpec(memory_space=space), pl.BlockSpec(memory_space=pltpu.VMEM)],
        scratch_shapes=[pltpu.SemaphoreType.DMA((7,)), pltpu.SemaphoreType.DMA((7,)), pltpu.SemaphoreType.DMA],
        compiler_params=_cp(),
    )(shard)


def all_gather_many(shards, *, name):
    n = len(shards)

    def body(*refs):
        x_refs, out_refs, token_ref = refs[:n], refs[n:2 * n], refs[2 * n]
        send_sems, recv_sems, local_sems = refs[2 * n + 1:]
        x, y, c = _position()
        me, sibling = (x, y, c), (x, y, 1 - c)
        chips = [(1 - x, y), (x, 1 - y), (1 - x, 1 - y)]
        token_ref[...] = jnp.zeros_like(token_ref)

        def rows(t, px, py, pc):
            return out_refs[t].at[4 * px + 2 * py + pc]

        def copy(t, k, block, to, src=None):
            return pltpu.make_async_remote_copy(
                src_ref=rows(t, *block) if src is None else src, dst_ref=rows(t, *block),
                send_sem=send_sems.at[7 * t + k], recv_sem=recv_sems.at[7 * t + k], device_id=to, device_id_type=MESH)

        mine = [pltpu.make_async_copy(x_refs[t], rows(t, *me), local_sems.at[t]) for t in range(n)]
        for cp in mine:
            cp.start()
        first = []
        for t in range(n):
            first.append(copy(t, 0, me, sibling, src=x_refs[t]))
            first += [copy(t, 1 + j, me, (*chip, c), src=x_refs[t]) for j, chip in enumerate(chips)]
        for cp in first:
            cp.start()
        passed = []
        for t in range(n):
            for j, chip in enumerate(chips):
                copy(t, 1 + j, (*chip, c), me).wait_recv()
                passed.append(copy(t, 4 + j, (*chip, c), sibling))
                passed[-1].start()
        for t in range(n):
            copy(t, 0, sibling, me).wait_recv()
            for j, chip in enumerate(chips):
                copy(t, 4 + j, (*chip, 1 - c), me).wait_recv()
        for cp in first + passed:
            cp.wait_send()
        for cp in mine:
            cp.wait()

    hbm = pl.BlockSpec(memory_space=pltpu.HBM)
    out = pl.pallas_call(
        body, name=name,
        out_shape=[jax.ShapeDtypeStruct((N_DEV,) + s.shape, s.dtype) for s in shards] + [jax.ShapeDtypeStruct((8, LANES), F32)],
        in_specs=[hbm] * n, out_specs=[hbm] * n + [pl.BlockSpec(memory_space=pltpu.VMEM)],
        scratch_shapes=[pltpu.SemaphoreType.DMA((7 * n,)), pltpu.SemaphoreType.DMA((7 * n,)), pltpu.SemaphoreType.DMA((n,))],
        compiler_params=_cp(),
    )(*shards)
    return out[:n], out[n]


def exchange(srcs, plan, n_copies, slots, *, name):
    n = len(srcs)

    def body(*refs):
        src_refs, out_refs, (send_sems, recv_sems) = refs[:n], refs[n:2 * n], refs[2 * n:]
        copies = []
        for t in range(n):
            for k in range(n_copies):
                s, d, to = plan(src_refs[t], out_refs[t], k)
                copies.append(pltpu.make_async_remote_copy(
                    src_ref=s, dst_ref=d, send_sem=send_sems.at[n_copies * t + k], recv_sem=recv_sems.at[n_copies * t + k],
                    device_id=to, device_id_type=MESH))
        for cp in copies:
            cp.start()
        for cp in copies:
            cp.wait_recv()
        for cp in copies:
            cp.wait_send()

    hbm = pl.BlockSpec(memory_space=pltpu.HBM)
    return pl.pallas_call(
        body, name=name,
        out_shape=[jax.ShapeDtypeStruct((slots,) + s.shape[1:], s.dtype) for s in srcs],
        in_specs=[hbm] * n, out_specs=[hbm] * n,
        scratch_shapes=[pltpu.SemaphoreType.DMA((n_copies * n,)), pltpu.SemaphoreType.DMA((n_copies * n,))],
        compiler_params=_cp(),
    )(*srcs)


_HBM = pl.BlockSpec(memory_space=pltpu.HBM)
_SEM = pl.BlockSpec(memory_space=pltpu.SEMAPHORE)
_EFFECT = pltpu.SideEffectType.DATAFLOW_SIDE_EFFECTING


def _split_copies(src_refs, land_refs, send_sems, recv_sems, plan, n_copies):
    copies = []
    for t, (src_ref, land_ref) in enumerate(zip(src_refs, land_refs)):
        for k in range(n_copies):
            s, d, to = plan(src_ref, land_ref, k)
            copies.append(pltpu.make_async_remote_copy(
                src_ref=s, dst_ref=d, send_sem=send_sems.at[n_copies * t + k], recv_sem=recv_sems.at[n_copies * t + k],
                device_id=to, device_id_type=MESH))
    return copies


def split_start(srcs, lands, plan, n_copies, *, name):
    n = len(srcs)

    def body(*refs):
        src_refs, land_refs, send_sems, recv_sems, token = refs[:n], refs[n:2 * n], refs[2 * n], refs[2 * n + 1], refs[-1]
        for cp in _split_copies(src_refs, land_refs, send_sems, recv_sems, plan, n_copies):
            cp.start()
        token[...] = jnp.zeros_like(token)

    both = list(srcs) + list(lands)
    out = pl.pallas_call(
        body, name=name,
        out_shape=(pltpu.SemaphoreType.DMA((n_copies * n,)), pltpu.SemaphoreType.DMA((n_copies * n,)),
                   *[pltpu.HBM(a.shape, a.dtype) for a in both], jax.ShapeDtypeStruct((8, LANES), F32)),
        in_specs=(_HBM,) * (2 * n), out_specs=(_SEM, _SEM) + (_HBM,) * (2 * n) + (pl.BlockSpec(memory_space=pltpu.VMEM),),
        input_output_aliases={i: 2 + i for i in range(2 * n)},
        compiler_params=pltpu.CompilerParams(has_side_effects=_EFFECT),
    )(*[pltpu.with_memory_space_constraint(a, pltpu.HBM) for a in both])
    return out[0], out[1], list(out[2:2 + n]), list(out[2 + n:2 + 2 * n]), out[-1]


def split_wait(started, after, plan, n_copies, *, name):
    send_sems, recv_sems, srcs, lands, _ = started
    n = len(srcs)

    def body(*refs):
        src_refs, land_refs, send_sems, recv_sems = refs[:n], refs[n:2 * n], refs[2 * n], refs[2 * n + 1]
        for cp in _split_copies(src_refs, land_refs, send_sems, recv_sems, plan, n_copies):
            cp.wait_send()
            cp.wait_recv()

    both = list(srcs) + list(lands)
    out = pl.pallas_call(
        body, name=name,
        out_shape=tuple(pltpu.HBM(a.shape, a.dtype) for a in both),
        in_specs=(_HBM,) * (2 * n) + (_SEM, _SEM, pl.BlockSpec(memory_space=pl.ANY)), out_specs=(_HBM,) * (2 * n),
        input_output_aliases={i: i for i in range(2 * n)},
        compiler_params=pltpu.CompilerParams(has_side_effects=_EFFECT),
    )(*both, send_sems, recv_sems, after)
    return list(out[:n]), list(out[n:])


def _other_chips(x, y):
    return [(1 - x, y), (x, 1 - y), (1 - x, 1 - y)]


def _remote(src, dst, send_sem, recv_sem, to):
    return pltpu.make_async_remote_copy(src_ref=src, dst_ref=dst, send_sem=send_sem, recv_sem=recv_sem,
                                        device_id=to, device_id_type=MESH)


def gather_start(groups, *, name):
    counts = [len(shards) for shards, _ in groups]
    flat = [a for shards, _ in groups for a in shards] + [a for _, lands in groups for a in lands]
    n_all, n_groups = sum(counts), len(groups)

    def body(*refs):
        s_refs, l_refs = refs[:n_all], refs[n_all:2 * n_all]
        sems = refs[2 * n_all:2 * n_all + 3 * n_groups]
        x, y, c = _position()
        me = 4 * x + 2 * y + c
        at = 0
        for gi, n in enumerate(counts):
            send, recv_sibling, recv_ici = sems[3 * gi:3 * gi + 3]
            for t in range(n):
                src, dst = s_refs[at + t], l_refs[at + t].at[me]
                _remote(src, dst, send.at[4 * t], recv_sibling.at[t], (x, y, 1 - c)).start()
                for j, chip in enumerate(_other_chips(x, y)):
                    _remote(src, dst, send.at[4 * t + 1 + j], recv_ici.at[3 * t + j], (*chip, c)).start()
            at += n
        refs[-1][...] = jnp.zeros_like(refs[-1])

    sem_shapes = [pltpu.SemaphoreType.DMA((k * n,)) for n in counts for k in (4, 1, 3)]
    out = pl.pallas_call(
        body, name=name,
        out_shape=(*sem_shapes, *[pltpu.HBM(a.shape, a.dtype) for a in flat], jax.ShapeDtypeStruct((8, LANES), F32)),
        in_specs=(_HBM,) * (2 * n_all),
        out_specs=(_SEM,) * (3 * n_groups) + (_HBM,) * (2 * n_all) + (pl.BlockSpec(memory_space=pltpu.VMEM),),
        input_output_aliases={i: 3 * n_groups + i for i in range(2 * n_all)},
        compiler_params=pltpu.CompilerParams(has_side_effects=_EFFECT),
    )(*[pltpu.with_memory_space_constraint(a, pltpu.HBM) for a in flat])
    thru = out[3 * n_groups:-1]
    states, at = [], 0
    for gi, n in enumerate(counts):
        states.append(dict(shards=list(thru[at:at + n]), lands=list(thru[n_all + at:n_all + at + n]),
                           send=out[3 * gi], recv_sibling=out[3 * gi + 1], recv_ici=out[3 * gi + 2]))
        at += n
    return states, out[-1]


def gather_forward(states, after, *, name):
    counts = [len(s["lands"]) for s in states]
    flat = [a for s in states for a in s["lands"]]
    n_all, n_groups = sum(counts), len(states)

    def body(*refs):
        l_refs = refs[:n_all]
        recv_ici = refs[n_all:n_all + n_groups]
        fwd = refs[n_all + n_groups + 1:n_all + n_groups + 1 + 2 * n_groups]
        x, y, c = _position()
        at = 0
        for gi, n in enumerate(counts):
            fwd_send, fwd_recv = fwd[2 * gi], fwd[2 * gi + 1]
            for t in range(n):
                for j, (px, py) in enumerate(_other_chips(x, y)):
                    block = l_refs[at + t].at[4 * px + 2 * py + c]
                    _remote(block, block, fwd_send.at[3 * t + j], recv_ici[gi].at[3 * t + j], (px, py, c)).wait_recv()
                    _remote(block, block, fwd_send.at[3 * t + j], fwd_recv.at[3 * t + j], (x, y, 1 - c)).start()
            at += n
        refs[-1][...] = jnp.zeros_like(refs[-1])

    sem_shapes = [pltpu.SemaphoreType.DMA((3 * n,)) for n in counts for _ in range(2)]
    out = pl.pallas_call(
        body, name=name,
        out_shape=(*sem_shapes, *[pltpu.HBM(a.shape, a.dtype) for a in flat], jax.ShapeDtypeStruct((8, LANES), F32)),
        in_specs=(_HBM,) * n_all + (_SEM,) * n_groups + (pl.BlockSpec(memory_space=pl.ANY),),
        out_specs=(_SEM,) * (2 * n_groups) + (_HBM,) * n_all + (pl.BlockSpec(memory_space=pltpu.VMEM),),
        input_output_aliases={i: 2 * n_groups + i for i in range(n_all)},
        compiler_params=pltpu.CompilerParams(has_side_effects=_EFFECT),
    )(*flat, *[s["recv_ici"] for s in states], after)
    at = 0
    for gi, (s, n) in enumerate(zip(states, counts)):
        s.update(fwd_send=out[2 * gi], fwd_recv=out[2 * gi + 1], lands=list(out[2 * n_groups + at:2 * n_groups + at + n]))
        at += n
    return out[-1]


def gather_finish(state, after, *, name):
    n = len(state["lands"])

    def body(*refs):
        s_refs, l_refs = refs[:n], refs[n:2 * n]
        send, recv_sibling, fwd_send, fwd_recv = refs[2 * n:2 * n + 4]
        x, y, c = _position()
        me = 4 * x + 2 * y + c
        for t in range(n):
            own = l_refs[t].at[me]
            _remote(s_refs[t], own, send.at[4 * t], recv_sibling.at[t], (x, y, 1 - c)).wait_send()
            _remote(s_refs[t], l_refs[t].at[4 * x + 2 * y + 1 - c], send.at[4 * t], recv_sibling.at[t], (x, y, 1 - c)).wait_recv()
            for j, (px, py) in enumerate(_other_chips(x, y)):
                _remote(s_refs[t], own, send.at[4 * t + 1 + j], recv_sibling.at[t], (px, py, c)).wait_send()
                mine, theirs = l_refs[t].at[4 * px + 2 * py + c], l_refs[t].at[4 * px + 2 * py + 1 - c]
                _remote(mine, mine, fwd_send.at[3 * t + j], fwd_recv.at[3 * t + j], (x, y, 1 - c)).wait_send()
                _remote(theirs, theirs, fwd_send.at[3 * t + j], fwd_recv.at[3 * t + j], (x, y, 1 - c)).wait_recv()

    both = state["shards"] + state["lands"]
    out = pl.pallas_call(
        body, name=name,
        out_shape=tuple(pltpu.HBM(a.shape, a.dtype) for a in both),
        in_specs=(_HBM,) * (2 * n) + (_SEM,) * 4 + (pl.BlockSpec(memory_space=pl.ANY),), out_specs=(_HBM,) * (2 * n),
        input_output_aliases={i: i for i in range(2 * n)},
        compiler_params=pltpu.CompilerParams(has_side_effects=_EFFECT),
    )(*both, state["send"], state["recv_sibling"], state["fwd_send"], state["fwd_recv"], after)
    return list(out[n:])


def _sibling_plan(src_ref, land_ref, k):
    x, y, c = _position()
    return src_ref.at[2 * k + (1 - c)], land_ref.at[k], (x, y, 1 - c)


def _chips_plan(src_ref, land_ref, j):
    x, y, c = _position()
    px, py = [(1 - x, y), (x, 1 - y), (1 - x, 1 - y)][j]
    return src_ref.at[2 * px + py], land_ref.at[j], (px, py, c)


SUM_STEPS = 2


def sum_for_chips(parts, from_sibling, c_idx, *, name):
    n = len(parts)

    def body(c_ref, *refs):
        del c_ref
        for t in range(n):
            refs[2 * n + t][...] = (refs[t][...] + refs[n + t][...]).astype(BF16)

    def blk(a):
        return (None, a.shape[1] // SUM_STEPS, a.shape[2])

    return pl.pallas_call(
        body, name=name,
        grid_spec=pltpu.PrefetchScalarGridSpec(
            num_scalar_prefetch=1, grid=(4, SUM_STEPS),
            in_specs=[pl.BlockSpec(blk(a), lambda k, i, c_ref: (2 * k + c_ref[0], i, 0)) for a in parts]
            + [pl.BlockSpec(blk(a), lambda k, i, c_ref: (k, i, 0)) for a in from_sibling],
            out_specs=[pl.BlockSpec(blk(a), lambda k, i, c_ref: (k, i, 0)) for a in from_sibling]),
        out_shape=[jax.ShapeDtypeStruct(a.shape, BF16) for a in from_sibling], compiler_params=_cp(),
    )(c_idx, *parts, *from_sibling)


def sum_final(parts, from_sibling, from_chips, kc_idx, *, name):
    n = len(parts)

    def body(kc_ref, *refs):
        del kc_ref
        for t in range(n):
            p, s, a, b, d = (refs[j * n + t] for j in range(5))
            refs[5 * n + t][...] = (((p[...] + s[...]) + a[...].astype(F32)) + b[...].astype(F32)) + d[...].astype(F32)

    def blk(a):
        return (None, a.shape[1] // SUM_STEPS, a.shape[2])

    def chip_specs(j):
        return [pl.BlockSpec(blk(a), lambda i, kc: (j, i, 0)) for a in from_chips]

    return pl.pallas_call(
        body, name=name,
        grid_spec=pltpu.PrefetchScalarGridSpec(
            num_scalar_prefetch=1, grid=(SUM_STEPS,),
            in_specs=[pl.BlockSpec(blk(a), lambda i, kc: (2 * kc[0] + kc[1], i, 0)) for a in parts]
            + [pl.BlockSpec(blk(a), lambda i, kc: (kc[0], i, 0)) for a in from_sibling]
            + chip_specs(0) + chip_specs(1) + chip_specs(2),
            out_specs=[pl.BlockSpec(blk(a)[1:], lambda i, kc: (i, 0)) for a in parts]),
        out_shape=[jax.ShapeDtypeStruct(a.shape[1:], F32) for a in parts], compiler_params=_cp(),
    )(kc_idx, *parts, *from_sibling, *from_chips, *from_chips, *from_chips)


def sum_devices(gathered):
    n, r, c_ = gathered.shape

    def body(g_ref, o_ref):
        acc = g_ref[0]
        for k in range(1, n):
            acc = acc + g_ref[k]
        o_ref[...] = acc

    return pl.pallas_call(
        body, name="sum_devices", out_shape=jax.ShapeDtypeStruct((r, c_), F32), compiler_params=_cp(),
    )(gathered)


BIG = (
    ("w_in", IN_COLS, True), ("w_out", D_MODEL, False), ("wq_x", D_MODEL, False), ("wkv_x", 2 * D_MODEL, True),
    ("wo_x", D_MODEL, False), ("w_gate_up", 2 * D_FF, True), ("w_down", D_FF, False),
)
SHARD_ROWS = sum(rows // N_DEV for _, rows, _ in BIG)

SMALL = ("norm_mix_g", "q_norm_g", "k_norm_g", "sinks", "conv_b", "conv_ln_g", "conv_ln_b",
         "norm_x_g", "norm_mem_g", "xq_norm_g", "xk_norm_g", "norm_ffn_g")


def _pack_rows(vectors, width=LANES, row_multiple=8):
    flat = jnp.concatenate([v.reshape(-1) for v in vectors])
    per = width * row_multiple
    padded = -(-flat.shape[0] // per) * per
    return jnp.pad(flat, (0, padded - flat.shape[0])).reshape(-1, width)


def _unpack_rows(packed, shapes):
    flat = packed.reshape(-1)
    out, at = [], 0
    for s in shapes:
        n = 1
        for dim in s:
            n *= dim
        out.append(flat[at:at + n].reshape(s))
        at += n
    return out


WEIGHT_GROUPS = {"in": ("w_in",), "mid": ("w_out", "wq_x", "wkv_x", "wo_x"), "ffn": ("w_gate_up", "w_down")}


def _layer_fwd(x0, mem, weights_of, s, reached):
    w = dict(weights_of("in", x0))
    h0, u = norm_proj(x0, s["norm_mix_g"], w["w_in"])
    mixed = swa_fwd(u, s["q_norm_g"], s["k_norm_g"], s["sinks"])
    reached("attn", mixed)
    mixed, conv_y = conv_fwd(u, mixed, s["conv_w"], s["conv_b"], s["conv_ln_g"], s["conv_ln_b"])
    w.update(weights_of("mid", conv_y))
    memn = rms_fwd(mem, s["norm_mem_g"])
    kv = mm(memn, w["wkv_x"], trans_b=True, out_dtype=F32, name="mm_kv")
    x1, h1, qx, o, x2, h2 = mid_fwd(mixed, x0, w["w_out"], s["norm_x_g"], w["wq_x"], kv, s["xq_norm_g"], s["xk_norm_g"],
                                    w["wo_x"], s["norm_ffn_g"])
    reached("mid", x2)
    w.update(weights_of("ffn", x2))
    gu, a, x3 = ffn_fwd(h2, x2, w["w_gate_up"], w["w_down"])
    saved = dict(x0=x0, h0=h0, u=u, conv_y=conv_y, mixed=mixed, x1=x1, h1=h1, qx=qx, memn=memn, kv=kv, o=o, x2=x2, h2=h2,
                 gu=gu, a=a)
    return x3, saved, w


def _ordered_after(a, token):
    return a if token is None else a + token[0, 0]


def _layer_bwd(dx3, mem, w, s, sv, token, stage_done):
    gs = {}
    dgu, dx2, dg = ffn_bwd(dx3, sv["gu"], sv["x2"], _ordered_after(s["norm_ffn_g"], token), w["w_down"], w["w_gate_up"])
    gs["norm_ffn_g"] = dg
    gb = {"w_down": mm_tn(sv["a"], dx3, name="mm_dw_down")}
    gb["w_gate_up"] = mm_tn(dgu, sv["h2"], name="mm_dw_gate_up")
    token = stage_done("ffn", gb, gb["w_gate_up"])

    gb = {}
    dq, dx1, dmixed, dkv, dqg, dkg, dg = mid_bwd(dx2, sv["qx"], sv["kv"], s["xq_norm_g"], s["xk_norm_g"], sv["x1"],
                                                 _ordered_after(s["norm_x_g"], token), w["wo_x"], w["wq_x"], w["w_out"])
    gs["xq_norm_g"], gs["xk_norm_g"], gs["norm_x_g"] = dqg, dkg, dg
    gb["wo_x"] = mm_tn(sv["o"], dx2, name="mm_dwo")
    gb["wq_x"] = mm_tn(sv["h1"], dq, name="mm_dwq")
    dmemn = mm(dkv, w["wkv_x"], trans_b=False, out_dtype=F32, name="mm_dmemn")
    gb["wkv_x"] = mm_tn(dkv, sv["memn"], name="mm_dwkv")
    _, dg = rms_bwd(dmemn, mem, s["norm_mem_g"], None)
    gs["norm_mem_g"] = dg
    gb["w_out"] = mm_tn(sv["mixed"], dx1, name="mm_dw_out")
    token = stage_done("mid", gb, gb["w_out"])

    du, dqg, dkg, dsinks = swa_bwd(sv["u"], dmixed, _ordered_after(s["q_norm_g"], token), s["k_norm_g"], s["sinks"])
    gs["q_norm_g"], gs["k_norm_g"], gs["sinks"] = dqg[0, :HEAD_DIM], dkg[0, :HEAD_DIM], dsinks[0, :N_Q_HEADS]
    token = stage_done("attn", {}, dqg)
    du, dconv_w, dvec = conv_bwd(sv["u"], sv["conv_y"], dmixed, du, s["conv_w"], _ordered_after(s["conv_ln_g"], token),
                                 s["conv_ln_b"])
    gs["conv_w"] = dconv_w[:CONV_K]
    gs["conv_b"], gs["conv_ln_g"], gs["conv_ln_b"] = dvec[0], dvec[1], dvec[2]
    dx0, dg = in_bwd(du, w["w_in"], sv["x0"], s["norm_mix_g"], dx1)
    gs["norm_mix_g"] = dg
    token = stage_done("mix", {"w_in": mm_tn(du, sv["h0"], name="mm_dw_in")}, dx0)
    return dx0, gs, token


def _local_step(x, mem, target, weights_of, reached, smalls, stage_done):
    saved, weights = [], []
    h = x
    for l in range(DEPTH):
        h, sv, w = _layer_fwd(h, mem, functools.partial(weights_of, l), smalls[l], functools.partial(reached, l))
        saved.append(sv)
        weights.append(w)
    dx, loss_part = loss_head(h, target)
    gss, token = [None] * DEPTH, None
    for l in reversed(range(DEPTH)):
        dx, gss[l], token = _layer_bwd(dx, mem, weights[l], smalls[l], saved[l], token,
                                       functools.partial(stage_done, l))
    return loss_part[0, 0], dx, gss


def kernel(x, mem, norm_mix_g, w_in, q_norm_g, k_norm_g, sinks, conv_w, conv_b, conv_ln_g, conv_ln_b, w_out, norm_x_g, norm_mem_g, wq_x, wkv_x, xq_norm_g, xk_norm_g, wo_x, norm_ffn_g, w_gate_up, w_down, loss_target, m_norm_mix_g, m_w_in, m_q_norm_g, m_k_norm_g, m_sinks, m_conv_w, m_conv_b, m_conv_ln_g, m_conv_ln_b, m_w_out, m_norm_x_g, m_norm_mem_g, m_wq_x, m_wkv_x, m_xq_norm_g, m_xk_norm_g, m_wo_x, m_norm_ffn_g, m_w_gate_up, m_w_down, v_norm_mix_g, v_w_in, v_q_norm_g, v_k_norm_g, v_sinks, v_conv_w, v_conv_b, v_conv_ln_g, v_conv_ln_b, v_w_out, v_norm_x_g, v_norm_mem_g, v_wq_x, v_wkv_x, v_xq_norm_g, v_xk_norm_g, v_wo_x, v_norm_ffn_g, v_w_gate_up, v_w_down):
    P = dict(norm_mix_g=norm_mix_g, w_in=w_in, q_norm_g=q_norm_g, k_norm_g=k_norm_g, sinks=sinks, conv_w=conv_w, conv_b=conv_b,
             conv_ln_g=conv_ln_g, conv_ln_b=conv_ln_b, w_out=w_out, norm_x_g=norm_x_g, norm_mem_g=norm_mem_g, wq_x=wq_x,
             wkv_x=wkv_x, xq_norm_g=xq_norm_g, xk_norm_g=xk_norm_g, wo_x=wo_x, norm_ffn_g=norm_ffn_g, w_gate_up=w_gate_up,
             w_down=w_down)
    M = dict(norm_mix_g=m_norm_mix_g, w_in=m_w_in, q_norm_g=m_q_norm_g, k_norm_g=m_k_norm_g, sinks=m_sinks, conv_w=m_conv_w,
             conv_b=m_conv_b, conv_ln_g=m_conv_ln_g, conv_ln_b=m_conv_ln_b, w_out=m_w_out, norm_x_g=m_norm_x_g,
             norm_mem_g=m_norm_mem_g, wq_x=m_wq_x, wkv_x=m_wkv_x, xq_norm_g=m_xq_norm_g, xk_norm_g=m_xk_norm_g, wo_x=m_wo_x,
             norm_ffn_g=m_norm_ffn_g, w_gate_up=m_w_gate_up, w_down=m_w_down)
    V = dict(norm_mix_g=v_norm_mix_g, w_in=v_w_in, q_norm_g=v_q_norm_g, k_norm_g=v_k_norm_g, sinks=v_sinks, conv_w=v_conv_w,
             conv_b=v_conv_b, conv_ln_g=v_conv_ln_g, conv_ln_b=v_conv_ln_b, w_out=v_w_out, norm_x_g=v_norm_x_g,
             norm_mem_g=v_norm_mem_g, wq_x=v_wq_x, wkv_x=v_wkv_x, xq_norm_g=v_xq_norm_g, xk_norm_g=v_xk_norm_g, wo_x=v_wo_x,
             norm_ffn_g=v_norm_ffn_g, w_gate_up=v_w_gate_up, w_down=v_w_down)
    order = ["norm_mix_g", "w_in", "q_norm_g", "k_norm_g", "sinks", "conv_w", "conv_b", "conv_ln_g", "conv_ln_b", "w_out",
             "norm_x_g", "norm_mem_g", "wq_x", "wkv_x", "xq_norm_g", "xk_norm_g", "wo_x", "norm_ffn_g", "w_gate_up", "w_down"]
    xi, yi, ci = _position()
    dev = 4 * xi + 2 * yi + ci
    x2d, mem2d, tgt2d = x[0], mem[0], loss_target[0]

    def travelling(name, l, transposed):
        a = P[name][l]
        return (a.T if transposed else a).astype(BF16)

    rows_of = {n: rows for n, rows, _ in BIG}
    transposed_of = {n: tr for n, _, tr in BIG}

    def whole(names, gathered):
        return {n: g.reshape(rows_of[n], D_MODEL) for n, g in zip(names, gathered)}

    cw = jnp.pad(conv_w.reshape(DEPTH * CONV_K, CONV_CH // N_DEV), ((0, 2), (0, LANES - CONV_CH // N_DEV)))
    cw_all, cw_token = all_gather(cw, name="ag_conv_w", in_vmem=True)
    first, token0 = all_gather_many([_ordered_after(travelling("w_in", 0, True), cw_token.astype(BF16))], name="ag_w_in0")
    travel_order = [(0, "mid"), (0, "ffn"), (1, "in"), (1, "mid"), (1, "ffn")]
    travel_groups = []
    for l, group in travel_order:
        shards = [_ordered_after(travelling(n, l, transposed_of[n]), token0.astype(BF16)) for n in WEIGHT_GROUPS[group]]
        lands = [lax.dynamic_update_slice(lax.empty((N_DEV,) + s.shape, BF16), s[None], (dev, 0, 0)) for s in shards]
        travel_groups.append((shards, lands))
    travel_states, travel_token = gather_start(travel_groups, name="ag_weights_start")
    travelling_state = dict(zip(travel_order, travel_states))
    forward_at = {(0, "attn"): [(0, "mid")], (0, "mid"): [(0, "ffn"), (1, "in")], (1, "attn"): [(1, "mid"), (1, "ffn")]}

    def reached(l, stage, marker):
        keys = forward_at.get((l, stage))
        if keys:
            gather_forward([travelling_state[k] for k in keys], marker,
                           name="ag_weights_forward_" + "_".join(f"{g}{ll}" for ll, g in keys))

    def weights_of(l, group, marker):
        if (l, group) == (0, "in"):
            return whole(WEIGHT_GROUPS[group], first)
        gathered = gather_finish(travelling_state[(l, group)], marker, name=f"ag_weights_finish_{group}{l}")
        return whole(WEIGHT_GROUPS[group], gathered)

    cw_full = cw_all[:, :DEPTH * CONV_K, :CONV_CH // N_DEV].reshape(N_DEV, DEPTH, CONV_K, CONV_CH // N_DEV)
    cw_full = jnp.transpose(cw_full, (1, 2, 0, 3)).reshape(DEPTH, CONV_K, CONV_CH)
    smalls = []
    for l in range(DEPTH):
        sl = {n: P[n][l] for n in SMALL}
        sl["conv_w"] = jnp.pad(cw_full[l], ((0, CONV_HALO - CONV_K), (0, 0)))
        smalls.append(sl)
    smalls[0]["norm_mix_g"] = _ordered_after(smalls[0]["norm_mix_g"], travel_token)

    c_idx = jnp.reshape(ci, (1,)).astype(jnp.int32)
    kc_idx = jnp.stack([2 * xi + yi, ci]).astype(jnp.int32)
    got, flight, reduced = {}, {}, {}

    def as_parts(gb):
        keys = sorted(gb)
        return keys, [gb[k].reshape(N_DEV, rows_of[k[1]] // N_DEV, D_MODEL) for k in keys]

    def lands_like(parts, blocks, dtype):
        return [lax.empty((blocks,) + p.shape[1:], dtype) for p in parts]

    def to_sibling(group, gb):
        keys, parts = as_parts(gb)
        flight[group] = (keys, split_start(parts, lands_like(parts, 4, F32), _sibling_plan, 4,
                                           name=f"rs_sibling_{group}_start"))
        return flight[group][1][4]

    def to_chips(group, marker):
        keys, started = flight[group]
        parts, from_sibling = split_wait(started, marker, _sibling_plan, 4, name=f"rs_sibling_{group}_wait")
        chip_sums = sum_for_chips(parts, from_sibling, c_idx, name=f"rs_sum_for_chips_{group}")
        started = split_start(chip_sums, lands_like(parts, 3, BF16), _chips_plan, 3, name=f"rs_chips_{group}_start")
        flight[group] = (keys, parts, from_sibling, started)
        return started[4]

    def finish(group, marker):
        keys, parts, from_sibling, started = flight[group]
        _, from_chips = split_wait(started, marker, _chips_plan, 3, name=f"rs_chips_{group}_wait")
        reduced.update(zip(keys, sum_final(parts, from_sibling, from_chips, kc_idx, name=f"rs_sum_final_{group}")))

    def stage_done(l, stage, gb, marker):
        gb = {(l, n): g for n, g in gb.items()}
        if l == 1:
            got.update(gb)
            return to_sibling("l1", got) if stage == "mix" else None
        if stage == "ffn":
            return to_chips("l1", marker) + to_sibling("ffn", gb)
        if stage == "mid":
            return to_chips("ffn", marker) + to_sibling("mid", gb)
        if stage == "attn":
            return to_chips("mid", marker)
        finish("l1", marker)
        finish("ffn", marker)
        finish("mid", marker)
        keys, parts = as_parts(gb)
        from_sibling = exchange(parts, _sibling_plan, 4, 4, name="rs_sibling_in")
        chip_sums = sum_for_chips(parts, from_sibling, c_idx, name="rs_sum_for_chips_in")
        from_chips = exchange(chip_sums, _chips_plan, 3, 3, name="rs_chips_in")
        reduced.update(zip(keys, sum_final(parts, from_sibling, from_chips, kc_idx, name="rs_sum_final_in")))
        return None

    loss_part, grad_x, gss = _local_step(x2d, mem2d, tgt2d, weights_of, reached, smalls, stage_done)
    loss = lax.psum(loss_part, ("x", "y", "c"))

    grads = {n: jnp.stack([reduced[(l, n)].T if transposed else reduced[(l, n)] for l in range(DEPTH)])
             for n, _, transposed in BIG}

    small_names = SMALL + ("conv_w",)
    small_shapes = [(DEPTH,) + ((CONV_K, CONV_CH) if n == "conv_w" else P[n].shape[1:]) for n in small_names]
    small_parts = _pack_rows([jnp.stack([gss[l][n].reshape(sh[1:]) for l in range(DEPTH)])
                              for n, sh in zip(small_names, small_shapes)])
    small_sum = sum_devices(all_gather(small_parts, name="ag_small_grads", in_vmem=True)[0])
    for n, g in zip(small_names, _unpack_rows(small_sum, small_shapes)):
        if n == "conv_w":
            g = lax.dynamic_slice_in_dim(g, dev * (CONV_CH // N_DEV), CONV_CH // N_DEV, axis=2)
        grads[n] = g

    delta, new_m, new_v = {}, {}, {}
    for n, _, _ in BIG:
        shape = P[n].shape
        two_d = lambda a: a.reshape(shape[0] * shape[1], shape[2])
        d_, m_, v_ = adamw(two_d(P[n]), two_d(grads[n]), two_d(M[n]), two_d(V[n]), name="adamw_" + n)
        delta[n], new_m[n], new_v[n] = d_.reshape(shape), m_.reshape(shape), v_.reshape(shape)
    shapes = [P[n].shape for n in small_names]
    d_, m_, v_ = adamw(_pack_rows([P[n] for n in small_names]), _pack_rows([grads[n] for n in small_names]),
                       _pack_rows([M[n] for n in small_names]), _pack_rows([V[n] for n in small_names]), name="adamw_small")
    for n, dd, mm_, vv in zip(small_names, _unpack_rows(d_, shapes), _unpack_rows(m_, shapes), _unpack_rows(v_, shapes)):
        delta[n], new_m[n], new_v[n] = dd, mm_, vv

    return (loss, grad_x[None], *[grads[n] for n in order], *[delta[n] for n in order],
            *[new_m[n] for n in order], *[new_v[n] for n in order])
```

```python
import functools

import jax
import jax.numpy as jnp
import numpy as np
from jax import lax
from jax.experimental import pallas as pl
from jax.experimental.pallas import tpu as pltpu

F32 = jnp.float32
BF16 = jnp.bfloat16

D_MODEL = 1024
HEAD_DIM = 64
N_Q_HEADS = 8
N_KV_HEADS = 2
GROUP = N_Q_HEADS // N_KV_HEADS
ATTN_WIDTH = N_Q_HEADS * HEAD_DIM
KV_WIDTH = N_KV_HEADS * HEAD_DIM
QKV_WIDTH = ATTN_WIDTH + 2 * KV_WIDTH
CONV_CH = 512
IN_COLS = QKV_WIDTH + 2 * CONV_CH
CONV_K = 31
CONV_HALO = 32
BLOCK = 128
N_X_HEADS = 4
X_HEAD_DIM = 256
D_FF = 2816
EPS = 1e-6
NEG = -1e30
DEPTH = 2
N_DEV = 8

ADAM_LR = 0.001
ADAM_B1 = 0.9
ADAM_B2 = 0.999
ADAM_EPS = 1e-08
ADAM_WD = 0.01
ADAM_STEP = 10

V7X_VMEM_LIMIT = 56 * 1024 * 1024
LANES = 128

MESH = pl.DeviceIdType.MESH


def _cp(**kw):
    return pltpu.CompilerParams(vmem_limit_bytes=V7X_VMEM_LIMIT, **kw)


def _dot(a, b, dims):
    return lax.dot_general(a.astype(BF16), b.astype(BF16), (dims, ((), ())), preferred_element_type=F32)


def _dot_nn(a, b):
    return _dot(a, b, ((1,), (0,)))


def _dot_nt(a, b):
    return _dot(a, b, ((1,), (1,)))


def _dot_tn(a, b):
    return _dot(a, b, ((0,), (0,)))


def _sigmoid(x):
    return jax.nn.sigmoid(x)


def _rms(x):
    r = lax.rsqrt(jnp.mean(x * x, axis=-1, keepdims=True) + EPS)
    return x * r, r


def _rms_bwd(dy, xhat, r, g):
    dxh = dy * g
    return r * (dxh - xhat * jnp.mean(dxh * xhat, axis=-1, keepdims=True))


def rms_fwd(x, g, *, tm=512):
    m, d = x.shape
    tm = min(tm, m)

    def body(x_ref, g_ref, o_ref):
        xh, _ = _rms(x_ref[...])
        o_ref[...] = (xh * g_ref[...]).astype(o_ref.dtype)

    return pl.pallas_call(
        body, name="rms_fwd", grid=(m // tm,),
        in_specs=[pl.BlockSpec((tm, d), lambda i: (i, 0)), pl.BlockSpec((1, d), lambda i: (0, 0))],
        out_specs=pl.BlockSpec((tm, d), lambda i: (i, 0)),
        out_shape=jax.ShapeDtypeStruct((m, d), BF16), compiler_params=_cp(),
    )(x, g.reshape(1, d))


def rms_bwd(dh, x, g, dres, *, tm=512):
    m, d = x.shape
    tm = min(tm, m)
    has_res = dres is not None

    def body(*refs):
        if has_res:
            dh_ref, x_ref, g_ref, r_ref, dx_ref, dg_ref = refs
        else:
            dh_ref, x_ref, g_ref, dx_ref, dg_ref = refs
        xh, r = _rms(x_ref[...])
        dy = dh_ref[...].astype(F32)

        @pl.when(pl.program_id(0) == 0)
        def _():
            dg_ref[...] = jnp.zeros_like(dg_ref)

        dg_ref[...] += jnp.sum(dy * xh, axis=0, keepdims=True)
        dx = _rms_bwd(dy, xh, r, g_ref[...])
        if has_res:
            dx = dx + r_ref[...]
        dx_ref[...] = dx

    row = pl.BlockSpec((tm, d), lambda i: (i, 0))
    vec = pl.BlockSpec((1, d), lambda i: (0, 0))
    ins = [dh, x, g.reshape(1, d)] + ([dres] if has_res else [])
    return pl.pallas_call(
        body, name="rms_bwd" + ("_res" if has_res else ""), grid=(m // tm,),
        in_specs=[row, row, vec] + ([row] if has_res else []),
        out_specs=[row, vec],
        out_shape=[jax.ShapeDtypeStruct((m, d), F32), jax.ShapeDtypeStruct((1, d), F32)],
        compiler_params=_cp(),
    )(*ins)


def loss_head(y, target, *, tm=512):
    m, d = y.shape

    def body(y_ref, t_ref, dy_ref, l_ref):
        err = y_ref[...] - t_ref[...]
        dy_ref[...] = err * (1.0 / d)

        @pl.when(pl.program_id(0) == 0)
        def _():
            l_ref[...] = jnp.zeros_like(l_ref)

        part = jnp.sum(jnp.sum(err * err, axis=-1, keepdims=True), axis=0, keepdims=True)
        l_ref[...] += jnp.broadcast_to(part * (0.5 / d), l_ref.shape)

    row = pl.BlockSpec((tm, d), lambda i: (i, 0))
    return pl.pallas_call(
        body, name="loss_head", grid=(m // tm,),
        in_specs=[row, row],
        out_specs=[row, pl.BlockSpec((1, LANES), lambda i: (0, 0))],
        out_shape=[jax.ShapeDtypeStruct((m, d), F32), jax.ShapeDtypeStruct((1, LANES), F32)],
        compiler_params=_cp(),
    )(y, target)


def _tile(n, cap):
    if n <= cap:
        return n
    best = None
    for t in range(LANES, cap + 1, LANES):
        if n % t == 0:
            best = t
    assert best is not None, (n, cap)
    return best


def mm(a, b, *, trans_b, out_dtype, res=None, tm=1024, tn_cap=1536, name):
    m, k = a.shape
    n = b.shape[0] if trans_b else b.shape[1]
    assert (b.shape[1] if trans_b else b.shape[0]) == k
    tm = min(tm, m)
    tn = _tile(n, tn_cap)
    has_res = res is not None

    def body(*refs):
        if has_res:
            a_ref, b_ref, r_ref, o_ref = refs
        else:
            a_ref, b_ref, o_ref = refs
        acc = _dot_nt(a_ref[...], b_ref[...]) if trans_b else _dot_nn(a_ref[...], b_ref[...])
        if has_res:
            acc = acc + r_ref[...]
        o_ref[...] = acc.astype(o_ref.dtype)

    b_spec = pl.BlockSpec((tn, k), lambda i, j: (j, 0)) if trans_b else pl.BlockSpec((k, tn), lambda i, j: (0, j))
    o_spec = pl.BlockSpec((tm, tn), lambda i, j: (i, j))
    return pl.pallas_call(
        body, name=name, grid=(m // tm, n // tn),
        in_specs=[pl.BlockSpec((tm, k), lambda i, j: (i, 0)), b_spec] + ([o_spec] if has_res else []),
        out_specs=o_spec,
        out_shape=jax.ShapeDtypeStruct((m, n), out_dtype), compiler_params=_cp(),
    )(*([a, b] + ([res] if has_res else [])))


def mm_tn(a, b, *, name, ta_cap=1536, tb_cap=1024, tk=2048):
    m, ka = a.shape
    nb = b.shape[1]
    assert b.shape[0] == m
    tk = min(tk, m)
    ta = _tile(ka, ta_cap)
    tb = _tile(nb, tb_cap)

    def body(a_ref, b_ref, o_ref):
        @pl.when(pl.program_id(2) == 0)
        def _():
            o_ref[...] = jnp.zeros_like(o_ref)

        o_ref[...] += _dot_tn(a_ref[...], b_ref[...])

    return pl.pallas_call(
        body, name=name, grid=(ka // ta, nb // tb, m // tk),
        in_specs=[pl.BlockSpec((tk, ta), lambda i, j, kk: (kk, i)), pl.BlockSpec((tk, tb), lambda i, j, kk: (kk, j))],
        out_specs=pl.BlockSpec((ta, tb), lambda i, j, kk: (i, j)),
        out_shape=jax.ShapeDtypeStruct((ka, nb), F32), compiler_params=_cp(),
    )(a, b)


def _whole(shape):
    return pl.BlockSpec(shape, lambda i: (0,) * len(shape), pipeline_mode=pl.Buffered(1))


def _rows(tm, n):
    return pl.BlockSpec((tm, n), lambda i: (i, 0))


def _vec(n):
    return pl.BlockSpec((1, n), lambda i: (0, 0))


def _chunks(n, cap=1408):
    size = _tile(n, cap)
    return [(s, size) for s in range(0, n, size)]


def _zero_at_first_step(*refs):
    @pl.when(pl.program_id(0) == 0)
    def _():
        for r in refs:
            r[...] = jnp.zeros_like(r)


def norm_proj(x, g, wt, *, tm=512):
    m, d = x.shape
    n = wt.shape[0]

    def body(x_ref, g_ref, wt_ref, h_ref, u_ref):
        h = (_rms(x_ref[...])[0] * g_ref[...]).astype(BF16)
        h_ref[...] = h
        for s, sz in _chunks(n):
            u_ref[:, s:s + sz] = _dot_nt(h, wt_ref[s:s + sz, :])

    return pl.pallas_call(
        body, name="norm_proj", grid=(m // tm,),
        in_specs=[_rows(tm, d), _vec(d), _whole((n, d))],
        out_specs=[_rows(tm, d), _rows(tm, n)],
        out_shape=[jax.ShapeDtypeStruct((m, d), BF16), jax.ShapeDtypeStruct((m, n), F32)],
        compiler_params=_cp(),
    )(x, g.reshape(1, d), wt)


def _xattn_heads(q_ref, kv_ref, qg_v, kg_v, d):
    out = []
    for h in range(N_X_HEADS):
        cols = slice(h * X_HEAD_DIM, (h + 1) * X_HEAD_DIM)
        qh, rq = _rms(q_ref[:, cols])
        qn = qh * qg_v
        kn = _rms(kv_ref[:, cols])[0] * kg_v
        v = kv_ref[:, d + h * X_HEAD_DIM:d + (h + 1) * X_HEAD_DIM]
        out.append((qh, rq, qn, kn, v, _xattn_probs(qn, kn)))
    return out


def mid_fwd(mixed, x0, w_out, g_x, wq, kv, xqg, xkg, wo, g_f, *, tm=512):
    m, d = x0.shape
    n_mem = kv.shape[0]

    def body(mixed_ref, x0_ref, w_out_ref, g_x_ref, wq_ref, kv_ref, xqg_ref, xkg_ref, wo_ref, g_f_ref,
             x1_ref, h1_ref, qx_ref, o_ref, x2_ref, h2_ref):
        x1 = x0_ref[...] + _dot_nn(mixed_ref[...], w_out_ref[...])
        x1_ref[...] = x1
        h1 = (_rms(x1)[0] * g_x_ref[...]).astype(BF16)
        h1_ref[...] = h1
        qx_ref[...] = _dot_nn(h1, wq_ref[...])
        for h, (_, _, _, _, v, p) in enumerate(_xattn_heads(qx_ref, kv_ref, xqg_ref[...], xkg_ref[...], d)):
            o_ref[:, h * X_HEAD_DIM:(h + 1) * X_HEAD_DIM] = _dot_nn(p, v).astype(o_ref.dtype)
        x2 = x1 + _dot_nn(o_ref[...], wo_ref[...])
        x2_ref[...] = x2
        h2_ref[...] = (_rms(x2)[0] * g_f_ref[...]).astype(BF16)

    sq = _whole((d, d))
    f32_rows, bf_rows = jax.ShapeDtypeStruct((m, d), F32), jax.ShapeDtypeStruct((m, d), BF16)
    return pl.pallas_call(
        body, name="mid_fwd", grid=(m // tm,),
        in_specs=[_rows(tm, d), _rows(tm, d), sq, _vec(d), sq, _whole((n_mem, 2 * d)), _vec(X_HEAD_DIM), _vec(X_HEAD_DIM),
                  sq, _vec(d)],
        out_specs=[_rows(tm, d)] * 6,
        out_shape=[f32_rows, bf_rows, f32_rows, bf_rows, f32_rows, bf_rows],
        compiler_params=_cp(),
    )(mixed, x0, w_out, g_x.reshape(1, d), wq, kv, xqg.reshape(1, X_HEAD_DIM), xkg.reshape(1, X_HEAD_DIM), wo,
      g_f.reshape(1, d))


def ffn_fwd(h2, x2, wt_gu, w_down, *, tm=256):
    m, d = x2.shape
    f = w_down.shape[0]

    def body(h2_ref, x2_ref, wt_gu_ref, w_down_ref, gu_ref, a_ref, x3_ref):
        h = h2_ref[...]
        for s, sz in _chunks(2 * f):
            gu_ref[:, s:s + sz] = _dot_nt(h, wt_gu_ref[s:s + sz, :])
        for s, sz in _chunks(f):
            g = gu_ref[:, s:s + sz]
            a_ref[:, s:s + sz] = (g * _sigmoid(g) * gu_ref[:, f + s:f + s + sz]).astype(a_ref.dtype)
        x3_ref[...] = x2_ref[...] + _dot_nn(a_ref[...], w_down_ref[...])

    return pl.pallas_call(
        body, name="ffn_fwd", grid=(m // tm,),
        in_specs=[_rows(tm, d), _rows(tm, d), _whole((2 * f, d)), _whole((f, d))],
        out_specs=[_rows(tm, 2 * f), _rows(tm, f), _rows(tm, d)],
        out_shape=[jax.ShapeDtypeStruct((m, 2 * f), F32), jax.ShapeDtypeStruct((m, f), BF16),
                   jax.ShapeDtypeStruct((m, d), F32)],
        compiler_params=_cp(),
    )(h2, x2, wt_gu, w_down)


def ffn_bwd(dx3, gu, x2, g_f, w_down, wt_gu, *, tm=256):
    m, d = x2.shape
    f = w_down.shape[0]

    def body(dx3_ref, gu_ref, x2_ref, g_ref, w_down_ref, wt_gu_ref, dgu_ref, dx2_ref, dg_ref):
        _zero_at_first_step(dg_ref)
        dx3 = dx3_ref[...]
        dx3_b = dx3.astype(BF16)
        for s, sz in _chunks(f):
            da = _dot_nt(dx3_b, w_down_ref[s:s + sz, :])
            g = gu_ref[:, s:s + sz]
            u = gu_ref[:, f + s:f + s + sz]
            sg = _sigmoid(g)
            dgu_ref[:, s:s + sz] = (da * u * (sg * (1.0 + g * (1.0 - sg)))).astype(dgu_ref.dtype)
            dgu_ref[:, f + s:f + s + sz] = (da * (g * sg)).astype(dgu_ref.dtype)
        dh2 = _dot_nn(dgu_ref[...], wt_gu_ref[...])
        xh, r = _rms(x2_ref[...])
        dg_ref[...] += jnp.sum(dh2 * xh, axis=0, keepdims=True)
        dx2_ref[...] = dx3 + _rms_bwd(dh2, xh, r, g_ref[...])

    return pl.pallas_call(
        body, name="ffn_bwd", grid=(m // tm,),
        in_specs=[_rows(tm, d), _rows(tm, 2 * f), _rows(tm, d), _vec(d), _whole((f, d)), _whole((2 * f, d))],
        out_specs=[_rows(tm, 2 * f), _rows(tm, d), _vec(d)],
        out_shape=[jax.ShapeDtypeStruct((m, 2 * f), BF16), jax.ShapeDtypeStruct((m, d), F32),
                   jax.ShapeDtypeStruct((1, d), F32)],
        compiler_params=_cp(),
    )(dx3, gu, x2, g_f.reshape(1, d), w_down, wt_gu)


def mid_bwd(dx2, qx, kv, xqg, xkg, x1, g_x, wo, wq, w_out, *, tm=512):
    m, d = x1.shape
    n_mem = kv.shape[0]
    nt = m // tm

    def body(dx2_ref, qx_ref, kv_ref, xqg_ref, xkg_ref, x1_ref, g_x_ref, wo_ref, wq_ref, w_out_ref,
             dq_ref, dx1_ref, dmixed_ref, dkv_ref, dqg_ref, dkg_ref, dg_ref):
        i = pl.program_id(0)
        _zero_at_first_step(dkv_ref, dqg_ref, dkg_ref, dg_ref)
        qg_v, kg_v = xqg_ref[...], xkg_ref[...]
        dx2 = dx2_ref[...]
        do = _dot_nt(dx2, wo_ref[...])
        dqg_acc = jnp.zeros((1, X_HEAD_DIM), F32)
        for h, (qh, rq, qn, kn, v, p) in enumerate(_xattn_heads(qx_ref, kv_ref, qg_v, kg_v, d)):
            cols = slice(h * X_HEAD_DIM, (h + 1) * X_HEAD_DIM)
            vcols = slice(d + h * X_HEAD_DIM, d + (h + 1) * X_HEAD_DIM)
            do_h = do[:, cols]
            dp = _dot_nt(do_h, v)
            ds = p * (dp - jnp.sum(p * dp, axis=-1, keepdims=True))
            dkv_ref[:, vcols] += _dot_tn(p, do_h)
            dqn = _dot_nn(ds, kn) * (X_HEAD_DIM ** -0.5)
            dkv_ref[:, cols] += _dot_tn(ds, qn) * (X_HEAD_DIM ** -0.5)
            dqg_acc = dqg_acc + jnp.sum(dqn * qh, axis=0, keepdims=True)
            dq_ref[:, cols] = _rms_bwd(dqn, qh, rq, qg_v).astype(dq_ref.dtype)
        dqg_ref[...] += dqg_acc
        dh1 = _dot_nt(dq_ref[...], wq_ref[...])
        xh, r = _rms(x1_ref[...])
        dg_ref[...] += jnp.sum(dh1 * xh, axis=0, keepdims=True)
        dx1 = dx2 + _rms_bwd(dh1, xh, r, g_x_ref[...])
        dx1_ref[...] = dx1
        dmixed_ref[...] = _dot_nt(dx1, w_out_ref[...])

        @pl.when(i == nt - 1)
        def _():
            dkg_acc = jnp.zeros((1, X_HEAD_DIM), F32)
            for h in range(N_X_HEADS):
                cols = slice(h * X_HEAD_DIM, (h + 1) * X_HEAD_DIM)
                kh, rk = _rms(kv_ref[:, cols])
                dkn = dkv_ref[:, cols]
                dkg_acc = dkg_acc + jnp.sum(dkn * kh, axis=0, keepdims=True)
                dkv_ref[:, cols] = _rms_bwd(dkn, kh, rk, kg_v)
            dkg_ref[...] = dkg_acc

    sq = _whole((d, d))
    full = pl.BlockSpec((n_mem, 2 * d), lambda i: (0, 0))
    return pl.pallas_call(
        body, name="mid_bwd", grid=(nt,),
        in_specs=[_rows(tm, d), _rows(tm, d), _whole((n_mem, 2 * d)), _vec(X_HEAD_DIM), _vec(X_HEAD_DIM), _rows(tm, d),
                  _vec(d), sq, sq, sq],
        out_specs=[_rows(tm, d), _rows(tm, d), _rows(tm, d), full, _vec(X_HEAD_DIM), _vec(X_HEAD_DIM), _vec(d)],
        out_shape=[jax.ShapeDtypeStruct((m, d), BF16), jax.ShapeDtypeStruct((m, d), F32), jax.ShapeDtypeStruct((m, d), F32),
                   jax.ShapeDtypeStruct((n_mem, 2 * d), F32), jax.ShapeDtypeStruct((1, X_HEAD_DIM), F32),
                   jax.ShapeDtypeStruct((1, X_HEAD_DIM), F32), jax.ShapeDtypeStruct((1, d), F32)],
        compiler_params=_cp(),
    )(dx2, qx, kv, xqg.reshape(1, X_HEAD_DIM), xkg.reshape(1, X_HEAD_DIM), x1, g_x.reshape(1, d), wo, wq, w_out)


def in_bwd(du, wt_in, x0, g_mix, dx1, *, tm=512):
    m, d = x0.shape
    n = wt_in.shape[0]

    def body(du_ref, wt_ref, x0_ref, g_ref, dx1_ref, dx0_ref, dg_ref):
        _zero_at_first_step(dg_ref)
        dh0 = _dot_nn(du_ref[...], wt_ref[...])
        xh, r = _rms(x0_ref[...])
        dg_ref[...] += jnp.sum(dh0 * xh, axis=0, keepdims=True)
        dx0_ref[...] = dx1_ref[...] + _rms_bwd(dh0, xh, r, g_ref[...])

    return pl.pallas_call(
        body, name="in_bwd", grid=(m // tm,),
        in_specs=[_rows(tm, n), _whole((n, d)), _rows(tm, d), _vec(d), _rows(tm, d)],
        out_specs=[_rows(tm, d), _vec(d)],
        out_shape=[jax.ShapeDtypeStruct((m, d), F32), jax.ShapeDtypeStruct((1, d), F32)],
        compiler_params=_cp(),
    )(du, wt_in, x0, g_mix.reshape(1, d), dx1)


SWA_TILE = 512
SWA_SUB = SWA_TILE // BLOCK
SWA_KEYS = SWA_TILE + BLOCK
PAIR = 2 * HEAD_DIM
KCOL = ATTN_WIDTH
VCOL = ATTN_WIDTH + KV_WIDTH


def _swa_constants():
    r = np.arange(2 * BLOCK)[:, None]
    j = np.arange(4 * BLOCK)[None, :]
    dist = (r % BLOCK) + BLOCK - (j % (2 * BLOCK))
    valid = (dist >= 0) & (dist < BLOCK)
    first_valid = valid & ((j % (2 * BLOCK)) >= BLOCK)
    bias, bias_first = [], []
    for kv in range(N_KV_HEADS):
        head = kv * GROUP + 2 * (r // BLOCK) + j // (2 * BLOCK)
        b = -(2.0 ** -(head + 1.0)) * dist
        bias.append(np.where(valid, b, NEG))
        bias_first.append(np.where(first_valid, b, NEG))
    lane = np.arange(LANES)
    seg = (lane[:, None] // HEAD_DIM == lane[None, :] // HEAD_DIM) / HEAD_DIM
    row = np.arange(4 * BLOCK)[:, None]
    ones = (row // (2 * BLOCK)) == (lane[None, :] // HEAD_DIM)
    return (jnp.asarray(np.stack(bias), F32), jnp.asarray(np.stack(bias_first), F32), jnp.asarray(seg, BF16),
            jnp.asarray(ones, BF16))


def _segmean(x, seg_ref):
    hi = x.astype(BF16)
    lo = (x - hi.astype(F32)).astype(BF16)
    return _dot_nn(hi, seg_ref[...]) + _dot_nn(lo, seg_ref[...])


def _two_heads(x, kv):
    lane = lax.broadcasted_iota(jnp.int32, (1, LANES), 1)
    mine = (lane < HEAD_DIM) if kv == 0 else (lane >= HEAD_DIM)
    base = jnp.where(mine, x, 0.0)
    other = pltpu.roll(base, HEAD_DIM, 1)
    return jnp.concatenate([base, other] if kv == 0 else [other, base], axis=0)


def _from_two_heads(y, kv):
    rows = y.shape[0] // 2
    lane = lax.broadcasted_iota(jnp.int32, (1, LANES), 1)
    top, bot = y[:rows], y[rows:]
    if kv == 0:
        return jnp.where(lane < HEAD_DIM, top + pltpu.roll(bot, HEAD_DIM, 1), 0.0)
    return jnp.where(lane >= HEAD_DIM, pltpu.roll(top, HEAD_DIM, 1) + bot, 0.0)


def _pair_rows(ref, rows, kv):
    c = kv * 2 * PAIR
    return jnp.concatenate([ref[rows, c:c + PAIR], ref[rows, c + PAIR:c + 2 * PAIR]], axis=0)


def _head_cols(fn, kv):
    return [jnp.concatenate([fn(kv * GROUP + half), fn(kv * GROUP + 2 + half)], axis=0) for half in range(2)]


def _swa_prologue(cur_ref, prev_ref, qg_ref, kg_ref, seg_ref, qg_s, kn_s, v_s):
    qg_s[...] = (cur_ref[:, 0:ATTN_WIDTH] * qg_ref[...]).astype(BF16)
    k = jnp.concatenate([prev_ref[:, KCOL:KCOL + KV_WIDTH], cur_ref[:, KCOL:KCOL + KV_WIDTH]], axis=0)
    kn_s[...] = k * lax.rsqrt(_segmean(k * k, seg_ref) + EPS) * kg_ref[...]
    v_s[0:BLOCK, :] = prev_ref[:, VCOL:VCOL + KV_WIDTH]
    v_s[BLOCK:SWA_KEYS, :] = cur_ref[:, VCOL:VCOL + KV_WIDTH]


def _swa_scores(cur_ref, sinks_ref, qg_s, kn_s, bias, rows, keys, kv):
    q2 = _pair_rows(qg_s, rows, kv)
    k2 = _two_heads(kn_s[keys, :], kv)
    t = _dot_nt(q2, k2)

    def rq(h):
        x = cur_ref[rows, h * HEAD_DIM:(h + 1) * HEAD_DIM]
        return lax.rsqrt(jnp.mean(x * x, axis=-1, keepdims=True) + EPS)

    scale = _head_cols(lambda h: rq(h) * (HEAD_DIM ** -0.5), kv)
    sink = _head_cols(lambda h: jnp.full((BLOCK, 1), sinks_ref[h], F32), kv)
    halves = []
    for half in range(2):
        cols = slice(half * 2 * BLOCK, (half + 1) * 2 * BLOCK)
        s = t[:, cols] * scale[half] + bias[:, cols]
        mx = jnp.maximum(jnp.max(s, axis=-1, keepdims=True), sink[half])
        halves.append((scale[half], jnp.exp(s - mx), jnp.exp(sink[half] - mx)))
    return q2, k2, t, halves


def swa_fwd(u, qg, kg, sinks):
    t_rows = u.shape[0]
    nt = t_rows // SWA_TILE
    bias_c, bias_first_c, seg_c, ones_c = _swa_constants()

    def body(sinks_ref, cur_ref, prev_ref, qg_ref, kg_ref, seg_ref, bias_ref, biasf_ref, ones_ref, o_ref, qg_s, kn_s, v_s):
        i = pl.program_id(0)
        _swa_prologue(cur_ref, prev_ref, qg_ref, kg_ref, seg_ref, qg_s, kn_s, v_s)
        lane = lax.broadcasted_iota(jnp.int32, (1, LANES), 1)
        for b in range(SWA_SUB):
            rows = slice(b * BLOCK, (b + 1) * BLOCK)
            keys = slice(b * BLOCK, (b + 2) * BLOCK)
            for kv in range(N_KV_HEADS):
                bias = jnp.where(i == 0, biasf_ref[kv], bias_ref[kv]) if b == 0 else bias_ref[kv]
                _, _, _, halves = _swa_scores(cur_ref, sinks_ref, qg_s, kn_s, bias, rows, keys, kv)
                e = jnp.concatenate([halves[0][1], halves[1][1]], axis=1)
                v2 = jnp.concatenate([_two_heads(v_s[keys, :], kv).astype(BF16), ones_ref[...]], axis=1)
                ox = _dot_nn(e, v2)
                den = ox[:, LANES:] + jnp.where(lane < HEAD_DIM, halves[0][2], halves[1][2])
                out = (ox[:, :LANES] / den).astype(o_ref.dtype)
                c = kv * 2 * PAIR
                o_ref[rows, c:c + PAIR] = out[:BLOCK]
                o_ref[rows, c + PAIR:c + 2 * PAIR] = out[BLOCK:]

    const3 = pl.BlockSpec((N_KV_HEADS, 2 * BLOCK, 4 * BLOCK), lambda i: (0, 0, 0))
    return pl.pallas_call(
        body, name="swa_fwd", grid=(nt,),
        in_specs=[
            pl.BlockSpec(memory_space=pltpu.SMEM),
            pl.BlockSpec((SWA_TILE, QKV_WIDTH), lambda i: (i, 0)),
            pl.BlockSpec((BLOCK, QKV_WIDTH), lambda i: (jnp.maximum(i * SWA_SUB - 1, 0), 0)),
            pl.BlockSpec((1, ATTN_WIDTH), lambda i: (0, 0)), pl.BlockSpec((1, KV_WIDTH), lambda i: (0, 0)),
            pl.BlockSpec((LANES, LANES), lambda i: (0, 0)), const3, const3,
            pl.BlockSpec((4 * BLOCK, LANES), lambda i: (0, 0)),
        ],
        out_specs=pl.BlockSpec((SWA_TILE, ATTN_WIDTH), lambda i: (i, 0)),
        out_shape=jax.ShapeDtypeStruct((t_rows, 2 * ATTN_WIDTH), BF16),
        scratch_shapes=[pltpu.VMEM((SWA_TILE, ATTN_WIDTH), BF16), pltpu.VMEM((SWA_KEYS, KV_WIDTH), F32),
                        pltpu.VMEM((SWA_KEYS, KV_WIDTH), F32)],
        compiler_params=_cp(),
    )(sinks, u, u, jnp.tile(qg, N_Q_HEADS).reshape(1, ATTN_WIDTH), jnp.tile(kg, N_KV_HEADS).reshape(1, KV_WIDTH),
      seg_c, bias_c, bias_first_c, ones_c)


def swa_bwd(u, dmixed, qg, kg, sinks):
    t_rows = u.shape[0]
    nt = t_rows // SWA_TILE
    bias_c, bias_first_c, seg_c, _ = _swa_constants()

    def body(sinks_ref, cur_ref, prev_ref, do_ref, qg_ref, kg_ref, seg_ref, bias_ref, biasf_ref,
             du_ref, dqg_ref, dkg_ref, dsk_ref, qg_s, kn_s, v_s, acck_s, accv_s, carryk_s, carryv_s):
        step = pl.program_id(0)
        i = nt - 1 - step

        @pl.when(step == 0)
        def _():
            for r in (carryk_s, carryv_s, dqg_ref, dkg_ref, dsk_ref):
                r[...] = jnp.zeros_like(r)

        _swa_prologue(cur_ref, prev_ref, qg_ref, kg_ref, seg_ref, qg_s, kn_s, v_s)
        for acc, carry in ((acck_s, carryk_s), (accv_s, carryv_s)):
            acc[0:SWA_TILE, :] = jnp.zeros((SWA_TILE, KV_WIDTH), F32)
            acc[SWA_TILE:SWA_KEYS, :] = carry[...]

        lane = lax.broadcasted_iota(jnp.int32, (1, LANES), 1)
        g_pair = qg_ref[:, 0:PAIR]
        dqg_acc = jnp.zeros((1, PAIR), F32)
        dsk_acc = jnp.zeros((1, LANES), F32)
        for b in range(SWA_SUB):
            rows = slice(b * BLOCK, (b + 1) * BLOCK)
            keys = slice(b * BLOCK, (b + 2) * BLOCK)
            for kv in range(N_KV_HEADS):
                bias = jnp.where(i == 0, biasf_ref[kv], bias_ref[kv]) if b == 0 else bias_ref[kv]
                q2, k2, t, halves = _swa_scores(cur_ref, sinks_ref, qg_s, kn_s, bias, rows, keys, kv)
                v2 = _two_heads(v_s[keys, :], kv)
                do2 = _pair_rows(do_ref, rows, kv)
                dp = _dot_nt(do2, v2)
                p_parts, dt_parts, coef = [], [], []
                for half, (scale, e, es) in enumerate(halves):
                    cols = slice(half * 2 * BLOCK, (half + 1) * 2 * BLOCK)
                    rden = 1.0 / (jnp.sum(e, axis=-1, keepdims=True) + es)
                    p = e * rden
                    dp_h = dp[:, cols]
                    delta = jnp.sum(p * dp_h, axis=-1, keepdims=True)
                    ds = p * (dp_h - delta)
                    dsink = -(es * rden) * delta
                    for pair in range(2):
                        part = jnp.sum(dsink[pair * BLOCK:(pair + 1) * BLOCK], axis=0, keepdims=True)
                        dsk_acc = dsk_acc + jnp.where(lane == kv * GROUP + 2 * pair + half, part, 0.0)
                    dscale = jnp.sum(ds * t[:, cols], axis=-1, keepdims=True)
                    coef.append(-dscale * scale * scale * scale)
                    p_parts.append(p)
                    dt_parts.append(ds * scale)
                p2 = jnp.concatenate(p_parts, axis=1)
                dt = jnp.concatenate(dt_parts, axis=1)
                dqg2 = _dot_nn(dt, k2)
                q_raw = _pair_rows(cur_ref, rows, kv)
                dq = dqg2 * g_pair + jnp.where(lane < HEAD_DIM, coef[0], coef[1]) * q_raw
                dqg_acc = dqg_acc + jnp.sum(dqg2 * q_raw, axis=0, keepdims=True)
                c = kv * 2 * PAIR
                du_ref[rows, c:c + PAIR] = dq[:BLOCK].astype(du_ref.dtype)
                du_ref[rows, c + PAIR:c + 2 * PAIR] = dq[BLOCK:].astype(du_ref.dtype)
                acck_s[keys, :] += _from_two_heads(_dot_tn(dt, q2), kv)
                accv_s[keys, :] += _from_two_heads(_dot_tn(p2, do2), kv)
        dqg_ref[...] += dqg_acc + pltpu.roll(dqg_acc, HEAD_DIM, 1)
        dsk_ref[...] += dsk_acc

        own = slice(BLOCK, SWA_KEYS)
        k = cur_ref[:, KCOL:KCOL + KV_WIDTH]
        rk = lax.rsqrt(_segmean(k * k, seg_ref) + EPS)
        kh = k * rk
        dkn = acck_s[own, :]
        dkh = dkn * kg_ref[...]
        du_ref[:, KCOL:KCOL + KV_WIDTH] = (rk * (dkh - kh * _segmean(dkh * kh, seg_ref))).astype(du_ref.dtype)
        du_ref[:, VCOL:VCOL + KV_WIDTH] = accv_s[own, :].astype(du_ref.dtype)
        dkg_part = jnp.sum(dkn * kh, axis=0, keepdims=True)
        dkg_ref[...] += dkg_part + pltpu.roll(dkg_part, HEAD_DIM, 1)
        carryk_s[...] = acck_s[0:BLOCK, :]
        carryv_s[...] = accv_s[0:BLOCK, :]

    const3 = pl.BlockSpec((N_KV_HEADS, 2 * BLOCK, 4 * BLOCK), lambda s: (0, 0, 0))
    vec = pl.BlockSpec((1, LANES), lambda s: (0, 0))
    return pl.pallas_call(
        body, name="swa_bwd", grid=(nt,),
        in_specs=[
            pl.BlockSpec(memory_space=pltpu.SMEM),
            pl.BlockSpec((SWA_TILE, QKV_WIDTH), lambda s: (nt - 1 - s, 0)),
            pl.BlockSpec((BLOCK, QKV_WIDTH), lambda s: (jnp.maximum((nt - 1 - s) * SWA_SUB - 1, 0), 0)),
            pl.BlockSpec((SWA_TILE, ATTN_WIDTH), lambda s: (nt - 1 - s, 0)),
            pl.BlockSpec((1, ATTN_WIDTH), lambda s: (0, 0)), vec,
            pl.BlockSpec((LANES, LANES), lambda s: (0, 0)), const3, const3,
        ],
        out_specs=[pl.BlockSpec((SWA_TILE, QKV_WIDTH), lambda s: (nt - 1 - s, 0)), vec, vec, vec],
        out_shape=[jax.ShapeDtypeStruct((t_rows, IN_COLS), BF16)] + [jax.ShapeDtypeStruct((1, LANES), F32)] * 3,
        scratch_shapes=[pltpu.VMEM((SWA_TILE, ATTN_WIDTH), BF16)] + [pltpu.VMEM((SWA_KEYS, KV_WIDTH), F32)] * 4
        + [pltpu.VMEM((BLOCK, KV_WIDTH), F32)] * 2,
        compiler_params=_cp(),
    )(sinks, u, u, dmixed, jnp.tile(qg, N_Q_HEADS).reshape(1, ATTN_WIDTH), jnp.tile(kg, N_KV_HEADS).reshape(1, KV_WIDTH),
      seg_c, bias_c, bias_first_c)


CONV_TILE = 512
CONV_CHUNK = 64
VAL0 = QKV_WIDTH
GATE0 = QKV_WIDTH + CONV_CH


def _glu(ref):
    return ref[:, VAL0:GATE0] * _sigmoid(ref[:, GATE0:GATE0 + CONV_CH])


SUBLANES = 8
CONV_BUF = CONV_HALO + CONV_TILE + SUBLANES
CONV_EXT = CONV_HALO + CONV_TILE


def _fill_shifted(sh_ref):
    for r in range(1, SUBLANES):
        sh_ref[r, 0:CONV_EXT, :] = sh_ref[0, pl.ds(r, CONV_EXT), :]


def _shifted(sh_ref, start, offset, n):
    return sh_ref[offset % SUBLANES, pl.ds(start + offset - offset % SUBLANES, n), :]


def _layernorm_stats(y):
    mu = jnp.mean(y, axis=-1, keepdims=True)
    yc = y - mu
    rstd = lax.rsqrt(jnp.mean(yc * yc, axis=-1, keepdims=True) + EPS)
    return yc * rstd, rstd


def conv_fwd(u, mixed, conv_w, conv_b, ln_g, ln_b):
    t = u.shape[0]
    nt = t // CONV_TILE
    per = CONV_TILE // CONV_HALO

    def body(cur_ref, prev_ref, mixed_ref, w_ref, b_ref, g_ref, b2_ref, o_ref, y_ref, gl_ref):
        del mixed_ref
        i = pl.program_id(0)
        gl_ref[0, 0:CONV_HALO, :] = jnp.where(i > 0, _glu(prev_ref), 0.0)
        gl_ref[0, CONV_HALO:CONV_EXT, :] = _glu(cur_ref)
        gl_ref[0, CONV_EXT:CONV_BUF, :] = jnp.zeros((SUBLANES, CONV_CH), F32)
        _fill_shifted(gl_ref)
        for c0 in range(0, CONV_TILE, CONV_CHUNK):
            acc = jnp.broadcast_to(b_ref[...], (CONV_CHUNK, CONV_CH))
            for k in range(CONV_K):
                acc = acc + w_ref[k:k + 1, :] * _shifted(gl_ref, c0, 2 + k, CONV_CHUNK)
            y_ref[c0:c0 + CONV_CHUNK, :] = acc
        yh, _ = _layernorm_stats(y_ref[...])
        yln = yh * g_ref[...] + b2_ref[...]
        o_ref[...] = (yln * _sigmoid(yln)).astype(o_ref.dtype)

    vec = pl.BlockSpec((1, CONV_CH), lambda i: (0, 0))
    return pl.pallas_call(
        body, name="conv_fwd", grid=(nt,),
        in_specs=[
            pl.BlockSpec((CONV_TILE, IN_COLS), lambda i: (i, 0)),
            pl.BlockSpec((CONV_HALO, IN_COLS), lambda i: (jnp.maximum(i * per - 1, 0), 0)),
            pl.BlockSpec(memory_space=pl.ANY),
            pl.BlockSpec((CONV_HALO, CONV_CH), lambda i: (0, 0)),
            vec, vec, vec,
        ],
        out_specs=[pl.BlockSpec((CONV_TILE, CONV_CH), lambda i: (i, 1)), pl.BlockSpec((CONV_TILE, CONV_CH), lambda i: (i, 0))],
        out_shape=[jax.ShapeDtypeStruct(mixed.shape, mixed.dtype), jax.ShapeDtypeStruct((t, CONV_CH), F32)],
        scratch_shapes=[pltpu.VMEM((SUBLANES, CONV_BUF, CONV_CH), F32)],
        input_output_aliases={2: 0}, compiler_params=_cp(),
    )(u, u, mixed, conv_w, conv_b.reshape(1, CONV_CH), ln_g.reshape(1, CONV_CH), ln_b.reshape(1, CONV_CH))


def conv_bwd(u, y, dmixed, du, conv_w, ln_g, ln_b):
    t = u.shape[0]
    nt = t // CONV_TILE
    per = CONV_TILE // CONV_HALO

    def body(cur_ref, prev_ref, y_ref, yn_ref, do_ref, don_ref, du_in_ref, w_ref, g_ref, b2_ref,
             du_ref, dw_ref, dvec_ref, gl_ref, dy_ref):
        i = pl.program_id(0)
        last = i == nt - 1
        _zero_at_first_step(dw_ref, dvec_ref)

        gl_ref[0, 0:CONV_HALO, :] = jnp.where(i > 0, _glu(prev_ref), 0.0)
        gl_ref[0, CONV_HALO:CONV_EXT, :] = _glu(cur_ref)
        gl_ref[0, CONV_EXT:CONV_BUF, :] = jnp.zeros((SUBLANES, CONV_CH), F32)
        _fill_shifted(gl_ref)

        yh, rstd = _layernorm_stats(jnp.concatenate([y_ref[...], yn_ref[...]], axis=0))
        g = g_ref[...]
        yln = yh * g + b2_ref[...]
        sg = _sigmoid(yln)
        dout = jnp.concatenate([do_ref[...], jnp.where(last, 0.0, don_ref[...])], axis=0)
        dyln = dout * (sg * (1.0 + yln * (1.0 - sg)))
        dyh = dyln * g
        dy = rstd * (dyh - jnp.mean(dyh, axis=-1, keepdims=True) - yh * jnp.mean(dyh * yh, axis=-1, keepdims=True))
        dy_ref[0, 0:CONV_EXT, :] = dy
        dy_ref[0, CONV_EXT:CONV_BUF, :] = jnp.zeros((SUBLANES, CONV_CH), F32)
        _fill_shifted(dy_ref)

        own = slice(0, CONV_TILE)
        dvec_ref[0:1, :] += jnp.sum(dy[own], axis=0, keepdims=True)
        dvec_ref[1:2, :] += jnp.sum(dyln[own] * yh[own], axis=0, keepdims=True)
        dvec_ref[2:3, :] += jnp.sum(dyln[own], axis=0, keepdims=True)
        for k in range(CONV_K):
            dw_ref[k:k + 1, :] += jnp.sum(dy[own] * _shifted(gl_ref, 0, 2 + k, CONV_TILE), axis=0, keepdims=True)

        for c0 in range(0, CONV_TILE, CONV_CHUNK):
            acc = jnp.zeros((CONV_CHUNK, CONV_CH), F32)
            for k in range(CONV_K):
                acc = acc + w_ref[k:k + 1, :] * _shifted(dy_ref, c0, CONV_K - 1 - k, CONV_CHUNK)
            rows = slice(c0, c0 + CONV_CHUNK)
            val = cur_ref[rows, VAL0:GATE0]
            sgate = _sigmoid(cur_ref[rows, GATE0:GATE0 + CONV_CH])
            du_ref[rows, VAL0:GATE0] = (acc * sgate).astype(du_ref.dtype)
            du_ref[rows, GATE0:GATE0 + CONV_CH] = (acc * val * sgate * (1.0 - sgate)).astype(du_ref.dtype)
        du_ref[:, 0:QKV_WIDTH] = du_in_ref[:, 0:QKV_WIDTH]

    vec = pl.BlockSpec((1, CONV_CH), lambda i: (0, 0))
    n_halo = t // CONV_HALO
    return pl.pallas_call(
        body, name="conv_bwd", grid=(nt,),
        in_specs=[
            pl.BlockSpec((CONV_TILE, IN_COLS), lambda i: (i, 0)),
            pl.BlockSpec((CONV_HALO, IN_COLS), lambda i: (jnp.maximum(i * per - 1, 0), 0)),
            pl.BlockSpec((CONV_TILE, CONV_CH), lambda i: (i, 0)),
            pl.BlockSpec((CONV_HALO, CONV_CH), lambda i: (jnp.minimum((i + 1) * per, n_halo - 1), 0)),
            pl.BlockSpec((CONV_TILE, CONV_CH), lambda i: (i, 1)),
            pl.BlockSpec((CONV_HALO, CONV_CH), lambda i: (jnp.minimum((i + 1) * per, n_halo - 1), 1)),
            pl.BlockSpec((CONV_TILE, IN_COLS), lambda i: (i, 0)),
            pl.BlockSpec((CONV_HALO, CONV_CH), lambda i: (0, 0)),
            vec, vec,
        ],
        out_specs=[
            pl.BlockSpec((CONV_TILE, IN_COLS), lambda i: (i, 0)),
            pl.BlockSpec((CONV_HALO, CONV_CH), lambda i: (0, 0)),
            pl.BlockSpec((8, CONV_CH), lambda i: (0, 0)),
        ],
        out_shape=[
            jax.ShapeDtypeStruct(du.shape, du.dtype),
            jax.ShapeDtypeStruct((CONV_HALO, CONV_CH), F32),
            jax.ShapeDtypeStruct((8, CONV_CH), F32),
        ],
        scratch_shapes=[pltpu.VMEM((SUBLANES, CONV_BUF, CONV_CH), F32), pltpu.VMEM((SUBLANES, CONV_BUF, CONV_CH), F32)],
        input_output_aliases={6: 0}, compiler_params=_cp(),
    )(u, u, y, y, dmixed, dmixed, du, conv_w, ln_g.reshape(1, CONV_CH), ln_b.reshape(1, CONV_CH))


def _xattn_probs(qn, kn):
    s = _dot_nt(qn, kn) * (X_HEAD_DIM ** -0.5)
    e = jnp.exp(s - jnp.max(s, axis=-1, keepdims=True))
    return e / jnp.sum(e, axis=-1, keepdims=True)


def adamw(w, g, m, v, *, name):
    r, c = w.shape
    tr = r
    for cand in (512, 256, 128, 64, 32, 16, 8):
        if r % cand == 0 and r > cand:
            tr = cand
            break

    def body(w_ref, g_ref, m_ref, v_ref, d_ref, nm_ref, nv_ref):
        g_v = g_ref[...]
        m2 = ADAM_B1 * m_ref[...] + (1.0 - ADAM_B1) * g_v
        v2 = ADAM_B2 * v_ref[...] + (1.0 - ADAM_B2) * jnp.square(g_v)
        m_hat = m2 / (1.0 - ADAM_B1 ** ADAM_STEP)
        v_hat = v2 / (1.0 - ADAM_B2 ** ADAM_STEP)
        d_ref[...] = -ADAM_LR * (m_hat / (jnp.sqrt(v_hat) + ADAM_EPS) + ADAM_WD * w_ref[...])
        nm_ref[...] = m2
        nv_ref[...] = v2

    spec = pl.BlockSpec((tr, c), lambda i: (i, 0))
    shape = jax.ShapeDtypeStruct((r, c), F32)
    return pl.pallas_call(
        body, name=name, grid=(r // tr,), in_specs=[spec] * 4, out_specs=[spec] * 3,
        out_shape=[shape] * 3, compiler_params=_cp(),
    )(w, g, m, v)


def _position():
    return lax.axis_index("x"), lax.axis_index("y"), lax.axis_index("c")


def all_gather(shard, *, name, in_vmem):
    r, c_ = shard.shape

    def body(x_ref, out_ref, token_ref, send_sems, recv_sems, local_sem):
        x, y, c = _position()
        me, sibling = (x, y, c), (x, y, 1 - c)
        chips = [(1 - x, y), (x, 1 - y), (1 - x, 1 - y)]
        token_ref[...] = jnp.zeros_like(token_ref)

        def rows(px, py, pc):
            return out_ref.at[4 * px + 2 * py + pc]

        def copy(k, block, to, src=None):
            return pltpu.make_async_remote_copy(
                src_ref=rows(*block) if src is None else src, dst_ref=rows(*block),
                send_sem=send_sems.at[k], recv_sem=recv_sems.at[k], device_id=to, device_id_type=MESH)

        mine = pltpu.make_async_copy(x_ref, rows(*me), local_sem)
        mine.start()
        first = [copy(0, me, sibling, src=x_ref)]
        first += [copy(1 + j, me, (*chip, c), src=x_ref) for j, chip in enumerate(chips)]
        for cp in first:
            cp.start()
        passed = [copy(4 + j, (*chip, c), sibling) for j, chip in enumerate(chips)]
        for j, chip in enumerate(chips):
            copy(1 + j, (*chip, c), me).wait_recv()
            passed[j].start()
        copy(0, sibling, me).wait_recv()
        for j, chip in enumerate(chips):
            copy(4 + j, (*chip, 1 - c), me).wait_recv()
        for cp in first + passed:
            cp.wait_send()
        mine.wait()

    space = pltpu.VMEM if in_vmem else pltpu.HBM
    return pl.pallas_call(
        body, name=name,
        out_shape=[jax.ShapeDtypeStruct((N_DEV, r, c_), shard.dtype), jax.ShapeDtypeStruct((8, LANES), F32)],
        in_specs=[pl.BlockSpec(memory_space=space)],
        out_specs=[pl.BlockSpec(memory_space=space), pl.BlockSpec(memory_space=pltpu.VMEM)],
        scratch_shapes=[pltpu.SemaphoreType.DMA((7,)), pltpu.SemaphoreType.DMA((7,)), pltpu.SemaphoreType.DMA],
        compiler_params=_cp(),
    )(shard)


def all_gather_many(shards, *, name):
    n = len(shards)

    def body(*refs):
        x_refs, out_refs, token_ref = refs[:n], refs[n:2 * n], refs[2 * n]
        send_sems, recv_sems, local_sems = refs[2 * n + 1:]
        x, y, c = _position()
        me, sibling = (x, y, c), (x, y, 1 - c)
        chips = [(1 - x, y), (x, 1 - y), (1 - x, 1 - y)]
        token_ref[...] = jnp.zeros_like(token_ref)

        def rows(t, px, py, pc):
            return out_refs[t].at[4 * px + 2 * py + pc]

        def copy(t, k, block, to, src=None):
            return pltpu.make_async_remote_copy(
                src_ref=rows(t, *block) if src is None else src, dst_ref=rows(t, *block),
                send_sem=send_sems.at[7 * t + k], recv_sem=recv_sems.at[7 * t + k], device_id=to, device_id_type=MESH)

        mine = [pltpu.make_async_copy(x_refs[t], rows(t, *me), local_sems.at[t]) for t in range(n)]
        for cp in mine:
            cp.start()
        first = []
        for t in range(n):
            first.append(copy(t, 0, me, sibling, src=x_refs[t]))
            first += [copy(t, 1 + j, me, (*chip, c), src=x_refs[t]) for j, chip in enumerate(chips)]
        for cp in first:
            cp.start()
        passed = []
        for t in range(n):
            for j, chip in enumerate(chips):
                copy(t, 1 + j, (*chip, c), me).wait_recv()
                passed.append(copy(t, 4 + j, (*chip, c), sibling))
                passed[-1].start()
        for t in range(n):
            copy(t, 0, sibling, me).wait_recv()
            for j, chip in enumerate(chips):
                copy(t, 4 + j, (*chip, 1 - c), me).wait_recv()
        for cp in first + passed:
            cp.wait_send()
        for cp in mine:
            cp.wait()

    hbm = pl.BlockSpec(memory_space=pltpu.HBM)
    out = pl.pallas_call(
        body, name=name,
        out_shape=[jax.ShapeDtypeStruct((N_DEV,) + s.shape, s.dtype) for s in shards] + [jax.ShapeDtypeStruct((8, LANES), F32)],
        in_specs=[hbm] * n, out_specs=[hbm] * n + [pl.BlockSpec(memory_space=pltpu.VMEM)],
        scratch_shapes=[pltpu.SemaphoreType.DMA((7 * n,)), pltpu.SemaphoreType.DMA((7 * n,)), pltpu.SemaphoreType.DMA((n,))],
        compiler_params=_cp(),
    )(*shards)
    return out[:n], out[n]


def exchange(srcs, plan, n_copies, slots, *, name):
    n = len(srcs)

    def body(*refs):
        src_refs, out_refs, (send_sems, recv_sems) = refs[:n], refs[n:2 * n], refs[2 * n:]
        copies = []
        for t in range(n):
            for k in range(n_copies):
                s, d, to = plan(src_refs[t], out_refs[t], k)
                copies.append(pltpu.make_async_remote_copy(
                    src_ref=s, dst_ref=d, send_sem=send_sems.at[n_copies * t + k], recv_sem=recv_sems.at[n_copies * t + k],
                    device_id=to, device_id_type=MESH))
        for cp in copies:
            cp.start()
        for cp in copies:
            cp.wait_recv()
        for cp in copies:
            cp.wait_send()

    hbm = pl.BlockSpec(memory_space=pltpu.HBM)
    return pl.pallas_call(
        body, name=name,
        out_shape=[jax.ShapeDtypeStruct((slots,) + s.shape[1:], s.dtype) for s in srcs],
        in_specs=[hbm] * n, out_specs=[hbm] * n,
        scratch_shapes=[pltpu.SemaphoreType.DMA((n_copies * n,)), pltpu.SemaphoreType.DMA((n_copies * n,))],
        compiler_params=_cp(),
    )(*srcs)


_HBM = pl.BlockSpec(memory_space=pltpu.HBM)
_SEM = pl.BlockSpec(memory_space=pltpu.SEMAPHORE)
_EFFECT = pltpu.SideEffectType.DATAFLOW_SIDE_EFFECTING


def _split_copies(src_refs, land_refs, send_sems, recv_sems, plan, n_copies):
    copies = []
    for t, (src_ref, land_ref) in enumerate(zip(src_refs, land_refs)):
        for k in range(n_copies):
            s, d, to = plan(src_ref, land_ref, k)
            copies.append(pltpu.make_async_remote_copy(
                src_ref=s, dst_ref=d, send_sem=send_sems.at[n_copies * t + k], recv_sem=recv_sems.at[n_copies * t + k],
                device_id=to, device_id_type=MESH))
    return copies


def split_start(srcs, lands, plan, n_copies, *, name):
    n = len(srcs)

    def body(*refs):
        src_refs, land_refs, send_sems, recv_sems, token = refs[:n], refs[n:2 * n], refs[2 * n], refs[2 * n + 1], refs[-1]
        for cp in _split_copies(src_refs, land_refs, send_sems, recv_sems, plan, n_copies):
            cp.start()
        token[...] = jnp.zeros_like(token)

    both = list(srcs) + list(lands)
    out = pl.pallas_call(
        body, name=name,
        out_shape=(pltpu.SemaphoreType.DMA((n_copies * n,)), pltpu.SemaphoreType.DMA((n_copies * n,)),
                   *[pltpu.HBM(a.shape, a.dtype) for a in both], jax.ShapeDtypeStruct((8, LANES), F32)),
        in_specs=(_HBM,) * (2 * n), out_specs=(_SEM, _SEM) + (_HBM,) * (2 * n) + (pl.BlockSpec(memory_space=pltpu.VMEM),),
        input_output_aliases={i: 2 + i for i in range(2 * n)},
        compiler_params=pltpu.CompilerParams(has_side_effects=_EFFECT),
    )(*[pltpu.with_memory_space_constraint(a, pltpu.HBM) for a in both])
    return out[0], out[1], list(out[2:2 + n]), list(out[2 + n:2 + 2 * n]), out[-1]


def split_wait(started, after, plan, n_copies, *, name):
    send_sems, recv_sems, srcs, lands, _ = started
    n = len(srcs)

    def body(*refs):
        src_refs, land_refs, send_sems, recv_sems = refs[:n], refs[n:2 * n], refs[2 * n], refs[2 * n + 1]
        for cp in _split_copies(src_refs, land_refs, send_sems, recv_sems, plan, n_copies):
            cp.wait_send()
            cp.wait_recv()

    both = list(srcs) + list(lands)
    out = pl.pallas_call(
        body, name=name,
        out_shape=tuple(pltpu.HBM(a.shape, a.dtype) for a in both),
        in_specs=(_HBM,) * (2 * n) + (_SEM, _SEM, pl.BlockSpec(memory_space=pl.ANY)), out_specs=(_HBM,) * (2 * n),
        input_output_aliases={i: i for i in range(2 * n)},
        compiler_params=pltpu.CompilerParams(has_side_effects=_EFFECT),
    )(*both, send_sems, recv_sems, after)
    return list(out[:n]), list(out[n:])


def _other_chips(x, y):
    return [(1 - x, y), (x, 1 - y), (1 - x, 1 - y)]


def _remote(src, dst, send_sem, recv_sem, to):
    return pltpu.make_async_remote_copy(src_ref=src, dst_ref=dst, send_sem=send_sem, recv_sem=recv_sem,
                                        device_id=to, device_id_type=MESH)


def gather_start(groups, *, name):
    counts = [len(shards) for shards, _ in groups]
    flat = [a for shards, _ in groups for a in shards] + [a for _, lands in groups for a in lands]
    n_all, n_groups = sum(counts), len(groups)

    def body(*refs):
        s_refs, l_refs = refs[:n_all], refs[n_all:2 * n_all]
        sems = refs[2 * n_all:2 * n_all + 3 * n_groups]
        x, y, c = _position()
        me = 4 * x + 2 * y + c
        at = 0
        for gi, n in enumerate(counts):
            send, recv_sibling, recv_ici = sems[3 * gi:3 * gi + 3]
            for t in range(n):
                src, dst = s_refs[at + t], l_refs[at + t].at[me]
                _remote(src, dst, send.at[4 * t], recv_sibling.at[t], (x, y, 1 - c)).start()
                for j, chip in enumerate(_other_chips(x, y)):
                    _remote(src, dst, send.at[4 * t + 1 + j], recv_ici.at[3 * t + j], (*chip, c)).start()
            at += n
        refs[-1][...] = jnp.zeros_like(refs[-1])

    sem_shapes = [pltpu.SemaphoreType.DMA((k * n,)) for n in counts for k in (4, 1, 3)]
    out = pl.pallas_call(
        body, name=name,
        out_shape=(*sem_shapes, *[pltpu.HBM(a.shape, a.dtype) for a in flat], jax.ShapeDtypeStruct((8, LANES), F32)),
        in_specs=(_HBM,) * (2 * n_all),
        out_specs=(_SEM,) * (3 * n_groups) + (_HBM,) * (2 * n_all) + (pl.BlockSpec(memory_space=pltpu.VMEM),),
        input_output_aliases={i: 3 * n_groups + i for i in range(2 * n_all)},
        compiler_params=pltpu.CompilerParams(has_side_effects=_EFFECT),
    )(*[pltpu.with_memory_space_constraint(a, pltpu.HBM) for a in flat])
    thru = out[3 * n_groups:-1]
    states, at = [], 0
    for gi, n in enumerate(counts):
        states.append(dict(shards=list(thru[at:at + n]), lands=list(thru[n_all + at:n_all + at + n]),
                           send=out[3 * gi], recv_sibling=out[3 * gi + 1], recv_ici=out[3 * gi + 2]))
        at += n
    return states, out[-1]


def gather_forward(states, after, *, name):
    counts = [len(s["lands"]) for s in states]
    flat = [a for s in states for a in s["lands"]]
    n_all, n_groups = sum(counts), len(states)

    def body(*refs):
        l_refs = refs[:n_all]
        recv_ici = refs[n_all:n_all + n_groups]
        fwd = refs[n_all + n_groups + 1:n_all + n_groups + 1 + 2 * n_groups]
        x, y, c = _position()
        at = 0
        for gi, n in enumerate(counts):
            fwd_send, fwd_recv = fwd[2 * gi], fwd[2 * gi + 1]
            for t in range(n):
                for j, (px, py) in enumerate(_other_chips(x, y)):
                    block = l_refs[at + t].at[4 * px + 2 * py + c]
                    _remote(block, block, fwd_send.at[3 * t + j], recv_ici[gi].at[3 * t + j], (px, py, c)).wait_recv()
                    _remote(block, block, fwd_send.at[3 * t + j], fwd_recv.at[3 * t + j], (x, y, 1 - c)).start()
            at += n
        refs[-1][...] = jnp.zeros_like(refs[-1])

    sem_shapes = [pltpu.SemaphoreType.DMA((3 * n,)) for n in counts for _ in range(2)]
    out = pl.pallas_call(
        body, name=name,
        out_shape=(*sem_shapes, *[pltpu.HBM(a.shape, a.dtype) for a in flat], jax.ShapeDtypeStruct((8, LANES), F32)),
        in_specs=(_HBM,) * n_all + (_SEM,) * n_groups + (pl.BlockSpec(memory_space=pl.ANY),),
        out_specs=(_SEM,) * (2 * n_groups) + (_HBM,) * n_all + (pl.BlockSpec(memory_space=pltpu.VMEM),),
        input_output_aliases={i: 2 * n_groups + i for i in range(n_all)},
        compiler_params=pltpu.CompilerParams(has_side_effects=_EFFECT),
    )(*flat, *[s["recv_ici"] for s in states], after)
    at = 0
    for gi, (s, n) in enumerate(zip(states, counts)):
        s.update(fwd_send=out[2 * gi], fwd_recv=out[2 * gi + 1], lands=list(out[2 * n_groups + at:2 * n_groups + at + n]))
        at += n
    return out[-1]


def gather_finish(state, after, *, name):
    n = len(state["lands"])

    def body(*refs):
        s_refs, l_refs = refs[:n], refs[n:2 * n]
        send, recv_sibling, fwd_send, fwd_recv = refs[2 * n:2 * n + 4]
        x, y, c = _position()
        me = 4 * x + 2 * y + c
        for t in range(n):
            own = l_refs[t].at[me]
            _remote(s_refs[t], own, send.at[4 * t], recv_sibling.at[t], (x, y, 1 - c)).wait_send()
            _remote(s_refs[t], l_refs[t].at[4 * x + 2 * y + 1 - c], send.at[4 * t], recv_sibling.at[t], (x, y, 1 - c)).wait_recv()
            for j, (px, py) in enumerate(_other_chips(x, y)):
                _remote(s_refs[t], own, send.at[4 * t + 1 + j], recv_sibling.at[t], (px, py, c)).wait_send()
                mine, theirs = l_refs[t].at[4 * px + 2 * py + c], l_refs[t].at[4 * px + 2 * py + 1 - c]
                _remote(mine, mine, fwd_send.at[3 * t + j], fwd_recv.at[3 * t + j], (x, y, 1 - c)).wait_send()
                _remote(theirs, theirs, fwd_send.at[3 * t + j], fwd_recv.at[3 * t + j], (x, y, 1 - c)).wait_recv()

    both = state["shards"] + state["lands"]
    out = pl.pallas_call(
        body, name=name,
        out_shape=tuple(pltpu.HBM(a.shape, a.dtype) for a in both),
        in_specs=(_HBM,) * (2 * n) + (_SEM,) * 4 + (pl.BlockSpec(memory_space=pl.ANY),), out_specs=(_HBM,) * (2 * n),
        input_output_aliases={i: i for i in range(2 * n)},
        compiler_params=pltpu.CompilerParams(has_side_effects=_EFFECT),
    )(*both, state["send"], state["recv_sibling"], state["fwd_send"], state["fwd_recv"], after)
    return list(out[n:])


def _sibling_plan(src_ref, land_ref, k):
    x, y, c = _position()
    return src_ref.at[2 * k + (1 - c)], land_ref.at[k], (x, y, 1 - c)


def _chips_plan(src_ref, land_ref, j):
    x, y, c = _position()
    px, py = _other_chips(x, y)[j]
    return src_ref.at[j], land_ref.at[j], (px, py, c)


SUM_STEPS = 2


def sum_for_chips(parts, from_sibling, ck_idx, *, name):
    n = len(parts)

    def body(ck_ref, *refs):
        del ck_ref
        for t in range(n):
            refs[2 * n + t][...] = (refs[t][...] + refs[n + t][...]).astype(BF16)

    def blk(a):
        return (None, a.shape[1] // SUM_STEPS, a.shape[2])

    return pl.pallas_call(
        body, name=name,
        grid_spec=pltpu.PrefetchScalarGridSpec(
            num_scalar_prefetch=1, grid=(3, SUM_STEPS),
            in_specs=[pl.BlockSpec(blk(a), lambda j, i, ck: (2 * ck[1 + j] + ck[0], i, 0)) for a in parts]
            + [pl.BlockSpec(blk(a), lambda j, i, ck: (ck[1 + j], i, 0)) for a in from_sibling],
            out_specs=[pl.BlockSpec(blk(a), lambda j, i, ck: (j, i, 0)) for a in from_sibling]),
        out_shape=[jax.ShapeDtypeStruct((3,) + a.shape[1:], BF16) for a in from_sibling], compiler_params=_cp(),
    )(ck_idx, *parts, *from_sibling)


def sum_final(parts, from_sibling, from_chips, kc_idx, *, name):
    n = len(parts)

    def body(kc_ref, *refs):
        del kc_ref
        for t in range(n):
            p, s, a, b, d = (refs[j * n + t] for j in range(5))
            refs[5 * n + t][...] = (((p[...] + s[...]) + a[...].astype(F32)) + b[...].astype(F32)) + d[...].astype(F32)

    def blk(a):
        return (None, a.shape[1] // SUM_STEPS, a.shape[2])

    def chip_specs(j):
        return [pl.BlockSpec(blk(a), lambda i, kc: (j, i, 0)) for a in from_chips]

    return pl.pallas_call(
        body, name=name,
        grid_spec=pltpu.PrefetchScalarGridSpec(
            num_scalar_prefetch=1, grid=(SUM_STEPS,),
            in_specs=[pl.BlockSpec(blk(a), lambda i, kc: (2 * kc[0] + kc[1], i, 0)) for a in parts]
            + [pl.BlockSpec(blk(a), lambda i, kc: (kc[0], i, 0)) for a in from_sibling]
            + chip_specs(0) + chip_specs(1) + chip_specs(2),
            out_specs=[pl.BlockSpec(blk(a)[1:], lambda i, kc: (i, 0)) for a in parts]),
        out_shape=[jax.ShapeDtypeStruct(a.shape[1:], F32) for a in parts], compiler_params=_cp(),
    )(kc_idx, *parts, *from_sibling, *from_chips, *from_chips, *from_chips)


def sum_devices(gathered):
    n, r, c_ = gathered.shape

    def body(g_ref, o_ref):
        acc = g_ref[0]
        for k in range(1, n):
            acc = acc + g_ref[k]
        o_ref[...] = acc

    return pl.pallas_call(
        body, name="sum_devices", out_shape=jax.ShapeDtypeStruct((r, c_), F32), compiler_params=_cp(),
    )(gathered)


BIG = (
    ("w_in", IN_COLS, True), ("w_out", D_MODEL, False), ("wq_x", D_MODEL, False), ("wkv_x", 2 * D_MODEL, True),
    ("wo_x", D_MODEL, False), ("w_gate_up", 2 * D_FF, True), ("w_down", D_FF, False),
)
SHARD_ROWS = sum(rows // N_DEV for _, rows, _ in BIG)

SMALL = ("norm_mix_g", "q_norm_g", "k_norm_g", "sinks", "conv_b", "conv_ln_g", "conv_ln_b",
         "norm_x_g", "norm_mem_g", "xq_norm_g", "xk_norm_g", "norm_ffn_g")


def _pack_rows(vectors, width=LANES, row_multiple=8):
    flat = jnp.concatenate([v.reshape(-1) for v in vectors])
    per = width * row_multiple
    padded = -(-flat.shape[0] // per) * per
    return jnp.pad(flat, (0, padded - flat.shape[0])).reshape(-1, width)


def _unpack_rows(packed, shapes):
    flat = packed.reshape(-1)
    out, at = [], 0
    for s in shapes:
        n = 1
        for dim in s:
            n *= dim
        out.append(flat[at:at + n].reshape(s))
        at += n
    return out


WEIGHT_GROUPS = {"in": ("w_in",), "mid": ("w_out", "wq_x", "wkv_x", "wo_x"), "ffn": ("w_gate_up", "w_down")}


def _layer_fwd(x0, mem, weights_of, s, reached):
    w = dict(weights_of("in", x0))
    h0, u = norm_proj(x0, s["norm_mix_g"], w["w_in"])
    mixed = swa_fwd(u, s["q_norm_g"], s["k_norm_g"], s["sinks"])
    reached("attn", mixed)
    mixed, conv_y = conv_fwd(u, mixed, s["conv_w"], s["conv_b"], s["conv_ln_g"], s["conv_ln_b"])
    w.update(weights_of("mid", conv_y))
    memn = rms_fwd(mem, s["norm_mem_g"])
    kv = mm(memn, w["wkv_x"], trans_b=True, out_dtype=F32, name="mm_kv")
    x1, h1, qx, o, x2, h2 = mid_fwd(mixed, x0, w["w_out"], s["norm_x_g"], w["wq_x"], kv, s["xq_norm_g"], s["xk_norm_g"],
                                    w["wo_x"], s["norm_ffn_g"])
    reached("mid", x2)
    w.update(weights_of("ffn", x2))
    gu, a, x3 = ffn_fwd(h2, x2, w["w_gate_up"], w["w_down"])
    saved = dict(x0=x0, h0=h0, u=u, conv_y=conv_y, mixed=mixed, x1=x1, h1=h1, qx=qx, memn=memn, kv=kv, o=o, x2=x2, h2=h2,
                 gu=gu, a=a)
    return x3, saved, w


def _ordered_after(a, token):
    return a if token is None else a + token[0, 0]


def _layer_bwd(dx3, mem, w, s, sv, token, stage_done):
    gs = {}
    dgu, dx2, dg = ffn_bwd(dx3, sv["gu"], sv["x2"], _ordered_after(s["norm_ffn_g"], token), w["w_down"], w["w_gate_up"])
    gs["norm_ffn_g"] = dg
    gb = {"w_down": mm_tn(sv["a"], dx3, name="mm_dw_down")}
    gb["w_gate_up"] = mm_tn(dgu, sv["h2"], tk=dgu.shape[0], name="mm_dw_gate_up")
    token = stage_done("ffn", gb, gb["w_gate_up"])

    gb = {}
    dq, dx1, dmixed, dkv, dqg, dkg, dg = mid_bwd(dx2, sv["qx"], sv["kv"], s["xq_norm_g"], s["xk_norm_g"], sv["x1"],
                                                 _ordered_after(s["norm_x_g"], token), w["wo_x"], w["wq_x"], w["w_out"])
    gs["xq_norm_g"], gs["xk_norm_g"], gs["norm_x_g"] = dqg, dkg, dg
    gb["wo_x"] = mm_tn(sv["o"], dx2, name="mm_dwo")
    gb["wq_x"] = mm_tn(sv["h1"], dq, name="mm_dwq")
    dmemn = mm(dkv, w["wkv_x"], trans_b=False, out_dtype=F32, name="mm_dmemn")
    gb["wkv_x"] = mm_tn(dkv, sv["memn"], name="mm_dwkv")
    _, dg = rms_bwd(dmemn, mem, s["norm_mem_g"], None)
    gs["norm_mem_g"] = dg
    gb["w_out"] = mm_tn(sv["mixed"], dx1, name="mm_dw_out")
    token = stage_done("mid", gb, gb["w_out"])

    du, dqg, dkg, dsinks = swa_bwd(sv["u"], dmixed, _ordered_after(s["q_norm_g"], token), s["k_norm_g"], s["sinks"])
    gs["q_norm_g"], gs["k_norm_g"], gs["sinks"] = dqg[0, :HEAD_DIM], dkg[0, :HEAD_DIM], dsinks[0, :N_Q_HEADS]
    token = stage_done("attn", {}, dqg)
    du, dconv_w, dvec = conv_bwd(sv["u"], sv["conv_y"], dmixed, du, s["conv_w"], _ordered_after(s["conv_ln_g"], token),
                                 s["conv_ln_b"])
    gs["conv_w"] = dconv_w[:CONV_K]
    gs["conv_b"], gs["conv_ln_g"], gs["conv_ln_b"] = dvec[0], dvec[1], dvec[2]
    dw_in = mm_tn(du, sv["h0"], name="mm_dw_in")
    token = stage_done("in", {"w_in": dw_in}, dw_in)
    dx0, dg = in_bwd(du, w["w_in"], sv["x0"], _ordered_after(s["norm_mix_g"], token), dx1)
    gs["norm_mix_g"] = dg
    token = stage_done("mix", {}, dx0)
    return dx0, gs, token


def _local_step(x, mem, target, weights_of, reached, smalls, stage_done):
    saved, weights = [], []
    h = x
    for l in range(DEPTH):
        h, sv, w = _layer_fwd(h, mem, functools.partial(weights_of, l), smalls[l], functools.partial(reached, l))
        saved.append(sv)
        weights.append(w)
    dx, loss_part = loss_head(h, target)
    gss, token = [None] * DEPTH, None
    for l in reversed(range(DEPTH)):
        dx, gss[l], token = _layer_bwd(dx, mem, weights[l], smalls[l], saved[l], token,
                                       functools.partial(stage_done, l))
    return loss_part[0, 0], dx, gss


def kernel(x, mem, norm_mix_g, w_in, q_norm_g, k_norm_g, sinks, conv_w, conv_b, conv_ln_g, conv_ln_b, w_out, norm_x_g, norm_mem_g, wq_x, wkv_x, xq_norm_g, xk_norm_g, wo_x, norm_ffn_g, w_gate_up, w_down, loss_target, m_norm_mix_g, m_w_in, m_q_norm_g, m_k_norm_g, m_sinks, m_conv_w, m_conv_b, m_conv_ln_g, m_conv_ln_b, m_w_out, m_norm_x_g, m_norm_mem_g, m_wq_x, m_wkv_x, m_xq_norm_g, m_xk_norm_g, m_wo_x, m_norm_ffn_g, m_w_gate_up, m_w_down, v_norm_mix_g, v_w_in, v_q_norm_g, v_k_norm_g, v_sinks, v_conv_w, v_conv_b, v_conv_ln_g, v_conv_ln_b, v_w_out, v_norm_x_g, v_norm_mem_g, v_wq_x, v_wkv_x, v_xq_norm_g, v_xk_norm_g, v_wo_x, v_norm_ffn_g, v_w_gate_up, v_w_down):
    P = dict(norm_mix_g=norm_mix_g, w_in=w_in, q_norm_g=q_norm_g, k_norm_g=k_norm_g, sinks=sinks, conv_w=conv_w, conv_b=conv_b,
             conv_ln_g=conv_ln_g, conv_ln_b=conv_ln_b, w_out=w_out, norm_x_g=norm_x_g, norm_mem_g=norm_mem_g, wq_x=wq_x,
             wkv_x=wkv_x, xq_norm_g=xq_norm_g, xk_norm_g=xk_norm_g, wo_x=wo_x, norm_ffn_g=norm_ffn_g, w_gate_up=w_gate_up,
             w_down=w_down)
    M = dict(norm_mix_g=m_norm_mix_g, w_in=m_w_in, q_norm_g=m_q_norm_g, k_norm_g=m_k_norm_g, sinks=m_sinks, conv_w=m_conv_w,
             conv_b=m_conv_b, conv_ln_g=m_conv_ln_g, conv_ln_b=m_conv_ln_b, w_out=m_w_out, norm_x_g=m_norm_x_g,
             norm_mem_g=m_norm_mem_g, wq_x=m_wq_x, wkv_x=m_wkv_x, xq_norm_g=m_xq_norm_g, xk_norm_g=m_xk_norm_g, wo_x=m_wo_x,
             norm_ffn_g=m_norm_ffn_g, w_gate_up=m_w_gate_up, w_down=m_w_down)
    V = dict(norm_mix_g=v_norm_mix_g, w_in=v_w_in, q_norm_g=v_q_norm_g, k_norm_g=v_k_norm_g, sinks=v_sinks, conv_w=v_conv_w,
             conv_b=v_conv_b, conv_ln_g=v_conv_ln_g, conv_ln_b=v_conv_ln_b, w_out=v_w_out, norm_x_g=v_norm_x_g,
             norm_mem_g=v_norm_mem_g, wq_x=v_wq_x, wkv_x=v_wkv_x, xq_norm_g=v_xq_norm_g, xk_norm_g=v_xk_norm_g, wo_x=v_wo_x,
             norm_ffn_g=v_norm_ffn_g, w_gate_up=v_w_gate_up, w_down=v_w_down)
    order = ["norm_mix_g", "w_in", "q_norm_g", "k_norm_g", "sinks", "conv_w", "conv_b", "conv_ln_g", "conv_ln_b", "w_out",
             "norm_x_g", "norm_mem_g", "wq_x", "wkv_x", "xq_norm_g", "xk_norm_g", "wo_x", "norm_ffn_g", "w_gate_up", "w_down"]
    xi, yi, ci = _position()
    dev = 4 * xi + 2 * yi + ci
    x2d, mem2d, tgt2d = x[0], mem[0], loss_target[0]

    def travelling(name, l, transposed):
        a = P[name][l]
        return (a.T if transposed else a).astype(BF16)

    rows_of = {n: rows for n, rows, _ in BIG}
    transposed_of = {n: tr for n, _, tr in BIG}

    def whole(names, gathered):
        return {n: g.reshape(rows_of[n], D_MODEL) for n, g in zip(names, gathered)}

    cw = jnp.pad(conv_w.reshape(DEPTH * CONV_K, CONV_CH // N_DEV), ((0, 2), (0, LANES - CONV_CH // N_DEV)))
    cw_all, cw_token = all_gather(cw, name="ag_conv_w", in_vmem=True)
    first, token0 = all_gather_many([_ordered_after(travelling("w_in", 0, True), cw_token.astype(BF16))], name="ag_w_in0")
    travel_order = [(0, "mid"), (0, "ffn"), (1, "in"), (1, "mid"), (1, "ffn")]
    travel_groups = []
    for l, group in travel_order:
        shards = [_ordered_after(travelling(n, l, transposed_of[n]), token0.astype(BF16)) for n in WEIGHT_GROUPS[group]]
        lands = [lax.dynamic_update_slice(lax.empty((N_DEV,) + s.shape, BF16), s[None], (dev, 0, 0)) for s in shards]
        travel_groups.append((shards, lands))
    travel_states, travel_token = gather_start(travel_groups, name="ag_weights_start")
    travelling_state = dict(zip(travel_order, travel_states))
    forward_at = {(0, "attn"): [(0, "mid")], (0, "mid"): [(0, "ffn"), (1, "in")], (1, "attn"): [(1, "mid"), (1, "ffn")]}

    def reached(l, stage, marker):
        keys = forward_at.get((l, stage))
        if keys:
            gather_forward([travelling_state[k] for k in keys], marker,
                           name="ag_weights_forward_" + "_".join(f"{g}{ll}" for ll, g in keys))

    def weights_of(l, group, marker):
        if (l, group) == (0, "in"):
            return whole(WEIGHT_GROUPS[group], first)
        gathered = gather_finish(travelling_state[(l, group)], marker, name=f"ag_weights_finish_{group}{l}")
        return whole(WEIGHT_GROUPS[group], gathered)

    cw_full = cw_all[:, :DEPTH * CONV_K, :CONV_CH // N_DEV].reshape(N_DEV, DEPTH, CONV_K, CONV_CH // N_DEV)
    cw_full = jnp.transpose(cw_full, (1, 2, 0, 3)).reshape(DEPTH, CONV_K, CONV_CH)
    smalls = []
    for l in range(DEPTH):
        sl = {n: P[n][l] if n == "sinks" else P[n][l:l + 1] for n in SMALL}
        sl["conv_w"] = jnp.pad(cw_full[l], ((0, CONV_HALO - CONV_K), (0, 0)))
        smalls.append(sl)
    smalls[0]["norm_mix_g"] = _ordered_after(smalls[0]["norm_mix_g"], travel_token)

    ck_idx = jnp.stack([ci] + [2 * px + py for px, py in _other_chips(xi, yi)]).astype(jnp.int32)
    kc_idx = jnp.stack([2 * xi + yi, ci]).astype(jnp.int32)
    got, flight, reduced = {}, {}, {}

    def as_parts(gb):
        keys = sorted(gb)
        return keys, [gb[k].reshape(N_DEV, rows_of[k[1]] // N_DEV, D_MODEL) for k in keys]

    def lands_like(parts, blocks, dtype):
        return [lax.empty((blocks,) + p.shape[1:], dtype) for p in parts]

    def to_sibling(group, gb):
        keys, parts = as_parts(gb)
        flight[group] = (keys, split_start(parts, lands_like(parts, 4, F32), _sibling_plan, 4,
                                           name=f"rs_sibling_{group}_start"))
        return flight[group][1][4]

    def to_chips(group, marker):
        keys, started = flight[group]
        parts, from_sibling = split_wait(started, marker, _sibling_plan, 4, name=f"rs_sibling_{group}_wait")
        chip_sums = sum_for_chips(parts, from_sibling, ck_idx, name=f"rs_sum_for_chips_{group}")
        started = split_start(chip_sums, lands_like(parts, 3, BF16), _chips_plan, 3, name=f"rs_chips_{group}_start")
        flight[group] = (keys, parts, from_sibling, started)
        return started[4]

    def finish(group, marker):
        keys, parts, from_sibling, started = flight[group]
        _, from_chips = split_wait(started, marker, _chips_plan, 3, name=f"rs_chips_{group}_wait")
        reduced.update(zip(keys, sum_final(parts, from_sibling, from_chips, kc_idx, name=f"rs_sum_final_{group}")))

    def stage_done(l, stage, gb, marker):
        gb = {(l, n): g for n, g in gb.items()}
        if l == 1:
            got.update(gb)
            return to_sibling("l1", got) if stage == "mix" else None
        if stage == "ffn":
            return to_chips("l1", marker) + to_sibling("ffn", gb)
        if stage == "mid":
            return to_chips("ffn", marker) + to_sibling("mid", gb)
        if stage == "attn":
            return to_chips("mid", marker)
        if stage == "in":
            return to_sibling("in", gb)
        to_chips("in", marker)
        for group in ("l1", "ffn", "mid", "in"):
            finish(group, marker)
        return None

    loss_part, grad_x, gss = _local_step(x2d, mem2d, tgt2d, weights_of, reached, smalls, stage_done)
    loss = lax.psum(loss_part, ("x", "y", "c"))

    grads = {n: jnp.stack([reduced[(l, n)].T if transposed else reduced[(l, n)] for l in range(DEPTH)])
             for n, _, transposed in BIG}

    small_names = SMALL + ("conv_w",)
    small_shapes = [(DEPTH,) + ((CONV_K, CONV_CH) if n == "conv_w" else P[n].shape[1:]) for n in small_names]
    small_parts = _pack_rows([jnp.stack([gss[l][n].reshape(sh[1:]) for l in range(DEPTH)])
                              for n, sh in zip(small_names, small_shapes)])
    small_sum = sum_devices(all_gather(small_parts, name="ag_small_grads", in_vmem=True)[0])
    for n, g in zip(small_names, _unpack_rows(small_sum, small_shapes)):
        if n == "conv_w":
            g = lax.dynamic_slice_in_dim(g, dev * (CONV_CH // N_DEV), CONV_CH // N_DEV, axis=2)
        grads[n] = g

    delta, new_m, new_v = {}, {}, {}
    for n, _, _ in BIG:
        shape = P[n].shape
        two_d = lambda a: a.reshape(shape[0] * shape[1], shape[2])
        d_, m_, v_ = adamw(two_d(P[n]), two_d(grads[n]), two_d(M[n]), two_d(V[n]), name="adamw_" + n)
        delta[n], new_m[n], new_v[n] = d_.reshape(shape), m_.reshape(shape), v_.reshape(shape)
    shapes = [P[n].shape for n in small_names]
    d_, m_, v_ = adamw(_pack_rows([P[n] for n in small_names]), _pack_rows([grads[n] for n in small_names]),
                       _pack_rows([M[n] for n in small_names]), _pack_rows([V[n] for n in small_names]), name="adamw_small")
    for n, dd, mm_, vv in zip(small_names, _unpack_rows(d_, shapes), _unpack_rows(m_, shapes), _unpack_rows(v_, shapes)):
        delta[n], new_m[n], new_v[n] = dd, mm_, vv

    return (loss, grad_x[None], *[grads[n] for n in order], *[delta[n] for n in order],
            *[new_m[n] for n in order], *[new_v[n] for n in order])
```

```python
import functools

import jax
import jax.numpy as jnp
import numpy as np
from jax import lax
from jax.experimental import pallas as pl
from jax.experimental.pallas import tpu as pltpu

F32 = jnp.float32
BF16 = jnp.bfloat16

D_MODEL = 1024
HEAD_DIM = 64
N_Q_HEADS = 8
N_KV_HEADS = 2
GROUP = N_Q_HEADS // N_KV_HEADS
ATTN_WIDTH = N_Q_HEADS * HEAD_DIM
KV_WIDTH = N_KV_HEADS * HEAD_DIM
QKV_WIDTH = ATTN_WIDTH + 2 * KV_WIDTH
CONV_CH = 512
IN_COLS = QKV_WIDTH + 2 * CONV_CH
CONV_K = 31
CONV_HALO = 32
BLOCK = 128
N_X_HEADS = 4
X_HEAD_DIM = 256
D_FF = 2816
EPS = 1e-6
NEG = -1e30
DEPTH = 2
N_DEV = 8

ADAM_LR = 0.001
ADAM_B1 = 0.9
ADAM_B2 = 0.999
ADAM_EPS = 1e-08
ADAM_WD = 0.01
ADAM_STEP = 10

V7X_VMEM_LIMIT = 56 * 1024 * 1024
LANES = 128

MESH = pl.DeviceIdType.MESH


def _cp(**kw):
    return pltpu.CompilerParams(vmem_limit_bytes=V7X_VMEM_LIMIT, **kw)


def _dot(a, b, dims):
    return lax.dot_general(a.astype(BF16), b.astype(BF16), (dims, ((), ())), preferred_element_type=F32)


def _dot_nn(a, b):
    return _dot(a, b, ((1,), (0,)))


def _dot_nt(a, b):
    return _dot(a, b, ((1,), (1,)))


def _dot_tn(a, b):
    return _dot(a, b, ((0,), (0,)))


def _sigmoid(x):
    return jax.nn.sigmoid(x)


def _rms(x):
    r = lax.rsqrt(jnp.mean(x * x, axis=-1, keepdims=True) + EPS)
    return x * r, r


def _rms_bwd(dy, xhat, r, g):
    dxh = dy * g
    return r * (dxh - xhat * jnp.mean(dxh * xhat, axis=-1, keepdims=True))


def rms_fwd(x, g, *, tm=512):
    m, d = x.shape
    tm = min(tm, m)

    def body(x_ref, g_ref, o_ref):
        xh, _ = _rms(x_ref[...])
        o_ref[...] = (xh * g_ref[...]).astype(o_ref.dtype)

    return pl.pallas_call(
        body, name="rms_fwd", grid=(m // tm,),
        in_specs=[pl.BlockSpec((tm, d), lambda i: (i, 0)), pl.BlockSpec((1, d), lambda i: (0, 0))],
        out_specs=pl.BlockSpec((tm, d), lambda i: (i, 0)),
        out_shape=jax.ShapeDtypeStruct((m, d), BF16), compiler_params=_cp(),
    )(x, g.reshape(1, d))


def rms_bwd(dh, x, g, dres, *, tm=512):
    m, d = x.shape
    tm = min(tm, m)
    has_res = dres is not None

    def body(*refs):
        if has_res:
            dh_ref, x_ref, g_ref, r_ref, dx_ref, dg_ref = refs
        else:
            dh_ref, x_ref, g_ref, dx_ref, dg_ref = refs
        xh, r = _rms(x_ref[...])
        dy = dh_ref[...].astype(F32)

        @pl.when(pl.program_id(0) == 0)
        def _():
            dg_ref[...] = jnp.zeros_like(dg_ref)

        dg_ref[...] += jnp.sum(dy * xh, axis=0, keepdims=True)
        dx = _rms_bwd(dy, xh, r, g_ref[...])
        if has_res:
            dx = dx + r_ref[...]
        dx_ref[...] = dx

    row = pl.BlockSpec((tm, d), lambda i: (i, 0))
    vec = pl.BlockSpec((1, d), lambda i: (0, 0))
    ins = [dh, x, g.reshape(1, d)] + ([dres] if has_res else [])
    return pl.pallas_call(
        body, name="rms_bwd" + ("_res" if has_res else ""), grid=(m // tm,),
        in_specs=[row, row, vec] + ([row] if has_res else []),
        out_specs=[row, vec],
        out_shape=[jax.ShapeDtypeStruct((m, d), F32), jax.ShapeDtypeStruct((1, d), F32)],
        compiler_params=_cp(),
    )(*ins)


def loss_head(y, target, *, tm=512):
    m, d = y.shape

    def body(y_ref, t_ref, dy_ref, l_ref):
        err = y_ref[...] - t_ref[...]
        dy_ref[...] = err * (1.0 / d)

        @pl.when(pl.program_id(0) == 0)
        def _():
            l_ref[...] = jnp.zeros_like(l_ref)

        part = jnp.sum(jnp.sum(err * err, axis=-1, keepdims=True), axis=0, keepdims=True)
        l_ref[...] += jnp.broadcast_to(part * (0.5 / d), l_ref.shape)

    row = pl.BlockSpec((tm, d), lambda i: (i, 0))
    return pl.pallas_call(
        body, name="loss_head", grid=(m // tm,),
        in_specs=[row, row],
        out_specs=[row, pl.BlockSpec((1, LANES), lambda i: (0, 0))],
        out_shape=[jax.ShapeDtypeStruct((m, d), F32), jax.ShapeDtypeStruct((1, LANES), F32)],
        compiler_params=_cp(),
    )(y, target)


def _tile(n, cap):
    if n <= cap:
        return n
    best = None
    for t in range(LANES, cap + 1, LANES):
        if n % t == 0:
            best = t
    assert best is not None, (n, cap)
    return best


def mm(a, b, *, trans_b, out_dtype, res=None, tm=1024, tn_cap=1536, name):
    m, k = a.shape
    n = b.shape[0] if trans_b else b.shape[1]
    assert (b.shape[1] if trans_b else b.shape[0]) == k
    tm = min(tm, m)
    tn = _tile(n, tn_cap)
    has_res = res is not None

    def body(*refs):
        if has_res:
            a_ref, b_ref, r_ref, o_ref = refs
        else:
            a_ref, b_ref, o_ref = refs
        acc = _dot_nt(a_ref[...], b_ref[...]) if trans_b else _dot_nn(a_ref[...], b_ref[...])
        if has_res:
            acc = acc + r_ref[...]
        o_ref[...] = acc.astype(o_ref.dtype)

    b_spec = pl.BlockSpec((tn, k), lambda i, j: (j, 0)) if trans_b else pl.BlockSpec((k, tn), lambda i, j: (0, j))
    o_spec = pl.BlockSpec((tm, tn), lambda i, j: (i, j))
    return pl.pallas_call(
        body, name=name, grid=(m // tm, n // tn),
        in_specs=[pl.BlockSpec((tm, k), lambda i, j: (i, 0)), b_spec] + ([o_spec] if has_res else []),
        out_specs=o_spec,
        out_shape=jax.ShapeDtypeStruct((m, n), out_dtype), compiler_params=_cp(),
    )(*([a, b] + ([res] if has_res else [])))


def mm_tn(a, b, *, name, ta_cap=1536, tb_cap=1024, tk=2048):
    m, ka = a.shape
    nb = b.shape[1]
    assert b.shape[0] == m
    tk = min(tk, m)
    ta = _tile(ka, ta_cap)
    tb = _tile(nb, tb_cap)

    def body(a_ref, b_ref, o_ref):
        @pl.when(pl.program_id(2) == 0)
        def _():
            o_ref[...] = jnp.zeros_like(o_ref)

        o_ref[...] += _dot_tn(a_ref[...], b_ref[...])

    return pl.pallas_call(
        body, name=name, grid=(ka // ta, nb // tb, m // tk),
        in_specs=[pl.BlockSpec((tk, ta), lambda i, j, kk: (kk, i)), pl.BlockSpec((tk, tb), lambda i, j, kk: (kk, j))],
        out_specs=pl.BlockSpec((ta, tb), lambda i, j, kk: (i, j)),
        out_shape=jax.ShapeDtypeStruct((ka, nb), F32), compiler_params=_cp(),
    )(a, b)


def _whole(shape):
    return pl.BlockSpec(shape, lambda i: (0,) * len(shape), pipeline_mode=pl.Buffered(1))


def _rows(tm, n):
    return pl.BlockSpec((tm, n), lambda i: (i, 0))


def _vec(n):
    return pl.BlockSpec((1, n), lambda i: (0, 0))


def _chunks(n, cap=1408):
    size = _tile(n, cap)
    return [(s, size) for s in range(0, n, size)]


def _zero_at_first_step(*refs):
    @pl.when(pl.program_id(0) == 0)
    def _():
        for r in refs:
            r[...] = jnp.zeros_like(r)


def norm_proj(x, g, wt, *, tm=512):
    m, d = x.shape
    n = wt.shape[0]

    def body(x_ref, g_ref, wt_ref, h_ref, u_ref):
        h = (_rms(x_ref[...])[0] * g_ref[...]).astype(BF16)
        h_ref[...] = h
        for s, sz in _chunks(n):
            u_ref[:, s:s + sz] = _dot_nt(h, wt_ref[s:s + sz, :])

    return pl.pallas_call(
        body, name="norm_proj", grid=(m // tm,),
        in_specs=[_rows(tm, d), _vec(d), _whole((n, d))],
        out_specs=[_rows(tm, d), _rows(tm, n)],
        out_shape=[jax.ShapeDtypeStruct((m, d), BF16), jax.ShapeDtypeStruct((m, n), F32)],
        compiler_params=_cp(),
    )(x, g.reshape(1, d), wt)


def _xattn_heads(q_ref, kv_ref, qg_v, kg_v, d):
    out = []
    for h in range(N_X_HEADS):
        cols = slice(h * X_HEAD_DIM, (h + 1) * X_HEAD_DIM)
        qh, rq = _rms(q_ref[:, cols])
        qn = qh * qg_v
        kn = _rms(kv_ref[:, cols])[0] * kg_v
        v = kv_ref[:, d + h * X_HEAD_DIM:d + (h + 1) * X_HEAD_DIM]
        out.append((qh, rq, qn, kn, v, _xattn_probs(qn, kn)))
    return out


def mid_fwd(mixed, x0, w_out, g_x, wq, kv, xqg, xkg, wo, g_f, *, tm=512):
    m, d = x0.shape
    n_mem = kv.shape[0]

    def body(mixed_ref, x0_ref, w_out_ref, g_x_ref, wq_ref, kv_ref, xqg_ref, xkg_ref, wo_ref, g_f_ref,
             x1_ref, h1_ref, qx_ref, o_ref, x2_ref, h2_ref):
        x1 = x0_ref[...] + _dot_nn(mixed_ref[...], w_out_ref[...])
        x1_ref[...] = x1
        h1 = (_rms(x1)[0] * g_x_ref[...]).astype(BF16)
        h1_ref[...] = h1
        qx_ref[...] = _dot_nn(h1, wq_ref[...])
        for h, (_, _, _, _, v, p) in enumerate(_xattn_heads(qx_ref, kv_ref, xqg_ref[...], xkg_ref[...], d)):
            o_ref[:, h * X_HEAD_DIM:(h + 1) * X_HEAD_DIM] = _dot_nn(p, v).astype(o_ref.dtype)
        x2 = x1 + _dot_nn(o_ref[...], wo_ref[...])
        x2_ref[...] = x2
        h2_ref[...] = (_rms(x2)[0] * g_f_ref[...]).astype(BF16)

    sq = _whole((d, d))
    f32_rows, bf_rows = jax.ShapeDtypeStruct((m, d), F32), jax.ShapeDtypeStruct((m, d), BF16)
    return pl.pallas_call(
        body, name="mid_fwd", grid=(m // tm,),
        in_specs=[_rows(tm, d), _rows(tm, d), sq, _vec(d), sq, _whole((n_mem, 2 * d)), _vec(X_HEAD_DIM), _vec(X_HEAD_DIM),
                  sq, _vec(d)],
        out_specs=[_rows(tm, d)] * 6,
        out_shape=[f32_rows, bf_rows, f32_rows, bf_rows, f32_rows, bf_rows],
        compiler_params=_cp(),
    )(mixed, x0, w_out, g_x.reshape(1, d), wq, kv, xqg.reshape(1, X_HEAD_DIM), xkg.reshape(1, X_HEAD_DIM), wo,
      g_f.reshape(1, d))


def ffn_fwd(h2, x2, wt_gu, w_down, *, tm=256):
    m, d = x2.shape
    f = w_down.shape[0]

    def body(h2_ref, x2_ref, wt_gu_ref, w_down_ref, gu_ref, a_ref, x3_ref):
        h = h2_ref[...]
        for s, sz in _chunks(2 * f):
            gu_ref[:, s:s + sz] = _dot_nt(h, wt_gu_ref[s:s + sz, :])
        for s, sz in _chunks(f):
            g = gu_ref[:, s:s + sz]
            a_ref[:, s:s + sz] = (g * _sigmoid(g) * gu_ref[:, f + s:f + s + sz]).astype(a_ref.dtype)
        x3_ref[...] = x2_ref[...] + _dot_nn(a_ref[...], w_down_ref[...])

    return pl.pallas_call(
        body, name="ffn_fwd", grid=(m // tm,),
        in_specs=[_rows(tm, d), _rows(tm, d), _whole((2 * f, d)), _whole((f, d))],
        out_specs=[_rows(tm, 2 * f), _rows(tm, f), _rows(tm, d)],
        out_shape=[jax.ShapeDtypeStruct((m, 2 * f), F32), jax.ShapeDtypeStruct((m, f), BF16),
                   jax.ShapeDtypeStruct((m, d), F32)],
        compiler_params=_cp(),
    )(h2, x2, wt_gu, w_down)


def ffn_bwd(dx3, gu, x2, g_f, w_down, wt_gu, *, tm=256):
    m, d = x2.shape
    f = w_down.shape[0]

    def body(dx3_ref, gu_ref, x2_ref, g_ref, w_down_ref, wt_gu_ref, dgu_ref, dx2_ref, dg_ref):
        _zero_at_first_step(dg_ref)
        dx3 = dx3_ref[...]
        dx3_b = dx3.astype(BF16)
        for s, sz in _chunks(f):
            da = _dot_nt(dx3_b, w_down_ref[s:s + sz, :])
            g = gu_ref[:, s:s + sz]
            u = gu_ref[:, f + s:f + s + sz]
            sg = _sigmoid(g)
            dgu_ref[:, s:s + sz] = (da * u * (sg * (1.0 + g * (1.0 - sg)))).astype(dgu_ref.dtype)
            dgu_ref[:, f + s:f + s + sz] = (da * (g * sg)).astype(dgu_ref.dtype)
        dh2 = _dot_nn(dgu_ref[...], wt_gu_ref[...])
        xh, r = _rms(x2_ref[...])
        dg_ref[...] += jnp.sum(dh2 * xh, axis=0, keepdims=True)
        dx2_ref[...] = dx3 + _rms_bwd(dh2, xh, r, g_ref[...])

    return pl.pallas_call(
        body, name="ffn_bwd", grid=(m // tm,),
        in_specs=[_rows(tm, d), _rows(tm, 2 * f), _rows(tm, d), _vec(d), _whole((f, d)), _whole((2 * f, d))],
        out_specs=[_rows(tm, 2 * f), _rows(tm, d), _vec(d)],
        out_shape=[jax.ShapeDtypeStruct((m, 2 * f), BF16), jax.ShapeDtypeStruct((m, d), F32),
                   jax.ShapeDtypeStruct((1, d), F32)],
        compiler_params=_cp(),
    )(dx3, gu, x2, g_f.reshape(1, d), w_down, wt_gu)


def mid_bwd(dx2, qx, kv, xqg, xkg, x1, g_x, wo, wq, w_out, *, tm=512):
    m, d = x1.shape
    n_mem = kv.shape[0]
    nt = m // tm

    def body(dx2_ref, qx_ref, kv_ref, xqg_ref, xkg_ref, x1_ref, g_x_ref, wo_ref, wq_ref, w_out_ref,
             dq_ref, dx1_ref, dmixed_ref, dkv_ref, dqg_ref, dkg_ref, dg_ref):
        i = pl.program_id(0)
        _zero_at_first_step(dkv_ref, dqg_ref, dkg_ref, dg_ref)
        qg_v, kg_v = xqg_ref[...], xkg_ref[...]
        dx2 = dx2_ref[...]
        do = _dot_nt(dx2, wo_ref[...])
        dqg_acc = jnp.zeros((1, X_HEAD_DIM), F32)
        for h, (qh, rq, qn, kn, v, p) in enumerate(_xattn_heads(qx_ref, kv_ref, qg_v, kg_v, d)):
            cols = slice(h * X_HEAD_DIM, (h + 1) * X_HEAD_DIM)
            vcols = slice(d + h * X_HEAD_DIM, d + (h + 1) * X_HEAD_DIM)
            do_h = do[:, cols]
            dp = _dot_nt(do_h, v)
            ds = p * (dp - jnp.sum(p * dp, axis=-1, keepdims=True))
            dkv_ref[:, vcols] += _dot_tn(p, do_h)
            dqn = _dot_nn(ds, kn) * (X_HEAD_DIM ** -0.5)
            dkv_ref[:, cols] += _dot_tn(ds, qn) * (X_HEAD_DIM ** -0.5)
            dqg_acc = dqg_acc + jnp.sum(dqn * qh, axis=0, keepdims=True)
            dq_ref[:, cols] = _rms_bwd(dqn, qh, rq, qg_v).astype(dq_ref.dtype)
        dqg_ref[...] += dqg_acc
        dh1 = _dot_nt(dq_ref[...], wq_ref[...])
        xh, r = _rms(x1_ref[...])
        dg_ref[...] += jnp.sum(dh1 * xh, axis=0, keepdims=True)
        dx1 = dx2 + _rms_bwd(dh1, xh, r, g_x_ref[...])
        dx1_ref[...] = dx1
        dmixed_ref[...] = _dot_nt(dx1, w_out_ref[...])

        @pl.when(i == nt - 1)
        def _():
            dkg_acc = jnp.zeros((1, X_HEAD_DIM), F32)
            for h in range(N_X_HEADS):
                cols = slice(h * X_HEAD_DIM, (h + 1) * X_HEAD_DIM)
                kh, rk = _rms(kv_ref[:, cols])
                dkn = dkv_ref[:, cols]
                dkg_acc = dkg_acc + jnp.sum(dkn * kh, axis=0, keepdims=True)
                dkv_ref[:, cols] = _rms_bwd(dkn, kh, rk, kg_v)
            dkg_ref[...] = dkg_acc

    sq = _whole((d, d))
    full = pl.BlockSpec((n_mem, 2 * d), lambda i: (0, 0))
    return pl.pallas_call(
        body, name="mid_bwd", grid=(nt,),
        in_specs=[_rows(tm, d), _rows(tm, d), _whole((n_mem, 2 * d)), _vec(X_HEAD_DIM), _vec(X_HEAD_DIM), _rows(tm, d),
                  _vec(d), sq, sq, sq],
        out_specs=[_rows(tm, d), _rows(tm, d), _rows(tm, d), full, _vec(X_HEAD_DIM), _vec(X_HEAD_DIM), _vec(d)],
        out_shape=[jax.ShapeDtypeStruct((m, d), BF16), jax.ShapeDtypeStruct((m, d), F32), jax.ShapeDtypeStruct((m, d), F32),
                   jax.ShapeDtypeStruct((n_mem, 2 * d), F32), jax.ShapeDtypeStruct((1, X_HEAD_DIM), F32),
                   jax.ShapeDtypeStruct((1, X_HEAD_DIM), F32), jax.ShapeDtypeStruct((1, d), F32)],
        compiler_params=_cp(),
    )(dx2, qx, kv, xqg.reshape(1, X_HEAD_DIM), xkg.reshape(1, X_HEAD_DIM), x1, g_x.reshape(1, d), wo, wq, w_out)


def in_bwd(du, wt_in, x0, g_mix, dx1, *, tm=512):
    m, d = x0.shape
    n = wt_in.shape[0]

    def body(du_ref, wt_ref, x0_ref, g_ref, dx1_ref, dx0_ref, dg_ref):
        _zero_at_first_step(dg_ref)
        dh0 = _dot_nn(du_ref[...], wt_ref[...])
        xh, r = _rms(x0_ref[...])
        dg_ref[...] += jnp.sum(dh0 * xh, axis=0, keepdims=True)
        dx0_ref[...] = dx1_ref[...] + _rms_bwd(dh0, xh, r, g_ref[...])

    return pl.pallas_call(
        body, name="in_bwd", grid=(m // tm,),
        in_specs=[_rows(tm, n), _whole((n, d)), _rows(tm, d), _vec(d), _rows(tm, d)],
        out_specs=[_rows(tm, d), _vec(d)],
        out_shape=[jax.ShapeDtypeStruct((m, d), F32), jax.ShapeDtypeStruct((1, d), F32)],
        compiler_params=_cp(),
    )(du, wt_in, x0, g_mix.reshape(1, d), dx1)


SWA_TILE = 512
SWA_SUB = SWA_TILE // BLOCK
SWA_KEYS = SWA_TILE + BLOCK
PAIR = 2 * HEAD_DIM
KCOL = ATTN_WIDTH
VCOL = ATTN_WIDTH + KV_WIDTH


def _swa_constants():
    r = np.arange(2 * BLOCK)[:, None]
    j = np.arange(4 * BLOCK)[None, :]
    dist = (r % BLOCK) + BLOCK - (j % (2 * BLOCK))
    valid = (dist >= 0) & (dist < BLOCK)
    first_valid = valid & ((j % (2 * BLOCK)) >= BLOCK)
    bias, bias_first = [], []
    for kv in range(N_KV_HEADS):
        head = kv * GROUP + 2 * (r // BLOCK) + j // (2 * BLOCK)
        b = -(2.0 ** -(head + 1.0)) * dist
        bias.append(np.where(valid, b, NEG))
        bias_first.append(np.where(first_valid, b, NEG))
    lane = np.arange(LANES)
    seg = (lane[:, None] // HEAD_DIM == lane[None, :] // HEAD_DIM) / HEAD_DIM
    row = np.arange(4 * BLOCK)[:, None]
    ones = (row // (2 * BLOCK)) == (lane[None, :] // HEAD_DIM)
    return (jnp.asarray(np.stack(bias), F32), jnp.asarray(np.stack(bias_first), F32), jnp.asarray(seg, BF16),
            jnp.asarray(ones, BF16))


def _segmean(x, seg_ref):
    hi = x.astype(BF16)
    lo = (x - hi.astype(F32)).astype(BF16)
    return _dot_nn(hi, seg_ref[...]) + _dot_nn(lo, seg_ref[...])


def _two_heads(x, kv):
    lane = lax.broadcasted_iota(jnp.int32, (1, LANES), 1)
    mine = (lane < HEAD_DIM) if kv == 0 else (lane >= HEAD_DIM)
    base = jnp.where(mine, x, 0.0)
    other = pltpu.roll(base, HEAD_DIM, 1)
    return jnp.concatenate([base, other] if kv == 0 else [other, base], axis=0)


def _from_two_heads(y, kv):
    rows = y.shape[0] // 2
    lane = lax.broadcasted_iota(jnp.int32, (1, LANES), 1)
    top, bot = y[:rows], y[rows:]
    if kv == 0:
        return jnp.where(lane < HEAD_DIM, top + pltpu.roll(bot, HEAD_DIM, 1), 0.0)
    return jnp.where(lane >= HEAD_DIM, pltpu.roll(top, HEAD_DIM, 1) + bot, 0.0)


def _pair_rows(ref, rows, kv):
    c = kv * 2 * PAIR
    return jnp.concatenate([ref[rows, c:c + PAIR], ref[rows, c + PAIR:c + 2 * PAIR]], axis=0)


def _head_cols(fn, kv):
    return [jnp.concatenate([fn(kv * GROUP + half), fn(kv * GROUP + 2 + half)], axis=0) for half in range(2)]


def _swa_prologue(cur_ref, prev_ref, qg_ref, kg_ref, seg_ref, qg_s, kn_s, v_s):
    qg_s[...] = (cur_ref[:, 0:ATTN_WIDTH] * qg_ref[...]).astype(BF16)
    k = jnp.concatenate([prev_ref[:, KCOL:KCOL + KV_WIDTH], cur_ref[:, KCOL:KCOL + KV_WIDTH]], axis=0)
    kn_s[...] = k * lax.rsqrt(_segmean(k * k, seg_ref) + EPS) * kg_ref[...]
    v_s[0:BLOCK, :] = prev_ref[:, VCOL:VCOL + KV_WIDTH]
    v_s[BLOCK:SWA_KEYS, :] = cur_ref[:, VCOL:VCOL + KV_WIDTH]


def _swa_scores(cur_ref, sinks_ref, qg_s, kn_s, bias, rows, keys, kv):
    q2 = _pair_rows(qg_s, rows, kv)
    k2 = _two_heads(kn_s[keys, :], kv)
    t = _dot_nt(q2, k2)

    def rq(h):
        x = cur_ref[rows, h * HEAD_DIM:(h + 1) * HEAD_DIM]
        return lax.rsqrt(jnp.mean(x * x, axis=-1, keepdims=True) + EPS)

    scale = _head_cols(lambda h: rq(h) * (HEAD_DIM ** -0.5), kv)
    sink = _head_cols(lambda h: jnp.full((BLOCK, 1), sinks_ref[h], F32), kv)
    halves = []
    for half in range(2):
        cols = slice(half * 2 * BLOCK, (half + 1) * 2 * BLOCK)
        s = t[:, cols] * scale[half] + bias[:, cols]
        mx = jnp.maximum(jnp.max(s, axis=-1, keepdims=True), sink[half])
        halves.append((scale[half], jnp.exp(s - mx), jnp.exp(sink[half] - mx)))
    return q2, k2, t, halves


def swa_fwd(u, qg, kg, sinks):
    t_rows = u.shape[0]
    nt = t_rows // SWA_TILE
    bias_c, bias_first_c, seg_c, ones_c = _swa_constants()

    def body(sinks_ref, cur_ref, prev_ref, qg_ref, kg_ref, seg_ref, bias_ref, biasf_ref, ones_ref, o_ref, qg_s, kn_s, v_s):
        i = pl.program_id(0)
        _swa_prologue(cur_ref, prev_ref, qg_ref, kg_ref, seg_ref, qg_s, kn_s, v_s)
        lane = lax.broadcasted_iota(jnp.int32, (1, LANES), 1)
        for b in range(SWA_SUB):
            rows = slice(b * BLOCK, (b + 1) * BLOCK)
            keys = slice(b * BLOCK, (b + 2) * BLOCK)
            for kv in range(N_KV_HEADS):
                bias = jnp.where(i == 0, biasf_ref[kv], bias_ref[kv]) if b == 0 else bias_ref[kv]
                _, _, _, halves = _swa_scores(cur_ref, sinks_ref, qg_s, kn_s, bias, rows, keys, kv)
                e = jnp.concatenate([halves[0][1], halves[1][1]], axis=1)
                v2 = jnp.concatenate([_two_heads(v_s[keys, :], kv).astype(BF16), ones_ref[...]], axis=1)
                ox = _dot_nn(e, v2)
                den = ox[:, LANES:] + jnp.where(lane < HEAD_DIM, halves[0][2], halves[1][2])
                out = (ox[:, :LANES] / den).astype(o_ref.dtype)
                c = kv * 2 * PAIR
                o_ref[rows, c:c + PAIR] = out[:BLOCK]
                o_ref[rows, c + PAIR:c + 2 * PAIR] = out[BLOCK:]

    const3 = pl.BlockSpec((N_KV_HEADS, 2 * BLOCK, 4 * BLOCK), lambda i: (0, 0, 0))
    return pl.pallas_call(
        body, name="swa_fwd", grid=(nt,),
        in_specs=[
            pl.BlockSpec(memory_space=pltpu.SMEM),
            pl.BlockSpec((SWA_TILE, QKV_WIDTH), lambda i: (i, 0)),
            pl.BlockSpec((BLOCK, QKV_WIDTH), lambda i: (jnp.maximum(i * SWA_SUB - 1, 0), 0)),
            pl.BlockSpec((1, ATTN_WIDTH), lambda i: (0, 0)), pl.BlockSpec((1, KV_WIDTH), lambda i: (0, 0)),
            pl.BlockSpec((LANES, LANES), lambda i: (0, 0)), const3, const3,
            pl.BlockSpec((4 * BLOCK, LANES), lambda i: (0, 0)),
        ],
        out_specs=pl.BlockSpec((SWA_TILE, ATTN_WIDTH), lambda i: (i, 0)),
        out_shape=jax.ShapeDtypeStruct((t_rows, 2 * ATTN_WIDTH), BF16),
        scratch_shapes=[pltpu.VMEM((SWA_TILE, ATTN_WIDTH), BF16), pltpu.VMEM((SWA_KEYS, KV_WIDTH), F32),
                        pltpu.VMEM((SWA_KEYS, KV_WIDTH), F32)],
        compiler_params=_cp(),
    )(sinks, u, u, jnp.tile(qg, N_Q_HEADS).reshape(1, ATTN_WIDTH), jnp.tile(kg, N_KV_HEADS).reshape(1, KV_WIDTH),
      seg_c, bias_c, bias_first_c, ones_c)


def swa_bwd(u, dmixed, qg, kg, sinks):
    t_rows = u.shape[0]
    nt = t_rows // SWA_TILE
    bias_c, bias_first_c, seg_c, _ = _swa_constants()

    def body(sinks_ref, cur_ref, prev_ref, do_ref, qg_ref, kg_ref, seg_ref, bias_ref, biasf_ref,
             du_ref, dqg_ref, dkg_ref, dsk_ref, qg_s, kn_s, v_s, acck_s, accv_s, carryk_s, carryv_s):
        step = pl.program_id(0)
        i = nt - 1 - step

        @pl.when(step == 0)
        def _():
            for r in (carryk_s, carryv_s, dqg_ref, dkg_ref, dsk_ref):
                r[...] = jnp.zeros_like(r)

        _swa_prologue(cur_ref, prev_ref, qg_ref, kg_ref, seg_ref, qg_s, kn_s, v_s)
        for acc, carry in ((acck_s, carryk_s), (accv_s, carryv_s)):
            acc[0:SWA_TILE, :] = jnp.zeros((SWA_TILE, KV_WIDTH), F32)
            acc[SWA_TILE:SWA_KEYS, :] = carry[...]

        lane = lax.broadcasted_iota(jnp.int32, (1, LANES), 1)
        g_pair = qg_ref[:, 0:PAIR]
        dqg_acc = jnp.zeros((1, PAIR), F32)
        dsk_acc = jnp.zeros((1, LANES), F32)
        for b in range(SWA_SUB):
            rows = slice(b * BLOCK, (b + 1) * BLOCK)
            keys = slice(b * BLOCK, (b + 2) * BLOCK)
            for kv in range(N_KV_HEADS):
                bias = jnp.where(i == 0, biasf_ref[kv], bias_ref[kv]) if b == 0 else bias_ref[kv]
                q2, k2, t, halves = _swa_scores(cur_ref, sinks_ref, qg_s, kn_s, bias, rows, keys, kv)
                v2 = _two_heads(v_s[keys, :], kv)
                do2 = _pair_rows(do_ref, rows, kv)
                dp = _dot_nt(do2, v2)
                p_parts, dt_parts, coef = [], [], []
                for half, (scale, e, es) in enumerate(halves):
                    cols = slice(half * 2 * BLOCK, (half + 1) * 2 * BLOCK)
                    rden = 1.0 / (jnp.sum(e, axis=-1, keepdims=True) + es)
                    p = e * rden
                    dp_h = dp[:, cols]
                    delta = jnp.sum(p * dp_h, axis=-1, keepdims=True)
                    ds = p * (dp_h - delta)
                    dsink = -(es * rden) * delta
                    for pair in range(2):
                        part = jnp.sum(dsink[pair * BLOCK:(pair + 1) * BLOCK], axis=0, keepdims=True)
                        dsk_acc = dsk_acc + jnp.where(lane == kv * GROUP + 2 * pair + half, part, 0.0)
                    dscale = jnp.sum(ds * t[:, cols], axis=-1, keepdims=True)
                    coef.append(-dscale * scale * scale * scale)
                    p_parts.append(p)
                    dt_parts.append(ds * scale)
                p2 = jnp.concatenate(p_parts, axis=1)
                dt = jnp.concatenate(dt_parts, axis=1)
                dqg2 = _dot_nn(dt, k2)
                q_raw = _pair_rows(cur_ref, rows, kv)
                dq = dqg2 * g_pair + jnp.where(lane < HEAD_DIM, coef[0], coef[1]) * q_raw
                dqg_acc = dqg_acc + jnp.sum(dqg2 * q_raw, axis=0, keepdims=True)
                c = kv * 2 * PAIR
                du_ref[rows, c:c + PAIR] = dq[:BLOCK].astype(du_ref.dtype)
                du_ref[rows, c + PAIR:c + 2 * PAIR] = dq[BLOCK:].astype(du_ref.dtype)
                acck_s[keys, :] += _from_two_heads(_dot_tn(dt, q2), kv)
                accv_s[keys, :] += _from_two_heads(_dot_tn(p2, do2), kv)
        dqg_ref[...] += dqg_acc + pltpu.roll(dqg_acc, HEAD_DIM, 1)
        dsk_ref[...] += dsk_acc

        own = slice(BLOCK, SWA_KEYS)
        k = cur_ref[:, KCOL:KCOL + KV_WIDTH]
        rk = lax.rsqrt(_segmean(k * k, seg_ref) + EPS)
        kh = k * rk
        dkn = acck_s[own, :]
        dkh = dkn * kg_ref[...]
        du_ref[:, KCOL:KCOL + KV_WIDTH] = (rk * (dkh - kh * _segmean(dkh * kh, seg_ref))).astype(du_ref.dtype)
        du_ref[:, VCOL:VCOL + KV_WIDTH] = accv_s[own, :].astype(du_ref.dtype)
        dkg_part = jnp.sum(dkn * kh, axis=0, keepdims=True)
        dkg_ref[...] += dkg_part + pltpu.roll(dkg_part, HEAD_DIM, 1)
        carryk_s[...] = acck_s[0:BLOCK, :]
        carryv_s[...] = accv_s[0:BLOCK, :]

    const3 = pl.BlockSpec((N_KV_HEADS, 2 * BLOCK, 4 * BLOCK), lambda s: (0, 0, 0))
    vec = pl.BlockSpec((1, LANES), lambda s: (0, 0))
    return pl.pallas_call(
        body, name="swa_bwd", grid=(nt,),
        in_specs=[
            pl.BlockSpec(memory_space=pltpu.SMEM),
            pl.BlockSpec((SWA_TILE, QKV_WIDTH), lambda s: (nt - 1 - s, 0)),
            pl.BlockSpec((BLOCK, QKV_WIDTH), lambda s: (jnp.maximum((nt - 1 - s) * SWA_SUB - 1, 0), 0)),
            pl.BlockSpec((SWA_TILE, ATTN_WIDTH), lambda s: (nt - 1 - s, 0)),
            pl.BlockSpec((1, ATTN_WIDTH), lambda s: (0, 0)), vec,
            pl.BlockSpec((LANES, LANES), lambda s: (0, 0)), const3, const3,
        ],
        out_specs=[pl.BlockSpec((SWA_TILE, QKV_WIDTH), lambda s: (nt - 1 - s, 0)), vec, vec, vec],
        out_shape=[jax.ShapeDtypeStruct((t_rows, IN_COLS), BF16)] + [jax.ShapeDtypeStruct((1, LANES), F32)] * 3,
        scratch_shapes=[pltpu.VMEM((SWA_TILE, ATTN_WIDTH), BF16)] + [pltpu.VMEM((SWA_KEYS, KV_WIDTH), F32)] * 4
        + [pltpu.VMEM((BLOCK, KV_WIDTH), F32)] * 2,
        compiler_params=_cp(),
    )(sinks, u, u, dmixed, jnp.tile(qg, N_Q_HEADS).reshape(1, ATTN_WIDTH), jnp.tile(kg, N_KV_HEADS).reshape(1, KV_WIDTH),
      seg_c, bias_c, bias_first_c)


CONV_TILE = 512
CONV_CHUNK = 64
VAL0 = QKV_WIDTH
GATE0 = QKV_WIDTH + CONV_CH


def _glu(ref):
    return ref[:, VAL0:GATE0] * _sigmoid(ref[:, GATE0:GATE0 + CONV_CH])


SUBLANES = 8
CONV_BUF = CONV_HALO + CONV_TILE + SUBLANES
CONV_EXT = CONV_HALO + CONV_TILE


def _fill_shifted(sh_ref):
    for r in range(1, SUBLANES):
        sh_ref[r, 0:CONV_EXT, :] = sh_ref[0, pl.ds(r, CONV_EXT), :]


def _shifted(sh_ref, start, offset, n):
    return sh_ref[offset % SUBLANES, pl.ds(start + offset - offset % SUBLANES, n), :]


def _layernorm_stats(y):
    mu = jnp.mean(y, axis=-1, keepdims=True)
    yc = y - mu
    rstd = lax.rsqrt(jnp.mean(yc * yc, axis=-1, keepdims=True) + EPS)
    return yc * rstd, rstd


def conv_fwd(u, mixed, conv_w, conv_b, ln_g, ln_b):
    t = u.shape[0]
    nt = t // CONV_TILE
    per = CONV_TILE // CONV_HALO

    def body(cur_ref, prev_ref, mixed_ref, w_ref, b_ref, g_ref, b2_ref, o_ref, y_ref, gl_ref):
        del mixed_ref
        i = pl.program_id(0)
        gl_ref[0, 0:CONV_HALO, :] = jnp.where(i > 0, _glu(prev_ref), 0.0)
        gl_ref[0, CONV_HALO:CONV_EXT, :] = _glu(cur_ref)
        gl_ref[0, CONV_EXT:CONV_BUF, :] = jnp.zeros((SUBLANES, CONV_CH), F32)
        _fill_shifted(gl_ref)
        for c0 in range(0, CONV_TILE, CONV_CHUNK):
            acc = jnp.broadcast_to(b_ref[...], (CONV_CHUNK, CONV_CH))
            for k in range(CONV_K):
                acc = acc + w_ref[k:k + 1, :] * _shifted(gl_ref, c0, 2 + k, CONV_CHUNK)
            y_ref[c0:c0 + CONV_CHUNK, :] = acc
        yh, _ = _layernorm_stats(y_ref[...])
        yln = yh * g_ref[...] + b2_ref[...]
        o_ref[...] = (yln * _sigmoid(yln)).astype(o_ref.dtype)

    vec = pl.BlockSpec((1, CONV_CH), lambda i: (0, 0))
    return pl.pallas_call(
        body, name="conv_fwd", grid=(nt,),
        in_specs=[
            pl.BlockSpec((CONV_TILE, IN_COLS), lambda i: (i, 0)),
            pl.BlockSpec((CONV_HALO, IN_COLS), lambda i: (jnp.maximum(i * per - 1, 0), 0)),
            pl.BlockSpec(memory_space=pl.ANY),
            pl.BlockSpec((CONV_HALO, CONV_CH), lambda i: (0, 0)),
            vec, vec, vec,
        ],
        out_specs=[pl.BlockSpec((CONV_TILE, CONV_CH), lambda i: (i, 1)), pl.BlockSpec((CONV_TILE, CONV_CH), lambda i: (i, 0))],
        out_shape=[jax.ShapeDtypeStruct(mixed.shape, mixed.dtype), jax.ShapeDtypeStruct((t, CONV_CH), F32)],
        scratch_shapes=[pltpu.VMEM((SUBLANES, CONV_BUF, CONV_CH), F32)],
        input_output_aliases={2: 0}, compiler_params=_cp(),
    )(u, u, mixed, conv_w, conv_b.reshape(1, CONV_CH), ln_g.reshape(1, CONV_CH), ln_b.reshape(1, CONV_CH))


def conv_bwd(u, y, dmixed, du, conv_w, ln_g, ln_b):
    t = u.shape[0]
    nt = t // CONV_TILE
    per = CONV_TILE // CONV_HALO

    def body(cur_ref, prev_ref, y_ref, yn_ref, do_ref, don_ref, du_in_ref, w_ref, g_ref, b2_ref,
             du_ref, dw_ref, dvec_ref, gl_ref, dy_ref):
        i = pl.program_id(0)
        last = i == nt - 1
        _zero_at_first_step(dw_ref, dvec_ref)

        gl_ref[0, 0:CONV_HALO, :] = jnp.where(i > 0, _glu(prev_ref), 0.0)
        gl_ref[0, CONV_HALO:CONV_EXT, :] = _glu(cur_ref)
        gl_ref[0, CONV_EXT:CONV_BUF, :] = jnp.zeros((SUBLANES, CONV_CH), F32)
        _fill_shifted(gl_ref)

        yh, rstd = _layernorm_stats(jnp.concatenate([y_ref[...], yn_ref[...]], axis=0))
        g = g_ref[...]
        yln = yh * g + b2_ref[...]
        sg = _sigmoid(yln)
        dout = jnp.concatenate([do_ref[...], jnp.where(last, 0.0, don_ref[...])], axis=0)
        dyln = dout * (sg * (1.0 + yln * (1.0 - sg)))
        dyh = dyln * g
        dy = rstd * (dyh - jnp.mean(dyh, axis=-1, keepdims=True) - yh * jnp.mean(dyh * yh, axis=-1, keepdims=True))
        dy_ref[0, 0:CONV_EXT, :] = dy
        dy_ref[0, CONV_EXT:CONV_BUF, :] = jnp.zeros((SUBLANES, CONV_CH), F32)
        _fill_shifted(dy_ref)

        own = slice(0, CONV_TILE)
        dvec_ref[0:1, :] += jnp.sum(dy[own], axis=0, keepdims=True)
        dvec_ref[1:2, :] += jnp.sum(dyln[own] * yh[own], axis=0, keepdims=True)
        dvec_ref[2:3, :] += jnp.sum(dyln[own], axis=0, keepdims=True)
        for k in range(CONV_K):
            dw_ref[k:k + 1, :] += jnp.sum(dy[own] * _shifted(gl_ref, 0, 2 + k, CONV_TILE), axis=0, keepdims=True)

        for c0 in range(0, CONV_TILE, CONV_CHUNK):
            acc = jnp.zeros((CONV_CHUNK, CONV_CH), F32)
            for k in range(CONV_K):
                acc = acc + w_ref[k:k + 1, :] * _shifted(dy_ref, c0, CONV_K - 1 - k, CONV_CHUNK)
            rows = slice(c0, c0 + CONV_CHUNK)
            val = cur_ref[rows, VAL0:GATE0]
            sgate = _sigmoid(cur_ref[rows, GATE0:GATE0 + CONV_CH])
            du_ref[rows, VAL0:GATE0] = (acc * sgate).astype(du_ref.dtype)
            du_ref[rows, GATE0:GATE0 + CONV_CH] = (acc * val * sgate * (1.0 - sgate)).astype(du_ref.dtype)
        du_ref[:, 0:QKV_WIDTH] = du_in_ref[:, 0:QKV_WIDTH]

    vec = pl.BlockSpec((1, CONV_CH), lambda i: (0, 0))
    n_halo = t // CONV_HALO
    return pl.pallas_call(
        body, name="conv_bwd", grid=(nt,),
        in_specs=[
            pl.BlockSpec((CONV_TILE, IN_COLS), lambda i: (i, 0)),
            pl.BlockSpec((CONV_HALO, IN_COLS), lambda i: (jnp.maximum(i * per - 1, 0), 0)),
            pl.BlockSpec((CONV_TILE, CONV_CH), lambda i: (i, 0)),
            pl.BlockSpec((CONV_HALO, CONV_CH), lambda i: (jnp.minimum((i + 1) * per, n_halo - 1), 0)),
            pl.BlockSpec((CONV_TILE, CONV_CH), lambda i: (i, 1)),
            pl.BlockSpec((CONV_HALO, CONV_CH), lambda i: (jnp.minimum((i + 1) * per, n_halo - 1), 1)),
            pl.BlockSpec((CONV_TILE, IN_COLS), lambda i: (i, 0)),
            pl.BlockSpec((CONV_HALO, CONV_CH), lambda i: (0, 0)),
            vec, vec,
        ],
        out_specs=[
            pl.BlockSpec((CONV_TILE, IN_COLS), lambda i: (i, 0)),
            pl.BlockSpec((CONV_HALO, CONV_CH), lambda i: (0, 0)),
            pl.BlockSpec((8, CONV_CH), lambda i: (0, 0)),
        ],
        out_shape=[
            jax.ShapeDtypeStruct(du.shape, du.dtype),
            jax.ShapeDtypeStruct((CONV_HALO, CONV_CH), F32),
            jax.ShapeDtypeStruct((8, CONV_CH), F32),
        ],
        scratch_shapes=[pltpu.VMEM((SUBLANES, CONV_BUF, CONV_CH), F32), pltpu.VMEM((SUBLANES, CONV_BUF, CONV_CH), F32)],
        input_output_aliases={6: 0}, compiler_params=_cp(),
    )(u, u, y, y, dmixed, dmixed, du, conv_w, ln_g.reshape(1, CONV_CH), ln_b.reshape(1, CONV_CH))


def _xattn_probs(qn, kn):
    s = _dot_nt(qn, kn) * (X_HEAD_DIM ** -0.5)
    e = jnp.exp(s - jnp.max(s, axis=-1, keepdims=True))
    return e / jnp.sum(e, axis=-1, keepdims=True)


def adamw(w, g, m, v, *, name):
    r, c = w.shape
    tr = r
    for cand in (512, 256, 128, 64, 32, 16, 8):
        if r % cand == 0 and r > cand:
            tr = cand
            break

    def body(w_ref, g_ref, m_ref, v_ref, d_ref, nm_ref, nv_ref):
        g_v = g_ref[...]
        m2 = ADAM_B1 * m_ref[...] + (1.0 - ADAM_B1) * g_v
        v2 = ADAM_B2 * v_ref[...] + (1.0 - ADAM_B2) * jnp.square(g_v)
        m_hat = m2 / (1.0 - ADAM_B1 ** ADAM_STEP)
        v_hat = v2 / (1.0 - ADAM_B2 ** ADAM_STEP)
        d_ref[...] = -ADAM_LR * (m_hat / (jnp.sqrt(v_hat) + ADAM_EPS) + ADAM_WD * w_ref[...])
        nm_ref[...] = m2
        nv_ref[...] = v2

    spec = pl.BlockSpec((tr, c), lambda i: (i, 0))
    shape = jax.ShapeDtypeStruct((r, c), F32)
    return pl.pallas_call(
        body, name=name, grid=(r // tr,), in_specs=[spec] * 4, out_specs=[spec] * 3,
        out_shape=[shape] * 3, compiler_params=_cp(),
    )(w, g, m, v)


def _position():
    return lax.axis_index("x"), lax.axis_index("y"), lax.axis_index("c")


def all_gather_many(shards, *, name):
    n = len(shards)

    def body(*refs):
        x_refs, out_refs, token_ref = refs[:n], refs[n:2 * n], refs[2 * n]
        send_sems, recv_sems, local_sems = refs[2 * n + 1:]
        x, y, c = _position()
        me, sibling = (x, y, c), (x, y, 1 - c)
        chips = [(1 - x, y), (x, 1 - y), (1 - x, 1 - y)]
        token_ref[...] = jnp.zeros_like(token_ref)

        def rows(t, px, py, pc):
            return out_refs[t].at[4 * px + 2 * py + pc]

        def copy(t, k, block, to, src=None):
            return pltpu.make_async_remote_copy(
                src_ref=rows(t, *block) if src is None else src, dst_ref=rows(t, *block),
                send_sem=send_sems.at[7 * t + k], recv_sem=recv_sems.at[7 * t + k], device_id=to, device_id_type=MESH)

        mine = [pltpu.make_async_copy(x_refs[t], rows(t, *me), local_sems.at[t]) for t in range(n)]
        for cp in mine:
            cp.start()
        first = []
        for t in range(n):
            first.append(copy(t, 0, me, sibling, src=x_refs[t]))
            first += [copy(t, 1 + j, me, (*chip, c), src=x_refs[t]) for j, chip in enumerate(chips)]
        for cp in first:
            cp.start()
        passed = []
        for t in range(n):
            for j, chip in enumerate(chips):
                copy(t, 1 + j, (*chip, c), me).wait_recv()
                passed.append(copy(t, 4 + j, (*chip, c), sibling))
                passed[-1].start()
        for t in range(n):
            copy(t, 0, sibling, me).wait_recv()
            for j, chip in enumerate(chips):
                copy(t, 4 + j, (*chip, 1 - c), me).wait_recv()
        for cp in first + passed:
            cp.wait_send()
        for cp in mine:
            cp.wait()

    hbm = pl.BlockSpec(memory_space=pltpu.HBM)
    out = pl.pallas_call(
        body, name=name,
        out_shape=[jax.ShapeDtypeStruct((N_DEV,) + s.shape, s.dtype) for s in shards] + [jax.ShapeDtypeStruct((8, LANES), F32)],
        in_specs=[hbm] * n, out_specs=[hbm] * n + [pl.BlockSpec(memory_space=pltpu.VMEM)],
        scratch_shapes=[pltpu.SemaphoreType.DMA((7 * n,)), pltpu.SemaphoreType.DMA((7 * n,)), pltpu.SemaphoreType.DMA((n,))],
        compiler_params=_cp(),
    )(*shards)
    return out[:n], out[n]


_HBM = pl.BlockSpec(memory_space=pltpu.HBM)
_SEM = pl.BlockSpec(memory_space=pltpu.SEMAPHORE)
_EFFECT = pltpu.SideEffectType.DATAFLOW_SIDE_EFFECTING


def _split_copies(src_refs, land_refs, send_sems, recv_sems, plan, n_copies):
    copies = []
    for t, (src_ref, land_ref) in enumerate(zip(src_refs, land_refs)):
        for k in range(n_copies):
            s, d, to = plan(src_ref, land_ref, k)
            copies.append(pltpu.make_async_remote_copy(
                src_ref=s, dst_ref=d, send_sem=send_sems.at[n_copies * t + k], recv_sem=recv_sems.at[n_copies * t + k],
                device_id=to, device_id_type=MESH))
    return copies


def split_start(srcs, lands, plan, n_copies, *, name):
    n = len(srcs)

    def body(*refs):
        src_refs, land_refs, send_sems, recv_sems, token = refs[:n], refs[n:2 * n], refs[2 * n], refs[2 * n + 1], refs[-1]
        for cp in _split_copies(src_refs, land_refs, send_sems, recv_sems, plan, n_copies):
            cp.start()
        token[...] = jnp.zeros_like(token)

    both = list(srcs) + list(lands)
    out = pl.pallas_call(
        body, name=name,
        out_shape=(pltpu.SemaphoreType.DMA((n_copies * n,)), pltpu.SemaphoreType.DMA((n_copies * n,)),
                   *[pltpu.HBM(a.shape, a.dtype) for a in both], jax.ShapeDtypeStruct((8, LANES), F32)),
        in_specs=(_HBM,) * (2 * n), out_specs=(_SEM, _SEM) + (_HBM,) * (2 * n) + (pl.BlockSpec(memory_space=pltpu.VMEM),),
        input_output_aliases={i: 2 + i for i in range(2 * n)},
        compiler_params=pltpu.CompilerParams(has_side_effects=_EFFECT),
    )(*[pltpu.with_memory_space_constraint(a, pltpu.HBM) for a in both])
    return out[0], out[1], list(out[2:2 + n]), list(out[2 + n:2 + 2 * n]), out[-1]


def split_wait(started, after, plan, n_copies, *, name):
    send_sems, recv_sems, srcs, lands, _ = started
    n = len(srcs)

    def body(*refs):
        src_refs, land_refs, send_sems, recv_sems = refs[:n], refs[n:2 * n], refs[2 * n], refs[2 * n + 1]
        for cp in _split_copies(src_refs, land_refs, send_sems, recv_sems, plan, n_copies):
            cp.wait_send()
            cp.wait_recv()

    both = list(srcs) + list(lands)
    out = pl.pallas_call(
        body, name=name,
        out_shape=tuple(pltpu.HBM(a.shape, a.dtype) for a in both),
        in_specs=(_HBM,) * (2 * n) + (_SEM, _SEM, pl.BlockSpec(memory_space=pl.ANY)), out_specs=(_HBM,) * (2 * n),
        input_output_aliases={i: i for i in range(2 * n)},
        compiler_params=pltpu.CompilerParams(has_side_effects=_EFFECT),
    )(*both, send_sems, recv_sems, after)
    return list(out[:n]), list(out[n:])


def _other_chips(x, y):
    return [(1 - x, y), (x, 1 - y), (1 - x, 1 - y)]


def _remote(src, dst, send_sem, recv_sem, to):
    return pltpu.make_async_remote_copy(src_ref=src, dst_ref=dst, send_sem=send_sem, recv_sem=recv_sem,
                                        device_id=to, device_id_type=MESH)


def gather_start(groups, *, name):
    counts = [len(shards) for shards, _ in groups]
    flat = [a for shards, _ in groups for a in shards] + [a for _, lands in groups for a in lands]
    n_all, n_groups = sum(counts), len(groups)

    def body(*refs):
        s_refs, l_refs = refs[:n_all], refs[n_all:2 * n_all]
        sems = refs[2 * n_all:2 * n_all + 3 * n_groups]
        x, y, c = _position()
        me = 4 * x + 2 * y + c
        at = 0
        for gi, n in enumerate(counts):
            send, recv_sibling, recv_ici = sems[3 * gi:3 * gi + 3]
            for t in range(n):
                src, dst = s_refs[at + t], l_refs[at + t].at[me]
                _remote(src, dst, send.at[4 * t], recv_sibling.at[t], (x, y, 1 - c)).start()
                for j, chip in enumerate(_other_chips(x, y)):
                    _remote(src, dst, send.at[4 * t + 1 + j], recv_ici.at[3 * t + j], (*chip, c)).start()
            at += n
        refs[-1][...] = jnp.zeros_like(refs[-1])

    sem_shapes = [pltpu.SemaphoreType.DMA((k * n,)) for n in counts for k in (4, 1, 3)]
    out = pl.pallas_call(
        body, name=name,
        out_shape=(*sem_shapes, *[pltpu.HBM(a.shape, a.dtype) for a in flat], jax.ShapeDtypeStruct((8, LANES), F32)),
        in_specs=(_HBM,) * (2 * n_all),
        out_specs=(_SEM,) * (3 * n_groups) + (_HBM,) * (2 * n_all) + (pl.BlockSpec(memory_space=pltpu.VMEM),),
        input_output_aliases={i: 3 * n_groups + i for i in range(2 * n_all)},
        compiler_params=pltpu.CompilerParams(has_side_effects=_EFFECT),
    )(*[pltpu.with_memory_space_constraint(a, pltpu.HBM) for a in flat])
    thru = out[3 * n_groups:-1]
    states, at = [], 0
    for gi, n in enumerate(counts):
        states.append(dict(shards=list(thru[at:at + n]), lands=list(thru[n_all + at:n_all + at + n]),
                           send=out[3 * gi], recv_sibling=out[3 * gi + 1], recv_ici=out[3 * gi + 2]))
        at += n
    return states, out[-1]


def gather_forward(states, after, *, name):
    counts = [len(s["lands"]) for s in states]
    flat = [a for s in states for a in s["lands"]]
    n_all, n_groups = sum(counts), len(states)

    def body(*refs):
        l_refs = refs[:n_all]
        recv_ici = refs[n_all:n_all + n_groups]
        fwd = refs[n_all + n_groups + 1:n_all + n_groups + 1 + 2 * n_groups]
        x, y, c = _position()
        at = 0
        for gi, n in enumerate(counts):
            fwd_send, fwd_recv = fwd[2 * gi], fwd[2 * gi + 1]
            for t in range(n):
                for j, (px, py) in enumerate(_other_chips(x, y)):
                    block = l_refs[at + t].at[4 * px + 2 * py + c]
                    _remote(block, block, fwd_send.at[3 * t + j], recv_ici[gi].at[3 * t + j], (px, py, c)).wait_recv()
                    _remote(block, block, fwd_send.at[3 * t + j], fwd_recv.at[3 * t + j], (x, y, 1 - c)).start()
            at += n
        refs[-1][...] = jnp.zeros_like(refs[-1])

    sem_shapes = [pltpu.SemaphoreType.DMA((3 * n,)) for n in counts for _ in range(2)]
    out = pl.pallas_call(
        body, name=name,
        out_shape=(*sem_shapes, *[pltpu.HBM(a.shape, a.dtype) for a in flat], jax.ShapeDtypeStruct((8, LANES), F32)),
        in_specs=(_HBM,) * n_all + (_SEM,) * n_groups + (pl.BlockSpec(memory_space=pl.ANY),),
        out_specs=(_SEM,) * (2 * n_groups) + (_HBM,) * n_all + (pl.BlockSpec(memory_space=pltpu.VMEM),),
        input_output_aliases={i: 2 * n_groups + i for i in range(n_all)},
        compiler_params=pltpu.CompilerParams(has_side_effects=_EFFECT),
    )(*flat, *[s["recv_ici"] for s in states], after)
    at = 0
    for gi, (s, n) in enumerate(zip(states, counts)):
        s.update(fwd_send=out[2 * gi], fwd_recv=out[2 * gi + 1], lands=list(out[2 * n_groups + at:2 * n_groups + at + n]))
        at += n
    return out[-1]


def gather_finish(state, after, *, name):
    n = len(state["lands"])

    def body(*refs):
        s_refs, l_refs = refs[:n], refs[n:2 * n]
        send, recv_sibling, fwd_send, fwd_recv = refs[2 * n:2 * n + 4]
        x, y, c = _position()
        me = 4 * x + 2 * y + c
        for t in range(n):
            own = l_refs[t].at[me]
            _remote(s_refs[t], own, send.at[4 * t], recv_sibling.at[t], (x, y, 1 - c)).wait_send()
            _remote(s_refs[t], l_refs[t].at[4 * x + 2 * y + 1 - c], send.at[4 * t], recv_sibling.at[t], (x, y, 1 - c)).wait_recv()
            for j, (px, py) in enumerate(_other_chips(x, y)):
                _remote(s_refs[t], own, send.at[4 * t + 1 + j], recv_sibling.at[t], (px, py, c)).wait_send()
                mine, theirs = l_refs[t].at[4 * px + 2 * py + c], l_refs[t].at[4 * px + 2 * py + 1 - c]
                _remote(mine, mine, fwd_send.at[3 * t + j], fwd_recv.at[3 * t + j], (x, y, 1 - c)).wait_send()
                _remote(theirs, theirs, fwd_send.at[3 * t + j], fwd_recv.at[3 * t + j], (x, y, 1 - c)).wait_recv()

    both = state["shards"] + state["lands"]
    out = pl.pallas_call(
        body, name=name,
        out_shape=tuple(pltpu.HBM(a.shape, a.dtype) for a in both),
        in_specs=(_HBM,) * (2 * n) + (_SEM,) * 4 + (pl.BlockSpec(memory_space=pl.ANY),), out_specs=(_HBM,) * (2 * n),
        input_output_aliases={i: i for i in range(2 * n)},
        compiler_params=pltpu.CompilerParams(has_side_effects=_EFFECT),
    )(*both, state["send"], state["recv_sibling"], state["fwd_send"], state["fwd_recv"], after)
    return list(out[n:])


def _all_peers_plan(src_ref, land_ref, k):
    x, y, c = _position()
    bits = k + 1
    peer = ((1 - x) if bits & 4 else x, (1 - y) if bits & 2 else y, (1 - c) if bits & 1 else c)
    return src_ref, land_ref.at[4 * x + 2 * y + c], peer


def _sibling_plan(src_ref, land_ref, k):
    x, y, c = _position()
    return src_ref.at[2 * k + (1 - c)], land_ref.at[k], (x, y, 1 - c)


def _chips_plan(src_ref, land_ref, j):
    x, y, c = _position()
    px, py = _other_chips(x, y)[j]
    return src_ref.at[j], land_ref.at[j], (px, py, c)


SUM_STEPS = 2


def sum_for_chips(parts, from_sibling, ck_idx, *, name):
    n = len(parts)

    def body(ck_ref, *refs):
        del ck_ref
        for t in range(n):
            refs[2 * n + t][...] = (refs[t][...] + refs[n + t][...]).astype(BF16)

    def blk(a):
        return (None, a.shape[1] // SUM_STEPS, a.shape[2])

    return pl.pallas_call(
        body, name=name,
        grid_spec=pltpu.PrefetchScalarGridSpec(
            num_scalar_prefetch=1, grid=(3, SUM_STEPS),
            in_specs=[pl.BlockSpec(blk(a), lambda j, i, ck: (2 * ck[1 + j] + ck[0], i, 0)) for a in parts]
            + [pl.BlockSpec(blk(a), lambda j, i, ck: (ck[1 + j], i, 0)) for a in from_sibling],
            out_specs=[pl.BlockSpec(blk(a), lambda j, i, ck: (j, i, 0)) for a in from_sibling]),
        out_shape=[jax.ShapeDtypeStruct((3,) + a.shape[1:], BF16) for a in from_sibling], compiler_params=_cp(),
    )(ck_idx, *parts, *from_sibling)


def sum_final(parts, from_sibling, from_chips, kc_idx, *, name):
    n = len(parts)

    def body(kc_ref, *refs):
        del kc_ref
        for t in range(n):
            p, s, a, b, d = (refs[j * n + t] for j in range(5))
            refs[5 * n + t][...] = (((p[...] + s[...]) + a[...].astype(F32)) + b[...].astype(F32)) + d[...].astype(F32)

    def blk(a):
        return (None, a.shape[1] // SUM_STEPS, a.shape[2])

    def chip_specs(j):
        return [pl.BlockSpec(blk(a), lambda i, kc: (j, i, 0)) for a in from_chips]

    return pl.pallas_call(
        body, name=name,
        grid_spec=pltpu.PrefetchScalarGridSpec(
            num_scalar_prefetch=1, grid=(SUM_STEPS,),
            in_specs=[pl.BlockSpec(blk(a), lambda i, kc: (2 * kc[0] + kc[1], i, 0)) for a in parts]
            + [pl.BlockSpec(blk(a), lambda i, kc: (kc[0], i, 0)) for a in from_sibling]
            + chip_specs(0) + chip_specs(1) + chip_specs(2),
            out_specs=[pl.BlockSpec(blk(a)[1:], lambda i, kc: (i, 0)) for a in parts]),
        out_shape=[jax.ShapeDtypeStruct(a.shape[1:], F32) for a in parts], compiler_params=_cp(),
    )(kc_idx, *parts, *from_sibling, *from_chips, *from_chips, *from_chips)


def sum_devices(gathered):
    n, r, c_ = gathered.shape

    def body(g_ref, o_ref):
        acc = g_ref[0]
        for k in range(1, n):
            acc = acc + g_ref[k]
        o_ref[...] = acc

    return pl.pallas_call(
        body, name="sum_devices", out_shape=jax.ShapeDtypeStruct((r, c_), F32), compiler_params=_cp(),
    )(gathered)


BIG = (
    ("w_in", IN_COLS, True), ("w_out", D_MODEL, False), ("wq_x", D_MODEL, False), ("wkv_x", 2 * D_MODEL, True),
    ("wo_x", D_MODEL, False), ("w_gate_up", 2 * D_FF, True), ("w_down", D_FF, False),
)
SHARD_ROWS = sum(rows // N_DEV for _, rows, _ in BIG)

SMALL = ("norm_mix_g", "q_norm_g", "k_norm_g", "sinks", "conv_b", "conv_ln_g", "conv_ln_b",
         "norm_x_g", "norm_mem_g", "xq_norm_g", "xk_norm_g", "norm_ffn_g")


def _pack_rows(vectors, width=LANES, row_multiple=8):
    flat = jnp.concatenate([v.reshape(-1) for v in vectors])
    per = width * row_multiple
    padded = -(-flat.shape[0] // per) * per
    return jnp.pad(flat, (0, padded - flat.shape[0])).reshape(-1, width)


def _unpack_rows(packed, shapes):
    flat = packed.reshape(-1)
    out, at = [], 0
    for s in shapes:
        n = 1
        for dim in s:
            n *= dim
        out.append(flat[at:at + n].reshape(s))
        at += n
    return out


WEIGHT_GROUPS = {"in": ("w_in",), "mid": ("w_out", "wq_x", "wkv_x", "wo_x"), "ffn": ("w_gate_up", "w_down")}


def _layer_fwd(x0, mem, weights_of, s, reached):
    w = dict(weights_of("in", x0))
    h0, u = norm_proj(x0, s["norm_mix_g"], w["w_in"])
    mixed = swa_fwd(u, s["q_norm_g"], s["k_norm_g"], s["sinks"])
    reached("attn", mixed)
    mixed, conv_y = conv_fwd(u, mixed, s["conv_w"], s["conv_b"], s["conv_ln_g"], s["conv_ln_b"])
    w.update(weights_of("mid", conv_y))
    memn = rms_fwd(mem, s["norm_mem_g"])
    kv = mm(memn, w["wkv_x"], trans_b=True, out_dtype=F32, name="mm_kv")
    x1, h1, qx, o, x2, h2 = mid_fwd(mixed, x0, w["w_out"], s["norm_x_g"], w["wq_x"], kv, s["xq_norm_g"], s["xk_norm_g"],
                                    w["wo_x"], s["norm_ffn_g"])
    reached("mid", x2)
    w.update(weights_of("ffn", x2))
    gu, a, x3 = ffn_fwd(h2, x2, w["w_gate_up"], w["w_down"])
    saved = dict(x0=x0, h0=h0, u=u, conv_y=conv_y, mixed=mixed, x1=x1, h1=h1, qx=qx, memn=memn, kv=kv, o=o, x2=x2, h2=h2,
                 gu=gu, a=a)
    return x3, saved, w


def _ordered_after(a, token):
    return a if token is None else a + token[0, 0]


def _layer_bwd(dx3, mem, w, s, sv, token, stage_done):
    gs = {}
    dgu, dx2, dg = ffn_bwd(dx3, sv["gu"], sv["x2"], _ordered_after(s["norm_ffn_g"], token), w["w_down"], w["w_gate_up"])
    gs["norm_ffn_g"] = dg
    gb = {"w_down": mm_tn(sv["a"], dx3, name="mm_dw_down")}
    gb["w_gate_up"] = mm_tn(dgu, sv["h2"], tk=dgu.shape[0], name="mm_dw_gate_up")
    token = stage_done("ffn", gb, gb["w_gate_up"])

    gb = {}
    dq, dx1, dmixed, dkv, dqg, dkg, dg = mid_bwd(dx2, sv["qx"], sv["kv"], s["xq_norm_g"], s["xk_norm_g"], sv["x1"],
                                                 _ordered_after(s["norm_x_g"], token), w["wo_x"], w["wq_x"], w["w_out"])
    gs["xq_norm_g"], gs["xk_norm_g"], gs["norm_x_g"] = dqg, dkg, dg
    gb["wo_x"] = mm_tn(sv["o"], dx2, name="mm_dwo")
    gb["wq_x"] = mm_tn(sv["h1"], dq, name="mm_dwq")
    dmemn = mm(dkv, w["wkv_x"], trans_b=False, out_dtype=F32, name="mm_dmemn")
    gb["wkv_x"] = mm_tn(dkv, sv["memn"], name="mm_dwkv")
    _, dg = rms_bwd(dmemn, mem, s["norm_mem_g"], None)
    gs["norm_mem_g"] = dg
    gb["w_out"] = mm_tn(sv["mixed"], dx1, name="mm_dw_out")
    token = stage_done("mid", gb, gb["w_out"])

    du, dqg, dkg, dsinks = swa_bwd(sv["u"], dmixed, _ordered_after(s["q_norm_g"], token), s["k_norm_g"], s["sinks"])
    gs["q_norm_g"], gs["k_norm_g"], gs["sinks"] = dqg[0, :HEAD_DIM], dkg[0, :HEAD_DIM], dsinks[0, :N_Q_HEADS]
    token = stage_done("attn", {}, dqg)
    du, dconv_w, dvec = conv_bwd(sv["u"], sv["conv_y"], dmixed, du, s["conv_w"], _ordered_after(s["conv_ln_g"], token),
                                 s["conv_ln_b"])
    gs["conv_w"] = dconv_w[:CONV_K]
    gs["conv_b"], gs["conv_ln_g"], gs["conv_ln_b"] = dvec[0], dvec[1], dvec[2]
    dw_in = mm_tn(du, sv["h0"], name="mm_dw_in")
    token = stage_done("in", {"w_in": dw_in}, dw_in)
    dx0, dg = in_bwd(du, w["w_in"], sv["x0"], _ordered_after(s["norm_mix_g"], token), dx1)
    gs["norm_mix_g"] = dg
    token = stage_done("mix", {}, dx0)
    return dx0, gs, token


def _local_step(x, mem, target, weights_of, reached, smalls, stage_done):
    saved, weights = [], []
    h = x
    for l in range(DEPTH):
        h, sv, w = _layer_fwd(h, mem, functools.partial(weights_of, l), smalls[l], functools.partial(reached, l))
        saved.append(sv)
        weights.append(w)
    dx, loss_part = loss_head(h, target)
    gss, token = [None] * DEPTH, None
    for l in reversed(range(DEPTH)):
        dx, gss[l], token = _layer_bwd(dx, mem, weights[l], smalls[l], saved[l], token,
                                       functools.partial(stage_done, l))
    return loss_part[0, 0], dx, gss


def kernel(x, mem, norm_mix_g, w_in, q_norm_g, k_norm_g, sinks, conv_w, conv_b, conv_ln_g, conv_ln_b, w_out, norm_x_g, norm_mem_g, wq_x, wkv_x, xq_norm_g, xk_norm_g, wo_x, norm_ffn_g, w_gate_up, w_down, loss_target, m_norm_mix_g, m_w_in, m_q_norm_g, m_k_norm_g, m_sinks, m_conv_w, m_conv_b, m_conv_ln_g, m_conv_ln_b, m_w_out, m_norm_x_g, m_norm_mem_g, m_wq_x, m_wkv_x, m_xq_norm_g, m_xk_norm_g, m_wo_x, m_norm_ffn_g, m_w_gate_up, m_w_down, v_norm_mix_g, v_w_in, v_q_norm_g, v_k_norm_g, v_sinks, v_conv_w, v_conv_b, v_conv_ln_g, v_conv_ln_b, v_w_out, v_norm_x_g, v_norm_mem_g, v_wq_x, v_wkv_x, v_xq_norm_g, v_xk_norm_g, v_wo_x, v_norm_ffn_g, v_w_gate_up, v_w_down):
    P = dict(norm_mix_g=norm_mix_g, w_in=w_in, q_norm_g=q_norm_g, k_norm_g=k_norm_g, sinks=sinks, conv_w=conv_w, conv_b=conv_b,
             conv_ln_g=conv_ln_g, conv_ln_b=conv_ln_b, w_out=w_out, norm_x_g=norm_x_g, norm_mem_g=norm_mem_g, wq_x=wq_x,
             wkv_x=wkv_x, xq_norm_g=xq_norm_g, xk_norm_g=xk_norm_g, wo_x=wo_x, norm_ffn_g=norm_ffn_g, w_gate_up=w_gate_up,
             w_down=w_down)
    M = dict(norm_mix_g=m_norm_mix_g, w_in=m_w_in, q_norm_g=m_q_norm_g, k_norm_g=m_k_norm_g, sinks=m_sinks, conv_w=m_conv_w,
             conv_b=m_conv_b, conv_ln_g=m_conv_ln_g, conv_ln_b=m_conv_ln_b, w_out=m_w_out, norm_x_g=m_norm_x_g,
             norm_mem_g=m_norm_mem_g, wq_x=m_wq_x, wkv_x=m_wkv_x, xq_norm_g=m_xq_norm_g, xk_norm_g=m_xk_norm_g, wo_x=m_wo_x,
             norm_ffn_g=m_norm_ffn_g, w_gate_up=m_w_gate_up, w_down=m_w_down)
    V = dict(norm_mix_g=v_norm_mix_g, w_in=v_w_in, q_norm_g=v_q_norm_g, k_norm_g=v_k_norm_g, sinks=v_sinks, conv_w=v_conv_w,
             conv_b=v_conv_b, conv_ln_g=v_conv_ln_g, conv_ln_b=v_conv_ln_b, w_out=v_w_out, norm_x_g=v_norm_x_g,
             norm_mem_g=v_norm_mem_g, wq_x=v_wq_x, wkv_x=v_wkv_x, xq_norm_g=v_xq_norm_g, xk_norm_g=v_xk_norm_g, wo_x=v_wo_x,
             norm_ffn_g=v_norm_ffn_g, w_gate_up=v_w_gate_up, w_down=v_w_down)
    order = ["norm_mix_g", "w_in", "q_norm_g", "k_norm_g", "sinks", "conv_w", "conv_b", "conv_ln_g", "conv_ln_b", "w_out",
             "norm_x_g", "norm_mem_g", "wq_x", "wkv_x", "xq_norm_g", "xk_norm_g", "wo_x", "norm_ffn_g", "w_gate_up", "w_down"]
    xi, yi, ci = _position()
    dev = 4 * xi + 2 * yi + ci
    x2d, mem2d, tgt2d = x[0], mem[0], loss_target[0]

    def travelling(name, l, transposed):
        a = P[name][l]
        return (a.T if transposed else a).astype(BF16)

    rows_of = {n: rows for n, rows, _ in BIG}
    transposed_of = {n: tr for n, _, tr in BIG}

    def whole(names, gathered):
        return {n: g.reshape(rows_of[n], D_MODEL) for n, g in zip(names, gathered)}

    cw = jnp.pad(conv_w.reshape(DEPTH * CONV_K, CONV_CH // N_DEV), ((0, 2), (0, LANES - CONV_CH // N_DEV)))
    (w_in0, cw_all), token0 = all_gather_many([travelling("w_in", 0, True), cw], name="ag_w_in0_conv_w")
    travel_order = [(0, "mid"), (0, "ffn"), (1, "in"), (1, "mid"), (1, "ffn")]
    travel_groups = []
    for l, group in travel_order:
        shards = [_ordered_after(travelling(n, l, transposed_of[n]), token0.astype(BF16)) for n in WEIGHT_GROUPS[group]]
        lands = [lax.dynamic_update_slice(lax.empty((N_DEV,) + s.shape, BF16), s[None], (dev, 0, 0)) for s in shards]
        travel_groups.append((shards, lands))
    travel_states, travel_token = gather_start(travel_groups, name="ag_weights_start")
    travelling_state = dict(zip(travel_order, travel_states))
    forward_at = {(0, "attn"): [(0, "mid")], (0, "mid"): [(0, "ffn"), (1, "in")], (1, "attn"): [(1, "mid"), (1, "ffn")]}

    def reached(l, stage, marker):
        keys = forward_at.get((l, stage))
        if keys:
            gather_forward([travelling_state[k] for k in keys], marker,
                           name="ag_weights_forward_" + "_".join(f"{g}{ll}" for ll, g in keys))

    def weights_of(l, group, marker):
        if (l, group) == (0, "in"):
            return whole(WEIGHT_GROUPS[group], [w_in0])
        gathered = gather_finish(travelling_state[(l, group)], marker, name=f"ag_weights_finish_{group}{l}")
        return whole(WEIGHT_GROUPS[group], gathered)

    cw_full = cw_all[:, :DEPTH * CONV_K, :CONV_CH // N_DEV].reshape(N_DEV, DEPTH, CONV_K, CONV_CH // N_DEV)
    cw_full = jnp.transpose(cw_full, (1, 2, 0, 3)).reshape(DEPTH, CONV_K, CONV_CH)
    smalls = []
    for l in range(DEPTH):
        sl = {n: P[n][l] if n == "sinks" else P[n][l:l + 1] for n in SMALL}
        sl["conv_w"] = jnp.pad(cw_full[l], ((0, CONV_HALO - CONV_K), (0, 0)))
        smalls.append(sl)
    smalls[0]["norm_mix_g"] = _ordered_after(smalls[0]["norm_mix_g"], travel_token)

    ck_idx = jnp.stack([ci] + [2 * px + py for px, py in _other_chips(xi, yi)]).astype(jnp.int32)
    kc_idx = jnp.stack([2 * xi + yi, ci]).astype(jnp.int32)
    got, flight, reduced = {}, {}, {}

    def as_parts(gb):
        keys = sorted(gb)
        return keys, [gb[k].reshape(N_DEV, rows_of[k[1]] // N_DEV, D_MODEL) for k in keys]

    def lands_like(parts, blocks, dtype):
        return [lax.empty((blocks,) + p.shape[1:], dtype) for p in parts]

    def to_sibling(group, gb):
        keys, parts = as_parts(gb)
        flight[group] = (keys, split_start(parts, lands_like(parts, 4, F32), _sibling_plan, 4,
                                           name=f"rs_sibling_{group}_start"))
        return flight[group][1][4]

    def to_chips(group, marker):
        keys, started = flight[group]
        parts, from_sibling = split_wait(started, marker, _sibling_plan, 4, name=f"rs_sibling_{group}_wait")
        chip_sums = sum_for_chips(parts, from_sibling, ck_idx, name=f"rs_sum_for_chips_{group}")
        started = split_start(chip_sums, lands_like(parts, 3, BF16), _chips_plan, 3, name=f"rs_chips_{group}_start")
        flight[group] = (keys, parts, from_sibling, started)
        return started[4]

    def finish(group, marker):
        keys, parts, from_sibling, started = flight[group]
        _, from_chips = split_wait(started, marker, _chips_plan, 3, name=f"rs_chips_{group}_wait")
        reduced.update(zip(keys, sum_final(parts, from_sibling, from_chips, kc_idx, name=f"rs_sum_final_{group}")))

    def stage_done(l, stage, gb, marker):
        gb = {(l, n): g for n, g in gb.items()}
        if l == 1:
            got.update(gb)
            return to_sibling("l1", got) if stage == "mix" else None
        if stage == "ffn":
            return to_chips("l1", marker) + to_sibling("ffn", gb)
        if stage == "mid":
            return to_chips("ffn", marker) + to_sibling("mid", gb)
        if stage == "attn":
            return to_chips("mid", marker)
        if stage == "in":
            return to_sibling("in", gb)
        to_chips("in", marker)
        for group in ("l1", "ffn", "mid"):
            finish(group, marker)
        return None

    loss_part, grad_x, gss = _local_step(x2d, mem2d, tgt2d, weights_of, reached, smalls, stage_done)
    loss = lax.psum(loss_part, ("x", "y", "c"))

    small_names = SMALL + ("conv_w",)
    small_shapes = [(DEPTH,) + ((CONV_K, CONV_CH) if n == "conv_w" else P[n].shape[1:]) for n in small_names]
    small_parts = _pack_rows([jnp.stack([gss[l][n].reshape(sh[1:]) for l in range(DEPTH)])
                              for n, sh in zip(small_names, small_shapes)])
    small_land = lax.dynamic_update_slice(lax.empty((N_DEV,) + small_parts.shape, F32), small_parts[None], (dev, 0, 0))
    small_flight = split_start([small_parts], [small_land], _all_peers_plan, N_DEV - 1, name="ag_small_grads_start")

    grads, delta, new_m, new_v = {}, {}, {}, {}

    def update(n, transposed):
        shape = P[n].shape
        two_d = lambda a: a.reshape(shape[0] * shape[1], shape[2])
        grads[n] = jnp.stack([reduced[(l, n)].T if transposed else reduced[(l, n)] for l in range(DEPTH)])
        d_, m_, v_ = adamw(two_d(P[n]), two_d(grads[n]), two_d(M[n]), two_d(V[n]), name="adamw_" + n)
        delta[n], new_m[n], new_v[n] = d_.reshape(shape), m_.reshape(shape), v_.reshape(shape)

    for n, _, transposed in BIG:
        if n != "w_in":
            update(n, transposed)
    finish("in", delta["w_down"])
    update("w_in", True)
    small_all = split_wait(small_flight, delta["w_in"], _all_peers_plan, N_DEV - 1, name="ag_small_grads_wait")[1][0]
    for n, g in zip(small_names, _unpack_rows(sum_devices(small_all), small_shapes)):
        if n == "conv_w":
            g = lax.dynamic_slice_in_dim(g, dev * (CONV_CH // N_DEV), CONV_CH // N_DEV, axis=2)
        grads[n] = g
    shapes = [P[n].shape for n in small_names]
    d_, m_, v_ = adamw(_pack_rows([P[n] for n in small_names]), _pack_rows([grads[n] for n in small_names]),
                       _pack_rows([M[n] for n in small_names]), _pack_rows([V[n] for n in small_names]), name="adamw_small")
    for n, dd, mm_, vv in zip(small_names, _unpack_rows(d_, shapes), _unpack_rows(m_, shapes), _unpack_rows(v_, shapes)):
        delta[n], new_m[n], new_v[n] = dd, mm_, vv

    return (loss, grad_x[None], *[grads[n] for n in order], *[delta[n] for n in order],
            *[new_m[n] for n in order], *[new_v[n] for n in order])
```

```python
import functools

import jax
import jax.numpy as jnp
import numpy as np
from jax import lax
from jax.experimental import pallas as pl
from jax.experimental.pallas import tpu as pltpu

F32 = jnp.float32
BF16 = jnp.bfloat16

D_MODEL = 1024
HEAD_DIM = 64
N_Q_HEADS = 8
N_KV_HEADS = 2
GROUP = N_Q_HEADS // N_KV_HEADS
ATTN_WIDTH = N_Q_HEADS * HEAD_DIM
KV_WIDTH = N_KV_HEADS * HEAD_DIM
QKV_WIDTH = ATTN_WIDTH + 2 * KV_WIDTH
CONV_CH = 512
IN_COLS = QKV_WIDTH + 2 * CONV_CH
CONV_K = 31
CONV_HALO = 32
BLOCK = 128
N_X_HEADS = 4
X_HEAD_DIM = 256
D_FF = 2816
EPS = 1e-6
NEG = -1e30
DEPTH = 2
N_DEV = 8

ADAM_LR = 0.001
ADAM_B1 = 0.9
ADAM_B2 = 0.999
ADAM_EPS = 1e-08
ADAM_WD = 0.01
ADAM_STEP = 10

V7X_VMEM_LIMIT = 56 * 1024 * 1024
LANES = 128

MESH = pl.DeviceIdType.MESH


def _cp(**kw):
    return pltpu.CompilerParams(vmem_limit_bytes=V7X_VMEM_LIMIT, **kw)


def _dot(a, b, dims):
    return lax.dot_general(a.astype(BF16), b.astype(BF16), (dims, ((), ())), preferred_element_type=F32)


def _dot_nn(a, b):
    return _dot(a, b, ((1,), (0,)))


def _dot_nt(a, b):
    return _dot(a, b, ((1,), (1,)))


def _dot_tn(a, b):
    return _dot(a, b, ((0,), (0,)))


def _sigmoid(x):
    return jax.nn.sigmoid(x)


def _rms(x):
    r = lax.rsqrt(jnp.mean(x * x, axis=-1, keepdims=True) + EPS)
    return x * r, r


def _rms_bwd(dy, xhat, r, g):
    dxh = dy * g
    return r * (dxh - xhat * jnp.mean(dxh * xhat, axis=-1, keepdims=True))


def rms_fwd(x, g, *, tm=512):
    m, d = x.shape
    tm = min(tm, m)

    def body(x_ref, g_ref, o_ref):
        xh, _ = _rms(x_ref[...])
        o_ref[...] = (xh * g_ref[...]).astype(o_ref.dtype)

    return pl.pallas_call(
        body, name="rms_fwd", grid=(m // tm,),
        in_specs=[pl.BlockSpec((tm, d), lambda i: (i, 0)), pl.BlockSpec((1, d), lambda i: (0, 0))],
        out_specs=pl.BlockSpec((tm, d), lambda i: (i, 0)),
        out_shape=jax.ShapeDtypeStruct((m, d), BF16), compiler_params=_cp(),
    )(x, g.reshape(1, d))


def rms_gain_bwd(dh, x, *, tm=512):
    m, d = x.shape
    tm = min(tm, m)

    def body(dh_ref, x_ref, dg_ref):
        @pl.when(pl.program_id(0) == 0)
        def _():
            dg_ref[...] = jnp.zeros_like(dg_ref)

        dg_ref[...] += jnp.sum(dh_ref[...] * _rms(x_ref[...])[0], axis=0, keepdims=True)

    row = pl.BlockSpec((tm, d), lambda i: (i, 0))
    return pl.pallas_call(
        body, name="rms_gain_bwd", grid=(m // tm,), in_specs=[row, row],
        out_specs=pl.BlockSpec((1, d), lambda i: (0, 0)),
        out_shape=jax.ShapeDtypeStruct((1, d), F32), compiler_params=_cp(),
    )(dh, x)


def loss_head(y, target, *, tm=512):
    m, d = y.shape

    def body(y_ref, t_ref, dy_ref, l_ref):
        err = y_ref[...] - t_ref[...]
        dy_ref[...] = err * (1.0 / d)

        @pl.when(pl.program_id(0) == 0)
        def _():
            l_ref[...] = jnp.zeros_like(l_ref)

        part = jnp.sum(jnp.sum(err * err, axis=-1, keepdims=True), axis=0, keepdims=True)
        l_ref[...] += jnp.broadcast_to(part * (0.5 / d), l_ref.shape)

    row = pl.BlockSpec((tm, d), lambda i: (i, 0))
    return pl.pallas_call(
        body, name="loss_head", grid=(m // tm,),
        in_specs=[row, row],
        out_specs=[row, pl.BlockSpec((1, LANES), lambda i: (0, 0))],
        out_shape=[jax.ShapeDtypeStruct((m, d), F32), jax.ShapeDtypeStruct((1, LANES), F32)],
        compiler_params=_cp(),
    )(y, target)


def _tile(n, cap):
    if n <= cap:
        return n
    best = None
    for t in range(LANES, cap + 1, LANES):
        if n % t == 0:
            best = t
    assert best is not None, (n, cap)
    return best


def mm(a, b, *, trans_b, out_dtype, tm=1024, tn_cap=1536, name):
    m, k = a.shape
    n = b.shape[0] if trans_b else b.shape[1]
    assert (b.shape[1] if trans_b else b.shape[0]) == k
    tm = min(tm, m)
    tn = _tile(n, tn_cap)

    def body(a_ref, b_ref, o_ref):
        acc = _dot_nt(a_ref[...], b_ref[...]) if trans_b else _dot_nn(a_ref[...], b_ref[...])
        o_ref[...] = acc.astype(o_ref.dtype)

    b_spec = pl.BlockSpec((tn, k), lambda i, j: (j, 0)) if trans_b else pl.BlockSpec((k, tn), lambda i, j: (0, j))
    return pl.pallas_call(
        body, name=name, grid=(m // tm, n // tn),
        in_specs=[pl.BlockSpec((tm, k), lambda i, j: (i, 0)), b_spec],
        out_specs=pl.BlockSpec((tm, tn), lambda i, j: (i, j)),
        out_shape=jax.ShapeDtypeStruct((m, n), out_dtype), compiler_params=_cp(),
    )(a, b)


def mm_tn(a, b, *, name, ta_cap=1536, tb_cap=1024, tk=2048):
    m, ka = a.shape
    nb = b.shape[1]
    assert b.shape[0] == m
    tk = min(tk, m)
    ta = _tile(ka, ta_cap)
    tb = _tile(nb, tb_cap)

    def body(a_ref, b_ref, o_ref):
        @pl.when(pl.program_id(2) == 0)
        def _():
            o_ref[...] = jnp.zeros_like(o_ref)

        o_ref[...] += _dot_tn(a_ref[...], b_ref[...])

    return pl.pallas_call(
        body, name=name, grid=(ka // ta, nb // tb, m // tk),
        in_specs=[pl.BlockSpec((tk, ta), lambda i, j, kk: (kk, i)), pl.BlockSpec((tk, tb), lambda i, j, kk: (kk, j))],
        out_specs=pl.BlockSpec((ta, tb), lambda i, j, kk: (i, j)),
        out_shape=jax.ShapeDtypeStruct((ka, nb), F32), compiler_params=_cp(),
    )(a, b)


def _whole(shape):
    return pl.BlockSpec(shape, lambda i: (0,) * len(shape), pipeline_mode=pl.Buffered(1))


def _rows(tm, n):
    return pl.BlockSpec((tm, n), lambda i: (i, 0))


def _vec(n):
    return pl.BlockSpec((1, n), lambda i: (0, 0))


def _chunks(n, cap=1408):
    size = _tile(n, cap)
    return [(s, size) for s in range(0, n, size)]


def _zero_at_first_step(*refs):
    @pl.when(pl.program_id(0) == 0)
    def _():
        for r in refs:
            r[...] = jnp.zeros_like(r)


def norm_proj(x, g, wt, *, tm=512):
    m, d = x.shape
    n = wt.shape[0]

    def body(x_ref, g_ref, wt_ref, h_ref, u_ref):
        h = (_rms(x_ref[...])[0] * g_ref[...]).astype(BF16)
        h_ref[...] = h
        for s, sz in _chunks(n):
            u_ref[:, s:s + sz] = _dot_nt(h, wt_ref[s:s + sz, :])

    return pl.pallas_call(
        body, name="norm_proj", grid=(m // tm,),
        in_specs=[_rows(tm, d), _vec(d), _whole((n, d))],
        out_specs=[_rows(tm, d), _rows(tm, n)],
        out_shape=[jax.ShapeDtypeStruct((m, d), BF16), jax.ShapeDtypeStruct((m, n), F32)],
        compiler_params=_cp(),
    )(x, g.reshape(1, d), wt)


def _xattn_heads(q_ref, kv_ref, qg_v, kg_v, d):
    out = []
    for h in range(N_X_HEADS):
        cols = slice(h * X_HEAD_DIM, (h + 1) * X_HEAD_DIM)
        qh, rq = _rms(q_ref[:, cols])
        qn = qh * qg_v
        kn = _rms(kv_ref[:, cols])[0] * kg_v
        v = kv_ref[:, d + h * X_HEAD_DIM:d + (h + 1) * X_HEAD_DIM]
        out.append((qh, rq, qn, kn, v, _xattn_probs(qn, kn)))
    return out


def mid_fwd(mixed, x0, w_out, g_x, wq, kv, xqg, xkg, wo, g_f, *, tm=512):
    m, d = x0.shape
    n_mem = kv.shape[0]

    def body(mixed_ref, x0_ref, w_out_ref, g_x_ref, wq_ref, kv_ref, xqg_ref, xkg_ref, wo_ref, g_f_ref,
             x1_ref, h1_ref, qx_ref, o_ref, x2_ref, h2_ref):
        x1 = x0_ref[...] + _dot_nn(mixed_ref[...], w_out_ref[...])
        x1_ref[...] = x1
        h1 = (_rms(x1)[0] * g_x_ref[...]).astype(BF16)
        h1_ref[...] = h1
        qx_ref[...] = _dot_nn(h1, wq_ref[...])
        for h, (_, _, _, _, v, p) in enumerate(_xattn_heads(qx_ref, kv_ref, xqg_ref[...], xkg_ref[...], d)):
            o_ref[:, h * X_HEAD_DIM:(h + 1) * X_HEAD_DIM] = _dot_nn(p, v).astype(o_ref.dtype)
        x2 = x1 + _dot_nn(o_ref[...], wo_ref[...])
        x2_ref[...] = x2
        h2_ref[...] = (_rms(x2)[0] * g_f_ref[...]).astype(BF16)

    sq = _whole((d, d))
    f32_rows, bf_rows = jax.ShapeDtypeStruct((m, d), F32), jax.ShapeDtypeStruct((m, d), BF16)
    return pl.pallas_call(
        body, name="mid_fwd", grid=(m // tm,),
        in_specs=[_rows(tm, d), _rows(tm, d), sq, _vec(d), sq, _whole((n_mem, 2 * d)), _vec(X_HEAD_DIM), _vec(X_HEAD_DIM),
                  sq, _vec(d)],
        out_specs=[_rows(tm, d)] * 6,
        out_shape=[f32_rows, bf_rows, f32_rows, bf_rows, f32_rows, bf_rows],
        compiler_params=_cp(),
    )(mixed, x0, w_out, g_x.reshape(1, d), wq, kv, xqg.reshape(1, X_HEAD_DIM), xkg.reshape(1, X_HEAD_DIM), wo,
      g_f.reshape(1, d))


def ffn_fwd(h2, x2, wt_gu, w_down, *, tm=256):
    m, d = x2.shape
    f = w_down.shape[0]

    def body(h2_ref, x2_ref, wt_gu_ref, w_down_ref, gu_ref, a_ref, x3_ref):
        h = h2_ref[...]
        for s, sz in _chunks(2 * f):
            gu_ref[:, s:s + sz] = _dot_nt(h, wt_gu_ref[s:s + sz, :])
        for s, sz in _chunks(f):
            g = gu_ref[:, s:s + sz]
            a_ref[:, s:s + sz] = (g * _sigmoid(g) * gu_ref[:, f + s:f + s + sz]).astype(a_ref.dtype)
        x3_ref[...] = x2_ref[...] + _dot_nn(a_ref[...], w_down_ref[...])

    return pl.pallas_call(
        body, name="ffn_fwd", grid=(m // tm,),
        in_specs=[_rows(tm, d), _rows(tm, d), _whole((2 * f, d)), _whole((f, d))],
        out_specs=[_rows(tm, 2 * f), _rows(tm, f), _rows(tm, d)],
        out_shape=[jax.ShapeDtypeStruct((m, 2 * f), F32), jax.ShapeDtypeStruct((m, f), BF16),
                   jax.ShapeDtypeStruct((m, d), F32)],
        compiler_params=_cp(),
    )(h2, x2, wt_gu, w_down)


def ffn_bwd(dx3, gu, x2, g_f, w_down, wt_gu, *, tm=256):
    m, d = x2.shape
    f = w_down.shape[0]

    def body(dx3_ref, gu_ref, x2_ref, g_ref, w_down_ref, wt_gu_ref, dgu_ref, dx2_ref, dg_ref):
        _zero_at_first_step(dg_ref)
        dx3 = dx3_ref[...]
        dx3_b = dx3.astype(BF16)
        for s, sz in _chunks(f):
            da = _dot_nt(dx3_b, w_down_ref[s:s + sz, :])
            g = gu_ref[:, s:s + sz]
            u = gu_ref[:, f + s:f + s + sz]
            sg = _sigmoid(g)
            dgu_ref[:, s:s + sz] = (da * u * (sg * (1.0 + g * (1.0 - sg)))).astype(dgu_ref.dtype)
            dgu_ref[:, f + s:f + s + sz] = (da * (g * sg)).astype(dgu_ref.dtype)
        dh2 = _dot_nn(dgu_ref[...], wt_gu_ref[...])
        xh, r = _rms(x2_ref[...])
        dg_ref[...] += jnp.sum(dh2 * xh, axis=0, keepdims=True)
        dx2_ref[...] = dx3 + _rms_bwd(dh2, xh, r, g_ref[...])

    return pl.pallas_call(
        body, name="ffn_bwd", grid=(m // tm,),
        in_specs=[_rows(tm, d), _rows(tm, 2 * f), _rows(tm, d), _vec(d), _whole((f, d)), _whole((2 * f, d))],
        out_specs=[_rows(tm, 2 * f), _rows(tm, d), _vec(d)],
        out_shape=[jax.ShapeDtypeStruct((m, 2 * f), BF16), jax.ShapeDtypeStruct((m, d), F32),
                   jax.ShapeDtypeStruct((1, d), F32)],
        compiler_params=_cp(),
    )(dx3, gu, x2, g_f.reshape(1, d), w_down, wt_gu)


def mid_bwd(dx2, qx, kv, xqg, xkg, x1, g_x, wo, wq, w_out, *, tm=512):
    m, d = x1.shape
    n_mem = kv.shape[0]
    nt = m // tm

    def body(dx2_ref, qx_ref, kv_ref, xqg_ref, xkg_ref, x1_ref, g_x_ref, wo_ref, wq_ref, w_out_ref,
             dq_ref, dx1_ref, dmixed_ref, dkv_ref, dqg_ref, dkg_ref, dg_ref):
        i = pl.program_id(0)
        _zero_at_first_step(dkv_ref, dqg_ref, dkg_ref, dg_ref)
        qg_v, kg_v = xqg_ref[...], xkg_ref[...]
        dx2 = dx2_ref[...]
        do = _dot_nt(dx2, wo_ref[...])
        dqg_acc = jnp.zeros((1, X_HEAD_DIM), F32)
        for h, (qh, rq, qn, kn, v, p) in enumerate(_xattn_heads(qx_ref, kv_ref, qg_v, kg_v, d)):
            cols = slice(h * X_HEAD_DIM, (h + 1) * X_HEAD_DIM)
            vcols = slice(d + h * X_HEAD_DIM, d + (h + 1) * X_HEAD_DIM)
            do_h = do[:, cols]
            dp = _dot_nt(do_h, v)
            ds = p * (dp - jnp.sum(p * dp, axis=-1, keepdims=True))
            dkv_ref[:, vcols] += _dot_tn(p, do_h)
            dqn = _dot_nn(ds, kn) * (X_HEAD_DIM ** -0.5)
            dkv_ref[:, cols] += _dot_tn(ds, qn) * (X_HEAD_DIM ** -0.5)
            dqg_acc = dqg_acc + jnp.sum(dqn * qh, axis=0, keepdims=True)
            dq_ref[:, cols] = _rms_bwd(dqn, qh, rq, qg_v).astype(dq_ref.dtype)
        dqg_ref[...] += dqg_acc
        dh1 = _dot_nt(dq_ref[...], wq_ref[...])
        xh, r = _rms(x1_ref[...])
        dg_ref[...] += jnp.sum(dh1 * xh, axis=0, keepdims=True)
        dx1 = dx2 + _rms_bwd(dh1, xh, r, g_x_ref[...])
        dx1_ref[...] = dx1
        dmixed_ref[...] = _dot_nt(dx1, w_out_ref[...])

        @pl.when(i == nt - 1)
        def _():
            dkg_acc = jnp.zeros((1, X_HEAD_DIM), F32)
            for h in range(N_X_HEADS):
                cols = slice(h * X_HEAD_DIM, (h + 1) * X_HEAD_DIM)
                kh, rk = _rms(kv_ref[:, cols])
                dkn = dkv_ref[:, cols]
                dkg_acc = dkg_acc + jnp.sum(dkn * kh, axis=0, keepdims=True)
                dkv_ref[:, cols] = _rms_bwd(dkn, kh, rk, kg_v)
            dkg_ref[...] = dkg_acc

    sq = _whole((d, d))
    full = pl.BlockSpec((n_mem, 2 * d), lambda i: (0, 0))
    return pl.pallas_call(
        body, name="mid_bwd", grid=(nt,),
        in_specs=[_rows(tm, d), _rows(tm, d), _whole((n_mem, 2 * d)), _vec(X_HEAD_DIM), _vec(X_HEAD_DIM), _rows(tm, d),
                  _vec(d), sq, sq, sq],
        out_specs=[_rows(tm, d), _rows(tm, d), _rows(tm, d), full, _vec(X_HEAD_DIM), _vec(X_HEAD_DIM), _vec(d)],
        out_shape=[jax.ShapeDtypeStruct((m, d), BF16), jax.ShapeDtypeStruct((m, d), F32), jax.ShapeDtypeStruct((m, d), F32),
                   jax.ShapeDtypeStruct((n_mem, 2 * d), F32), jax.ShapeDtypeStruct((1, X_HEAD_DIM), F32),
                   jax.ShapeDtypeStruct((1, X_HEAD_DIM), F32), jax.ShapeDtypeStruct((1, d), F32)],
        compiler_params=_cp(),
    )(dx2, qx, kv, xqg.reshape(1, X_HEAD_DIM), xkg.reshape(1, X_HEAD_DIM), x1, g_x.reshape(1, d), wo, wq, w_out)


def in_bwd(du, wt_in, x0, g_mix, dx1, *, tm=512):
    m, d = x0.shape
    n = wt_in.shape[0]

    def body(du_ref, wt_ref, x0_ref, g_ref, dx1_ref, dx0_ref, dg_ref):
        _zero_at_first_step(dg_ref)
        dh0 = _dot_nn(du_ref[...], wt_ref[...])
        xh, r = _rms(x0_ref[...])
        dg_ref[...] += jnp.sum(dh0 * xh, axis=0, keepdims=True)
        dx0_ref[...] = dx1_ref[...] + _rms_bwd(dh0, xh, r, g_ref[...])

    return pl.pallas_call(
        body, name="in_bwd", grid=(m // tm,),
        in_specs=[_rows(tm, n), _whole((n, d)), _rows(tm, d), _vec(d), _rows(tm, d)],
        out_specs=[_rows(tm, d), _vec(d)],
        out_shape=[jax.ShapeDtypeStruct((m, d), F32), jax.ShapeDtypeStruct((1, d), F32)],
        compiler_params=_cp(),
    )(du, wt_in, x0, g_mix.reshape(1, d), dx1)


SWA_TILE = 512
SWA_SUB = SWA_TILE // BLOCK
SWA_KEYS = SWA_TILE + BLOCK
PAIR = 2 * HEAD_DIM
KCOL = ATTN_WIDTH
VCOL = ATTN_WIDTH + KV_WIDTH


def _swa_constants():
    r = np.arange(2 * BLOCK)[:, None]
    j = np.arange(4 * BLOCK)[None, :]
    dist = (r % BLOCK) + BLOCK - (j % (2 * BLOCK))
    valid = (dist >= 0) & (dist < BLOCK)
    first_valid = valid & ((j % (2 * BLOCK)) >= BLOCK)
    bias, bias_first = [], []
    for kv in range(N_KV_HEADS):
        head = kv * GROUP + 2 * (r // BLOCK) + j // (2 * BLOCK)
        b = -(2.0 ** -(head + 1.0)) * dist
        bias.append(np.where(valid, b, NEG))
        bias_first.append(np.where(first_valid, b, NEG))
    lane = np.arange(LANES)
    seg = (lane[:, None] // HEAD_DIM == lane[None, :] // HEAD_DIM) / HEAD_DIM
    row = np.arange(4 * BLOCK)[:, None]
    ones = (row // (2 * BLOCK)) == (lane[None, :] // HEAD_DIM)
    return (jnp.asarray(np.stack(bias), F32), jnp.asarray(np.stack(bias_first), F32), jnp.asarray(seg, BF16),
            jnp.asarray(ones, BF16))


def _segmean(x, seg_ref):
    hi = x.astype(BF16)
    lo = (x - hi.astype(F32)).astype(BF16)
    return _dot_nn(hi, seg_ref[...]) + _dot_nn(lo, seg_ref[...])


def _two_heads(x, kv):
    lane = lax.broadcasted_iota(jnp.int32, (1, LANES), 1)
    mine = (lane < HEAD_DIM) if kv == 0 else (lane >= HEAD_DIM)
    base = jnp.where(mine, x, 0.0)
    other = pltpu.roll(base, HEAD_DIM, 1)
    return jnp.concatenate([base, other] if kv == 0 else [other, base], axis=0)


def _from_two_heads(y, kv):
    rows = y.shape[0] // 2
    lane = lax.broadcasted_iota(jnp.int32, (1, LANES), 1)
    top, bot = y[:rows], y[rows:]
    if kv == 0:
        return jnp.where(lane < HEAD_DIM, top + pltpu.roll(bot, HEAD_DIM, 1), 0.0)
    return jnp.where(lane >= HEAD_DIM, pltpu.roll(top, HEAD_DIM, 1) + bot, 0.0)


def _pair_rows(ref, rows, kv):
    c = kv * 2 * PAIR
    return jnp.concatenate([ref[rows, c:c + PAIR], ref[rows, c + PAIR:c + 2 * PAIR]], axis=0)


def _head_cols(fn, kv):
    return [jnp.concatenate([fn(kv * GROUP + half), fn(kv * GROUP + 2 + half)], axis=0) for half in range(2)]


def _swa_prologue(cur_ref, prev_ref, qg_ref, kg_ref, seg_ref, qg_s, kn_s, v_s):
    qg_s[...] = (cur_ref[:, 0:ATTN_WIDTH] * qg_ref[...]).astype(BF16)
    k = jnp.concatenate([prev_ref[:, KCOL:KCOL + KV_WIDTH], cur_ref[:, KCOL:KCOL + KV_WIDTH]], axis=0)
    kn_s[...] = k * lax.rsqrt(_segmean(k * k, seg_ref) + EPS) * kg_ref[...]
    v_s[0:BLOCK, :] = prev_ref[:, VCOL:VCOL + KV_WIDTH]
    v_s[BLOCK:SWA_KEYS, :] = cur_ref[:, VCOL:VCOL + KV_WIDTH]


def _swa_scores(cur_ref, sinks_ref, qg_s, kn_s, bias, rows, keys, kv):
    q2 = _pair_rows(qg_s, rows, kv)
    k2 = _two_heads(kn_s[keys, :], kv)
    t = _dot_nt(q2, k2)

    def rq(h):
        x = cur_ref[rows, h * HEAD_DIM:(h + 1) * HEAD_DIM]
        return lax.rsqrt(jnp.mean(x * x, axis=-1, keepdims=True) + EPS)

    scale = _head_cols(lambda h: rq(h) * (HEAD_DIM ** -0.5), kv)
    sink = _head_cols(lambda h: jnp.full((BLOCK, 1), sinks_ref[h], F32), kv)
    halves = []
    for half in range(2):
        cols = slice(half * 2 * BLOCK, (half + 1) * 2 * BLOCK)
        s = t[:, cols] * scale[half] + bias[:, cols]
        mx = jnp.maximum(jnp.max(s, axis=-1, keepdims=True), sink[half])
        halves.append((scale[half], jnp.exp(s - mx), jnp.exp(sink[half] - mx)))
    return q2, k2, t, halves


def swa_fwd(u, qg, kg, sinks):
    t_rows = u.shape[0]
    nt = t_rows // SWA_TILE
    bias_c, bias_first_c, seg_c, ones_c = _swa_constants()

    def body(sinks_ref, cur_ref, prev_ref, qg_ref, kg_ref, seg_ref, bias_ref, biasf_ref, ones_ref, o_ref, qg_s, kn_s, v_s):
        i = pl.program_id(0)
        _swa_prologue(cur_ref, prev_ref, qg_ref, kg_ref, seg_ref, qg_s, kn_s, v_s)
        lane = lax.broadcasted_iota(jnp.int32, (1, LANES), 1)
        for b in range(SWA_SUB):
            rows = slice(b * BLOCK, (b + 1) * BLOCK)
            keys = slice(b * BLOCK, (b + 2) * BLOCK)
            for kv in range(N_KV_HEADS):
                bias = jnp.where(i == 0, biasf_ref[kv], bias_ref[kv]) if b == 0 else bias_ref[kv]
                _, _, _, halves = _swa_scores(cur_ref, sinks_ref, qg_s, kn_s, bias, rows, keys, kv)
                e = jnp.concatenate([halves[0][1], halves[1][1]], axis=1)
                v2 = jnp.concatenate([_two_heads(v_s[keys, :], kv).astype(BF16), ones_ref[...]], axis=1)
                ox = _dot_nn(e, v2)
                den = ox[:, LANES:] + jnp.where(lane < HEAD_DIM, halves[0][2], halves[1][2])
                out = (ox[:, :LANES] / den).astype(o_ref.dtype)
                c = kv * 2 * PAIR
                o_ref[rows, c:c + PAIR] = out[:BLOCK]
                o_ref[rows, c + PAIR:c + 2 * PAIR] = out[BLOCK:]

    const3 = pl.BlockSpec((N_KV_HEADS, 2 * BLOCK, 4 * BLOCK), lambda i: (0, 0, 0))
    return pl.pallas_call(
        body, name="swa_fwd", grid=(nt,),
        in_specs=[
            pl.BlockSpec(memory_space=pltpu.SMEM),
            pl.BlockSpec((SWA_TILE, QKV_WIDTH), lambda i: (i, 0)),
            pl.BlockSpec((BLOCK, QKV_WIDTH), lambda i: (jnp.maximum(i * SWA_SUB - 1, 0), 0)),
            pl.BlockSpec((1, ATTN_WIDTH), lambda i: (0, 0)), pl.BlockSpec((1, KV_WIDTH), lambda i: (0, 0)),
            pl.BlockSpec((LANES, LANES), lambda i: (0, 0)), const3, const3,
            pl.BlockSpec((4 * BLOCK, LANES), lambda i: (0, 0)),
        ],
        out_specs=pl.BlockSpec((SWA_TILE, ATTN_WIDTH), lambda i: (i, 0)),
        out_shape=jax.ShapeDtypeStruct((t_rows, 2 * ATTN_WIDTH), BF16),
        scratch_shapes=[pltpu.VMEM((SWA_TILE, ATTN_WIDTH), BF16), pltpu.VMEM((SWA_KEYS, KV_WIDTH), F32),
                        pltpu.VMEM((SWA_KEYS, KV_WIDTH), F32)],
        compiler_params=_cp(),
    )(sinks, u, u, jnp.tile(qg, N_Q_HEADS).reshape(1, ATTN_WIDTH), jnp.tile(kg, N_KV_HEADS).reshape(1, KV_WIDTH),
      seg_c, bias_c, bias_first_c, ones_c)


def swa_bwd(u, dmixed, qg, kg, sinks):
    t_rows = u.shape[0]
    nt = t_rows // SWA_TILE
    bias_c, bias_first_c, seg_c, _ = _swa_constants()

    def body(sinks_ref, cur_ref, prev_ref, do_ref, qg_ref, kg_ref, seg_ref, bias_ref, biasf_ref,
             du_ref, dqg_ref, dkg_ref, dsk_ref, qg_s, kn_s, v_s, acck_s, accv_s, carryk_s, carryv_s):
        step = pl.program_id(0)
        i = nt - 1 - step

        @pl.when(step == 0)
        def _():
            for r in (carryk_s, carryv_s, dqg_ref, dkg_ref, dsk_ref):
                r[...] = jnp.zeros_like(r)

        _swa_prologue(cur_ref, prev_ref, qg_ref, kg_ref, seg_ref, qg_s, kn_s, v_s)
        for acc, carry in ((acck_s, carryk_s), (accv_s, carryv_s)):
            acc[0:SWA_TILE, :] = jnp.zeros((SWA_TILE, KV_WIDTH), F32)
            acc[SWA_TILE:SWA_KEYS, :] = carry[...]

        lane = lax.broadcasted_iota(jnp.int32, (1, LANES), 1)
        g_pair = qg_ref[:, 0:PAIR]
        dqg_acc = jnp.zeros((1, PAIR), F32)
        dsk_acc = jnp.zeros((1, LANES), F32)
        for b in range(SWA_SUB):
            rows = slice(b * BLOCK, (b + 1) * BLOCK)
            keys = slice(b * BLOCK, (b + 2) * BLOCK)
            for kv in range(N_KV_HEADS):
                bias = jnp.where(i == 0, biasf_ref[kv], bias_ref[kv]) if b == 0 else bias_ref[kv]
                q2, k2, t, halves = _swa_scores(cur_ref, sinks_ref, qg_s, kn_s, bias, rows, keys, kv)
                v2 = _two_heads(v_s[keys, :], kv)
                do2 = _pair_rows(do_ref, rows, kv)
                dp = _dot_nt(do2, v2)
                p_parts, dt_parts, coef = [], [], []
                for half, (scale, e, es) in enumerate(halves):
                    cols = slice(half * 2 * BLOCK, (half + 1) * 2 * BLOCK)
                    rden = 1.0 / (jnp.sum(e, axis=-1, keepdims=True) + es)
                    p = e * rden
                    dp_h = dp[:, cols]
                    delta = jnp.sum(p * dp_h, axis=-1, keepdims=True)
                    ds = p * (dp_h - delta)
                    dsink = -(es * rden) * delta
                    for pair in range(2):
                        part = jnp.sum(dsink[pair * BLOCK:(pair + 1) * BLOCK], axis=0, keepdims=True)
                        dsk_acc = dsk_acc + jnp.where(lane == kv * GROUP + 2 * pair + half, part, 0.0)
                    dscale = jnp.sum(ds * t[:, cols], axis=-1, keepdims=True)
                    coef.append(-dscale * scale * scale * scale)
                    p_parts.append(p)
                    dt_parts.append(ds * scale)
                p2 = jnp.concatenate(p_parts, axis=1)
                dt = jnp.concatenate(dt_parts, axis=1)
                dqg2 = _dot_nn(dt, k2)
                q_raw = _pair_rows(cur_ref, rows, kv)
                dq = dqg2 * g_pair + jnp.where(lane < HEAD_DIM, coef[0], coef[1]) * q_raw
                dqg_acc = dqg_acc + jnp.sum(dqg2 * q_raw, axis=0, keepdims=True)
                c = kv * 2 * PAIR
                du_ref[rows, c:c + PAIR] = dq[:BLOCK].astype(du_ref.dtype)
                du_ref[rows, c + PAIR:c + 2 * PAIR] = dq[BLOCK:].astype(du_ref.dtype)
                acck_s[keys, :] += _from_two_heads(_dot_tn(dt, q2), kv)
                accv_s[keys, :] += _from_two_heads(_dot_tn(p2, do2), kv)
        dqg_ref[...] += dqg_acc + pltpu.roll(dqg_acc, HEAD_DIM, 1)
        dsk_ref[...] += dsk_acc

        own = slice(BLOCK, SWA_KEYS)
        k = cur_ref[:, KCOL:KCOL + KV_WIDTH]
        rk = lax.rsqrt(_segmean(k * k, seg_ref) + EPS)
        kh = k * rk
        dkn = acck_s[own, :]
        dkh = dkn * kg_ref[...]
        du_ref[:, KCOL:KCOL + KV_WIDTH] = (rk * (dkh - kh * _segmean(dkh * kh, seg_ref))).astype(du_ref.dtype)
        du_ref[:, VCOL:VCOL + KV_WIDTH] = accv_s[own, :].astype(du_ref.dtype)
        dkg_part = jnp.sum(dkn * kh, axis=0, keepdims=True)
        dkg_ref[...] += dkg_part + pltpu.roll(dkg_part, HEAD_DIM, 1)
        carryk_s[...] = acck_s[0:BLOCK, :]
        carryv_s[...] = accv_s[0:BLOCK, :]

    const3 = pl.BlockSpec((N_KV_HEADS, 2 * BLOCK, 4 * BLOCK), lambda s: (0, 0, 0))
    vec = pl.BlockSpec((1, LANES), lambda s: (0, 0))
    return pl.pallas_call(
        body, name="swa_bwd", grid=(nt,),
        in_specs=[
            pl.BlockSpec(memory_space=pltpu.SMEM),
            pl.BlockSpec((SWA_TILE, QKV_WIDTH), lambda s: (nt - 1 - s, 0)),
            pl.BlockSpec((BLOCK, QKV_WIDTH), lambda s: (jnp.maximum((nt - 1 - s) * SWA_SUB - 1, 0), 0)),
            pl.BlockSpec((SWA_TILE, ATTN_WIDTH), lambda s: (nt - 1 - s, 0)),
            pl.BlockSpec((1, ATTN_WIDTH), lambda s: (0, 0)), vec,
            pl.BlockSpec((LANES, LANES), lambda s: (0, 0)), const3, const3,
        ],
        out_specs=[pl.BlockSpec((SWA_TILE, QKV_WIDTH), lambda s: (nt - 1 - s, 0)), vec, vec, vec],
        out_shape=[jax.ShapeDtypeStruct((t_rows, IN_COLS), BF16)] + [jax.ShapeDtypeStruct((1, LANES), F32)] * 3,
        scratch_shapes=[pltpu.VMEM((SWA_TILE, ATTN_WIDTH), BF16)] + [pltpu.VMEM((SWA_KEYS, KV_WIDTH), F32)] * 4
        + [pltpu.VMEM((BLOCK, KV_WIDTH), F32)] * 2,
        compiler_params=_cp(),
    )(sinks, u, u, dmixed, jnp.tile(qg, N_Q_HEADS).reshape(1, ATTN_WIDTH), jnp.tile(kg, N_KV_HEADS).reshape(1, KV_WIDTH),
      seg_c, bias_c, bias_first_c)


CONV_TILE = 512
CONV_CHUNK = 64
VAL0 = QKV_WIDTH
GATE0 = QKV_WIDTH + CONV_CH


def _glu(ref):
    return ref[:, VAL0:GATE0] * _sigmoid(ref[:, GATE0:GATE0 + CONV_CH])


SUBLANES = 8
CONV_BUF = CONV_HALO + CONV_TILE + SUBLANES
CONV_EXT = CONV_HALO + CONV_TILE


def _fill_shifted(sh_ref):
    for r in range(1, SUBLANES):
        sh_ref[r, 0:CONV_EXT, :] = sh_ref[0, pl.ds(r, CONV_EXT), :]


def _shifted(sh_ref, start, offset, n):
    return sh_ref[offset % SUBLANES, pl.ds(start + offset - offset % SUBLANES, n), :]


def _layernorm_stats(y):
    mu = jnp.mean(y, axis=-1, keepdims=True)
    yc = y - mu
    rstd = lax.rsqrt(jnp.mean(yc * yc, axis=-1, keepdims=True) + EPS)
    return yc * rstd, rstd


def conv_fwd(u, mixed, conv_w, conv_b, ln_g, ln_b):
    t = u.shape[0]
    nt = t // CONV_TILE
    per = CONV_TILE // CONV_HALO

    def body(cur_ref, prev_ref, mixed_ref, w_ref, b_ref, g_ref, b2_ref, o_ref, y_ref, gl_ref):
        del mixed_ref
        i = pl.program_id(0)
        gl_ref[0, 0:CONV_HALO, :] = jnp.where(i > 0, _glu(prev_ref), 0.0)
        gl_ref[0, CONV_HALO:CONV_EXT, :] = _glu(cur_ref)
        gl_ref[0, CONV_EXT:CONV_BUF, :] = jnp.zeros((SUBLANES, CONV_CH), F32)
        _fill_shifted(gl_ref)
        for c0 in range(0, CONV_TILE, CONV_CHUNK):
            acc = jnp.broadcast_to(b_ref[...], (CONV_CHUNK, CONV_CH))
            for k in range(CONV_K):
                acc = acc + w_ref[k:k + 1, :] * _shifted(gl_ref, c0, 2 + k, CONV_CHUNK)
            y_ref[c0:c0 + CONV_CHUNK, :] = acc
        yh, _ = _layernorm_stats(y_ref[...])
        yln = yh * g_ref[...] + b2_ref[...]
        o_ref[...] = (yln * _sigmoid(yln)).astype(o_ref.dtype)

    vec = pl.BlockSpec((1, CONV_CH), lambda i: (0, 0))
    return pl.pallas_call(
        body, name="conv_fwd", grid=(nt,),
        in_specs=[
            pl.BlockSpec((CONV_TILE, IN_COLS), lambda i: (i, 0)),
            pl.BlockSpec((CONV_HALO, IN_COLS), lambda i: (jnp.maximum(i * per - 1, 0), 0)),
            pl.BlockSpec(memory_space=pl.ANY),
            pl.BlockSpec((CONV_HALO, CONV_CH), lambda i: (0, 0)),
            vec, vec, vec,
        ],
        out_specs=[pl.BlockSpec((CONV_TILE, CONV_CH), lambda i: (i, 1)), pl.BlockSpec((CONV_TILE, CONV_CH), lambda i: (i, 0))],
        out_shape=[jax.ShapeDtypeStruct(mixed.shape, mixed.dtype), jax.ShapeDtypeStruct((t, CONV_CH), F32)],
        scratch_shapes=[pltpu.VMEM((SUBLANES, CONV_BUF, CONV_CH), F32)],
        input_output_aliases={2: 0}, compiler_params=_cp(),
    )(u, u, mixed, conv_w, conv_b.reshape(1, CONV_CH), ln_g.reshape(1, CONV_CH), ln_b.reshape(1, CONV_CH))


def conv_bwd(u, y, dmixed, du, conv_w, ln_g, ln_b):
    t = u.shape[0]
    nt = t // CONV_TILE
    per = CONV_TILE // CONV_HALO

    def body(cur_ref, prev_ref, y_ref, yn_ref, do_ref, don_ref, du_in_ref, w_ref, g_ref, b2_ref,
             du_ref, dw_ref, dvec_ref, gl_ref, dy_ref):
        i = pl.program_id(0)
        last = i == nt - 1
        _zero_at_first_step(dw_ref, dvec_ref)

        gl_ref[0, 0:CONV_HALO, :] = jnp.where(i > 0, _glu(prev_ref), 0.0)
        gl_ref[0, CONV_HALO:CONV_EXT, :] = _glu(cur_ref)
        gl_ref[0, CONV_EXT:CONV_BUF, :] = jnp.zeros((SUBLANES, CONV_CH), F32)
        _fill_shifted(gl_ref)

        yh, rstd = _layernorm_stats(jnp.concatenate([y_ref[...], yn_ref[...]], axis=0))
        g = g_ref[...]
        yln = yh * g + b2_ref[...]
        sg = _sigmoid(yln)
        dout = jnp.concatenate([do_ref[...], jnp.where(last, 0.0, don_ref[...])], axis=0)
        dyln = dout * (sg * (1.0 + yln * (1.0 - sg)))
        dyh = dyln * g
        dy = rstd * (dyh - jnp.mean(dyh, axis=-1, keepdims=True) - yh * jnp.mean(dyh * yh, axis=-1, keepdims=True))
        dy_ref[0, 0:CONV_EXT, :] = dy
        dy_ref[0, CONV_EXT:CONV_BUF, :] = jnp.zeros((SUBLANES, CONV_CH), F32)
        _fill_shifted(dy_ref)

        own = slice(0, CONV_TILE)
        dvec_ref[0:1, :] += jnp.sum(dy[own], axis=0, keepdims=True)
        dvec_ref[1:2, :] += jnp.sum(dyln[own] * yh[own], axis=0, keepdims=True)
        dvec_ref[2:3, :] += jnp.sum(dyln[own], axis=0, keepdims=True)
        for k in range(CONV_K):
            dw_ref[k:k + 1, :] += jnp.sum(dy[own] * _shifted(gl_ref, 0, 2 + k, CONV_TILE), axis=0, keepdims=True)

        for c0 in range(0, CONV_TILE, CONV_CHUNK):
            acc = jnp.zeros((CONV_CHUNK, CONV_CH), F32)
            for k in range(CONV_K):
                acc = acc + w_ref[k:k + 1, :] * _shifted(dy_ref, c0, CONV_K - 1 - k, CONV_CHUNK)
            rows = slice(c0, c0 + CONV_CHUNK)
            val = cur_ref[rows, VAL0:GATE0]
            sgate = _sigmoid(cur_ref[rows, GATE0:GATE0 + CONV_CH])
            du_ref[rows, VAL0:GATE0] = (acc * sgate).astype(du_ref.dtype)
            du_ref[rows, GATE0:GATE0 + CONV_CH] = (acc * val * sgate * (1.0 - sgate)).astype(du_ref.dtype)
        du_ref[:, 0:QKV_WIDTH] = du_in_ref[:, 0:QKV_WIDTH]

    vec = pl.BlockSpec((1, CONV_CH), lambda i: (0, 0))
    n_halo = t // CONV_HALO
    return pl.pallas_call(
        body, name="conv_bwd", grid=(nt,),
        in_specs=[
            pl.BlockSpec((CONV_TILE, IN_COLS), lambda i: (i, 0)),
            pl.BlockSpec((CONV_HALO, IN_COLS), lambda i: (jnp.maximum(i * per - 1, 0), 0)),
            pl.BlockSpec((CONV_TILE, CONV_CH), lambda i: (i, 0)),
            pl.BlockSpec((CONV_HALO, CONV_CH), lambda i: (jnp.minimum((i + 1) * per, n_halo - 1), 0)),
            pl.BlockSpec((CONV_TILE, CONV_CH), lambda i: (i, 1)),
            pl.BlockSpec((CONV_HALO, CONV_CH), lambda i: (jnp.minimum((i + 1) * per, n_halo - 1), 1)),
            pl.BlockSpec((CONV_TILE, IN_COLS), lambda i: (i, 0)),
            pl.BlockSpec((CONV_HALO, CONV_CH), lambda i: (0, 0)),
            vec, vec,
        ],
        out_specs=[
            pl.BlockSpec((CONV_TILE, IN_COLS), lambda i: (i, 0)),
            pl.BlockSpec((CONV_HALO, CONV_CH), lambda i: (0, 0)),
            pl.BlockSpec((8, CONV_CH), lambda i: (0, 0)),
        ],
        out_shape=[
            jax.ShapeDtypeStruct(du.shape, du.dtype),
            jax.ShapeDtypeStruct((CONV_HALO, CONV_CH), F32),
            jax.ShapeDtypeStruct((8, CONV_CH), F32),
        ],
        scratch_shapes=[pltpu.VMEM((SUBLANES, CONV_BUF, CONV_CH), F32), pltpu.VMEM((SUBLANES, CONV_BUF, CONV_CH), F32)],
        input_output_aliases={6: 0}, compiler_params=_cp(),
    )(u, u, y, y, dmixed, dmixed, du, conv_w, ln_g.reshape(1, CONV_CH), ln_b.reshape(1, CONV_CH))


def _xattn_probs(qn, kn):
    s = _dot_nt(qn, kn) * (X_HEAD_DIM ** -0.5)
    e = jnp.exp(s - jnp.max(s, axis=-1, keepdims=True))
    return e / jnp.sum(e, axis=-1, keepdims=True)


def adamw(w, g, m, v, *, name):
    r, c = w.shape
    tr = r
    for cand in (512, 256, 128, 64, 32, 16, 8):
        if r % cand == 0 and r > cand:
            tr = cand
            break

    def body(w_ref, g_ref, m_ref, v_ref, d_ref, nm_ref, nv_ref):
        d_ref[...], nm_ref[...], nv_ref[...] = _adamw_math(w_ref[...], g_ref[...], m_ref[...], v_ref[...])

    spec = pl.BlockSpec((tr, c), lambda i: (i, 0))
    shape = jax.ShapeDtypeStruct((r, c), F32)
    return pl.pallas_call(
        body, name=name, grid=(r // tr,), in_specs=[spec] * 4, out_specs=[spec] * 3,
        out_shape=[shape] * 3, compiler_params=_cp(),
    )(w, g, m, v)


def _position():
    return lax.axis_index("x"), lax.axis_index("y"), lax.axis_index("c")


def all_gather_many(shards, *, name):
    n = len(shards)

    def body(*refs):
        x_refs, out_refs, token_ref = refs[:n], refs[n:2 * n], refs[2 * n]
        send_sems, recv_sems, local_sems = refs[2 * n + 1:]
        x, y, c = _position()
        me, sibling = (x, y, c), (x, y, 1 - c)
        chips = [(1 - x, y), (x, 1 - y), (1 - x, 1 - y)]
        token_ref[...] = jnp.zeros_like(token_ref)

        def rows(t, px, py, pc):
            return out_refs[t].at[4 * px + 2 * py + pc]

        def copy(t, k, block, to, src=None):
            return pltpu.make_async_remote_copy(
                src_ref=rows(t, *block) if src is None else src, dst_ref=rows(t, *block),
                send_sem=send_sems.at[7 * t + k], recv_sem=recv_sems.at[7 * t + k], device_id=to, device_id_type=MESH)

        mine = [pltpu.make_async_copy(x_refs[t], rows(t, *me), local_sems.at[t]) for t in range(n)]
        for cp in mine:
            cp.start()
        first = []
        for t in range(n):
            first.append(copy(t, 0, me, sibling, src=x_refs[t]))
            first += [copy(t, 1 + j, me, (*chip, c), src=x_refs[t]) for j, chip in enumerate(chips)]
        for cp in first:
            cp.start()
        passed = []
        for t in range(n):
            for j, chip in enumerate(chips):
                copy(t, 1 + j, (*chip, c), me).wait_recv()
                passed.append(copy(t, 4 + j, (*chip, c), sibling))
                passed[-1].start()
        for t in range(n):
            copy(t, 0, sibling, me).wait_recv()
            for j, chip in enumerate(chips):
                copy(t, 4 + j, (*chip, 1 - c), me).wait_recv()
        for cp in first + passed:
            cp.wait_send()
        for cp in mine:
            cp.wait()

    hbm = pl.BlockSpec(memory_space=pltpu.HBM)
    out = pl.pallas_call(
        body, name=name,
        out_shape=[jax.ShapeDtypeStruct((N_DEV,) + s.shape, s.dtype) for s in shards] + [jax.ShapeDtypeStruct((8, LANES), F32)],
        in_specs=[hbm] * n, out_specs=[hbm] * n + [pl.BlockSpec(memory_space=pltpu.VMEM)],
        scratch_shapes=[pltpu.SemaphoreType.DMA((7 * n,)), pltpu.SemaphoreType.DMA((7 * n,)), pltpu.SemaphoreType.DMA((n,))],
        compiler_params=_cp(),
    )(*shards)
    return out[:n], out[n]


_HBM = pl.BlockSpec(memory_space=pltpu.HBM)
_SEM = pl.BlockSpec(memory_space=pltpu.SEMAPHORE)
_EFFECT = pltpu.SideEffectType.DATAFLOW_SIDE_EFFECTING


def _split_copies(src_refs, land_refs, send_sems, recv_sems, plan, n_copies):
    copies = []
    for t, (src_ref, land_ref) in enumerate(zip(src_refs, land_refs)):
        for k in range(n_copies):
            s, d, to = plan(src_ref, land_ref, k)
            copies.append(pltpu.make_async_remote_copy(
                src_ref=s, dst_ref=d, send_sem=send_sems.at[n_copies * t + k], recv_sem=recv_sems.at[n_copies * t + k],
                device_id=to, device_id_type=MESH))
    return copies


def split_start(srcs, lands, plan, n_copies, *, name):
    n = len(srcs)

    def body(*refs):
        src_refs, land_refs, send_sems, recv_sems, token = refs[:n], refs[n:2 * n], refs[2 * n], refs[2 * n + 1], refs[-1]
        for cp in _split_copies(src_refs, land_refs, send_sems, recv_sems, plan, n_copies):
            cp.start()
        token[...] = jnp.zeros_like(token)

    both = list(srcs) + list(lands)
    out = pl.pallas_call(
        body, name=name,
        out_shape=(pltpu.SemaphoreType.DMA((n_copies * n,)), pltpu.SemaphoreType.DMA((n_copies * n,)),
                   *[pltpu.HBM(a.shape, a.dtype) for a in both], jax.ShapeDtypeStruct((8, LANES), F32)),
        in_specs=(_HBM,) * (2 * n), out_specs=(_SEM, _SEM) + (_HBM,) * (2 * n) + (pl.BlockSpec(memory_space=pltpu.VMEM),),
        input_output_aliases={i: 2 + i for i in range(2 * n)},
        compiler_params=pltpu.CompilerParams(has_side_effects=_EFFECT),
    )(*[pltpu.with_memory_space_constraint(a, pltpu.HBM) for a in both])
    return out[0], out[1], list(out[2:2 + n]), list(out[2 + n:2 + 2 * n]), out[-1]


def split_wait(started, after, plan, n_copies, *, name):
    send_sems, recv_sems, srcs, lands, _ = started
    n = len(srcs)

    def body(*refs):
        src_refs, land_refs, send_sems, recv_sems = refs[:n], refs[n:2 * n], refs[2 * n], refs[2 * n + 1]
        for cp in _split_copies(src_refs, land_refs, send_sems, recv_sems, plan, n_copies):
            cp.wait_send()
            cp.wait_recv()

    both = list(srcs) + list(lands)
    out = pl.pallas_call(
        body, name=name,
        out_shape=tuple(pltpu.HBM(a.shape, a.dtype) for a in both),
        in_specs=(_HBM,) * (2 * n) + (_SEM, _SEM, pl.BlockSpec(memory_space=pl.ANY)), out_specs=(_HBM,) * (2 * n),
        input_output_aliases={i: i for i in range(2 * n)},
        compiler_params=pltpu.CompilerParams(has_side_effects=_EFFECT),
    )(*both, send_sems, recv_sems, after)
    return list(out[:n]), list(out[n:])


def _other_chips(x, y):
    return [(1 - x, y), (x, 1 - y), (1 - x, 1 - y)]


def _remote(src, dst, send_sem, recv_sem, to):
    return pltpu.make_async_remote_copy(src_ref=src, dst_ref=dst, send_sem=send_sem, recv_sem=recv_sem,
                                        device_id=to, device_id_type=MESH)


def gather_start(groups, *, name):
    counts = [len(shards) for shards, _ in groups]
    flat = [a for shards, _ in groups for a in shards] + [a for _, lands in groups for a in lands]
    n_all, n_groups = sum(counts), len(groups)

    def body(*refs):
        s_refs, l_refs = refs[:n_all], refs[n_all:2 * n_all]
        sems = refs[2 * n_all:2 * n_all + 3 * n_groups]
        x, y, c = _position()
        me = 4 * x + 2 * y + c
        at = 0
        for gi, n in enumerate(counts):
            send, recv_sibling, recv_ici = sems[3 * gi:3 * gi + 3]
            for t in range(n):
                src, dst = s_refs[at + t], l_refs[at + t].at[me]
                _remote(src, dst, send.at[4 * t], recv_sibling.at[t], (x, y, 1 - c)).start()
                for j, chip in enumerate(_other_chips(x, y)):
                    _remote(src, dst, send.at[4 * t + 1 + j], recv_ici.at[3 * t + j], (*chip, c)).start()
            at += n
        refs[-1][...] = jnp.zeros_like(refs[-1])

    sem_shapes = [pltpu.SemaphoreType.DMA((k * n,)) for n in counts for k in (4, 1, 3)]
    out = pl.pallas_call(
        body, name=name,
        out_shape=(*sem_shapes, *[pltpu.HBM(a.shape, a.dtype) for a in flat], jax.ShapeDtypeStruct((8, LANES), F32)),
        in_specs=(_HBM,) * (2 * n_all),
        out_specs=(_SEM,) * (3 * n_groups) + (_HBM,) * (2 * n_all) + (pl.BlockSpec(memory_space=pltpu.VMEM),),
        input_output_aliases={i: 3 * n_groups + i for i in range(2 * n_all)},
        compiler_params=pltpu.CompilerParams(has_side_effects=_EFFECT),
    )(*[pltpu.with_memory_space_constraint(a, pltpu.HBM) for a in flat])
    thru = out[3 * n_groups:-1]
    states, at = [], 0
    for gi, n in enumerate(counts):
        states.append(dict(shards=list(thru[at:at + n]), lands=list(thru[n_all + at:n_all + at + n]),
                           send=out[3 * gi], recv_sibling=out[3 * gi + 1], recv_ici=out[3 * gi + 2]))
        at += n
    return states, out[-1]


def gather_forward(states, after, *, name):
    counts = [len(s["lands"]) for s in states]
    flat = [a for s in states for a in s["lands"]]
    n_all, n_groups = sum(counts), len(states)

    def body(*refs):
        l_refs = refs[:n_all]
        recv_ici = refs[n_all:n_all + n_groups]
        fwd = refs[n_all + n_groups + 1:n_all + n_groups + 1 + 2 * n_groups]
        x, y, c = _position()
        at = 0
        for gi, n in enumerate(counts):
            fwd_send, fwd_recv = fwd[2 * gi], fwd[2 * gi + 1]
            for t in range(n):
                for j, (px, py) in enumerate(_other_chips(x, y)):
                    block = l_refs[at + t].at[4 * px + 2 * py + c]
                    _remote(block, block, fwd_send.at[3 * t + j], recv_ici[gi].at[3 * t + j], (px, py, c)).wait_recv()
                    _remote(block, block, fwd_send.at[3 * t + j], fwd_recv.at[3 * t + j], (x, y, 1 - c)).start()
            at += n
        refs[-1][...] = jnp.zeros_like(refs[-1])

    sem_shapes = [pltpu.SemaphoreType.DMA((3 * n,)) for n in counts for _ in range(2)]
    out = pl.pallas_call(
        body, name=name,
        out_shape=(*sem_shapes, *[pltpu.HBM(a.shape, a.dtype) for a in flat], jax.ShapeDtypeStruct((8, LANES), F32)),
        in_specs=(_HBM,) * n_all + (_SEM,) * n_groups + (pl.BlockSpec(memory_space=pl.ANY),),
        out_specs=(_SEM,) * (2 * n_groups) + (_HBM,) * n_all + (pl.BlockSpec(memory_space=pltpu.VMEM),),
        input_output_aliases={i: 2 * n_groups + i for i in range(n_all)},
        compiler_params=pltpu.CompilerParams(has_side_effects=_EFFECT),
    )(*flat, *[s["recv_ici"] for s in states], after)
    at = 0
    for gi, (s, n) in enumerate(zip(states, counts)):
        s.update(fwd_send=out[2 * gi], fwd_recv=out[2 * gi + 1], lands=list(out[2 * n_groups + at:2 * n_groups + at + n]))
        at += n
    return out[-1]


def gather_finish(state, after, *, name):
    n = len(state["lands"])

    def body(*refs):
        s_refs, l_refs = refs[:n], refs[n:2 * n]
        send, recv_sibling, fwd_send, fwd_recv = refs[2 * n:2 * n + 4]
        x, y, c = _position()
        me = 4 * x + 2 * y + c
        for t in range(n):
            own = l_refs[t].at[me]
            _remote(s_refs[t], own, send.at[4 * t], recv_sibling.at[t], (x, y, 1 - c)).wait_send()
            _remote(s_refs[t], l_refs[t].at[4 * x + 2 * y + 1 - c], send.at[4 * t], recv_sibling.at[t], (x, y, 1 - c)).wait_recv()
            for j, (px, py) in enumerate(_other_chips(x, y)):
                _remote(s_refs[t], own, send.at[4 * t + 1 + j], recv_sibling.at[t], (px, py, c)).wait_send()
                mine, theirs = l_refs[t].at[4 * px + 2 * py + c], l_refs[t].at[4 * px + 2 * py + 1 - c]
                _remote(mine, mine, fwd_send.at[3 * t + j], fwd_recv.at[3 * t + j], (x, y, 1 - c)).wait_send()
                _remote(theirs, theirs, fwd_send.at[3 * t + j], fwd_recv.at[3 * t + j], (x, y, 1 - c)).wait_recv()

    both = state["shards"] + state["lands"]
    out = pl.pallas_call(
        body, name=name,
        out_shape=tuple(pltpu.HBM(a.shape, a.dtype) for a in both),
        in_specs=(_HBM,) * (2 * n) + (_SEM,) * 4 + (pl.BlockSpec(memory_space=pl.ANY),), out_specs=(_HBM,) * (2 * n),
        input_output_aliases={i: i for i in range(2 * n)},
        compiler_params=pltpu.CompilerParams(has_side_effects=_EFFECT),
    )(*both, state["send"], state["recv_sibling"], state["fwd_send"], state["fwd_recv"], after)
    return list(out[n:])


def _all_peers_plan(src_ref, land_ref, k):
    x, y, c = _position()
    bits = k + 1
    peer = ((1 - x) if bits & 4 else x, (1 - y) if bits & 2 else y, (1 - c) if bits & 1 else c)
    return src_ref, land_ref.at[4 * x + 2 * y + c], peer


def _sibling_plan(src_ref, land_ref, k):
    x, y, c = _position()
    return src_ref.at[2 * k + (1 - c)], land_ref.at[k], (x, y, 1 - c)


def _chips_plan(src_ref, land_ref, j):
    x, y, c = _position()
    px, py = _other_chips(x, y)[j]
    return src_ref.at[j], land_ref.at[j], (px, py, c)


SUM_STEPS = 2


def sum_for_chips(parts, from_sibling, ck_idx, *, name):
    n = len(parts)

    def body(ck_ref, *refs):
        del ck_ref
        for t in range(n):
            refs[2 * n + t][...] = (refs[t][...] + refs[n + t][...]).astype(BF16)

    def blk(a):
        return (None, a.shape[1] // SUM_STEPS, a.shape[2])

    return pl.pallas_call(
        body, name=name,
        grid_spec=pltpu.PrefetchScalarGridSpec(
            num_scalar_prefetch=1, grid=(3, SUM_STEPS),
            in_specs=[pl.BlockSpec(blk(a), lambda j, i, ck: (2 * ck[1 + j] + ck[0], i, 0)) for a in parts]
            + [pl.BlockSpec(blk(a), lambda j, i, ck: (ck[1 + j], i, 0)) for a in from_sibling],
            out_specs=[pl.BlockSpec(blk(a), lambda j, i, ck: (j, i, 0)) for a in from_sibling]),
        out_shape=[jax.ShapeDtypeStruct((3,) + a.shape[1:], BF16) for a in from_sibling], compiler_params=_cp(),
    )(ck_idx, *parts, *from_sibling)


def sum_final(parts, from_sibling, from_chips, kc_idx, *, name):
    n = len(parts)

    def body(kc_ref, *refs):
        del kc_ref
        for t in range(n):
            p, s, a, b, d = (refs[j * n + t] for j in range(5))
            refs[5 * n + t][...] = (((p[...] + s[...]) + a[...].astype(F32)) + b[...].astype(F32)) + d[...].astype(F32)

    def blk(a):
        return (None, a.shape[1] // SUM_STEPS, a.shape[2])

    def chip_specs(j):
        return [pl.BlockSpec(blk(a), lambda i, kc: (j, i, 0)) for a in from_chips]

    return pl.pallas_call(
        body, name=name,
        grid_spec=pltpu.PrefetchScalarGridSpec(
            num_scalar_prefetch=1, grid=(SUM_STEPS,),
            in_specs=[pl.BlockSpec(blk(a), lambda i, kc: (2 * kc[0] + kc[1], i, 0)) for a in parts]
            + [pl.BlockSpec(blk(a), lambda i, kc: (kc[0], i, 0)) for a in from_sibling]
            + chip_specs(0) + chip_specs(1) + chip_specs(2),
            out_specs=[pl.BlockSpec(blk(a)[1:], lambda i, kc: (i, 0)) for a in parts]),
        out_shape=[jax.ShapeDtypeStruct(a.shape[1:], F32) for a in parts], compiler_params=_cp(),
    )(kc_idx, *parts, *from_sibling, *from_chips, *from_chips, *from_chips)


BIG = (
    ("w_in", IN_COLS, True), ("w_out", D_MODEL, False), ("wq_x", D_MODEL, False), ("wkv_x", 2 * D_MODEL, True),
    ("wo_x", D_MODEL, False), ("w_gate_up", 2 * D_FF, True), ("w_down", D_FF, False),
)

SMALL = ("norm_mix_g", "q_norm_g", "k_norm_g", "sinks", "conv_b", "conv_ln_g", "conv_ln_b",
         "norm_x_g", "norm_mem_g", "xq_norm_g", "xk_norm_g", "norm_ffn_g")


def _adamw_math(w, g, m, v):
    m2 = ADAM_B1 * m + (1.0 - ADAM_B1) * g
    v2 = ADAM_B2 * v + (1.0 - ADAM_B2) * jnp.square(g)
    m_hat = m2 / (1.0 - ADAM_B1 ** ADAM_STEP)
    v_hat = v2 / (1.0 - ADAM_B2 ** ADAM_STEP)
    return -ADAM_LR * (m_hat / (jnp.sqrt(v_hat) + ADAM_EPS) + ADAM_WD * w), m2, v2


def _small_rows(per_layer_shape):
    return 1 if len(per_layer_shape) == 1 else per_layer_shape[0]


def pack_small(parts, shapes):
    blocks = []
    for per_layer, sh in zip(parts, shapes):
        for g in per_layer:
            g = g.reshape(_small_rows(sh), sh[-1])
            blocks.append(jnp.pad(g, ((0, 0), (0, D_MODEL - sh[-1]))))
    rows = sum(b.shape[0] for b in blocks)
    blocks.append(jnp.zeros((-rows % 8, D_MODEL), F32))
    return jnp.concatenate(blocks, axis=0)


def update_small(gathered, shapes, weights, moments_m, moments_v, n_update):
    n_all = len(shapes)

    def body(*refs):
        g_ref = refs[0]
        w_refs, m_refs, v_refs = (refs[1 + j * n_update:1 + (j + 1) * n_update] for j in range(3))
        out = refs[1 + 3 * n_update:]
        grad_refs = out[:n_all]
        d_refs, nm_refs, nv_refs = (out[n_all + j * n_update:n_all + (j + 1) * n_update] for j in range(3))
        at = 0
        for p, sh in enumerate(shapes):
            rows, lanes = _small_rows(sh), sh[-1]
            for l in range(DEPTH):
                g = g_ref[0, at:at + rows, 0:lanes]
                for k in range(1, N_DEV):
                    g = g + g_ref[k, at:at + rows, 0:lanes]
                at += rows
                here = (slice(l, l + 1),) + (slice(None),) * (len(sh) - 1) if len(sh) == 1 else (l,)
                grad_refs[p][here] = g
                if p < n_update:
                    d, m2, v2 = _adamw_math(w_refs[p][here], g, m_refs[p][here], v_refs[p][here])
                    d_refs[p][here] = d
                    nm_refs[p][here] = m2
                    nv_refs[p][here] = v2

    full = [jax.ShapeDtypeStruct((DEPTH,) + tuple(sh), F32) for sh in shapes]
    out = pl.pallas_call(
        body, name="update_small", out_shape=full + full[:n_update] * 3, compiler_params=_cp(),
    )(gathered, *weights, *moments_m, *moments_v)
    return (out[:n_all], out[n_all:n_all + n_update], out[n_all + n_update:n_all + 2 * n_update],
            out[n_all + 2 * n_update:])


WEIGHT_GROUPS = {"in": ("w_in",), "mid": ("w_out", "wq_x", "wkv_x", "wo_x"), "ffn": ("w_gate_up", "w_down")}


def _layer_fwd(x0, mem, weights_of, s, reached):
    w = dict(weights_of("in", x0))
    h0, u = norm_proj(x0, s["norm_mix_g"], w["w_in"])
    mixed = swa_fwd(u, s["q_norm_g"], s["k_norm_g"], s["sinks"])
    reached("attn", mixed)
    mixed, conv_y = conv_fwd(u, mixed, s["conv_w"], s["conv_b"], s["conv_ln_g"], s["conv_ln_b"])
    w.update(weights_of("mid", conv_y))
    memn = rms_fwd(mem, s["norm_mem_g"])
    kv = mm(memn, w["wkv_x"], trans_b=True, out_dtype=F32, name="mm_kv")
    x1, h1, qx, o, x2, h2 = mid_fwd(mixed, x0, w["w_out"], s["norm_x_g"], w["wq_x"], kv, s["xq_norm_g"], s["xk_norm_g"],
                                    w["wo_x"], s["norm_ffn_g"])
    reached("mid", x2)
    w.update(weights_of("ffn", x2))
    gu, a, x3 = ffn_fwd(h2, x2, w["w_gate_up"], w["w_down"])
    saved = dict(x0=x0, h0=h0, u=u, conv_y=conv_y, mixed=mixed, x1=x1, h1=h1, qx=qx, memn=memn, kv=kv, o=o, x2=x2, h2=h2,
                 gu=gu, a=a)
    return x3, saved, w


def _ordered_after(a, token):
    return a if token is None else a + token[0, 0]


def _layer_bwd(dx3, mem, w, s, sv, token, stage_done):
    gs = {}
    dgu, dx2, dg = ffn_bwd(dx3, sv["gu"], sv["x2"], _ordered_after(s["norm_ffn_g"], token), w["w_down"], w["w_gate_up"])
    gs["norm_ffn_g"] = dg
    gb = {"w_down": mm_tn(sv["a"], dx3, name="mm_dw_down")}
    gb["w_gate_up"] = mm_tn(dgu, sv["h2"], tk=dgu.shape[0], name="mm_dw_gate_up")
    token = stage_done("ffn", gb, gb["w_gate_up"])

    gb = {}
    dq, dx1, dmixed, dkv, dqg, dkg, dg = mid_bwd(dx2, sv["qx"], sv["kv"], s["xq_norm_g"], s["xk_norm_g"], sv["x1"],
                                                 _ordered_after(s["norm_x_g"], token), w["wo_x"], w["wq_x"], w["w_out"])
    gs["xq_norm_g"], gs["xk_norm_g"], gs["norm_x_g"] = dqg, dkg, dg
    gb["wo_x"] = mm_tn(sv["o"], dx2, name="mm_dwo")
    gb["wq_x"] = mm_tn(sv["h1"], dq, name="mm_dwq")
    dmemn = mm(dkv, w["wkv_x"], trans_b=False, out_dtype=F32, name="mm_dmemn")
    gb["wkv_x"] = mm_tn(dkv, sv["memn"], name="mm_dwkv")
    gs["norm_mem_g"] = rms_gain_bwd(dmemn, mem)
    gb["w_out"] = mm_tn(sv["mixed"], dx1, name="mm_dw_out")
    token = stage_done("mid", gb, gb["w_out"])

    du, dqg, dkg, dsinks = swa_bwd(sv["u"], dmixed, _ordered_after(s["q_norm_g"], token), s["k_norm_g"], s["sinks"])
    gs["q_norm_g"], gs["k_norm_g"], gs["sinks"] = dqg[0, :HEAD_DIM], dkg[0, :HEAD_DIM], dsinks[0, :N_Q_HEADS]
    token = stage_done("attn", {}, dqg)
    du, dconv_w, dvec = conv_bwd(sv["u"], sv["conv_y"], dmixed, du, s["conv_w"], _ordered_after(s["conv_ln_g"], token),
                                 s["conv_ln_b"])
    gs["conv_w"] = dconv_w[:CONV_K]
    gs["conv_b"], gs["conv_ln_g"], gs["conv_ln_b"] = dvec[0], dvec[1], dvec[2]
    dw_in = mm_tn(du, sv["h0"], name="mm_dw_in")
    token = stage_done("in", {"w_in": dw_in}, dw_in)
    dx0, dg = in_bwd(du, w["w_in"], sv["x0"], _ordered_after(s["norm_mix_g"], token), dx1)
    gs["norm_mix_g"] = dg
    token = stage_done("mix", {}, dx0)
    return dx0, gs, token


def _local_step(x, mem, target, weights_of, reached, smalls, stage_done):
    saved, weights = [], []
    h = x
    for l in range(DEPTH):
        h, sv, w = _layer_fwd(h, mem, functools.partial(weights_of, l), smalls[l], functools.partial(reached, l))
        saved.append(sv)
        weights.append(w)
    dx, loss_part = loss_head(h, target)
    gss, token = [None] * DEPTH, None
    for l in reversed(range(DEPTH)):
        dx, gss[l], token = _layer_bwd(dx, mem, weights[l], smalls[l], saved[l], token,
                                       functools.partial(stage_done, l))
    return loss_part[0, 0], dx, gss


def kernel(x, mem, norm_mix_g, w_in, q_norm_g, k_norm_g, sinks, conv_w, conv_b, conv_ln_g, conv_ln_b, w_out, norm_x_g, norm_mem_g, wq_x, wkv_x, xq_norm_g, xk_norm_g, wo_x, norm_ffn_g, w_gate_up, w_down, loss_target, m_norm_mix_g, m_w_in, m_q_norm_g, m_k_norm_g, m_sinks, m_conv_w, m_conv_b, m_conv_ln_g, m_conv_ln_b, m_w_out, m_norm_x_g, m_norm_mem_g, m_wq_x, m_wkv_x, m_xq_norm_g, m_xk_norm_g, m_wo_x, m_norm_ffn_g, m_w_gate_up, m_w_down, v_norm_mix_g, v_w_in, v_q_norm_g, v_k_norm_g, v_sinks, v_conv_w, v_conv_b, v_conv_ln_g, v_conv_ln_b, v_w_out, v_norm_x_g, v_norm_mem_g, v_wq_x, v_wkv_x, v_xq_norm_g, v_xk_norm_g, v_wo_x, v_norm_ffn_g, v_w_gate_up, v_w_down):
    P = dict(norm_mix_g=norm_mix_g, w_in=w_in, q_norm_g=q_norm_g, k_norm_g=k_norm_g, sinks=sinks, conv_w=conv_w, conv_b=conv_b,
             conv_ln_g=conv_ln_g, conv_ln_b=conv_ln_b, w_out=w_out, norm_x_g=norm_x_g, norm_mem_g=norm_mem_g, wq_x=wq_x,
             wkv_x=wkv_x, xq_norm_g=xq_norm_g, xk_norm_g=xk_norm_g, wo_x=wo_x, norm_ffn_g=norm_ffn_g, w_gate_up=w_gate_up,
             w_down=w_down)
    M = dict(norm_mix_g=m_norm_mix_g, w_in=m_w_in, q_norm_g=m_q_norm_g, k_norm_g=m_k_norm_g, sinks=m_sinks, conv_w=m_conv_w,
             conv_b=m_conv_b, conv_ln_g=m_conv_ln_g, conv_ln_b=m_conv_ln_b, w_out=m_w_out, norm_x_g=m_norm_x_g,
             norm_mem_g=m_norm_mem_g, wq_x=m_wq_x, wkv_x=m_wkv_x, xq_norm_g=m_xq_norm_g, xk_norm_g=m_xk_norm_g, wo_x=m_wo_x,
             norm_ffn_g=m_norm_ffn_g, w_gate_up=m_w_gate_up, w_down=m_w_down)
    V = dict(norm_mix_g=v_norm_mix_g, w_in=v_w_in, q_norm_g=v_q_norm_g, k_norm_g=v_k_norm_g, sinks=v_sinks, conv_w=v_conv_w,
             conv_b=v_conv_b, conv_ln_g=v_conv_ln_g, conv_ln_b=v_conv_ln_b, w_out=v_w_out, norm_x_g=v_norm_x_g,
             norm_mem_g=v_norm_mem_g, wq_x=v_wq_x, wkv_x=v_wkv_x, xq_norm_g=v_xq_norm_g, xk_norm_g=v_xk_norm_g, wo_x=v_wo_x,
             norm_ffn_g=v_norm_ffn_g, w_gate_up=v_w_gate_up, w_down=v_w_down)
    order = ["norm_mix_g", "w_in", "q_norm_g", "k_norm_g", "sinks", "conv_w", "conv_b", "conv_ln_g", "conv_ln_b", "w_out",
             "norm_x_g", "norm_mem_g", "wq_x", "wkv_x", "xq_norm_g", "xk_norm_g", "wo_x", "norm_ffn_g", "w_gate_up", "w_down"]
    xi, yi, ci = _position()
    dev = 4 * xi + 2 * yi + ci
    x2d, mem2d, tgt2d = x[0], mem[0], loss_target[0]

    def travelling(name, l, transposed):
        a = P[name][l]
        return (a.T if transposed else a).astype(BF16)

    rows_of = {n: rows for n, rows, _ in BIG}
    transposed_of = {n: tr for n, _, tr in BIG}

    def whole(names, gathered):
        return {n: g.reshape(rows_of[n], D_MODEL) for n, g in zip(names, gathered)}

    cw = jnp.pad(conv_w.reshape(DEPTH * CONV_K, CONV_CH // N_DEV), ((0, 2), (0, LANES - CONV_CH // N_DEV)))
    (w_in0, cw_all), token0 = all_gather_many([travelling("w_in", 0, True), cw], name="ag_w_in0_conv_w")
    travel_order = [(0, "mid"), (0, "ffn"), (1, "in"), (1, "mid"), (1, "ffn")]
    travel_groups = []
    for l, group in travel_order:
        shards = [_ordered_after(travelling(n, l, transposed_of[n]), token0.astype(BF16)) for n in WEIGHT_GROUPS[group]]
        lands = [lax.dynamic_update_slice(lax.empty((N_DEV,) + s.shape, BF16), s[None], (dev, 0, 0)) for s in shards]
        travel_groups.append((shards, lands))
    travel_states, travel_token = gather_start(travel_groups, name="ag_weights_start")
    travelling_state = dict(zip(travel_order, travel_states))
    forward_at = {(0, "attn"): [(0, "mid")], (0, "mid"): [(0, "ffn"), (1, "in")], (1, "attn"): [(1, "mid"), (1, "ffn")]}

    def reached(l, stage, marker):
        keys = forward_at.get((l, stage))
        if keys:
            gather_forward([travelling_state[k] for k in keys], marker,
                           name="ag_weights_forward_" + "_".join(f"{g}{ll}" for ll, g in keys))

    def weights_of(l, group, marker):
        if (l, group) == (0, "in"):
            return whole(WEIGHT_GROUPS[group], [w_in0])
        gathered = gather_finish(travelling_state[(l, group)], marker, name=f"ag_weights_finish_{group}{l}")
        return whole(WEIGHT_GROUPS[group], gathered)

    cw_full = cw_all[:, :DEPTH * CONV_K, :CONV_CH // N_DEV].reshape(N_DEV, DEPTH, CONV_K, CONV_CH // N_DEV)
    cw_full = jnp.transpose(cw_full, (1, 2, 0, 3)).reshape(DEPTH, CONV_K, CONV_CH)
    smalls = []
    for l in range(DEPTH):
        sl = {n: P[n][l] if n == "sinks" else P[n][l:l + 1] for n in SMALL}
        sl["conv_w"] = jnp.pad(cw_full[l], ((0, CONV_HALO - CONV_K), (0, 0)))
        smalls.append(sl)
    smalls[0]["norm_mix_g"] = _ordered_after(smalls[0]["norm_mix_g"], travel_token)

    ck_idx = jnp.stack([ci] + [2 * px + py for px, py in _other_chips(xi, yi)]).astype(jnp.int32)
    kc_idx = jnp.stack([2 * xi + yi, ci]).astype(jnp.int32)
    got, flight, reduced = {}, {}, {}

    def as_parts(gb):
        keys = sorted(gb)
        return keys, [gb[k].reshape(N_DEV, rows_of[k[1]] // N_DEV, D_MODEL) for k in keys]

    def lands_like(parts, blocks, dtype):
        return [lax.empty((blocks,) + p.shape[1:], dtype) for p in parts]

    def to_sibling(group, gb):
        keys, parts = as_parts(gb)
        flight[group] = (keys, split_start(parts, lands_like(parts, 4, F32), _sibling_plan, 4,
                                           name=f"rs_sibling_{group}_start"))
        return flight[group][1][4]

    def to_chips(group, marker):
        keys, started = flight[group]
        parts, from_sibling = split_wait(started, marker, _sibling_plan, 4, name=f"rs_sibling_{group}_wait")
        chip_sums = sum_for_chips(parts, from_sibling, ck_idx, name=f"rs_sum_for_chips_{group}")
        started = split_start(chip_sums, lands_like(parts, 3, BF16), _chips_plan, 3, name=f"rs_chips_{group}_start")
        flight[group] = (keys, parts, from_sibling, started)
        return started[4]

    def finish(group, marker):
        keys, parts, from_sibling, started = flight[group]
        _, from_chips = split_wait(started, marker, _chips_plan, 3, name=f"rs_chips_{group}_wait")
        reduced.update(zip(keys, sum_final(parts, from_sibling, from_chips, kc_idx, name=f"rs_sum_final_{group}")))

    def stage_done(l, stage, gb, marker):
        gb = {(l, n): g for n, g in gb.items()}
        if l == 1:
            got.update(gb)
            return to_sibling("l1", got) if stage == "mix" else None
        if stage == "ffn":
            return to_chips("l1", marker) + to_sibling("ffn", gb)
        if stage == "mid":
            return to_chips("ffn", marker) + to_sibling("mid", gb)
        if stage == "attn":
            return to_chips("mid", marker)
        if stage == "in":
            return to_sibling("in", gb)
        to_chips("in", marker)
        for group in ("l1", "ffn", "mid"):
            finish(group, marker)
        return None

    loss_part, grad_x, gss = _local_step(x2d, mem2d, tgt2d, weights_of, reached, smalls, stage_done)
    loss = lax.psum(loss_part, ("x", "y", "c"))

    small_names = SMALL + ("conv_w",)
    small_shapes = [(CONV_K, CONV_CH) if n == "conv_w" else P[n].shape[1:] for n in small_names]
    small_parts = pack_small([[gss[l][n] for l in range(DEPTH)] for n in small_names], small_shapes)
    small_land = lax.dynamic_update_slice(lax.empty((N_DEV,) + small_parts.shape, F32), small_parts[None], (dev, 0, 0))
    small_flight = split_start([small_parts], [small_land], _all_peers_plan, N_DEV - 1, name="ag_small_grads_start")

    grads, delta, new_m, new_v = {}, {}, {}, {}

    def update(n, transposed):
        shape = P[n].shape
        two_d = lambda a: a.reshape(shape[0] * shape[1], shape[2])
        grads[n] = jnp.stack([reduced[(l, n)].T if transposed else reduced[(l, n)] for l in range(DEPTH)])
        d_, m_, v_ = adamw(two_d(P[n]), two_d(grads[n]), two_d(M[n]), two_d(V[n]), name="adamw_" + n)
        delta[n], new_m[n], new_v[n] = d_.reshape(shape), m_.reshape(shape), v_.reshape(shape)

    for n, _, transposed in BIG:
        if n != "w_in":
            update(n, transposed)
    finish("in", delta["w_down"])
    update("w_in", True)
    small_all = split_wait(small_flight, delta["w_in"], _all_peers_plan, N_DEV - 1, name="ag_small_grads_wait")[1][0]
    g_, d_, m_, v_ = update_small(small_all, small_shapes, [P[n] for n in SMALL], [M[n] for n in SMALL],
                                  [V[n] for n in SMALL], len(SMALL))
    for i, n in enumerate(SMALL):
        grads[n], delta[n], new_m[n], new_v[n] = g_[i], d_[i], m_[i], v_[i]
    cols = CONV_CH // N_DEV
    grads["conv_w"] = lax.dynamic_slice_in_dim(g_[-1], dev * cols, cols, axis=2)
    flat = lambda a: a.reshape(DEPTH * CONV_K, cols)
    d_, m_, v_ = adamw(flat(conv_w), flat(grads["conv_w"]), flat(m_conv_w), flat(v_conv_w), name="adamw_conv_w")
    delta["conv_w"], new_m["conv_w"], new_v["conv_w"] = (a.reshape(conv_w.shape) for a in (d_, m_, v_))

    return (loss, grad_x[None], *[grads[n] for n in order], *[delta[n] for n in order],
            *[new_m[n] for n in order], *[new_v[n] for n in order])
```

```python
import functools

import jax
import jax.numpy as jnp
import numpy as np
from jax import lax
from jax.experimental import pallas as pl
from jax.experimental.pallas import tpu as pltpu

F32 = jnp.float32
BF16 = jnp.bfloat16

D_MODEL = 1024
HEAD_DIM = 64
N_Q_HEADS = 8
N_KV_HEADS = 2
GROUP = N_Q_HEADS // N_KV_HEADS
ATTN_WIDTH = N_Q_HEADS * HEAD_DIM
KV_WIDTH = N_KV_HEADS * HEAD_DIM
QKV_WIDTH = ATTN_WIDTH + 2 * KV_WIDTH
CONV_CH = 512
IN_COLS = QKV_WIDTH + 2 * CONV_CH
CONV_K = 31
CONV_HALO = 32
BLOCK = 128
N_X_HEADS = 4
X_HEAD_DIM = 256
D_FF = 2816
EPS = 1e-6
NEG = -1e30
DEPTH = 2
N_DEV = 8

ADAM_LR = 0.001
ADAM_B1 = 0.9
ADAM_B2 = 0.999
ADAM_EPS = 1e-08
ADAM_WD = 0.01
ADAM_STEP = 10

V7X_VMEM_LIMIT = 56 * 1024 * 1024
LANES = 128

MESH = pl.DeviceIdType.MESH


def _cp(**kw):
    return pltpu.CompilerParams(vmem_limit_bytes=V7X_VMEM_LIMIT, **kw)


def _dot(a, b, dims):
    return lax.dot_general(a.astype(BF16), b.astype(BF16), (dims, ((), ())), preferred_element_type=F32)


def _dot_nn(a, b):
    return _dot(a, b, ((1,), (0,)))


def _dot_nt(a, b):
    return _dot(a, b, ((1,), (1,)))


def _dot_tn(a, b):
    return _dot(a, b, ((0,), (0,)))


def _sigmoid(x):
    return jax.nn.sigmoid(x)


def _rms(x):
    r = lax.rsqrt(jnp.mean(x * x, axis=-1, keepdims=True) + EPS)
    return x * r, r


def _rms_bwd(dy, xhat, r, g):
    dxh = dy * g
    return r * (dxh - xhat * jnp.mean(dxh * xhat, axis=-1, keepdims=True))


def rms_fwd(x, g, *, tm=512):
    m, d = x.shape
    tm = min(tm, m)

    def body(x_ref, g_ref, o_ref):
        xh, _ = _rms(x_ref[...])
        o_ref[...] = (xh * g_ref[...]).astype(o_ref.dtype)

    return pl.pallas_call(
        body, name="rms_fwd", grid=(m // tm,),
        in_specs=[pl.BlockSpec((tm, d), lambda i: (i, 0)), pl.BlockSpec((1, d), lambda i: (0, 0))],
        out_specs=pl.BlockSpec((tm, d), lambda i: (i, 0)),
        out_shape=jax.ShapeDtypeStruct((m, d), BF16), compiler_params=_cp(),
    )(x, g.reshape(1, d))


def rms_gain_bwd(dh, x, *, tm=512):
    m, d = x.shape
    tm = min(tm, m)

    def body(dh_ref, x_ref, dg_ref):
        @pl.when(pl.program_id(0) == 0)
        def _():
            dg_ref[...] = jnp.zeros_like(dg_ref)

        dg_ref[...] += jnp.sum(dh_ref[...] * _rms(x_ref[...])[0], axis=0, keepdims=True)

    row = pl.BlockSpec((tm, d), lambda i: (i, 0))
    return pl.pallas_call(
        body, name="rms_gain_bwd", grid=(m // tm,), in_specs=[row, row],
        out_specs=pl.BlockSpec((1, d), lambda i: (0, 0)),
        out_shape=jax.ShapeDtypeStruct((1, d), F32), compiler_params=_cp(),
    )(dh, x)


def _tile(n, cap):
    if n <= cap:
        return n
    best = None
    for t in range(LANES, cap + 1, LANES):
        if n % t == 0:
            best = t
    assert best is not None, (n, cap)
    return best


def mm(a, b, *, trans_b, out_dtype, tm=1024, tn_cap=1536, name):
    m, k = a.shape
    n = b.shape[0] if trans_b else b.shape[1]
    assert (b.shape[1] if trans_b else b.shape[0]) == k
    tm = min(tm, m)
    tn = _tile(n, tn_cap)

    def body(a_ref, b_ref, o_ref):
        acc = _dot_nt(a_ref[...], b_ref[...]) if trans_b else _dot_nn(a_ref[...], b_ref[...])
        o_ref[...] = acc.astype(o_ref.dtype)

    b_spec = pl.BlockSpec((tn, k), lambda i, j: (j, 0)) if trans_b else pl.BlockSpec((k, tn), lambda i, j: (0, j))
    return pl.pallas_call(
        body, name=name, grid=(m // tm, n // tn),
        in_specs=[pl.BlockSpec((tm, k), lambda i, j: (i, 0)), b_spec],
        out_specs=pl.BlockSpec((tm, tn), lambda i, j: (i, j)),
        out_shape=jax.ShapeDtypeStruct((m, n), out_dtype), compiler_params=_cp(),
    )(a, b)


def mm_tn(a, b, *, name, ta_cap=1536, tb_cap=1024, tk=2048):
    m, ka = a.shape
    nb = b.shape[1]
    assert b.shape[0] == m
    tk = min(tk, m)
    ta = _tile(ka, ta_cap)
    tb = _tile(nb, tb_cap)

    def body(a_ref, b_ref, o_ref):
        @pl.when(pl.program_id(2) == 0)
        def _():
            o_ref[...] = jnp.zeros_like(o_ref)

        o_ref[...] += _dot_tn(a_ref[...], b_ref[...])

    return pl.pallas_call(
        body, name=name, grid=(ka // ta, nb // tb, m // tk),
        in_specs=[pl.BlockSpec((tk, ta), lambda i, j, kk: (kk, i)), pl.BlockSpec((tk, tb), lambda i, j, kk: (kk, j))],
        out_specs=pl.BlockSpec((ta, tb), lambda i, j, kk: (i, j)),
        out_shape=jax.ShapeDtypeStruct((ka, nb), F32), compiler_params=_cp(),
    )(a, b)


def _whole(shape):
    return pl.BlockSpec(shape, lambda i: (0,) * len(shape), pipeline_mode=pl.Buffered(1))


def _rows(tm, n):
    return pl.BlockSpec((tm, n), lambda i: (i, 0))


def _vec(n):
    return pl.BlockSpec((1, n), lambda i: (0, 0))


def _chunks(n, cap=1408):
    size = _tile(n, cap)
    return [(s, size) for s in range(0, n, size)]


def _zero_at_first_step(*refs):
    @pl.when(pl.program_id(0) == 0)
    def _():
        for r in refs:
            r[...] = jnp.zeros_like(r)


def norm_proj(x, g, wt, *, tm=512):
    m, d = x.shape
    n = wt.shape[0]

    def body(x_ref, g_ref, wt_ref, h_ref, u_ref):
        h = (_rms(x_ref[...])[0] * g_ref[...]).astype(BF16)
        h_ref[...] = h
        for s, sz in _chunks(n):
            u_ref[:, s:s + sz] = _dot_nt(h, wt_ref[s:s + sz, :])

    return pl.pallas_call(
        body, name="norm_proj", grid=(m // tm,),
        in_specs=[_rows(tm, d), _vec(d), _whole((n, d))],
        out_specs=[_rows(tm, d), _rows(tm, n)],
        out_shape=[jax.ShapeDtypeStruct((m, d), BF16), jax.ShapeDtypeStruct((m, n), F32)],
        compiler_params=_cp(),
    )(x, g.reshape(1, d), wt)


def _xattn_heads(q_ref, kv_ref, qg_v, kg_v, d):
    out = []
    for h in range(N_X_HEADS):
        cols = slice(h * X_HEAD_DIM, (h + 1) * X_HEAD_DIM)
        qh, rq = _rms(q_ref[:, cols])
        qn = qh * qg_v
        kn = _rms(kv_ref[:, cols])[0] * kg_v
        v = kv_ref[:, d + h * X_HEAD_DIM:d + (h + 1) * X_HEAD_DIM]
        out.append((qh, rq, qn, kn, v, _xattn_probs(qn, kn)))
    return out


def mid_fwd(mixed, x0, w_out, g_x, wq, kv, xqg, xkg, wo, g_f, *, tm=512):
    m, d = x0.shape
    n_mem = kv.shape[0]

    def body(mixed_ref, x0_ref, w_out_ref, g_x_ref, wq_ref, kv_ref, xqg_ref, xkg_ref, wo_ref, g_f_ref,
             x1_ref, h1_ref, qx_ref, o_ref, x2_ref, h2_ref):
        x1 = x0_ref[...] + _dot_nn(mixed_ref[...], w_out_ref[...])
        x1_ref[...] = x1
        h1 = (_rms(x1)[0] * g_x_ref[...]).astype(BF16)
        h1_ref[...] = h1
        qx_ref[...] = _dot_nn(h1, wq_ref[...])
        for h, (_, _, _, _, v, p) in enumerate(_xattn_heads(qx_ref, kv_ref, xqg_ref[...], xkg_ref[...], d)):
            o_ref[:, h * X_HEAD_DIM:(h + 1) * X_HEAD_DIM] = _dot_nn(p, v).astype(o_ref.dtype)
        x2 = x1 + _dot_nn(o_ref[...], wo_ref[...])
        x2_ref[...] = x2
        h2_ref[...] = (_rms(x2)[0] * g_f_ref[...]).astype(BF16)

    sq = _whole((d, d))
    f32_rows, bf_rows = jax.ShapeDtypeStruct((m, d), F32), jax.ShapeDtypeStruct((m, d), BF16)
    return pl.pallas_call(
        body, name="mid_fwd", grid=(m // tm,),
        in_specs=[_rows(tm, d), _rows(tm, d), sq, _vec(d), sq, _whole((n_mem, 2 * d)), _vec(X_HEAD_DIM), _vec(X_HEAD_DIM),
                  sq, _vec(d)],
        out_specs=[_rows(tm, d)] * 6,
        out_shape=[f32_rows, bf_rows, f32_rows, bf_rows, f32_rows, bf_rows],
        compiler_params=_cp(),
    )(mixed, x0, w_out, g_x.reshape(1, d), wq, kv, xqg.reshape(1, X_HEAD_DIM), xkg.reshape(1, X_HEAD_DIM), wo,
      g_f.reshape(1, d))


def ffn_fwd(h2, x2, wt_gu, w_down, target=None, *, tm=256):
    m, d = x2.shape
    f = w_down.shape[0]
    with_loss = target is not None

    def body(*refs):
        if with_loss:
            h2_ref, x2_ref, wt_gu_ref, w_down_ref, t_ref, gu_ref, a_ref, dy_ref, l_ref = refs
        else:
            h2_ref, x2_ref, wt_gu_ref, w_down_ref, gu_ref, a_ref, x3_ref = refs
        h = h2_ref[...]
        for s, sz in _chunks(2 * f):
            gu_ref[:, s:s + sz] = _dot_nt(h, wt_gu_ref[s:s + sz, :])
        for s, sz in _chunks(f):
            g = gu_ref[:, s:s + sz]
            a_ref[:, s:s + sz] = (g * _sigmoid(g) * gu_ref[:, f + s:f + s + sz]).astype(a_ref.dtype)
        x3 = x2_ref[...] + _dot_nn(a_ref[...], w_down_ref[...])
        if not with_loss:
            x3_ref[...] = x3
            return
        err = x3 - t_ref[...]
        dy_ref[...] = err * (1.0 / d)
        _zero_at_first_step(l_ref)
        part = jnp.sum(jnp.sum(err * err, axis=-1, keepdims=True), axis=0, keepdims=True)
        l_ref[...] += jnp.broadcast_to(part * (0.5 / d), l_ref.shape)

    last = [_rows(tm, d), pl.BlockSpec((1, LANES), lambda i: (0, 0))] if with_loss else [_rows(tm, d)]
    last_shape = [jax.ShapeDtypeStruct((m, d), F32)] + ([jax.ShapeDtypeStruct((1, LANES), F32)] if with_loss else [])
    return pl.pallas_call(
        body, name="ffn_fwd_loss" if with_loss else "ffn_fwd", grid=(m // tm,),
        in_specs=[_rows(tm, d), _rows(tm, d), _whole((2 * f, d)), _whole((f, d))] + ([_rows(tm, d)] if with_loss else []),
        out_specs=[_rows(tm, 2 * f), _rows(tm, f)] + last,
        out_shape=[jax.ShapeDtypeStruct((m, 2 * f), F32), jax.ShapeDtypeStruct((m, f), BF16)] + last_shape,
        compiler_params=_cp(),
    )(*([h2, x2, wt_gu, w_down] + ([target] if with_loss else [])))


def ffn_bwd(dx3, gu, x2, g_f, w_down, wt_gu, *, tm=256):
    m, d = x2.shape
    f = w_down.shape[0]

    def body(dx3_ref, gu_ref, x2_ref, g_ref, w_down_ref, wt_gu_ref, dgu_ref, dx2_ref, dg_ref):
        _zero_at_first_step(dg_ref)
        dx3 = dx3_ref[...]
        dx3_b = dx3.astype(BF16)
        for s, sz in _chunks(f):
            da = _dot_nt(dx3_b, w_down_ref[s:s + sz, :])
            g = gu_ref[:, s:s + sz]
            u = gu_ref[:, f + s:f + s + sz]
            sg = _sigmoid(g)
            dgu_ref[:, s:s + sz] = (da * u * (sg * (1.0 + g * (1.0 - sg)))).astype(dgu_ref.dtype)
            dgu_ref[:, f + s:f + s + sz] = (da * (g * sg)).astype(dgu_ref.dtype)
        dh2 = _dot_nn(dgu_ref[...], wt_gu_ref[...])
        xh, r = _rms(x2_ref[...])
        dg_ref[...] += jnp.sum(dh2 * xh, axis=0, keepdims=True)
        dx2_ref[...] = dx3 + _rms_bwd(dh2, xh, r, g_ref[...])

    return pl.pallas_call(
        body, name="ffn_bwd", grid=(m // tm,),
        in_specs=[_rows(tm, d), _rows(tm, 2 * f), _rows(tm, d), _vec(d), _whole((f, d)), _whole((2 * f, d))],
        out_specs=[_rows(tm, 2 * f), _rows(tm, d), _vec(d)],
        out_shape=[jax.ShapeDtypeStruct((m, 2 * f), BF16), jax.ShapeDtypeStruct((m, d), F32),
                   jax.ShapeDtypeStruct((1, d), F32)],
        compiler_params=_cp(),
    )(dx3, gu, x2, g_f.reshape(1, d), w_down, wt_gu)


def mid_bwd(dx2, qx, kv, xqg, xkg, x1, g_x, wo, wq, w_out, *, tm=512):
    m, d = x1.shape
    n_mem = kv.shape[0]
    nt = m // tm

    def body(dx2_ref, qx_ref, kv_ref, xqg_ref, xkg_ref, x1_ref, g_x_ref, wo_ref, wq_ref, w_out_ref,
             dq_ref, dx1_ref, dmixed_ref, dkv_ref, dqg_ref, dkg_ref, dg_ref):
        i = pl.program_id(0)
        _zero_at_first_step(dkv_ref, dqg_ref, dkg_ref, dg_ref)
        qg_v, kg_v = xqg_ref[...], xkg_ref[...]
        dx2 = dx2_ref[...]
        do = _dot_nt(dx2, wo_ref[...])
        dqg_acc = jnp.zeros((1, X_HEAD_DIM), F32)
        for h, (qh, rq, qn, kn, v, p) in enumerate(_xattn_heads(qx_ref, kv_ref, qg_v, kg_v, d)):
            cols = slice(h * X_HEAD_DIM, (h + 1) * X_HEAD_DIM)
            vcols = slice(d + h * X_HEAD_DIM, d + (h + 1) * X_HEAD_DIM)
            do_h = do[:, cols]
            dp = _dot_nt(do_h, v)
            ds = p * (dp - jnp.sum(p * dp, axis=-1, keepdims=True))
            dkv_ref[:, vcols] += _dot_tn(p, do_h)
            dqn = _dot_nn(ds, kn) * (X_HEAD_DIM ** -0.5)
            dkv_ref[:, cols] += _dot_tn(ds, qn) * (X_HEAD_DIM ** -0.5)
            dqg_acc = dqg_acc + jnp.sum(dqn * qh, axis=0, keepdims=True)
            dq_ref[:, cols] = _rms_bwd(dqn, qh, rq, qg_v).astype(dq_ref.dtype)
        dqg_ref[...] += dqg_acc
        dh1 = _dot_nt(dq_ref[...], wq_ref[...])
        xh, r = _rms(x1_ref[...])
        dg_ref[...] += jnp.sum(dh1 * xh, axis=0, keepdims=True)
        dx1 = dx2 + _rms_bwd(dh1, xh, r, g_x_ref[...])
        dx1_ref[...] = dx1
        dmixed_ref[...] = _dot_nt(dx1, w_out_ref[...])

        @pl.when(i == nt - 1)
        def _():
            dkg_acc = jnp.zeros((1, X_HEAD_DIM), F32)
            for h in range(N_X_HEADS):
                cols = slice(h * X_HEAD_DIM, (h + 1) * X_HEAD_DIM)
                kh, rk = _rms(kv_ref[:, cols])
                dkn = dkv_ref[:, cols]
                dkg_acc = dkg_acc + jnp.sum(dkn * kh, axis=0, keepdims=True)
                dkv_ref[:, cols] = _rms_bwd(dkn, kh, rk, kg_v)
            dkg_ref[...] = dkg_acc

    sq = _whole((d, d))
    full = pl.BlockSpec((n_mem, 2 * d), lambda i: (0, 0))
    return pl.pallas_call(
        body, name="mid_bwd", grid=(nt,),
        in_specs=[_rows(tm, d), _rows(tm, d), _whole((n_mem, 2 * d)), _vec(X_HEAD_DIM), _vec(X_HEAD_DIM), _rows(tm, d),
                  _vec(d), sq, sq, sq],
        out_specs=[_rows(tm, d), _rows(tm, d), _rows(tm, d), full, _vec(X_HEAD_DIM), _vec(X_HEAD_DIM), _vec(d)],
        out_shape=[jax.ShapeDtypeStruct((m, d), BF16), jax.ShapeDtypeStruct((m, d), F32), jax.ShapeDtypeStruct((m, d), F32),
                   jax.ShapeDtypeStruct((n_mem, 2 * d), F32), jax.ShapeDtypeStruct((1, X_HEAD_DIM), F32),
                   jax.ShapeDtypeStruct((1, X_HEAD_DIM), F32), jax.ShapeDtypeStruct((1, d), F32)],
        compiler_params=_cp(),
    )(dx2, qx, kv, xqg.reshape(1, X_HEAD_DIM), xkg.reshape(1, X_HEAD_DIM), x1, g_x.reshape(1, d), wo, wq, w_out)


def in_bwd(du, wt_in, x0, g_mix, dx1, *, tm=512):
    m, d = x0.shape
    n = wt_in.shape[0]

    def body(du_ref, wt_ref, x0_ref, g_ref, dx1_ref, dx0_ref, dg_ref):
        _zero_at_first_step(dg_ref)
        dh0 = _dot_nn(du_ref[...], wt_ref[...])
        xh, r = _rms(x0_ref[...])
        dg_ref[...] += jnp.sum(dh0 * xh, axis=0, keepdims=True)
        dx0_ref[...] = dx1_ref[...] + _rms_bwd(dh0, xh, r, g_ref[...])

    return pl.pallas_call(
        body, name="in_bwd", grid=(m // tm,),
        in_specs=[_rows(tm, n), _whole((n, d)), _rows(tm, d), _vec(d), _rows(tm, d)],
        out_specs=[_rows(tm, d), _vec(d)],
        out_shape=[jax.ShapeDtypeStruct((m, d), F32), jax.ShapeDtypeStruct((1, d), F32)],
        compiler_params=_cp(),
    )(du, wt_in, x0, g_mix.reshape(1, d), dx1)


SWA_TILE = 512
SWA_SUB = SWA_TILE // BLOCK
SWA_KEYS = SWA_TILE + BLOCK
PAIR = 2 * HEAD_DIM
KCOL = ATTN_WIDTH
VCOL = ATTN_WIDTH + KV_WIDTH


def _swa_constants():
    r = np.arange(2 * BLOCK)[:, None]
    j = np.arange(4 * BLOCK)[None, :]
    dist = (r % BLOCK) + BLOCK - (j % (2 * BLOCK))
    valid = (dist >= 0) & (dist < BLOCK)
    first_valid = valid & ((j % (2 * BLOCK)) >= BLOCK)
    bias, bias_first = [], []
    for kv in range(N_KV_HEADS):
        head = kv * GROUP + 2 * (r // BLOCK) + j // (2 * BLOCK)
        b = -(2.0 ** -(head + 1.0)) * dist
        bias.append(np.where(valid, b, NEG))
        bias_first.append(np.where(first_valid, b, NEG))
    lane = np.arange(LANES)
    seg = (lane[:, None] // HEAD_DIM == lane[None, :] // HEAD_DIM) / HEAD_DIM
    row = np.arange(4 * BLOCK)[:, None]
    ones = (row // (2 * BLOCK)) == (lane[None, :] // HEAD_DIM)
    return (jnp.asarray(np.stack(bias), F32), jnp.asarray(np.stack(bias_first), F32), jnp.asarray(seg, BF16),
            jnp.asarray(ones, BF16))


def _segmean(x, seg_ref):
    hi = x.astype(BF16)
    lo = (x - hi.astype(F32)).astype(BF16)
    return _dot_nn(hi, seg_ref[...]) + _dot_nn(lo, seg_ref[...])


def _two_heads(x, kv):
    lane = lax.broadcasted_iota(jnp.int32, (1, LANES), 1)
    mine = (lane < HEAD_DIM) if kv == 0 else (lane >= HEAD_DIM)
    base = jnp.where(mine, x, 0.0)
    other = pltpu.roll(base, HEAD_DIM, 1)
    return jnp.concatenate([base, other] if kv == 0 else [other, base], axis=0)


def _from_two_heads(y, kv):
    rows = y.shape[0] // 2
    lane = lax.broadcasted_iota(jnp.int32, (1, LANES), 1)
    top, bot = y[:rows], y[rows:]
    if kv == 0:
        return jnp.where(lane < HEAD_DIM, top + pltpu.roll(bot, HEAD_DIM, 1), 0.0)
    return jnp.where(lane >= HEAD_DIM, pltpu.roll(top, HEAD_DIM, 1) + bot, 0.0)


def _pair_rows(ref, rows, kv):
    c = kv * 2 * PAIR
    return jnp.concatenate([ref[rows, c:c + PAIR], ref[rows, c + PAIR:c + 2 * PAIR]], axis=0)


def _head_cols(fn, kv):
    return [jnp.concatenate([fn(kv * GROUP + half), fn(kv * GROUP + 2 + half)], axis=0) for half in range(2)]


def _swa_prologue(cur_ref, prev_ref, qg_ref, kg_ref, seg_ref, qg_s, kn_s, v_s):
    qg_s[...] = (cur_ref[:, 0:ATTN_WIDTH] * qg_ref[...]).astype(BF16)
    k = jnp.concatenate([prev_ref[:, KCOL:KCOL + KV_WIDTH], cur_ref[:, KCOL:KCOL + KV_WIDTH]], axis=0)
    kn_s[...] = k * lax.rsqrt(_segmean(k * k, seg_ref) + EPS) * kg_ref[...]
    v_s[0:BLOCK, :] = prev_ref[:, VCOL:VCOL + KV_WIDTH]
    v_s[BLOCK:SWA_KEYS, :] = cur_ref[:, VCOL:VCOL + KV_WIDTH]


def _swa_products(qg_s, kn_s, rows, keys, kv):
    q2 = _pair_rows(qg_s, rows, kv)
    k2 = _two_heads(kn_s[keys, :], kv)
    return q2, k2, _dot_nt(q2, k2)


def _swa_scores(cur_ref, sinks_ref, qg_s, kn_s, bias, rows, keys, kv):
    q2, k2, t = _swa_products(qg_s, kn_s, rows, keys, kv)
    return q2, k2, t, _swa_softmax(cur_ref, sinks_ref, t, bias, rows, kv)


def _swa_softmax(cur_ref, sinks_ref, t, bias, rows, kv):
    def rq(h):
        x = cur_ref[rows, h * HEAD_DIM:(h + 1) * HEAD_DIM]
        return lax.rsqrt(jnp.mean(x * x, axis=-1, keepdims=True) + EPS)

    scale = _head_cols(lambda h: rq(h) * (HEAD_DIM ** -0.5), kv)
    sink = _head_cols(lambda h: jnp.full((BLOCK, 1), sinks_ref[h], F32), kv)
    halves = []
    for half in range(2):
        cols = slice(half * 2 * BLOCK, (half + 1) * 2 * BLOCK)
        s = t[:, cols] * scale[half] + bias[:, cols]
        mx = jnp.maximum(jnp.max(s, axis=-1, keepdims=True), sink[half])
        halves.append((scale[half], jnp.exp(s - mx), jnp.exp(sink[half] - mx)))
    return halves


def swa_fwd(u, qg, kg, sinks):
    t_rows = u.shape[0]
    nt = t_rows // SWA_TILE
    bias_c, bias_first_c, seg_c, ones_c = _swa_constants()

    def body(sinks_ref, cur_ref, prev_ref, qg_ref, kg_ref, seg_ref, bias_ref, biasf_ref, ones_ref, o_ref, qg_s, kn_s, v_s):
        i = pl.program_id(0)
        _swa_prologue(cur_ref, prev_ref, qg_ref, kg_ref, seg_ref, qg_s, kn_s, v_s)
        lane = lax.broadcasted_iota(jnp.int32, (1, LANES), 1)
        work = [(b, kv, slice(b * BLOCK, (b + 1) * BLOCK), slice(b * BLOCK, (b + 2) * BLOCK))
                for b in range(SWA_SUB) for kv in range(N_KV_HEADS)]
        products = [_swa_products(qg_s, kn_s, rows, keys, kv)[2] for _, kv, rows, keys in work]
        scored = []
        for (b, kv, rows, _), t in zip(work, products):
            bias = jnp.where(i == 0, biasf_ref[kv], bias_ref[kv]) if b == 0 else bias_ref[kv]
            halves = _swa_softmax(cur_ref, sinks_ref, t, bias, rows, kv)
            scored.append((jnp.concatenate([halves[0][1], halves[1][1]], axis=1).astype(BF16), halves[0][2], halves[1][2]))
        for (b, kv, rows, keys), (e, es0, es1) in zip(work, scored):
            v2 = jnp.concatenate([_two_heads(v_s[keys, :], kv).astype(BF16), ones_ref[...]], axis=1)
            ox = _dot_nn(e, v2)
            den = ox[:, LANES:] + jnp.where(lane < HEAD_DIM, es0, es1)
            out = (ox[:, :LANES] / den).astype(o_ref.dtype)
            c = kv * 2 * PAIR
            o_ref[rows, c:c + PAIR] = out[:BLOCK]
            o_ref[rows, c + PAIR:c + 2 * PAIR] = out[BLOCK:]

    const3 = pl.BlockSpec((N_KV_HEADS, 2 * BLOCK, 4 * BLOCK), lambda i: (0, 0, 0))
    return pl.pallas_call(
        body, name="swa_fwd", grid=(nt,),
        in_specs=[
            pl.BlockSpec(memory_space=pltpu.SMEM),
            pl.BlockSpec((SWA_TILE, QKV_WIDTH), lambda i: (i, 0)),
            pl.BlockSpec((BLOCK, QKV_WIDTH), lambda i: (jnp.maximum(i * SWA_SUB - 1, 0), 0)),
            pl.BlockSpec((1, ATTN_WIDTH), lambda i: (0, 0)), pl.BlockSpec((1, KV_WIDTH), lambda i: (0, 0)),
            pl.BlockSpec((LANES, LANES), lambda i: (0, 0)), const3, const3,
            pl.BlockSpec((4 * BLOCK, LANES), lambda i: (0, 0)),
        ],
        out_specs=pl.BlockSpec((SWA_TILE, ATTN_WIDTH), lambda i: (i, 0)),
        out_shape=jax.ShapeDtypeStruct((t_rows, 2 * ATTN_WIDTH), BF16),
        scratch_shapes=[pltpu.VMEM((SWA_TILE, ATTN_WIDTH), BF16), pltpu.VMEM((SWA_KEYS, KV_WIDTH), F32),
                        pltpu.VMEM((SWA_KEYS, KV_WIDTH), F32)],
        compiler_params=_cp(),
    )(sinks, u, u, jnp.tile(qg, N_Q_HEADS).reshape(1, ATTN_WIDTH), jnp.tile(kg, N_KV_HEADS).reshape(1, KV_WIDTH),
      seg_c, bias_c, bias_first_c, ones_c)


def swa_bwd(u, dmixed, qg, kg, sinks):
    t_rows = u.shape[0]
    nt = t_rows // SWA_TILE
    bias_c, bias_first_c, seg_c, _ = _swa_constants()

    def body(sinks_ref, cur_ref, prev_ref, do_ref, qg_ref, kg_ref, seg_ref, bias_ref, biasf_ref,
             du_ref, dqg_ref, dkg_ref, dsk_ref, qg_s, kn_s, v_s, acck_s, accv_s, carryk_s, carryv_s):
        step = pl.program_id(0)
        i = nt - 1 - step

        @pl.when(step == 0)
        def _():
            for r in (carryk_s, carryv_s, dqg_ref, dkg_ref, dsk_ref):
                r[...] = jnp.zeros_like(r)

        _swa_prologue(cur_ref, prev_ref, qg_ref, kg_ref, seg_ref, qg_s, kn_s, v_s)
        for acc, carry in ((acck_s, carryk_s), (accv_s, carryv_s)):
            acc[0:SWA_TILE, :] = jnp.zeros((SWA_TILE, KV_WIDTH), F32)
            acc[SWA_TILE:SWA_KEYS, :] = carry[...]

        lane = lax.broadcasted_iota(jnp.int32, (1, LANES), 1)
        g_pair = qg_ref[:, 0:PAIR]
        dqg_acc = jnp.zeros((1, PAIR), F32)
        dsk_acc = jnp.zeros((1, LANES), F32)
        work = [(b, kv, slice(b * BLOCK, (b + 1) * BLOCK), slice(b * BLOCK, (b + 2) * BLOCK))
                for b in range(SWA_SUB) for kv in range(N_KV_HEADS)]
        products = []
        for _, kv, rows, keys in work:
            q2, k2, t = _swa_products(qg_s, kn_s, rows, keys, kv)
            do2 = _pair_rows(do_ref, rows, kv).astype(BF16)
            products.append((q2, k2, t, do2, _dot_nt(do2, _two_heads(v_s[keys, :], kv))))
        softmaxed = []
        for (b, kv, rows, _), (_, _, t, _, dp) in zip(work, products):
            bias = jnp.where(i == 0, biasf_ref[kv], bias_ref[kv]) if b == 0 else bias_ref[kv]
            p_parts, dt_parts, coef = [], [], []
            for half, (scale, e, es) in enumerate(_swa_softmax(cur_ref, sinks_ref, t, bias, rows, kv)):
                cols = slice(half * 2 * BLOCK, (half + 1) * 2 * BLOCK)
                rden = 1.0 / (jnp.sum(e, axis=-1, keepdims=True) + es)
                p = e * rden
                dp_h = dp[:, cols]
                delta = jnp.sum(p * dp_h, axis=-1, keepdims=True)
                ds = p * (dp_h - delta)
                dsink = -(es * rden) * delta
                for pair in range(2):
                    part = jnp.sum(dsink[pair * BLOCK:(pair + 1) * BLOCK], axis=0, keepdims=True)
                    dsk_acc = dsk_acc + jnp.where(lane == kv * GROUP + 2 * pair + half, part, 0.0)
                dscale = jnp.sum(ds * t[:, cols], axis=-1, keepdims=True)
                coef.append(-dscale * scale * scale * scale)
                p_parts.append(p.astype(BF16))
                dt_parts.append((ds * scale).astype(BF16))
            softmaxed.append((jnp.concatenate(p_parts, axis=1), jnp.concatenate(dt_parts, axis=1),
                              jnp.where(lane < HEAD_DIM, coef[0], coef[1])))
        for (_, kv, rows, keys), (q2, k2, _, do2, _), (p2, dt, coef) in zip(work, products, softmaxed):
            dqg2 = _dot_nn(dt, k2)
            q_raw = _pair_rows(cur_ref, rows, kv)
            dq = dqg2 * g_pair + coef * q_raw
            dqg_acc = dqg_acc + jnp.sum(dqg2 * q_raw, axis=0, keepdims=True)
            c = kv * 2 * PAIR
            du_ref[rows, c:c + PAIR] = dq[:BLOCK].astype(du_ref.dtype)
            du_ref[rows, c + PAIR:c + 2 * PAIR] = dq[BLOCK:].astype(du_ref.dtype)
            acck_s[keys, :] += _from_two_heads(_dot_tn(dt, q2), kv)
            accv_s[keys, :] += _from_two_heads(_dot_tn(p2, do2), kv)
        dqg_ref[...] += dqg_acc + pltpu.roll(dqg_acc, HEAD_DIM, 1)
        dsk_ref[...] += dsk_acc

        own = slice(BLOCK, SWA_KEYS)
        k = cur_ref[:, KCOL:KCOL + KV_WIDTH]
        rk = lax.rsqrt(_segmean(k * k, seg_ref) + EPS)
        kh = k * rk
        dkn = acck_s[own, :]
        dkh = dkn * kg_ref[...]
        du_ref[:, KCOL:KCOL + KV_WIDTH] = (rk * (dkh - kh * _segmean(dkh * kh, seg_ref))).astype(du_ref.dtype)
        du_ref[:, VCOL:VCOL + KV_WIDTH] = accv_s[own, :].astype(du_ref.dtype)
        dkg_part = jnp.sum(dkn * kh, axis=0, keepdims=True)
        dkg_ref[...] += dkg_part + pltpu.roll(dkg_part, HEAD_DIM, 1)
        carryk_s[...] = acck_s[0:BLOCK, :]
        carryv_s[...] = accv_s[0:BLOCK, :]

    const3 = pl.BlockSpec((N_KV_HEADS, 2 * BLOCK, 4 * BLOCK), lambda s: (0, 0, 0))
    vec = pl.BlockSpec((1, LANES), lambda s: (0, 0))
    return pl.pallas_call(
        body, name="swa_bwd", grid=(nt,),
        in_specs=[
            pl.BlockSpec(memory_space=pltpu.SMEM),
            pl.BlockSpec((SWA_TILE, QKV_WIDTH), lambda s: (nt - 1 - s, 0)),
            pl.BlockSpec((BLOCK, QKV_WIDTH), lambda s: (jnp.maximum((nt - 1 - s) * SWA_SUB - 1, 0), 0)),
            pl.BlockSpec((SWA_TILE, ATTN_WIDTH), lambda s: (nt - 1 - s, 0)),
            pl.BlockSpec((1, ATTN_WIDTH), lambda s: (0, 0)), vec,
            pl.BlockSpec((LANES, LANES), lambda s: (0, 0)), const3, const3,
        ],
        out_specs=[pl.BlockSpec((SWA_TILE, QKV_WIDTH), lambda s: (nt - 1 - s, 0)), vec, vec, vec],
        out_shape=[jax.ShapeDtypeStruct((t_rows, IN_COLS), BF16)] + [jax.ShapeDtypeStruct((1, LANES), F32)] * 3,
        scratch_shapes=[pltpu.VMEM((SWA_TILE, ATTN_WIDTH), BF16)] + [pltpu.VMEM((SWA_KEYS, KV_WIDTH), F32)] * 4
        + [pltpu.VMEM((BLOCK, KV_WIDTH), F32)] * 2,
        compiler_params=_cp(),
    )(sinks, u, u, dmixed, jnp.tile(qg, N_Q_HEADS).reshape(1, ATTN_WIDTH), jnp.tile(kg, N_KV_HEADS).reshape(1, KV_WIDTH),
      seg_c, bias_c, bias_first_c)


CONV_TILE = 512
CONV_CHUNK = 64
VAL0 = QKV_WIDTH
GATE0 = QKV_WIDTH + CONV_CH


def _glu(ref):
    return ref[:, VAL0:GATE0] * _sigmoid(ref[:, GATE0:GATE0 + CONV_CH])


SUBLANES = 8
CONV_BUF = CONV_HALO + CONV_TILE + SUBLANES
CONV_EXT = CONV_HALO + CONV_TILE


def _fill_shifted(sh_ref):
    for r in range(1, SUBLANES):
        sh_ref[r, 0:CONV_EXT, :] = sh_ref[0, pl.ds(r, CONV_EXT), :]


def _shifted(sh_ref, start, offset, n):
    return sh_ref[offset % SUBLANES, pl.ds(start + offset - offset % SUBLANES, n), :]


def _layernorm_stats(y):
    mu = jnp.mean(y, axis=-1, keepdims=True)
    yc = y - mu
    rstd = lax.rsqrt(jnp.mean(yc * yc, axis=-1, keepdims=True) + EPS)
    return yc * rstd, rstd


def conv_fwd(u, mixed, conv_w, conv_b, ln_g, ln_b):
    t = u.shape[0]
    nt = t // CONV_TILE
    per = CONV_TILE // CONV_HALO

    def body(cur_ref, prev_ref, mixed_ref, w_ref, b_ref, g_ref, b2_ref, o_ref, y_ref, gl_ref):
        del mixed_ref
        i = pl.program_id(0)
        gl_ref[0, 0:CONV_HALO, :] = jnp.where(i > 0, _glu(prev_ref), 0.0)
        gl_ref[0, CONV_HALO:CONV_EXT, :] = _glu(cur_ref)
        gl_ref[0, CONV_EXT:CONV_BUF, :] = jnp.zeros((SUBLANES, CONV_CH), F32)
        _fill_shifted(gl_ref)
        for c0 in range(0, CONV_TILE, CONV_CHUNK):
            acc = jnp.broadcast_to(b_ref[...], (CONV_CHUNK, CONV_CH))
            for k in range(CONV_K):
                acc = acc + w_ref[k:k + 1, :] * _shifted(gl_ref, c0, 2 + k, CONV_CHUNK)
            y_ref[c0:c0 + CONV_CHUNK, :] = acc
        yh, _ = _layernorm_stats(y_ref[...])
        yln = yh * g_ref[...] + b2_ref[...]
        o_ref[...] = (yln * _sigmoid(yln)).astype(o_ref.dtype)

    vec = pl.BlockSpec((1, CONV_CH), lambda i: (0, 0))
    return pl.pallas_call(
        body, name="conv_fwd", grid=(nt,),
        in_specs=[
            pl.BlockSpec((CONV_TILE, IN_COLS), lambda i: (i, 0)),
            pl.BlockSpec((CONV_HALO, IN_COLS), lambda i: (jnp.maximum(i * per - 1, 0), 0)),
            pl.BlockSpec(memory_space=pl.ANY),
            pl.BlockSpec((CONV_HALO, CONV_CH), lambda i: (0, 0)),
            vec, vec, vec,
        ],
        out_specs=[pl.BlockSpec((CONV_TILE, CONV_CH), lambda i: (i, 1)), pl.BlockSpec((CONV_TILE, CONV_CH), lambda i: (i, 0))],
        out_shape=[jax.ShapeDtypeStruct(mixed.shape, mixed.dtype), jax.ShapeDtypeStruct((t, CONV_CH), F32)],
        scratch_shapes=[pltpu.VMEM((SUBLANES, CONV_BUF, CONV_CH), F32)],
        input_output_aliases={2: 0}, compiler_params=_cp(),
    )(u, u, mixed, conv_w, conv_b.reshape(1, CONV_CH), ln_g.reshape(1, CONV_CH), ln_b.reshape(1, CONV_CH))


def conv_bwd(u, y, dmixed, du, conv_w, ln_g, ln_b):
    t = u.shape[0]
    nt = t // CONV_TILE
    per = CONV_TILE // CONV_HALO

    def body(cur_ref, prev_ref, y_ref, yn_ref, do_ref, don_ref, du_in_ref, w_ref, g_ref, b2_ref,
             du_ref, dw_ref, dvec_ref, gl_ref, dy_ref):
        i = pl.program_id(0)
        last = i == nt - 1
        _zero_at_first_step(dw_ref, dvec_ref)

        gl_ref[0, 0:CONV_HALO, :] = jnp.where(i > 0, _glu(prev_ref), 0.0)
        gl_ref[0, CONV_HALO:CONV_EXT, :] = _glu(cur_ref)
        gl_ref[0, CONV_EXT:CONV_BUF, :] = jnp.zeros((SUBLANES, CONV_CH), F32)
        _fill_shifted(gl_ref)

        yh, rstd = _layernorm_stats(jnp.concatenate([y_ref[...], yn_ref[...]], axis=0))
        g = g_ref[...]
        yln = yh * g + b2_ref[...]
        sg = _sigmoid(yln)
        dout = jnp.concatenate([do_ref[...], jnp.where(last, 0.0, don_ref[...])], axis=0)
        dyln = dout * (sg * (1.0 + yln * (1.0 - sg)))
        dyh = dyln * g
        dy = rstd * (dyh - jnp.mean(dyh, axis=-1, keepdims=True) - yh * jnp.mean(dyh * yh, axis=-1, keepdims=True))
        dy_ref[0, 0:CONV_EXT, :] = dy
        dy_ref[0, CONV_EXT:CONV_BUF, :] = jnp.zeros((SUBLANES, CONV_CH), F32)
        _fill_shifted(dy_ref)

        own = slice(0, CONV_TILE)
        dvec_ref[0:1, :] += jnp.sum(dy[own], axis=0, keepdims=True)
        dvec_ref[1:2, :] += jnp.sum(dyln[own] * yh[own], axis=0, keepdims=True)
        dvec_ref[2:3, :] += jnp.sum(dyln[own], axis=0, keepdims=True)
        for k in range(CONV_K):
            dw_ref[k:k + 1, :] += jnp.sum(dy[own] * _shifted(gl_ref, 0, 2 + k, CONV_TILE), axis=0, keepdims=True)

        for c0 in range(0, CONV_TILE, CONV_CHUNK):
            acc = jnp.zeros((CONV_CHUNK, CONV_CH), F32)
            for k in range(CONV_K):
                acc = acc + w_ref[k:k + 1, :] * _shifted(dy_ref, c0, CONV_K - 1 - k, CONV_CHUNK)
            rows = slice(c0, c0 + CONV_CHUNK)
            val = cur_ref[rows, VAL0:GATE0]
            sgate = _sigmoid(cur_ref[rows, GATE0:GATE0 + CONV_CH])
            du_ref[rows, VAL0:GATE0] = (acc * sgate).astype(du_ref.dtype)
            du_ref[rows, GATE0:GATE0 + CONV_CH] = (acc * val * sgate * (1.0 - sgate)).astype(du_ref.dtype)
        du_ref[:, 0:QKV_WIDTH] = du_in_ref[:, 0:QKV_WIDTH]

    vec = pl.BlockSpec((1, CONV_CH), lambda i: (0, 0))
    n_halo = t // CONV_HALO
    return pl.pallas_call(
        body, name="conv_bwd", grid=(nt,),
        in_specs=[
            pl.BlockSpec((CONV_TILE, IN_COLS), lambda i: (i, 0)),
            pl.BlockSpec((CONV_HALO, IN_COLS), lambda i: (jnp.maximum(i * per - 1, 0), 0)),
            pl.BlockSpec((CONV_TILE, CONV_CH), lambda i: (i, 0)),
            pl.BlockSpec((CONV_HALO, CONV_CH), lambda i: (jnp.minimum((i + 1) * per, n_halo - 1), 0)),
            pl.BlockSpec((CONV_TILE, CONV_CH), lambda i: (i, 1)),
            pl.BlockSpec((CONV_HALO, CONV_CH), lambda i: (jnp.minimum((i + 1) * per, n_halo - 1), 1)),
            pl.BlockSpec((CONV_TILE, IN_COLS), lambda i: (i, 0)),
            pl.BlockSpec((CONV_HALO, CONV_CH), lambda i: (0, 0)),
            vec, vec,
        ],
        out_specs=[
            pl.BlockSpec((CONV_TILE, IN_COLS), lambda i: (i, 0)),
            pl.BlockSpec((CONV_HALO, CONV_CH), lambda i: (0, 0)),
            pl.BlockSpec((8, CONV_CH), lambda i: (0, 0)),
        ],
        out_shape=[
            jax.ShapeDtypeStruct(du.shape, du.dtype),
            jax.ShapeDtypeStruct((CONV_HALO, CONV_CH), F32),
            jax.ShapeDtypeStruct((8, CONV_CH), F32),
        ],
        scratch_shapes=[pltpu.VMEM((SUBLANES, CONV_BUF, CONV_CH), F32), pltpu.VMEM((SUBLANES, CONV_BUF, CONV_CH), F32)],
        input_output_aliases={6: 0}, compiler_params=_cp(),
    )(u, u, y, y, dmixed, dmixed, du, conv_w, ln_g.reshape(1, CONV_CH), ln_b.reshape(1, CONV_CH))


def _xattn_probs(qn, kn):
    s = _dot_nt(qn, kn) * (X_HEAD_DIM ** -0.5)
    e = jnp.exp(s - jnp.max(s, axis=-1, keepdims=True))
    return e / jnp.sum(e, axis=-1, keepdims=True)


def adamw(w, g, m, v, *, name):
    r, c = w.shape
    tr = r
    for cand in (512, 256, 128, 64, 32, 16, 8):
        if r % cand == 0 and r > cand:
            tr = cand
            break

    def body(w_ref, g_ref, m_ref, v_ref, d_ref, nm_ref, nv_ref):
        d_ref[...], nm_ref[...], nv_ref[...] = _adamw_math(w_ref[...], g_ref[...], m_ref[...], v_ref[...])

    spec = pl.BlockSpec((tr, c), lambda i: (i, 0))
    shape = jax.ShapeDtypeStruct((r, c), F32)
    return pl.pallas_call(
        body, name=name, grid=(r // tr,), in_specs=[spec] * 4, out_specs=[spec] * 3,
        out_shape=[shape] * 3, compiler_params=_cp(),
    )(w, g, m, v)


def _position():
    return lax.axis_index("x"), lax.axis_index("y"), lax.axis_index("c")


def all_gather_many(shards, *, name):
    n = len(shards)

    def body(*refs):
        x_refs, out_refs, token_ref = refs[:n], refs[n:2 * n], refs[2 * n]
        send_sems, recv_sems, local_sems = refs[2 * n + 1:]
        x, y, c = _position()
        me, sibling = (x, y, c), (x, y, 1 - c)
        chips = [(1 - x, y), (x, 1 - y), (1 - x, 1 - y)]
        token_ref[...] = jnp.zeros_like(token_ref)

        def rows(t, px, py, pc):
            return out_refs[t].at[4 * px + 2 * py + pc]

        def copy(t, k, block, to, src=None):
            return pltpu.make_async_remote_copy(
                src_ref=rows(t, *block) if src is None else src, dst_ref=rows(t, *block),
                send_sem=send_sems.at[7 * t + k], recv_sem=recv_sems.at[7 * t + k], device_id=to, device_id_type=MESH)

        mine = [pltpu.make_async_copy(x_refs[t], rows(t, *me), local_sems.at[t]) for t in range(n)]
        for cp in mine:
            cp.start()
        first = []
        for t in range(n):
            first.append(copy(t, 0, me, sibling, src=x_refs[t]))
            first += [copy(t, 1 + j, me, (*chip, c), src=x_refs[t]) for j, chip in enumerate(chips)]
        for cp in first:
            cp.start()
        passed = []
        for t in range(n):
            for j, chip in enumerate(chips):
                copy(t, 1 + j, (*chip, c), me).wait_recv()
                passed.append(copy(t, 4 + j, (*chip, c), sibling))
                passed[-1].start()
        for t in range(n):
            copy(t, 0, sibling, me).wait_recv()
            for j, chip in enumerate(chips):
                copy(t, 4 + j, (*chip, 1 - c), me).wait_recv()
        for cp in first + passed:
            cp.wait_send()
        for cp in mine:
            cp.wait()

    hbm = pl.BlockSpec(memory_space=pltpu.HBM)
    out = pl.pallas_call(
        body, name=name,
        out_shape=[jax.ShapeDtypeStruct((N_DEV,) + s.shape, s.dtype) for s in shards] + [jax.ShapeDtypeStruct((8, LANES), F32)],
        in_specs=[hbm] * n, out_specs=[hbm] * n + [pl.BlockSpec(memory_space=pltpu.VMEM)],
        scratch_shapes=[pltpu.SemaphoreType.DMA((7 * n,)), pltpu.SemaphoreType.DMA((7 * n,)), pltpu.SemaphoreType.DMA((n,))],
        compiler_params=_cp(),
    )(*shards)
    return out[:n], out[n]


_HBM = pl.BlockSpec(memory_space=pltpu.HBM)
_SEM = pl.BlockSpec(memory_space=pltpu.SEMAPHORE)
_EFFECT = pltpu.SideEffectType.DATAFLOW_SIDE_EFFECTING


def _split_copies(src_refs, land_refs, send_sems, recv_sems, plan, n_copies):
    copies = []
    for t, (src_ref, land_ref) in enumerate(zip(src_refs, land_refs)):
        for k in range(n_copies):
            s, d, to = plan(src_ref, land_ref, k)
            copies.append(pltpu.make_async_remote_copy(
                src_ref=s, dst_ref=d, send_sem=send_sems.at[n_copies * t + k], recv_sem=recv_sems.at[n_copies * t + k],
                device_id=to, device_id_type=MESH))
    return copies


def split_start(srcs, lands, plan, n_copies, *, name):
    n = len(srcs)

    def body(*refs):
        src_refs, land_refs, send_sems, recv_sems, token = refs[:n], refs[n:2 * n], refs[2 * n], refs[2 * n + 1], refs[-1]
        for cp in _split_copies(src_refs, land_refs, send_sems, recv_sems, plan, n_copies):
            cp.start()
        token[...] = jnp.zeros_like(token)

    both = list(srcs) + list(lands)
    out = pl.pallas_call(
        body, name=name,
        out_shape=(pltpu.SemaphoreType.DMA((n_copies * n,)), pltpu.SemaphoreType.DMA((n_copies * n,)),
                   *[pltpu.HBM(a.shape, a.dtype) for a in both], jax.ShapeDtypeStruct((8, LANES), F32)),
        in_specs=(_HBM,) * (2 * n), out_specs=(_SEM, _SEM) + (_HBM,) * (2 * n) + (pl.BlockSpec(memory_space=pltpu.VMEM),),
        input_output_aliases={i: 2 + i for i in range(2 * n)},
        compiler_params=pltpu.CompilerParams(has_side_effects=_EFFECT),
    )(*[pltpu.with_memory_space_constraint(a, pltpu.HBM) for a in both])
    return out[0], out[1], list(out[2:2 + n]), list(out[2 + n:2 + 2 * n]), out[-1]


def split_wait(started, after, plan, n_copies, *, name):
    send_sems, recv_sems, srcs, lands, _ = started
    n = len(srcs)

    def body(*refs):
        src_refs, land_refs, send_sems, recv_sems = refs[:n], refs[n:2 * n], refs[2 * n], refs[2 * n + 1]
        for cp in _split_copies(src_refs, land_refs, send_sems, recv_sems, plan, n_copies):
            cp.wait_send()
            cp.wait_recv()

    both = list(srcs) + list(lands)
    out = pl.pallas_call(
        body, name=name,
        out_shape=tuple(pltpu.HBM(a.shape, a.dtype) for a in both),
        in_specs=(_HBM,) * (2 * n) + (_SEM, _SEM, pl.BlockSpec(memory_space=pl.ANY)), out_specs=(_HBM,) * (2 * n),
        input_output_aliases={i: i for i in range(2 * n)},
        compiler_params=pltpu.CompilerParams(has_side_effects=_EFFECT),
    )(*both, send_sems, recv_sems, after)
    return list(out[:n]), list(out[n:])


def _other_chips(x, y):
    return [(1 - x, y), (x, 1 - y), (1 - x, 1 - y)]


def _remote(src, dst, send_sem, recv_sem, to):
    return pltpu.make_async_remote_copy(src_ref=src, dst_ref=dst, send_sem=send_sem, recv_sem=recv_sem,
                                        device_id=to, device_id_type=MESH)


def gather_start(groups, *, name):
    counts = [len(shards) for shards, _ in groups]
    flat = [a for shards, _ in groups for a in shards] + [a for _, lands in groups for a in lands]
    n_all, n_groups = sum(counts), len(groups)

    def body(*refs):
        s_refs, l_refs = refs[:n_all], refs[n_all:2 * n_all]
        sems = refs[2 * n_all:2 * n_all + 3 * n_groups]
        x, y, c = _position()
        me = 4 * x + 2 * y + c
        at = 0
        for gi, n in enumerate(counts):
            send, recv_sibling, recv_ici = sems[3 * gi:3 * gi + 3]
            for t in range(n):
                src, dst = s_refs[at + t], l_refs[at + t].at[me]
                _remote(src, dst, send.at[4 * t], recv_sibling.at[t], (x, y, 1 - c)).start()
                for j, chip in enumerate(_other_chips(x, y)):
                    _remote(src, dst, send.at[4 * t + 1 + j], recv_ici.at[3 * t + j], (*chip, c)).start()
            at += n
        refs[-1][...] = jnp.zeros_like(refs[-1])

    sem_shapes = [pltpu.SemaphoreType.DMA((k * n,)) for n in counts for k in (4, 1, 3)]
    out = pl.pallas_call(
        body, name=name,
        out_shape=(*sem_shapes, *[pltpu.HBM(a.shape, a.dtype) for a in flat], jax.ShapeDtypeStruct((8, LANES), F32)),
        in_specs=(_HBM,) * (2 * n_all),
        out_specs=(_SEM,) * (3 * n_groups) + (_HBM,) * (2 * n_all) + (pl.BlockSpec(memory_space=pltpu.VMEM),),
        input_output_aliases={i: 3 * n_groups + i for i in range(2 * n_all)},
        compiler_params=pltpu.CompilerParams(has_side_effects=_EFFECT),
    )(*[pltpu.with_memory_space_constraint(a, pltpu.HBM) for a in flat])
    thru = out[3 * n_groups:-1]
    states, at = [], 0
    for gi, n in enumerate(counts):
        states.append(dict(shards=list(thru[at:at + n]), lands=list(thru[n_all + at:n_all + at + n]),
                           send=out[3 * gi], recv_sibling=out[3 * gi + 1], recv_ici=out[3 * gi + 2]))
        at += n
    return states, out[-1]


def gather_forward(states, after, *, name):
    counts = [len(s["lands"]) for s in states]
    flat = [a for s in states for a in s["lands"]]
    n_all, n_groups = sum(counts), len(states)

    def body(*refs):
        l_refs = refs[:n_all]
        recv_ici = refs[n_all:n_all + n_groups]
        fwd = refs[n_all + n_groups + 1:n_all + n_groups + 1 + 2 * n_groups]
        x, y, c = _position()
        at = 0
        for gi, n in enumerate(counts):
            fwd_send, fwd_recv = fwd[2 * gi], fwd[2 * gi + 1]
            for t in range(n):
                for j, (px, py) in enumerate(_other_chips(x, y)):
                    block = l_refs[at + t].at[4 * px + 2 * py + c]
                    _remote(block, block, fwd_send.at[3 * t + j], recv_ici[gi].at[3 * t + j], (px, py, c)).wait_recv()
                    _remote(block, block, fwd_send.at[3 * t + j], fwd_recv.at[3 * t + j], (x, y, 1 - c)).start()
            at += n
        refs[-1][...] = jnp.zeros_like(refs[-1])

    sem_shapes = [pltpu.SemaphoreType.DMA((3 * n,)) for n in counts for _ in range(2)]
    out = pl.pallas_call(
        body, name=name,
        out_shape=(*sem_shapes, *[pltpu.HBM(a.shape, a.dtype) for a in flat], jax.ShapeDtypeStruct((8, LANES), F32)),
        in_specs=(_HBM,) * n_all + (_SEM,) * n_groups + (pl.BlockSpec(memory_space=pl.ANY),),
        out_specs=(_SEM,) * (2 * n_groups) + (_HBM,) * n_all + (pl.BlockSpec(memory_space=pltpu.VMEM),),
        input_output_aliases={i: 2 * n_groups + i for i in range(n_all)},
        compiler_params=pltpu.CompilerParams(has_side_effects=_EFFECT),
    )(*flat, *[s["recv_ici"] for s in states], after)
    at = 0
    for gi, (s, n) in enumerate(zip(states, counts)):
        s.update(fwd_send=out[2 * gi], fwd_recv=out[2 * gi + 1], lands=list(out[2 * n_groups + at:2 * n_groups + at + n]))
        at += n
    return out[-1]


def gather_finish(state, after, *, name):
    n = len(state["lands"])

    def body(*refs):
        s_refs, l_refs = refs[:n], refs[n:2 * n]
        send, recv_sibling, fwd_send, fwd_recv = refs[2 * n:2 * n + 4]
        x, y, c = _position()
        me = 4 * x + 2 * y + c
        for t in range(n):
            own = l_refs[t].at[me]
            _remote(s_refs[t], own, send.at[4 * t], recv_sibling.at[t], (x, y, 1 - c)).wait_send()
            _remote(s_refs[t], l_refs[t].at[4 * x + 2 * y + 1 - c], send.at[4 * t], recv_sibling.at[t], (x, y, 1 - c)).wait_recv()
            for j, (px, py) in enumerate(_other_chips(x, y)):
                _remote(s_refs[t], own, send.at[4 * t + 1 + j], recv_sibling.at[t], (px, py, c)).wait_send()
                mine, theirs = l_refs[t].at[4 * px + 2 * py + c], l_refs[t].at[4 * px + 2 * py + 1 - c]
                _remote(mine, mine, fwd_send.at[3 * t + j], fwd_recv.at[3 * t + j], (x, y, 1 - c)).wait_send()
                _remote(theirs, theirs, fwd_send.at[3 * t + j], fwd_recv.at[3 * t + j], (x, y, 1 - c)).wait_recv()

    both = state["shards"] + state["lands"]
    out = pl.pallas_call(
        body, name=name,
        out_shape=tuple(pltpu.HBM(a.shape, a.dtype) for a in both),
        in_specs=(_HBM,) * (2 * n) + (_SEM,) * 4 + (pl.BlockSpec(memory_space=pl.ANY),), out_specs=(_HBM,) * (2 * n),
        input_output_aliases={i: i for i in range(2 * n)},
        compiler_params=pltpu.CompilerParams(has_side_effects=_EFFECT),
    )(*both, state["send"], state["recv_sibling"], state["fwd_send"], state["fwd_recv"], after)
    return list(out[n:])


def _all_peers_plan(src_ref, land_ref, k):
    x, y, c = _position()
    bits = k + 1
    peer = ((1 - x) if bits & 4 else x, (1 - y) if bits & 2 else y, (1 - c) if bits & 1 else c)
    return src_ref, land_ref.at[4 * x + 2 * y + c], peer


def _sibling_plan(src_ref, land_ref, k):
    x, y, c = _position()
    return src_ref.at[2 * k + (1 - c)], land_ref.at[k], (x, y, 1 - c)


def _chips_plan(src_ref, land_ref, j):
    x, y, c = _position()
    px, py = _other_chips(x, y)[j]
    return src_ref.at[j], land_ref.at[j], (px, py, c)


SUM_STEPS = 2


def sum_for_chips(parts, from_sibling, ck_idx, *, name):
    n = len(parts)

    def body(ck_ref, *refs):
        del ck_ref
        for t in range(n):
            refs[2 * n + t][...] = (refs[t][...] + refs[n + t][...]).astype(BF16)

    def blk(a):
        return (None, a.shape[1] // SUM_STEPS, a.shape[2])

    return pl.pallas_call(
        body, name=name,
        grid_spec=pltpu.PrefetchScalarGridSpec(
            num_scalar_prefetch=1, grid=(3, SUM_STEPS),
            in_specs=[pl.BlockSpec(blk(a), lambda j, i, ck: (2 * ck[1 + j] + ck[0], i, 0)) for a in parts]
            + [pl.BlockSpec(blk(a), lambda j, i, ck: (ck[1 + j], i, 0)) for a in from_sibling],
            out_specs=[pl.BlockSpec(blk(a), lambda j, i, ck: (j, i, 0)) for a in from_sibling]),
        out_shape=[jax.ShapeDtypeStruct((3,) + a.shape[1:], BF16) for a in from_sibling], compiler_params=_cp(),
    )(ck_idx, *parts, *from_sibling)


def sum_final(parts, from_sibling, from_chips, kc_idx, *, name):
    n = len(parts)

    def body(kc_ref, *refs):
        del kc_ref
        for t in range(n):
            p, s, a, b, d = (refs[j * n + t] for j in range(5))
            refs[5 * n + t][...] = (((p[...] + s[...]) + a[...].astype(F32)) + b[...].astype(F32)) + d[...].astype(F32)

    def blk(a):
        return (None, a.shape[1] // SUM_STEPS, a.shape[2])

    def chip_specs(j):
        return [pl.BlockSpec(blk(a), lambda i, kc: (j, i, 0)) for a in from_chips]

    return pl.pallas_call(
        body, name=name,
        grid_spec=pltpu.PrefetchScalarGridSpec(
            num_scalar_prefetch=1, grid=(SUM_STEPS,),
            in_specs=[pl.BlockSpec(blk(a), lambda i, kc: (2 * kc[0] + kc[1], i, 0)) for a in parts]
            + [pl.BlockSpec(blk(a), lambda i, kc: (kc[0], i, 0)) for a in from_sibling]
            + chip_specs(0) + chip_specs(1) + chip_specs(2),
            out_specs=[pl.BlockSpec(blk(a)[1:], lambda i, kc: (i, 0)) for a in parts]),
        out_shape=[jax.ShapeDtypeStruct(a.shape[1:], F32) for a in parts], compiler_params=_cp(),
    )(kc_idx, *parts, *from_sibling, *from_chips, *from_chips, *from_chips)


BIG = (
    ("w_in", IN_COLS, True), ("w_out", D_MODEL, False), ("wq_x", D_MODEL, False), ("wkv_x", 2 * D_MODEL, True),
    ("wo_x", D_MODEL, False), ("w_gate_up", 2 * D_FF, True), ("w_down", D_FF, False),
)

SMALL = ("norm_mix_g", "q_norm_g", "k_norm_g", "sinks", "conv_b", "conv_ln_g", "conv_ln_b",
         "norm_x_g", "norm_mem_g", "xq_norm_g", "xk_norm_g", "norm_ffn_g")


def _adamw_math(w, g, m, v):
    m2 = ADAM_B1 * m + (1.0 - ADAM_B1) * g
    v2 = ADAM_B2 * v + (1.0 - ADAM_B2) * jnp.square(g)
    m_hat = m2 / (1.0 - ADAM_B1 ** ADAM_STEP)
    v_hat = v2 / (1.0 - ADAM_B2 ** ADAM_STEP)
    return -ADAM_LR * (m_hat / (jnp.sqrt(v_hat) + ADAM_EPS) + ADAM_WD * w), m2, v2


def _small_rows(per_layer_shape):
    return 1 if len(per_layer_shape) == 1 else per_layer_shape[0]


def pack_small(parts, shapes):
    blocks = []
    for per_layer, sh in zip(parts, shapes):
        for g in per_layer:
            g = g.reshape(_small_rows(sh), sh[-1])
            blocks.append(jnp.pad(g, ((0, 0), (0, D_MODEL - sh[-1]))))
    rows = sum(b.shape[0] for b in blocks)
    blocks.append(jnp.zeros((-rows % 8, D_MODEL), F32))
    return jnp.concatenate(blocks, axis=0)


def update_small(gathered, shapes, weights, moments_m, moments_v, n_update):
    n_all = len(shapes)

    def body(*refs):
        g_ref = refs[0]
        w_refs, m_refs, v_refs = (refs[1 + j * n_update:1 + (j + 1) * n_update] for j in range(3))
        out = refs[1 + 3 * n_update:]
        grad_refs = out[:n_all]
        d_refs, nm_refs, nv_refs = (out[n_all + j * n_update:n_all + (j + 1) * n_update] for j in range(3))
        at = 0
        for p, sh in enumerate(shapes):
            rows, lanes = _small_rows(sh), sh[-1]
            for l in range(DEPTH):
                g = g_ref[0, at:at + rows, 0:lanes]
                for k in range(1, N_DEV):
                    g = g + g_ref[k, at:at + rows, 0:lanes]
                at += rows
                here = (slice(l, l + 1),) + (slice(None),) * (len(sh) - 1) if len(sh) == 1 else (l,)
                grad_refs[p][here] = g
                if p < n_update:
                    d, m2, v2 = _adamw_math(w_refs[p][here], g, m_refs[p][here], v_refs[p][here])
                    d_refs[p][here] = d
                    nm_refs[p][here] = m2
                    nv_refs[p][here] = v2

    full = [jax.ShapeDtypeStruct((DEPTH,) + tuple(sh), F32) for sh in shapes]
    out = pl.pallas_call(
        body, name="update_small", out_shape=full + full[:n_update] * 3, compiler_params=_cp(),
    )(gathered, *weights, *moments_m, *moments_v)
    return (out[:n_all], out[n_all:n_all + n_update], out[n_all + n_update:n_all + 2 * n_update],
            out[n_all + 2 * n_update:])


WEIGHT_GROUPS = {"in": ("w_in",), "mid": ("w_out", "wq_x", "wkv_x", "wo_x"), "ffn": ("w_gate_up", "w_down")}


def _layer_fwd(x0, mem, weights_of, s, reached, target=None):
    w = dict(weights_of("in", x0))
    h0, u = norm_proj(x0, s["norm_mix_g"], w["w_in"])
    mixed = swa_fwd(u, s["q_norm_g"], s["k_norm_g"], s["sinks"])
    reached("attn", mixed)
    mixed, conv_y = conv_fwd(u, mixed, s["conv_w"], s["conv_b"], s["conv_ln_g"], s["conv_ln_b"])
    w.update(weights_of("mid", conv_y))
    memn = rms_fwd(mem, s["norm_mem_g"])
    kv = mm(memn, w["wkv_x"], trans_b=True, out_dtype=F32, name="mm_kv")
    x1, h1, qx, o, x2, h2 = mid_fwd(mixed, x0, w["w_out"], s["norm_x_g"], w["wq_x"], kv, s["xq_norm_g"], s["xk_norm_g"],
                                    w["wo_x"], s["norm_ffn_g"])
    reached("mid", x2)
    w.update(weights_of("ffn", x2))
    gu, a, *out = ffn_fwd(h2, x2, w["w_gate_up"], w["w_down"], target)
    saved = dict(x0=x0, h0=h0, u=u, conv_y=conv_y, mixed=mixed, x1=x1, h1=h1, qx=qx, memn=memn, kv=kv, o=o, x2=x2, h2=h2,
                 gu=gu, a=a)
    return out, saved, w


def _ordered_after(a, token):
    return a if token is None else a + token[0, 0]


def _layer_bwd(dx3, mem, w, s, sv, token, stage_done):
    gs = {}
    dgu, dx2, dg = ffn_bwd(dx3, sv["gu"], sv["x2"], _ordered_after(s["norm_ffn_g"], token), w["w_down"], w["w_gate_up"])
    gs["norm_ffn_g"] = dg
    gb = {"w_down": mm_tn(sv["a"], dx3, name="mm_dw_down")}
    gb["w_gate_up"] = mm_tn(dgu, sv["h2"], tk=dgu.shape[0], name="mm_dw_gate_up")
    token = stage_done("ffn", gb, gb["w_gate_up"])

    gb = {}
    dq, dx1, dmixed, dkv, dqg, dkg, dg = mid_bwd(dx2, sv["qx"], sv["kv"], s["xq_norm_g"], s["xk_norm_g"], sv["x1"],
                                                 _ordered_after(s["norm_x_g"], token), w["wo_x"], w["wq_x"], w["w_out"])
    gs["xq_norm_g"], gs["xk_norm_g"], gs["norm_x_g"] = dqg, dkg, dg
    gb["wo_x"] = mm_tn(sv["o"], dx2, name="mm_dwo")
    gb["wq_x"] = mm_tn(sv["h1"], dq, name="mm_dwq")
    dmemn = mm(dkv, w["wkv_x"], trans_b=False, out_dtype=F32, name="mm_dmemn")
    gb["wkv_x"] = mm_tn(dkv, sv["memn"], name="mm_dwkv")
    gs["norm_mem_g"] = rms_gain_bwd(dmemn, mem)
    gb["w_out"] = mm_tn(sv["mixed"], dx1, name="mm_dw_out")
    token = stage_done("mid", gb, gb["w_out"])

    du, dqg, dkg, dsinks = swa_bwd(sv["u"], dmixed, _ordered_after(s["q_norm_g"], token), s["k_norm_g"], s["sinks"])
    gs["q_norm_g"], gs["k_norm_g"], gs["sinks"] = dqg[0, :HEAD_DIM], dkg[0, :HEAD_DIM], dsinks[0, :N_Q_HEADS]
    token = stage_done("attn", {}, dqg)
    du, dconv_w, dvec = conv_bwd(sv["u"], sv["conv_y"], dmixed, du, s["conv_w"], _ordered_after(s["conv_ln_g"], token),
                                 s["conv_ln_b"])
    gs["conv_w"] = dconv_w[:CONV_K]
    gs["conv_b"], gs["conv_ln_g"], gs["conv_ln_b"] = dvec[0], dvec[1], dvec[2]
    dw_in = mm_tn(du, sv["h0"], name="mm_dw_in")
    token = stage_done("in", {"w_in": dw_in}, dw_in)
    dx0, dg = in_bwd(du, w["w_in"], sv["x0"], _ordered_after(s["norm_mix_g"], token), dx1)
    gs["norm_mix_g"] = dg
    token = stage_done("mix", {}, dx0)
    return dx0, gs, token


def _local_step(x, mem, target, weights_of, reached, smalls, stage_done):
    saved, weights = [], []
    out = [x]
    for l in range(DEPTH):
        out, sv, w = _layer_fwd(out[0], mem, functools.partial(weights_of, l), smalls[l], functools.partial(reached, l),
                                target if l == DEPTH - 1 else None)
        saved.append(sv)
        weights.append(w)
    dx, loss_part = out
    gss, token = [None] * DEPTH, None
    for l in reversed(range(DEPTH)):
        dx, gss[l], token = _layer_bwd(dx, mem, weights[l], smalls[l], saved[l], token,
                                       functools.partial(stage_done, l))
    return loss_part[0, 0], dx, gss


def kernel(x, mem, norm_mix_g, w_in, q_norm_g, k_norm_g, sinks, conv_w, conv_b, conv_ln_g, conv_ln_b, w_out, norm_x_g, norm_mem_g, wq_x, wkv_x, xq_norm_g, xk_norm_g, wo_x, norm_ffn_g, w_gate_up, w_down, loss_target, m_norm_mix_g, m_w_in, m_q_norm_g, m_k_norm_g, m_sinks, m_conv_w, m_conv_b, m_conv_ln_g, m_conv_ln_b, m_w_out, m_norm_x_g, m_norm_mem_g, m_wq_x, m_wkv_x, m_xq_norm_g, m_xk_norm_g, m_wo_x, m_norm_ffn_g, m_w_gate_up, m_w_down, v_norm_mix_g, v_w_in, v_q_norm_g, v_k_norm_g, v_sinks, v_conv_w, v_conv_b, v_conv_ln_g, v_conv_ln_b, v_w_out, v_norm_x_g, v_norm_mem_g, v_wq_x, v_wkv_x, v_xq_norm_g, v_xk_norm_g, v_wo_x, v_norm_ffn_g, v_w_gate_up, v_w_down):
    P = dict(norm_mix_g=norm_mix_g, w_in=w_in, q_norm_g=q_norm_g, k_norm_g=k_norm_g, sinks=sinks, conv_w=conv_w, conv_b=conv_b,
             conv_ln_g=conv_ln_g, conv_ln_b=conv_ln_b, w_out=w_out, norm_x_g=norm_x_g, norm_mem_g=norm_mem_g, wq_x=wq_x,
             wkv_x=wkv_x, xq_norm_g=xq_norm_g, xk_norm_g=xk_norm_g, wo_x=wo_x, norm_ffn_g=norm_ffn_g, w_gate_up=w_gate_up,
             w_down=w_down)
    M = dict(norm_mix_g=m_norm_mix_g, w_in=m_w_in, q_norm_g=m_q_norm_g, k_norm_g=m_k_norm_g, sinks=m_sinks, conv_w=m_conv_w,
             conv_b=m_conv_b, conv_ln_g=m_conv_ln_g, conv_ln_b=m_conv_ln_b, w_out=m_w_out, norm_x_g=m_norm_x_g,
             norm_mem_g=m_norm_mem_g, wq_x=m_wq_x, wkv_x=m_wkv_x, xq_norm_g=m_xq_norm_g, xk_norm_g=m_xk_norm_g, wo_x=m_wo_x,
             norm_ffn_g=m_norm_ffn_g, w_gate_up=m_w_gate_up, w_down=m_w_down)
    V = dict(norm_mix_g=v_norm_mix_g, w_in=v_w_in, q_norm_g=v_q_norm_g, k_norm_g=v_k_norm_g, sinks=v_sinks, conv_w=v_conv_w,
             conv_b=v_conv_b, conv_ln_g=v_conv_ln_g, conv_ln_b=v_conv_ln_b, w_out=v_w_out, norm_x_g=v_norm_x_g,
             norm_mem_g=v_norm_mem_g, wq_x=v_wq_x, wkv_x=v_wkv_x, xq_norm_g=v_xq_norm_g, xk_norm_g=v_xk_norm_g, wo_x=v_wo_x,
             norm_ffn_g=v_norm_ffn_g, w_gate_up=v_w_gate_up, w_down=v_w_down)
    order = ["norm_mix_g", "w_in", "q_norm_g", "k_norm_g", "sinks", "conv_w", "conv_b", "conv_ln_g", "conv_ln_b", "w_out",
             "norm_x_g", "norm_mem_g", "wq_x", "wkv_x", "xq_norm_g", "xk_norm_g", "wo_x", "norm_ffn_g", "w_gate_up", "w_down"]
    xi, yi, ci = _position()
    dev = 4 * xi + 2 * yi + ci
    x2d, mem2d, tgt2d = x[0], mem[0], loss_target[0]

    def travelling(name, l, transposed):
        a = P[name][l]
        return (a.T if transposed else a).astype(BF16)

    rows_of = {n: rows for n, rows, _ in BIG}
    transposed_of = {n: tr for n, _, tr in BIG}

    def whole(names, gathered):
        return {n: g.reshape(rows_of[n], D_MODEL) for n, g in zip(names, gathered)}

    cw = jnp.pad(conv_w.reshape(DEPTH * CONV_K, CONV_CH // N_DEV), ((0, 2), (0, LANES - CONV_CH // N_DEV)))
    (w_in0, cw_all), token0 = all_gather_many([travelling("w_in", 0, True), cw], name="ag_w_in0_conv_w")
    travel_order = [(0, "mid"), (0, "ffn"), (1, "in"), (1, "mid"), (1, "ffn")]
    travel_groups = []
    for l, group in travel_order:
        shards = [_ordered_after(travelling(n, l, transposed_of[n]), token0.astype(BF16)) for n in WEIGHT_GROUPS[group]]
        lands = [lax.dynamic_update_slice(lax.empty((N_DEV,) + s.shape, BF16), s[None], (dev, 0, 0)) for s in shards]
        travel_groups.append((shards, lands))
    travel_states, travel_token = gather_start(travel_groups, name="ag_weights_start")
    travelling_state = dict(zip(travel_order, travel_states))
    forward_at = {(0, "attn"): [(0, "mid")], (0, "mid"): [(0, "ffn"), (1, "in")], (1, "attn"): [(1, "mid"), (1, "ffn")]}

    def reached(l, stage, marker):
        keys = forward_at.get((l, stage))
        if keys:
            gather_forward([travelling_state[k] for k in keys], marker,
                           name="ag_weights_forward_" + "_".join(f"{g}{ll}" for ll, g in keys))

    def weights_of(l, group, marker):
        if (l, group) == (0, "in"):
            return whole(WEIGHT_GROUPS[group], [w_in0])
        gathered = gather_finish(travelling_state[(l, group)], marker, name=f"ag_weights_finish_{group}{l}")
        return whole(WEIGHT_GROUPS[group], gathered)

    cw_full = cw_all[:, :DEPTH * CONV_K, :CONV_CH // N_DEV].reshape(N_DEV, DEPTH, CONV_K, CONV_CH // N_DEV)
    cw_full = jnp.transpose(cw_full, (1, 2, 0, 3)).reshape(DEPTH, CONV_K, CONV_CH)
    smalls = []
    for l in range(DEPTH):
        sl = {n: P[n][l] if n == "sinks" else P[n][l:l + 1] for n in SMALL}
        sl["conv_w"] = jnp.pad(cw_full[l], ((0, CONV_HALO - CONV_K), (0, 0)))
        smalls.append(sl)
    smalls[0]["norm_mix_g"] = _ordered_after(smalls[0]["norm_mix_g"], travel_token)

    ck_idx = jnp.stack([ci] + [2 * px + py for px, py in _other_chips(xi, yi)]).astype(jnp.int32)
    kc_idx = jnp.stack([2 * xi + yi, ci]).astype(jnp.int32)
    got, flight, reduced = {}, {}, {}

    def as_parts(gb):
        keys = sorted(gb)
        return keys, [gb[k].reshape(N_DEV, rows_of[k[1]] // N_DEV, D_MODEL) for k in keys]

    def lands_like(parts, blocks, dtype):
        return [lax.empty((blocks,) + p.shape[1:], dtype) for p in parts]

    def to_sibling(group, gb):
        keys, parts = as_parts(gb)
        flight[group] = (keys, split_start(parts, lands_like(parts, 4, F32), _sibling_plan, 4,
                                           name=f"rs_sibling_{group}_start"))
        return flight[group][1][4]

    def to_chips(group, marker):
        keys, started = flight[group]
        parts, from_sibling = split_wait(started, marker, _sibling_plan, 4, name=f"rs_sibling_{group}_wait")
        chip_sums = sum_for_chips(parts, from_sibling, ck_idx, name=f"rs_sum_for_chips_{group}")
        started = split_start(chip_sums, lands_like(parts, 3, BF16), _chips_plan, 3, name=f"rs_chips_{group}_start")
        flight[group] = (keys, parts, from_sibling, started)
        return started[4]

    def finish(group, marker):
        keys, parts, from_sibling, started = flight[group]
        _, from_chips = split_wait(started, marker, _chips_plan, 3, name=f"rs_chips_{group}_wait")
        reduced.update(zip(keys, sum_final(parts, from_sibling, from_chips, kc_idx, name=f"rs_sum_final_{group}")))

    def stage_done(l, stage, gb, marker):
        gb = {(l, n): g for n, g in gb.items()}
        if l == 1:
            got.update(gb)
            return to_sibling("l1", got) if stage == "mix" else None
        if stage == "ffn":
            return to_chips("l1", marker) + to_sibling("ffn", gb)
        if stage == "mid":
            return to_chips("ffn", marker) + to_sibling("mid", gb)
        if stage == "attn":
            return to_chips("mid", marker)
        if stage == "in":
            return to_sibling("in", gb)
        to_chips("in", marker)
        for group in ("l1", "ffn", "mid"):
            finish(group, marker)
        return None

    loss_part, grad_x, gss = _local_step(x2d, mem2d, tgt2d, weights_of, reached, smalls, stage_done)
    loss = lax.psum(loss_part, ("x", "y", "c"))

    small_names = SMALL + ("conv_w",)
    small_shapes = [(CONV_K, CONV_CH) if n == "conv_w" else P[n].shape[1:] for n in small_names]
    small_parts = pack_small([[gss[l][n] for l in range(DEPTH)] for n in small_names], small_shapes)
    small_land = lax.dynamic_update_slice(lax.empty((N_DEV,) + small_parts.shape, F32), small_parts[None], (dev, 0, 0))
    small_flight = split_start([small_parts], [small_land], _all_peers_plan, N_DEV - 1, name="ag_small_grads_start")

    grads, delta, new_m, new_v = {}, {}, {}, {}

    def update(n, transposed):
        shape = P[n].shape
        two_d = lambda a: a.reshape(shape[0] * shape[1], shape[2])
        grads[n] = jnp.stack([reduced[(l, n)].T if transposed else reduced[(l, n)] for l in range(DEPTH)])
        d_, m_, v_ = adamw(two_d(P[n]), two_d(grads[n]), two_d(M[n]), two_d(V[n]), name="adamw_" + n)
        delta[n], new_m[n], new_v[n] = d_.reshape(shape), m_.reshape(shape), v_.reshape(shape)

    for n, _, transposed in BIG:
        if n != "w_in":
            update(n, transposed)
    finish("in", delta["w_down"])
    update("w_in", True)
    small_all = split_wait(small_flight, delta["w_in"], _all_peers_plan, N_DEV - 1, name="ag_small_grads_wait")[1][0]
    g_, d_, m_, v_ = update_small(small_all, small_shapes, [P[n] for n in SMALL], [M[n] for n in SMALL],
                                  [V[n] for n in SMALL], len(SMALL))
    for i, n in enumerate(SMALL):
        grads[n], delta[n], new_m[n], new_v[n] = g_[i], d_[i], m_[i], v_[i]
    cols = CONV_CH // N_DEV
    grads["conv_w"] = lax.dynamic_slice_in_dim(g_[-1], dev * cols, cols, axis=2)
    flat = lambda a: a.reshape(DEPTH * CONV_K, cols)
    d_, m_, v_ = adamw(flat(conv_w), flat(grads["conv_w"]), flat(m_conv_w), flat(v_conv_w), name="adamw_conv_w")
    delta["conv_w"], new_m["conv_w"], new_v["conv_w"] = (a.reshape(conv_w.shape) for a in (d_, m_, v_))

    return (loss, grad_x[None], *[grads[n] for n in order], *[delta[n] for n in order],
            *[new_m[n] for n in order], *[new_v[n] for n in order])
```

```python
import functools

import jax
import jax.numpy as jnp
import numpy as np
from jax import lax
from jax.experimental import pallas as pl
from jax.experimental.pallas import tpu as pltpu

F32 = jnp.float32
BF16 = jnp.bfloat16

D_MODEL = 1024
HEAD_DIM = 64
N_Q_HEADS = 8
N_KV_HEADS = 2
GROUP = N_Q_HEADS // N_KV_HEADS
ATTN_WIDTH = N_Q_HEADS * HEAD_DIM
KV_WIDTH = N_KV_HEADS * HEAD_DIM
QKV_WIDTH = ATTN_WIDTH + 2 * KV_WIDTH
CONV_CH = 512
IN_COLS = QKV_WIDTH + 2 * CONV_CH
CONV_K = 31
CONV_HALO = 32
BLOCK = 128
N_X_HEADS = 4
X_HEAD_DIM = 256
D_FF = 2816
EPS = 1e-6
NEG = -1e30
DEPTH = 2
N_DEV = 8

ADAM_LR = 0.001
ADAM_B1 = 0.9
ADAM_B2 = 0.999
ADAM_EPS = 1e-08
ADAM_WD = 0.01
ADAM_STEP = 10

V7X_VMEM_LIMIT = 56 * 1024 * 1024
LANES = 128

MESH = pl.DeviceIdType.MESH


def _cp(**kw):
    return pltpu.CompilerParams(vmem_limit_bytes=V7X_VMEM_LIMIT, **kw)


def _dot(a, b, dims):
    return lax.dot_general(a.astype(BF16), b.astype(BF16), (dims, ((), ())), preferred_element_type=F32)


def _dot_nn(a, b):
    return _dot(a, b, ((1,), (0,)))


def _dot_nt(a, b):
    return _dot(a, b, ((1,), (1,)))


def _dot_tn(a, b):
    return _dot(a, b, ((0,), (0,)))


def _sigmoid(x):
    return jax.nn.sigmoid(x)


def _rms(x):
    r = lax.rsqrt(jnp.mean(x * x, axis=-1, keepdims=True) + EPS)
    return x * r, r


def _rms_bwd(dy, xhat, r, g):
    dxh = dy * g
    return r * (dxh - xhat * jnp.mean(dxh * xhat, axis=-1, keepdims=True))


def rms_fwd(x, g, *, tm=512):
    m, d = x.shape
    tm = min(tm, m)

    def body(x_ref, g_ref, o_ref):
        xh, _ = _rms(x_ref[...])
        o_ref[...] = (xh * g_ref[...]).astype(o_ref.dtype)

    return pl.pallas_call(
        body, name="rms_fwd", grid=(m // tm,),
        in_specs=[pl.BlockSpec((tm, d), lambda i: (i, 0)), pl.BlockSpec((1, d), lambda i: (0, 0))],
        out_specs=pl.BlockSpec((tm, d), lambda i: (i, 0)),
        out_shape=jax.ShapeDtypeStruct((m, d), BF16), compiler_params=_cp(),
    )(x, g.reshape(1, d))


def rms_gain_bwd(dh, x, *, tm=512):
    m, d = x.shape
    tm = min(tm, m)

    def body(dh_ref, x_ref, dg_ref):
        @pl.when(pl.program_id(0) == 0)
        def _():
            dg_ref[...] = jnp.zeros_like(dg_ref)

        dg_ref[...] += jnp.sum(dh_ref[...] * _rms(x_ref[...])[0], axis=0, keepdims=True)

    row = pl.BlockSpec((tm, d), lambda i: (i, 0))
    return pl.pallas_call(
        body, name="rms_gain_bwd", grid=(m // tm,), in_specs=[row, row],
        out_specs=pl.BlockSpec((1, d), lambda i: (0, 0)),
        out_shape=jax.ShapeDtypeStruct((1, d), F32), compiler_params=_cp(),
    )(dh, x)


def _tile(n, cap):
    if n <= cap:
        return n
    best = None
    for t in range(LANES, cap + 1, LANES):
        if n % t == 0:
            best = t
    assert best is not None, (n, cap)
    return best


def mm(a, b, *, trans_b, out_dtype, tm=1024, tn_cap=1536, name):
    m, k = a.shape
    n = b.shape[0] if trans_b else b.shape[1]
    assert (b.shape[1] if trans_b else b.shape[0]) == k
    tm = min(tm, m)
    tn = _tile(n, tn_cap)

    def body(a_ref, b_ref, o_ref):
        acc = _dot_nt(a_ref[...], b_ref[...]) if trans_b else _dot_nn(a_ref[...], b_ref[...])
        o_ref[...] = acc.astype(o_ref.dtype)

    b_spec = pl.BlockSpec((tn, k), lambda i, j: (j, 0)) if trans_b else pl.BlockSpec((k, tn), lambda i, j: (0, j))
    return pl.pallas_call(
        body, name=name, grid=(m // tm, n // tn),
        in_specs=[pl.BlockSpec((tm, k), lambda i, j: (i, 0)), b_spec],
        out_specs=pl.BlockSpec((tm, tn), lambda i, j: (i, j)),
        out_shape=jax.ShapeDtypeStruct((m, n), out_dtype), compiler_params=_cp(),
    )(a, b)


def mm_tn(a, b, *, name, ta_cap=1536, tb_cap=1024, tk=2048):
    m, ka = a.shape
    nb = b.shape[1]
    assert b.shape[0] == m
    tk = min(tk, m)
    ta = _tile(ka, ta_cap)
    tb = _tile(nb, tb_cap)

    def body(a_ref, b_ref, o_ref):
        @pl.when(pl.program_id(2) == 0)
        def _():
            o_ref[...] = jnp.zeros_like(o_ref)

        o_ref[...] += _dot_tn(a_ref[...], b_ref[...])

    return pl.pallas_call(
        body, name=name, grid=(ka // ta, nb // tb, m // tk),
        in_specs=[pl.BlockSpec((tk, ta), lambda i, j, kk: (kk, i)), pl.BlockSpec((tk, tb), lambda i, j, kk: (kk, j))],
        out_specs=pl.BlockSpec((ta, tb), lambda i, j, kk: (i, j)),
        out_shape=jax.ShapeDtypeStruct((ka, nb), F32), compiler_params=_cp(),
    )(a, b)


def _whole(shape):
    return pl.BlockSpec(shape, lambda i: (0,) * len(shape), pipeline_mode=pl.Buffered(1))


def _rows(tm, n):
    return pl.BlockSpec((tm, n), lambda i: (i, 0))


def _vec(n):
    return pl.BlockSpec((1, n), lambda i: (0, 0))


def _chunks(n, cap=1408):
    size = _tile(n, cap)
    return [(s, size) for s in range(0, n, size)]


def _zero_at_first_step(*refs):
    @pl.when(pl.program_id(0) == 0)
    def _():
        for r in refs:
            r[...] = jnp.zeros_like(r)


def norm_proj(x, g, wt, *, tm=512):
    m, d = x.shape
    n = wt.shape[0]

    def body(x_ref, g_ref, wt_ref, h_ref, u_ref):
        h = (_rms(x_ref[...])[0] * g_ref[...]).astype(BF16)
        h_ref[...] = h
        for s, sz in _chunks(n):
            u_ref[:, s:s + sz] = _dot_nt(h, wt_ref[s:s + sz, :])

    return pl.pallas_call(
        body, name="norm_proj", grid=(m // tm,),
        in_specs=[_rows(tm, d), _vec(d), _whole((n, d))],
        out_specs=[_rows(tm, d), _rows(tm, n)],
        out_shape=[jax.ShapeDtypeStruct((m, d), BF16), jax.ShapeDtypeStruct((m, n), F32)],
        compiler_params=_cp(),
    )(x, g.reshape(1, d), wt)


def _xattn_heads(q_ref, kv_ref, qg_v, kg_v, d):
    normed = []
    for h in range(N_X_HEADS):
        cols = slice(h * X_HEAD_DIM, (h + 1) * X_HEAD_DIM)
        qh, rq = _rms(q_ref[:, cols])
        normed.append((qh, rq, (qh * qg_v).astype(BF16), (_rms(kv_ref[:, cols])[0] * kg_v).astype(BF16),
                       kv_ref[:, d + h * X_HEAD_DIM:d + (h + 1) * X_HEAD_DIM].astype(BF16)))
    scores = [_dot_nt(qn, kn) * (X_HEAD_DIM ** -0.5) for _, _, qn, kn, _ in normed]
    out = []
    for (qh, rq, qn, kn, v), s in zip(normed, scores):
        e = jnp.exp(s - jnp.max(s, axis=-1, keepdims=True))
        out.append((qh, rq, qn, kn, v, e / jnp.sum(e, axis=-1, keepdims=True)))
    return out


def mid_fwd(mixed, x0, w_out, g_x, wq, kv, xqg, xkg, wo, g_f, *, tm=512):
    m, d = x0.shape
    n_mem = kv.shape[0]

    def body(mixed_ref, x0_ref, w_out_ref, g_x_ref, wq_ref, kv_ref, xqg_ref, xkg_ref, wo_ref, g_f_ref,
             x1_ref, h1_ref, qx_ref, o_ref, x2_ref, h2_ref):
        x1 = x0_ref[...] + _dot_nn(mixed_ref[...], w_out_ref[...])
        x1_ref[...] = x1
        h1 = (_rms(x1)[0] * g_x_ref[...]).astype(BF16)
        h1_ref[...] = h1
        qx_ref[...] = _dot_nn(h1, wq_ref[...])
        for h, (_, _, _, _, v, p) in enumerate(_xattn_heads(qx_ref, kv_ref, xqg_ref[...], xkg_ref[...], d)):
            o_ref[:, h * X_HEAD_DIM:(h + 1) * X_HEAD_DIM] = _dot_nn(p, v).astype(o_ref.dtype)
        x2 = x1 + _dot_nn(o_ref[...], wo_ref[...])
        x2_ref[...] = x2
        h2_ref[...] = (_rms(x2)[0] * g_f_ref[...]).astype(BF16)

    sq = _whole((d, d))
    f32_rows, bf_rows = jax.ShapeDtypeStruct((m, d), F32), jax.ShapeDtypeStruct((m, d), BF16)
    return pl.pallas_call(
        body, name="mid_fwd", grid=(m // tm,),
        in_specs=[_rows(tm, d), _rows(tm, d), sq, _vec(d), sq, _whole((n_mem, 2 * d)), _vec(X_HEAD_DIM), _vec(X_HEAD_DIM),
                  sq, _vec(d)],
        out_specs=[_rows(tm, d)] * 6,
        out_shape=[f32_rows, bf_rows, f32_rows, bf_rows, f32_rows, bf_rows],
        compiler_params=_cp(),
    )(mixed, x0, w_out, g_x.reshape(1, d), wq, kv, xqg.reshape(1, X_HEAD_DIM), xkg.reshape(1, X_HEAD_DIM), wo,
      g_f.reshape(1, d))


def ffn_fwd(h2, x2, wt_gu, w_down, target=None, *, tm=256):
    m, d = x2.shape
    f = w_down.shape[0]
    with_loss = target is not None

    def body(*refs):
        if with_loss:
            h2_ref, x2_ref, wt_gu_ref, w_down_ref, t_ref, gu_ref, a_ref, dy_ref, l_ref = refs
        else:
            h2_ref, x2_ref, wt_gu_ref, w_down_ref, gu_ref, a_ref, x3_ref = refs
        h = h2_ref[...]
        for s, sz in _chunks(2 * f):
            gu_ref[:, s:s + sz] = _dot_nt(h, wt_gu_ref[s:s + sz, :])
        for s, sz in _chunks(f):
            g = gu_ref[:, s:s + sz]
            a_ref[:, s:s + sz] = (g * _sigmoid(g) * gu_ref[:, f + s:f + s + sz]).astype(a_ref.dtype)
        x3 = x2_ref[...] + _dot_nn(a_ref[...], w_down_ref[...])
        if not with_loss:
            x3_ref[...] = x3
            return
        err = x3 - t_ref[...]
        dy_ref[...] = err * (1.0 / d)
        _zero_at_first_step(l_ref)
        part = jnp.sum(jnp.sum(err * err, axis=-1, keepdims=True), axis=0, keepdims=True)
        l_ref[...] += jnp.broadcast_to(part * (0.5 / d), l_ref.shape)

    last = [_rows(tm, d), pl.BlockSpec((1, LANES), lambda i: (0, 0))] if with_loss else [_rows(tm, d)]
    last_shape = [jax.ShapeDtypeStruct((m, d), F32)] + ([jax.ShapeDtypeStruct((1, LANES), F32)] if with_loss else [])
    return pl.pallas_call(
        body, name="ffn_fwd_loss" if with_loss else "ffn_fwd", grid=(m // tm,),
        in_specs=[_rows(tm, d), _rows(tm, d), _whole((2 * f, d)), _whole((f, d))] + ([_rows(tm, d)] if with_loss else []),
        out_specs=[_rows(tm, 2 * f), _rows(tm, f)] + last,
        out_shape=[jax.ShapeDtypeStruct((m, 2 * f), F32), jax.ShapeDtypeStruct((m, f), BF16)] + last_shape,
        compiler_params=_cp(),
    )(*([h2, x2, wt_gu, w_down] + ([target] if with_loss else [])))


def ffn_bwd(dx3, gu, x2, g_f, w_down, wt_gu, *, tm=256):
    m, d = x2.shape
    f = w_down.shape[0]

    def body(dx3_ref, gu_ref, x2_ref, g_ref, w_down_ref, wt_gu_ref, dgu_ref, dx2_ref, dg_ref):
        _zero_at_first_step(dg_ref)
        dx3 = dx3_ref[...]
        dx3_b = dx3.astype(BF16)
        for s, sz in _chunks(f):
            da = _dot_nt(dx3_b, w_down_ref[s:s + sz, :])
            g = gu_ref[:, s:s + sz]
            u = gu_ref[:, f + s:f + s + sz]
            sg = _sigmoid(g)
            dgu_ref[:, s:s + sz] = (da * u * (sg * (1.0 + g * (1.0 - sg)))).astype(dgu_ref.dtype)
            dgu_ref[:, f + s:f + s + sz] = (da * (g * sg)).astype(dgu_ref.dtype)
        dh2 = _dot_nn(dgu_ref[...], wt_gu_ref[...])
        xh, r = _rms(x2_ref[...])
        dg_ref[...] += jnp.sum(dh2 * xh, axis=0, keepdims=True)
        dx2_ref[...] = dx3 + _rms_bwd(dh2, xh, r, g_ref[...])

    return pl.pallas_call(
        body, name="ffn_bwd", grid=(m // tm,),
        in_specs=[_rows(tm, d), _rows(tm, 2 * f), _rows(tm, d), _vec(d), _whole((f, d)), _whole((2 * f, d))],
        out_specs=[_rows(tm, 2 * f), _rows(tm, d), _vec(d)],
        out_shape=[jax.ShapeDtypeStruct((m, 2 * f), BF16), jax.ShapeDtypeStruct((m, d), F32),
                   jax.ShapeDtypeStruct((1, d), F32)],
        compiler_params=_cp(),
    )(dx3, gu, x2, g_f.reshape(1, d), w_down, wt_gu)


def mid_bwd(dx2, qx, kv, xqg, xkg, x1, g_x, wo, wq, w_out, *, tm=512):
    m, d = x1.shape
    n_mem = kv.shape[0]
    nt = m // tm

    def body(dx2_ref, qx_ref, kv_ref, xqg_ref, xkg_ref, x1_ref, g_x_ref, wo_ref, wq_ref, w_out_ref,
             dq_ref, dx1_ref, dmixed_ref, dkv_ref, dqg_ref, dkg_ref, dg_ref):
        i = pl.program_id(0)
        _zero_at_first_step(dkv_ref, dqg_ref, dkg_ref, dg_ref)
        qg_v, kg_v = xqg_ref[...], xkg_ref[...]
        dx2 = dx2_ref[...]
        do = _dot_nt(dx2, wo_ref[...])
        dqg_acc = jnp.zeros((1, X_HEAD_DIM), F32)
        heads = _xattn_heads(qx_ref, kv_ref, qg_v, kg_v, d)
        head_cols = [slice(h * X_HEAD_DIM, (h + 1) * X_HEAD_DIM) for h in range(N_X_HEADS)]
        do_h = [do[:, cols].astype(BF16) for cols in head_cols]
        dps = [_dot_nt(do_h[h], heads[h][4]) for h in range(N_X_HEADS)]
        dss = []
        for (_, _, _, _, _, p), dp in zip(heads, dps):
            dss.append((p.astype(BF16), (p * (dp - jnp.sum(p * dp, axis=-1, keepdims=True))).astype(BF16)))
        for h, ((qh, rq, qn, kn, _, _), (p, ds)) in enumerate(zip(heads, dss)):
            cols = head_cols[h]
            vcols = slice(d + h * X_HEAD_DIM, d + (h + 1) * X_HEAD_DIM)
            dkv_ref[:, vcols] += _dot_tn(p, do_h[h])
            dqn = _dot_nn(ds, kn) * (X_HEAD_DIM ** -0.5)
            dkv_ref[:, cols] += _dot_tn(ds, qn) * (X_HEAD_DIM ** -0.5)
            dqg_acc = dqg_acc + jnp.sum(dqn * qh, axis=0, keepdims=True)
            dq_ref[:, cols] = _rms_bwd(dqn, qh, rq, qg_v).astype(dq_ref.dtype)
        dqg_ref[...] += dqg_acc
        dh1 = _dot_nt(dq_ref[...], wq_ref[...])
        xh, r = _rms(x1_ref[...])
        dg_ref[...] += jnp.sum(dh1 * xh, axis=0, keepdims=True)
        dx1 = dx2 + _rms_bwd(dh1, xh, r, g_x_ref[...])
        dx1_ref[...] = dx1
        dmixed_ref[...] = _dot_nt(dx1, w_out_ref[...])

        @pl.when(i == nt - 1)
        def _():
            dkg_acc = jnp.zeros((1, X_HEAD_DIM), F32)
            for h in range(N_X_HEADS):
                cols = slice(h * X_HEAD_DIM, (h + 1) * X_HEAD_DIM)
                kh, rk = _rms(kv_ref[:, cols])
                dkn = dkv_ref[:, cols]
                dkg_acc = dkg_acc + jnp.sum(dkn * kh, axis=0, keepdims=True)
                dkv_ref[:, cols] = _rms_bwd(dkn, kh, rk, kg_v)
            dkg_ref[...] = dkg_acc

    sq = _whole((d, d))
    full = pl.BlockSpec((n_mem, 2 * d), lambda i: (0, 0))
    return pl.pallas_call(
        body, name="mid_bwd", grid=(nt,),
        in_specs=[_rows(tm, d), _rows(tm, d), _whole((n_mem, 2 * d)), _vec(X_HEAD_DIM), _vec(X_HEAD_DIM), _rows(tm, d),
                  _vec(d), sq, sq, sq],
        out_specs=[_rows(tm, d), _rows(tm, d), _rows(tm, d), full, _vec(X_HEAD_DIM), _vec(X_HEAD_DIM), _vec(d)],
        out_shape=[jax.ShapeDtypeStruct((m, d), BF16), jax.ShapeDtypeStruct((m, d), F32), jax.ShapeDtypeStruct((m, d), F32),
                   jax.ShapeDtypeStruct((n_mem, 2 * d), F32), jax.ShapeDtypeStruct((1, X_HEAD_DIM), F32),
                   jax.ShapeDtypeStruct((1, X_HEAD_DIM), F32), jax.ShapeDtypeStruct((1, d), F32)],
        compiler_params=_cp(),
    )(dx2, qx, kv, xqg.reshape(1, X_HEAD_DIM), xkg.reshape(1, X_HEAD_DIM), x1, g_x.reshape(1, d), wo, wq, w_out)


def in_bwd(du, wt_in, x0, g_mix, dx1, *, tm=512):
    m, d = x0.shape
    n = wt_in.shape[0]

    def body(du_ref, wt_ref, x0_ref, g_ref, dx1_ref, dx0_ref, dg_ref):
        _zero_at_first_step(dg_ref)
        dh0 = _dot_nn(du_ref[...], wt_ref[...])
        xh, r = _rms(x0_ref[...])
        dg_ref[...] += jnp.sum(dh0 * xh, axis=0, keepdims=True)
        dx0_ref[...] = dx1_ref[...] + _rms_bwd(dh0, xh, r, g_ref[...])

    return pl.pallas_call(
        body, name="in_bwd", grid=(m // tm,),
        in_specs=[_rows(tm, n), _whole((n, d)), _rows(tm, d), _vec(d), _rows(tm, d)],
        out_specs=[_rows(tm, d), _vec(d)],
        out_shape=[jax.ShapeDtypeStruct((m, d), F32), jax.ShapeDtypeStruct((1, d), F32)],
        compiler_params=_cp(),
    )(du, wt_in, x0, g_mix.reshape(1, d), dx1)


SWA_TILE = 512
SWA_SUB = SWA_TILE // BLOCK
SWA_KEYS = SWA_TILE + BLOCK
PAIR = 2 * HEAD_DIM
KCOL = ATTN_WIDTH
VCOL = ATTN_WIDTH + KV_WIDTH


def _swa_constants():
    r = np.arange(2 * BLOCK)[:, None]
    j = np.arange(4 * BLOCK)[None, :]
    dist = (r % BLOCK) + BLOCK - (j % (2 * BLOCK))
    valid = (dist >= 0) & (dist < BLOCK)
    first_valid = valid & ((j % (2 * BLOCK)) >= BLOCK)
    bias, bias_first = [], []
    for kv in range(N_KV_HEADS):
        head = kv * GROUP + 2 * (r // BLOCK) + j // (2 * BLOCK)
        b = -(2.0 ** -(head + 1.0)) * dist
        bias.append(np.where(valid, b, NEG))
        bias_first.append(np.where(first_valid, b, NEG))
    lane = np.arange(LANES)
    seg = (lane[:, None] // HEAD_DIM == lane[None, :] // HEAD_DIM) / HEAD_DIM
    row = np.arange(4 * BLOCK)[:, None]
    ones = (row // (2 * BLOCK)) == (lane[None, :] // HEAD_DIM)
    return (jnp.asarray(np.stack(bias), F32), jnp.asarray(np.stack(bias_first), F32), jnp.asarray(seg, BF16),
            jnp.asarray(ones, BF16))


def _segmean(x, seg_ref):
    hi = x.astype(BF16)
    lo = (x - hi.astype(F32)).astype(BF16)
    return _dot_nn(hi, seg_ref[...]) + _dot_nn(lo, seg_ref[...])


def _two_heads(x, kv):
    lane = lax.broadcasted_iota(jnp.int32, (1, LANES), 1)
    mine = (lane < HEAD_DIM) if kv == 0 else (lane >= HEAD_DIM)
    base = jnp.where(mine, x, 0.0)
    other = pltpu.roll(base, HEAD_DIM, 1)
    return jnp.concatenate([base, other] if kv == 0 else [other, base], axis=0)


def _from_two_heads(y, kv):
    rows = y.shape[0] // 2
    lane = lax.broadcasted_iota(jnp.int32, (1, LANES), 1)
    top, bot = y[:rows], y[rows:]
    if kv == 0:
        return jnp.where(lane < HEAD_DIM, top + pltpu.roll(bot, HEAD_DIM, 1), 0.0)
    return jnp.where(lane >= HEAD_DIM, pltpu.roll(top, HEAD_DIM, 1) + bot, 0.0)


def _pair_rows(ref, rows, kv):
    c = kv * 2 * PAIR
    return jnp.concatenate([ref[rows, c:c + PAIR], ref[rows, c + PAIR:c + 2 * PAIR]], axis=0)


def _head_cols(fn, kv):
    return [jnp.concatenate([fn(kv * GROUP + half), fn(kv * GROUP + 2 + half)], axis=0) for half in range(2)]


def _swa_prologue(cur_ref, prev_ref, qg_ref, kg_ref, seg_ref, qg_s, kn_s, v_s):
    qg_s[...] = (cur_ref[:, 0:ATTN_WIDTH] * qg_ref[...]).astype(BF16)
    k = jnp.concatenate([prev_ref[:, KCOL:KCOL + KV_WIDTH], cur_ref[:, KCOL:KCOL + KV_WIDTH]], axis=0)
    kn_s[...] = k * lax.rsqrt(_segmean(k * k, seg_ref) + EPS) * kg_ref[...]
    v_s[0:BLOCK, :] = prev_ref[:, VCOL:VCOL + KV_WIDTH]
    v_s[BLOCK:SWA_KEYS, :] = cur_ref[:, VCOL:VCOL + KV_WIDTH]


def _swa_products(qg_s, kn_s, rows, keys, kv):
    q2 = _pair_rows(qg_s, rows, kv)
    k2 = _two_heads(kn_s[keys, :], kv)
    return q2, k2, _dot_nt(q2, k2)


def _swa_scores(cur_ref, sinks_ref, qg_s, kn_s, bias, rows, keys, kv):
    q2, k2, t = _swa_products(qg_s, kn_s, rows, keys, kv)
    return q2, k2, t, _swa_softmax(cur_ref, sinks_ref, t, bias, rows, kv)


def _swa_softmax(cur_ref, sinks_ref, t, bias, rows, kv):
    def rq(h):
        x = cur_ref[rows, h * HEAD_DIM:(h + 1) * HEAD_DIM]
        return lax.rsqrt(jnp.mean(x * x, axis=-1, keepdims=True) + EPS)

    scale = _head_cols(lambda h: rq(h) * (HEAD_DIM ** -0.5), kv)
    sink = _head_cols(lambda h: jnp.full((BLOCK, 1), sinks_ref[h], F32), kv)
    halves = []
    for half in range(2):
        cols = slice(half * 2 * BLOCK, (half + 1) * 2 * BLOCK)
        s = t[:, cols] * scale[half] + bias[:, cols]
        mx = jnp.maximum(jnp.max(s, axis=-1, keepdims=True), sink[half])
        halves.append((scale[half], jnp.exp(s - mx), jnp.exp(sink[half] - mx)))
    return halves


def swa_fwd(u, qg, kg, sinks):
    t_rows = u.shape[0]
    nt = t_rows // SWA_TILE
    bias_c, bias_first_c, seg_c, ones_c = _swa_constants()

    def body(sinks_ref, cur_ref, prev_ref, qg_ref, kg_ref, seg_ref, bias_ref, biasf_ref, ones_ref, o_ref, qg_s, kn_s, v_s):
        i = pl.program_id(0)
        _swa_prologue(cur_ref, prev_ref, qg_ref, kg_ref, seg_ref, qg_s, kn_s, v_s)
        lane = lax.broadcasted_iota(jnp.int32, (1, LANES), 1)
        work = [(b, kv, slice(b * BLOCK, (b + 1) * BLOCK), slice(b * BLOCK, (b + 2) * BLOCK))
                for b in range(SWA_SUB) for kv in range(N_KV_HEADS)]
        products = [_swa_products(qg_s, kn_s, rows, keys, kv)[2] for _, kv, rows, keys in work]
        scored = []
        for (b, kv, rows, _), t in zip(work, products):
            bias = jnp.where(i == 0, biasf_ref[kv], bias_ref[kv]) if b == 0 else bias_ref[kv]
            halves = _swa_softmax(cur_ref, sinks_ref, t, bias, rows, kv)
            scored.append((jnp.concatenate([halves[0][1], halves[1][1]], axis=1).astype(BF16), halves[0][2], halves[1][2]))
        for (b, kv, rows, keys), (e, es0, es1) in zip(work, scored):
            v2 = jnp.concatenate([_two_heads(v_s[keys, :], kv).astype(BF16), ones_ref[...]], axis=1)
            ox = _dot_nn(e, v2)
            den = ox[:, LANES:] + jnp.where(lane < HEAD_DIM, es0, es1)
            out = (ox[:, :LANES] / den).astype(o_ref.dtype)
            c = kv * 2 * PAIR
            o_ref[rows, c:c + PAIR] = out[:BLOCK]
            o_ref[rows, c + PAIR:c + 2 * PAIR] = out[BLOCK:]

    const3 = pl.BlockSpec((N_KV_HEADS, 2 * BLOCK, 4 * BLOCK), lambda i: (0, 0, 0))
    return pl.pallas_call(
        body, name="swa_fwd", grid=(nt,),
        in_specs=[
            pl.BlockSpec(memory_space=pltpu.SMEM),
            pl.BlockSpec((SWA_TILE, QKV_WIDTH), lambda i: (i, 0)),
            pl.BlockSpec((BLOCK, QKV_WIDTH), lambda i: (jnp.maximum(i * SWA_SUB - 1, 0), 0)),
            pl.BlockSpec((1, ATTN_WIDTH), lambda i: (0, 0)), pl.BlockSpec((1, KV_WIDTH), lambda i: (0, 0)),
            pl.BlockSpec((LANES, LANES), lambda i: (0, 0)), const3, const3,
            pl.BlockSpec((4 * BLOCK, LANES), lambda i: (0, 0)),
        ],
        out_specs=pl.BlockSpec((SWA_TILE, ATTN_WIDTH), lambda i: (i, 0)),
        out_shape=jax.ShapeDtypeStruct((t_rows, 2 * ATTN_WIDTH), BF16),
        scratch_shapes=[pltpu.VMEM((SWA_TILE, ATTN_WIDTH), BF16), pltpu.VMEM((SWA_KEYS, KV_WIDTH), F32),
                        pltpu.VMEM((SWA_KEYS, KV_WIDTH), F32)],
        compiler_params=_cp(),
    )(sinks, u, u, jnp.tile(qg, N_Q_HEADS).reshape(1, ATTN_WIDTH), jnp.tile(kg, N_KV_HEADS).reshape(1, KV_WIDTH),
      seg_c, bias_c, bias_first_c, ones_c)


def swa_bwd(u, dmixed, qg, kg, sinks):
    t_rows = u.shape[0]
    nt = t_rows // SWA_TILE
    bias_c, bias_first_c, seg_c, _ = _swa_constants()

    def body(sinks_ref, cur_ref, prev_ref, do_ref, qg_ref, kg_ref, seg_ref, bias_ref, biasf_ref,
             du_ref, dqg_ref, dkg_ref, dsk_ref, qg_s, kn_s, v_s, acck_s, accv_s, carryk_s, carryv_s):
        step = pl.program_id(0)
        i = nt - 1 - step

        @pl.when(step == 0)
        def _():
            for r in (carryk_s, carryv_s, dqg_ref, dkg_ref, dsk_ref):
                r[...] = jnp.zeros_like(r)

        _swa_prologue(cur_ref, prev_ref, qg_ref, kg_ref, seg_ref, qg_s, kn_s, v_s)
        for acc, carry in ((acck_s, carryk_s), (accv_s, carryv_s)):
            acc[0:SWA_TILE, :] = jnp.zeros((SWA_TILE, KV_WIDTH), F32)
            acc[SWA_TILE:SWA_KEYS, :] = carry[...]

        lane = lax.broadcasted_iota(jnp.int32, (1, LANES), 1)
        g_pair = qg_ref[:, 0:PAIR]
        dqg_acc = jnp.zeros((1, PAIR), F32)
        dsk_acc = jnp.zeros((1, LANES), F32)
        work = [(b, kv, slice(b * BLOCK, (b + 1) * BLOCK), slice(b * BLOCK, (b + 2) * BLOCK))
                for b in range(SWA_SUB) for kv in range(N_KV_HEADS)]
        products = []
        for _, kv, rows, keys in work:
            q2, k2, t = _swa_products(qg_s, kn_s, rows, keys, kv)
            do2 = _pair_rows(do_ref, rows, kv).astype(BF16)
            products.append((q2, k2, t, do2, _dot_nt(do2, _two_heads(v_s[keys, :], kv))))
        exps = []
        for (b, kv, rows, _), (_, _, t, _, _) in zip(work, products):
            bias = jnp.where(i == 0, biasf_ref[kv], bias_ref[kv]) if b == 0 else bias_ref[kv]
            exps.append(_swa_softmax(cur_ref, sinks_ref, t, bias, rows, kv))
        softmaxed = []
        for (b, kv, rows, _), (_, _, t, _, dp), halves in zip(work, products, exps):
            p_parts, dt_parts, coef = [], [], []
            for half, (scale, e, es) in enumerate(halves):
                cols = slice(half * 2 * BLOCK, (half + 1) * 2 * BLOCK)
                rden = 1.0 / (jnp.sum(e, axis=-1, keepdims=True) + es)
                p = e * rden
                dp_h = dp[:, cols]
                delta = jnp.sum(p * dp_h, axis=-1, keepdims=True)
                ds = p * (dp_h - delta)
                dsink = -(es * rden) * delta
                for pair in range(2):
                    part = jnp.sum(dsink[pair * BLOCK:(pair + 1) * BLOCK], axis=0, keepdims=True)
                    dsk_acc = dsk_acc + jnp.where(lane == kv * GROUP + 2 * pair + half, part, 0.0)
                dscale = jnp.sum(ds * t[:, cols], axis=-1, keepdims=True)
                coef.append(-dscale * scale * scale * scale)
                p_parts.append(p.astype(BF16))
                dt_parts.append((ds * scale).astype(BF16))
            softmaxed.append((jnp.concatenate(p_parts, axis=1), jnp.concatenate(dt_parts, axis=1),
                              jnp.where(lane < HEAD_DIM, coef[0], coef[1])))
        for (_, kv, rows, keys), (q2, k2, _, do2, _), (p2, dt, coef) in zip(work, products, softmaxed):
            dqg2 = _dot_nn(dt, k2)
            q_raw = _pair_rows(cur_ref, rows, kv)
            dq = dqg2 * g_pair + coef * q_raw
            dqg_acc = dqg_acc + jnp.sum(dqg2 * q_raw, axis=0, keepdims=True)
            c = kv * 2 * PAIR
            du_ref[rows, c:c + PAIR] = dq[:BLOCK].astype(du_ref.dtype)
            du_ref[rows, c + PAIR:c + 2 * PAIR] = dq[BLOCK:].astype(du_ref.dtype)
        to_keys = [(_from_two_heads(_dot_tn(dt, q2), kv), _from_two_heads(_dot_tn(p2, do2), kv))
                   for (_, kv, _, _), (q2, _, _, do2, _), (p2, dt, _) in zip(work, products, softmaxed)]
        for (_, _, _, keys), (dk, dv) in zip(work, to_keys):
            acck_s[keys, :] += dk
            accv_s[keys, :] += dv
        dqg_ref[...] += dqg_acc + pltpu.roll(dqg_acc, HEAD_DIM, 1)
        dsk_ref[...] += dsk_acc

        own = slice(BLOCK, SWA_KEYS)
        k = cur_ref[:, KCOL:KCOL + KV_WIDTH]
        rk = lax.rsqrt(_segmean(k * k, seg_ref) + EPS)
        kh = k * rk
        dkn = acck_s[own, :]
        dkh = dkn * kg_ref[...]
        du_ref[:, KCOL:KCOL + KV_WIDTH] = (rk * (dkh - kh * _segmean(dkh * kh, seg_ref))).astype(du_ref.dtype)
        du_ref[:, VCOL:VCOL + KV_WIDTH] = accv_s[own, :].astype(du_ref.dtype)
        dkg_part = jnp.sum(dkn * kh, axis=0, keepdims=True)
        dkg_ref[...] += dkg_part + pltpu.roll(dkg_part, HEAD_DIM, 1)
        carryk_s[...] = acck_s[0:BLOCK, :]
        carryv_s[...] = accv_s[0:BLOCK, :]

    const3 = pl.BlockSpec((N_KV_HEADS, 2 * BLOCK, 4 * BLOCK), lambda s: (0, 0, 0))
    vec = pl.BlockSpec((1, LANES), lambda s: (0, 0))
    return pl.pallas_call(
        body, name="swa_bwd", grid=(nt,),
        in_specs=[
            pl.BlockSpec(memory_space=pltpu.SMEM),
            pl.BlockSpec((SWA_TILE, QKV_WIDTH), lambda s: (nt - 1 - s, 0)),
            pl.BlockSpec((BLOCK, QKV_WIDTH), lambda s: (jnp.maximum((nt - 1 - s) * SWA_SUB - 1, 0), 0)),
            pl.BlockSpec((SWA_TILE, ATTN_WIDTH), lambda s: (nt - 1 - s, 0)),
            pl.BlockSpec((1, ATTN_WIDTH), lambda s: (0, 0)), vec,
            pl.BlockSpec((LANES, LANES), lambda s: (0, 0)), const3, const3,
        ],
        out_specs=[pl.BlockSpec((SWA_TILE, QKV_WIDTH), lambda s: (nt - 1 - s, 0)), vec, vec, vec],
        out_shape=[jax.ShapeDtypeStruct((t_rows, IN_COLS), BF16)] + [jax.ShapeDtypeStruct((1, LANES), F32)] * 3,
        scratch_shapes=[pltpu.VMEM((SWA_TILE, ATTN_WIDTH), BF16)] + [pltpu.VMEM((SWA_KEYS, KV_WIDTH), F32)] * 4
        + [pltpu.VMEM((BLOCK, KV_WIDTH), F32)] * 2,
        compiler_params=_cp(),
    )(sinks, u, u, dmixed, jnp.tile(qg, N_Q_HEADS).reshape(1, ATTN_WIDTH), jnp.tile(kg, N_KV_HEADS).reshape(1, KV_WIDTH),
      seg_c, bias_c, bias_first_c)


CONV_TILE = 512
CONV_CHUNK = 64
VAL0 = QKV_WIDTH
GATE0 = QKV_WIDTH + CONV_CH


def _glu(ref):
    return ref[:, VAL0:GATE0] * _sigmoid(ref[:, GATE0:GATE0 + CONV_CH])


SUBLANES = 8
CONV_BUF = CONV_HALO + CONV_TILE + SUBLANES
CONV_EXT = CONV_HALO + CONV_TILE


def _fill_shifted(sh_ref):
    for r in range(1, SUBLANES):
        sh_ref[r, 0:CONV_EXT, :] = sh_ref[0, pl.ds(r, CONV_EXT), :]


def _shifted(sh_ref, start, offset, n):
    return sh_ref[offset % SUBLANES, pl.ds(start + offset - offset % SUBLANES, n), :]


def _layernorm_stats(y):
    mu = jnp.mean(y, axis=-1, keepdims=True)
    yc = y - mu
    rstd = lax.rsqrt(jnp.mean(yc * yc, axis=-1, keepdims=True) + EPS)
    return yc * rstd, rstd


def conv_fwd(u, mixed, conv_w, conv_b, ln_g, ln_b):
    t = u.shape[0]
    nt = t // CONV_TILE
    per = CONV_TILE // CONV_HALO

    def body(cur_ref, prev_ref, mixed_ref, w_ref, b_ref, g_ref, b2_ref, o_ref, y_ref, gl_ref):
        del mixed_ref
        i = pl.program_id(0)
        gl_ref[0, 0:CONV_HALO, :] = jnp.where(i > 0, _glu(prev_ref), 0.0)
        gl_ref[0, CONV_HALO:CONV_EXT, :] = _glu(cur_ref)
        gl_ref[0, CONV_EXT:CONV_BUF, :] = jnp.zeros((SUBLANES, CONV_CH), F32)
        _fill_shifted(gl_ref)
        for c0 in range(0, CONV_TILE, CONV_CHUNK):
            acc = jnp.broadcast_to(b_ref[...], (CONV_CHUNK, CONV_CH))
            for k in range(CONV_K):
                acc = acc + w_ref[k:k + 1, :] * _shifted(gl_ref, c0, 2 + k, CONV_CHUNK)
            y_ref[c0:c0 + CONV_CHUNK, :] = acc
        yh, _ = _layernorm_stats(y_ref[...])
        yln = yh * g_ref[...] + b2_ref[...]
        o_ref[...] = (yln * _sigmoid(yln)).astype(o_ref.dtype)

    vec = pl.BlockSpec((1, CONV_CH), lambda i: (0, 0))
    return pl.pallas_call(
        body, name="conv_fwd", grid=(nt,),
        in_specs=[
            pl.BlockSpec((CONV_TILE, IN_COLS), lambda i: (i, 0)),
            pl.BlockSpec((CONV_HALO, IN_COLS), lambda i: (jnp.maximum(i * per - 1, 0), 0)),
            pl.BlockSpec(memory_space=pl.ANY),
            pl.BlockSpec((CONV_HALO, CONV_CH), lambda i: (0, 0)),
            vec, vec, vec,
        ],
        out_specs=[pl.BlockSpec((CONV_TILE, CONV_CH), lambda i: (i, 1)), pl.BlockSpec((CONV_TILE, CONV_CH), lambda i: (i, 0))],
        out_shape=[jax.ShapeDtypeStruct(mixed.shape, mixed.dtype), jax.ShapeDtypeStruct((t, CONV_CH), F32)],
        scratch_shapes=[pltpu.VMEM((SUBLANES, CONV_BUF, CONV_CH), F32)],
        input_output_aliases={2: 0}, compiler_params=_cp(),
    )(u, u, mixed, conv_w, conv_b.reshape(1, CONV_CH), ln_g.reshape(1, CONV_CH), ln_b.reshape(1, CONV_CH))


def conv_bwd(u, y, dmixed, du, conv_w, ln_g, ln_b):
    t = u.shape[0]
    nt = t // CONV_TILE
    per = CONV_TILE // CONV_HALO

    def body(cur_ref, prev_ref, y_ref, yn_ref, do_ref, don_ref, du_in_ref, w_ref, g_ref, b2_ref,
             du_ref, dw_ref, dvec_ref, gl_ref, dy_ref):
        i = pl.program_id(0)
        last = i == nt - 1
        _zero_at_first_step(dw_ref, dvec_ref)

        gl_ref[0, 0:CONV_HALO, :] = jnp.where(i > 0, _glu(prev_ref), 0.0)
        gl_ref[0, CONV_HALO:CONV_EXT, :] = _glu(cur_ref)
        gl_ref[0, CONV_EXT:CONV_BUF, :] = jnp.zeros((SUBLANES, CONV_CH), F32)
        _fill_shifted(gl_ref)

        yh, rstd = _layernorm_stats(jnp.concatenate([y_ref[...], yn_ref[...]], axis=0))
        g = g_ref[...]
        yln = yh * g + b2_ref[...]
        sg = _sigmoid(yln)
        dout = jnp.concatenate([do_ref[...], jnp.where(last, 0.0, don_ref[...])], axis=0)
        dyln = dout * (sg * (1.0 + yln * (1.0 - sg)))
        dyh = dyln * g
        dy = rstd * (dyh - jnp.mean(dyh, axis=-1, keepdims=True) - yh * jnp.mean(dyh * yh, axis=-1, keepdims=True))
        dy_ref[0, 0:CONV_EXT, :] = dy
        dy_ref[0, CONV_EXT:CONV_BUF, :] = jnp.zeros((SUBLANES, CONV_CH), F32)
        _fill_shifted(dy_ref)

        own = slice(0, CONV_TILE)
        dvec_ref[0:1, :] += jnp.sum(dy[own], axis=0, keepdims=True)
        dvec_ref[1:2, :] += jnp.sum(dyln[own] * yh[own], axis=0, keepdims=True)
        dvec_ref[2:3, :] += jnp.sum(dyln[own], axis=0, keepdims=True)
        for k in range(CONV_K):
            dw_ref[k:k + 1, :] += jnp.sum(dy[own] * _shifted(gl_ref, 0, 2 + k, CONV_TILE), axis=0, keepdims=True)

        for c0 in range(0, CONV_TILE, CONV_CHUNK):
            acc = jnp.zeros((CONV_CHUNK, CONV_CH), F32)
            for k in range(CONV_K):
                acc = acc + w_ref[k:k + 1, :] * _shifted(dy_ref, c0, CONV_K - 1 - k, CONV_CHUNK)
            rows = slice(c0, c0 + CONV_CHUNK)
            val = cur_ref[rows, VAL0:GATE0]
            sgate = _sigmoid(cur_ref[rows, GATE0:GATE0 + CONV_CH])
            du_ref[rows, VAL0:GATE0] = (acc * sgate).astype(du_ref.dtype)
            du_ref[rows, GATE0:GATE0 + CONV_CH] = (acc * val * sgate * (1.0 - sgate)).astype(du_ref.dtype)
        du_ref[:, 0:QKV_WIDTH] = du_in_ref[:, 0:QKV_WIDTH]

    vec = pl.BlockSpec((1, CONV_CH), lambda i: (0, 0))
    n_halo = t // CONV_HALO
    return pl.pallas_call(
        body, name="conv_bwd", grid=(nt,),
        in_specs=[
            pl.BlockSpec((CONV_TILE, IN_COLS), lambda i: (i, 0)),
            pl.BlockSpec((CONV_HALO, IN_COLS), lambda i: (jnp.maximum(i * per - 1, 0), 0)),
            pl.BlockSpec((CONV_TILE, CONV_CH), lambda i: (i, 0)),
            pl.BlockSpec((CONV_HALO, CONV_CH), lambda i: (jnp.minimum((i + 1) * per, n_halo - 1), 0)),
            pl.BlockSpec((CONV_TILE, CONV_CH), lambda i: (i, 1)),
            pl.BlockSpec((CONV_HALO, CONV_CH), lambda i: (jnp.minimum((i + 1) * per, n_halo - 1), 1)),
            pl.BlockSpec((CONV_TILE, IN_COLS), lambda i: (i, 0)),
            pl.BlockSpec((CONV_HALO, CONV_CH), lambda i: (0, 0)),
            vec, vec,
        ],
        out_specs=[
            pl.BlockSpec((CONV_TILE, IN_COLS), lambda i: (i, 0)),
            pl.BlockSpec((CONV_HALO, CONV_CH), lambda i: (0, 0)),
            pl.BlockSpec((8, CONV_CH), lambda i: (0, 0)),
        ],
        out_shape=[
            jax.ShapeDtypeStruct(du.shape, du.dtype),
            jax.ShapeDtypeStruct((CONV_HALO, CONV_CH), F32),
            jax.ShapeDtypeStruct((8, CONV_CH), F32),
        ],
        scratch_shapes=[pltpu.VMEM((SUBLANES, CONV_BUF, CONV_CH), F32), pltpu.VMEM((SUBLANES, CONV_BUF, CONV_CH), F32)],
        input_output_aliases={6: 0}, compiler_params=_cp(),
    )(u, u, y, y, dmixed, dmixed, du, conv_w, ln_g.reshape(1, CONV_CH), ln_b.reshape(1, CONV_CH))


def adamw(w, g, m, v, *, name):
    r, c = w.shape
    tr = r
    for cand in (512, 256, 128, 64, 32, 16, 8):
        if r % cand == 0 and r > cand:
            tr = cand
            break

    def body(w_ref, g_ref, m_ref, v_ref, d_ref, nm_ref, nv_ref):
        d_ref[...], nm_ref[...], nv_ref[...] = _adamw_math(w_ref[...], g_ref[...], m_ref[...], v_ref[...])

    spec = pl.BlockSpec((tr, c), lambda i: (i, 0))
    shape = jax.ShapeDtypeStruct((r, c), F32)
    return pl.pallas_call(
        body, name=name, grid=(r // tr,), in_specs=[spec] * 4, out_specs=[spec] * 3,
        out_shape=[shape] * 3, compiler_params=_cp(),
    )(w, g, m, v)


def _position():
    return lax.axis_index("x"), lax.axis_index("y"), lax.axis_index("c")


def all_gather_many(shards, *, name):
    n = len(shards)

    def body(*refs):
        x_refs, out_refs, token_ref = refs[:n], refs[n:2 * n], refs[2 * n]
        send_sems, recv_sems, local_sems = refs[2 * n + 1:]
        x, y, c = _position()
        me, sibling = (x, y, c), (x, y, 1 - c)
        chips = [(1 - x, y), (x, 1 - y), (1 - x, 1 - y)]
        token_ref[...] = jnp.zeros_like(token_ref)

        def rows(t, px, py, pc):
            return out_refs[t].at[4 * px + 2 * py + pc]

        def copy(t, k, block, to, src=None):
            return pltpu.make_async_remote_copy(
                src_ref=rows(t, *block) if src is None else src, dst_ref=rows(t, *block),
                send_sem=send_sems.at[7 * t + k], recv_sem=recv_sems.at[7 * t + k], device_id=to, device_id_type=MESH)

        mine = [pltpu.make_async_copy(x_refs[t], rows(t, *me), local_sems.at[t]) for t in range(n)]
        for cp in mine:
            cp.start()
        first = []
        for t in range(n):
            first.append(copy(t, 0, me, sibling, src=x_refs[t]))
            first += [copy(t, 1 + j, me, (*chip, c), src=x_refs[t]) for j, chip in enumerate(chips)]
        for cp in first:
            cp.start()
        passed = []
        for t in range(n):
            for j, chip in enumerate(chips):
                copy(t, 1 + j, (*chip, c), me).wait_recv()
                passed.append(copy(t, 4 + j, (*chip, c), sibling))
                passed[-1].start()
        for t in range(n):
            copy(t, 0, sibling, me).wait_recv()
            for j, chip in enumerate(chips):
                copy(t, 4 + j, (*chip, 1 - c), me).wait_recv()
        for cp in first + passed:
            cp.wait_send()
        for cp in mine:
            cp.wait()

    hbm = pl.BlockSpec(memory_space=pltpu.HBM)
    out = pl.pallas_call(
        body, name=name,
        out_shape=[jax.ShapeDtypeStruct((N_DEV,) + s.shape, s.dtype) for s in shards] + [jax.ShapeDtypeStruct((8, LANES), F32)],
        in_specs=[hbm] * n, out_specs=[hbm] * n + [pl.BlockSpec(memory_space=pltpu.VMEM)],
        scratch_shapes=[pltpu.SemaphoreType.DMA((7 * n,)), pltpu.SemaphoreType.DMA((7 * n,)), pltpu.SemaphoreType.DMA((n,))],
        compiler_params=_cp(),
    )(*shards)
    return out[:n], out[n]


_HBM = pl.BlockSpec(memory_space=pltpu.HBM)
_SEM = pl.BlockSpec(memory_space=pltpu.SEMAPHORE)
_EFFECT = pltpu.SideEffectType.DATAFLOW_SIDE_EFFECTING


def _split_copies(src_refs, land_refs, send_sems, recv_sems, plan, n_copies):
    copies = []
    for t, (src_ref, land_ref) in enumerate(zip(src_refs, land_refs)):
        for k in range(n_copies):
            s, d, to = plan(src_ref, land_ref, k)
            copies.append(pltpu.make_async_remote_copy(
                src_ref=s, dst_ref=d, send_sem=send_sems.at[n_copies * t + k], recv_sem=recv_sems.at[n_copies * t + k],
                device_id=to, device_id_type=MESH))
    return copies


def split_start(srcs, lands, plan, n_copies, *, name):
    n = len(srcs)

    def body(*refs):
        src_refs, land_refs, send_sems, recv_sems, token = refs[:n], refs[n:2 * n], refs[2 * n], refs[2 * n + 1], refs[-1]
        for cp in _split_copies(src_refs, land_refs, send_sems, recv_sems, plan, n_copies):
            cp.start()
        token[...] = jnp.zeros_like(token)

    both = list(srcs) + list(lands)
    out = pl.pallas_call(
        body, name=name,
        out_shape=(pltpu.SemaphoreType.DMA((n_copies * n,)), pltpu.SemaphoreType.DMA((n_copies * n,)),
                   *[pltpu.HBM(a.shape, a.dtype) for a in both], jax.ShapeDtypeStruct((8, LANES), F32)),
        in_specs=(_HBM,) * (2 * n), out_specs=(_SEM, _SEM) + (_HBM,) * (2 * n) + (pl.BlockSpec(memory_space=pltpu.VMEM),),
        input_output_aliases={i: 2 + i for i in range(2 * n)},
        compiler_params=pltpu.CompilerParams(has_side_effects=_EFFECT),
    )(*[pltpu.with_memory_space_constraint(a, pltpu.HBM) for a in both])
    return out[0], out[1], list(out[2:2 + n]), list(out[2 + n:2 + 2 * n]), out[-1]


def split_wait(started, after, plan, n_copies, *, name):
    send_sems, recv_sems, srcs, lands, _ = started
    n = len(srcs)

    def body(*refs):
        src_refs, land_refs, send_sems, recv_sems = refs[:n], refs[n:2 * n], refs[2 * n], refs[2 * n + 1]
        for cp in _split_copies(src_refs, land_refs, send_sems, recv_sems, plan, n_copies):
            cp.wait_send()
            cp.wait_recv()

    both = list(srcs) + list(lands)
    out = pl.pallas_call(
        body, name=name,
        out_shape=tuple(pltpu.HBM(a.shape, a.dtype) for a in both),
        in_specs=(_HBM,) * (2 * n) + (_SEM, _SEM, pl.BlockSpec(memory_space=pl.ANY)), out_specs=(_HBM,) * (2 * n),
        input_output_aliases={i: i for i in range(2 * n)},
        compiler_params=pltpu.CompilerParams(has_side_effects=_EFFECT),
    )(*both, send_sems, recv_sems, after)
    return list(out[:n]), list(out[n:])


def _other_chips(x, y):
    return [(1 - x, y), (x, 1 - y), (1 - x, 1 - y)]


def _remote(src, dst, send_sem, recv_sem, to):
    return pltpu.make_async_remote_copy(src_ref=src, dst_ref=dst, send_sem=send_sem, recv_sem=recv_sem,
                                        device_id=to, device_id_type=MESH)


def gather_start(groups, *, name):
    counts = [len(shards) for shards, _ in groups]
    flat = [a for shards, _ in groups for a in shards] + [a for _, lands in groups for a in lands]
    n_all, n_groups = sum(counts), len(groups)

    def body(*refs):
        s_refs, l_refs = refs[:n_all], refs[n_all:2 * n_all]
        sems = refs[2 * n_all:2 * n_all + 3 * n_groups]
        x, y, c = _position()
        me = 4 * x + 2 * y + c
        at = 0
        for gi, n in enumerate(counts):
            send, recv_sibling, recv_ici = sems[3 * gi:3 * gi + 3]
            for t in range(n):
                src, dst = s_refs[at + t], l_refs[at + t].at[me]
                _remote(src, dst, send.at[4 * t], recv_sibling.at[t], (x, y, 1 - c)).start()
                for j, chip in enumerate(_other_chips(x, y)):
                    _remote(src, dst, send.at[4 * t + 1 + j], recv_ici.at[3 * t + j], (*chip, c)).start()
            at += n
        refs[-1][...] = jnp.zeros_like(refs[-1])

    sem_shapes = [pltpu.SemaphoreType.DMA((k * n,)) for n in counts for k in (4, 1, 3)]
    out = pl.pallas_call(
        body, name=name,
        out_shape=(*sem_shapes, *[pltpu.HBM(a.shape, a.dtype) for a in flat], jax.ShapeDtypeStruct((8, LANES), F32)),
        in_specs=(_HBM,) * (2 * n_all),
        out_specs=(_SEM,) * (3 * n_groups) + (_HBM,) * (2 * n_all) + (pl.BlockSpec(memory_space=pltpu.VMEM),),
        input_output_aliases={i: 3 * n_groups + i for i in range(2 * n_all)},
        compiler_params=pltpu.CompilerParams(has_side_effects=_EFFECT),
    )(*[pltpu.with_memory_space_constraint(a, pltpu.HBM) for a in flat])
    thru = out[3 * n_groups:-1]
    states, at = [], 0
    for gi, n in enumerate(counts):
        states.append(dict(shards=list(thru[at:at + n]), lands=list(thru[n_all + at:n_all + at + n]),
                           send=out[3 * gi], recv_sibling=out[3 * gi + 1], recv_ici=out[3 * gi + 2]))
        at += n
    return states, out[-1]


def gather_forward(states, after, *, name):
    counts = [len(s["lands"]) for s in states]
    flat = [a for s in states for a in s["lands"]]
    n_all, n_groups = sum(counts), len(states)

    def body(*refs):
        l_refs = refs[:n_all]
        recv_ici = refs[n_all:n_all + n_groups]
        fwd = refs[n_all + n_groups + 1:n_all + n_groups + 1 + 2 * n_groups]
        x, y, c = _position()
        at = 0
        for gi, n in enumerate(counts):
            fwd_send, fwd_recv = fwd[2 * gi], fwd[2 * gi + 1]
            for t in range(n):
                for j, (px, py) in enumerate(_other_chips(x, y)):
                    block = l_refs[at + t].at[4 * px + 2 * py + c]
                    _remote(block, block, fwd_send.at[3 * t + j], recv_ici[gi].at[3 * t + j], (px, py, c)).wait_recv()
                    _remote(block, block, fwd_send.at[3 * t + j], fwd_recv.at[3 * t + j], (x, y, 1 - c)).start()
            at += n
        refs[-1][...] = jnp.zeros_like(refs[-1])

    sem_shapes = [pltpu.SemaphoreType.DMA((3 * n,)) for n in counts for _ in range(2)]
    out = pl.pallas_call(
        body, name=name,
        out_shape=(*sem_shapes, *[pltpu.HBM(a.shape, a.dtype) for a in flat], jax.ShapeDtypeStruct((8, LANES), F32)),
        in_specs=(_HBM,) * n_all + (_SEM,) * n_groups + (pl.BlockSpec(memory_space=pl.ANY),),
        out_specs=(_SEM,) * (2 * n_groups) + (_HBM,) * n_all + (pl.BlockSpec(memory_space=pltpu.VMEM),),
        input_output_aliases={i: 2 * n_groups + i for i in range(n_all)},
        compiler_params=pltpu.CompilerParams(has_side_effects=_EFFECT),
    )(*flat, *[s["recv_ici"] for s in states], after)
    at = 0
    for gi, (s, n) in enumerate(zip(states, counts)):
        s.update(fwd_send=out[2 * gi], fwd_recv=out[2 * gi + 1], lands=list(out[2 * n_groups + at:2 * n_groups + at + n]))
        at += n
    return out[-1]


def gather_finish(state, after, *, name):
    n = len(state["lands"])

    def body(*refs):
        s_refs, l_refs = refs[:n], refs[n:2 * n]
        send, recv_sibling, fwd_send, fwd_recv = refs[2 * n:2 * n + 4]
        x, y, c = _position()
        me = 4 * x + 2 * y + c
        for t in range(n):
            own = l_refs[t].at[me]
            _remote(s_refs[t], own, send.at[4 * t], recv_sibling.at[t], (x, y, 1 - c)).wait_send()
            _remote(s_refs[t], l_refs[t].at[4 * x + 2 * y + 1 - c], send.at[4 * t], recv_sibling.at[t], (x, y, 1 - c)).wait_recv()
            for j, (px, py) in enumerate(_other_chips(x, y)):
                _remote(s_refs[t], own, send.at[4 * t + 1 + j], recv_sibling.at[t], (px, py, c)).wait_send()
                mine, theirs = l_refs[t].at[4 * px + 2 * py + c], l_refs[t].at[4 * px + 2 * py + 1 - c]
                _remote(mine, mine, fwd_send.at[3 * t + j], fwd_recv.at[3 * t + j], (x, y, 1 - c)).wait_send()
                _remote(theirs, theirs, fwd_send.at[3 * t + j], fwd_recv.at[3 * t + j], (x, y, 1 - c)).wait_recv()

    both = state["shards"] + state["lands"]
    out = pl.pallas_call(
        body, name=name,
        out_shape=tuple(pltpu.HBM(a.shape, a.dtype) for a in both),
        in_specs=(_HBM,) * (2 * n) + (_SEM,) * 4 + (pl.BlockSpec(memory_space=pl.ANY),), out_specs=(_HBM,) * (2 * n),
        input_output_aliases={i: i for i in range(2 * n)},
        compiler_params=pltpu.CompilerParams(has_side_effects=_EFFECT),
    )(*both, state["send"], state["recv_sibling"], state["fwd_send"], state["fwd_recv"], after)
    return list(out[n:])


def _all_peers_plan(src_ref, land_ref, k):
    x, y, c = _position()
    bits = k + 1
    peer = ((1 - x) if bits & 4 else x, (1 - y) if bits & 2 else y, (1 - c) if bits & 1 else c)
    return src_ref, land_ref.at[4 * x + 2 * y + c], peer


def _sibling_plan(src_ref, land_ref, k):
    x, y, c = _position()
    return src_ref.at[2 * k + (1 - c)], land_ref.at[k], (x, y, 1 - c)


def _chips_plan(src_ref, land_ref, j):
    x, y, c = _position()
    px, py = _other_chips(x, y)[j]
    return src_ref.at[j], land_ref.at[j], (px, py, c)


SUM_STEPS = 2


def sum_for_chips(parts, from_sibling, ck_idx, *, name):
    n = len(parts)

    def body(ck_ref, *refs):
        del ck_ref
        for t in range(n):
            refs[2 * n + t][...] = (refs[t][...] + refs[n + t][...]).astype(BF16)

    def blk(a):
        return (None, a.shape[1] // SUM_STEPS, a.shape[2])

    return pl.pallas_call(
        body, name=name,
        grid_spec=pltpu.PrefetchScalarGridSpec(
            num_scalar_prefetch=1, grid=(3, SUM_STEPS),
            in_specs=[pl.BlockSpec(blk(a), lambda j, i, ck: (2 * ck[1 + j] + ck[0], i, 0)) for a in parts]
            + [pl.BlockSpec(blk(a), lambda j, i, ck: (ck[1 + j], i, 0)) for a in from_sibling],
            out_specs=[pl.BlockSpec(blk(a), lambda j, i, ck: (j, i, 0)) for a in from_sibling]),
        out_shape=[jax.ShapeDtypeStruct((3,) + a.shape[1:], BF16) for a in from_sibling], compiler_params=_cp(),
    )(ck_idx, *parts, *from_sibling)


def sum_final(parts, from_sibling, from_chips, kc_idx, *, name):
    n = len(parts)

    def body(kc_ref, *refs):
        del kc_ref
        for t in range(n):
            p, s, a, b, d = (refs[j * n + t] for j in range(5))
            refs[5 * n + t][...] = (((p[...] + s[...]) + a[...].astype(F32)) + b[...].astype(F32)) + d[...].astype(F32)

    def blk(a):
        return (None, a.shape[1] // SUM_STEPS, a.shape[2])

    def chip_specs(j):
        return [pl.BlockSpec(blk(a), lambda i, kc: (j, i, 0)) for a in from_chips]

    return pl.pallas_call(
        body, name=name,
        grid_spec=pltpu.PrefetchScalarGridSpec(
            num_scalar_prefetch=1, grid=(SUM_STEPS,),
            in_specs=[pl.BlockSpec(blk(a), lambda i, kc: (2 * kc[0] + kc[1], i, 0)) for a in parts]
            + [pl.BlockSpec(blk(a), lambda i, kc: (kc[0], i, 0)) for a in from_sibling]
            + chip_specs(0) + chip_specs(1) + chip_specs(2),
            out_specs=[pl.BlockSpec(blk(a)[1:], lambda i, kc: (i, 0)) for a in parts]),
        out_shape=[jax.ShapeDtypeStruct(a.shape[1:], F32) for a in parts], compiler_params=_cp(),
    )(kc_idx, *parts, *from_sibling, *from_chips, *from_chips, *from_chips)


BIG = (
    ("w_in", IN_COLS, True), ("w_out", D_MODEL, False), ("wq_x", D_MODEL, False), ("wkv_x", 2 * D_MODEL, True),
    ("wo_x", D_MODEL, False), ("w_gate_up", 2 * D_FF, True), ("w_down", D_FF, False),
)

SMALL = ("norm_mix_g", "q_norm_g", "k_norm_g", "sinks", "conv_b", "conv_ln_g", "conv_ln_b",
         "norm_x_g", "norm_mem_g", "xq_norm_g", "xk_norm_g", "norm_ffn_g")


def _adamw_math(w, g, m, v):
    m2 = ADAM_B1 * m + (1.0 - ADAM_B1) * g
    v2 = ADAM_B2 * v + (1.0 - ADAM_B2) * jnp.square(g)
    m_hat = m2 / (1.0 - ADAM_B1 ** ADAM_STEP)
    v_hat = v2 / (1.0 - ADAM_B2 ** ADAM_STEP)
    return -ADAM_LR * (m_hat / (jnp.sqrt(v_hat) + ADAM_EPS) + ADAM_WD * w), m2, v2


def _small_rows(per_layer_shape):
    return 1 if len(per_layer_shape) == 1 else per_layer_shape[0]


def pack_small(parts, shapes):
    blocks = []
    for per_layer, sh in zip(parts, shapes):
        for g in per_layer:
            g = g.reshape(_small_rows(sh), sh[-1])
            blocks.append(jnp.pad(g, ((0, 0), (0, D_MODEL - sh[-1]))))
    rows = sum(b.shape[0] for b in blocks)
    blocks.append(jnp.zeros((-rows % 8, D_MODEL), F32))
    return jnp.concatenate(blocks, axis=0)


def update_small(gathered, shapes, weights, moments_m, moments_v, n_update):
    n_all = len(shapes)

    def body(*refs):
        g_ref = refs[0]
        w_refs, m_refs, v_refs = (refs[1 + j * n_update:1 + (j + 1) * n_update] for j in range(3))
        out = refs[1 + 3 * n_update:]
        grad_refs = out[:n_all]
        d_refs, nm_refs, nv_refs = (out[n_all + j * n_update:n_all + (j + 1) * n_update] for j in range(3))
        at = 0
        for p, sh in enumerate(shapes):
            rows, lanes = _small_rows(sh), sh[-1]
            for l in range(DEPTH):
                g = g_ref[0, at:at + rows, 0:lanes]
                for k in range(1, N_DEV):
                    g = g + g_ref[k, at:at + rows, 0:lanes]
                at += rows
                here = (slice(l, l + 1),) + (slice(None),) * (len(sh) - 1) if len(sh) == 1 else (l,)
                grad_refs[p][here] = g
                if p < n_update:
                    d, m2, v2 = _adamw_math(w_refs[p][here], g, m_refs[p][here], v_refs[p][here])
                    d_refs[p][here] = d
                    nm_refs[p][here] = m2
                    nv_refs[p][here] = v2

    full = [jax.ShapeDtypeStruct((DEPTH,) + tuple(sh), F32) for sh in shapes]
    out = pl.pallas_call(
        body, name="update_small", out_shape=full + full[:n_update] * 3, compiler_params=_cp(),
    )(gathered, *weights, *moments_m, *moments_v)
    return (out[:n_all], out[n_all:n_all + n_update], out[n_all + n_update:n_all + 2 * n_update],
            out[n_all + 2 * n_update:])


WEIGHT_GROUPS = {"in": ("w_in",), "mid": ("w_out", "wq_x", "wkv_x", "wo_x"), "ffn": ("w_gate_up", "w_down")}


def _layer_fwd(x0, mem, weights_of, s, reached, target=None):
    w = dict(weights_of("in", x0))
    h0, u = norm_proj(x0, s["norm_mix_g"], w["w_in"])
    mixed = swa_fwd(u, s["q_norm_g"], s["k_norm_g"], s["sinks"])
    reached("attn", mixed)
    mixed, conv_y = conv_fwd(u, mixed, s["conv_w"], s["conv_b"], s["conv_ln_g"], s["conv_ln_b"])
    w.update(weights_of("mid", conv_y))
    memn = rms_fwd(mem, s["norm_mem_g"])
    kv = mm(memn, w["wkv_x"], trans_b=True, out_dtype=F32, name="mm_kv")
    x1, h1, qx, o, x2, h2 = mid_fwd(mixed, x0, w["w_out"], s["norm_x_g"], w["wq_x"], kv, s["xq_norm_g"], s["xk_norm_g"],
                                    w["wo_x"], s["norm_ffn_g"])
    reached("mid", x2)
    w.update(weights_of("ffn", x2))
    gu, a, *out = ffn_fwd(h2, x2, w["w_gate_up"], w["w_down"], target)
    saved = dict(x0=x0, h0=h0, u=u, conv_y=conv_y, mixed=mixed, x1=x1, h1=h1, qx=qx, memn=memn, kv=kv, o=o, x2=x2, h2=h2,
                 gu=gu, a=a)
    return out, saved, w


def _ordered_after(a, token):
    return a if token is None else a + token[0, 0]


def _layer_bwd(dx3, mem, w, s, sv, token, stage_done):
    gs = {}
    dgu, dx2, dg = ffn_bwd(dx3, sv["gu"], sv["x2"], _ordered_after(s["norm_ffn_g"], token), w["w_down"], w["w_gate_up"])
    gs["norm_ffn_g"] = dg
    gb = {"w_down": mm_tn(sv["a"], dx3, name="mm_dw_down")}
    gb["w_gate_up"] = mm_tn(dgu, sv["h2"], tk=dgu.shape[0], name="mm_dw_gate_up")
    token = stage_done("ffn", gb, gb["w_gate_up"])

    gb = {}
    dq, dx1, dmixed, dkv, dqg, dkg, dg = mid_bwd(dx2, sv["qx"], sv["kv"], s["xq_norm_g"], s["xk_norm_g"], sv["x1"],
                                                 _ordered_after(s["norm_x_g"], token), w["wo_x"], w["wq_x"], w["w_out"])
    gs["xq_norm_g"], gs["xk_norm_g"], gs["norm_x_g"] = dqg, dkg, dg
    gb["wo_x"] = mm_tn(sv["o"], dx2, name="mm_dwo")
    gb["wq_x"] = mm_tn(sv["h1"], dq, name="mm_dwq")
    dmemn = mm(dkv, w["wkv_x"], trans_b=False, out_dtype=F32, name="mm_dmemn")
    gb["wkv_x"] = mm_tn(dkv, sv["memn"], name="mm_dwkv")
    gs["norm_mem_g"] = rms_gain_bwd(dmemn, mem)
    gb["w_out"] = mm_tn(sv["mixed"], dx1, name="mm_dw_out")
    token = stage_done("mid", gb, gb["w_out"])

    du, dqg, dkg, dsinks = swa_bwd(sv["u"], dmixed, _ordered_after(s["q_norm_g"], token), s["k_norm_g"], s["sinks"])
    gs["q_norm_g"], gs["k_norm_g"], gs["sinks"] = dqg[0, :HEAD_DIM], dkg[0, :HEAD_DIM], dsinks[0, :N_Q_HEADS]
    token = stage_done("attn", {}, dqg)
    du, dconv_w, dvec = conv_bwd(sv["u"], sv["conv_y"], dmixed, du, s["conv_w"], _ordered_after(s["conv_ln_g"], token),
                                 s["conv_ln_b"])
    gs["conv_w"] = dconv_w[:CONV_K]
    gs["conv_b"], gs["conv_ln_g"], gs["conv_ln_b"] = dvec[0], dvec[1], dvec[2]
    dw_in = mm_tn(du, sv["h0"], name="mm_dw_in")
    token = stage_done("in", {"w_in": dw_in}, dw_in)
    dx0, dg = in_bwd(du, w["w_in"], sv["x0"], _ordered_after(s["norm_mix_g"], token), dx1)
    gs["norm_mix_g"] = dg
    token = stage_done("mix", {}, dx0)
    return dx0, gs, token


def _local_step(x, mem, target, weights_of, reached, smalls, stage_done):
    saved, weights = [], []
    out = [x]
    for l in range(DEPTH):
        out, sv, w = _layer_fwd(out[0], mem, functools.partial(weights_of, l), smalls[l], functools.partial(reached, l),
                                target if l == DEPTH - 1 else None)
        saved.append(sv)
        weights.append(w)
    dx, loss_part = out
    gss, token = [None] * DEPTH, None
    for l in reversed(range(DEPTH)):
        dx, gss[l], token = _layer_bwd(dx, mem, weights[l], smalls[l], saved[l], token,
                                       functools.partial(stage_done, l))
    return loss_part[0, 0], dx, gss


def kernel(x, mem, norm_mix_g, w_in, q_norm_g, k_norm_g, sinks, conv_w, conv_b, conv_ln_g, conv_ln_b, w_out, norm_x_g, norm_mem_g, wq_x, wkv_x, xq_norm_g, xk_norm_g, wo_x, norm_ffn_g, w_gate_up, w_down, loss_target, m_norm_mix_g, m_w_in, m_q_norm_g, m_k_norm_g, m_sinks, m_conv_w, m_conv_b, m_conv_ln_g, m_conv_ln_b, m_w_out, m_norm_x_g, m_norm_mem_g, m_wq_x, m_wkv_x, m_xq_norm_g, m_xk_norm_g, m_wo_x, m_norm_ffn_g, m_w_gate_up, m_w_down, v_norm_mix_g, v_w_in, v_q_norm_g, v_k_norm_g, v_sinks, v_conv_w, v_conv_b, v_conv_ln_g, v_conv_ln_b, v_w_out, v_norm_x_g, v_norm_mem_g, v_wq_x, v_wkv_x, v_xq_norm_g, v_xk_norm_g, v_wo_x, v_norm_ffn_g, v_w_gate_up, v_w_down):
    P = dict(norm_mix_g=norm_mix_g, w_in=w_in, q_norm_g=q_norm_g, k_norm_g=k_norm_g, sinks=sinks, conv_w=conv_w, conv_b=conv_b,
             conv_ln_g=conv_ln_g, conv_ln_b=conv_ln_b, w_out=w_out, norm_x_g=norm_x_g, norm_mem_g=norm_mem_g, wq_x=wq_x,
             wkv_x=wkv_x, xq_norm_g=xq_norm_g, xk_norm_g=xk_norm_g, wo_x=wo_x, norm_ffn_g=norm_ffn_g, w_gate_up=w_gate_up,
             w_down=w_down)
    M = dict(norm_mix_g=m_norm_mix_g, w_in=m_w_in, q_norm_g=m_q_norm_g, k_norm_g=m_k_norm_g, sinks=m_sinks, conv_w=m_conv_w,
             conv_b=m_conv_b, conv_ln_g=m_conv_ln_g, conv_ln_b=m_conv_ln_b, w_out=m_w_out, norm_x_g=m_norm_x_g,
             norm_mem_g=m_norm_mem_g, wq_x=m_wq_x, wkv_x=m_wkv_x, xq_norm_g=m_xq_norm_g, xk_norm_g=m_xk_norm_g, wo_x=m_wo_x,
             norm_ffn_g=m_norm_ffn_g, w_gate_up=m_w_gate_up, w_down=m_w_down)
    V = dict(norm_mix_g=v_norm_mix_g, w_in=v_w_in, q_norm_g=v_q_norm_g, k_norm_g=v_k_norm_g, sinks=v_sinks, conv_w=v_conv_w,
             conv_b=v_conv_b, conv_ln_g=v_conv_ln_g, conv_ln_b=v_conv_ln_b, w_out=v_w_out, norm_x_g=v_norm_x_g,
             norm_mem_g=v_norm_mem_g, wq_x=v_wq_x, wkv_x=v_wkv_x, xq_norm_g=v_xq_norm_g, xk_norm_g=v_xk_norm_g, wo_x=v_wo_x,
             norm_ffn_g=v_norm_ffn_g, w_gate_up=v_w_gate_up, w_down=v_w_down)
    order = ["norm_mix_g", "w_in", "q_norm_g", "k_norm_g", "sinks", "conv_w", "conv_b", "conv_ln_g", "conv_ln_b", "w_out",
             "norm_x_g", "norm_mem_g", "wq_x", "wkv_x", "xq_norm_g", "xk_norm_g", "wo_x", "norm_ffn_g", "w_gate_up", "w_down"]
    xi, yi, ci = _position()
    dev = 4 * xi + 2 * yi + ci
    x2d, mem2d, tgt2d = x[0], mem[0], loss_target[0]

    def travelling(name, l, transposed):
        a = P[name][l]
        return (a.T if transposed else a).astype(BF16)

    rows_of = {n: rows for n, rows, _ in BIG}
    transposed_of = {n: tr for n, _, tr in BIG}

    def whole(names, gathered):
        return {n: g.reshape(rows_of[n], D_MODEL) for n, g in zip(names, gathered)}

    cw = jnp.pad(conv_w.reshape(DEPTH * CONV_K, CONV_CH // N_DEV), ((0, 2), (0, LANES - CONV_CH // N_DEV)))
    (w_in0, cw_all), token0 = all_gather_many([travelling("w_in", 0, True), cw], name="ag_w_in0_conv_w")
    travel_order = [(0, "mid"), (0, "ffn"), (1, "in"), (1, "mid"), (1, "ffn")]
    travel_groups = []
    for l, group in travel_order:
        shards = [_ordered_after(travelling(n, l, transposed_of[n]), token0.astype(BF16)) for n in WEIGHT_GROUPS[group]]
        lands = [lax.dynamic_update_slice(lax.empty((N_DEV,) + s.shape, BF16), s[None], (dev, 0, 0)) for s in shards]
        travel_groups.append((shards, lands))
    travel_states, travel_token = gather_start(travel_groups, name="ag_weights_start")
    travelling_state = dict(zip(travel_order, travel_states))
    forward_at = {(0, "attn"): [(0, "mid")], (0, "mid"): [(0, "ffn"), (1, "in")], (1, "attn"): [(1, "mid"), (1, "ffn")]}

    def reached(l, stage, marker):
        keys = forward_at.get((l, stage))
        if keys:
            gather_forward([travelling_state[k] for k in keys], marker,
                           name="ag_weights_forward_" + "_".join(f"{g}{ll}" for ll, g in keys))

    def weights_of(l, group, marker):
        if (l, group) == (0, "in"):
            return whole(WEIGHT_GROUPS[group], [w_in0])
        gathered = gather_finish(travelling_state[(l, group)], marker, name=f"ag_weights_finish_{group}{l}")
        return whole(WEIGHT_GROUPS[group], gathered)

    cw_full = cw_all[:, :DEPTH * CONV_K, :CONV_CH // N_DEV].reshape(N_DEV, DEPTH, CONV_K, CONV_CH // N_DEV)
    cw_full = jnp.transpose(cw_full, (1, 2, 0, 3)).reshape(DEPTH, CONV_K, CONV_CH)
    smalls = []
    for l in range(DEPTH):
        sl = {n: P[n][l] if n == "sinks" else P[n][l:l + 1] for n in SMALL}
        sl["conv_w"] = jnp.pad(cw_full[l], ((0, CONV_HALO - CONV_K), (0, 0)))
        smalls.append(sl)
    smalls[0]["norm_mix_g"] = _ordered_after(smalls[0]["norm_mix_g"], travel_token)

    ck_idx = jnp.stack([ci] + [2 * px + py for px, py in _other_chips(xi, yi)]).astype(jnp.int32)
    kc_idx = jnp.stack([2 * xi + yi, ci]).astype(jnp.int32)
    got, flight, reduced = {}, {}, {}

    def as_parts(gb):
        keys = sorted(gb)
        return keys, [gb[k].reshape(N_DEV, rows_of[k[1]] // N_DEV, D_MODEL) for k in keys]

    def lands_like(parts, blocks, dtype):
        return [lax.empty((blocks,) + p.shape[1:], dtype) for p in parts]

    def to_sibling(group, gb):
        keys, parts = as_parts(gb)
        flight[group] = (keys, split_start(parts, lands_like(parts, 4, F32), _sibling_plan, 4,
                                           name=f"rs_sibling_{group}_start"))
        return flight[group][1][4]

    def to_chips(group, marker):
        keys, started = flight[group]
        parts, from_sibling = split_wait(started, marker, _sibling_plan, 4, name=f"rs_sibling_{group}_wait")
        chip_sums = sum_for_chips(parts, from_sibling, ck_idx, name=f"rs_sum_for_chips_{group}")
        started = split_start(chip_sums, lands_like(parts, 3, BF16), _chips_plan, 3, name=f"rs_chips_{group}_start")
        flight[group] = (keys, parts, from_sibling, started)
        return started[4]

    def finish(group, marker):
        keys, parts, from_sibling, started = flight[group]
        _, from_chips = split_wait(started, marker, _chips_plan, 3, name=f"rs_chips_{group}_wait")
        reduced.update(zip(keys, sum_final(parts, from_sibling, from_chips, kc_idx, name=f"rs_sum_final_{group}")))

    def stage_done(l, stage, gb, marker):
        gb = {(l, n): g for n, g in gb.items()}
        if l == 1:
            got.update(gb)
            return to_sibling("l1", got) if stage == "mix" else None
        if stage == "ffn":
            return to_chips("l1", marker) + to_sibling("ffn", gb)
        if stage == "mid":
            return to_chips("ffn", marker) + to_sibling("mid", gb)
        if stage == "attn":
            return to_chips("mid", marker)
        if stage == "in":
            return to_sibling("in", gb)
        to_chips("in", marker)
        for group in ("l1", "ffn", "mid"):
            finish(group, marker)
        return None

    loss_part, grad_x, gss = _local_step(x2d, mem2d, tgt2d, weights_of, reached, smalls, stage_done)
    loss = lax.psum(loss_part, ("x", "y", "c"))

    small_names = SMALL + ("conv_w",)
    small_shapes = [(CONV_K, CONV_CH) if n == "conv_w" else P[n].shape[1:] for n in small_names]
    small_parts = pack_small([[gss[l][n] for l in range(DEPTH)] for n in small_names], small_shapes)
    small_land = lax.dynamic_update_slice(lax.empty((N_DEV,) + small_parts.shape, F32), small_parts[None], (dev, 0, 0))
    small_flight = split_start([small_parts], [small_land], _all_peers_plan, N_DEV - 1, name="ag_small_grads_start")

    grads, delta, new_m, new_v = {}, {}, {}, {}

    def update(n, transposed):
        shape = P[n].shape
        two_d = lambda a: a.reshape(shape[0] * shape[1], shape[2])
        grads[n] = jnp.stack([reduced[(l, n)].T if transposed else reduced[(l, n)] for l in range(DEPTH)])
        d_, m_, v_ = adamw(two_d(P[n]), two_d(grads[n]), two_d(M[n]), two_d(V[n]), name="adamw_" + n)
        delta[n], new_m[n], new_v[n] = d_.reshape(shape), m_.reshape(shape), v_.reshape(shape)

    for n, _, transposed in BIG:
        if n != "w_in":
            update(n, transposed)
    finish("in", delta["w_down"])
    update("w_in", True)
    small_all = split_wait(small_flight, delta["w_in"], _all_peers_plan, N_DEV - 1, name="ag_small_grads_wait")[1][0]
    g_, d_, m_, v_ = update_small(small_all, small_shapes, [P[n] for n in SMALL], [M[n] for n in SMALL],
                                  [V[n] for n in SMALL], len(SMALL))
    for i, n in enumerate(SMALL):
        grads[n], delta[n], new_m[n], new_v[n] = g_[i], d_[i], m_[i], v_[i]
    cols = CONV_CH // N_DEV
    grads["conv_w"] = lax.dynamic_slice_in_dim(g_[-1], dev * cols, cols, axis=2)
    flat = lambda a: a.reshape(DEPTH * CONV_K, cols)
    d_, m_, v_ = adamw(flat(conv_w), flat(grads["conv_w"]), flat(m_conv_w), flat(v_conv_w), name="adamw_conv_w")
    delta["conv_w"], new_m["conv_w"], new_v["conv_w"] = (a.reshape(conv_w.shape) for a in (d_, m_, v_))

    return (loss, grad_x[None], *[grads[n] for n in order], *[delta[n] for n in order],
            *[new_m[n] for n in order], *[new_v[n] for n in order])
```

```python
import functools

import jax
import jax.numpy as jnp
import numpy as np
from jax import lax
from jax.experimental import pallas as pl
from jax.experimental.pallas import tpu as pltpu

F32 = jnp.float32
BF16 = jnp.bfloat16

D_MODEL = 1024
HEAD_DIM = 64
N_Q_HEADS = 8
N_KV_HEADS = 2
GROUP = N_Q_HEADS // N_KV_HEADS
ATTN_WIDTH = N_Q_HEADS * HEAD_DIM
KV_WIDTH = N_KV_HEADS * HEAD_DIM
QKV_WIDTH = ATTN_WIDTH + 2 * KV_WIDTH
CONV_CH = 512
IN_COLS = QKV_WIDTH + 2 * CONV_CH
CONV_K = 31
CONV_HALO = 32
BLOCK = 128
N_X_HEADS = 4
X_HEAD_DIM = 256
D_FF = 2816
EPS = 1e-6
NEG = -1e30
DEPTH = 2
N_DEV = 8

ADAM_LR = 0.001
ADAM_B1 = 0.9
ADAM_B2 = 0.999
ADAM_EPS = 1e-08
ADAM_WD = 0.01
ADAM_STEP = 10

V7X_VMEM_LIMIT = 56 * 1024 * 1024
LANES = 128

MESH = pl.DeviceIdType.MESH


def _cp(**kw):
    return pltpu.CompilerParams(vmem_limit_bytes=V7X_VMEM_LIMIT, **kw)


def _dot(a, b, dims):
    return lax.dot_general(a.astype(BF16), b.astype(BF16), (dims, ((), ())), preferred_element_type=F32)


def _dot_nn(a, b):
    return _dot(a, b, ((1,), (0,)))


def _dot_nt(a, b):
    return _dot(a, b, ((1,), (1,)))


def _dot_tn(a, b):
    return _dot(a, b, ((0,), (0,)))


def _sigmoid(x):
    return jax.nn.sigmoid(x)


def _rms(x):
    r = lax.rsqrt(jnp.mean(x * x, axis=-1, keepdims=True) + EPS)
    return x * r, r


def _rms_bwd(dy, xhat, r, g):
    dxh = dy * g
    return r * (dxh - xhat * jnp.mean(dxh * xhat, axis=-1, keepdims=True))


def rms_fwd(x, g, *, tm=512):
    m, d = x.shape
    tm = min(tm, m)

    def body(x_ref, g_ref, o_ref):
        xh, _ = _rms(x_ref[...])
        o_ref[...] = (xh * g_ref[...]).astype(o_ref.dtype)

    return pl.pallas_call(
        body, name="rms_fwd", grid=(m // tm,),
        in_specs=[pl.BlockSpec((tm, d), lambda i: (i, 0)), pl.BlockSpec((1, d), lambda i: (0, 0))],
        out_specs=pl.BlockSpec((tm, d), lambda i: (i, 0)),
        out_shape=jax.ShapeDtypeStruct((m, d), BF16), compiler_params=_cp(),
    )(x, g.reshape(1, d))


def rms_gain_bwd(dh, x, *, tm=512):
    m, d = x.shape
    tm = min(tm, m)

    def body(dh_ref, x_ref, dg_ref):
        @pl.when(pl.program_id(0) == 0)
        def _():
            dg_ref[...] = jnp.zeros_like(dg_ref)

        dg_ref[...] += jnp.sum(dh_ref[...] * _rms(x_ref[...])[0], axis=0, keepdims=True)

    row = pl.BlockSpec((tm, d), lambda i: (i, 0))
    return pl.pallas_call(
        body, name="rms_gain_bwd", grid=(m // tm,), in_specs=[row, row],
        out_specs=pl.BlockSpec((1, d), lambda i: (0, 0)),
        out_shape=jax.ShapeDtypeStruct((1, d), F32), compiler_params=_cp(),
    )(dh, x)


def _tile(n, cap):
    if n <= cap:
        return n
    best = None
    for t in range(LANES, cap + 1, LANES):
        if n % t == 0:
            best = t
    assert best is not None, (n, cap)
    return best


def mm(a, b, *, trans_b, out_dtype, tm=1024, tn_cap=1536, name):
    m, k = a.shape
    n = b.shape[0] if trans_b else b.shape[1]
    assert (b.shape[1] if trans_b else b.shape[0]) == k
    tm = min(tm, m)
    tn = _tile(n, tn_cap)

    def body(a_ref, b_ref, o_ref):
        acc = _dot_nt(a_ref[...], b_ref[...]) if trans_b else _dot_nn(a_ref[...], b_ref[...])
        o_ref[...] = acc.astype(o_ref.dtype)

    b_spec = pl.BlockSpec((tn, k), lambda i, j: (j, 0)) if trans_b else pl.BlockSpec((k, tn), lambda i, j: (0, j))
    return pl.pallas_call(
        body, name=name, grid=(m // tm, n // tn),
        in_specs=[pl.BlockSpec((tm, k), lambda i, j: (i, 0)), b_spec],
        out_specs=pl.BlockSpec((tm, tn), lambda i, j: (i, j)),
        out_shape=jax.ShapeDtypeStruct((m, n), out_dtype), compiler_params=_cp(),
    )(a, b)


def mm_tn(a, b, *, name, ta_cap=1536, tb_cap=1024, tk=1024):
    m, ka = a.shape
    nb = b.shape[1]
    assert b.shape[0] == m
    tk = min(tk, m)
    ta = _tile(ka, ta_cap)
    tb = _tile(nb, tb_cap)

    def body(a_ref, b_ref, o_ref):
        @pl.when(pl.program_id(2) == 0)
        def _():
            o_ref[...] = jnp.zeros_like(o_ref)

        o_ref[...] += _dot_tn(a_ref[...], b_ref[...])

    return pl.pallas_call(
        body, name=name, grid=(ka // ta, nb // tb, m // tk),
        in_specs=[pl.BlockSpec((tk, ta), lambda i, j, kk: (kk, i)), pl.BlockSpec((tk, tb), lambda i, j, kk: (kk, j))],
        out_specs=pl.BlockSpec((ta, tb), lambda i, j, kk: (i, j)),
        out_shape=jax.ShapeDtypeStruct((ka, nb), F32), compiler_params=_cp(),
    )(a, b)


def _whole(shape):
    return pl.BlockSpec(shape, lambda i: (0,) * len(shape), pipeline_mode=pl.Buffered(1))


def _rows(tm, n):
    return pl.BlockSpec((tm, n), lambda i: (i, 0))


def _vec(n):
    return pl.BlockSpec((1, n), lambda i: (0, 0))


def _chunks(n, cap=1408):
    size = _tile(n, cap)
    return [(s, size) for s in range(0, n, size)]


def _zero_at_first_step(*refs):
    @pl.when(pl.program_id(0) == 0)
    def _():
        for r in refs:
            r[...] = jnp.zeros_like(r)


def norm_proj(x, g, wt, *, tm=512):
    m, d = x.shape
    n = wt.shape[0]

    def body(x_ref, g_ref, wt_ref, h_ref, u_ref):
        h = (_rms(x_ref[...])[0] * g_ref[...]).astype(BF16)
        h_ref[...] = h
        for s, sz in _chunks(n):
            u_ref[:, s:s + sz] = _dot_nt(h, wt_ref[s:s + sz, :])

    return pl.pallas_call(
        body, name="norm_proj", grid=(m // tm,),
        in_specs=[_rows(tm, d), _vec(d), _whole((n, d))],
        out_specs=[_rows(tm, d), _rows(tm, n)],
        out_shape=[jax.ShapeDtypeStruct((m, d), BF16), jax.ShapeDtypeStruct((m, n), F32)],
        compiler_params=_cp(),
    )(x, g.reshape(1, d), wt)


def _xattn_heads(q_ref, kv_ref, qg_v, kg_v, d):
    normed = []
    for h in range(N_X_HEADS):
        cols = slice(h * X_HEAD_DIM, (h + 1) * X_HEAD_DIM)
        qh, rq = _rms(q_ref[:, cols])
        normed.append((qh, rq, (qh * qg_v).astype(BF16), (_rms(kv_ref[:, cols])[0] * kg_v).astype(BF16),
                       kv_ref[:, d + h * X_HEAD_DIM:d + (h + 1) * X_HEAD_DIM].astype(BF16)))
    scores = [_dot_nt(qn, kn) * (X_HEAD_DIM ** -0.5) for _, _, qn, kn, _ in normed]
    out = []
    for (qh, rq, qn, kn, v), s in zip(normed, scores):
        e = jnp.exp(s - jnp.max(s, axis=-1, keepdims=True))
        out.append((qh, rq, qn, kn, v, e / jnp.sum(e, axis=-1, keepdims=True)))
    return out


def mid_fwd(mixed, x0, w_out, g_x, wq, kv, xqg, xkg, wo, g_f, *, tm=512):
    m, d = x0.shape
    n_mem = kv.shape[0]

    def body(mixed_ref, x0_ref, w_out_ref, g_x_ref, wq_ref, kv_ref, xqg_ref, xkg_ref, wo_ref, g_f_ref,
             x1_ref, h1_ref, qx_ref, o_ref, x2_ref, h2_ref):
        x1 = x0_ref[...] + _dot_nn(mixed_ref[...], w_out_ref[...])
        x1_ref[...] = x1
        h1 = (_rms(x1)[0] * g_x_ref[...]).astype(BF16)
        h1_ref[...] = h1
        qx_ref[...] = _dot_nn(h1, wq_ref[...])
        for h, (_, _, _, _, v, p) in enumerate(_xattn_heads(qx_ref, kv_ref, xqg_ref[...], xkg_ref[...], d)):
            o_ref[:, h * X_HEAD_DIM:(h + 1) * X_HEAD_DIM] = _dot_nn(p, v).astype(o_ref.dtype)
        x2 = x1 + _dot_nn(o_ref[...], wo_ref[...])
        x2_ref[...] = x2
        h2_ref[...] = (_rms(x2)[0] * g_f_ref[...]).astype(BF16)

    sq = _whole((d, d))
    f32_rows, bf_rows = jax.ShapeDtypeStruct((m, d), F32), jax.ShapeDtypeStruct((m, d), BF16)
    return pl.pallas_call(
        body, name="mid_fwd", grid=(m // tm,),
        in_specs=[_rows(tm, d), _rows(tm, d), sq, _vec(d), sq, _whole((n_mem, 2 * d)), _vec(X_HEAD_DIM), _vec(X_HEAD_DIM),
                  sq, _vec(d)],
        out_specs=[_rows(tm, d)] * 6,
        out_shape=[f32_rows, bf_rows, f32_rows, bf_rows, f32_rows, bf_rows],
        compiler_params=_cp(),
    )(mixed, x0, w_out, g_x.reshape(1, d), wq, kv, xqg.reshape(1, X_HEAD_DIM), xkg.reshape(1, X_HEAD_DIM), wo,
      g_f.reshape(1, d))


def ffn_fwd(h2, x2, wt_gu, w_down, target=None, *, tm=256):
    m, d = x2.shape
    f = w_down.shape[0]
    with_loss = target is not None

    def body(*refs):
        if with_loss:
            h2_ref, x2_ref, wt_gu_ref, w_down_ref, t_ref, gu_ref, a_ref, dy_ref, l_ref = refs
        else:
            h2_ref, x2_ref, wt_gu_ref, w_down_ref, gu_ref, a_ref, x3_ref = refs
        h = h2_ref[...]
        for s, sz in _chunks(2 * f):
            gu_ref[:, s:s + sz] = _dot_nt(h, wt_gu_ref[s:s + sz, :])
        for s, sz in _chunks(f):
            g = gu_ref[:, s:s + sz]
            a_ref[:, s:s + sz] = (g * _sigmoid(g) * gu_ref[:, f + s:f + s + sz]).astype(a_ref.dtype)
        x3 = x2_ref[...] + _dot_nn(a_ref[...], w_down_ref[...])
        if not with_loss:
            x3_ref[...] = x3
            return
        err = x3 - t_ref[...]
        dy_ref[...] = err * (1.0 / d)
        _zero_at_first_step(l_ref)
        part = jnp.sum(jnp.sum(err * err, axis=-1, keepdims=True), axis=0, keepdims=True)
        l_ref[...] += jnp.broadcast_to(part * (0.5 / d), l_ref.shape)

    last = [_rows(tm, d), pl.BlockSpec((1, LANES), lambda i: (0, 0))] if with_loss else [_rows(tm, d)]
    last_shape = [jax.ShapeDtypeStruct((m, d), F32)] + ([jax.ShapeDtypeStruct((1, LANES), F32)] if with_loss else [])
    return pl.pallas_call(
        body, name="ffn_fwd_loss" if with_loss else "ffn_fwd", grid=(m // tm,),
        in_specs=[_rows(tm, d), _rows(tm, d), _whole((2 * f, d)), _whole((f, d))] + ([_rows(tm, d)] if with_loss else []),
        out_specs=[_rows(tm, 2 * f), _rows(tm, f)] + last,
        out_shape=[jax.ShapeDtypeStruct((m, 2 * f), F32), jax.ShapeDtypeStruct((m, f), BF16)] + last_shape,
        compiler_params=_cp(),
    )(*([h2, x2, wt_gu, w_down] + ([target] if with_loss else [])))


def ffn_bwd(dx3, gu, x2, g_f, w_down, wt_gu, *, tm=256):
    m, d = x2.shape
    f = w_down.shape[0]

    def body(dx3_ref, gu_ref, x2_ref, g_ref, w_down_ref, wt_gu_ref, dgu_ref, dx2_ref, dg_ref):
        _zero_at_first_step(dg_ref)
        dx3 = dx3_ref[...]
        dx3_b = dx3.astype(BF16)
        for s, sz in _chunks(f):
            da = _dot_nt(dx3_b, w_down_ref[s:s + sz, :])
            g = gu_ref[:, s:s + sz]
            u = gu_ref[:, f + s:f + s + sz]
            sg = _sigmoid(g)
            dgu_ref[:, s:s + sz] = (da * u * (sg * (1.0 + g * (1.0 - sg)))).astype(dgu_ref.dtype)
            dgu_ref[:, f + s:f + s + sz] = (da * (g * sg)).astype(dgu_ref.dtype)
        dh2 = _dot_nn(dgu_ref[...], wt_gu_ref[...])
        xh, r = _rms(x2_ref[...])
        dg_ref[...] += jnp.sum(dh2 * xh, axis=0, keepdims=True)
        dx2_ref[...] = dx3 + _rms_bwd(dh2, xh, r, g_ref[...])

    return pl.pallas_call(
        body, name="ffn_bwd", grid=(m // tm,),
        in_specs=[_rows(tm, d), _rows(tm, 2 * f), _rows(tm, d), _vec(d), _whole((f, d)), _whole((2 * f, d))],
        out_specs=[_rows(tm, 2 * f), _rows(tm, d), _vec(d)],
        out_shape=[jax.ShapeDtypeStruct((m, 2 * f), BF16), jax.ShapeDtypeStruct((m, d), F32),
                   jax.ShapeDtypeStruct((1, d), F32)],
        compiler_params=_cp(),
    )(dx3, gu, x2, g_f.reshape(1, d), w_down, wt_gu)


def mid_bwd(dx2, qx, kv, xqg, xkg, x1, g_x, wo, wq, w_out, *, tm=512):
    m, d = x1.shape
    n_mem = kv.shape[0]
    nt = m // tm

    def body(dx2_ref, qx_ref, kv_ref, xqg_ref, xkg_ref, x1_ref, g_x_ref, wo_ref, wq_ref, w_out_ref,
             dq_ref, dx1_ref, dmixed_ref, dkv_ref, dqg_ref, dkg_ref, dg_ref):
        i = pl.program_id(0)
        _zero_at_first_step(dkv_ref, dqg_ref, dkg_ref, dg_ref)
        qg_v, kg_v = xqg_ref[...], xkg_ref[...]
        dx2 = dx2_ref[...]
        do = _dot_nt(dx2, wo_ref[...])
        dqg_acc = jnp.zeros((1, X_HEAD_DIM), F32)
        heads = _xattn_heads(qx_ref, kv_ref, qg_v, kg_v, d)
        head_cols = [slice(h * X_HEAD_DIM, (h + 1) * X_HEAD_DIM) for h in range(N_X_HEADS)]
        do_h = [do[:, cols].astype(BF16) for cols in head_cols]
        dps = [_dot_nt(do_h[h], heads[h][4]) for h in range(N_X_HEADS)]
        dss = []
        for (_, _, _, _, _, p), dp in zip(heads, dps):
            dss.append((p.astype(BF16), (p * (dp - jnp.sum(p * dp, axis=-1, keepdims=True))).astype(BF16)))
        for h, ((qh, rq, qn, kn, _, _), (p, ds)) in enumerate(zip(heads, dss)):
            cols = head_cols[h]
            vcols = slice(d + h * X_HEAD_DIM, d + (h + 1) * X_HEAD_DIM)
            dkv_ref[:, vcols] += _dot_tn(p, do_h[h])
            dqn = _dot_nn(ds, kn) * (X_HEAD_DIM ** -0.5)
            dkv_ref[:, cols] += _dot_tn(ds, qn) * (X_HEAD_DIM ** -0.5)
            dqg_acc = dqg_acc + jnp.sum(dqn * qh, axis=0, keepdims=True)
            dq_ref[:, cols] = _rms_bwd(dqn, qh, rq, qg_v).astype(dq_ref.dtype)
        dqg_ref[...] += dqg_acc
        dh1 = _dot_nt(dq_ref[...], wq_ref[...])
        xh, r = _rms(x1_ref[...])
        dg_ref[...] += jnp.sum(dh1 * xh, axis=0, keepdims=True)
        dx1 = dx2 + _rms_bwd(dh1, xh, r, g_x_ref[...])
        dx1_ref[...] = dx1
        dmixed_ref[...] = _dot_nt(dx1, w_out_ref[...])

        @pl.when(i == nt - 1)
        def _():
            dkg_acc = jnp.zeros((1, X_HEAD_DIM), F32)
            for h in range(N_X_HEADS):
                cols = slice(h * X_HEAD_DIM, (h + 1) * X_HEAD_DIM)
                kh, rk = _rms(kv_ref[:, cols])
                dkn = dkv_ref[:, cols]
                dkg_acc = dkg_acc + jnp.sum(dkn * kh, axis=0, keepdims=True)
                dkv_ref[:, cols] = _rms_bwd(dkn, kh, rk, kg_v)
            dkg_ref[...] = dkg_acc

    sq = _whole((d, d))
    full = pl.BlockSpec((n_mem, 2 * d), lambda i: (0, 0))
    return pl.pallas_call(
        body, name="mid_bwd", grid=(nt,),
        in_specs=[_rows(tm, d), _rows(tm, d), _whole((n_mem, 2 * d)), _vec(X_HEAD_DIM), _vec(X_HEAD_DIM), _rows(tm, d),
                  _vec(d), sq, sq, sq],
        out_specs=[_rows(tm, d), _rows(tm, d), _rows(tm, d), full, _vec(X_HEAD_DIM), _vec(X_HEAD_DIM), _vec(d)],
        out_shape=[jax.ShapeDtypeStruct((m, d), BF16), jax.ShapeDtypeStruct((m, d), F32), jax.ShapeDtypeStruct((m, d), F32),
                   jax.ShapeDtypeStruct((n_mem, 2 * d), F32), jax.ShapeDtypeStruct((1, X_HEAD_DIM), F32),
                   jax.ShapeDtypeStruct((1, X_HEAD_DIM), F32), jax.ShapeDtypeStruct((1, d), F32)],
        compiler_params=_cp(),
    )(dx2, qx, kv, xqg.reshape(1, X_HEAD_DIM), xkg.reshape(1, X_HEAD_DIM), x1, g_x.reshape(1, d), wo, wq, w_out)


def in_bwd(du, wt_in, x0, g_mix, dx1, *, tm=512):
    m, d = x0.shape
    n = wt_in.shape[0]

    def body(du_ref, wt_ref, x0_ref, g_ref, dx1_ref, dx0_ref, dg_ref):
        _zero_at_first_step(dg_ref)
        dh0 = _dot_nn(du_ref[...], wt_ref[...])
        xh, r = _rms(x0_ref[...])
        dg_ref[...] += jnp.sum(dh0 * xh, axis=0, keepdims=True)
        dx0_ref[...] = dx1_ref[...] + _rms_bwd(dh0, xh, r, g_ref[...])

    return pl.pallas_call(
        body, name="in_bwd", grid=(m // tm,),
        in_specs=[_rows(tm, n), _whole((n, d)), _rows(tm, d), _vec(d), _rows(tm, d)],
        out_specs=[_rows(tm, d), _vec(d)],
        out_shape=[jax.ShapeDtypeStruct((m, d), F32), jax.ShapeDtypeStruct((1, d), F32)],
        compiler_params=_cp(),
    )(du, wt_in, x0, g_mix.reshape(1, d), dx1)


SWA_TILE = 512
SWA_SUB = SWA_TILE // BLOCK
SWA_KEYS = SWA_TILE + BLOCK
PAIR = 2 * HEAD_DIM
KCOL = ATTN_WIDTH
VCOL = ATTN_WIDTH + KV_WIDTH


def _swa_constants():
    r = np.arange(2 * BLOCK)[:, None]
    j = np.arange(4 * BLOCK)[None, :]
    dist = (r % BLOCK) + BLOCK - (j % (2 * BLOCK))
    valid = (dist >= 0) & (dist < BLOCK)
    first_valid = valid & ((j % (2 * BLOCK)) >= BLOCK)
    bias, bias_first = [], []
    for kv in range(N_KV_HEADS):
        head = kv * GROUP + 2 * (r // BLOCK) + j // (2 * BLOCK)
        b = -(2.0 ** -(head + 1.0)) * dist
        bias.append(np.where(valid, b, NEG))
        bias_first.append(np.where(first_valid, b, NEG))
    lane = np.arange(LANES)
    seg = (lane[:, None] // HEAD_DIM == lane[None, :] // HEAD_DIM) / HEAD_DIM
    row = np.arange(4 * BLOCK)[:, None]
    ones = (row // (2 * BLOCK)) == (lane[None, :] // HEAD_DIM)
    return (jnp.asarray(np.stack(bias), F32), jnp.asarray(np.stack(bias_first), F32), jnp.asarray(seg, BF16),
            jnp.asarray(ones, BF16))


def _segmean(x, seg_ref):
    hi = x.astype(BF16)
    lo = (x - hi.astype(F32)).astype(BF16)
    return _dot_nn(hi, seg_ref[...]) + _dot_nn(lo, seg_ref[...])


def _two_heads(x, kv):
    lane = lax.broadcasted_iota(jnp.int32, (1, LANES), 1)
    mine = (lane < HEAD_DIM) if kv == 0 else (lane >= HEAD_DIM)
    base = jnp.where(mine, x, 0.0)
    other = pltpu.roll(base, HEAD_DIM, 1)
    return jnp.concatenate([base, other] if kv == 0 else [other, base], axis=0)


def _from_two_heads(y, kv):
    rows = y.shape[0] // 2
    lane = lax.broadcasted_iota(jnp.int32, (1, LANES), 1)
    top, bot = y[:rows], y[rows:]
    if kv == 0:
        return jnp.where(lane < HEAD_DIM, top + pltpu.roll(bot, HEAD_DIM, 1), 0.0)
    return jnp.where(lane >= HEAD_DIM, pltpu.roll(top, HEAD_DIM, 1) + bot, 0.0)


def _pair_rows(ref, rows, kv):
    c = kv * 2 * PAIR
    return jnp.concatenate([ref[rows, c:c + PAIR], ref[rows, c + PAIR:c + 2 * PAIR]], axis=0)


def _head_cols(fn, kv):
    return [jnp.concatenate([fn(kv * GROUP + half), fn(kv * GROUP + 2 + half)], axis=0) for half in range(2)]


def _swa_prologue(cur_ref, prev_ref, qg_ref, kg_ref, seg_ref, qg_s, kn_s, v_s):
    qg_s[...] = (cur_ref[:, 0:ATTN_WIDTH] * qg_ref[...]).astype(BF16)
    k = jnp.concatenate([prev_ref[:, KCOL:KCOL + KV_WIDTH], cur_ref[:, KCOL:KCOL + KV_WIDTH]], axis=0)
    kn_s[...] = k * lax.rsqrt(_segmean(k * k, seg_ref) + EPS) * kg_ref[...]
    v_s[0:BLOCK, :] = prev_ref[:, VCOL:VCOL + KV_WIDTH]
    v_s[BLOCK:SWA_KEYS, :] = cur_ref[:, VCOL:VCOL + KV_WIDTH]


def _swa_products(qg_s, kn_s, rows, keys, kv):
    q2 = _pair_rows(qg_s, rows, kv)
    k2 = _two_heads(kn_s[keys, :], kv)
    return q2, k2, _dot_nt(q2, k2)


def _swa_scores(cur_ref, sinks_ref, qg_s, kn_s, bias, rows, keys, kv):
    q2, k2, t = _swa_products(qg_s, kn_s, rows, keys, kv)
    return q2, k2, t, _swa_softmax(cur_ref, sinks_ref, t, bias, rows, kv)


def _swa_softmax(cur_ref, sinks_ref, t, bias, rows, kv):
    def rq(h):
        x = cur_ref[rows, h * HEAD_DIM:(h + 1) * HEAD_DIM]
        return lax.rsqrt(jnp.mean(x * x, axis=-1, keepdims=True) + EPS)

    scale = _head_cols(lambda h: rq(h) * (HEAD_DIM ** -0.5), kv)
    sink = _head_cols(lambda h: jnp.full((BLOCK, 1), sinks_ref[h], F32), kv)
    halves = []
    for half in range(2):
        cols = slice(half * 2 * BLOCK, (half + 1) * 2 * BLOCK)
        s = t[:, cols] * scale[half] + bias[:, cols]
        mx = jnp.maximum(jnp.max(s, axis=-1, keepdims=True), sink[half])
        halves.append((scale[half], jnp.exp(s - mx), jnp.exp(sink[half] - mx)))
    return halves


def swa_fwd(u, qg, kg, sinks):
    t_rows = u.shape[0]
    nt = t_rows // SWA_TILE
    bias_c, bias_first_c, seg_c, ones_c = _swa_constants()

    def body(sinks_ref, cur_ref, prev_ref, qg_ref, kg_ref, seg_ref, bias_ref, biasf_ref, ones_ref, o_ref, qg_s, kn_s, v_s):
        i = pl.program_id(0)
        _swa_prologue(cur_ref, prev_ref, qg_ref, kg_ref, seg_ref, qg_s, kn_s, v_s)
        lane = lax.broadcasted_iota(jnp.int32, (1, LANES), 1)
        work = [(b, kv, slice(b * BLOCK, (b + 1) * BLOCK), slice(b * BLOCK, (b + 2) * BLOCK))
                for b in range(SWA_SUB) for kv in range(N_KV_HEADS)]
        products = [_swa_products(qg_s, kn_s, rows, keys, kv)[2] for _, kv, rows, keys in work]
        scored = []
        for (b, kv, rows, _), t in zip(work, products):
            bias = jnp.where(i == 0, biasf_ref[kv], bias_ref[kv]) if b == 0 else bias_ref[kv]
            halves = _swa_softmax(cur_ref, sinks_ref, t, bias, rows, kv)
            scored.append((jnp.concatenate([halves[0][1], halves[1][1]], axis=1).astype(BF16), halves[0][2], halves[1][2]))
        for (b, kv, rows, keys), (e, es0, es1) in zip(work, scored):
            v2 = jnp.concatenate([_two_heads(v_s[keys, :], kv).astype(BF16), ones_ref[...]], axis=1)
            ox = _dot_nn(e, v2)
            den = ox[:, LANES:] + jnp.where(lane < HEAD_DIM, es0, es1)
            out = (ox[:, :LANES] / den).astype(o_ref.dtype)
            c = kv * 2 * PAIR
            o_ref[rows, c:c + PAIR] = out[:BLOCK]
            o_ref[rows, c + PAIR:c + 2 * PAIR] = out[BLOCK:]

    const3 = pl.BlockSpec((N_KV_HEADS, 2 * BLOCK, 4 * BLOCK), lambda i: (0, 0, 0))
    return pl.pallas_call(
        body, name="swa_fwd", grid=(nt,),
        in_specs=[
            pl.BlockSpec(memory_space=pltpu.SMEM),
            pl.BlockSpec((SWA_TILE, QKV_WIDTH), lambda i: (i, 0)),
            pl.BlockSpec((BLOCK, QKV_WIDTH), lambda i: (jnp.maximum(i * SWA_SUB - 1, 0), 0)),
            pl.BlockSpec((1, ATTN_WIDTH), lambda i: (0, 0)), pl.BlockSpec((1, KV_WIDTH), lambda i: (0, 0)),
            pl.BlockSpec((LANES, LANES), lambda i: (0, 0)), const3, const3,
            pl.BlockSpec((4 * BLOCK, LANES), lambda i: (0, 0)),
        ],
        out_specs=pl.BlockSpec((SWA_TILE, ATTN_WIDTH), lambda i: (i, 0)),
        out_shape=jax.ShapeDtypeStruct((t_rows, 2 * ATTN_WIDTH), BF16),
        scratch_shapes=[pltpu.VMEM((SWA_TILE, ATTN_WIDTH), BF16), pltpu.VMEM((SWA_KEYS, KV_WIDTH), F32),
                        pltpu.VMEM((SWA_KEYS, KV_WIDTH), F32)],
        compiler_params=_cp(),
    )(sinks, u, u, jnp.tile(qg, N_Q_HEADS).reshape(1, ATTN_WIDTH), jnp.tile(kg, N_KV_HEADS).reshape(1, KV_WIDTH),
      seg_c, bias_c, bias_first_c, ones_c)


def swa_bwd(u, dmixed, qg, kg, sinks):
    t_rows = u.shape[0]
    nt = t_rows // SWA_TILE
    bias_c, bias_first_c, seg_c, _ = _swa_constants()

    def body(sinks_ref, cur_ref, prev_ref, do_ref, qg_ref, kg_ref, seg_ref, bias_ref, biasf_ref,
             du_ref, dqg_ref, dkg_ref, dsk_ref, qg_s, kn_s, v_s, acck_s, accv_s, carryk_s, carryv_s):
        step = pl.program_id(0)
        i = nt - 1 - step

        @pl.when(step == 0)
        def _():
            for r in (carryk_s, carryv_s, dqg_ref, dkg_ref, dsk_ref):
                r[...] = jnp.zeros_like(r)

        _swa_prologue(cur_ref, prev_ref, qg_ref, kg_ref, seg_ref, qg_s, kn_s, v_s)
        for acc, carry in ((acck_s, carryk_s), (accv_s, carryv_s)):
            acc[0:SWA_TILE, :] = jnp.zeros((SWA_TILE, KV_WIDTH), F32)
            acc[SWA_TILE:SWA_KEYS, :] = carry[...]

        lane = lax.broadcasted_iota(jnp.int32, (1, LANES), 1)
        g_pair = qg_ref[:, 0:PAIR]
        dqg_acc = jnp.zeros((1, PAIR), F32)
        dsk_acc = jnp.zeros((1, LANES), F32)
        work = [(b, kv, slice(b * BLOCK, (b + 1) * BLOCK), slice(b * BLOCK, (b + 2) * BLOCK))
                for b in range(SWA_SUB) for kv in range(N_KV_HEADS)]
        products = []
        for _, kv, rows, keys in work:
            q2, k2, t = _swa_products(qg_s, kn_s, rows, keys, kv)
            do2 = _pair_rows(do_ref, rows, kv).astype(BF16)
            products.append((q2, k2, t, do2, _dot_nt(do2, _two_heads(v_s[keys, :], kv))))
        exps = []
        for (b, kv, rows, _), (_, _, t, _, _) in zip(work, products):
            bias = jnp.where(i == 0, biasf_ref[kv], bias_ref[kv]) if b == 0 else bias_ref[kv]
            exps.append(_swa_softmax(cur_ref, sinks_ref, t, bias, rows, kv))
        softmaxed = []
        for (b, kv, rows, _), (_, _, t, _, dp), halves in zip(work, products, exps):
            p_parts, dt_parts, coef = [], [], []
            for half, (scale, e, es) in enumerate(halves):
                cols = slice(half * 2 * BLOCK, (half + 1) * 2 * BLOCK)
                rden = 1.0 / (jnp.sum(e, axis=-1, keepdims=True) + es)
                p = e * rden
                dp_h = dp[:, cols]
                delta = jnp.sum(p * dp_h, axis=-1, keepdims=True)
                ds = p * (dp_h - delta)
                dsink = -(es * rden) * delta
                for pair in range(2):
                    part = jnp.sum(dsink[pair * BLOCK:(pair + 1) * BLOCK], axis=0, keepdims=True)
                    dsk_acc = dsk_acc + jnp.where(lane == kv * GROUP + 2 * pair + half, part, 0.0)
                dscale = jnp.sum(ds * t[:, cols], axis=-1, keepdims=True)
                coef.append(-dscale * scale * scale * scale)
                p_parts.append(p.astype(BF16))
                dt_parts.append((ds * scale).astype(BF16))
            softmaxed.append((jnp.concatenate(p_parts, axis=1), jnp.concatenate(dt_parts, axis=1),
                              jnp.where(lane < HEAD_DIM, coef[0], coef[1])))
        for (_, kv, rows, keys), (q2, k2, _, do2, _), (p2, dt, coef) in zip(work, products, softmaxed):
            dqg2 = _dot_nn(dt, k2)
            q_raw = _pair_rows(cur_ref, rows, kv)
            dq = dqg2 * g_pair + coef * q_raw
            dqg_acc = dqg_acc + jnp.sum(dqg2 * q_raw, axis=0, keepdims=True)
            c = kv * 2 * PAIR
            du_ref[rows, c:c + PAIR] = dq[:BLOCK].astype(du_ref.dtype)
            du_ref[rows, c + PAIR:c + 2 * PAIR] = dq[BLOCK:].astype(du_ref.dtype)
        to_keys = [(_from_two_heads(_dot_tn(dt, q2), kv), _from_two_heads(_dot_tn(p2, do2), kv))
                   for (_, kv, _, _), (q2, _, _, do2, _), (p2, dt, _) in zip(work, products, softmaxed)]
        for (_, _, _, keys), (dk, dv) in zip(work, to_keys):
            acck_s[keys, :] += dk
            accv_s[keys, :] += dv
        dqg_ref[...] += dqg_acc + pltpu.roll(dqg_acc, HEAD_DIM, 1)
        dsk_ref[...] += dsk_acc

        own = slice(BLOCK, SWA_KEYS)
        k = cur_ref[:, KCOL:KCOL + KV_WIDTH]
        rk = lax.rsqrt(_segmean(k * k, seg_ref) + EPS)
        kh = k * rk
        dkn = acck_s[own, :]
        dkh = dkn * kg_ref[...]
        du_ref[:, KCOL:KCOL + KV_WIDTH] = (rk * (dkh - kh * _segmean(dkh * kh, seg_ref))).astype(du_ref.dtype)
        du_ref[:, VCOL:VCOL + KV_WIDTH] = accv_s[own, :].astype(du_ref.dtype)
        dkg_part = jnp.sum(dkn * kh, axis=0, keepdims=True)
        dkg_ref[...] += dkg_part + pltpu.roll(dkg_part, HEAD_DIM, 1)
        carryk_s[...] = acck_s[0:BLOCK, :]
        carryv_s[...] = accv_s[0:BLOCK, :]

    const3 = pl.BlockSpec((N_KV_HEADS, 2 * BLOCK, 4 * BLOCK), lambda s: (0, 0, 0))
    vec = pl.BlockSpec((1, LANES), lambda s: (0, 0))
    return pl.pallas_call(
        body, name="swa_bwd", grid=(nt,),
        in_specs=[
            pl.BlockSpec(memory_space=pltpu.SMEM),
            pl.BlockSpec((SWA_TILE, QKV_WIDTH), lambda s: (nt - 1 - s, 0)),
            pl.BlockSpec((BLOCK, QKV_WIDTH), lambda s: (jnp.maximum((nt - 1 - s) * SWA_SUB - 1, 0), 0)),
            pl.BlockSpec((SWA_TILE, ATTN_WIDTH), lambda s: (nt - 1 - s, 0)),
            pl.BlockSpec((1, ATTN_WIDTH), lambda s: (0, 0)), vec,
            pl.BlockSpec((LANES, LANES), lambda s: (0, 0)), const3, const3,
        ],
        out_specs=[pl.BlockSpec((SWA_TILE, QKV_WIDTH), lambda s: (nt - 1 - s, 0)), vec, vec, vec],
        out_shape=[jax.ShapeDtypeStruct((t_rows, IN_COLS), BF16)] + [jax.ShapeDtypeStruct((1, LANES), F32)] * 3,
        scratch_shapes=[pltpu.VMEM((SWA_TILE, ATTN_WIDTH), BF16)] + [pltpu.VMEM((SWA_KEYS, KV_WIDTH), F32)] * 4
        + [pltpu.VMEM((BLOCK, KV_WIDTH), F32)] * 2,
        compiler_params=_cp(),
    )(sinks, u, u, dmixed, jnp.tile(qg, N_Q_HEADS).reshape(1, ATTN_WIDTH), jnp.tile(kg, N_KV_HEADS).reshape(1, KV_WIDTH),
      seg_c, bias_c, bias_first_c)


CONV_TILE = 512
CONV_CHUNK = 64
VAL0 = QKV_WIDTH
GATE0 = QKV_WIDTH + CONV_CH


def _glu(ref):
    return ref[:, VAL0:GATE0] * _sigmoid(ref[:, GATE0:GATE0 + CONV_CH])


SUBLANES = 8
CONV_BUF = CONV_HALO + CONV_TILE + SUBLANES
CONV_EXT = CONV_HALO + CONV_TILE


def _fill_shifted(sh_ref):
    for r in range(1, SUBLANES):
        sh_ref[r, 0:CONV_EXT, :] = sh_ref[0, pl.ds(r, CONV_EXT), :]


def _shifted(sh_ref, start, offset, n):
    return sh_ref[offset % SUBLANES, pl.ds(start + offset - offset % SUBLANES, n), :]


def _layernorm_stats(y):
    mu = jnp.mean(y, axis=-1, keepdims=True)
    yc = y - mu
    rstd = lax.rsqrt(jnp.mean(yc * yc, axis=-1, keepdims=True) + EPS)
    return yc * rstd, rstd


def conv_fwd(u, mixed, conv_w, conv_b, ln_g, ln_b):
    t = u.shape[0]
    nt = t // CONV_TILE
    per = CONV_TILE // CONV_HALO

    def body(cur_ref, prev_ref, mixed_ref, w_ref, b_ref, g_ref, b2_ref, o_ref, y_ref, gl_ref):
        del mixed_ref
        i = pl.program_id(0)
        gl_ref[0, 0:CONV_HALO, :] = jnp.where(i > 0, _glu(prev_ref), 0.0)
        gl_ref[0, CONV_HALO:CONV_EXT, :] = _glu(cur_ref)
        gl_ref[0, CONV_EXT:CONV_BUF, :] = jnp.zeros((SUBLANES, CONV_CH), F32)
        _fill_shifted(gl_ref)
        for c0 in range(0, CONV_TILE, CONV_CHUNK):
            acc = jnp.broadcast_to(b_ref[...], (CONV_CHUNK, CONV_CH))
            for k in range(CONV_K):
                acc = acc + w_ref[k:k + 1, :] * _shifted(gl_ref, c0, 2 + k, CONV_CHUNK)
            y_ref[c0:c0 + CONV_CHUNK, :] = acc
        yh, _ = _layernorm_stats(y_ref[...])
        yln = yh * g_ref[...] + b2_ref[...]
        o_ref[...] = (yln * _sigmoid(yln)).astype(o_ref.dtype)

    vec = pl.BlockSpec((1, CONV_CH), lambda i: (0, 0))
    return pl.pallas_call(
        body, name="conv_fwd", grid=(nt,),
        in_specs=[
            pl.BlockSpec((CONV_TILE, IN_COLS), lambda i: (i, 0)),
            pl.BlockSpec((CONV_HALO, IN_COLS), lambda i: (jnp.maximum(i * per - 1, 0), 0)),
            pl.BlockSpec(memory_space=pl.ANY),
            pl.BlockSpec((CONV_HALO, CONV_CH), lambda i: (0, 0)),
            vec, vec, vec,
        ],
        out_specs=[pl.BlockSpec((CONV_TILE, CONV_CH), lambda i: (i, 1)), pl.BlockSpec((CONV_TILE, CONV_CH), lambda i: (i, 0))],
        out_shape=[jax.ShapeDtypeStruct(mixed.shape, mixed.dtype), jax.ShapeDtypeStruct((t, CONV_CH), F32)],
        scratch_shapes=[pltpu.VMEM((SUBLANES, CONV_BUF, CONV_CH), F32)],
        input_output_aliases={2: 0}, compiler_params=_cp(),
    )(u, u, mixed, conv_w, conv_b.reshape(1, CONV_CH), ln_g.reshape(1, CONV_CH), ln_b.reshape(1, CONV_CH))


def conv_bwd(u, y, dmixed, du, conv_w, ln_g, ln_b):
    t = u.shape[0]
    nt = t // CONV_TILE
    per = CONV_TILE // CONV_HALO

    def body(cur_ref, prev_ref, y_ref, yn_ref, do_ref, don_ref, du_in_ref, w_ref, g_ref, b2_ref,
             du_ref, dw_ref, dvec_ref, gl_ref, dy_ref):
        i = pl.program_id(0)
        last = i == nt - 1
        _zero_at_first_step(dw_ref, dvec_ref)

        gl_ref[0, 0:CONV_HALO, :] = jnp.where(i > 0, _glu(prev_ref), 0.0)
        gl_ref[0, CONV_HALO:CONV_EXT, :] = _glu(cur_ref)
        gl_ref[0, CONV_EXT:CONV_BUF, :] = jnp.zeros((SUBLANES, CONV_CH), F32)
        _fill_shifted(gl_ref)

        yh, rstd = _layernorm_stats(jnp.concatenate([y_ref[...], yn_ref[...]], axis=0))
        g = g_ref[...]
        yln = yh * g + b2_ref[...]
        sg = _sigmoid(yln)
        dout = jnp.concatenate([do_ref[...], jnp.where(last, 0.0, don_ref[...])], axis=0)
        dyln = dout * (sg * (1.0 + yln * (1.0 - sg)))
        dyh = dyln * g
        dy = rstd * (dyh - jnp.mean(dyh, axis=-1, keepdims=True) - yh * jnp.mean(dyh * yh, axis=-1, keepdims=True))
        dy_ref[0, 0:CONV_EXT, :] = dy
        dy_ref[0, CONV_EXT:CONV_BUF, :] = jnp.zeros((SUBLANES, CONV_CH), F32)
        _fill_shifted(dy_ref)

        own = slice(0, CONV_TILE)
        dvec_ref[0:1, :] += jnp.sum(dy[own], axis=0, keepdims=True)
        dvec_ref[1:2, :] += jnp.sum(dyln[own] * yh[own], axis=0, keepdims=True)
        dvec_ref[2:3, :] += jnp.sum(dyln[own], axis=0, keepdims=True)
        for k in range(CONV_K):
            dw_ref[k:k + 1, :] += jnp.sum(dy[own] * _shifted(gl_ref, 0, 2 + k, CONV_TILE), axis=0, keepdims=True)

        for c0 in range(0, CONV_TILE, CONV_CHUNK):
            acc = jnp.zeros((CONV_CHUNK, CONV_CH), F32)
            for k in range(CONV_K):
                acc = acc + w_ref[k:k + 1, :] * _shifted(dy_ref, c0, CONV_K - 1 - k, CONV_CHUNK)
            rows = slice(c0, c0 + CONV_CHUNK)
            val = cur_ref[rows, VAL0:GATE0]
            sgate = _sigmoid(cur_ref[rows, GATE0:GATE0 + CONV_CH])
            du_ref[rows, VAL0:GATE0] = (acc * sgate).astype(du_ref.dtype)
            du_ref[rows, GATE0:GATE0 + CONV_CH] = (acc * val * sgate * (1.0 - sgate)).astype(du_ref.dtype)
        du_ref[:, 0:QKV_WIDTH] = du_in_ref[:, 0:QKV_WIDTH]

    vec = pl.BlockSpec((1, CONV_CH), lambda i: (0, 0))
    n_halo = t // CONV_HALO
    return pl.pallas_call(
        body, name="conv_bwd", grid=(nt,),
        in_specs=[
            pl.BlockSpec((CONV_TILE, IN_COLS), lambda i: (i, 0)),
            pl.BlockSpec((CONV_HALO, IN_COLS), lambda i: (jnp.maximum(i * per - 1, 0), 0)),
            pl.BlockSpec((CONV_TILE, CONV_CH), lambda i: (i, 0)),
            pl.BlockSpec((CONV_HALO, CONV_CH), lambda i: (jnp.minimum((i + 1) * per, n_halo - 1), 0)),
            pl.BlockSpec((CONV_TILE, CONV_CH), lambda i: (i, 1)),
            pl.BlockSpec((CONV_HALO, CONV_CH), lambda i: (jnp.minimum((i + 1) * per, n_halo - 1), 1)),
            pl.BlockSpec((CONV_TILE, IN_COLS), lambda i: (i, 0)),
            pl.BlockSpec((CONV_HALO, CONV_CH), lambda i: (0, 0)),
            vec, vec,
        ],
        out_specs=[
            pl.BlockSpec((CONV_TILE, IN_COLS), lambda i: (i, 0)),
            pl.BlockSpec((CONV_HALO, CONV_CH), lambda i: (0, 0)),
            pl.BlockSpec((8, CONV_CH), lambda i: (0, 0)),
        ],
        out_shape=[
            jax.ShapeDtypeStruct(du.shape, du.dtype),
            jax.ShapeDtypeStruct((CONV_HALO, CONV_CH), F32),
            jax.ShapeDtypeStruct((8, CONV_CH), F32),
        ],
        scratch_shapes=[pltpu.VMEM((SUBLANES, CONV_BUF, CONV_CH), F32), pltpu.VMEM((SUBLANES, CONV_BUF, CONV_CH), F32)],
        input_output_aliases={6: 0}, compiler_params=_cp(),
    )(u, u, y, y, dmixed, dmixed, du, conv_w, ln_g.reshape(1, CONV_CH), ln_b.reshape(1, CONV_CH))


def adamw(w, g, m, v, *, name):
    r, c = w.shape
    tr = r
    for cand in (512, 256, 128, 64, 32, 16, 8):
        if r % cand == 0 and r > cand:
            tr = cand
            break

    def body(w_ref, g_ref, m_ref, v_ref, d_ref, nm_ref, nv_ref):
        d_ref[...], nm_ref[...], nv_ref[...] = _adamw_math(w_ref[...], g_ref[...], m_ref[...], v_ref[...])

    spec = pl.BlockSpec((tr, c), lambda i: (i, 0))
    shape = jax.ShapeDtypeStruct((r, c), F32)
    return pl.pallas_call(
        body, name=name, grid=(r // tr,), in_specs=[spec] * 4, out_specs=[spec] * 3,
        out_shape=[shape] * 3, compiler_params=_cp(),
    )(w, g, m, v)


def _position():
    return lax.axis_index("x"), lax.axis_index("y"), lax.axis_index("c")


def all_gather_many(shards, *, name):
    n = len(shards)

    def body(*refs):
        x_refs, out_refs, token_ref = refs[:n], refs[n:2 * n], refs[2 * n]
        send_sems, recv_sems, local_sems = refs[2 * n + 1:]
        x, y, c = _position()
        me, sibling = (x, y, c), (x, y, 1 - c)
        chips = [(1 - x, y), (x, 1 - y), (1 - x, 1 - y)]
        token_ref[...] = jnp.zeros_like(token_ref)

        def rows(t, px, py, pc):
            return out_refs[t].at[4 * px + 2 * py + pc]

        def copy(t, k, block, to, src=None):
            return pltpu.make_async_remote_copy(
                src_ref=rows(t, *block) if src is None else src, dst_ref=rows(t, *block),
                send_sem=send_sems.at[7 * t + k], recv_sem=recv_sems.at[7 * t + k], device_id=to, device_id_type=MESH)

        mine = [pltpu.make_async_copy(x_refs[t], rows(t, *me), local_sems.at[t]) for t in range(n)]
        for cp in mine:
            cp.start()
        first = []
        for t in range(n):
            first.append(copy(t, 0, me, sibling, src=x_refs[t]))
            first += [copy(t, 1 + j, me, (*chip, c), src=x_refs[t]) for j, chip in enumerate(chips)]
        for cp in first:
            cp.start()
        passed = []
        for t in range(n):
            for j, chip in enumerate(chips):
                copy(t, 1 + j, (*chip, c), me).wait_recv()
                passed.append(copy(t, 4 + j, (*chip, c), sibling))
                passed[-1].start()
        for t in range(n):
            copy(t, 0, sibling, me).wait_recv()
            for j, chip in enumerate(chips):
                copy(t, 4 + j, (*chip, 1 - c), me).wait_recv()
        for cp in first + passed:
            cp.wait_send()
        for cp in mine:
            cp.wait()

    hbm = pl.BlockSpec(memory_space=pltpu.HBM)
    out = pl.pallas_call(
        body, name=name,
        out_shape=[jax.ShapeDtypeStruct((N_DEV,) + s.shape, s.dtype) for s in shards] + [jax.ShapeDtypeStruct((8, LANES), F32)],
        in_specs=[hbm] * n, out_specs=[hbm] * n + [pl.BlockSpec(memory_space=pltpu.VMEM)],
        scratch_shapes=[pltpu.SemaphoreType.DMA((7 * n,)), pltpu.SemaphoreType.DMA((7 * n,)), pltpu.SemaphoreType.DMA((n,))],
        compiler_params=_cp(),
    )(*shards)
    return out[:n], out[n]


_HBM = pl.BlockSpec(memory_space=pltpu.HBM)
_SEM = pl.BlockSpec(memory_space=pltpu.SEMAPHORE)
_EFFECT = pltpu.SideEffectType.DATAFLOW_SIDE_EFFECTING


def _split_copies(src_refs, land_refs, send_sems, recv_sems, plan, n_copies):
    copies = []
    for t, (src_ref, land_ref) in enumerate(zip(src_refs, land_refs)):
        for k in range(n_copies):
            s, d, to = plan(src_ref, land_ref, k)
            copies.append(pltpu.make_async_remote_copy(
                src_ref=s, dst_ref=d, send_sem=send_sems.at[n_copies * t + k], recv_sem=recv_sems.at[n_copies * t + k],
                device_id=to, device_id_type=MESH))
    return copies


def split_start(srcs, lands, plan, n_copies, *, name):
    n = len(srcs)

    def body(*refs):
        src_refs, land_refs, send_sems, recv_sems, token = refs[:n], refs[n:2 * n], refs[2 * n], refs[2 * n + 1], refs[-1]
        for cp in _split_copies(src_refs, land_refs, send_sems, recv_sems, plan, n_copies):
            cp.start()
        token[...] = jnp.zeros_like(token)

    both = list(srcs) + list(lands)
    out = pl.pallas_call(
        body, name=name,
        out_shape=(pltpu.SemaphoreType.DMA((n_copies * n,)), pltpu.SemaphoreType.DMA((n_copies * n,)),
                   *[pltpu.HBM(a.shape, a.dtype) for a in both], jax.ShapeDtypeStruct((8, LANES), F32)),
        in_specs=(_HBM,) * (2 * n), out_specs=(_SEM, _SEM) + (_HBM,) * (2 * n) + (pl.BlockSpec(memory_space=pltpu.VMEM),),
        input_output_aliases={i: 2 + i for i in range(2 * n)},
        compiler_params=pltpu.CompilerParams(has_side_effects=_EFFECT),
    )(*[pltpu.with_memory_space_constraint(a, pltpu.HBM) for a in both])
    return out[0], out[1], list(out[2:2 + n]), list(out[2 + n:2 + 2 * n]), out[-1]


def split_wait(started, after, plan, n_copies, *, name):
    send_sems, recv_sems, srcs, lands, _ = started
    n = len(srcs)

    def body(*refs):
        src_refs, land_refs, send_sems, recv_sems = refs[:n], refs[n:2 * n], refs[2 * n], refs[2 * n + 1]
        for cp in _split_copies(src_refs, land_refs, send_sems, recv_sems, plan, n_copies):
            cp.wait_send()
            cp.wait_recv()

    both = list(srcs) + list(lands)
    out = pl.pallas_call(
        body, name=name,
        out_shape=tuple(pltpu.HBM(a.shape, a.dtype) for a in both),
        in_specs=(_HBM,) * (2 * n) + (_SEM, _SEM, pl.BlockSpec(memory_space=pl.ANY)), out_specs=(_HBM,) * (2 * n),
        input_output_aliases={i: i for i in range(2 * n)},
        compiler_params=pltpu.CompilerParams(has_side_effects=_EFFECT),
    )(*both, send_sems, recv_sems, after)
    return list(out[:n]), list(out[n:])


def _other_chips(x, y):
    return [(1 - x, y), (x, 1 - y), (1 - x, 1 - y)]


def _remote(src, dst, send_sem, recv_sem, to):
    return pltpu.make_async_remote_copy(src_ref=src, dst_ref=dst, send_sem=send_sem, recv_sem=recv_sem,
                                        device_id=to, device_id_type=MESH)


def gather_start(groups, *, name):
    counts = [len(shards) for shards, _ in groups]
    flat = [a for shards, _ in groups for a in shards] + [a for _, lands in groups for a in lands]
    n_all, n_groups = sum(counts), len(groups)

    def body(*refs):
        s_refs, l_refs = refs[:n_all], refs[n_all:2 * n_all]
        sems = refs[2 * n_all:2 * n_all + 3 * n_groups]
        x, y, c = _position()
        me = 4 * x + 2 * y + c
        at = 0
        for gi, n in enumerate(counts):
            send, recv_sibling, recv_ici = sems[3 * gi:3 * gi + 3]
            for t in range(n):
                src, dst = s_refs[at + t], l_refs[at + t].at[me]
                _remote(src, dst, send.at[4 * t], recv_sibling.at[t], (x, y, 1 - c)).start()
                for j, chip in enumerate(_other_chips(x, y)):
                    _remote(src, dst, send.at[4 * t + 1 + j], recv_ici.at[3 * t + j], (*chip, c)).start()
            at += n
        refs[-1][...] = jnp.zeros_like(refs[-1])

    sem_shapes = [pltpu.SemaphoreType.DMA((k * n,)) for n in counts for k in (4, 1, 3)]
    out = pl.pallas_call(
        body, name=name,
        out_shape=(*sem_shapes, *[pltpu.HBM(a.shape, a.dtype) for a in flat], jax.ShapeDtypeStruct((8, LANES), F32)),
        in_specs=(_HBM,) * (2 * n_all),
        out_specs=(_SEM,) * (3 * n_groups) + (_HBM,) * (2 * n_all) + (pl.BlockSpec(memory_space=pltpu.VMEM),),
        input_output_aliases={i: 3 * n_groups + i for i in range(2 * n_all)},
        compiler_params=pltpu.CompilerParams(has_side_effects=_EFFECT),
    )(*[pltpu.with_memory_space_constraint(a, pltpu.HBM) for a in flat])
    thru = out[3 * n_groups:-1]
    states, at = [], 0
    for gi, n in enumerate(counts):
        states.append(dict(shards=list(thru[at:at + n]), lands=list(thru[n_all + at:n_all + at + n]),
                           send=out[3 * gi], recv_sibling=out[3 * gi + 1], recv_ici=out[3 * gi + 2]))
        at += n
    return states, out[-1]


def gather_forward(states, after, *, name):
    counts = [len(s["lands"]) for s in states]
    flat = [a for s in states for a in s["lands"]]
    n_all, n_groups = sum(counts), len(states)

    def body(*refs):
        l_refs = refs[:n_all]
        recv_ici = refs[n_all:n_all + n_groups]
        fwd = refs[n_all + n_groups + 1:n_all + n_groups + 1 + 2 * n_groups]
        x, y, c = _position()
        at = 0
        for gi, n in enumerate(counts):
            fwd_send, fwd_recv = fwd[2 * gi], fwd[2 * gi + 1]
            for t in range(n):
                for j, (px, py) in enumerate(_other_chips(x, y)):
                    block = l_refs[at + t].at[4 * px + 2 * py + c]
                    _remote(block, block, fwd_send.at[3 * t + j], recv_ici[gi].at[3 * t + j], (px, py, c)).wait_recv()
                    _remote(block, block, fwd_send.at[3 * t + j], fwd_recv.at[3 * t + j], (x, y, 1 - c)).start()
            at += n
        refs[-1][...] = jnp.zeros_like(refs[-1])

    sem_shapes = [pltpu.SemaphoreType.DMA((3 * n,)) for n in counts for _ in range(2)]
    out = pl.pallas_call(
        body, name=name,
        out_shape=(*sem_shapes, *[pltpu.HBM(a.shape, a.dtype) for a in flat], jax.ShapeDtypeStruct((8, LANES), F32)),
        in_specs=(_HBM,) * n_all + (_SEM,) * n_groups + (pl.BlockSpec(memory_space=pl.ANY),),
        out_specs=(_SEM,) * (2 * n_groups) + (_HBM,) * n_all + (pl.BlockSpec(memory_space=pltpu.VMEM),),
        input_output_aliases={i: 2 * n_groups + i for i in range(n_all)},
        compiler_params=pltpu.CompilerParams(has_side_effects=_EFFECT),
    )(*flat, *[s["recv_ici"] for s in states], after)
    at = 0
    for gi, (s, n) in enumerate(zip(states, counts)):
        s.update(fwd_send=out[2 * gi], fwd_recv=out[2 * gi + 1], lands=list(out[2 * n_groups + at:2 * n_groups + at + n]))
        at += n
    return out[-1]


def gather_finish(state, after, *, name):
    n = len(state["lands"])

    def body(*refs):
        s_refs, l_refs = refs[:n], refs[n:2 * n]
        send, recv_sibling, fwd_send, fwd_recv = refs[2 * n:2 * n + 4]
        x, y, c = _position()
        me = 4 * x + 2 * y + c
        for t in range(n):
            own = l_refs[t].at[me]
            _remote(s_refs[t], own, send.at[4 * t], recv_sibling.at[t], (x, y, 1 - c)).wait_send()
            _remote(s_refs[t], l_refs[t].at[4 * x + 2 * y + 1 - c], send.at[4 * t], recv_sibling.at[t], (x, y, 1 - c)).wait_recv()
            for j, (px, py) in enumerate(_other_chips(x, y)):
                _remote(s_refs[t], own, send.at[4 * t + 1 + j], recv_sibling.at[t], (px, py, c)).wait_send()
                mine, theirs = l_refs[t].at[4 * px + 2 * py + c], l_refs[t].at[4 * px + 2 * py + 1 - c]
                _remote(mine, mine, fwd_send.at[3 * t + j], fwd_recv.at[3 * t + j], (x, y, 1 - c)).wait_send()
                _remote(theirs, theirs, fwd_send.at[3 * t + j], fwd_recv.at[3 * t + j], (x, y, 1 - c)).wait_recv()

    both = state["shards"] + state["lands"]
    out = pl.pallas_call(
        body, name=name,
        out_shape=tuple(pltpu.HBM(a.shape, a.dtype) for a in both),
        in_specs=(_HBM,) * (2 * n) + (_SEM,) * 4 + (pl.BlockSpec(memory_space=pl.ANY),), out_specs=(_HBM,) * (2 * n),
        input_output_aliases={i: i for i in range(2 * n)},
        compiler_params=pltpu.CompilerParams(has_side_effects=_EFFECT),
    )(*both, state["send"], state["recv_sibling"], state["fwd_send"], state["fwd_recv"], after)
    return list(out[n:])


def _all_peers_plan(src_ref, land_ref, k):
    x, y, c = _position()
    bits = k + 1
    peer = ((1 - x) if bits & 4 else x, (1 - y) if bits & 2 else y, (1 - c) if bits & 1 else c)
    return src_ref, land_ref.at[4 * x + 2 * y + c], peer


def _sibling_plan(src_ref, land_ref, k):
    x, y, c = _position()
    return src_ref.at[2 * k + (1 - c)], land_ref.at[k], (x, y, 1 - c)


def _chips_plan(src_ref, land_ref, j):
    x, y, c = _position()
    px, py = _other_chips(x, y)[j]
    return src_ref.at[j], land_ref.at[j], (px, py, c)


SUM_STEPS = 2


def sum_for_chips(parts, from_sibling, ck_idx, *, name):
    n = len(parts)

    def body(ck_ref, *refs):
        del ck_ref
        for t in range(n):
            refs[2 * n + t][...] = (refs[t][...] + refs[n + t][...]).astype(BF16)

    def blk(a):
        return (None, a.shape[1] // SUM_STEPS, a.shape[2])

    return pl.pallas_call(
        body, name=name,
        grid_spec=pltpu.PrefetchScalarGridSpec(
            num_scalar_prefetch=1, grid=(3, SUM_STEPS),
            in_specs=[pl.BlockSpec(blk(a), lambda j, i, ck: (2 * ck[1 + j] + ck[0], i, 0)) for a in parts]
            + [pl.BlockSpec(blk(a), lambda j, i, ck: (ck[1 + j], i, 0)) for a in from_sibling],
            out_specs=[pl.BlockSpec(blk(a), lambda j, i, ck: (j, i, 0)) for a in from_sibling]),
        out_shape=[jax.ShapeDtypeStruct((3,) + a.shape[1:], BF16) for a in from_sibling], compiler_params=_cp(),
    )(ck_idx, *parts, *from_sibling)


def sum_final(parts, from_sibling, from_chips, kc_idx, *, name):
    n = len(parts)

    def body(kc_ref, *refs):
        del kc_ref
        for t in range(n):
            p, s, a, b, d = (refs[j * n + t] for j in range(5))
            refs[5 * n + t][...] = (((p[...] + s[...]) + a[...].astype(F32)) + b[...].astype(F32)) + d[...].astype(F32)

    def blk(a):
        return (None, a.shape[1] // SUM_STEPS, a.shape[2])

    def chip_specs(j):
        return [pl.BlockSpec(blk(a), lambda i, kc: (j, i, 0)) for a in from_chips]

    return pl.pallas_call(
        body, name=name,
        grid_spec=pltpu.PrefetchScalarGridSpec(
            num_scalar_prefetch=1, grid=(SUM_STEPS,),
            in_specs=[pl.BlockSpec(blk(a), lambda i, kc: (2 * kc[0] + kc[1], i, 0)) for a in parts]
            + [pl.BlockSpec(blk(a), lambda i, kc: (kc[0], i, 0)) for a in from_sibling]
            + chip_specs(0) + chip_specs(1) + chip_specs(2),
            out_specs=[pl.BlockSpec(blk(a)[1:], lambda i, kc: (i, 0)) for a in parts]),
        out_shape=[jax.ShapeDtypeStruct(a.shape[1:], F32) for a in parts], compiler_params=_cp(),
    )(kc_idx, *parts, *from_sibling, *from_chips, *from_chips, *from_chips)


BIG = (
    ("w_in", IN_COLS, True), ("w_out", D_MODEL, False), ("wq_x", D_MODEL, False), ("wkv_x", 2 * D_MODEL, True),
    ("wo_x", D_MODEL, False), ("w_gate_up", 2 * D_FF, True), ("w_down", D_FF, False),
)

SMALL = ("norm_mix_g", "q_norm_g", "k_norm_g", "sinks", "conv_b", "conv_ln_g", "conv_ln_b",
         "norm_x_g", "norm_mem_g", "xq_norm_g", "xk_norm_g", "norm_ffn_g")


def _adamw_math(w, g, m, v):
    m2 = ADAM_B1 * m + (1.0 - ADAM_B1) * g
    v2 = ADAM_B2 * v + (1.0 - ADAM_B2) * jnp.square(g)
    m_hat = m2 / (1.0 - ADAM_B1 ** ADAM_STEP)
    v_hat = v2 / (1.0 - ADAM_B2 ** ADAM_STEP)
    return -ADAM_LR * (m_hat / (jnp.sqrt(v_hat) + ADAM_EPS) + ADAM_WD * w), m2, v2


def _small_rows(per_layer_shape):
    return 1 if len(per_layer_shape) == 1 else per_layer_shape[0]


def pack_small(parts, shapes):
    blocks = []
    for per_layer, sh in zip(parts, shapes):
        for g in per_layer:
            g = g.reshape(_small_rows(sh), sh[-1])
            blocks.append(jnp.pad(g, ((0, 0), (0, D_MODEL - sh[-1]))))
    rows = sum(b.shape[0] for b in blocks)
    blocks.append(jnp.zeros((-rows % 8, D_MODEL), F32))
    return jnp.concatenate(blocks, axis=0)


def update_small(gathered, shapes, weights, moments_m, moments_v, n_update):
    n_all = len(shapes)

    def body(*refs):
        g_ref = refs[0]
        w_refs, m_refs, v_refs = (refs[1 + j * n_update:1 + (j + 1) * n_update] for j in range(3))
        out = refs[1 + 3 * n_update:]
        grad_refs = out[:n_all]
        d_refs, nm_refs, nv_refs = (out[n_all + j * n_update:n_all + (j + 1) * n_update] for j in range(3))
        at = 0
        for p, sh in enumerate(shapes):
            rows, lanes = _small_rows(sh), sh[-1]
            for l in range(DEPTH):
                g = g_ref[0, at:at + rows, 0:lanes]
                for k in range(1, N_DEV):
                    g = g + g_ref[k, at:at + rows, 0:lanes]
                at += rows
                here = (slice(l, l + 1),) + (slice(None),) * (len(sh) - 1) if len(sh) == 1 else (l,)
                grad_refs[p][here] = g
                if p < n_update:
                    d, m2, v2 = _adamw_math(w_refs[p][here], g, m_refs[p][here], v_refs[p][here])
                    d_refs[p][here] = d
                    nm_refs[p][here] = m2
                    nv_refs[p][here] = v2

    full = [jax.ShapeDtypeStruct((DEPTH,) + tuple(sh), F32) for sh in shapes]
    out = pl.pallas_call(
        body, name="update_small", out_shape=full + full[:n_update] * 3, compiler_params=_cp(),
    )(gathered, *weights, *moments_m, *moments_v)
    return (out[:n_all], out[n_all:n_all + n_update], out[n_all + n_update:n_all + 2 * n_update],
            out[n_all + 2 * n_update:])


WEIGHT_GROUPS = {"in": ("w_in",), "mid": ("w_out", "wq_x", "wkv_x", "wo_x"), "ffn": ("w_gate_up", "w_down")}


def _layer_fwd(x0, mem, weights_of, s, reached, target=None):
    w = dict(weights_of("in", x0))
    h0, u = norm_proj(x0, s["norm_mix_g"], w["w_in"])
    mixed = swa_fwd(u, s["q_norm_g"], s["k_norm_g"], s["sinks"])
    reached("attn", mixed)
    mixed, conv_y = conv_fwd(u, mixed, s["conv_w"], s["conv_b"], s["conv_ln_g"], s["conv_ln_b"])
    w.update(weights_of("mid", conv_y))
    memn = rms_fwd(mem, s["norm_mem_g"])
    kv = mm(memn, w["wkv_x"], trans_b=True, out_dtype=F32, name="mm_kv")
    x1, h1, qx, o, x2, h2 = mid_fwd(mixed, x0, w["w_out"], s["norm_x_g"], w["wq_x"], kv, s["xq_norm_g"], s["xk_norm_g"],
                                    w["wo_x"], s["norm_ffn_g"])
    reached("mid", x2)
    w.update(weights_of("ffn", x2))
    gu, a, *out = ffn_fwd(h2, x2, w["w_gate_up"], w["w_down"], target)
    saved = dict(x0=x0, h0=h0, u=u, conv_y=conv_y, mixed=mixed, x1=x1, h1=h1, qx=qx, memn=memn, kv=kv, o=o, x2=x2, h2=h2,
                 gu=gu, a=a)
    return out, saved, w


def _ordered_after(a, token):
    return a if token is None else a + token[0, 0]


def _layer_bwd(dx3, mem, w, s, sv, token, stage_done):
    gs = {}
    dgu, dx2, dg = ffn_bwd(dx3, sv["gu"], sv["x2"], _ordered_after(s["norm_ffn_g"], token), w["w_down"], w["w_gate_up"])
    gs["norm_ffn_g"] = dg
    gb = {"w_down": mm_tn(sv["a"], dx3, name="mm_dw_down")}
    gb["w_gate_up"] = mm_tn(dgu, sv["h2"], tk=dgu.shape[0], name="mm_dw_gate_up")
    token = stage_done("ffn", gb, gb["w_gate_up"])

    gb = {}
    dq, dx1, dmixed, dkv, dqg, dkg, dg = mid_bwd(dx2, sv["qx"], sv["kv"], s["xq_norm_g"], s["xk_norm_g"], sv["x1"],
                                                 _ordered_after(s["norm_x_g"], token), w["wo_x"], w["wq_x"], w["w_out"])
    gs["xq_norm_g"], gs["xk_norm_g"], gs["norm_x_g"] = dqg, dkg, dg
    gb["wo_x"] = mm_tn(sv["o"], dx2, name="mm_dwo")
    gb["wq_x"] = mm_tn(sv["h1"], dq, name="mm_dwq")
    dmemn = mm(dkv, w["wkv_x"], trans_b=False, out_dtype=F32, name="mm_dmemn")
    gb["wkv_x"] = mm_tn(dkv, sv["memn"], name="mm_dwkv")
    gs["norm_mem_g"] = rms_gain_bwd(dmemn, mem)
    gb["w_out"] = mm_tn(sv["mixed"], dx1, name="mm_dw_out")
    token = stage_done("mid", gb, gb["w_out"])

    du, dqg, dkg, dsinks = swa_bwd(sv["u"], dmixed, _ordered_after(s["q_norm_g"], token), s["k_norm_g"], s["sinks"])
    gs["q_norm_g"], gs["k_norm_g"], gs["sinks"] = dqg[0, :HEAD_DIM], dkg[0, :HEAD_DIM], dsinks[0, :N_Q_HEADS]
    token = stage_done("attn", {}, dqg)
    du, dconv_w, dvec = conv_bwd(sv["u"], sv["conv_y"], dmixed, du, s["conv_w"], _ordered_after(s["conv_ln_g"], token),
                                 s["conv_ln_b"])
    gs["conv_w"] = dconv_w[:CONV_K]
    gs["conv_b"], gs["conv_ln_g"], gs["conv_ln_b"] = dvec[0], dvec[1], dvec[2]
    dw_in = mm_tn(du, sv["h0"], tk=2048, name="mm_dw_in")
    token = stage_done("in", {"w_in": dw_in}, dw_in)
    dx0, dg = in_bwd(du, w["w_in"], sv["x0"], _ordered_after(s["norm_mix_g"], token), dx1)
    gs["norm_mix_g"] = dg
    token = stage_done("mix", {}, dx0)
    return dx0, gs, token


def _local_step(x, mem, target, weights_of, reached, smalls, stage_done):
    saved, weights = [], []
    out = [x]
    for l in range(DEPTH):
        out, sv, w = _layer_fwd(out[0], mem, functools.partial(weights_of, l), smalls[l], functools.partial(reached, l),
                                target if l == DEPTH - 1 else None)
        saved.append(sv)
        weights.append(w)
    dx, loss_part = out
    gss, token = [None] * DEPTH, None
    for l in reversed(range(DEPTH)):
        dx, gss[l], token = _layer_bwd(dx, mem, weights[l], smalls[l], saved[l], token,
                                       functools.partial(stage_done, l))
    return loss_part[0, 0], dx, gss


def kernel(x, mem, norm_mix_g, w_in, q_norm_g, k_norm_g, sinks, conv_w, conv_b, conv_ln_g, conv_ln_b, w_out, norm_x_g, norm_mem_g, wq_x, wkv_x, xq_norm_g, xk_norm_g, wo_x, norm_ffn_g, w_gate_up, w_down, loss_target, m_norm_mix_g, m_w_in, m_q_norm_g, m_k_norm_g, m_sinks, m_conv_w, m_conv_b, m_conv_ln_g, m_conv_ln_b, m_w_out, m_norm_x_g, m_norm_mem_g, m_wq_x, m_wkv_x, m_xq_norm_g, m_xk_norm_g, m_wo_x, m_norm_ffn_g, m_w_gate_up, m_w_down, v_norm_mix_g, v_w_in, v_q_norm_g, v_k_norm_g, v_sinks, v_conv_w, v_conv_b, v_conv_ln_g, v_conv_ln_b, v_w_out, v_norm_x_g, v_norm_mem_g, v_wq_x, v_wkv_x, v_xq_norm_g, v_xk_norm_g, v_wo_x, v_norm_ffn_g, v_w_gate_up, v_w_down):
    P = dict(norm_mix_g=norm_mix_g, w_in=w_in, q_norm_g=q_norm_g, k_norm_g=k_norm_g, sinks=sinks, conv_w=conv_w, conv_b=conv_b,
             conv_ln_g=conv_ln_g, conv_ln_b=conv_ln_b, w_out=w_out, norm_x_g=norm_x_g, norm_mem_g=norm_mem_g, wq_x=wq_x,
             wkv_x=wkv_x, xq_norm_g=xq_norm_g, xk_norm_g=xk_norm_g, wo_x=wo_x, norm_ffn_g=norm_ffn_g, w_gate_up=w_gate_up,
             w_down=w_down)
    M = dict(norm_mix_g=m_norm_mix_g, w_in=m_w_in, q_norm_g=m_q_norm_g, k_norm_g=m_k_norm_g, sinks=m_sinks, conv_w=m_conv_w,
             conv_b=m_conv_b, conv_ln_g=m_conv_ln_g, conv_ln_b=m_conv_ln_b, w_out=m_w_out, norm_x_g=m_norm_x_g,
             norm_mem_g=m_norm_mem_g, wq_x=m_wq_x, wkv_x=m_wkv_x, xq_norm_g=m_xq_norm_g, xk_norm_g=m_xk_norm_g, wo_x=m_wo_x,
             norm_ffn_g=m_norm_ffn_g, w_gate_up=m_w_gate_up, w_down=m_w_down)
    V = dict(norm_mix_g=v_norm_mix_g, w_in=v_w_in, q_norm_g=v_q_norm_g, k_norm_g=v_k_norm_g, sinks=v_sinks, conv_w=v_conv_w,
             conv_b=v_conv_b, conv_ln_g=v_conv_ln_g, conv_ln_b=v_conv_ln_b, w_out=v_w_out, norm_x_g=v_norm_x_g,
             norm_mem_g=v_norm_mem_g, wq_x=v_wq_x, wkv_x=v_wkv_x, xq_norm_g=v_xq_norm_g, xk_norm_g=v_xk_norm_g, wo_x=v_wo_x,
             norm_ffn_g=v_norm_ffn_g, w_gate_up=v_w_gate_up, w_down=v_w_down)
    order = ["norm_mix_g", "w_in", "q_norm_g", "k_norm_g", "sinks", "conv_w", "conv_b", "conv_ln_g", "conv_ln_b", "w_out",
             "norm_x_g", "norm_mem_g", "wq_x", "wkv_x", "xq_norm_g", "xk_norm_g", "wo_x", "norm_ffn_g", "w_gate_up", "w_down"]
    xi, yi, ci = _position()
    dev = 4 * xi + 2 * yi + ci
    x2d, mem2d, tgt2d = x[0], mem[0], loss_target[0]

    def travelling(name, l, transposed):
        a = P[name][l]
        return (a.T if transposed else a).astype(BF16)

    rows_of = {n: rows for n, rows, _ in BIG}
    transposed_of = {n: tr for n, _, tr in BIG}

    def whole(names, gathered):
        return {n: g.reshape(rows_of[n], D_MODEL) for n, g in zip(names, gathered)}

    cw = jnp.pad(conv_w.reshape(DEPTH * CONV_K, CONV_CH // N_DEV), ((0, 2), (0, LANES - CONV_CH // N_DEV)))
    (w_in0, cw_all), token0 = all_gather_many([travelling("w_in", 0, True), cw], name="ag_w_in0_conv_w")
    travel_order = [(0, "mid"), (0, "ffn"), (1, "in"), (1, "mid"), (1, "ffn")]
    travel_groups = []
    for l, group in travel_order:
        shards = [_ordered_after(travelling(n, l, transposed_of[n]), token0.astype(BF16)) for n in WEIGHT_GROUPS[group]]
        lands = [lax.dynamic_update_slice(lax.empty((N_DEV,) + s.shape, BF16), s[None], (dev, 0, 0)) for s in shards]
        travel_groups.append((shards, lands))
    travel_states, travel_token = gather_start(travel_groups, name="ag_weights_start")
    travelling_state = dict(zip(travel_order, travel_states))
    forward_at = {(0, "attn"): [(0, "mid")], (0, "mid"): [(0, "ffn"), (1, "in")], (1, "attn"): [(1, "mid"), (1, "ffn")]}

    def reached(l, stage, marker):
        keys = forward_at.get((l, stage))
        if keys:
            gather_forward([travelling_state[k] for k in keys], marker,
                           name="ag_weights_forward_" + "_".join(f"{g}{ll}" for ll, g in keys))

    def weights_of(l, group, marker):
        if (l, group) == (0, "in"):
            return whole(WEIGHT_GROUPS[group], [w_in0])
        gathered = gather_finish(travelling_state[(l, group)], marker, name=f"ag_weights_finish_{group}{l}")
        return whole(WEIGHT_GROUPS[group], gathered)

    cw_full = cw_all[:, :DEPTH * CONV_K, :CONV_CH // N_DEV].reshape(N_DEV, DEPTH, CONV_K, CONV_CH // N_DEV)
    cw_full = jnp.transpose(cw_full, (1, 2, 0, 3)).reshape(DEPTH, CONV_K, CONV_CH)
    smalls = []
    for l in range(DEPTH):
        sl = {n: P[n][l] if n == "sinks" else P[n][l:l + 1] for n in SMALL}
        sl["conv_w"] = jnp.pad(cw_full[l], ((0, CONV_HALO - CONV_K), (0, 0)))
        smalls.append(sl)
    smalls[0]["norm_mix_g"] = _ordered_after(smalls[0]["norm_mix_g"], travel_token)

    ck_idx = jnp.stack([ci] + [2 * px + py for px, py in _other_chips(xi, yi)]).astype(jnp.int32)
    kc_idx = jnp.stack([2 * xi + yi, ci]).astype(jnp.int32)
    got, flight, reduced = {}, {}, {}

    def as_parts(gb):
        keys = sorted(gb)
        return keys, [gb[k].reshape(N_DEV, rows_of[k[1]] // N_DEV, D_MODEL) for k in keys]

    def lands_like(parts, blocks, dtype):
        return [lax.empty((blocks,) + p.shape[1:], dtype) for p in parts]

    def to_sibling(group, gb):
        keys, parts = as_parts(gb)
        flight[group] = (keys, split_start(parts, lands_like(parts, 4, F32), _sibling_plan, 4,
                                           name=f"rs_sibling_{group}_start"))
        return flight[group][1][4]

    def to_chips(group, marker):
        keys, started = flight[group]
        parts, from_sibling = split_wait(started, marker, _sibling_plan, 4, name=f"rs_sibling_{group}_wait")
        chip_sums = sum_for_chips(parts, from_sibling, ck_idx, name=f"rs_sum_for_chips_{group}")
        started = split_start(chip_sums, lands_like(parts, 3, BF16), _chips_plan, 3, name=f"rs_chips_{group}_start")
        flight[group] = (keys, parts, from_sibling, started)
        return started[4]

    def finish(group, marker):
        keys, parts, from_sibling, started = flight[group]
        _, from_chips = split_wait(started, marker, _chips_plan, 3, name=f"rs_chips_{group}_wait")
        reduced.update(zip(keys, sum_final(parts, from_sibling, from_chips, kc_idx, name=f"rs_sum_final_{group}")))

    def stage_done(l, stage, gb, marker):
        gb = {(l, n): g for n, g in gb.items()}
        if l == 1:
            got.update(gb)
            return to_sibling("l1", got) if stage == "mix" else None
        if stage == "ffn":
            return to_chips("l1", marker) + to_sibling("ffn", gb)
        if stage == "mid":
            return to_chips("ffn", marker) + to_sibling("mid", gb)
        if stage == "attn":
            return to_chips("mid", marker)
        if stage == "in":
            return to_sibling("in", gb)
        to_chips("in", marker)
        for group in ("l1", "ffn", "mid"):
            finish(group, marker)
        return None

    loss_part, grad_x, gss = _local_step(x2d, mem2d, tgt2d, weights_of, reached, smalls, stage_done)
    loss = lax.psum(loss_part, ("x", "y", "c"))

    small_names = SMALL + ("conv_w",)
    small_shapes = [(CONV_K, CONV_CH) if n == "conv_w" else P[n].shape[1:] for n in small_names]
    small_parts = pack_small([[gss[l][n] for l in range(DEPTH)] for n in small_names], small_shapes)
    small_land = lax.dynamic_update_slice(lax.empty((N_DEV,) + small_parts.shape, F32), small_parts[None], (dev, 0, 0))
    small_flight = split_start([small_parts], [small_land], _all_peers_plan, N_DEV - 1, name="ag_small_grads_start")

    grads, delta, new_m, new_v = {}, {}, {}, {}

    def update(n, transposed):
        shape = P[n].shape
        two_d = lambda a: a.reshape(shape[0] * shape[1], shape[2])
        grads[n] = jnp.stack([reduced[(l, n)].T if transposed else reduced[(l, n)] for l in range(DEPTH)])
        d_, m_, v_ = adamw(two_d(P[n]), two_d(grads[n]), two_d(M[n]), two_d(V[n]), name="adamw_" + n)
        delta[n], new_m[n], new_v[n] = d_.reshape(shape), m_.reshape(shape), v_.reshape(shape)

    for n, _, transposed in BIG:
        if n != "w_in":
            update(n, transposed)
    finish("in", delta["w_down"])
    update("w_in", True)
    small_all = split_wait(small_flight, delta["w_in"], _all_peers_plan, N_DEV - 1, name="ag_small_grads_wait")[1][0]
    g_, d_, m_, v_ = update_small(small_all, small_shapes, [P[n] for n in SMALL], [M[n] for n in SMALL],
                                  [V[n] for n in SMALL], len(SMALL))
    for i, n in enumerate(SMALL):
        grads[n], delta[n], new_m[n], new_v[n] = g_[i], d_[i], m_[i], v_[i]
    cols = CONV_CH // N_DEV
    grads["conv_w"] = lax.dynamic_slice_in_dim(g_[-1], dev * cols, cols, axis=2)
    flat = lambda a: a.reshape(DEPTH * CONV_K, cols)
    d_, m_, v_ = adamw(flat(conv_w), flat(grads["conv_w"]), flat(m_conv_w), flat(v_conv_w), name="adamw_conv_w")
    delta["conv_w"], new_m["conv_w"], new_v["conv_w"] = (a.reshape(conv_w.shape) for a in (d_, m_, v_))

    return (loss, grad_x[None], *[grads[n] for n in order], *[delta[n] for n in order],
            *[new_m[n] for n in order], *[new_v[n] for n in order])
```

```python
import functools

import jax
import jax.numpy as jnp
import numpy as np
from jax import lax
from jax.experimental import pallas as pl
from jax.experimental.pallas import tpu as pltpu

F32 = jnp.float32
BF16 = jnp.bfloat16

D_MODEL = 1024
HEAD_DIM = 64
N_Q_HEADS = 8
N_KV_HEADS = 2
GROUP = N_Q_HEADS // N_KV_HEADS
ATTN_WIDTH = N_Q_HEADS * HEAD_DIM
KV_WIDTH = N_KV_HEADS * HEAD_DIM
QKV_WIDTH = ATTN_WIDTH + 2 * KV_WIDTH
CONV_CH = 512
IN_COLS = QKV_WIDTH + 2 * CONV_CH
CONV_K = 31
CONV_HALO = 32
BLOCK = 128
N_X_HEADS = 4
X_HEAD_DIM = 256
D_FF = 2816
EPS = 1e-6
NEG = -1e30
DEPTH = 2
N_DEV = 8

ADAM_LR = 0.001
ADAM_B1 = 0.9
ADAM_B2 = 0.999
ADAM_EPS = 1e-08
ADAM_WD = 0.01
ADAM_STEP = 10

V7X_VMEM_LIMIT = 56 * 1024 * 1024
LANES = 128

MESH = pl.DeviceIdType.MESH


def _cp(**kw):
    return pltpu.CompilerParams(vmem_limit_bytes=V7X_VMEM_LIMIT, **kw)


def _dot(a, b, dims):
    return lax.dot_general(a.astype(BF16), b.astype(BF16), (dims, ((), ())), preferred_element_type=F32)


def _dot_nn(a, b):
    return _dot(a, b, ((1,), (0,)))


def _dot_nt(a, b):
    return _dot(a, b, ((1,), (1,)))


def _dot_tn(a, b):
    return _dot(a, b, ((0,), (0,)))


def _sigmoid(x):
    return jax.nn.sigmoid(x)


def _rms(x):
    r = lax.rsqrt(jnp.mean(x * x, axis=-1, keepdims=True) + EPS)
    return x * r, r


def _rms_bwd(dy, xhat, r, g):
    dxh = dy * g
    return r * (dxh - xhat * jnp.mean(dxh * xhat, axis=-1, keepdims=True))


KV_STEPS = 2


def kv_fwd(mem, g, wt_kv):
    n_mem, d = mem.shape
    rows = wt_kv.shape[0] // KV_STEPS

    def body(mem_ref, g_ref, wt_ref, memn_ref, kv_ref):
        @pl.when(pl.program_id(0) == 0)
        def _():
            memn_ref[...] = (_rms(mem_ref[...])[0] * g_ref[...]).astype(memn_ref.dtype)

        kv_ref[...] = _dot_nt(memn_ref[...], wt_ref[...])

    whole = pl.BlockSpec((n_mem, d), lambda j: (0, 0))
    return pl.pallas_call(
        body, name="kv_fwd", grid=(KV_STEPS,),
        in_specs=[whole, pl.BlockSpec((1, d), lambda j: (0, 0)), pl.BlockSpec((rows, d), lambda j: (j, 0))],
        out_specs=[whole, pl.BlockSpec((n_mem, rows), lambda j: (0, j))],
        out_shape=[jax.ShapeDtypeStruct((n_mem, d), BF16), jax.ShapeDtypeStruct((n_mem, wt_kv.shape[0]), F32)],
        compiler_params=_cp(),
    )(mem, g.reshape(1, d), wt_kv)


def kv_bwd(dkv, memn, mem, wt_kv):
    n_mem, d = mem.shape
    rows = wt_kv.shape[0] // KV_STEPS

    def body(dkv_ref, memn_ref, mem_ref, wt_ref, dwt_ref, dg_ref, dmemn_s):
        j = pl.program_id(0)
        dwt_ref[...] = _dot_tn(dkv_ref[...], memn_ref[...])
        part = _dot_nn(dkv_ref[...], wt_ref[...])

        @pl.when(j == 0)
        def _():
            dmemn_s[...] = part

        @pl.when(j > 0)
        def _():
            dmemn_s[...] += part

        @pl.when(j == KV_STEPS - 1)
        def _():
            dg_ref[...] = jnp.sum(dmemn_s[...] * _rms(mem_ref[...])[0], axis=0, keepdims=True)

    whole = pl.BlockSpec((n_mem, d), lambda j: (0, 0))
    return pl.pallas_call(
        body, name="kv_bwd", grid=(KV_STEPS,),
        in_specs=[pl.BlockSpec((n_mem, rows), lambda j: (0, j)), whole, whole, pl.BlockSpec((rows, d), lambda j: (j, 0))],
        out_specs=[pl.BlockSpec((rows, d), lambda j: (j, 0)), pl.BlockSpec((1, d), lambda j: (0, 0))],
        out_shape=[jax.ShapeDtypeStruct(wt_kv.shape, F32), jax.ShapeDtypeStruct((1, d), F32)],
        scratch_shapes=[pltpu.VMEM((n_mem, d), F32)], compiler_params=_cp(),
    )(dkv, memn, mem, wt_kv)


def _tile(n, cap):
    if n <= cap:
        return n
    best = None
    for t in range(LANES, cap + 1, LANES):
        if n % t == 0:
            best = t
    assert best is not None, (n, cap)
    return best


def mm_tn(a, b, *, name, ta_cap=1536, tb_cap=1024, tk=1024):
    m, ka = a.shape
    nb = b.shape[1]
    assert b.shape[0] == m
    tk = min(tk, m)
    ta = _tile(ka, ta_cap)
    tb = _tile(nb, tb_cap)

    def body(a_ref, b_ref, o_ref):
        @pl.when(pl.program_id(2) == 0)
        def _():
            o_ref[...] = jnp.zeros_like(o_ref)

        o_ref[...] += _dot_tn(a_ref[...], b_ref[...])

    return pl.pallas_call(
        body, name=name, grid=(ka // ta, nb // tb, m // tk),
        in_specs=[pl.BlockSpec((tk, ta), lambda i, j, kk: (kk, i)), pl.BlockSpec((tk, tb), lambda i, j, kk: (kk, j))],
        out_specs=pl.BlockSpec((ta, tb), lambda i, j, kk: (i, j)),
        out_shape=jax.ShapeDtypeStruct((ka, nb), F32), compiler_params=_cp(),
    )(a, b)


def _whole(shape):
    return pl.BlockSpec(shape, lambda i: (0,) * len(shape), pipeline_mode=pl.Buffered(1))


def _rows(tm, n):
    return pl.BlockSpec((tm, n), lambda i: (i, 0))


def _vec(n):
    return pl.BlockSpec((1, n), lambda i: (0, 0))


def _chunks(n, cap=1408):
    size = _tile(n, cap)
    return [(s, size) for s in range(0, n, size)]


def _zero_at_first_step(*refs):
    @pl.when(pl.program_id(0) == 0)
    def _():
        for r in refs:
            r[...] = jnp.zeros_like(r)


def norm_proj(x, g, wt, *, tm=512):
    m, d = x.shape
    n = wt.shape[0]

    def body(x_ref, g_ref, wt_ref, h_ref, u_ref):
        h = (_rms(x_ref[...])[0] * g_ref[...]).astype(BF16)
        h_ref[...] = h
        for s, sz in _chunks(n):
            u_ref[:, s:s + sz] = _dot_nt(h, wt_ref[s:s + sz, :])

    return pl.pallas_call(
        body, name="norm_proj", grid=(m // tm,),
        in_specs=[_rows(tm, d), _vec(d), _whole((n, d))],
        out_specs=[_rows(tm, d), _rows(tm, n)],
        out_shape=[jax.ShapeDtypeStruct((m, d), BF16), jax.ShapeDtypeStruct((m, n), F32)],
        compiler_params=_cp(),
    )(x, g.reshape(1, d), wt)


def _xattn_heads(q_ref, kv_ref, qg_v, kg_v, d):
    normed = []
    for h in range(N_X_HEADS):
        cols = slice(h * X_HEAD_DIM, (h + 1) * X_HEAD_DIM)
        qh, rq = _rms(q_ref[:, cols])
        normed.append((qh, rq, (qh * qg_v).astype(BF16), (_rms(kv_ref[:, cols])[0] * kg_v).astype(BF16),
                       kv_ref[:, d + h * X_HEAD_DIM:d + (h + 1) * X_HEAD_DIM].astype(BF16)))
    scores = [_dot_nt(qn, kn) * (X_HEAD_DIM ** -0.5) for _, _, qn, kn, _ in normed]
    out = []
    for (qh, rq, qn, kn, v), s in zip(normed, scores):
        e = jnp.exp(s - jnp.max(s, axis=-1, keepdims=True))
        out.append((qh, rq, qn, kn, v, e / jnp.sum(e, axis=-1, keepdims=True)))
    return out


def mid_fwd(mixed, x0, w_out, g_x, wq, kv, xqg, xkg, wo, g_f, *, tm=512):
    m, d = x0.shape
    n_mem = kv.shape[0]

    def body(mixed_ref, x0_ref, w_out_ref, g_x_ref, wq_ref, kv_ref, xqg_ref, xkg_ref, wo_ref, g_f_ref,
             x1_ref, h1_ref, qx_ref, o_ref, x2_ref, h2_ref):
        x1 = x0_ref[...] + _dot_nn(mixed_ref[...], w_out_ref[...])
        x1_ref[...] = x1
        h1 = (_rms(x1)[0] * g_x_ref[...]).astype(BF16)
        h1_ref[...] = h1
        qx_ref[...] = _dot_nn(h1, wq_ref[...])
        for h, (_, _, _, _, v, p) in enumerate(_xattn_heads(qx_ref, kv_ref, xqg_ref[...], xkg_ref[...], d)):
            o_ref[:, h * X_HEAD_DIM:(h + 1) * X_HEAD_DIM] = _dot_nn(p, v).astype(o_ref.dtype)
        x2 = x1 + _dot_nn(o_ref[...], wo_ref[...])
        x2_ref[...] = x2
        h2_ref[...] = (_rms(x2)[0] * g_f_ref[...]).astype(BF16)

    sq = _whole((d, d))
    f32_rows, bf_rows = jax.ShapeDtypeStruct((m, d), F32), jax.ShapeDtypeStruct((m, d), BF16)
    return pl.pallas_call(
        body, name="mid_fwd", grid=(m // tm,),
        in_specs=[_rows(tm, d), _rows(tm, d), sq, _vec(d), sq, _whole((n_mem, 2 * d)), _vec(X_HEAD_DIM), _vec(X_HEAD_DIM),
                  sq, _vec(d)],
        out_specs=[_rows(tm, d)] * 6,
        out_shape=[f32_rows, bf_rows, f32_rows, bf_rows, f32_rows, bf_rows],
        compiler_params=_cp(),
    )(mixed, x0, w_out, g_x.reshape(1, d), wq, kv, xqg.reshape(1, X_HEAD_DIM), xkg.reshape(1, X_HEAD_DIM), wo,
      g_f.reshape(1, d))


def ffn_fwd(h2, x2, wt_gu, w_down, target=None, *, tm=256):
    m, d = x2.shape
    f = w_down.shape[0]
    with_loss = target is not None

    def body(*refs):
        if with_loss:
            h2_ref, x2_ref, wt_gu_ref, w_down_ref, t_ref, gu_ref, a_ref, dy_ref, l_ref = refs
        else:
            h2_ref, x2_ref, wt_gu_ref, w_down_ref, gu_ref, a_ref, x3_ref = refs
        h = h2_ref[...]
        for s, sz in _chunks(2 * f):
            gu_ref[:, s:s + sz] = _dot_nt(h, wt_gu_ref[s:s + sz, :])
        for s, sz in _chunks(f):
            g = gu_ref[:, s:s + sz]
            a_ref[:, s:s + sz] = (g * _sigmoid(g) * gu_ref[:, f + s:f + s + sz]).astype(a_ref.dtype)
        x3 = x2_ref[...] + _dot_nn(a_ref[...], w_down_ref[...])
        if not with_loss:
            x3_ref[...] = x3
            return
        err = x3 - t_ref[...]
        dy_ref[...] = err * (1.0 / d)
        _zero_at_first_step(l_ref)
        part = jnp.sum(jnp.sum(err * err, axis=-1, keepdims=True), axis=0, keepdims=True)
        l_ref[...] += jnp.broadcast_to(part * (0.5 / d), l_ref.shape)

    last = [_rows(tm, d), pl.BlockSpec((1, LANES), lambda i: (0, 0))] if with_loss else [_rows(tm, d)]
    last_shape = [jax.ShapeDtypeStruct((m, d), F32)] + ([jax.ShapeDtypeStruct((1, LANES), F32)] if with_loss else [])
    return pl.pallas_call(
        body, name="ffn_fwd_loss" if with_loss else "ffn_fwd", grid=(m // tm,),
        in_specs=[_rows(tm, d), _rows(tm, d), _whole((2 * f, d)), _whole((f, d))] + ([_rows(tm, d)] if with_loss else []),
        out_specs=[_rows(tm, 2 * f), _rows(tm, f)] + last,
        out_shape=[jax.ShapeDtypeStruct((m, 2 * f), F32), jax.ShapeDtypeStruct((m, f), BF16)] + last_shape,
        compiler_params=_cp(),
    )(*([h2, x2, wt_gu, w_down] + ([target] if with_loss else [])))


def ffn_bwd(dx3, gu, x2, g_f, w_down, wt_gu, *, tm=256):
    m, d = x2.shape
    f = w_down.shape[0]

    def body(dx3_ref, gu_ref, x2_ref, g_ref, w_down_ref, wt_gu_ref, dgu_ref, dx2_ref, dg_ref):
        _zero_at_first_step(dg_ref)
        dx3 = dx3_ref[...]
        dx3_b = dx3.astype(BF16)
        for s, sz in _chunks(f):
            da = _dot_nt(dx3_b, w_down_ref[s:s + sz, :])
            g = gu_ref[:, s:s + sz]
            u = gu_ref[:, f + s:f + s + sz]
            sg = _sigmoid(g)
            dgu_ref[:, s:s + sz] = (da * u * (sg * (1.0 + g * (1.0 - sg)))).astype(dgu_ref.dtype)
            dgu_ref[:, f + s:f + s + sz] = (da * (g * sg)).astype(dgu_ref.dtype)
        dh2 = _dot_nn(dgu_ref[...], wt_gu_ref[...])
        xh, r = _rms(x2_ref[...])
        dg_ref[...] += jnp.sum(dh2 * xh, axis=0, keepdims=True)
        dx2_ref[...] = dx3 + _rms_bwd(dh2, xh, r, g_ref[...])

    return pl.pallas_call(
        body, name="ffn_bwd", grid=(m // tm,),
        in_specs=[_rows(tm, d), _rows(tm, 2 * f), _rows(tm, d), _vec(d), _whole((f, d)), _whole((2 * f, d))],
        out_specs=[_rows(tm, 2 * f), _rows(tm, d), _vec(d)],
        out_shape=[jax.ShapeDtypeStruct((m, 2 * f), BF16), jax.ShapeDtypeStruct((m, d), F32),
                   jax.ShapeDtypeStruct((1, d), F32)],
        compiler_params=_cp(),
    )(dx3, gu, x2, g_f.reshape(1, d), w_down, wt_gu)


def mid_bwd(dx2, qx, kv, xqg, xkg, x1, g_x, wo, wq, w_out, *, tm=512):
    m, d = x1.shape
    n_mem = kv.shape[0]
    nt = m // tm

    def body(dx2_ref, qx_ref, kv_ref, xqg_ref, xkg_ref, x1_ref, g_x_ref, wo_ref, wq_ref, w_out_ref,
             dq_ref, dx1_ref, dmixed_ref, dkv_ref, dqg_ref, dkg_ref, dg_ref):
        i = pl.program_id(0)
        _zero_at_first_step(dkv_ref, dqg_ref, dkg_ref, dg_ref)
        qg_v, kg_v = xqg_ref[...], xkg_ref[...]
        dx2 = dx2_ref[...]
        do = _dot_nt(dx2, wo_ref[...])
        dqg_acc = jnp.zeros((1, X_HEAD_DIM), F32)
        heads = _xattn_heads(qx_ref, kv_ref, qg_v, kg_v, d)
        head_cols = [slice(h * X_HEAD_DIM, (h + 1) * X_HEAD_DIM) for h in range(N_X_HEADS)]
        do_h = [do[:, cols].astype(BF16) for cols in head_cols]
        dps = [_dot_nt(do_h[h], heads[h][4]) for h in range(N_X_HEADS)]
        dss = []
        for (_, _, _, _, _, p), dp in zip(heads, dps):
            dss.append((p.astype(BF16), (p * (dp - jnp.sum(p * dp, axis=-1, keepdims=True))).astype(BF16)))
        for h, ((qh, rq, qn, kn, _, _), (p, ds)) in enumerate(zip(heads, dss)):
            cols = head_cols[h]
            vcols = slice(d + h * X_HEAD_DIM, d + (h + 1) * X_HEAD_DIM)
            dkv_ref[:, vcols] += _dot_tn(p, do_h[h])
            dqn = _dot_nn(ds, kn) * (X_HEAD_DIM ** -0.5)
            dkv_ref[:, cols] += _dot_tn(ds, qn) * (X_HEAD_DIM ** -0.5)
            dqg_acc = dqg_acc + jnp.sum(dqn * qh, axis=0, keepdims=True)
            dq_ref[:, cols] = _rms_bwd(dqn, qh, rq, qg_v).astype(dq_ref.dtype)
        dqg_ref[...] += dqg_acc
        dh1 = _dot_nt(dq_ref[...], wq_ref[...])
        xh, r = _rms(x1_ref[...])
        dg_ref[...] += jnp.sum(dh1 * xh, axis=0, keepdims=True)
        dx1 = dx2 + _rms_bwd(dh1, xh, r, g_x_ref[...])
        dx1_ref[...] = dx1
        dmixed_ref[...] = _dot_nt(dx1, w_out_ref[...])

        @pl.when(i == nt - 1)
        def _():
            dkg_acc = jnp.zeros((1, X_HEAD_DIM), F32)
            for h in range(N_X_HEADS):
                cols = slice(h * X_HEAD_DIM, (h + 1) * X_HEAD_DIM)
                kh, rk = _rms(kv_ref[:, cols])
                dkn = dkv_ref[:, cols]
                dkg_acc = dkg_acc + jnp.sum(dkn * kh, axis=0, keepdims=True)
                dkv_ref[:, cols] = _rms_bwd(dkn, kh, rk, kg_v)
            dkg_ref[...] = dkg_acc

    sq = _whole((d, d))
    full = pl.BlockSpec((n_mem, 2 * d), lambda i: (0, 0))
    return pl.pallas_call(
        body, name="mid_bwd", grid=(nt,),
        in_specs=[_rows(tm, d), _rows(tm, d), _whole((n_mem, 2 * d)), _vec(X_HEAD_DIM), _vec(X_HEAD_DIM), _rows(tm, d),
                  _vec(d), sq, sq, sq],
        out_specs=[_rows(tm, d), _rows(tm, d), _rows(tm, d), full, _vec(X_HEAD_DIM), _vec(X_HEAD_DIM), _vec(d)],
        out_shape=[jax.ShapeDtypeStruct((m, d), BF16), jax.ShapeDtypeStruct((m, d), F32), jax.ShapeDtypeStruct((m, d), F32),
                   jax.ShapeDtypeStruct((n_mem, 2 * d), F32), jax.ShapeDtypeStruct((1, X_HEAD_DIM), F32),
                   jax.ShapeDtypeStruct((1, X_HEAD_DIM), F32), jax.ShapeDtypeStruct((1, d), F32)],
        compiler_params=_cp(),
    )(dx2, qx, kv, xqg.reshape(1, X_HEAD_DIM), xkg.reshape(1, X_HEAD_DIM), x1, g_x.reshape(1, d), wo, wq, w_out)


def in_bwd(du, wt_in, x0, g_mix, dx1, *, tm=512):
    m, d = x0.shape
    n = wt_in.shape[0]

    def body(du_ref, wt_ref, x0_ref, g_ref, dx1_ref, dx0_ref, dg_ref):
        _zero_at_first_step(dg_ref)
        dh0 = _dot_nn(du_ref[...], wt_ref[...])
        xh, r = _rms(x0_ref[...])
        dg_ref[...] += jnp.sum(dh0 * xh, axis=0, keepdims=True)
        dx0_ref[...] = dx1_ref[...] + _rms_bwd(dh0, xh, r, g_ref[...])

    return pl.pallas_call(
        body, name="in_bwd", grid=(m // tm,),
        in_specs=[_rows(tm, n), _whole((n, d)), _rows(tm, d), _vec(d), _rows(tm, d)],
        out_specs=[_rows(tm, d), _vec(d)],
        out_shape=[jax.ShapeDtypeStruct((m, d), F32), jax.ShapeDtypeStruct((1, d), F32)],
        compiler_params=_cp(),
    )(du, wt_in, x0, g_mix.reshape(1, d), dx1)


SWA_TILE = 512
SWA_SUB = SWA_TILE // BLOCK
SWA_KEYS = SWA_TILE + BLOCK
PAIR = 2 * HEAD_DIM
KCOL = ATTN_WIDTH
VCOL = ATTN_WIDTH + KV_WIDTH


def _swa_constants():
    r = np.arange(2 * BLOCK)[:, None]
    j = np.arange(4 * BLOCK)[None, :]
    dist = (r % BLOCK) + BLOCK - (j % (2 * BLOCK))
    valid = (dist >= 0) & (dist < BLOCK)
    first_valid = valid & ((j % (2 * BLOCK)) >= BLOCK)
    bias, bias_first = [], []
    for kv in range(N_KV_HEADS):
        head = kv * GROUP + 2 * (r // BLOCK) + j // (2 * BLOCK)
        b = -(2.0 ** -(head + 1.0)) * dist
        bias.append(np.where(valid, b, NEG))
        bias_first.append(np.where(first_valid, b, NEG))
    lane = np.arange(LANES)
    seg = (lane[:, None] // HEAD_DIM == lane[None, :] // HEAD_DIM) / HEAD_DIM
    row = np.arange(4 * BLOCK)[:, None]
    ones = (row // (2 * BLOCK)) == (lane[None, :] // HEAD_DIM)
    return (jnp.asarray(np.stack(bias), F32), jnp.asarray(np.stack(bias_first), F32), jnp.asarray(seg, BF16),
            jnp.asarray(ones, BF16))


def _segmean(x, seg_ref):
    hi = x.astype(BF16)
    lo = (x - hi.astype(F32)).astype(BF16)
    return _dot_nn(hi, seg_ref[...]) + _dot_nn(lo, seg_ref[...])


def _two_heads(x, kv):
    lane = lax.broadcasted_iota(jnp.int32, (1, LANES), 1)
    mine = (lane < HEAD_DIM) if kv == 0 else (lane >= HEAD_DIM)
    base = jnp.where(mine, x, 0.0)
    other = pltpu.roll(base, HEAD_DIM, 1)
    return jnp.concatenate([base, other] if kv == 0 else [other, base], axis=0)


def _from_two_heads(y, kv):
    rows = y.shape[0] // 2
    lane = lax.broadcasted_iota(jnp.int32, (1, LANES), 1)
    top, bot = y[:rows], y[rows:]
    if kv == 0:
        return jnp.where(lane < HEAD_DIM, top + pltpu.roll(bot, HEAD_DIM, 1), 0.0)
    return jnp.where(lane >= HEAD_DIM, pltpu.roll(top, HEAD_DIM, 1) + bot, 0.0)


def _pair_rows(ref, rows, kv):
    c = kv * 2 * PAIR
    return jnp.concatenate([ref[rows, c:c + PAIR], ref[rows, c + PAIR:c + 2 * PAIR]], axis=0)


def _head_cols(fn, kv):
    return [jnp.concatenate([fn(kv * GROUP + half), fn(kv * GROUP + 2 + half)], axis=0) for half in range(2)]


def _swa_prologue(cur_ref, prev_ref, qg_ref, kg_ref, seg_ref, qg_s, kn_s, v_s):
    qg_s[...] = (cur_ref[:, 0:ATTN_WIDTH] * qg_ref[...]).astype(BF16)
    k = jnp.concatenate([prev_ref[:, KCOL:KCOL + KV_WIDTH], cur_ref[:, KCOL:KCOL + KV_WIDTH]], axis=0)
    kn_s[...] = k * lax.rsqrt(_segmean(k * k, seg_ref) + EPS) * kg_ref[...]
    v_s[0:BLOCK, :] = prev_ref[:, VCOL:VCOL + KV_WIDTH]
    v_s[BLOCK:SWA_KEYS, :] = cur_ref[:, VCOL:VCOL + KV_WIDTH]


def _swa_products(qg_s, kn_s, rows, keys, kv):
    q2 = _pair_rows(qg_s, rows, kv)
    k2 = _two_heads(kn_s[keys, :], kv)
    return q2, k2, _dot_nt(q2, k2)


def _swa_scores(cur_ref, sinks_ref, qg_s, kn_s, bias, rows, keys, kv):
    q2, k2, t = _swa_products(qg_s, kn_s, rows, keys, kv)
    return q2, k2, t, _swa_softmax(cur_ref, sinks_ref, t, bias, rows, kv)


def _swa_softmax(cur_ref, sinks_ref, t, bias, rows, kv):
    def rq(h):
        x = cur_ref[rows, h * HEAD_DIM:(h + 1) * HEAD_DIM]
        return lax.rsqrt(jnp.mean(x * x, axis=-1, keepdims=True) + EPS)

    scale = _head_cols(lambda h: rq(h) * (HEAD_DIM ** -0.5), kv)
    sink = _head_cols(lambda h: jnp.full((BLOCK, 1), sinks_ref[h], F32), kv)
    halves = []
    for half in range(2):
        cols = slice(half * 2 * BLOCK, (half + 1) * 2 * BLOCK)
        s = t[:, cols] * scale[half] + bias[:, cols]
        mx = jnp.maximum(jnp.max(s, axis=-1, keepdims=True), sink[half])
        halves.append((scale[half], jnp.exp(s - mx), jnp.exp(sink[half] - mx)))
    return halves


def swa_fwd(u, qg, kg, sinks):
    t_rows = u.shape[0]
    nt = t_rows // SWA_TILE
    bias_c, bias_first_c, seg_c, ones_c = _swa_constants()

    def body(sinks_ref, cur_ref, prev_ref, qg_ref, kg_ref, seg_ref, bias_ref, biasf_ref, ones_ref, o_ref, qg_s, kn_s, v_s):
        i = pl.program_id(0)
        _swa_prologue(cur_ref, prev_ref, qg_ref, kg_ref, seg_ref, qg_s, kn_s, v_s)
        lane = lax.broadcasted_iota(jnp.int32, (1, LANES), 1)
        work = [(b, kv, slice(b * BLOCK, (b + 1) * BLOCK), slice(b * BLOCK, (b + 2) * BLOCK))
                for b in range(SWA_SUB) for kv in range(N_KV_HEADS)]
        products = [_swa_products(qg_s, kn_s, rows, keys, kv)[2] for _, kv, rows, keys in work]
        scored = []
        for (b, kv, rows, _), t in zip(work, products):
            bias = jnp.where(i == 0, biasf_ref[kv], bias_ref[kv]) if b == 0 else bias_ref[kv]
            halves = _swa_softmax(cur_ref, sinks_ref, t, bias, rows, kv)
            scored.append((jnp.concatenate([halves[0][1], halves[1][1]], axis=1).astype(BF16), halves[0][2], halves[1][2]))
        for (b, kv, rows, keys), (e, es0, es1) in zip(work, scored):
            v2 = jnp.concatenate([_two_heads(v_s[keys, :], kv).astype(BF16), ones_ref[...]], axis=1)
            ox = _dot_nn(e, v2)
            den = ox[:, LANES:] + jnp.where(lane < HEAD_DIM, es0, es1)
            out = (ox[:, :LANES] / den).astype(o_ref.dtype)
            c = kv * 2 * PAIR
            o_ref[rows, c:c + PAIR] = out[:BLOCK]
            o_ref[rows, c + PAIR:c + 2 * PAIR] = out[BLOCK:]

    const3 = pl.BlockSpec((N_KV_HEADS, 2 * BLOCK, 4 * BLOCK), lambda i: (0, 0, 0))
    return pl.pallas_call(
        body, name="swa_fwd", grid=(nt,),
        in_specs=[
            pl.BlockSpec(memory_space=pltpu.SMEM),
            pl.BlockSpec((SWA_TILE, QKV_WIDTH), lambda i: (i, 0)),
            pl.BlockSpec((BLOCK, QKV_WIDTH), lambda i: (jnp.maximum(i * SWA_SUB - 1, 0), 0)),
            pl.BlockSpec((1, ATTN_WIDTH), lambda i: (0, 0)), pl.BlockSpec((1, KV_WIDTH), lambda i: (0, 0)),
            pl.BlockSpec((LANES, LANES), lambda i: (0, 0)), const3, const3,
            pl.BlockSpec((4 * BLOCK, LANES), lambda i: (0, 0)),
        ],
        out_specs=pl.BlockSpec((SWA_TILE, ATTN_WIDTH), lambda i: (i, 0)),
        out_shape=jax.ShapeDtypeStruct((t_rows, 2 * ATTN_WIDTH), BF16),
        scratch_shapes=[pltpu.VMEM((SWA_TILE, ATTN_WIDTH), BF16), pltpu.VMEM((SWA_KEYS, KV_WIDTH), F32),
                        pltpu.VMEM((SWA_KEYS, KV_WIDTH), F32)],
        compiler_params=_cp(),
    )(sinks, u, u, jnp.tile(qg, N_Q_HEADS).reshape(1, ATTN_WIDTH), jnp.tile(kg, N_KV_HEADS).reshape(1, KV_WIDTH),
      seg_c, bias_c, bias_first_c, ones_c)


def swa_bwd(u, dmixed, qg, kg, sinks):
    t_rows = u.shape[0]
    nt = t_rows // SWA_TILE
    bias_c, bias_first_c, seg_c, _ = _swa_constants()

    def body(sinks_ref, cur_ref, prev_ref, do_ref, qg_ref, kg_ref, seg_ref, bias_ref, biasf_ref,
             du_ref, dqg_ref, dkg_ref, dsk_ref, qg_s, kn_s, v_s, acck_s, accv_s, carryk_s, carryv_s):
        step = pl.program_id(0)
        i = nt - 1 - step

        @pl.when(step == 0)
        def _():
            for r in (carryk_s, carryv_s, dqg_ref, dkg_ref, dsk_ref):
                r[...] = jnp.zeros_like(r)

        _swa_prologue(cur_ref, prev_ref, qg_ref, kg_ref, seg_ref, qg_s, kn_s, v_s)
        for acc, carry in ((acck_s, carryk_s), (accv_s, carryv_s)):
            acc[0:SWA_TILE, :] = jnp.zeros((SWA_TILE, KV_WIDTH), F32)
            acc[SWA_TILE:SWA_KEYS, :] = carry[...]

        lane = lax.broadcasted_iota(jnp.int32, (1, LANES), 1)
        g_pair = qg_ref[:, 0:PAIR]
        dqg_acc = jnp.zeros((1, PAIR), F32)
        dsk_acc = jnp.zeros((1, LANES), F32)
        work = [(b, kv, slice(b * BLOCK, (b + 1) * BLOCK), slice(b * BLOCK, (b + 2) * BLOCK))
                for b in range(SWA_SUB) for kv in range(N_KV_HEADS)]
        products = []
        for _, kv, rows, keys in work:
            q2, k2, t = _swa_products(qg_s, kn_s, rows, keys, kv)
            do2 = _pair_rows(do_ref, rows, kv).astype(BF16)
            products.append((q2, k2, t, do2, _dot_nt(do2, _two_heads(v_s[keys, :], kv))))
        exps = []
        for (b, kv, rows, _), (_, _, t, _, _) in zip(work, products):
            bias = jnp.where(i == 0, biasf_ref[kv], bias_ref[kv]) if b == 0 else bias_ref[kv]
            exps.append(_swa_softmax(cur_ref, sinks_ref, t, bias, rows, kv))
        softmaxed = []
        for (b, kv, rows, _), (_, _, t, _, dp), halves in zip(work, products, exps):
            p_parts, dt_parts, coef = [], [], []
            for half, (scale, e, es) in enumerate(halves):
                cols = slice(half * 2 * BLOCK, (half + 1) * 2 * BLOCK)
                rden = 1.0 / (jnp.sum(e, axis=-1, keepdims=True) + es)
                p = e * rden
                dp_h = dp[:, cols]
                delta = jnp.sum(p * dp_h, axis=-1, keepdims=True)
                ds = p * (dp_h - delta)
                dsink = -(es * rden) * delta
                for pair in range(2):
                    part = jnp.sum(dsink[pair * BLOCK:(pair + 1) * BLOCK], axis=0, keepdims=True)
                    dsk_acc = dsk_acc + jnp.where(lane == kv * GROUP + 2 * pair + half, part, 0.0)
                dscale = jnp.sum(ds * t[:, cols], axis=-1, keepdims=True)
                coef.append(-dscale * scale * scale * scale)
                p_parts.append(p.astype(BF16))
                dt_parts.append((ds * scale).astype(BF16))
            softmaxed.append((jnp.concatenate(p_parts, axis=1), jnp.concatenate(dt_parts, axis=1),
                              jnp.where(lane < HEAD_DIM, coef[0], coef[1])))
        for (_, kv, rows, keys), (q2, k2, _, do2, _), (p2, dt, coef) in zip(work, products, softmaxed):
            dqg2 = _dot_nn(dt, k2)
            q_raw = _pair_rows(cur_ref, rows, kv)
            dq = dqg2 * g_pair + coef * q_raw
            dqg_acc = dqg_acc + jnp.sum(dqg2 * q_raw, axis=0, keepdims=True)
            c = kv * 2 * PAIR
            du_ref[rows, c:c + PAIR] = dq[:BLOCK].astype(du_ref.dtype)
            du_ref[rows, c + PAIR:c + 2 * PAIR] = dq[BLOCK:].astype(du_ref.dtype)
        to_keys = [(_from_two_heads(_dot_tn(dt, q2), kv), _from_two_heads(_dot_tn(p2, do2), kv))
                   for (_, kv, _, _), (q2, _, _, do2, _), (p2, dt, _) in zip(work, products, softmaxed)]
        for (_, _, _, keys), (dk, dv) in zip(work, to_keys):
            acck_s[keys, :] += dk
            accv_s[keys, :] += dv
        dqg_ref[...] += dqg_acc + pltpu.roll(dqg_acc, HEAD_DIM, 1)
        dsk_ref[...] += dsk_acc

        own = slice(BLOCK, SWA_KEYS)
        k = cur_ref[:, KCOL:KCOL + KV_WIDTH]
        rk = lax.rsqrt(_segmean(k * k, seg_ref) + EPS)
        kh = k * rk
        dkn = acck_s[own, :]
        dkh = dkn * kg_ref[...]
        du_ref[:, KCOL:KCOL + KV_WIDTH] = (rk * (dkh - kh * _segmean(dkh * kh, seg_ref))).astype(du_ref.dtype)
        du_ref[:, VCOL:VCOL + KV_WIDTH] = accv_s[own, :].astype(du_ref.dtype)
        dkg_part = jnp.sum(dkn * kh, axis=0, keepdims=True)
        dkg_ref[...] += dkg_part + pltpu.roll(dkg_part, HEAD_DIM, 1)
        carryk_s[...] = acck_s[0:BLOCK, :]
        carryv_s[...] = accv_s[0:BLOCK, :]

    const3 = pl.BlockSpec((N_KV_HEADS, 2 * BLOCK, 4 * BLOCK), lambda s: (0, 0, 0))
    vec = pl.BlockSpec((1, LANES), lambda s: (0, 0))
    return pl.pallas_call(
        body, name="swa_bwd", grid=(nt,),
        in_specs=[
            pl.BlockSpec(memory_space=pltpu.SMEM),
            pl.BlockSpec((SWA_TILE, QKV_WIDTH), lambda s: (nt - 1 - s, 0)),
            pl.BlockSpec((BLOCK, QKV_WIDTH), lambda s: (jnp.maximum((nt - 1 - s) * SWA_SUB - 1, 0), 0)),
            pl.BlockSpec((SWA_TILE, ATTN_WIDTH), lambda s: (nt - 1 - s, 0)),
            pl.BlockSpec((1, ATTN_WIDTH), lambda s: (0, 0)), vec,
            pl.BlockSpec((LANES, LANES), lambda s: (0, 0)), const3, const3,
        ],
        out_specs=[pl.BlockSpec((SWA_TILE, QKV_WIDTH), lambda s: (nt - 1 - s, 0)), vec, vec, vec],
        out_shape=[jax.ShapeDtypeStruct((t_rows, IN_COLS), BF16)] + [jax.ShapeDtypeStruct((1, LANES), F32)] * 3,
        scratch_shapes=[pltpu.VMEM((SWA_TILE, ATTN_WIDTH), BF16)] + [pltpu.VMEM((SWA_KEYS, KV_WIDTH), F32)] * 4
        + [pltpu.VMEM((BLOCK, KV_WIDTH), F32)] * 2,
        compiler_params=_cp(),
    )(sinks, u, u, dmixed, jnp.tile(qg, N_Q_HEADS).reshape(1, ATTN_WIDTH), jnp.tile(kg, N_KV_HEADS).reshape(1, KV_WIDTH),
      seg_c, bias_c, bias_first_c)


CONV_TILE = 512
CONV_CHUNK = 64
VAL0 = QKV_WIDTH
GATE0 = QKV_WIDTH + CONV_CH


def _glu(ref):
    return ref[:, VAL0:GATE0] * _sigmoid(ref[:, GATE0:GATE0 + CONV_CH])


SUBLANES = 8
CONV_BUF = CONV_HALO + CONV_TILE + SUBLANES
CONV_EXT = CONV_HALO + CONV_TILE


def _fill_shifted(sh_ref):
    for r in range(1, SUBLANES):
        sh_ref[r, 0:CONV_EXT, :] = sh_ref[0, pl.ds(r, CONV_EXT), :]


def _shifted(sh_ref, start, offset, n):
    return sh_ref[offset % SUBLANES, pl.ds(start + offset - offset % SUBLANES, n), :]


def _layernorm_stats(y):
    mu = jnp.mean(y, axis=-1, keepdims=True)
    yc = y - mu
    rstd = lax.rsqrt(jnp.mean(yc * yc, axis=-1, keepdims=True) + EPS)
    return yc * rstd, rstd


def conv_fwd(u, mixed, conv_w, conv_b, ln_g, ln_b):
    t = u.shape[0]
    nt = t // CONV_TILE
    per = CONV_TILE // CONV_HALO

    def body(cur_ref, prev_ref, mixed_ref, w_ref, b_ref, g_ref, b2_ref, o_ref, y_ref, gl_ref):
        del mixed_ref
        i = pl.program_id(0)
        gl_ref[0, 0:CONV_HALO, :] = jnp.where(i > 0, _glu(prev_ref), 0.0)
        gl_ref[0, CONV_HALO:CONV_EXT, :] = _glu(cur_ref)
        gl_ref[0, CONV_EXT:CONV_BUF, :] = jnp.zeros((SUBLANES, CONV_CH), F32)
        _fill_shifted(gl_ref)
        for c0 in range(0, CONV_TILE, CONV_CHUNK):
            acc = jnp.broadcast_to(b_ref[...], (CONV_CHUNK, CONV_CH))
            for k in range(CONV_K):
                acc = acc + w_ref[k:k + 1, :] * _shifted(gl_ref, c0, 2 + k, CONV_CHUNK)
            y_ref[c0:c0 + CONV_CHUNK, :] = acc
        yh, _ = _layernorm_stats(y_ref[...])
        yln = yh * g_ref[...] + b2_ref[...]
        o_ref[...] = (yln * _sigmoid(yln)).astype(o_ref.dtype)

    vec = pl.BlockSpec((1, CONV_CH), lambda i: (0, 0))
    return pl.pallas_call(
        body, name="conv_fwd", grid=(nt,),
        in_specs=[
            pl.BlockSpec((CONV_TILE, IN_COLS), lambda i: (i, 0)),
            pl.BlockSpec((CONV_HALO, IN_COLS), lambda i: (jnp.maximum(i * per - 1, 0), 0)),
            pl.BlockSpec(memory_space=pl.ANY),
            pl.BlockSpec((CONV_HALO, CONV_CH), lambda i: (0, 0)),
            vec, vec, vec,
        ],
        out_specs=[pl.BlockSpec((CONV_TILE, CONV_CH), lambda i: (i, 1)), pl.BlockSpec((CONV_TILE, CONV_CH), lambda i: (i, 0))],
        out_shape=[jax.ShapeDtypeStruct(mixed.shape, mixed.dtype), jax.ShapeDtypeStruct((t, CONV_CH), F32)],
        scratch_shapes=[pltpu.VMEM((SUBLANES, CONV_BUF, CONV_CH), F32)],
        input_output_aliases={2: 0}, compiler_params=_cp(),
    )(u, u, mixed, conv_w, conv_b.reshape(1, CONV_CH), ln_g.reshape(1, CONV_CH), ln_b.reshape(1, CONV_CH))


def conv_bwd(u, y, dmixed, du, conv_w, ln_g, ln_b):
    t = u.shape[0]
    nt = t // CONV_TILE
    per = CONV_TILE // CONV_HALO

    def body(cur_ref, prev_ref, y_ref, yn_ref, do_ref, don_ref, du_in_ref, w_ref, g_ref, b2_ref,
             du_ref, dw_ref, dvec_ref, gl_ref, dy_ref):
        i = pl.program_id(0)
        last = i == nt - 1
        _zero_at_first_step(dw_ref, dvec_ref)

        gl_ref[0, 0:CONV_HALO, :] = jnp.where(i > 0, _glu(prev_ref), 0.0)
        gl_ref[0, CONV_HALO:CONV_EXT, :] = _glu(cur_ref)
        gl_ref[0, CONV_EXT:CONV_BUF, :] = jnp.zeros((SUBLANES, CONV_CH), F32)
        _fill_shifted(gl_ref)

        yh, rstd = _layernorm_stats(jnp.concatenate([y_ref[...], yn_ref[...]], axis=0))
        g = g_ref[...]
        yln = yh * g + b2_ref[...]
        sg = _sigmoid(yln)
        dout = jnp.concatenate([do_ref[...], jnp.where(last, 0.0, don_ref[...])], axis=0)
        dyln = dout * (sg * (1.0 + yln * (1.0 - sg)))
        dyh = dyln * g
        dy = rstd * (dyh - jnp.mean(dyh, axis=-1, keepdims=True) - yh * jnp.mean(dyh * yh, axis=-1, keepdims=True))
        dy_ref[0, 0:CONV_EXT, :] = dy
        dy_ref[0, CONV_EXT:CONV_BUF, :] = jnp.zeros((SUBLANES, CONV_CH), F32)
        _fill_shifted(dy_ref)

        own = slice(0, CONV_TILE)
        dvec_ref[0:1, :] += jnp.sum(dy[own], axis=0, keepdims=True)
        dvec_ref[1:2, :] += jnp.sum(dyln[own] * yh[own], axis=0, keepdims=True)
        dvec_ref[2:3, :] += jnp.sum(dyln[own], axis=0, keepdims=True)
        for k in range(CONV_K):
            dw_ref[k:k + 1, :] += jnp.sum(dy[own] * _shifted(gl_ref, 0, 2 + k, CONV_TILE), axis=0, keepdims=True)

        for c0 in range(0, CONV_TILE, CONV_CHUNK):
            acc = jnp.zeros((CONV_CHUNK, CONV_CH), F32)
            for k in range(CONV_K):
                acc = acc + w_ref[k:k + 1, :] * _shifted(dy_ref, c0, CONV_K - 1 - k, CONV_CHUNK)
            rows = slice(c0, c0 + CONV_CHUNK)
            val = cur_ref[rows, VAL0:GATE0]
            sgate = _sigmoid(cur_ref[rows, GATE0:GATE0 + CONV_CH])
            du_ref[rows, VAL0:GATE0] = (acc * sgate).astype(du_ref.dtype)
            du_ref[rows, GATE0:GATE0 + CONV_CH] = (acc * val * sgate * (1.0 - sgate)).astype(du_ref.dtype)
        du_ref[:, 0:QKV_WIDTH] = du_in_ref[:, 0:QKV_WIDTH]

    vec = pl.BlockSpec((1, CONV_CH), lambda i: (0, 0))
    n_halo = t // CONV_HALO
    return pl.pallas_call(
        body, name="conv_bwd", grid=(nt,),
        in_specs=[
            pl.BlockSpec((CONV_TILE, IN_COLS), lambda i: (i, 0)),
            pl.BlockSpec((CONV_HALO, IN_COLS), lambda i: (jnp.maximum(i * per - 1, 0), 0)),
            pl.BlockSpec((CONV_TILE, CONV_CH), lambda i: (i, 0)),
            pl.BlockSpec((CONV_HALO, CONV_CH), lambda i: (jnp.minimum((i + 1) * per, n_halo - 1), 0)),
            pl.BlockSpec((CONV_TILE, CONV_CH), lambda i: (i, 1)),
            pl.BlockSpec((CONV_HALO, CONV_CH), lambda i: (jnp.minimum((i + 1) * per, n_halo - 1), 1)),
            pl.BlockSpec((CONV_TILE, IN_COLS), lambda i: (i, 0)),
            pl.BlockSpec((CONV_HALO, CONV_CH), lambda i: (0, 0)),
            vec, vec,
        ],
        out_specs=[
            pl.BlockSpec((CONV_TILE, IN_COLS), lambda i: (i, 0)),
            pl.BlockSpec((CONV_HALO, CONV_CH), lambda i: (0, 0)),
            pl.BlockSpec((8, CONV_CH), lambda i: (0, 0)),
        ],
        out_shape=[
            jax.ShapeDtypeStruct(du.shape, du.dtype),
            jax.ShapeDtypeStruct((CONV_HALO, CONV_CH), F32),
            jax.ShapeDtypeStruct((8, CONV_CH), F32),
        ],
        scratch_shapes=[pltpu.VMEM((SUBLANES, CONV_BUF, CONV_CH), F32), pltpu.VMEM((SUBLANES, CONV_BUF, CONV_CH), F32)],
        input_output_aliases={6: 0}, compiler_params=_cp(),
    )(u, u, y, y, dmixed, dmixed, du, conv_w, ln_g.reshape(1, CONV_CH), ln_b.reshape(1, CONV_CH))


def adamw(w, g, m, v, *, name):
    r, c = w.shape
    tr = r
    for cand in (512, 256, 128, 64, 32, 16, 8):
        if r % cand == 0 and r > cand:
            tr = cand
            break

    def body(w_ref, g_ref, m_ref, v_ref, d_ref, nm_ref, nv_ref):
        d_ref[...], nm_ref[...], nv_ref[...] = _adamw_math(w_ref[...], g_ref[...], m_ref[...], v_ref[...])

    spec = pl.BlockSpec((tr, c), lambda i: (i, 0))
    shape = jax.ShapeDtypeStruct((r, c), F32)
    return pl.pallas_call(
        body, name=name, grid=(r // tr,), in_specs=[spec] * 4, out_specs=[spec] * 3,
        out_shape=[shape] * 3, compiler_params=_cp(),
    )(w, g, m, v)


def _position():
    return lax.axis_index("x"), lax.axis_index("y"), lax.axis_index("c")


def all_gather_many(shards, *, name):
    n = len(shards)

    def body(*refs):
        x_refs, out_refs, token_ref = refs[:n], refs[n:2 * n], refs[2 * n]
        send_sems, recv_sems, local_sems = refs[2 * n + 1:]
        x, y, c = _position()
        me, sibling = (x, y, c), (x, y, 1 - c)
        chips = [(1 - x, y), (x, 1 - y), (1 - x, 1 - y)]
        token_ref[...] = jnp.zeros_like(token_ref)

        def rows(t, px, py, pc):
            return out_refs[t].at[4 * px + 2 * py + pc]

        def copy(t, k, block, to, src=None):
            return pltpu.make_async_remote_copy(
                src_ref=rows(t, *block) if src is None else src, dst_ref=rows(t, *block),
                send_sem=send_sems.at[7 * t + k], recv_sem=recv_sems.at[7 * t + k], device_id=to, device_id_type=MESH)

        mine = [pltpu.make_async_copy(x_refs[t], rows(t, *me), local_sems.at[t]) for t in range(n)]
        for cp in mine:
            cp.start()
        first = []
        for t in range(n):
            first.append(copy(t, 0, me, sibling, src=x_refs[t]))
            first += [copy(t, 1 + j, me, (*chip, c), src=x_refs[t]) for j, chip in enumerate(chips)]
        for cp in first:
            cp.start()
        passed = []
        for t in range(n):
            for j, chip in enumerate(chips):
                copy(t, 1 + j, (*chip, c), me).wait_recv()
                passed.append(copy(t, 4 + j, (*chip, c), sibling))
                passed[-1].start()
        for t in range(n):
            copy(t, 0, sibling, me).wait_recv()
            for j, chip in enumerate(chips):
                copy(t, 4 + j, (*chip, 1 - c), me).wait_recv()
        for cp in first + passed:
            cp.wait_send()
        for cp in mine:
            cp.wait()

    hbm = pl.BlockSpec(memory_space=pltpu.HBM)
    out = pl.pallas_call(
        body, name=name,
        out_shape=[jax.ShapeDtypeStruct((N_DEV,) + s.shape, s.dtype) for s in shards] + [jax.ShapeDtypeStruct((8, LANES), F32)],
        in_specs=[hbm] * n, out_specs=[hbm] * n + [pl.BlockSpec(memory_space=pltpu.VMEM)],
        scratch_shapes=[pltpu.SemaphoreType.DMA((7 * n,)), pltpu.SemaphoreType.DMA((7 * n,)), pltpu.SemaphoreType.DMA((n,))],
        compiler_params=_cp(),
    )(*shards)
    return out[:n], out[n]


_HBM = pl.BlockSpec(memory_space=pltpu.HBM)
_SEM = pl.BlockSpec(memory_space=pltpu.SEMAPHORE)
_EFFECT = pltpu.SideEffectType.DATAFLOW_SIDE_EFFECTING


def _split_copies(src_refs, land_refs, send_sems, recv_sems, plan, n_copies):
    copies = []
    for t, (src_ref, land_ref) in enumerate(zip(src_refs, land_refs)):
        for k in range(n_copies):
            s, d, to = plan(src_ref, land_ref, k)
            copies.append(pltpu.make_async_remote_copy(
                src_ref=s, dst_ref=d, send_sem=send_sems.at[n_copies * t + k], recv_sem=recv_sems.at[n_copies * t + k],
                device_id=to, device_id_type=MESH))
    return copies


def split_start(srcs, lands, plan, n_copies, *, name):
    n = len(srcs)

    def body(*refs):
        src_refs, land_refs, send_sems, recv_sems, token = refs[:n], refs[n:2 * n], refs[2 * n], refs[2 * n + 1], refs[-1]
        for cp in _split_copies(src_refs, land_refs, send_sems, recv_sems, plan, n_copies):
            cp.start()
        token[...] = jnp.zeros_like(token)

    both = list(srcs) + list(lands)
    out = pl.pallas_call(
        body, name=name,
        out_shape=(pltpu.SemaphoreType.DMA((n_copies * n,)), pltpu.SemaphoreType.DMA((n_copies * n,)),
                   *[pltpu.HBM(a.shape, a.dtype) for a in both], jax.ShapeDtypeStruct((8, LANES), F32)),
        in_specs=(_HBM,) * (2 * n), out_specs=(_SEM, _SEM) + (_HBM,) * (2 * n) + (pl.BlockSpec(memory_space=pltpu.VMEM),),
        input_output_aliases={i: 2 + i for i in range(2 * n)},
        compiler_params=pltpu.CompilerParams(has_side_effects=_EFFECT),
    )(*[pltpu.with_memory_space_constraint(a, pltpu.HBM) for a in both])
    return out[0], out[1], list(out[2:2 + n]), list(out[2 + n:2 + 2 * n]), out[-1]


def split_wait(started, after, plan, n_copies, *, name):
    send_sems, recv_sems, srcs, lands, _ = started
    n = len(srcs)

    def body(*refs):
        src_refs, land_refs, send_sems, recv_sems = refs[:n], refs[n:2 * n], refs[2 * n], refs[2 * n + 1]
        for cp in _split_copies(src_refs, land_refs, send_sems, recv_sems, plan, n_copies):
            cp.wait_send()
            cp.wait_recv()

    both = list(srcs) + list(lands)
    out = pl.pallas_call(
        body, name=name,
        out_shape=tuple(pltpu.HBM(a.shape, a.dtype) for a in both),
        in_specs=(_HBM,) * (2 * n) + (_SEM, _SEM, pl.BlockSpec(memory_space=pl.ANY)), out_specs=(_HBM,) * (2 * n),
        input_output_aliases={i: i for i in range(2 * n)},
        compiler_params=pltpu.CompilerParams(has_side_effects=_EFFECT),
    )(*both, send_sems, recv_sems, after)
    return list(out[:n]), list(out[n:])


def _other_chips(x, y):
    return [(1 - x, y), (x, 1 - y), (1 - x, 1 - y)]


def _remote(src, dst, send_sem, recv_sem, to):
    return pltpu.make_async_remote_copy(src_ref=src, dst_ref=dst, send_sem=send_sem, recv_sem=recv_sem,
                                        device_id=to, device_id_type=MESH)


def gather_start(groups, *, name):
    counts = [len(shards) for shards, _ in groups]
    flat = [a for shards, _ in groups for a in shards] + [a for _, lands in groups for a in lands]
    n_all, n_groups = sum(counts), len(groups)

    def body(*refs):
        s_refs, l_refs = refs[:n_all], refs[n_all:2 * n_all]
        sems = refs[2 * n_all:2 * n_all + 3 * n_groups]
        x, y, c = _position()
        me = 4 * x + 2 * y + c
        at = 0
        for gi, n in enumerate(counts):
            send, recv_sibling, recv_ici = sems[3 * gi:3 * gi + 3]
            for t in range(n):
                src, dst = s_refs[at + t], l_refs[at + t].at[me]
                _remote(src, dst, send.at[4 * t], recv_sibling.at[t], (x, y, 1 - c)).start()
                for j, chip in enumerate(_other_chips(x, y)):
                    _remote(src, dst, send.at[4 * t + 1 + j], recv_ici.at[3 * t + j], (*chip, c)).start()
            at += n
        refs[-1][...] = jnp.zeros_like(refs[-1])

    sem_shapes = [pltpu.SemaphoreType.DMA((k * n,)) for n in counts for k in (4, 1, 3)]
    out = pl.pallas_call(
        body, name=name,
        out_shape=(*sem_shapes, *[pltpu.HBM(a.shape, a.dtype) for a in flat], jax.ShapeDtypeStruct((8, LANES), F32)),
        in_specs=(_HBM,) * (2 * n_all),
        out_specs=(_SEM,) * (3 * n_groups) + (_HBM,) * (2 * n_all) + (pl.BlockSpec(memory_space=pltpu.VMEM),),
        input_output_aliases={i: 3 * n_groups + i for i in range(2 * n_all)},
        compiler_params=pltpu.CompilerParams(has_side_effects=_EFFECT),
    )(*[pltpu.with_memory_space_constraint(a, pltpu.HBM) for a in flat])
    thru = out[3 * n_groups:-1]
    states, at = [], 0
    for gi, n in enumerate(counts):
        states.append(dict(shards=list(thru[at:at + n]), lands=list(thru[n_all + at:n_all + at + n]),
                           send=out[3 * gi], recv_sibling=out[3 * gi + 1], recv_ici=out[3 * gi + 2]))
        at += n
    return states, out[-1]


def gather_forward(states, after, *, name):
    counts = [len(s["lands"]) for s in states]
    flat = [a for s in states for a in s["lands"]]
    n_all, n_groups = sum(counts), len(states)

    def body(*refs):
        l_refs = refs[:n_all]
        recv_ici = refs[n_all:n_all + n_groups]
        fwd = refs[n_all + n_groups + 1:n_all + n_groups + 1 + 2 * n_groups]
        x, y, c = _position()
        at = 0
        for gi, n in enumerate(counts):
            fwd_send, fwd_recv = fwd[2 * gi], fwd[2 * gi + 1]
            for t in range(n):
                for j, (px, py) in enumerate(_other_chips(x, y)):
                    block = l_refs[at + t].at[4 * px + 2 * py + c]
                    _remote(block, block, fwd_send.at[3 * t + j], recv_ici[gi].at[3 * t + j], (px, py, c)).wait_recv()
                    _remote(block, block, fwd_send.at[3 * t + j], fwd_recv.at[3 * t + j], (x, y, 1 - c)).start()
            at += n
        refs[-1][...] = jnp.zeros_like(refs[-1])

    sem_shapes = [pltpu.SemaphoreType.DMA((3 * n,)) for n in counts for _ in range(2)]
    out = pl.pallas_call(
        body, name=name,
        out_shape=(*sem_shapes, *[pltpu.HBM(a.shape, a.dtype) for a in flat], jax.ShapeDtypeStruct((8, LANES), F32)),
        in_specs=(_HBM,) * n_all + (_SEM,) * n_groups + (pl.BlockSpec(memory_space=pl.ANY),),
        out_specs=(_SEM,) * (2 * n_groups) + (_HBM,) * n_all + (pl.BlockSpec(memory_space=pltpu.VMEM),),
        input_output_aliases={i: 2 * n_groups + i for i in range(n_all)},
        compiler_params=pltpu.CompilerParams(has_side_effects=_EFFECT),
    )(*flat, *[s["recv_ici"] for s in states], after)
    at = 0
    for gi, (s, n) in enumerate(zip(states, counts)):
        s.update(fwd_send=out[2 * gi], fwd_recv=out[2 * gi + 1], lands=list(out[2 * n_groups + at:2 * n_groups + at + n]))
        at += n
    return out[-1]


def gather_finish(state, after, *, name):
    n = len(state["lands"])

    def body(*refs):
        s_refs, l_refs = refs[:n], refs[n:2 * n]
        send, recv_sibling, fwd_send, fwd_recv = refs[2 * n:2 * n + 4]
        x, y, c = _position()
        me = 4 * x + 2 * y + c
        for t in range(n):
            own = l_refs[t].at[me]
            _remote(s_refs[t], own, send.at[4 * t], recv_sibling.at[t], (x, y, 1 - c)).wait_send()
            _remote(s_refs[t], l_refs[t].at[4 * x + 2 * y + 1 - c], send.at[4 * t], recv_sibling.at[t], (x, y, 1 - c)).wait_recv()
            for j, (px, py) in enumerate(_other_chips(x, y)):
                _remote(s_refs[t], own, send.at[4 * t + 1 + j], recv_sibling.at[t], (px, py, c)).wait_send()
                mine, theirs = l_refs[t].at[4 * px + 2 * py + c], l_refs[t].at[4 * px + 2 * py + 1 - c]
                _remote(mine, mine, fwd_send.at[3 * t + j], fwd_recv.at[3 * t + j], (x, y, 1 - c)).wait_send()
                _remote(theirs, theirs, fwd_send.at[3 * t + j], fwd_recv.at[3 * t + j], (x, y, 1 - c)).wait_recv()

    both = state["shards"] + state["lands"]
    out = pl.pallas_call(
        body, name=name,
        out_shape=tuple(pltpu.HBM(a.shape, a.dtype) for a in both),
        in_specs=(_HBM,) * (2 * n) + (_SEM,) * 4 + (pl.BlockSpec(memory_space=pl.ANY),), out_specs=(_HBM,) * (2 * n),
        input_output_aliases={i: i for i in range(2 * n)},
        compiler_params=pltpu.CompilerParams(has_side_effects=_EFFECT),
    )(*both, state["send"], state["recv_sibling"], state["fwd_send"], state["fwd_recv"], after)
    return list(out[n:])


def _all_peers_plan(src_ref, land_ref, k):
    x, y, c = _position()
    bits = k + 1
    peer = ((1 - x) if bits & 4 else x, (1 - y) if bits & 2 else y, (1 - c) if bits & 1 else c)
    return src_ref, land_ref.at[4 * x + 2 * y + c], peer


def _sibling_plan(src_ref, land_ref, k):
    x, y, c = _position()
    return src_ref.at[2 * k + (1 - c)], land_ref.at[k], (x, y, 1 - c)


def _chips_plan(src_ref, land_ref, j):
    x, y, c = _position()
    px, py = _other_chips(x, y)[j]
    return src_ref.at[j], land_ref.at[j], (px, py, c)


SUM_STEPS = 2


def sum_for_chips(parts, from_sibling, ck_idx, *, name):
    n = len(parts)

    def body(ck_ref, *refs):
        del ck_ref
        for t in range(n):
            refs[2 * n + t][...] = (refs[t][...] + refs[n + t][...]).astype(BF16)

    def blk(a):
        return (None, a.shape[1] // SUM_STEPS, a.shape[2])

    return pl.pallas_call(
        body, name=name,
        grid_spec=pltpu.PrefetchScalarGridSpec(
            num_scalar_prefetch=1, grid=(3, SUM_STEPS),
            in_specs=[pl.BlockSpec(blk(a), lambda j, i, ck: (2 * ck[1 + j] + ck[0], i, 0)) for a in parts]
            + [pl.BlockSpec(blk(a), lambda j, i, ck: (ck[1 + j], i, 0)) for a in from_sibling],
            out_specs=[pl.BlockSpec(blk(a), lambda j, i, ck: (j, i, 0)) for a in from_sibling]),
        out_shape=[jax.ShapeDtypeStruct((3,) + a.shape[1:], BF16) for a in from_sibling], compiler_params=_cp(),
    )(ck_idx, *parts, *from_sibling)


def sum_final(parts, from_sibling, from_chips, kc_idx, *, name):
    n = len(parts)

    def body(kc_ref, *refs):
        del kc_ref
        for t in range(n):
            p, s, a, b, d = (refs[j * n + t] for j in range(5))
            refs[5 * n + t][...] = (((p[...] + s[...]) + a[...].astype(F32)) + b[...].astype(F32)) + d[...].astype(F32)

    def blk(a):
        return (None, a.shape[1] // SUM_STEPS, a.shape[2])

    def chip_specs(j):
        return [pl.BlockSpec(blk(a), lambda i, kc: (j, i, 0)) for a in from_chips]

    return pl.pallas_call(
        body, name=name,
        grid_spec=pltpu.PrefetchScalarGridSpec(
            num_scalar_prefetch=1, grid=(SUM_STEPS,),
            in_specs=[pl.BlockSpec(blk(a), lambda i, kc: (2 * kc[0] + kc[1], i, 0)) for a in parts]
            + [pl.BlockSpec(blk(a), lambda i, kc: (kc[0], i, 0)) for a in from_sibling]
            + chip_specs(0) + chip_specs(1) + chip_specs(2),
            out_specs=[pl.BlockSpec(blk(a)[1:], lambda i, kc: (i, 0)) for a in parts]),
        out_shape=[jax.ShapeDtypeStruct(a.shape[1:], F32) for a in parts], compiler_params=_cp(),
    )(kc_idx, *parts, *from_sibling, *from_chips, *from_chips, *from_chips)


BIG = (
    ("w_in", IN_COLS, True), ("w_out", D_MODEL, False), ("wq_x", D_MODEL, False), ("wkv_x", 2 * D_MODEL, True),
    ("wo_x", D_MODEL, False), ("w_gate_up", 2 * D_FF, True), ("w_down", D_FF, False),
)

SMALL = ("norm_mix_g", "q_norm_g", "k_norm_g", "sinks", "conv_b", "conv_ln_g", "conv_ln_b",
         "norm_x_g", "norm_mem_g", "xq_norm_g", "xk_norm_g", "norm_ffn_g")


def _adamw_math(w, g, m, v):
    m2 = ADAM_B1 * m + (1.0 - ADAM_B1) * g
    v2 = ADAM_B2 * v + (1.0 - ADAM_B2) * jnp.square(g)
    m_hat = m2 / (1.0 - ADAM_B1 ** ADAM_STEP)
    v_hat = v2 / (1.0 - ADAM_B2 ** ADAM_STEP)
    return -ADAM_LR * (m_hat / (jnp.sqrt(v_hat) + ADAM_EPS) + ADAM_WD * w), m2, v2


def _small_rows(per_layer_shape):
    return 1 if len(per_layer_shape) == 1 else per_layer_shape[0]


def pack_small(parts, shapes):
    blocks = []
    for per_layer, sh in zip(parts, shapes):
        for g in per_layer:
            g = g.reshape(_small_rows(sh), sh[-1])
            blocks.append(jnp.pad(g, ((0, 0), (0, D_MODEL - sh[-1]))))
    rows = sum(b.shape[0] for b in blocks)
    blocks.append(jnp.zeros((-rows % 8, D_MODEL), F32))
    return jnp.concatenate(blocks, axis=0)


def update_small(gathered, shapes, weights, moments_m, moments_v, n_update):
    n_all = len(shapes)

    def body(*refs):
        g_ref = refs[0]
        w_refs, m_refs, v_refs = (refs[1 + j * n_update:1 + (j + 1) * n_update] for j in range(3))
        out = refs[1 + 3 * n_update:]
        grad_refs = out[:n_all]
        d_refs, nm_refs, nv_refs = (out[n_all + j * n_update:n_all + (j + 1) * n_update] for j in range(3))
        at = 0
        for p, sh in enumerate(shapes):
            rows, lanes = _small_rows(sh), sh[-1]
            for l in range(DEPTH):
                g = g_ref[0, at:at + rows, 0:lanes]
                for k in range(1, N_DEV):
                    g = g + g_ref[k, at:at + rows, 0:lanes]
                at += rows
                here = (slice(l, l + 1),) + (slice(None),) * (len(sh) - 1) if len(sh) == 1 else (l,)
                grad_refs[p][here] = g
                if p < n_update:
                    d, m2, v2 = _adamw_math(w_refs[p][here], g, m_refs[p][here], v_refs[p][here])
                    d_refs[p][here] = d
                    nm_refs[p][here] = m2
                    nv_refs[p][here] = v2

    full = [jax.ShapeDtypeStruct((DEPTH,) + tuple(sh), F32) for sh in shapes]
    out = pl.pallas_call(
        body, name="update_small", out_shape=full + full[:n_update] * 3, compiler_params=_cp(),
    )(gathered, *weights, *moments_m, *moments_v)
    return (out[:n_all], out[n_all:n_all + n_update], out[n_all + n_update:n_all + 2 * n_update],
            out[n_all + 2 * n_update:])


WEIGHT_GROUPS = {"in": ("w_in",), "mid": ("w_out", "wq_x", "wkv_x", "wo_x"), "ffn": ("w_gate_up", "w_down")}


def _layer_fwd(x0, mem, weights_of, s, reached, target=None):
    w = dict(weights_of("in", x0))
    h0, u = norm_proj(x0, s["norm_mix_g"], w["w_in"])
    mixed = swa_fwd(u, s["q_norm_g"], s["k_norm_g"], s["sinks"])
    reached("attn", mixed)
    mixed, conv_y = conv_fwd(u, mixed, s["conv_w"], s["conv_b"], s["conv_ln_g"], s["conv_ln_b"])
    w.update(weights_of("mid", conv_y))
    memn, kv = kv_fwd(mem, s["norm_mem_g"], w["wkv_x"])
    x1, h1, qx, o, x2, h2 = mid_fwd(mixed, x0, w["w_out"], s["norm_x_g"], w["wq_x"], kv, s["xq_norm_g"], s["xk_norm_g"],
                                    w["wo_x"], s["norm_ffn_g"])
    reached("mid", x2)
    w.update(weights_of("ffn", x2))
    gu, a, *out = ffn_fwd(h2, x2, w["w_gate_up"], w["w_down"], target)
    saved = dict(x0=x0, h0=h0, u=u, conv_y=conv_y, mixed=mixed, x1=x1, h1=h1, qx=qx, memn=memn, kv=kv, o=o, x2=x2, h2=h2,
                 gu=gu, a=a)
    return out, saved, w


def _ordered_after(a, token):
    return a if token is None else a + token[0, 0]


def _layer_bwd(dx3, mem, w, s, sv, token, stage_done):
    gs = {}
    dgu, dx2, dg = ffn_bwd(dx3, sv["gu"], sv["x2"], _ordered_after(s["norm_ffn_g"], token), w["w_down"], w["w_gate_up"])
    gs["norm_ffn_g"] = dg
    gb = {"w_down": mm_tn(sv["a"], dx3, name="mm_dw_down")}
    gb["w_gate_up"] = mm_tn(dgu, sv["h2"], tk=dgu.shape[0], name="mm_dw_gate_up")
    token = stage_done("ffn", gb, gb["w_gate_up"])

    gb = {}
    dq, dx1, dmixed, dkv, dqg, dkg, dg = mid_bwd(dx2, sv["qx"], sv["kv"], s["xq_norm_g"], s["xk_norm_g"], sv["x1"],
                                                 _ordered_after(s["norm_x_g"], token), w["wo_x"], w["wq_x"], w["w_out"])
    gs["xq_norm_g"], gs["xk_norm_g"], gs["norm_x_g"] = dqg, dkg, dg
    gb["wo_x"] = mm_tn(sv["o"], dx2, name="mm_dwo")
    gb["wq_x"] = mm_tn(sv["h1"], dq, name="mm_dwq")
    gb["wkv_x"], gs["norm_mem_g"] = kv_bwd(dkv, sv["memn"], mem, w["wkv_x"])
    gb["w_out"] = mm_tn(sv["mixed"], dx1, name="mm_dw_out")
    token = stage_done("mid", gb, gb["w_out"])

    du, dqg, dkg, dsinks = swa_bwd(sv["u"], dmixed, _ordered_after(s["q_norm_g"], token), s["k_norm_g"], s["sinks"])
    gs["q_norm_g"], gs["k_norm_g"], gs["sinks"] = dqg[0, :HEAD_DIM], dkg[0, :HEAD_DIM], dsinks[0, :N_Q_HEADS]
    token = stage_done("attn", {}, dqg)
    du, dconv_w, dvec = conv_bwd(sv["u"], sv["conv_y"], dmixed, du, s["conv_w"], _ordered_after(s["conv_ln_g"], token),
                                 s["conv_ln_b"])
    gs["conv_w"] = dconv_w[:CONV_K]
    gs["conv_b"], gs["conv_ln_g"], gs["conv_ln_b"] = dvec[0], dvec[1], dvec[2]
    dw_in = mm_tn(du, sv["h0"], tk=2048, name="mm_dw_in")
    token = stage_done("in", {"w_in": dw_in}, dw_in)
    dx0, dg = in_bwd(du, w["w_in"], sv["x0"], _ordered_after(s["norm_mix_g"], token), dx1)
    gs["norm_mix_g"] = dg
    token = stage_done("mix", {}, dx0)
    return dx0, gs, token


def _local_step(x, mem, target, weights_of, reached, smalls, stage_done):
    saved, weights = [], []
    out = [x]
    for l in range(DEPTH):
        out, sv, w = _layer_fwd(out[0], mem, functools.partial(weights_of, l), smalls[l], functools.partial(reached, l),
                                target if l == DEPTH - 1 else None)
        saved.append(sv)
        weights.append(w)
    dx, loss_part = out
    gss, token = [None] * DEPTH, None
    for l in reversed(range(DEPTH)):
        dx, gss[l], token = _layer_bwd(dx, mem, weights[l], smalls[l], saved[l], token,
                                       functools.partial(stage_done, l))
    return loss_part[0, 0], dx, gss


def kernel(x, mem, norm_mix_g, w_in, q_norm_g, k_norm_g, sinks, conv_w, conv_b, conv_ln_g, conv_ln_b, w_out, norm_x_g, norm_mem_g, wq_x, wkv_x, xq_norm_g, xk_norm_g, wo_x, norm_ffn_g, w_gate_up, w_down, loss_target, m_norm_mix_g, m_w_in, m_q_norm_g, m_k_norm_g, m_sinks, m_conv_w, m_conv_b, m_conv_ln_g, m_conv_ln_b, m_w_out, m_norm_x_g, m_norm_mem_g, m_wq_x, m_wkv_x, m_xq_norm_g, m_xk_norm_g, m_wo_x, m_norm_ffn_g, m_w_gate_up, m_w_down, v_norm_mix_g, v_w_in, v_q_norm_g, v_k_norm_g, v_sinks, v_conv_w, v_conv_b, v_conv_ln_g, v_conv_ln_b, v_w_out, v_norm_x_g, v_norm_mem_g, v_wq_x, v_wkv_x, v_xq_norm_g, v_xk_norm_g, v_wo_x, v_norm_ffn_g, v_w_gate_up, v_w_down):
    P = dict(norm_mix_g=norm_mix_g, w_in=w_in, q_norm_g=q_norm_g, k_norm_g=k_norm_g, sinks=sinks, conv_w=conv_w, conv_b=conv_b,
             conv_ln_g=conv_ln_g, conv_ln_b=conv_ln_b, w_out=w_out, norm_x_g=norm_x_g, norm_mem_g=norm_mem_g, wq_x=wq_x,
             wkv_x=wkv_x, xq_norm_g=xq_norm_g, xk_norm_g=xk_norm_g, wo_x=wo_x, norm_ffn_g=norm_ffn_g, w_gate_up=w_gate_up,
             w_down=w_down)
    M = dict(norm_mix_g=m_norm_mix_g, w_in=m_w_in, q_norm_g=m_q_norm_g, k_norm_g=m_k_norm_g, sinks=m_sinks, conv_w=m_conv_w,
             conv_b=m_conv_b, conv_ln_g=m_conv_ln_g, conv_ln_b=m_conv_ln_b, w_out=m_w_out, norm_x_g=m_norm_x_g,
             norm_mem_g=m_norm_mem_g, wq_x=m_wq_x, wkv_x=m_wkv_x, xq_norm_g=m_xq_norm_g, xk_norm_g=m_xk_norm_g, wo_x=m_wo_x,
             norm_ffn_g=m_norm_ffn_g, w_gate_up=m_w_gate_up, w_down=m_w_down)
    V = dict(norm_mix_g=v_norm_mix_g, w_in=v_w_in, q_norm_g=v_q_norm_g, k_norm_g=v_k_norm_g, sinks=v_sinks, conv_w=v_conv_w,
             conv_b=v_conv_b, conv_ln_g=v_conv_ln_g, conv_ln_b=v_conv_ln_b, w_out=v_w_out, norm_x_g=v_norm_x_g,
             norm_mem_g=v_norm_mem_g, wq_x=v_wq_x, wkv_x=v_wkv_x, xq_norm_g=v_xq_norm_g, xk_norm_g=v_xk_norm_g, wo_x=v_wo_x,
             norm_ffn_g=v_norm_ffn_g, w_gate_up=v_w_gate_up, w_down=v_w_down)
    order = ["norm_mix_g", "w_in", "q_norm_g", "k_norm_g", "sinks", "conv_w", "conv_b", "conv_ln_g", "conv_ln_b", "w_out",
             "norm_x_g", "norm_mem_g", "wq_x", "wkv_x", "xq_norm_g", "xk_norm_g", "wo_x", "norm_ffn_g", "w_gate_up", "w_down"]
    xi, yi, ci = _position()
    dev = 4 * xi + 2 * yi + ci
    x2d, mem2d, tgt2d = x[0], mem[0], loss_target[0]

    def travelling(name, l, transposed):
        a = P[name][l]
        return (a.T if transposed else a).astype(BF16)

    rows_of = {n: rows for n, rows, _ in BIG}
    transposed_of = {n: tr for n, _, tr in BIG}

    def whole(names, gathered):
        return {n: g.reshape(rows_of[n], D_MODEL) for n, g in zip(names, gathered)}

    cw = jnp.pad(conv_w.reshape(DEPTH * CONV_K, CONV_CH // N_DEV), ((0, 2), (0, LANES - CONV_CH // N_DEV)))
    (w_in0, cw_all), token0 = all_gather_many([travelling("w_in", 0, True), cw], name="ag_w_in0_conv_w")
    travel_order = [(0, "mid"), (0, "ffn"), (1, "in"), (1, "mid"), (1, "ffn")]
    travel_groups = []
    for l, group in travel_order:
        shards = [_ordered_after(travelling(n, l, transposed_of[n]), token0.astype(BF16)) for n in WEIGHT_GROUPS[group]]
        lands = [lax.dynamic_update_slice(lax.empty((N_DEV,) + s.shape, BF16), s[None], (dev, 0, 0)) for s in shards]
        travel_groups.append((shards, lands))
    travel_states, travel_token = gather_start(travel_groups, name="ag_weights_start")
    travelling_state = dict(zip(travel_order, travel_states))
    forward_at = {(0, "attn"): [(0, "mid")], (0, "mid"): [(0, "ffn"), (1, "in")], (1, "attn"): [(1, "mid"), (1, "ffn")]}

    def reached(l, stage, marker):
        keys = forward_at.get((l, stage))
        if keys:
            gather_forward([travelling_state[k] for k in keys], marker,
                           name="ag_weights_forward_" + "_".join(f"{g}{ll}" for ll, g in keys))

    def weights_of(l, group, marker):
        if (l, group) == (0, "in"):
            return whole(WEIGHT_GROUPS[group], [w_in0])
        gathered = gather_finish(travelling_state[(l, group)], marker, name=f"ag_weights_finish_{group}{l}")
        return whole(WEIGHT_GROUPS[group], gathered)

    cw_full = cw_all[:, :DEPTH * CONV_K, :CONV_CH // N_DEV].reshape(N_DEV, DEPTH, CONV_K, CONV_CH // N_DEV)
    cw_full = jnp.transpose(cw_full, (1, 2, 0, 3)).reshape(DEPTH, CONV_K, CONV_CH)
    smalls = []
    for l in range(DEPTH):
        sl = {n: P[n][l] if n == "sinks" else P[n][l:l + 1] for n in SMALL}
        sl["conv_w"] = jnp.pad(cw_full[l], ((0, CONV_HALO - CONV_K), (0, 0)))
        smalls.append(sl)
    smalls[0]["norm_mix_g"] = _ordered_after(smalls[0]["norm_mix_g"], travel_token)

    ck_idx = jnp.stack([ci] + [2 * px + py for px, py in _other_chips(xi, yi)]).astype(jnp.int32)
    kc_idx = jnp.stack([2 * xi + yi, ci]).astype(jnp.int32)
    got, flight, reduced = {}, {}, {}

    def as_parts(gb):
        keys = sorted(gb)
        return keys, [gb[k].reshape(N_DEV, rows_of[k[1]] // N_DEV, D_MODEL) for k in keys]

    def lands_like(parts, blocks, dtype):
        return [lax.empty((blocks,) + p.shape[1:], dtype) for p in parts]

    def to_sibling(group, gb):
        keys, parts = as_parts(gb)
        flight[group] = (keys, split_start(parts, lands_like(parts, 4, F32), _sibling_plan, 4,
                                           name=f"rs_sibling_{group}_start"))
        return flight[group][1][4]

    def to_chips(group, marker):
        keys, started = flight[group]
        parts, from_sibling = split_wait(started, marker, _sibling_plan, 4, name=f"rs_sibling_{group}_wait")
        chip_sums = sum_for_chips(parts, from_sibling, ck_idx, name=f"rs_sum_for_chips_{group}")
        started = split_start(chip_sums, lands_like(parts, 3, BF16), _chips_plan, 3, name=f"rs_chips_{group}_start")
        flight[group] = (keys, parts, from_sibling, started)
        return started[4]

    def finish(group, marker):
        keys, parts, from_sibling, started = flight[group]
        _, from_chips = split_wait(started, marker, _chips_plan, 3, name=f"rs_chips_{group}_wait")
        reduced.update(zip(keys, sum_final(parts, from_sibling, from_chips, kc_idx, name=f"rs_sum_final_{group}")))

    def stage_done(l, stage, gb, marker):
        gb = {(l, n): g for n, g in gb.items()}
        if l == 1:
            got.update(gb)
            return to_sibling("l1", got) if stage == "mix" else None
        if stage == "ffn":
            return to_chips("l1", marker) + to_sibling("ffn", gb)
        if stage == "mid":
            return to_chips("ffn", marker) + to_sibling("mid", gb)
        if stage == "attn":
            return to_chips("mid", marker)
        if stage == "in":
            return to_sibling("in", gb)
        to_chips("in", marker)
        for group in ("l1", "ffn", "mid"):
            finish(group, marker)
        return None

    loss_part, grad_x, gss = _local_step(x2d, mem2d, tgt2d, weights_of, reached, smalls, stage_done)
    loss = lax.psum(loss_part, ("x", "y", "c"))

    small_names = SMALL + ("conv_w",)
    small_shapes = [(CONV_K, CONV_CH) if n == "conv_w" else P[n].shape[1:] for n in small_names]
    small_parts = pack_small([[gss[l][n] for l in range(DEPTH)] for n in small_names], small_shapes)
    small_land = lax.dynamic_update_slice(lax.empty((N_DEV,) + small_parts.shape, F32), small_parts[None], (dev, 0, 0))
    small_flight = split_start([small_parts], [small_land], _all_peers_plan, N_DEV - 1, name="ag_small_grads_start")

    grads, delta, new_m, new_v = {}, {}, {}, {}

    def update(n, transposed):
        shape = P[n].shape
        two_d = lambda a: a.reshape(shape[0] * shape[1], shape[2])
        grads[n] = jnp.stack([reduced[(l, n)].T if transposed else reduced[(l, n)] for l in range(DEPTH)])
        d_, m_, v_ = adamw(two_d(P[n]), two_d(grads[n]), two_d(M[n]), two_d(V[n]), name="adamw_" + n)
        delta[n], new_m[n], new_v[n] = d_.reshape(shape), m_.reshape(shape), v_.reshape(shape)

    for n, _, transposed in BIG:
        if n != "w_in":
            update(n, transposed)
    finish("in", delta["w_down"])
    update("w_in", True)
    small_all = split_wait(small_flight, delta["w_in"], _all_peers_plan, N_DEV - 1, name="ag_small_grads_wait")[1][0]
    g_, d_, m_, v_ = update_small(small_all, small_shapes, [P[n] for n in SMALL], [M[n] for n in SMALL],
                                  [V[n] for n in SMALL], len(SMALL))
    for i, n in enumerate(SMALL):
        grads[n], delta[n], new_m[n], new_v[n] = g_[i], d_[i], m_[i], v_[i]
    cols = CONV_CH // N_DEV
    grads["conv_w"] = lax.dynamic_slice_in_dim(g_[-1], dev * cols, cols, axis=2)
    flat = lambda a: a.reshape(DEPTH * CONV_K, cols)
    d_, m_, v_ = adamw(flat(conv_w), flat(grads["conv_w"]), flat(m_conv_w), flat(v_conv_w), name="adamw_conv_w")
    delta["conv_w"], new_m["conv_w"], new_v["conv_w"] = (a.reshape(conv_w.shape) for a in (d_, m_, v_))

    return (loss, grad_x[None], *[grads[n] for n in order], *[delta[n] for n in order],
            *[new_m[n] for n in order], *[new_v[n] for n in order])
```

```python
import functools

import jax
import jax.numpy as jnp
import numpy as np
from jax import lax
from jax.experimental import pallas as pl
from jax.experimental.pallas import tpu as pltpu

F32 = jnp.float32
BF16 = jnp.bfloat16

D_MODEL = 1024
HEAD_DIM = 64
N_Q_HEADS = 8
N_KV_HEADS = 2
GROUP = N_Q_HEADS // N_KV_HEADS
ATTN_WIDTH = N_Q_HEADS * HEAD_DIM
KV_WIDTH = N_KV_HEADS * HEAD_DIM
QKV_WIDTH = ATTN_WIDTH + 2 * KV_WIDTH
CONV_CH = 512
IN_COLS = QKV_WIDTH + 2 * CONV_CH
CONV_K = 31
CONV_HALO = 32
BLOCK = 128
N_X_HEADS = 4
X_HEAD_DIM = 256
D_FF = 2816
EPS = 1e-6
NEG = -1e30
DEPTH = 2
N_DEV = 8

ADAM_LR = 0.001
ADAM_B1 = 0.9
ADAM_B2 = 0.999
ADAM_EPS = 1e-08
ADAM_WD = 0.01
ADAM_STEP = 10

V7X_VMEM_LIMIT = 56 * 1024 * 1024
LANES = 128

MESH = pl.DeviceIdType.MESH


def _cp(**kw):
    return pltpu.CompilerParams(vmem_limit_bytes=V7X_VMEM_LIMIT, **kw)


def _dot(a, b, dims):
    return lax.dot_general(a.astype(BF16), b.astype(BF16), (dims, ((), ())), preferred_element_type=F32)


def _dot_nn(a, b):
    return _dot(a, b, ((1,), (0,)))


def _dot_nt(a, b):
    return _dot(a, b, ((1,), (1,)))


def _dot_tn(a, b):
    return _dot(a, b, ((0,), (0,)))


def _sigmoid(x):
    return jax.nn.sigmoid(x)


def _rms(x):
    r = lax.rsqrt(jnp.mean(x * x, axis=-1, keepdims=True) + EPS)
    return x * r, r


def _rms_bwd(dy, xhat, r, g):
    dxh = dy * g
    return r * (dxh - xhat * jnp.mean(dxh * xhat, axis=-1, keepdims=True))


KV_STEPS = 2


def kv_fwd(mem, g, wt_kv):
    n_mem, d = mem.shape
    rows = wt_kv.shape[0] // KV_STEPS

    def body(mem_ref, g_ref, wt_ref, memn_ref, kv_ref):
        @pl.when(pl.program_id(0) == 0)
        def _():
            memn_ref[...] = (_rms(mem_ref[...])[0] * g_ref[...]).astype(memn_ref.dtype)

        kv_ref[...] = _dot_nt(memn_ref[...], wt_ref[...])

    whole = pl.BlockSpec((n_mem, d), lambda j: (0, 0))
    return pl.pallas_call(
        body, name="kv_fwd", grid=(KV_STEPS,),
        in_specs=[whole, pl.BlockSpec((1, d), lambda j: (0, 0)), pl.BlockSpec((rows, d), lambda j: (j, 0))],
        out_specs=[whole, pl.BlockSpec((n_mem, rows), lambda j: (0, j))],
        out_shape=[jax.ShapeDtypeStruct((n_mem, d), BF16), jax.ShapeDtypeStruct((n_mem, wt_kv.shape[0]), F32)],
        compiler_params=_cp(),
    )(mem, g.reshape(1, d), wt_kv)


def kv_bwd(dkv, memn, mem, wt_kv):
    n_mem, d = mem.shape
    rows = wt_kv.shape[0] // KV_STEPS

    def body(dkv_ref, memn_ref, mem_ref, wt_ref, dwt_ref, dg_ref, dmemn_s):
        j = pl.program_id(0)
        dwt_ref[...] = _dot_tn(dkv_ref[...], memn_ref[...])
        part = _dot_nn(dkv_ref[...], wt_ref[...])

        @pl.when(j == 0)
        def _():
            dmemn_s[...] = part

        @pl.when(j > 0)
        def _():
            dmemn_s[...] += part

        @pl.when(j == KV_STEPS - 1)
        def _():
            dg_ref[...] = jnp.sum(dmemn_s[...] * _rms(mem_ref[...])[0], axis=0, keepdims=True)

    whole = pl.BlockSpec((n_mem, d), lambda j: (0, 0))
    return pl.pallas_call(
        body, name="kv_bwd", grid=(KV_STEPS,),
        in_specs=[pl.BlockSpec((n_mem, rows), lambda j: (0, j)), whole, whole, pl.BlockSpec((rows, d), lambda j: (j, 0))],
        out_specs=[pl.BlockSpec((rows, d), lambda j: (j, 0)), pl.BlockSpec((1, d), lambda j: (0, 0))],
        out_shape=[jax.ShapeDtypeStruct(wt_kv.shape, F32), jax.ShapeDtypeStruct((1, d), F32)],
        scratch_shapes=[pltpu.VMEM((n_mem, d), F32)], compiler_params=_cp(),
    )(dkv, memn, mem, wt_kv)


def _tile(n, cap):
    if n <= cap:
        return n
    best = None
    for t in range(LANES, cap + 1, LANES):
        if n % t == 0:
            best = t
    assert best is not None, (n, cap)
    return best


def mm_tn(a, b, *, name, ta_cap=1536, tb_cap=1024, tk=1024):
    m, ka = a.shape
    nb = b.shape[1]
    assert b.shape[0] == m
    tk = min(tk, m)
    ta = _tile(ka, ta_cap)
    tb = _tile(nb, tb_cap)

    def body(a_ref, b_ref, o_ref):
        @pl.when(pl.program_id(2) == 0)
        def _():
            o_ref[...] = jnp.zeros_like(o_ref)

        o_ref[...] += _dot_tn(a_ref[...], b_ref[...])

    return pl.pallas_call(
        body, name=name, grid=(ka // ta, nb // tb, m // tk),
        in_specs=[pl.BlockSpec((tk, ta), lambda i, j, kk: (kk, i)), pl.BlockSpec((tk, tb), lambda i, j, kk: (kk, j))],
        out_specs=pl.BlockSpec((ta, tb), lambda i, j, kk: (i, j)),
        out_shape=jax.ShapeDtypeStruct((ka, nb), F32), compiler_params=_cp(),
    )(a, b)


def _whole(shape):
    return pl.BlockSpec(shape, lambda i: (0,) * len(shape), pipeline_mode=pl.Buffered(1))


def _rows(tm, n):
    return pl.BlockSpec((tm, n), lambda i: (i, 0))


def _vec(n):
    return pl.BlockSpec((1, n), lambda i: (0, 0))


def _chunks(n, cap=1408):
    size = _tile(n, cap)
    return [(s, size) for s in range(0, n, size)]


def _zero_at_first_step(*refs):
    @pl.when(pl.program_id(0) == 0)
    def _():
        for r in refs:
            r[...] = jnp.zeros_like(r)


def norm_proj(x, g, wt, *, tm=512):
    m, d = x.shape
    n = wt.shape[0]

    def body(x_ref, g_ref, wt_ref, h_ref, u_ref):
        h = (_rms(x_ref[...])[0] * g_ref[...]).astype(BF16)
        h_ref[...] = h
        for s, sz in _chunks(n):
            u_ref[:, s:s + sz] = _dot_nt(h, wt_ref[s:s + sz, :])

    return pl.pallas_call(
        body, name="norm_proj", grid=(m // tm,),
        in_specs=[_rows(tm, d), _vec(d), _whole((n, d))],
        out_specs=[_rows(tm, d), _rows(tm, n)],
        out_shape=[jax.ShapeDtypeStruct((m, d), BF16), jax.ShapeDtypeStruct((m, n), F32)],
        compiler_params=_cp(),
    )(x, g.reshape(1, d), wt)


def _xattn_heads(q_ref, kv_ref, qg_v, kg_v, d):
    normed = []
    for h in range(N_X_HEADS):
        cols = slice(h * X_HEAD_DIM, (h + 1) * X_HEAD_DIM)
        qh, rq = _rms(q_ref[:, cols])
        normed.append((qh, rq, (qh * qg_v).astype(BF16), (_rms(kv_ref[:, cols])[0] * kg_v).astype(BF16),
                       kv_ref[:, d + h * X_HEAD_DIM:d + (h + 1) * X_HEAD_DIM].astype(BF16)))
    scores = [_dot_nt(qn, kn) * (X_HEAD_DIM ** -0.5) for _, _, qn, kn, _ in normed]
    out = []
    for (qh, rq, qn, kn, v), s in zip(normed, scores):
        e = jnp.exp(s - jnp.max(s, axis=-1, keepdims=True))
        out.append((qh, rq, qn, kn, v, e / jnp.sum(e, axis=-1, keepdims=True)))
    return out


def mid_fwd(mixed, x0, w_out, g_x, wq, kv, xqg, xkg, wo, g_f, *, tm=512):
    m, d = x0.shape
    n_mem = kv.shape[0]

    def body(mixed_ref, x0_ref, w_out_ref, g_x_ref, wq_ref, kv_ref, xqg_ref, xkg_ref, wo_ref, g_f_ref,
             x1_ref, h1_ref, qx_ref, o_ref, x2_ref, h2_ref):
        x1 = x0_ref[...] + _dot_nn(mixed_ref[...], w_out_ref[...])
        x1_ref[...] = x1
        h1 = (_rms(x1)[0] * g_x_ref[...]).astype(BF16)
        h1_ref[...] = h1
        qx_ref[...] = _dot_nn(h1, wq_ref[...])
        for h, (_, _, _, _, v, p) in enumerate(_xattn_heads(qx_ref, kv_ref, xqg_ref[...], xkg_ref[...], d)):
            o_ref[:, h * X_HEAD_DIM:(h + 1) * X_HEAD_DIM] = _dot_nn(p, v).astype(o_ref.dtype)
        x2 = x1 + _dot_nn(o_ref[...], wo_ref[...])
        x2_ref[...] = x2
        h2_ref[...] = (_rms(x2)[0] * g_f_ref[...]).astype(BF16)

    sq = _whole((d, d))
    f32_rows, bf_rows = jax.ShapeDtypeStruct((m, d), F32), jax.ShapeDtypeStruct((m, d), BF16)
    return pl.pallas_call(
        body, name="mid_fwd", grid=(m // tm,),
        in_specs=[_rows(tm, d), _rows(tm, d), sq, _vec(d), sq, _whole((n_mem, 2 * d)), _vec(X_HEAD_DIM), _vec(X_HEAD_DIM),
                  sq, _vec(d)],
        out_specs=[_rows(tm, d)] * 6,
        out_shape=[f32_rows, bf_rows, f32_rows, bf_rows, f32_rows, bf_rows],
        compiler_params=_cp(),
    )(mixed, x0, w_out, g_x.reshape(1, d), wq, kv, xqg.reshape(1, X_HEAD_DIM), xkg.reshape(1, X_HEAD_DIM), wo,
      g_f.reshape(1, d))


def ffn_fwd(h2, x2, wt_gu, w_down, target=None, *, tm=256):
    m, d = x2.shape
    f = w_down.shape[0]
    with_loss = target is not None

    def body(*refs):
        if with_loss:
            h2_ref, x2_ref, wt_gu_ref, w_down_ref, t_ref, gu_ref, a_ref, dy_ref, l_ref = refs
        else:
            h2_ref, x2_ref, wt_gu_ref, w_down_ref, gu_ref, a_ref, x3_ref = refs
        h = h2_ref[...]
        for s, sz in _chunks(2 * f):
            gu_ref[:, s:s + sz] = _dot_nt(h, wt_gu_ref[s:s + sz, :])
        for s, sz in _chunks(f):
            g = gu_ref[:, s:s + sz]
            a_ref[:, s:s + sz] = (g * _sigmoid(g) * gu_ref[:, f + s:f + s + sz]).astype(a_ref.dtype)
        x3 = x2_ref[...] + _dot_nn(a_ref[...], w_down_ref[...])
        if not with_loss:
            x3_ref[...] = x3
            return
        err = x3 - t_ref[...]
        dy_ref[...] = err * (1.0 / d)
        _zero_at_first_step(l_ref)
        part = jnp.sum(jnp.sum(err * err, axis=-1, keepdims=True), axis=0, keepdims=True)
        l_ref[...] += jnp.broadcast_to(part * (0.5 / d), l_ref.shape)

    last = [_rows(tm, d), pl.BlockSpec((1, LANES), lambda i: (0, 0))] if with_loss else [_rows(tm, d)]
    last_shape = [jax.ShapeDtypeStruct((m, d), F32)] + ([jax.ShapeDtypeStruct((1, LANES), F32)] if with_loss else [])
    return pl.pallas_call(
        body, name="ffn_fwd_loss" if with_loss else "ffn_fwd", grid=(m // tm,),
        in_specs=[_rows(tm, d), _rows(tm, d), _whole((2 * f, d)), _whole((f, d))] + ([_rows(tm, d)] if with_loss else []),
        out_specs=[_rows(tm, 2 * f), _rows(tm, f)] + last,
        out_shape=[jax.ShapeDtypeStruct((m, 2 * f), F32), jax.ShapeDtypeStruct((m, f), BF16)] + last_shape,
        compiler_params=_cp(),
    )(*([h2, x2, wt_gu, w_down] + ([target] if with_loss else [])))


def ffn_bwd(dx3, gu, x2, g_f, w_down, wt_gu, *, tm=256):
    m, d = x2.shape
    f = w_down.shape[0]

    def body(dx3_ref, gu_ref, x2_ref, g_ref, w_down_ref, wt_gu_ref, dgu_ref, dx2_ref, dg_ref):
        _zero_at_first_step(dg_ref)
        dx3 = dx3_ref[...]
        dx3_b = dx3.astype(BF16)
        for s, sz in _chunks(f):
            da = _dot_nt(dx3_b, w_down_ref[s:s + sz, :])
            g = gu_ref[:, s:s + sz]
            u = gu_ref[:, f + s:f + s + sz]
            sg = _sigmoid(g)
            dgu_ref[:, s:s + sz] = (da * u * (sg * (1.0 + g * (1.0 - sg)))).astype(dgu_ref.dtype)
            dgu_ref[:, f + s:f + s + sz] = (da * (g * sg)).astype(dgu_ref.dtype)
        dh2 = _dot_nn(dgu_ref[...], wt_gu_ref[...])
        xh, r = _rms(x2_ref[...])
        dg_ref[...] += jnp.sum(dh2 * xh, axis=0, keepdims=True)
        dx2_ref[...] = dx3 + _rms_bwd(dh2, xh, r, g_ref[...])

    return pl.pallas_call(
        body, name="ffn_bwd", grid=(m // tm,),
        in_specs=[_rows(tm, d), _rows(tm, 2 * f), _rows(tm, d), _vec(d), _whole((f, d)), _whole((2 * f, d))],
        out_specs=[_rows(tm, 2 * f), _rows(tm, d), _vec(d)],
        out_shape=[jax.ShapeDtypeStruct((m, 2 * f), BF16), jax.ShapeDtypeStruct((m, d), F32),
                   jax.ShapeDtypeStruct((1, d), F32)],
        compiler_params=_cp(),
    )(dx3, gu, x2, g_f.reshape(1, d), w_down, wt_gu)


def mid_bwd(dx2, qx, kv, xqg, xkg, x1, g_x, wo, wq, w_out, *, tm=512):
    m, d = x1.shape
    n_mem = kv.shape[0]
    nt = m // tm

    def body(dx2_ref, qx_ref, kv_ref, xqg_ref, xkg_ref, x1_ref, g_x_ref, wo_ref, wq_ref, w_out_ref,
             dq_ref, dx1_ref, dmixed_ref, dkv_ref, dqg_ref, dkg_ref, dg_ref):
        i = pl.program_id(0)
        _zero_at_first_step(dkv_ref, dqg_ref, dkg_ref, dg_ref)
        qg_v, kg_v = xqg_ref[...], xkg_ref[...]
        dx2 = dx2_ref[...]
        do = _dot_nt(dx2, wo_ref[...])
        dqg_acc = jnp.zeros((1, X_HEAD_DIM), F32)
        heads = _xattn_heads(qx_ref, kv_ref, qg_v, kg_v, d)
        head_cols = [slice(h * X_HEAD_DIM, (h + 1) * X_HEAD_DIM) for h in range(N_X_HEADS)]
        do_h = [do[:, cols].astype(BF16) for cols in head_cols]
        dps = [_dot_nt(do_h[h], heads[h][4]) for h in range(N_X_HEADS)]
        dss = []
        for (_, _, _, _, _, p), dp in zip(heads, dps):
            dss.append((p.astype(BF16), (p * (dp - jnp.sum(p * dp, axis=-1, keepdims=True))).astype(BF16)))
        for h, ((qh, rq, qn, kn, _, _), (p, ds)) in enumerate(zip(heads, dss)):
            cols = head_cols[h]
            vcols = slice(d + h * X_HEAD_DIM, d + (h + 1) * X_HEAD_DIM)
            dkv_ref[:, vcols] += _dot_tn(p, do_h[h])
            dqn = _dot_nn(ds, kn) * (X_HEAD_DIM ** -0.5)
            dkv_ref[:, cols] += _dot_tn(ds, qn) * (X_HEAD_DIM ** -0.5)
            dqg_acc = dqg_acc + jnp.sum(dqn * qh, axis=0, keepdims=True)
            dq_ref[:, cols] = _rms_bwd(dqn, qh, rq, qg_v).astype(dq_ref.dtype)
        dqg_ref[...] += dqg_acc
        dh1 = _dot_nt(dq_ref[...], wq_ref[...])
        xh, r = _rms(x1_ref[...])
        dg_ref[...] += jnp.sum(dh1 * xh, axis=0, keepdims=True)
        dx1 = dx2 + _rms_bwd(dh1, xh, r, g_x_ref[...])
        dx1_ref[...] = dx1
        dmixed_ref[...] = _dot_nt(dx1, w_out_ref[...])

        @pl.when(i == nt - 1)
        def _():
            dkg_acc = jnp.zeros((1, X_HEAD_DIM), F32)
            for h in range(N_X_HEADS):
                cols = slice(h * X_HEAD_DIM, (h + 1) * X_HEAD_DIM)
                kh, rk = _rms(kv_ref[:, cols])
                dkn = dkv_ref[:, cols]
                dkg_acc = dkg_acc + jnp.sum(dkn * kh, axis=0, keepdims=True)
                dkv_ref[:, cols] = _rms_bwd(dkn, kh, rk, kg_v)
            dkg_ref[...] = dkg_acc

    sq = _whole((d, d))
    full = pl.BlockSpec((n_mem, 2 * d), lambda i: (0, 0))
    return pl.pallas_call(
        body, name="mid_bwd", grid=(nt,),
        in_specs=[_rows(tm, d), _rows(tm, d), _whole((n_mem, 2 * d)), _vec(X_HEAD_DIM), _vec(X_HEAD_DIM), _rows(tm, d),
                  _vec(d), sq, sq, sq],
        out_specs=[_rows(tm, d), _rows(tm, d), _rows(tm, d), full, _vec(X_HEAD_DIM), _vec(X_HEAD_DIM), _vec(d)],
        out_shape=[jax.ShapeDtypeStruct((m, d), BF16), jax.ShapeDtypeStruct((m, d), F32), jax.ShapeDtypeStruct((m, d), F32),
                   jax.ShapeDtypeStruct((n_mem, 2 * d), F32), jax.ShapeDtypeStruct((1, X_HEAD_DIM), F32),
                   jax.ShapeDtypeStruct((1, X_HEAD_DIM), F32), jax.ShapeDtypeStruct((1, d), F32)],
        compiler_params=_cp(),
    )(dx2, qx, kv, xqg.reshape(1, X_HEAD_DIM), xkg.reshape(1, X_HEAD_DIM), x1, g_x.reshape(1, d), wo, wq, w_out)


def in_bwd(du, wt_in, x0, g_mix, dx1, *, tm=512):
    m, d = x0.shape
    n = wt_in.shape[0]

    def body(du_ref, wt_ref, x0_ref, g_ref, dx1_ref, dx0_ref, dg_ref):
        _zero_at_first_step(dg_ref)
        dh0 = _dot_nn(du_ref[...], wt_ref[...])
        xh, r = _rms(x0_ref[...])
        dg_ref[...] += jnp.sum(dh0 * xh, axis=0, keepdims=True)
        dx0_ref[...] = dx1_ref[...] + _rms_bwd(dh0, xh, r, g_ref[...])

    return pl.pallas_call(
        body, name="in_bwd", grid=(m // tm,),
        in_specs=[_rows(tm, n), _whole((n, d)), _rows(tm, d), _vec(d), _rows(tm, d)],
        out_specs=[_rows(tm, d), _vec(d)],
        out_shape=[jax.ShapeDtypeStruct((m, d), F32), jax.ShapeDtypeStruct((1, d), F32)],
        compiler_params=_cp(),
    )(du, wt_in, x0, g_mix.reshape(1, d), dx1)


SWA_TILE = 512
SWA_SUB = SWA_TILE // BLOCK
SWA_KEYS = SWA_TILE + BLOCK
PAIR = 2 * HEAD_DIM
KCOL = ATTN_WIDTH
VCOL = ATTN_WIDTH + KV_WIDTH


def _swa_constants():
    r = np.arange(2 * BLOCK)[:, None]
    j = np.arange(4 * BLOCK)[None, :]
    dist = (r % BLOCK) + BLOCK - (j % (2 * BLOCK))
    valid = (dist >= 0) & (dist < BLOCK)
    first_valid = valid & ((j % (2 * BLOCK)) >= BLOCK)
    bias, bias_first = [], []
    for kv in range(N_KV_HEADS):
        head = kv * GROUP + 2 * (r // BLOCK) + j // (2 * BLOCK)
        b = -(2.0 ** -(head + 1.0)) * dist
        bias.append(np.where(valid, b, NEG))
        bias_first.append(np.where(first_valid, b, NEG))
    lane = np.arange(LANES)
    seg = (lane[:, None] // HEAD_DIM == lane[None, :] // HEAD_DIM) / HEAD_DIM
    row = np.arange(4 * BLOCK)[:, None]
    ones = (row // (2 * BLOCK)) == (lane[None, :] // HEAD_DIM)
    return (jnp.asarray(np.stack(bias), F32), jnp.asarray(np.stack(bias_first), F32), jnp.asarray(seg, BF16),
            jnp.asarray(ones, BF16))


def _segmean(x, seg_ref):
    hi = x.astype(BF16)
    lo = (x - hi.astype(F32)).astype(BF16)
    return _dot_nn(hi, seg_ref[...]) + _dot_nn(lo, seg_ref[...])


def _two_heads(x, kv):
    lane = lax.broadcasted_iota(jnp.int32, (1, LANES), 1)
    mine = (lane < HEAD_DIM) if kv == 0 else (lane >= HEAD_DIM)
    base = jnp.where(mine, x, 0.0)
    other = pltpu.roll(base, HEAD_DIM, 1)
    return jnp.concatenate([base, other] if kv == 0 else [other, base], axis=0)


def _from_two_heads(y, kv):
    rows = y.shape[0] // 2
    lane = lax.broadcasted_iota(jnp.int32, (1, LANES), 1)
    top, bot = y[:rows], y[rows:]
    if kv == 0:
        return jnp.where(lane < HEAD_DIM, top + pltpu.roll(bot, HEAD_DIM, 1), 0.0)
    return jnp.where(lane >= HEAD_DIM, pltpu.roll(top, HEAD_DIM, 1) + bot, 0.0)


def _pair_rows(ref, rows, kv):
    c = kv * 2 * PAIR
    return jnp.concatenate([ref[rows, c:c + PAIR], ref[rows, c + PAIR:c + 2 * PAIR]], axis=0)


def _head_cols(fn, kv):
    return [jnp.concatenate([fn(kv * GROUP + half), fn(kv * GROUP + 2 + half)], axis=0) for half in range(2)]


def _swa_prologue(cur_ref, prev_ref, qg_ref, kg_ref, seg_ref, qg_s, kn_s, v_s):
    qg_s[...] = (cur_ref[:, 0:ATTN_WIDTH] * qg_ref[...]).astype(BF16)
    k = jnp.concatenate([prev_ref[:, KCOL:KCOL + KV_WIDTH], cur_ref[:, KCOL:KCOL + KV_WIDTH]], axis=0)
    kn_s[...] = k * lax.rsqrt(_segmean(k * k, seg_ref) + EPS) * kg_ref[...]
    v_s[0:BLOCK, :] = prev_ref[:, VCOL:VCOL + KV_WIDTH]
    v_s[BLOCK:SWA_KEYS, :] = cur_ref[:, VCOL:VCOL + KV_WIDTH]


def _swa_products(qg_s, kn_s, rows, keys, kv):
    q2 = _pair_rows(qg_s, rows, kv)
    k2 = _two_heads(kn_s[keys, :], kv)
    return q2, k2, _dot_nt(q2, k2)


def _swa_scores(cur_ref, sinks_ref, qg_s, kn_s, bias, rows, keys, kv):
    q2, k2, t = _swa_products(qg_s, kn_s, rows, keys, kv)
    return q2, k2, t, _swa_softmax(cur_ref, sinks_ref, t, bias, rows, kv)


def _swa_softmax(cur_ref, sinks_ref, t, bias, rows, kv):
    def rq(h):
        x = cur_ref[rows, h * HEAD_DIM:(h + 1) * HEAD_DIM]
        return lax.rsqrt(jnp.mean(x * x, axis=-1, keepdims=True) + EPS)

    scale = _head_cols(lambda h: rq(h) * (HEAD_DIM ** -0.5), kv)
    sink = _head_cols(lambda h: jnp.full((BLOCK, 1), sinks_ref[h], F32), kv)
    halves = []
    for half in range(2):
        cols = slice(half * 2 * BLOCK, (half + 1) * 2 * BLOCK)
        s = t[:, cols] * scale[half] + bias[:, cols]
        mx = jnp.maximum(jnp.max(s, axis=-1, keepdims=True), sink[half])
        halves.append((scale[half], jnp.exp(s - mx), jnp.exp(sink[half] - mx)))
    return halves


def swa_fwd(u, qg, kg, sinks):
    t_rows = u.shape[0]
    nt = t_rows // SWA_TILE
    bias_c, bias_first_c, seg_c, ones_c = _swa_constants()

    def body(sinks_ref, cur_ref, prev_ref, qg_ref, kg_ref, seg_ref, bias_ref, biasf_ref, ones_ref, o_ref, qg_s, kn_s, v_s):
        i = pl.program_id(0)
        _swa_prologue(cur_ref, prev_ref, qg_ref, kg_ref, seg_ref, qg_s, kn_s, v_s)
        lane = lax.broadcasted_iota(jnp.int32, (1, LANES), 1)
        work = [(b, kv, slice(b * BLOCK, (b + 1) * BLOCK), slice(b * BLOCK, (b + 2) * BLOCK))
                for b in range(SWA_SUB) for kv in range(N_KV_HEADS)]
        products = [_swa_products(qg_s, kn_s, rows, keys, kv)[2] for _, kv, rows, keys in work]
        scored = []
        for (b, kv, rows, _), t in zip(work, products):
            bias = jnp.where(i == 0, biasf_ref[kv], bias_ref[kv]) if b == 0 else bias_ref[kv]
            halves = _swa_softmax(cur_ref, sinks_ref, t, bias, rows, kv)
            scored.append((jnp.concatenate([halves[0][1], halves[1][1]], axis=1).astype(BF16), halves[0][2], halves[1][2]))
        for (b, kv, rows, keys), (e, es0, es1) in zip(work, scored):
            v2 = jnp.concatenate([_two_heads(v_s[keys, :], kv).astype(BF16), ones_ref[...]], axis=1)
            ox = _dot_nn(e, v2)
            den = ox[:, LANES:] + jnp.where(lane < HEAD_DIM, es0, es1)
            out = (ox[:, :LANES] / den).astype(o_ref.dtype)
            c = kv * 2 * PAIR
            o_ref[rows, c:c + PAIR] = out[:BLOCK]
            o_ref[rows, c + PAIR:c + 2 * PAIR] = out[BLOCK:]

    const3 = pl.BlockSpec((N_KV_HEADS, 2 * BLOCK, 4 * BLOCK), lambda i: (0, 0, 0))
    return pl.pallas_call(
        body, name="swa_fwd", grid=(nt,),
        in_specs=[
            pl.BlockSpec(memory_space=pltpu.SMEM),
            pl.BlockSpec((SWA_TILE, QKV_WIDTH), lambda i: (i, 0)),
            pl.BlockSpec((BLOCK, QKV_WIDTH), lambda i: (jnp.maximum(i * SWA_SUB - 1, 0), 0)),
            pl.BlockSpec((1, ATTN_WIDTH), lambda i: (0, 0)), pl.BlockSpec((1, KV_WIDTH), lambda i: (0, 0)),
            pl.BlockSpec((LANES, LANES), lambda i: (0, 0)), const3, const3,
            pl.BlockSpec((4 * BLOCK, LANES), lambda i: (0, 0)),
        ],
        out_specs=pl.BlockSpec((SWA_TILE, ATTN_WIDTH), lambda i: (i, 0)),
        out_shape=jax.ShapeDtypeStruct((t_rows, 2 * ATTN_WIDTH), BF16),
        scratch_shapes=[pltpu.VMEM((SWA_TILE, ATTN_WIDTH), BF16), pltpu.VMEM((SWA_KEYS, KV_WIDTH), F32),
                        pltpu.VMEM((SWA_KEYS, KV_WIDTH), F32)],
        compiler_params=_cp(),
    )(sinks, u, u, jnp.tile(qg, N_Q_HEADS).reshape(1, ATTN_WIDTH), jnp.tile(kg, N_KV_HEADS).reshape(1, KV_WIDTH),
      seg_c, bias_c, bias_first_c, ones_c)


def swa_bwd(u, dmixed, qg, kg, sinks):
    t_rows = u.shape[0]
    nt = t_rows // SWA_TILE
    bias_c, bias_first_c, seg_c, _ = _swa_constants()

    def body(sinks_ref, cur_ref, prev_ref, do_ref, qg_ref, kg_ref, seg_ref, bias_ref, biasf_ref,
             du_ref, dqg_ref, dkg_ref, dsk_ref, qg_s, kn_s, v_s, acck_s, accv_s, carryk_s, carryv_s):
        step = pl.program_id(0)
        i = nt - 1 - step

        @pl.when(step == 0)
        def _():
            for r in (carryk_s, carryv_s, dqg_ref, dkg_ref, dsk_ref):
                r[...] = jnp.zeros_like(r)

        _swa_prologue(cur_ref, prev_ref, qg_ref, kg_ref, seg_ref, qg_s, kn_s, v_s)
        for acc, carry in ((acck_s, carryk_s), (accv_s, carryv_s)):
            acc[0:SWA_TILE, :] = jnp.zeros((SWA_TILE, KV_WIDTH), F32)
            acc[SWA_TILE:SWA_KEYS, :] = carry[...]

        lane = lax.broadcasted_iota(jnp.int32, (1, LANES), 1)
        g_pair = qg_ref[:, 0:PAIR]
        dqg_acc = jnp.zeros((1, PAIR), F32)
        dsk_acc = jnp.zeros((1, LANES), F32)
        work = [(b, kv, slice(b * BLOCK, (b + 1) * BLOCK), slice(b * BLOCK, (b + 2) * BLOCK))
                for b in range(SWA_SUB) for kv in range(N_KV_HEADS)]
        products = []
        for _, kv, rows, keys in work:
            q2, k2, t = _swa_products(qg_s, kn_s, rows, keys, kv)
            do2 = _pair_rows(do_ref, rows, kv).astype(BF16)
            products.append((q2, k2, t, do2, _dot_nt(do2, _two_heads(v_s[keys, :], kv))))
        exps = []
        for (b, kv, rows, _), (_, _, t, _, _) in zip(work, products):
            bias = jnp.where(i == 0, biasf_ref[kv], bias_ref[kv]) if b == 0 else bias_ref[kv]
            exps.append(_swa_softmax(cur_ref, sinks_ref, t, bias, rows, kv))
        softmaxed = []
        for (b, kv, rows, _), (_, _, t, _, dp), halves in zip(work, products, exps):
            p_parts, dt_parts, coef = [], [], []
            for half, (scale, e, es) in enumerate(halves):
                cols = slice(half * 2 * BLOCK, (half + 1) * 2 * BLOCK)
                rden = 1.0 / (jnp.sum(e, axis=-1, keepdims=True) + es)
                p = e * rden
                dp_h = dp[:, cols]
                delta = jnp.sum(p * dp_h, axis=-1, keepdims=True)
                ds = p * (dp_h - delta)
                dsink = -(es * rden) * delta
                for pair in range(2):
                    part = jnp.sum(dsink[pair * BLOCK:(pair + 1) * BLOCK], axis=0, keepdims=True)
                    dsk_acc = dsk_acc + jnp.where(lane == kv * GROUP + 2 * pair + half, part, 0.0)
                dscale = jnp.sum(ds * t[:, cols], axis=-1, keepdims=True)
                coef.append(-dscale * scale * scale * scale)
                p_parts.append(p.astype(BF16))
                dt_parts.append((ds * scale).astype(BF16))
            softmaxed.append((jnp.concatenate(p_parts, axis=1), jnp.concatenate(dt_parts, axis=1),
                              jnp.where(lane < HEAD_DIM, coef[0], coef[1])))
        for (_, kv, rows, keys), (q2, k2, _, do2, _), (p2, dt, coef) in zip(work, products, softmaxed):
            dqg2 = _dot_nn(dt, k2)
            q_raw = _pair_rows(cur_ref, rows, kv)
            dq = dqg2 * g_pair + coef * q_raw
            dqg_acc = dqg_acc + jnp.sum(dqg2 * q_raw, axis=0, keepdims=True)
            c = kv * 2 * PAIR
            du_ref[rows, c:c + PAIR] = dq[:BLOCK].astype(du_ref.dtype)
            du_ref[rows, c + PAIR:c + 2 * PAIR] = dq[BLOCK:].astype(du_ref.dtype)
        to_keys = [(_from_two_heads(_dot_tn(dt, q2), kv), _from_two_heads(_dot_tn(p2, do2), kv))
                   for (_, kv, _, _), (q2, _, _, do2, _), (p2, dt, _) in zip(work, products, softmaxed)]
        for (_, _, _, keys), (dk, dv) in zip(work, to_keys):
            acck_s[keys, :] += dk
            accv_s[keys, :] += dv
        dqg_ref[...] += dqg_acc + pltpu.roll(dqg_acc, HEAD_DIM, 1)
        dsk_ref[...] += dsk_acc

        own = slice(BLOCK, SWA_KEYS)
        k = cur_ref[:, KCOL:KCOL + KV_WIDTH]
        rk = lax.rsqrt(_segmean(k * k, seg_ref) + EPS)
        kh = k * rk
        dkn = acck_s[own, :]
        dkh = dkn * kg_ref[...]
        du_ref[:, KCOL:KCOL + KV_WIDTH] = (rk * (dkh - kh * _segmean(dkh * kh, seg_ref))).astype(du_ref.dtype)
        du_ref[:, VCOL:VCOL + KV_WIDTH] = accv_s[own, :].astype(du_ref.dtype)
        dkg_part = jnp.sum(dkn * kh, axis=0, keepdims=True)
        dkg_ref[...] += dkg_part + pltpu.roll(dkg_part, HEAD_DIM, 1)
        carryk_s[...] = acck_s[0:BLOCK, :]
        carryv_s[...] = accv_s[0:BLOCK, :]

    const3 = pl.BlockSpec((N_KV_HEADS, 2 * BLOCK, 4 * BLOCK), lambda s: (0, 0, 0))
    vec = pl.BlockSpec((1, LANES), lambda s: (0, 0))
    return pl.pallas_call(
        body, name="swa_bwd", grid=(nt,),
        in_specs=[
            pl.BlockSpec(memory_space=pltpu.SMEM),
            pl.BlockSpec((SWA_TILE, QKV_WIDTH), lambda s: (nt - 1 - s, 0)),
            pl.BlockSpec((BLOCK, QKV_WIDTH), lambda s: (jnp.maximum((nt - 1 - s) * SWA_SUB - 1, 0), 0)),
            pl.BlockSpec((SWA_TILE, ATTN_WIDTH), lambda s: (nt - 1 - s, 0)),
            pl.BlockSpec((1, ATTN_WIDTH), lambda s: (0, 0)), vec,
            pl.BlockSpec((LANES, LANES), lambda s: (0, 0)), const3, const3,
        ],
        out_specs=[pl.BlockSpec((SWA_TILE, QKV_WIDTH), lambda s: (nt - 1 - s, 0)), vec, vec, vec],
        out_shape=[jax.ShapeDtypeStruct((t_rows, IN_COLS), BF16)] + [jax.ShapeDtypeStruct((1, LANES), F32)] * 3,
        scratch_shapes=[pltpu.VMEM((SWA_TILE, ATTN_WIDTH), BF16)] + [pltpu.VMEM((SWA_KEYS, KV_WIDTH), F32)] * 4
        + [pltpu.VMEM((BLOCK, KV_WIDTH), F32)] * 2,
        compiler_params=_cp(),
    )(sinks, u, u, dmixed, jnp.tile(qg, N_Q_HEADS).reshape(1, ATTN_WIDTH), jnp.tile(kg, N_KV_HEADS).reshape(1, KV_WIDTH),
      seg_c, bias_c, bias_first_c)


CONV_TILE = 512
CONV_CHUNK = 64
VAL0 = QKV_WIDTH
GATE0 = QKV_WIDTH + CONV_CH


def _glu(ref):
    return ref[:, VAL0:GATE0] * _sigmoid(ref[:, GATE0:GATE0 + CONV_CH])


SUBLANES = 8
CONV_BUF = CONV_HALO + CONV_TILE + SUBLANES
CONV_EXT = CONV_HALO + CONV_TILE


def _fill_shifted(sh_ref):
    for r in range(1, SUBLANES):
        sh_ref[r, 0:CONV_EXT, :] = sh_ref[0, pl.ds(r, CONV_EXT), :]


def _shifted(sh_ref, start, offset, n):
    return sh_ref[offset % SUBLANES, pl.ds(start + offset - offset % SUBLANES, n), :]


def _layernorm_stats(y):
    mu = jnp.mean(y, axis=-1, keepdims=True)
    yc = y - mu
    rstd = lax.rsqrt(jnp.mean(yc * yc, axis=-1, keepdims=True) + EPS)
    return yc * rstd, rstd


def conv_fwd(u, mixed, conv_w, conv_b, ln_g, ln_b):
    t = u.shape[0]
    nt = t // CONV_TILE
    per = CONV_TILE // CONV_HALO

    def body(cur_ref, prev_ref, mixed_ref, w_ref, b_ref, g_ref, b2_ref, o_ref, y_ref, gl_ref):
        del mixed_ref
        i = pl.program_id(0)
        gl_ref[0, 0:CONV_HALO, :] = jnp.where(i > 0, _glu(prev_ref), 0.0)
        gl_ref[0, CONV_HALO:CONV_EXT, :] = _glu(cur_ref)
        gl_ref[0, CONV_EXT:CONV_BUF, :] = jnp.zeros((SUBLANES, CONV_CH), F32)
        _fill_shifted(gl_ref)
        for c0 in range(0, CONV_TILE, CONV_CHUNK):
            acc = jnp.broadcast_to(b_ref[...], (CONV_CHUNK, CONV_CH))
            for k in range(CONV_K):
                acc = acc + w_ref[k:k + 1, :] * _shifted(gl_ref, c0, 2 + k, CONV_CHUNK)
            y_ref[c0:c0 + CONV_CHUNK, :] = acc
        yh, _ = _layernorm_stats(y_ref[...])
        yln = yh * g_ref[...] + b2_ref[...]
        o_ref[...] = (yln * _sigmoid(yln)).astype(o_ref.dtype)

    vec = pl.BlockSpec((1, CONV_CH), lambda i: (0, 0))
    return pl.pallas_call(
        body, name="conv_fwd", grid=(nt,),
        in_specs=[
            pl.BlockSpec((CONV_TILE, IN_COLS), lambda i: (i, 0)),
            pl.BlockSpec((CONV_HALO, IN_COLS), lambda i: (jnp.maximum(i * per - 1, 0), 0)),
            pl.BlockSpec(memory_space=pl.ANY),
            pl.BlockSpec((CONV_HALO, CONV_CH), lambda i: (0, 0)),
            vec, vec, vec,
        ],
        out_specs=[pl.BlockSpec((CONV_TILE, CONV_CH), lambda i: (i, 1)), pl.BlockSpec((CONV_TILE, CONV_CH), lambda i: (i, 0))],
        out_shape=[jax.ShapeDtypeStruct(mixed.shape, mixed.dtype), jax.ShapeDtypeStruct((t, CONV_CH), F32)],
        scratch_shapes=[pltpu.VMEM((SUBLANES, CONV_BUF, CONV_CH), F32)],
        input_output_aliases={2: 0}, compiler_params=_cp(),
    )(u, u, mixed, conv_w, conv_b.reshape(1, CONV_CH), ln_g.reshape(1, CONV_CH), ln_b.reshape(1, CONV_CH))


def conv_bwd(u, y, dmixed, du, conv_w, ln_g, ln_b):
    t = u.shape[0]
    nt = t // CONV_TILE
    per = CONV_TILE // CONV_HALO

    def body(cur_ref, prev_ref, y_ref, yn_ref, do_ref, don_ref, du_in_ref, w_ref, g_ref, b2_ref,
             du_ref, dw_ref, dvec_ref, gl_ref, dy_ref):
        i = pl.program_id(0)
        last = i == nt - 1
        _zero_at_first_step(dw_ref, dvec_ref)

        gl_ref[0, 0:CONV_HALO, :] = jnp.where(i > 0, _glu(prev_ref), 0.0)
        gl_ref[0, CONV_HALO:CONV_EXT, :] = _glu(cur_ref)
        gl_ref[0, CONV_EXT:CONV_BUF, :] = jnp.zeros((SUBLANES, CONV_CH), F32)
        _fill_shifted(gl_ref)

        yh, rstd = _layernorm_stats(jnp.concatenate([y_ref[...], yn_ref[...]], axis=0))
        g = g_ref[...]
        yln = yh * g + b2_ref[...]
        sg = _sigmoid(yln)
        dout = jnp.concatenate([do_ref[...], jnp.where(last, 0.0, don_ref[...])], axis=0)
        dyln = dout * (sg * (1.0 + yln * (1.0 - sg)))
        dyh = dyln * g
        dy = rstd * (dyh - jnp.mean(dyh, axis=-1, keepdims=True) - yh * jnp.mean(dyh * yh, axis=-1, keepdims=True))
        dy_ref[0, 0:CONV_EXT, :] = dy
        dy_ref[0, CONV_EXT:CONV_BUF, :] = jnp.zeros((SUBLANES, CONV_CH), F32)
        _fill_shifted(dy_ref)

        own = slice(0, CONV_TILE)
        dvec_ref[0:1, :] += jnp.sum(dy[own], axis=0, keepdims=True)
        dvec_ref[1:2, :] += jnp.sum(dyln[own] * yh[own], axis=0, keepdims=True)
        dvec_ref[2:3, :] += jnp.sum(dyln[own], axis=0, keepdims=True)
        for k in range(CONV_K):
            dw_ref[k:k + 1, :] += jnp.sum(dy[own] * _shifted(gl_ref, 0, 2 + k, CONV_TILE), axis=0, keepdims=True)

        for c0 in range(0, CONV_TILE, CONV_CHUNK):
            acc = jnp.zeros((CONV_CHUNK, CONV_CH), F32)
            for k in range(CONV_K):
                acc = acc + w_ref[k:k + 1, :] * _shifted(dy_ref, c0, CONV_K - 1 - k, CONV_CHUNK)
            rows = slice(c0, c0 + CONV_CHUNK)
            val = cur_ref[rows, VAL0:GATE0]
            sgate = _sigmoid(cur_ref[rows, GATE0:GATE0 + CONV_CH])
            du_ref[rows, VAL0:GATE0] = (acc * sgate).astype(du_ref.dtype)
            du_ref[rows, GATE0:GATE0 + CONV_CH] = (acc * val * sgate * (1.0 - sgate)).astype(du_ref.dtype)
        du_ref[:, 0:QKV_WIDTH] = du_in_ref[:, 0:QKV_WIDTH]

    vec = pl.BlockSpec((1, CONV_CH), lambda i: (0, 0))
    n_halo = t // CONV_HALO
    return pl.pallas_call(
        body, name="conv_bwd", grid=(nt,),
        in_specs=[
            pl.BlockSpec((CONV_TILE, IN_COLS), lambda i: (i, 0)),
            pl.BlockSpec((CONV_HALO, IN_COLS), lambda i: (jnp.maximum(i * per - 1, 0), 0)),
            pl.BlockSpec((CONV_TILE, CONV_CH), lambda i: (i, 0)),
            pl.BlockSpec((CONV_HALO, CONV_CH), lambda i: (jnp.minimum((i + 1) * per, n_halo - 1), 0)),
            pl.BlockSpec((CONV_TILE, CONV_CH), lambda i: (i, 1)),
            pl.BlockSpec((CONV_HALO, CONV_CH), lambda i: (jnp.minimum((i + 1) * per, n_halo - 1), 1)),
            pl.BlockSpec((CONV_TILE, IN_COLS), lambda i: (i, 0)),
            pl.BlockSpec((CONV_HALO, CONV_CH), lambda i: (0, 0)),
            vec, vec,
        ],
        out_specs=[
            pl.BlockSpec((CONV_TILE, IN_COLS), lambda i: (i, 0)),
            pl.BlockSpec((CONV_HALO, CONV_CH), lambda i: (0, 0)),
            pl.BlockSpec((8, CONV_CH), lambda i: (0, 0)),
        ],
        out_shape=[
            jax.ShapeDtypeStruct(du.shape, du.dtype),
            jax.ShapeDtypeStruct((CONV_HALO, CONV_CH), F32),
            jax.ShapeDtypeStruct((8, CONV_CH), F32),
        ],
        scratch_shapes=[pltpu.VMEM((SUBLANES, CONV_BUF, CONV_CH), F32), pltpu.VMEM((SUBLANES, CONV_BUF, CONV_CH), F32)],
        input_output_aliases={6: 0}, compiler_params=_cp(),
    )(u, u, y, y, dmixed, dmixed, du, conv_w, ln_g.reshape(1, CONV_CH), ln_b.reshape(1, CONV_CH))


def adamw(w, g, m, v, *, name):
    r, c = w.shape
    tr = r
    for cand in (512, 256, 128, 64, 32, 16, 8):
        if r % cand == 0 and r > cand:
            tr = cand
            break

    def body(w_ref, g_ref, m_ref, v_ref, d_ref, nm_ref, nv_ref):
        d_ref[...], nm_ref[...], nv_ref[...] = _adamw_math(w_ref[...], g_ref[...], m_ref[...], v_ref[...])

    spec = pl.BlockSpec((tr, c), lambda i: (i, 0))
    shape = jax.ShapeDtypeStruct((r, c), F32)
    return pl.pallas_call(
        body, name=name, grid=(r // tr,), in_specs=[spec] * 4, out_specs=[spec] * 3,
        out_shape=[shape] * 3, compiler_params=_cp(),
    )(w, g, m, v)


def _position():
    return lax.axis_index("x"), lax.axis_index("y"), lax.axis_index("c")


def all_gather_many(shards, *, name):
    n = len(shards)

    def body(*refs):
        x_refs, out_refs, token_ref = refs[:n], refs[n:2 * n], refs[2 * n]
        send_sems, recv_sems, local_sems = refs[2 * n + 1:]
        x, y, c = _position()
        me, sibling = (x, y, c), (x, y, 1 - c)
        chips = [(1 - x, y), (x, 1 - y), (1 - x, 1 - y)]
        token_ref[...] = jnp.zeros_like(token_ref)

        def rows(t, px, py, pc):
            return out_refs[t].at[4 * px + 2 * py + pc]

        def copy(t, k, block, to, src=None):
            return pltpu.make_async_remote_copy(
                src_ref=rows(t, *block) if src is None else src, dst_ref=rows(t, *block),
                send_sem=send_sems.at[7 * t + k], recv_sem=recv_sems.at[7 * t + k], device_id=to, device_id_type=MESH)

        mine = [pltpu.make_async_copy(x_refs[t], rows(t, *me), local_sems.at[t]) for t in range(n)]
        for cp in mine:
            cp.start()
        first = []
        for t in range(n):
            first.append(copy(t, 0, me, sibling, src=x_refs[t]))
            first += [copy(t, 1 + j, me, (*chip, c), src=x_refs[t]) for j, chip in enumerate(chips)]
        for cp in first:
            cp.start()
        passed = []
        for t in range(n):
            for j, chip in enumerate(chips):
                copy(t, 1 + j, (*chip, c), me).wait_recv()
                passed.append(copy(t, 4 + j, (*chip, c), sibling))
                passed[-1].start()
        for t in range(n):
            copy(t, 0, sibling, me).wait_recv()
            for j, chip in enumerate(chips):
                copy(t, 4 + j, (*chip, 1 - c), me).wait_recv()
        for cp in first + passed:
            cp.wait_send()
        for cp in mine:
            cp.wait()

    hbm = pl.BlockSpec(memory_space=pltpu.HBM)
    out = pl.pallas_call(
        body, name=name,
        out_shape=[jax.ShapeDtypeStruct((N_DEV,) + s.shape, s.dtype) for s in shards] + [jax.ShapeDtypeStruct((8, LANES), F32)],
        in_specs=[hbm] * n, out_specs=[hbm] * n + [pl.BlockSpec(memory_space=pltpu.VMEM)],
        scratch_shapes=[pltpu.SemaphoreType.DMA((7 * n,)), pltpu.SemaphoreType.DMA((7 * n,)), pltpu.SemaphoreType.DMA((n,))],
        compiler_params=_cp(),
    )(*shards)
    return out[:n], out[n]


_HBM = pl.BlockSpec(memory_space=pltpu.HBM)
_SEM = pl.BlockSpec(memory_space=pltpu.SEMAPHORE)
_EFFECT = pltpu.SideEffectType.DATAFLOW_SIDE_EFFECTING


def _split_copies(src_refs, land_refs, send_sems, recv_sems, plan, n_copies):
    copies = []
    for t, (src_ref, land_ref) in enumerate(zip(src_refs, land_refs)):
        for k in range(n_copies):
            s, d, to = plan(src_ref, land_ref, k)
            copies.append(pltpu.make_async_remote_copy(
                src_ref=s, dst_ref=d, send_sem=send_sems.at[n_copies * t + k], recv_sem=recv_sems.at[n_copies * t + k],
                device_id=to, device_id_type=MESH))
    return copies


def split_start(srcs, lands, plan, n_copies, *, name):
    n = len(srcs)

    def body(*refs):
        src_refs, land_refs, send_sems, recv_sems, token = refs[:n], refs[n:2 * n], refs[2 * n], refs[2 * n + 1], refs[-1]
        for cp in _split_copies(src_refs, land_refs, send_sems, recv_sems, plan, n_copies):
            cp.start()
        token[...] = jnp.zeros_like(token)

    both = list(srcs) + list(lands)
    out = pl.pallas_call(
        body, name=name,
        out_shape=(pltpu.SemaphoreType.DMA((n_copies * n,)), pltpu.SemaphoreType.DMA((n_copies * n,)),
                   *[pltpu.HBM(a.shape, a.dtype) for a in both], jax.ShapeDtypeStruct((8, LANES), F32)),
        in_specs=(_HBM,) * (2 * n), out_specs=(_SEM, _SEM) + (_HBM,) * (2 * n) + (pl.BlockSpec(memory_space=pltpu.VMEM),),
        input_output_aliases={i: 2 + i for i in range(2 * n)},
        compiler_params=pltpu.CompilerParams(has_side_effects=_EFFECT),
    )(*[pltpu.with_memory_space_constraint(a, pltpu.HBM) for a in both])
    return out[0], out[1], list(out[2:2 + n]), list(out[2 + n:2 + 2 * n]), out[-1]


def split_wait(started, after, plan, n_copies, *, name):
    send_sems, recv_sems, srcs, lands, _ = started
    n = len(srcs)

    def body(*refs):
        src_refs, land_refs, send_sems, recv_sems = refs[:n], refs[n:2 * n], refs[2 * n], refs[2 * n + 1]
        for cp in _split_copies(src_refs, land_refs, send_sems, recv_sems, plan, n_copies):
            cp.wait_send()
            cp.wait_recv()

    both = list(srcs) + list(lands)
    out = pl.pallas_call(
        body, name=name,
        out_shape=tuple(pltpu.HBM(a.shape, a.dtype) for a in both),
        in_specs=(_HBM,) * (2 * n) + (_SEM, _SEM, pl.BlockSpec(memory_space=pl.ANY)), out_specs=(_HBM,) * (2 * n),
        input_output_aliases={i: i for i in range(2 * n)},
        compiler_params=pltpu.CompilerParams(has_side_effects=_EFFECT),
    )(*both, send_sems, recv_sems, after)
    return list(out[:n]), list(out[n:])


def _other_chips(x, y):
    return [(1 - x, y), (x, 1 - y), (1 - x, 1 - y)]


def _remote(src, dst, send_sem, recv_sem, to):
    return pltpu.make_async_remote_copy(src_ref=src, dst_ref=dst, send_sem=send_sem, recv_sem=recv_sem,
                                        device_id=to, device_id_type=MESH)


def gather_start(groups, *, name):
    counts = [len(shards) for shards, _ in groups]
    flat = [a for shards, _ in groups for a in shards] + [a for _, lands in groups for a in lands]
    n_all, n_groups = sum(counts), len(groups)

    def body(*refs):
        s_refs, l_refs = refs[:n_all], refs[n_all:2 * n_all]
        sems = refs[2 * n_all:2 * n_all + 3 * n_groups]
        x, y, c = _position()
        me = 4 * x + 2 * y + c
        at = 0
        for gi, n in enumerate(counts):
            send, recv_sibling, recv_ici = sems[3 * gi:3 * gi + 3]
            for t in range(n):
                src, dst = s_refs[at + t], l_refs[at + t].at[me]
                _remote(src, dst, send.at[4 * t], recv_sibling.at[t], (x, y, 1 - c)).start()
                for j, chip in enumerate(_other_chips(x, y)):
                    _remote(src, dst, send.at[4 * t + 1 + j], recv_ici.at[3 * t + j], (*chip, c)).start()
            at += n
        refs[-1][...] = jnp.zeros_like(refs[-1])

    sem_shapes = [pltpu.SemaphoreType.DMA((k * n,)) for n in counts for k in (4, 1, 3)]
    out = pl.pallas_call(
        body, name=name,
        out_shape=(*sem_shapes, *[pltpu.HBM(a.shape, a.dtype) for a in flat], jax.ShapeDtypeStruct((8, LANES), F32)),
        in_specs=(_HBM,) * (2 * n_all),
        out_specs=(_SEM,) * (3 * n_groups) + (_HBM,) * (2 * n_all) + (pl.BlockSpec(memory_space=pltpu.VMEM),),
        input_output_aliases={i: 3 * n_groups + i for i in range(2 * n_all)},
        compiler_params=pltpu.CompilerParams(has_side_effects=_EFFECT),
    )(*[pltpu.with_memory_space_constraint(a, pltpu.HBM) for a in flat])
    thru = out[3 * n_groups:-1]
    states, at = [], 0
    for gi, n in enumerate(counts):
        states.append(dict(shards=list(thru[at:at + n]), lands=list(thru[n_all + at:n_all + at + n]),
                           send=out[3 * gi], recv_sibling=out[3 * gi + 1], recv_ici=out[3 * gi + 2]))
        at += n
    return states, out[-1]


def gather_forward(states, after, *, name):
    counts = [len(s["lands"]) for s in states]
    flat = [a for s in states for a in s["lands"]]
    n_all, n_groups = sum(counts), len(states)

    def body(*refs):
        l_refs = refs[:n_all]
        recv_ici = refs[n_all:n_all + n_groups]
        fwd = refs[n_all + n_groups + 1:n_all + n_groups + 1 + 2 * n_groups]
        x, y, c = _position()
        at = 0
        for gi, n in enumerate(counts):
            fwd_send, fwd_recv = fwd[2 * gi], fwd[2 * gi + 1]
            for t in range(n):
                for j, (px, py) in enumerate(_other_chips(x, y)):
                    block = l_refs[at + t].at[4 * px + 2 * py + c]
                    _remote(block, block, fwd_send.at[3 * t + j], recv_ici[gi].at[3 * t + j], (px, py, c)).wait_recv()
                    _remote(block, block, fwd_send.at[3 * t + j], fwd_recv.at[3 * t + j], (x, y, 1 - c)).start()
            at += n
        refs[-1][...] = jnp.zeros_like(refs[-1])

    sem_shapes = [pltpu.SemaphoreType.DMA((3 * n,)) for n in counts for _ in range(2)]
    out = pl.pallas_call(
        body, name=name,
        out_shape=(*sem_shapes, *[pltpu.HBM(a.shape, a.dtype) for a in flat], jax.ShapeDtypeStruct((8, LANES), F32)),
        in_specs=(_HBM,) * n_all + (_SEM,) * n_groups + (pl.BlockSpec(memory_space=pl.ANY),),
        out_specs=(_SEM,) * (2 * n_groups) + (_HBM,) * n_all + (pl.BlockSpec(memory_space=pltpu.VMEM),),
        input_output_aliases={i: 2 * n_groups + i for i in range(n_all)},
        compiler_params=pltpu.CompilerParams(has_side_effects=_EFFECT),
    )(*flat, *[s["recv_ici"] for s in states], after)
    at = 0
    for gi, (s, n) in enumerate(zip(states, counts)):
        s.update(fwd_send=out[2 * gi], fwd_recv=out[2 * gi + 1], lands=list(out[2 * n_groups + at:2 * n_groups + at + n]))
        at += n
    return out[-1]


def gather_finish(state, after, *, name):
    n = len(state["lands"])

    def body(*refs):
        s_refs, l_refs = refs[:n], refs[n:2 * n]
        send, recv_sibling, fwd_send, fwd_recv = refs[2 * n:2 * n + 4]
        x, y, c = _position()
        me = 4 * x + 2 * y + c
        for t in range(n):
            own = l_refs[t].at[me]
            _remote(s_refs[t], own, send.at[4 * t], recv_sibling.at[t], (x, y, 1 - c)).wait_send()
            _remote(s_refs[t], l_refs[t].at[4 * x + 2 * y + 1 - c], send.at[4 * t], recv_sibling.at[t], (x, y, 1 - c)).wait_recv()
            for j, (px, py) in enumerate(_other_chips(x, y)):
                _remote(s_refs[t], own, send.at[4 * t + 1 + j], recv_sibling.at[t], (px, py, c)).wait_send()
                mine, theirs = l_refs[t].at[4 * px + 2 * py + c], l_refs[t].at[4 * px + 2 * py + 1 - c]
                _remote(mine, mine, fwd_send.at[3 * t + j], fwd_recv.at[3 * t + j], (x, y, 1 - c)).wait_send()
                _remote(theirs, theirs, fwd_send.at[3 * t + j], fwd_recv.at[3 * t + j], (x, y, 1 - c)).wait_recv()

    both = state["shards"] + state["lands"]
    out = pl.pallas_call(
        body, name=name,
        out_shape=tuple(pltpu.HBM(a.shape, a.dtype) for a in both),
        in_specs=(_HBM,) * (2 * n) + (_SEM,) * 4 + (pl.BlockSpec(memory_space=pl.ANY),), out_specs=(_HBM,) * (2 * n),
        input_output_aliases={i: i for i in range(2 * n)},
        compiler_params=pltpu.CompilerParams(has_side_effects=_EFFECT),
    )(*both, state["send"], state["recv_sibling"], state["fwd_send"], state["fwd_recv"], after)
    return list(out[n:])


def _all_peers_plan(src_ref, land_ref, k):
    x, y, c = _position()
    bits = k + 1
    peer = ((1 - x) if bits & 4 else x, (1 - y) if bits & 2 else y, (1 - c) if bits & 1 else c)
    return src_ref, land_ref.at[4 * x + 2 * y + c], peer


def _sibling_plan(src_ref, land_ref, k):
    x, y, c = _position()
    return src_ref.at[2 * k + (1 - c)], land_ref.at[k], (x, y, 1 - c)


def _chips_plan(src_ref, land_ref, j):
    x, y, c = _position()
    px, py = _other_chips(x, y)[j]
    return src_ref.at[j], land_ref.at[j], (px, py, c)


SUM_STEPS = 2


def sum_for_chips(parts, from_sibling, ck_idx, *, name):
    n = len(parts)

    def body(ck_ref, *refs):
        del ck_ref
        for t in range(n):
            refs[2 * n + t][...] = (refs[t][...] + refs[n + t][...]).astype(BF16)

    def blk(a):
        return (None, a.shape[1] // SUM_STEPS, a.shape[2])

    return pl.pallas_call(
        body, name=name,
        grid_spec=pltpu.PrefetchScalarGridSpec(
            num_scalar_prefetch=1, grid=(3, SUM_STEPS),
            in_specs=[pl.BlockSpec(blk(a), lambda j, i, ck: (2 * ck[1 + j] + ck[0], i, 0)) for a in parts]
            + [pl.BlockSpec(blk(a), lambda j, i, ck: (ck[1 + j], i, 0)) for a in from_sibling],
            out_specs=[pl.BlockSpec(blk(a), lambda j, i, ck: (j, i, 0)) for a in from_sibling]),
        out_shape=[jax.ShapeDtypeStruct((3,) + a.shape[1:], BF16) for a in from_sibling], compiler_params=_cp(),
    )(ck_idx, *parts, *from_sibling)


def sum_final(parts, from_sibling, from_chips, kc_idx, *, name):
    n = len(parts)

    def body(kc_ref, *refs):
        del kc_ref
        for t in range(n):
            p, s, a, b, d = (refs[j * n + t] for j in range(5))
            refs[5 * n + t][...] = (((p[...] + s[...]) + a[...].astype(F32)) + b[...].astype(F32)) + d[...].astype(F32)

    def blk(a):
        return (None, a.shape[1] // SUM_STEPS, a.shape[2])

    def chip_specs(j):
        return [pl.BlockSpec(blk(a), lambda i, kc: (j, i, 0)) for a in from_chips]

    return pl.pallas_call(
        body, name=name,
        grid_spec=pltpu.PrefetchScalarGridSpec(
            num_scalar_prefetch=1, grid=(SUM_STEPS,),
            in_specs=[pl.BlockSpec(blk(a), lambda i, kc: (2 * kc[0] + kc[1], i, 0)) for a in parts]
            + [pl.BlockSpec(blk(a), lambda i, kc: (kc[0], i, 0)) for a in from_sibling]
            + chip_specs(0) + chip_specs(1) + chip_specs(2),
            out_specs=[pl.BlockSpec(blk(a)[1:], lambda i, kc: (i, 0)) for a in parts]),
        out_shape=[jax.ShapeDtypeStruct(a.shape[1:], F32) for a in parts], compiler_params=_cp(),
    )(kc_idx, *parts, *from_sibling, *from_chips, *from_chips, *from_chips)


BIG = (
    ("w_in", IN_COLS, True), ("w_out", D_MODEL, False), ("wq_x", D_MODEL, False), ("wkv_x", 2 * D_MODEL, True),
    ("wo_x", D_MODEL, False), ("w_gate_up", 2 * D_FF, True), ("w_down", D_FF, False),
)

SMALL = ("norm_mix_g", "q_norm_g", "k_norm_g", "sinks", "conv_b", "conv_ln_g", "conv_ln_b",
         "norm_x_g", "norm_mem_g", "xq_norm_g", "xk_norm_g", "norm_ffn_g")


ADAMW_STEPS = 8


def adamw_many(ws, gs, ms, vs, *, name):
    n = len(ws)

    def body(*refs):
        for t in range(n):
            w, g, m, v = (refs[j * n + t] for j in range(4))
            d_ref, nm_ref, nv_ref = (refs[(4 + j) * n + t] for j in range(3))
            d_ref[...], nm_ref[...], nv_ref[...] = _adamw_math(w[...], g[...], m[...], v[...])

    specs = [pl.BlockSpec((a.shape[0] // ADAMW_STEPS, a.shape[1]), lambda i: (i, 0)) for a in ws]
    shapes = [jax.ShapeDtypeStruct(a.shape, F32) for a in ws]
    out = pl.pallas_call(
        body, name=name, grid=(ADAMW_STEPS,), in_specs=specs * 4, out_specs=specs * 3, out_shape=shapes * 3,
        compiler_params=_cp(),
    )(*ws, *gs, *ms, *vs)
    return out[:n], out[n:2 * n], out[2 * n:]


def _adamw_math(w, g, m, v):
    m2 = ADAM_B1 * m + (1.0 - ADAM_B1) * g
    v2 = ADAM_B2 * v + (1.0 - ADAM_B2) * jnp.square(g)
    m_hat = m2 / (1.0 - ADAM_B1 ** ADAM_STEP)
    v_hat = v2 / (1.0 - ADAM_B2 ** ADAM_STEP)
    return -ADAM_LR * (m_hat / (jnp.sqrt(v_hat) + ADAM_EPS) + ADAM_WD * w), m2, v2


def _small_rows(per_layer_shape):
    return 1 if len(per_layer_shape) == 1 else per_layer_shape[0]


def pack_small(parts, shapes):
    blocks = []
    for per_layer, sh in zip(parts, shapes):
        for g in per_layer:
            g = g.reshape(_small_rows(sh), sh[-1])
            blocks.append(jnp.pad(g, ((0, 0), (0, D_MODEL - sh[-1]))))
    rows = sum(b.shape[0] for b in blocks)
    blocks.append(jnp.zeros((-rows % 8, D_MODEL), F32))
    return jnp.concatenate(blocks, axis=0)


def update_small(gathered, shapes, weights, moments_m, moments_v, n_update):
    n_all = len(shapes)

    def body(*refs):
        g_ref = refs[0]
        w_refs, m_refs, v_refs = (refs[1 + j * n_update:1 + (j + 1) * n_update] for j in range(3))
        out = refs[1 + 3 * n_update:]
        grad_refs = out[:n_all]
        d_refs, nm_refs, nv_refs = (out[n_all + j * n_update:n_all + (j + 1) * n_update] for j in range(3))
        at = 0
        for p, sh in enumerate(shapes):
            rows, lanes = _small_rows(sh), sh[-1]
            for l in range(DEPTH):
                g = g_ref[0, at:at + rows, 0:lanes]
                for k in range(1, N_DEV):
                    g = g + g_ref[k, at:at + rows, 0:lanes]
                at += rows
                here = (slice(l, l + 1),) + (slice(None),) * (len(sh) - 1) if len(sh) == 1 else (l,)
                grad_refs[p][here] = g
                if p < n_update:
                    d, m2, v2 = _adamw_math(w_refs[p][here], g, m_refs[p][here], v_refs[p][here])
                    d_refs[p][here] = d
                    nm_refs[p][here] = m2
                    nv_refs[p][here] = v2

    full = [jax.ShapeDtypeStruct((DEPTH,) + tuple(sh), F32) for sh in shapes]
    out = pl.pallas_call(
        body, name="update_small", out_shape=full + full[:n_update] * 3, compiler_params=_cp(),
    )(gathered, *weights, *moments_m, *moments_v)
    return (out[:n_all], out[n_all:n_all + n_update], out[n_all + n_update:n_all + 2 * n_update],
            out[n_all + 2 * n_update:])


WEIGHT_GROUPS = {"in": ("w_in",), "mid": ("w_out", "wq_x", "wkv_x", "wo_x"), "ffn": ("w_gate_up", "w_down")}


def _layer_fwd(x0, mem, weights_of, s, reached, target=None):
    w = dict(weights_of("in", x0))
    h0, u = norm_proj(x0, s["norm_mix_g"], w["w_in"])
    mixed = swa_fwd(u, s["q_norm_g"], s["k_norm_g"], s["sinks"])
    reached("attn", mixed)
    mixed, conv_y = conv_fwd(u, mixed, s["conv_w"], s["conv_b"], s["conv_ln_g"], s["conv_ln_b"])
    w.update(weights_of("mid", conv_y))
    memn, kv = kv_fwd(mem, s["norm_mem_g"], w["wkv_x"])
    x1, h1, qx, o, x2, h2 = mid_fwd(mixed, x0, w["w_out"], s["norm_x_g"], w["wq_x"], kv, s["xq_norm_g"], s["xk_norm_g"],
                                    w["wo_x"], s["norm_ffn_g"])
    reached("mid", x2)
    w.update(weights_of("ffn", x2))
    gu, a, *out = ffn_fwd(h2, x2, w["w_gate_up"], w["w_down"], target)
    saved = dict(x0=x0, h0=h0, u=u, conv_y=conv_y, mixed=mixed, x1=x1, h1=h1, qx=qx, memn=memn, kv=kv, o=o, x2=x2, h2=h2,
                 gu=gu, a=a)
    return out, saved, w


def _ordered_after(a, token):
    return a if token is None else a + token[0, 0]


def _layer_bwd(dx3, mem, w, s, sv, token, stage_done):
    gs = {}
    dgu, dx2, dg = ffn_bwd(dx3, sv["gu"], sv["x2"], _ordered_after(s["norm_ffn_g"], token), w["w_down"], w["w_gate_up"])
    gs["norm_ffn_g"] = dg
    gb = {"w_down": mm_tn(sv["a"], dx3, name="mm_dw_down")}
    gb["w_gate_up"] = mm_tn(dgu, sv["h2"], tk=dgu.shape[0], name="mm_dw_gate_up")
    token = stage_done("ffn", gb, gb["w_gate_up"])

    gb = {}
    dq, dx1, dmixed, dkv, dqg, dkg, dg = mid_bwd(dx2, sv["qx"], sv["kv"], s["xq_norm_g"], s["xk_norm_g"], sv["x1"],
                                                 _ordered_after(s["norm_x_g"], token), w["wo_x"], w["wq_x"], w["w_out"])
    gs["xq_norm_g"], gs["xk_norm_g"], gs["norm_x_g"] = dqg, dkg, dg
    gb["wo_x"] = mm_tn(sv["o"], dx2, name="mm_dwo")
    gb["wq_x"] = mm_tn(sv["h1"], dq, name="mm_dwq")
    gb["wkv_x"], gs["norm_mem_g"] = kv_bwd(dkv, sv["memn"], mem, w["wkv_x"])
    gb["w_out"] = mm_tn(sv["mixed"], dx1, name="mm_dw_out")
    token = stage_done("mid", gb, gb["w_out"])

    du, dqg, dkg, dsinks = swa_bwd(sv["u"], dmixed, _ordered_after(s["q_norm_g"], token), s["k_norm_g"], s["sinks"])
    gs["q_norm_g"], gs["k_norm_g"], gs["sinks"] = dqg[0, :HEAD_DIM], dkg[0, :HEAD_DIM], dsinks[0, :N_Q_HEADS]
    token = stage_done("attn", {}, dqg)
    du, dconv_w, dvec = conv_bwd(sv["u"], sv["conv_y"], dmixed, du, s["conv_w"], _ordered_after(s["conv_ln_g"], token),
                                 s["conv_ln_b"])
    gs["conv_w"] = dconv_w[:CONV_K]
    gs["conv_b"], gs["conv_ln_g"], gs["conv_ln_b"] = dvec[0], dvec[1], dvec[2]
    dw_in = mm_tn(du, sv["h0"], tk=2048, name="mm_dw_in")
    token = stage_done("in", {"w_in": dw_in}, dw_in)
    dx0, dg = in_bwd(du, w["w_in"], sv["x0"], _ordered_after(s["norm_mix_g"], token), dx1)
    gs["norm_mix_g"] = dg
    token = stage_done("mix", {}, dx0)
    return dx0, gs, token


def _local_step(x, mem, target, weights_of, reached, smalls, stage_done):
    saved, weights = [], []
    out = [x]
    for l in range(DEPTH):
        out, sv, w = _layer_fwd(out[0], mem, functools.partial(weights_of, l), smalls[l], functools.partial(reached, l),
                                target if l == DEPTH - 1 else None)
        saved.append(sv)
        weights.append(w)
    dx, loss_part = out
    gss, token = [None] * DEPTH, None
    for l in reversed(range(DEPTH)):
        dx, gss[l], token = _layer_bwd(dx, mem, weights[l], smalls[l], saved[l], token,
                                       functools.partial(stage_done, l))
    return loss_part[0, 0], dx, gss


def kernel(x, mem, norm_mix_g, w_in, q_norm_g, k_norm_g, sinks, conv_w, conv_b, conv_ln_g, conv_ln_b, w_out, norm_x_g, norm_mem_g, wq_x, wkv_x, xq_norm_g, xk_norm_g, wo_x, norm_ffn_g, w_gate_up, w_down, loss_target, m_norm_mix_g, m_w_in, m_q_norm_g, m_k_norm_g, m_sinks, m_conv_w, m_conv_b, m_conv_ln_g, m_conv_ln_b, m_w_out, m_norm_x_g, m_norm_mem_g, m_wq_x, m_wkv_x, m_xq_norm_g, m_xk_norm_g, m_wo_x, m_norm_ffn_g, m_w_gate_up, m_w_down, v_norm_mix_g, v_w_in, v_q_norm_g, v_k_norm_g, v_sinks, v_conv_w, v_conv_b, v_conv_ln_g, v_conv_ln_b, v_w_out, v_norm_x_g, v_norm_mem_g, v_wq_x, v_wkv_x, v_xq_norm_g, v_xk_norm_g, v_wo_x, v_norm_ffn_g, v_w_gate_up, v_w_down):
    P = dict(norm_mix_g=norm_mix_g, w_in=w_in, q_norm_g=q_norm_g, k_norm_g=k_norm_g, sinks=sinks, conv_w=conv_w, conv_b=conv_b,
             conv_ln_g=conv_ln_g, conv_ln_b=conv_ln_b, w_out=w_out, norm_x_g=norm_x_g, norm_mem_g=norm_mem_g, wq_x=wq_x,
             wkv_x=wkv_x, xq_norm_g=xq_norm_g, xk_norm_g=xk_norm_g, wo_x=wo_x, norm_ffn_g=norm_ffn_g, w_gate_up=w_gate_up,
             w_down=w_down)
    M = dict(norm_mix_g=m_norm_mix_g, w_in=m_w_in, q_norm_g=m_q_norm_g, k_norm_g=m_k_norm_g, sinks=m_sinks, conv_w=m_conv_w,
             conv_b=m_conv_b, conv_ln_g=m_conv_ln_g, conv_ln_b=m_conv_ln_b, w_out=m_w_out, norm_x_g=m_norm_x_g,
             norm_mem_g=m_norm_mem_g, wq_x=m_wq_x, wkv_x=m_wkv_x, xq_norm_g=m_xq_norm_g, xk_norm_g=m_xk_norm_g, wo_x=m_wo_x,
             norm_ffn_g=m_norm_ffn_g, w_gate_up=m_w_gate_up, w_down=m_w_down)
    V = dict(norm_mix_g=v_norm_mix_g, w_in=v_w_in, q_norm_g=v_q_norm_g, k_norm_g=v_k_norm_g, sinks=v_sinks, conv_w=v_conv_w,
             conv_b=v_conv_b, conv_ln_g=v_conv_ln_g, conv_ln_b=v_conv_ln_b, w_out=v_w_out, norm_x_g=v_norm_x_g,
             norm_mem_g=v_norm_mem_g, wq_x=v_wq_x, wkv_x=v_wkv_x, xq_norm_g=v_xq_norm_g, xk_norm_g=v_xk_norm_g, wo_x=v_wo_x,
             norm_ffn_g=v_norm_ffn_g, w_gate_up=v_w_gate_up, w_down=v_w_down)
    order = ["norm_mix_g", "w_in", "q_norm_g", "k_norm_g", "sinks", "conv_w", "conv_b", "conv_ln_g", "conv_ln_b", "w_out",
             "norm_x_g", "norm_mem_g", "wq_x", "wkv_x", "xq_norm_g", "xk_norm_g", "wo_x", "norm_ffn_g", "w_gate_up", "w_down"]
    xi, yi, ci = _position()
    dev = 4 * xi + 2 * yi + ci
    x2d, mem2d, tgt2d = x[0], mem[0], loss_target[0]

    def travelling(name, l, transposed):
        a = P[name][l]
        return (a.T if transposed else a).astype(BF16)

    rows_of = {n: rows for n, rows, _ in BIG}
    transposed_of = {n: tr for n, _, tr in BIG}

    def whole(names, gathered):
        return {n: g.reshape(rows_of[n], D_MODEL) for n, g in zip(names, gathered)}

    cw = jnp.pad(conv_w.reshape(DEPTH * CONV_K, CONV_CH // N_DEV), ((0, 2), (0, LANES - CONV_CH // N_DEV)))
    (w_in0, cw_all), token0 = all_gather_many([travelling("w_in", 0, True), cw], name="ag_w_in0_conv_w")
    travel_order = [(0, "mid"), (0, "ffn"), (1, "in"), (1, "mid"), (1, "ffn")]
    travel_groups = []
    for l, group in travel_order:
        shards = [_ordered_after(travelling(n, l, transposed_of[n]), token0.astype(BF16)) for n in WEIGHT_GROUPS[group]]
        lands = [lax.dynamic_update_slice(lax.empty((N_DEV,) + s.shape, BF16), s[None], (dev, 0, 0)) for s in shards]
        travel_groups.append((shards, lands))
    travel_states, travel_token = gather_start(travel_groups, name="ag_weights_start")
    travelling_state = dict(zip(travel_order, travel_states))
    forward_at = {(0, "attn"): [(0, "mid")], (0, "mid"): [(0, "ffn"), (1, "in")], (1, "attn"): [(1, "mid"), (1, "ffn")]}

    def reached(l, stage, marker):
        keys = forward_at.get((l, stage))
        if keys:
            gather_forward([travelling_state[k] for k in keys], marker,
                           name="ag_weights_forward_" + "_".join(f"{g}{ll}" for ll, g in keys))

    def weights_of(l, group, marker):
        if (l, group) == (0, "in"):
            return whole(WEIGHT_GROUPS[group], [w_in0])
        gathered = gather_finish(travelling_state[(l, group)], marker, name=f"ag_weights_finish_{group}{l}")
        return whole(WEIGHT_GROUPS[group], gathered)

    cw_full = cw_all[:, :DEPTH * CONV_K, :CONV_CH // N_DEV].reshape(N_DEV, DEPTH, CONV_K, CONV_CH // N_DEV)
    cw_full = jnp.transpose(cw_full, (1, 2, 0, 3)).reshape(DEPTH, CONV_K, CONV_CH)
    smalls = []
    for l in range(DEPTH):
        sl = {n: P[n][l] if n == "sinks" else P[n][l:l + 1] for n in SMALL}
        sl["conv_w"] = jnp.pad(cw_full[l], ((0, CONV_HALO - CONV_K), (0, 0)))
        smalls.append(sl)
    smalls[0]["norm_mix_g"] = _ordered_after(smalls[0]["norm_mix_g"], travel_token)

    ck_idx = jnp.stack([ci] + [2 * px + py for px, py in _other_chips(xi, yi)]).astype(jnp.int32)
    kc_idx = jnp.stack([2 * xi + yi, ci]).astype(jnp.int32)
    got, flight, reduced = {}, {}, {}

    def as_parts(gb):
        keys = sorted(gb)
        return keys, [gb[k].reshape(N_DEV, rows_of[k[1]] // N_DEV, D_MODEL) for k in keys]

    def lands_like(parts, blocks, dtype):
        return [lax.empty((blocks,) + p.shape[1:], dtype) for p in parts]

    def to_sibling(group, gb):
        keys, parts = as_parts(gb)
        flight[group] = (keys, split_start(parts, lands_like(parts, 4, F32), _sibling_plan, 4,
                                           name=f"rs_sibling_{group}_start"))
        return flight[group][1][4]

    def to_chips(group, marker):
        keys, started = flight[group]
        parts, from_sibling = split_wait(started, marker, _sibling_plan, 4, name=f"rs_sibling_{group}_wait")
        chip_sums = sum_for_chips(parts, from_sibling, ck_idx, name=f"rs_sum_for_chips_{group}")
        started = split_start(chip_sums, lands_like(parts, 3, BF16), _chips_plan, 3, name=f"rs_chips_{group}_start")
        flight[group] = (keys, parts, from_sibling, started)
        return started[4]

    def finish(group, marker):
        keys, parts, from_sibling, started = flight[group]
        _, from_chips = split_wait(started, marker, _chips_plan, 3, name=f"rs_chips_{group}_wait")
        reduced.update(zip(keys, sum_final(parts, from_sibling, from_chips, kc_idx, name=f"rs_sum_final_{group}")))

    def stage_done(l, stage, gb, marker):
        gb = {(l, n): g for n, g in gb.items()}
        if l == 1:
            got.update(gb)
            return to_sibling("l1", got) if stage == "mix" else None
        if stage == "ffn":
            return to_chips("l1", marker) + to_sibling("ffn", gb)
        if stage == "mid":
            return to_chips("ffn", marker) + to_sibling("mid", gb)
        if stage == "attn":
            return to_chips("mid", marker)
        if stage == "in":
            return to_sibling("in", gb)
        to_chips("in", marker)
        for group in ("l1", "ffn", "mid"):
            finish(group, marker)
        return None

    loss_part, grad_x, gss = _local_step(x2d, mem2d, tgt2d, weights_of, reached, smalls, stage_done)
    loss = lax.psum(loss_part, ("x", "y", "c"))

    small_names = SMALL + ("conv_w",)
    small_shapes = [(CONV_K, CONV_CH) if n == "conv_w" else P[n].shape[1:] for n in small_names]
    small_parts = pack_small([[gss[l][n] for l in range(DEPTH)] for n in small_names], small_shapes)
    small_land = lax.dynamic_update_slice(lax.empty((N_DEV,) + small_parts.shape, F32), small_parts[None], (dev, 0, 0))
    small_flight = split_start([small_parts], [small_land], _all_peers_plan, N_DEV - 1, name="ag_small_grads_start")

    grads, delta, new_m, new_v = {}, {}, {}, {}

    def update(names):
        two_d = lambda n, a: a.reshape(P[n].shape[0] * P[n].shape[1], P[n].shape[2])
        for n in names:
            grads[n] = jnp.stack([reduced[(l, n)].T if transposed_of[n] else reduced[(l, n)] for l in range(DEPTH)])
        d_, m_, v_ = adamw_many(*[[two_d(n, src[n]) for n in names] for src in (P, grads, M, V)],
                                name="adamw_" + "_".join(names))
        for i, n in enumerate(names):
            delta[n], new_m[n], new_v[n] = (a[i].reshape(P[n].shape) for a in (d_, m_, v_))

    update([n for n, _, _ in BIG if n != "w_in"])
    finish("in", delta["w_down"])
    update(["w_in"])
    small_all = split_wait(small_flight, delta["w_in"], _all_peers_plan, N_DEV - 1, name="ag_small_grads_wait")[1][0]
    g_, d_, m_, v_ = update_small(small_all, small_shapes, [P[n] for n in SMALL], [M[n] for n in SMALL],
                                  [V[n] for n in SMALL], len(SMALL))
    for i, n in enumerate(SMALL):
        grads[n], delta[n], new_m[n], new_v[n] = g_[i], d_[i], m_[i], v_[i]
    cols = CONV_CH // N_DEV
    grads["conv_w"] = lax.dynamic_slice_in_dim(g_[-1], dev * cols, cols, axis=2)
    flat = lambda a: a.reshape(DEPTH * CONV_K, cols)
    d_, m_, v_ = adamw(flat(conv_w), flat(grads["conv_w"]), flat(m_conv_w), flat(v_conv_w), name="adamw_conv_w")
    delta["conv_w"], new_m["conv_w"], new_v["conv_w"] = (a.reshape(conv_w.shape) for a in (d_, m_, v_))

    return (loss, grad_x[None], *[grads[n] for n in order], *[delta[n] for n in order],
            *[new_m[n] for n in order], *[new_v[n] for n in order])
```

```python
import functools

import jax
import jax.numpy as jnp
import numpy as np
from jax import lax
from jax.experimental import pallas as pl
from jax.experimental.pallas import tpu as pltpu

F32 = jnp.float32
BF16 = jnp.bfloat16

D_MODEL = 1024
HEAD_DIM = 64
N_Q_HEADS = 8
N_KV_HEADS = 2
GROUP = N_Q_HEADS // N_KV_HEADS
ATTN_WIDTH = N_Q_HEADS * HEAD_DIM
KV_WIDTH = N_KV_HEADS * HEAD_DIM
QKV_WIDTH = ATTN_WIDTH + 2 * KV_WIDTH
CONV_CH = 512
IN_COLS = QKV_WIDTH + 2 * CONV_CH
CONV_K = 31
CONV_HALO = 32
BLOCK = 128
N_X_HEADS = 4
X_HEAD_DIM = 256
D_FF = 2816
EPS = 1e-6
NEG = -1e30
DEPTH = 2
N_DEV = 8

ADAM_LR = 0.001
ADAM_B1 = 0.9
ADAM_B2 = 0.999
ADAM_EPS = 1e-08
ADAM_WD = 0.01
ADAM_STEP = 10

V7X_VMEM_LIMIT = 56 * 1024 * 1024
LANES = 128

MESH = pl.DeviceIdType.MESH


def _cp(**kw):
    return pltpu.CompilerParams(vmem_limit_bytes=V7X_VMEM_LIMIT, **kw)


def _dot(a, b, dims):
    return lax.dot_general(a.astype(BF16), b.astype(BF16), (dims, ((), ())), preferred_element_type=F32)


def _dot_nn(a, b):
    return _dot(a, b, ((1,), (0,)))


def _dot_nt(a, b):
    return _dot(a, b, ((1,), (1,)))


def _dot_tn(a, b):
    return _dot(a, b, ((0,), (0,)))


def _sigmoid(x):
    return jax.nn.sigmoid(x)


def _rms(x):
    r = lax.rsqrt(jnp.mean(x * x, axis=-1, keepdims=True) + EPS)
    return x * r, r


def _rms_bwd(dy, xhat, r, g):
    dxh = dy * g
    return r * (dxh - xhat * jnp.mean(dxh * xhat, axis=-1, keepdims=True))


KV_STEPS = 2


def kv_fwd(mem, g, wt_kv):
    n_mem, d = mem.shape
    rows = wt_kv.shape[0] // KV_STEPS

    def body(mem_ref, g_ref, wt_ref, memn_ref, kv_ref):
        @pl.when(pl.program_id(0) == 0)
        def _():
            memn_ref[...] = (_rms(mem_ref[...])[0] * g_ref[...]).astype(memn_ref.dtype)

        kv_ref[...] = _dot_nt(memn_ref[...], wt_ref[...])

    whole = pl.BlockSpec((n_mem, d), lambda j: (0, 0))
    return pl.pallas_call(
        body, name="kv_fwd", grid=(KV_STEPS,),
        in_specs=[whole, pl.BlockSpec((1, d), lambda j: (0, 0)), pl.BlockSpec((rows, d), lambda j: (j, 0))],
        out_specs=[whole, pl.BlockSpec((n_mem, rows), lambda j: (0, j))],
        out_shape=[jax.ShapeDtypeStruct((n_mem, d), BF16), jax.ShapeDtypeStruct((n_mem, wt_kv.shape[0]), F32)],
        compiler_params=_cp(),
    )(mem, g.reshape(1, d), wt_kv)


def kv_bwd(dkv, memn, mem, wt_kv):
    n_mem, d = mem.shape
    rows = wt_kv.shape[0] // KV_STEPS

    def body(dkv_ref, memn_ref, mem_ref, wt_ref, dwt_ref, dg_ref, dmemn_s):
        j = pl.program_id(0)
        dwt_ref[...] = _dot_tn(dkv_ref[...], memn_ref[...])
        part = _dot_nn(dkv_ref[...], wt_ref[...])

        @pl.when(j == 0)
        def _():
            dmemn_s[...] = part

        @pl.when(j > 0)
        def _():
            dmemn_s[...] += part

        @pl.when(j == KV_STEPS - 1)
        def _():
            dg_ref[...] = jnp.sum(dmemn_s[...] * _rms(mem_ref[...])[0], axis=0, keepdims=True)

    whole = pl.BlockSpec((n_mem, d), lambda j: (0, 0))
    return pl.pallas_call(
        body, name="kv_bwd", grid=(KV_STEPS,),
        in_specs=[pl.BlockSpec((n_mem, rows), lambda j: (0, j)), whole, whole, pl.BlockSpec((rows, d), lambda j: (j, 0))],
        out_specs=[pl.BlockSpec((rows, d), lambda j: (j, 0)), pl.BlockSpec((1, d), lambda j: (0, 0))],
        out_shape=[jax.ShapeDtypeStruct(wt_kv.shape, F32), jax.ShapeDtypeStruct((1, d), F32)],
        scratch_shapes=[pltpu.VMEM((n_mem, d), F32)], compiler_params=_cp(),
    )(dkv, memn, mem, wt_kv)


def _tile(n, cap):
    if n <= cap:
        return n
    best = None
    for t in range(LANES, cap + 1, LANES):
        if n % t == 0:
            best = t
    assert best is not None, (n, cap)
    return best


def mm_tn(a, b, *, name, ta_cap=1536, tb_cap=1024, tk=1024):
    m, ka = a.shape
    nb = b.shape[1]
    assert b.shape[0] == m
    tk = min(tk, m)
    ta = _tile(ka, ta_cap)
    tb = _tile(nb, tb_cap)

    def body(a_ref, b_ref, o_ref):
        @pl.when(pl.program_id(2) == 0)
        def _():
            o_ref[...] = jnp.zeros_like(o_ref)

        o_ref[...] += _dot_tn(a_ref[...], b_ref[...])

    return pl.pallas_call(
        body, name=name, grid=(ka // ta, nb // tb, m // tk),
        in_specs=[pl.BlockSpec((tk, ta), lambda i, j, kk: (kk, i)), pl.BlockSpec((tk, tb), lambda i, j, kk: (kk, j))],
        out_specs=pl.BlockSpec((ta, tb), lambda i, j, kk: (i, j)),
        out_shape=jax.ShapeDtypeStruct((ka, nb), F32), compiler_params=_cp(),
    )(a, b)


def _whole(shape):
    return pl.BlockSpec(shape, lambda i: (0,) * len(shape), pipeline_mode=pl.Buffered(1))


def _rows(tm, n):
    return pl.BlockSpec((tm, n), lambda i: (i, 0))


def _vec(n):
    return pl.BlockSpec((1, n), lambda i: (0, 0))


def _chunks(n, cap=1408):
    size = _tile(n, cap)
    return [(s, size) for s in range(0, n, size)]


def _zero_at_first_step(*refs):
    @pl.when(pl.program_id(0) == 0)
    def _():
        for r in refs:
            r[...] = jnp.zeros_like(r)


def norm_proj(x, g, wt, *, tm=1024):
    m, d = x.shape
    n = wt.shape[0]
    tm = min(tm, m)

    def body(x_ref, g_ref, wt_ref, h_ref, u_ref):
        h = (_rms(x_ref[...])[0] * g_ref[...]).astype(BF16)
        h_ref[...] = h
        for s, sz in _chunks(n):
            u_ref[:, s:s + sz] = _dot_nt(h, wt_ref[s:s + sz, :])

    return pl.pallas_call(
        body, name="norm_proj", grid=(m // tm,),
        in_specs=[_rows(tm, d), _vec(d), _whole((n, d))],
        out_specs=[_rows(tm, d), _rows(tm, n)],
        out_shape=[jax.ShapeDtypeStruct((m, d), BF16), jax.ShapeDtypeStruct((m, n), F32)],
        compiler_params=_cp(),
    )(x, g.reshape(1, d), wt)


def _xattn_heads(q_ref, kv_ref, qg_v, kg_v, d):
    normed = []
    for h in range(N_X_HEADS):
        cols = slice(h * X_HEAD_DIM, (h + 1) * X_HEAD_DIM)
        qh, rq = _rms(q_ref[:, cols])
        normed.append((qh, rq, (qh * qg_v).astype(BF16), (_rms(kv_ref[:, cols])[0] * kg_v).astype(BF16),
                       kv_ref[:, d + h * X_HEAD_DIM:d + (h + 1) * X_HEAD_DIM].astype(BF16)))
    scores = [_dot_nt(qn, kn) * (X_HEAD_DIM ** -0.5) for _, _, qn, kn, _ in normed]
    out = []
    for (qh, rq, qn, kn, v), s in zip(normed, scores):
        e = jnp.exp(s - jnp.max(s, axis=-1, keepdims=True))
        out.append((qh, rq, qn, kn, v, e / jnp.sum(e, axis=-1, keepdims=True)))
    return out


def mid_fwd(mixed, x0, w_out, g_x, wq, kv, xqg, xkg, wo, g_f, *, tm=512):
    m, d = x0.shape
    n_mem = kv.shape[0]

    def body(mixed_ref, x0_ref, w_out_ref, g_x_ref, wq_ref, kv_ref, xqg_ref, xkg_ref, wo_ref, g_f_ref,
             x1_ref, h1_ref, qx_ref, o_ref, x2_ref, h2_ref):
        x1 = x0_ref[...] + _dot_nn(mixed_ref[...], w_out_ref[...])
        x1_ref[...] = x1
        h1 = (_rms(x1)[0] * g_x_ref[...]).astype(BF16)
        h1_ref[...] = h1
        qx_ref[...] = _dot_nn(h1, wq_ref[...])
        for h, (_, _, _, _, v, p) in enumerate(_xattn_heads(qx_ref, kv_ref, xqg_ref[...], xkg_ref[...], d)):
            o_ref[:, h * X_HEAD_DIM:(h + 1) * X_HEAD_DIM] = _dot_nn(p, v).astype(o_ref.dtype)
        x2 = x1 + _dot_nn(o_ref[...], wo_ref[...])
        x2_ref[...] = x2
        h2_ref[...] = (_rms(x2)[0] * g_f_ref[...]).astype(BF16)

    sq = _whole((d, d))
    f32_rows, bf_rows = jax.ShapeDtypeStruct((m, d), F32), jax.ShapeDtypeStruct((m, d), BF16)
    return pl.pallas_call(
        body, name="mid_fwd", grid=(m // tm,),
        in_specs=[_rows(tm, d), _rows(tm, d), sq, _vec(d), sq, _whole((n_mem, 2 * d)), _vec(X_HEAD_DIM), _vec(X_HEAD_DIM),
                  sq, _vec(d)],
        out_specs=[_rows(tm, d)] * 6,
        out_shape=[f32_rows, bf_rows, f32_rows, bf_rows, f32_rows, bf_rows],
        compiler_params=_cp(),
    )(mixed, x0, w_out, g_x.reshape(1, d), wq, kv, xqg.reshape(1, X_HEAD_DIM), xkg.reshape(1, X_HEAD_DIM), wo,
      g_f.reshape(1, d))


def ffn_fwd(h2, x2, wt_gu, w_down, target=None, *, tm=256):
    m, d = x2.shape
    f = w_down.shape[0]
    with_loss = target is not None

    def body(*refs):
        if with_loss:
            h2_ref, x2_ref, wt_gu_ref, w_down_ref, t_ref, gu_ref, a_ref, dy_ref, l_ref = refs
        else:
            h2_ref, x2_ref, wt_gu_ref, w_down_ref, gu_ref, a_ref, x3_ref = refs
        h = h2_ref[...]
        for s, sz in _chunks(2 * f):
            gu_ref[:, s:s + sz] = _dot_nt(h, wt_gu_ref[s:s + sz, :])
        for s, sz in _chunks(f):
            g = gu_ref[:, s:s + sz]
            a_ref[:, s:s + sz] = (g * _sigmoid(g) * gu_ref[:, f + s:f + s + sz]).astype(a_ref.dtype)
        x3 = x2_ref[...] + _dot_nn(a_ref[...], w_down_ref[...])
        if not with_loss:
            x3_ref[...] = x3
            return
        err = x3 - t_ref[...]
        dy_ref[...] = err * (1.0 / d)
        _zero_at_first_step(l_ref)
        part = jnp.sum(jnp.sum(err * err, axis=-1, keepdims=True), axis=0, keepdims=True)
        l_ref[...] += jnp.broadcast_to(part * (0.5 / d), l_ref.shape)

    last = [_rows(tm, d), pl.BlockSpec((1, LANES), lambda i: (0, 0))] if with_loss else [_rows(tm, d)]
    last_shape = [jax.ShapeDtypeStruct((m, d), F32)] + ([jax.ShapeDtypeStruct((1, LANES), F32)] if with_loss else [])
    return pl.pallas_call(
        body, name="ffn_fwd_loss" if with_loss else "ffn_fwd", grid=(m // tm,),
        in_specs=[_rows(tm, d), _rows(tm, d), _whole((2 * f, d)), _whole((f, d))] + ([_rows(tm, d)] if with_loss else []),
        out_specs=[_rows(tm, 2 * f), _rows(tm, f)] + last,
        out_shape=[jax.ShapeDtypeStruct((m, 2 * f), F32), jax.ShapeDtypeStruct((m, f), BF16)] + last_shape,
        compiler_params=_cp(),
    )(*([h2, x2, wt_gu, w_down] + ([target] if with_loss else [])))


def ffn_bwd(dx3, gu, x2, g_f, w_down, wt_gu, *, tm=256):
    m, d = x2.shape
    f = w_down.shape[0]

    def body(dx3_ref, gu_ref, x2_ref, g_ref, w_down_ref, wt_gu_ref, dgu_ref, dx2_ref, dg_ref):
        _zero_at_first_step(dg_ref)
        dx3 = dx3_ref[...]
        dx3_b = dx3.astype(BF16)
        for s, sz in _chunks(f):
            da = _dot_nt(dx3_b, w_down_ref[s:s + sz, :])
            g = gu_ref[:, s:s + sz]
            u = gu_ref[:, f + s:f + s + sz]
            sg = _sigmoid(g)
            dgu_ref[:, s:s + sz] = (da * u * (sg * (1.0 + g * (1.0 - sg)))).astype(dgu_ref.dtype)
            dgu_ref[:, f + s:f + s + sz] = (da * (g * sg)).astype(dgu_ref.dtype)
        dh2 = _dot_nn(dgu_ref[...], wt_gu_ref[...])
        xh, r = _rms(x2_ref[...])
        dg_ref[...] += jnp.sum(dh2 * xh, axis=0, keepdims=True)
        dx2_ref[...] = dx3 + _rms_bwd(dh2, xh, r, g_ref[...])

    return pl.pallas_call(
        body, name="ffn_bwd", grid=(m // tm,),
        in_specs=[_rows(tm, d), _rows(tm, 2 * f), _rows(tm, d), _vec(d), _whole((f, d)), _whole((2 * f, d))],
        out_specs=[_rows(tm, 2 * f), _rows(tm, d), _vec(d)],
        out_shape=[jax.ShapeDtypeStruct((m, 2 * f), BF16), jax.ShapeDtypeStruct((m, d), F32),
                   jax.ShapeDtypeStruct((1, d), F32)],
        compiler_params=_cp(),
    )(dx3, gu, x2, g_f.reshape(1, d), w_down, wt_gu)


def mid_bwd(dx2, qx, kv, xqg, xkg, x1, g_x, wo, wq, w_out, *, tm=512):
    m, d = x1.shape
    n_mem = kv.shape[0]
    nt = m // tm

    def body(dx2_ref, qx_ref, kv_ref, xqg_ref, xkg_ref, x1_ref, g_x_ref, wo_ref, wq_ref, w_out_ref,
             dq_ref, dx1_ref, dmixed_ref, dkv_ref, dqg_ref, dkg_ref, dg_ref):
        i = pl.program_id(0)
        _zero_at_first_step(dkv_ref, dqg_ref, dkg_ref, dg_ref)
        qg_v, kg_v = xqg_ref[...], xkg_ref[...]
        dx2 = dx2_ref[...]
        do = _dot_nt(dx2, wo_ref[...])
        dqg_acc = jnp.zeros((1, X_HEAD_DIM), F32)
        heads = _xattn_heads(qx_ref, kv_ref, qg_v, kg_v, d)
        head_cols = [slice(h * X_HEAD_DIM, (h + 1) * X_HEAD_DIM) for h in range(N_X_HEADS)]
        do_h = [do[:, cols].astype(BF16) for cols in head_cols]
        dps = [_dot_nt(do_h[h], heads[h][4]) for h in range(N_X_HEADS)]
        dss = []
        for (_, _, _, _, _, p), dp in zip(heads, dps):
            dss.append((p.astype(BF16), (p * (dp - jnp.sum(p * dp, axis=-1, keepdims=True))).astype(BF16)))
        for h, ((qh, rq, qn, kn, _, _), (p, ds)) in enumerate(zip(heads, dss)):
            cols = head_cols[h]
            vcols = slice(d + h * X_HEAD_DIM, d + (h + 1) * X_HEAD_DIM)
            dkv_ref[:, vcols] += _dot_tn(p, do_h[h])
            dqn = _dot_nn(ds, kn) * (X_HEAD_DIM ** -0.5)
            dkv_ref[:, cols] += _dot_tn(ds, qn) * (X_HEAD_DIM ** -0.5)
            dqg_acc = dqg_acc + jnp.sum(dqn * qh, axis=0, keepdims=True)
            dq_ref[:, cols] = _rms_bwd(dqn, qh, rq, qg_v).astype(dq_ref.dtype)
        dqg_ref[...] += dqg_acc
        dh1 = _dot_nt(dq_ref[...], wq_ref[...])
        xh, r = _rms(x1_ref[...])
        dg_ref[...] += jnp.sum(dh1 * xh, axis=0, keepdims=True)
        dx1 = dx2 + _rms_bwd(dh1, xh, r, g_x_ref[...])
        dx1_ref[...] = dx1
        dmixed_ref[...] = _dot_nt(dx1, w_out_ref[...])

        @pl.when(i == nt - 1)
        def _():
            dkg_acc = jnp.zeros((1, X_HEAD_DIM), F32)
            for h in range(N_X_HEADS):
                cols = slice(h * X_HEAD_DIM, (h + 1) * X_HEAD_DIM)
                kh, rk = _rms(kv_ref[:, cols])
                dkn = dkv_ref[:, cols]
                dkg_acc = dkg_acc + jnp.sum(dkn * kh, axis=0, keepdims=True)
                dkv_ref[:, cols] = _rms_bwd(dkn, kh, rk, kg_v)
            dkg_ref[...] = dkg_acc

    sq = _whole((d, d))
    full = pl.BlockSpec((n_mem, 2 * d), lambda i: (0, 0))
    return pl.pallas_call(
        body, name="mid_bwd", grid=(nt,),
        in_specs=[_rows(tm, d), _rows(tm, d), _whole((n_mem, 2 * d)), _vec(X_HEAD_DIM), _vec(X_HEAD_DIM), _rows(tm, d),
                  _vec(d), sq, sq, sq],
        out_specs=[_rows(tm, d), _rows(tm, d), _rows(tm, d), full, _vec(X_HEAD_DIM), _vec(X_HEAD_DIM), _vec(d)],
        out_shape=[jax.ShapeDtypeStruct((m, d), BF16), jax.ShapeDtypeStruct((m, d), F32), jax.ShapeDtypeStruct((m, d), F32),
                   jax.ShapeDtypeStruct((n_mem, 2 * d), F32), jax.ShapeDtypeStruct((1, X_HEAD_DIM), F32),
                   jax.ShapeDtypeStruct((1, X_HEAD_DIM), F32), jax.ShapeDtypeStruct((1, d), F32)],
        compiler_params=_cp(),
    )(dx2, qx, kv, xqg.reshape(1, X_HEAD_DIM), xkg.reshape(1, X_HEAD_DIM), x1, g_x.reshape(1, d), wo, wq, w_out)


def in_bwd(du, wt_in, x0, g_mix, dx1, *, tm=1024):
    m, d = x0.shape
    n = wt_in.shape[0]
    tm = min(tm, m)

    def body(du_ref, wt_ref, x0_ref, g_ref, dx1_ref, dx0_ref, dg_ref):
        _zero_at_first_step(dg_ref)
        dh0 = _dot_nn(du_ref[...], wt_ref[...])
        xh, r = _rms(x0_ref[...])
        dg_ref[...] += jnp.sum(dh0 * xh, axis=0, keepdims=True)
        dx0_ref[...] = dx1_ref[...] + _rms_bwd(dh0, xh, r, g_ref[...])

    return pl.pallas_call(
        body, name="in_bwd", grid=(m // tm,),
        in_specs=[_rows(tm, n), _whole((n, d)), _rows(tm, d), _vec(d), _rows(tm, d)],
        out_specs=[_rows(tm, d), _vec(d)],
        out_shape=[jax.ShapeDtypeStruct((m, d), F32), jax.ShapeDtypeStruct((1, d), F32)],
        compiler_params=_cp(),
    )(du, wt_in, x0, g_mix.reshape(1, d), dx1)


SWA_TILE = 512
SWA_SUB = SWA_TILE // BLOCK
SWA_KEYS = SWA_TILE + BLOCK
PAIR = 2 * HEAD_DIM
KCOL = ATTN_WIDTH
VCOL = ATTN_WIDTH + KV_WIDTH


def _swa_constants():
    r = np.arange(2 * BLOCK)[:, None]
    j = np.arange(4 * BLOCK)[None, :]
    dist = (r % BLOCK) + BLOCK - (j % (2 * BLOCK))
    valid = (dist >= 0) & (dist < BLOCK)
    first_valid = valid & ((j % (2 * BLOCK)) >= BLOCK)
    bias, bias_first = [], []
    for kv in range(N_KV_HEADS):
        head = kv * GROUP + 2 * (r // BLOCK) + j // (2 * BLOCK)
        b = -(2.0 ** -(head + 1.0)) * dist
        bias.append(np.where(valid, b, NEG))
        bias_first.append(np.where(first_valid, b, NEG))
    lane = np.arange(LANES)
    seg = (lane[:, None] // HEAD_DIM == lane[None, :] // HEAD_DIM) / HEAD_DIM
    row = np.arange(4 * BLOCK)[:, None]
    ones = (row // (2 * BLOCK)) == (lane[None, :] // HEAD_DIM)
    return (jnp.asarray(np.stack(bias), F32), jnp.asarray(np.stack(bias_first), F32), jnp.asarray(seg, BF16),
            jnp.asarray(ones, BF16))


def _segmean(x, seg_ref):
    hi = x.astype(BF16)
    lo = (x - hi.astype(F32)).astype(BF16)
    return _dot_nn(hi, seg_ref[...]) + _dot_nn(lo, seg_ref[...])


def _two_heads(x, kv):
    lane = lax.broadcasted_iota(jnp.int32, (1, LANES), 1)
    mine = (lane < HEAD_DIM) if kv == 0 else (lane >= HEAD_DIM)
    base = jnp.where(mine, x, 0.0)
    other = pltpu.roll(base, HEAD_DIM, 1)
    return jnp.concatenate([base, other] if kv == 0 else [other, base], axis=0)


def _from_two_heads(y, kv):
    rows = y.shape[0] // 2
    lane = lax.broadcasted_iota(jnp.int32, (1, LANES), 1)
    top, bot = y[:rows], y[rows:]
    if kv == 0:
        return jnp.where(lane < HEAD_DIM, top + pltpu.roll(bot, HEAD_DIM, 1), 0.0)
    return jnp.where(lane >= HEAD_DIM, pltpu.roll(top, HEAD_DIM, 1) + bot, 0.0)


def _pair_rows(ref, rows, kv):
    c = kv * 2 * PAIR
    return jnp.concatenate([ref[rows, c:c + PAIR], ref[rows, c + PAIR:c + 2 * PAIR]], axis=0)


def _head_cols(fn, kv):
    return [jnp.concatenate([fn(kv * GROUP + half), fn(kv * GROUP + 2 + half)], axis=0) for half in range(2)]


def _swa_prologue(cur_ref, prev_ref, qg_ref, kg_ref, seg_ref, qg_s, kn_s, v_s):
    qg_s[...] = (cur_ref[:, 0:ATTN_WIDTH] * qg_ref[...]).astype(BF16)
    k = jnp.concatenate([prev_ref[:, KCOL:KCOL + KV_WIDTH], cur_ref[:, KCOL:KCOL + KV_WIDTH]], axis=0)
    kn_s[...] = k * lax.rsqrt(_segmean(k * k, seg_ref) + EPS) * kg_ref[...]
    v_s[0:BLOCK, :] = prev_ref[:, VCOL:VCOL + KV_WIDTH]
    v_s[BLOCK:SWA_KEYS, :] = cur_ref[:, VCOL:VCOL + KV_WIDTH]


def _swa_products(qg_s, kn_s, rows, keys, kv):
    q2 = _pair_rows(qg_s, rows, kv)
    k2 = _two_heads(kn_s[keys, :], kv)
    return q2, k2, _dot_nt(q2, k2)


def _swa_scores(cur_ref, sinks_ref, qg_s, kn_s, bias, rows, keys, kv):
    q2, k2, t = _swa_products(qg_s, kn_s, rows, keys, kv)
    return q2, k2, t, _swa_softmax(cur_ref, sinks_ref, t, bias, rows, kv)


def _swa_softmax(cur_ref, sinks_ref, t, bias, rows, kv):
    def rq(h):
        x = cur_ref[rows, h * HEAD_DIM:(h + 1) * HEAD_DIM]
        return lax.rsqrt(jnp.mean(x * x, axis=-1, keepdims=True) + EPS)

    scale = _head_cols(lambda h: rq(h) * (HEAD_DIM ** -0.5), kv)
    sink = _head_cols(lambda h: jnp.full((BLOCK, 1), sinks_ref[h], F32), kv)
    halves = []
    for half in range(2):
        cols = slice(half * 2 * BLOCK, (half + 1) * 2 * BLOCK)
        s = t[:, cols] * scale[half] + bias[:, cols]
        mx = jnp.maximum(jnp.max(s, axis=-1, keepdims=True), sink[half])
        halves.append((scale[half], jnp.exp(s - mx), jnp.exp(sink[half] - mx)))
    return halves


def swa_fwd(u, qg, kg, sinks):
    t_rows = u.shape[0]
    nt = t_rows // SWA_TILE
    bias_c, bias_first_c, seg_c, ones_c = _swa_constants()

    def body(sinks_ref, cur_ref, prev_ref, qg_ref, kg_ref, seg_ref, bias_ref, biasf_ref, ones_ref, o_ref, qg_s, kn_s, v_s):
        i = pl.program_id(0)
        _swa_prologue(cur_ref, prev_ref, qg_ref, kg_ref, seg_ref, qg_s, kn_s, v_s)
        lane = lax.broadcasted_iota(jnp.int32, (1, LANES), 1)
        work = [(b, kv, slice(b * BLOCK, (b + 1) * BLOCK), slice(b * BLOCK, (b + 2) * BLOCK))
                for b in range(SWA_SUB) for kv in range(N_KV_HEADS)]
        products = [_swa_products(qg_s, kn_s, rows, keys, kv)[2] for _, kv, rows, keys in work]
        scored = []
        for (b, kv, rows, _), t in zip(work, products):
            bias = jnp.where(i == 0, biasf_ref[kv], bias_ref[kv]) if b == 0 else bias_ref[kv]
            halves = _swa_softmax(cur_ref, sinks_ref, t, bias, rows, kv)
            scored.append((jnp.concatenate([halves[0][1], halves[1][1]], axis=1).astype(BF16), halves[0][2], halves[1][2]))
        for (b, kv, rows, keys), (e, es0, es1) in zip(work, scored):
            v2 = jnp.concatenate([_two_heads(v_s[keys, :], kv).astype(BF16), ones_ref[...]], axis=1)
            ox = _dot_nn(e, v2)
            den = ox[:, LANES:] + jnp.where(lane < HEAD_DIM, es0, es1)
            out = (ox[:, :LANES] / den).astype(o_ref.dtype)
            c = kv * 2 * PAIR
            o_ref[rows, c:c + PAIR] = out[:BLOCK]
            o_ref[rows, c + PAIR:c + 2 * PAIR] = out[BLOCK:]

    const3 = pl.BlockSpec((N_KV_HEADS, 2 * BLOCK, 4 * BLOCK), lambda i: (0, 0, 0))
    return pl.pallas_call(
        body, name="swa_fwd", grid=(nt,),
        in_specs=[
            pl.BlockSpec(memory_space=pltpu.SMEM),
            pl.BlockSpec((SWA_TILE, QKV_WIDTH), lambda i: (i, 0)),
            pl.BlockSpec((BLOCK, QKV_WIDTH), lambda i: (jnp.maximum(i * SWA_SUB - 1, 0), 0)),
            pl.BlockSpec((1, ATTN_WIDTH), lambda i: (0, 0)), pl.BlockSpec((1, KV_WIDTH), lambda i: (0, 0)),
            pl.BlockSpec((LANES, LANES), lambda i: (0, 0)), const3, const3,
            pl.BlockSpec((4 * BLOCK, LANES), lambda i: (0, 0)),
        ],
        out_specs=pl.BlockSpec((SWA_TILE, ATTN_WIDTH), lambda i: (i, 0)),
        out_shape=jax.ShapeDtypeStruct((t_rows, 2 * ATTN_WIDTH), BF16),
        scratch_shapes=[pltpu.VMEM((SWA_TILE, ATTN_WIDTH), BF16), pltpu.VMEM((SWA_KEYS, KV_WIDTH), F32),
                        pltpu.VMEM((SWA_KEYS, KV_WIDTH), F32)],
        compiler_params=_cp(),
    )(sinks, u, u, jnp.tile(qg, N_Q_HEADS).reshape(1, ATTN_WIDTH), jnp.tile(kg, N_KV_HEADS).reshape(1, KV_WIDTH),
      seg_c, bias_c, bias_first_c, ones_c)


def swa_bwd(u, dmixed, qg, kg, sinks):
    t_rows = u.shape[0]
    nt = t_rows // SWA_TILE
    bias_c, bias_first_c, seg_c, _ = _swa_constants()

    def body(sinks_ref, cur_ref, prev_ref, do_ref, qg_ref, kg_ref, seg_ref, bias_ref, biasf_ref,
             du_ref, dqg_ref, dkg_ref, dsk_ref, qg_s, kn_s, v_s, acck_s, accv_s, carryk_s, carryv_s):
        step = pl.program_id(0)
        i = nt - 1 - step

        @pl.when(step == 0)
        def _():
            for r in (carryk_s, carryv_s, dqg_ref, dkg_ref, dsk_ref):
                r[...] = jnp.zeros_like(r)

        _swa_prologue(cur_ref, prev_ref, qg_ref, kg_ref, seg_ref, qg_s, kn_s, v_s)
        for acc, carry in ((acck_s, carryk_s), (accv_s, carryv_s)):
            acc[0:SWA_TILE, :] = jnp.zeros((SWA_TILE, KV_WIDTH), F32)
            acc[SWA_TILE:SWA_KEYS, :] = carry[...]

        lane = lax.broadcasted_iota(jnp.int32, (1, LANES), 1)
        g_pair = qg_ref[:, 0:PAIR]
        dqg_acc = jnp.zeros((1, PAIR), F32)
        dsk_acc = jnp.zeros((1, LANES), F32)
        work = [(b, kv, slice(b * BLOCK, (b + 1) * BLOCK), slice(b * BLOCK, (b + 2) * BLOCK))
                for b in range(SWA_SUB) for kv in range(N_KV_HEADS)]
        products = []
        for _, kv, rows, keys in work:
            q2, k2, t = _swa_products(qg_s, kn_s, rows, keys, kv)
            do2 = _pair_rows(do_ref, rows, kv).astype(BF16)
            products.append((q2, k2, t, do2, _dot_nt(do2, _two_heads(v_s[keys, :], kv))))
        exps = []
        for (b, kv, rows, _), (_, _, t, _, _) in zip(work, products):
            bias = jnp.where(i == 0, biasf_ref[kv], bias_ref[kv]) if b == 0 else bias_ref[kv]
            exps.append(_swa_softmax(cur_ref, sinks_ref, t, bias, rows, kv))
        softmaxed = []
        for (b, kv, rows, _), (_, _, t, _, dp), halves in zip(work, products, exps):
            p_parts, dt_parts, coef = [], [], []
            for half, (scale, e, es) in enumerate(halves):
                cols = slice(half * 2 * BLOCK, (half + 1) * 2 * BLOCK)
                rden = 1.0 / (jnp.sum(e, axis=-1, keepdims=True) + es)
                p = e * rden
                dp_h = dp[:, cols]
                delta = jnp.sum(p * dp_h, axis=-1, keepdims=True)
                ds = p * (dp_h - delta)
                dsink = -(es * rden) * delta
                for pair in range(2):
                    part = jnp.sum(dsink[pair * BLOCK:(pair + 1) * BLOCK], axis=0, keepdims=True)
                    dsk_acc = dsk_acc + jnp.where(lane == kv * GROUP + 2 * pair + half, part, 0.0)
                dscale = jnp.sum(ds * t[:, cols], axis=-1, keepdims=True)
                coef.append(-dscale * scale * scale * scale)
                p_parts.append(p.astype(BF16))
                dt_parts.append((ds * scale).astype(BF16))
            softmaxed.append((jnp.concatenate(p_parts, axis=1), jnp.concatenate(dt_parts, axis=1),
                              jnp.where(lane < HEAD_DIM, coef[0], coef[1])))
        for (_, kv, rows, keys), (q2, k2, _, do2, _), (p2, dt, coef) in zip(work, products, softmaxed):
            dqg2 = _dot_nn(dt, k2)
            q_raw = _pair_rows(cur_ref, rows, kv)
            dq = dqg2 * g_pair + coef * q_raw
            dqg_acc = dqg_acc + jnp.sum(dqg2 * q_raw, axis=0, keepdims=True)
            c = kv * 2 * PAIR
            du_ref[rows, c:c + PAIR] = dq[:BLOCK].astype(du_ref.dtype)
            du_ref[rows, c + PAIR:c + 2 * PAIR] = dq[BLOCK:].astype(du_ref.dtype)
        to_keys = [(_from_two_heads(_dot_tn(dt, q2), kv), _from_two_heads(_dot_tn(p2, do2), kv))
                   for (_, kv, _, _), (q2, _, _, do2, _), (p2, dt, _) in zip(work, products, softmaxed)]
        for (_, _, _, keys), (dk, dv) in zip(work, to_keys):
            acck_s[keys, :] += dk
            accv_s[keys, :] += dv
        dqg_ref[...] += dqg_acc + pltpu.roll(dqg_acc, HEAD_DIM, 1)
        dsk_ref[...] += dsk_acc

        own = slice(BLOCK, SWA_KEYS)
        k = cur_ref[:, KCOL:KCOL + KV_WIDTH]
        rk = lax.rsqrt(_segmean(k * k, seg_ref) + EPS)
        kh = k * rk
        dkn = acck_s[own, :]
        dkh = dkn * kg_ref[...]
        du_ref[:, KCOL:KCOL + KV_WIDTH] = (rk * (dkh - kh * _segmean(dkh * kh, seg_ref))).astype(du_ref.dtype)
        du_ref[:, VCOL:VCOL + KV_WIDTH] = accv_s[own, :].astype(du_ref.dtype)
        dkg_part = jnp.sum(dkn * kh, axis=0, keepdims=True)
        dkg_ref[...] += dkg_part + pltpu.roll(dkg_part, HEAD_DIM, 1)
        carryk_s[...] = acck_s[0:BLOCK, :]
        carryv_s[...] = accv_s[0:BLOCK, :]

    const3 = pl.BlockSpec((N_KV_HEADS, 2 * BLOCK, 4 * BLOCK), lambda s: (0, 0, 0))
    vec = pl.BlockSpec((1, LANES), lambda s: (0, 0))
    return pl.pallas_call(
        body, name="swa_bwd", grid=(nt,),
        in_specs=[
            pl.BlockSpec(memory_space=pltpu.SMEM),
            pl.BlockSpec((SWA_TILE, QKV_WIDTH), lambda s: (nt - 1 - s, 0)),
            pl.BlockSpec((BLOCK, QKV_WIDTH), lambda s: (jnp.maximum((nt - 1 - s) * SWA_SUB - 1, 0), 0)),
            pl.BlockSpec((SWA_TILE, ATTN_WIDTH), lambda s: (nt - 1 - s, 0)),
            pl.BlockSpec((1, ATTN_WIDTH), lambda s: (0, 0)), vec,
            pl.BlockSpec((LANES, LANES), lambda s: (0, 0)), const3, const3,
        ],
        out_specs=[pl.BlockSpec((SWA_TILE, QKV_WIDTH), lambda s: (nt - 1 - s, 0)), vec, vec, vec],
        out_shape=[jax.ShapeDtypeStruct((t_rows, IN_COLS), BF16)] + [jax.ShapeDtypeStruct((1, LANES), F32)] * 3,
        scratch_shapes=[pltpu.VMEM((SWA_TILE, ATTN_WIDTH), BF16)] + [pltpu.VMEM((SWA_KEYS, KV_WIDTH), F32)] * 4
        + [pltpu.VMEM((BLOCK, KV_WIDTH), F32)] * 2,
        compiler_params=_cp(),
    )(sinks, u, u, dmixed, jnp.tile(qg, N_Q_HEADS).reshape(1, ATTN_WIDTH), jnp.tile(kg, N_KV_HEADS).reshape(1, KV_WIDTH),
      seg_c, bias_c, bias_first_c)


CONV_TILE = 512
CONV_CHUNK = 64
VAL0 = QKV_WIDTH
GATE0 = QKV_WIDTH + CONV_CH


def _glu(ref):
    return ref[:, VAL0:GATE0] * _sigmoid(ref[:, GATE0:GATE0 + CONV_CH])


SUBLANES = 8
CONV_BUF = CONV_HALO + CONV_TILE + SUBLANES
CONV_EXT = CONV_HALO + CONV_TILE


def _fill_shifted(sh_ref):
    for r in range(1, SUBLANES):
        sh_ref[r, 0:CONV_EXT, :] = sh_ref[0, pl.ds(r, CONV_EXT), :]


def _shifted(sh_ref, start, offset, n):
    return sh_ref[offset % SUBLANES, pl.ds(start + offset - offset % SUBLANES, n), :]


def _layernorm_stats(y):
    mu = jnp.mean(y, axis=-1, keepdims=True)
    yc = y - mu
    rstd = lax.rsqrt(jnp.mean(yc * yc, axis=-1, keepdims=True) + EPS)
    return yc * rstd, rstd


def conv_fwd(u, mixed, conv_w, conv_b, ln_g, ln_b):
    t = u.shape[0]
    nt = t // CONV_TILE
    per = CONV_TILE // CONV_HALO

    def body(cur_ref, prev_ref, mixed_ref, w_ref, b_ref, g_ref, b2_ref, o_ref, y_ref, gl_ref):
        del mixed_ref
        i = pl.program_id(0)
        gl_ref[0, 0:CONV_HALO, :] = jnp.where(i > 0, _glu(prev_ref), 0.0)
        gl_ref[0, CONV_HALO:CONV_EXT, :] = _glu(cur_ref)
        gl_ref[0, CONV_EXT:CONV_BUF, :] = jnp.zeros((SUBLANES, CONV_CH), F32)
        _fill_shifted(gl_ref)
        for c0 in range(0, CONV_TILE, CONV_CHUNK):
            acc = jnp.broadcast_to(b_ref[...], (CONV_CHUNK, CONV_CH))
            for k in range(CONV_K):
                acc = acc + w_ref[k:k + 1, :] * _shifted(gl_ref, c0, 2 + k, CONV_CHUNK)
            y_ref[c0:c0 + CONV_CHUNK, :] = acc
        yh, _ = _layernorm_stats(y_ref[...])
        yln = yh * g_ref[...] + b2_ref[...]
        o_ref[...] = (yln * _sigmoid(yln)).astype(o_ref.dtype)

    vec = pl.BlockSpec((1, CONV_CH), lambda i: (0, 0))
    return pl.pallas_call(
        body, name="conv_fwd", grid=(nt,),
        in_specs=[
            pl.BlockSpec((CONV_TILE, IN_COLS), lambda i: (i, 0)),
            pl.BlockSpec((CONV_HALO, IN_COLS), lambda i: (jnp.maximum(i * per - 1, 0), 0)),
            pl.BlockSpec(memory_space=pl.ANY),
            pl.BlockSpec((CONV_HALO, CONV_CH), lambda i: (0, 0)),
            vec, vec, vec,
        ],
        out_specs=[pl.BlockSpec((CONV_TILE, CONV_CH), lambda i: (i, 1)), pl.BlockSpec((CONV_TILE, CONV_CH), lambda i: (i, 0))],
        out_shape=[jax.ShapeDtypeStruct(mixed.shape, mixed.dtype), jax.ShapeDtypeStruct((t, CONV_CH), F32)],
        scratch_shapes=[pltpu.VMEM((SUBLANES, CONV_BUF, CONV_CH), F32)],
        input_output_aliases={2: 0}, compiler_params=_cp(),
    )(u, u, mixed, conv_w, conv_b.reshape(1, CONV_CH), ln_g.reshape(1, CONV_CH), ln_b.reshape(1, CONV_CH))


def conv_bwd(u, y, dmixed, du, conv_w, ln_g, ln_b):
    t = u.shape[0]
    nt = t // CONV_TILE
    per = CONV_TILE // CONV_HALO

    def body(cur_ref, prev_ref, y_ref, yn_ref, do_ref, don_ref, du_in_ref, w_ref, g_ref, b2_ref,
             du_ref, dw_ref, dvec_ref, gl_ref, dy_ref):
        i = pl.program_id(0)
        last = i == nt - 1
        _zero_at_first_step(dw_ref, dvec_ref)

        gl_ref[0, 0:CONV_HALO, :] = jnp.where(i > 0, _glu(prev_ref), 0.0)
        gl_ref[0, CONV_HALO:CONV_EXT, :] = _glu(cur_ref)
        gl_ref[0, CONV_EXT:CONV_BUF, :] = jnp.zeros((SUBLANES, CONV_CH), F32)
        _fill_shifted(gl_ref)

        yh, rstd = _layernorm_stats(jnp.concatenate([y_ref[...], yn_ref[...]], axis=0))
        g = g_ref[...]
        yln = yh * g + b2_ref[...]
        sg = _sigmoid(yln)
        dout = jnp.concatenate([do_ref[...], jnp.where(last, 0.0, don_ref[...])], axis=0)
        dyln = dout * (sg * (1.0 + yln * (1.0 - sg)))
        dyh = dyln * g
        dy = rstd * (dyh - jnp.mean(dyh, axis=-1, keepdims=True) - yh * jnp.mean(dyh * yh, axis=-1, keepdims=True))
        dy_ref[0, 0:CONV_EXT, :] = dy
        dy_ref[0, CONV_EXT:CONV_BUF, :] = jnp.zeros((SUBLANES, CONV_CH), F32)
        _fill_shifted(dy_ref)

        own = slice(0, CONV_TILE)
        dvec_ref[0:1, :] += jnp.sum(dy[own], axis=0, keepdims=True)
        dvec_ref[1:2, :] += jnp.sum(dyln[own] * yh[own], axis=0, keepdims=True)
        dvec_ref[2:3, :] += jnp.sum(dyln[own], axis=0, keepdims=True)
        for k in range(CONV_K):
            dw_ref[k:k + 1, :] += jnp.sum(dy[own] * _shifted(gl_ref, 0, 2 + k, CONV_TILE), axis=0, keepdims=True)

        for c0 in range(0, CONV_TILE, CONV_CHUNK):
            acc = jnp.zeros((CONV_CHUNK, CONV_CH), F32)
            for k in range(CONV_K):
                acc = acc + w_ref[k:k + 1, :] * _shifted(dy_ref, c0, CONV_K - 1 - k, CONV_CHUNK)
            rows = slice(c0, c0 + CONV_CHUNK)
            val = cur_ref[rows, VAL0:GATE0]
            sgate = _sigmoid(cur_ref[rows, GATE0:GATE0 + CONV_CH])
            du_ref[rows, VAL0:GATE0] = (acc * sgate).astype(du_ref.dtype)
            du_ref[rows, GATE0:GATE0 + CONV_CH] = (acc * val * sgate * (1.0 - sgate)).astype(du_ref.dtype)
        du_ref[:, 0:QKV_WIDTH] = du_in_ref[:, 0:QKV_WIDTH]

    vec = pl.BlockSpec((1, CONV_CH), lambda i: (0, 0))
    n_halo = t // CONV_HALO
    return pl.pallas_call(
        body, name="conv_bwd", grid=(nt,),
        in_specs=[
            pl.BlockSpec((CONV_TILE, IN_COLS), lambda i: (i, 0)),
            pl.BlockSpec((CONV_HALO, IN_COLS), lambda i: (jnp.maximum(i * per - 1, 0), 0)),
            pl.BlockSpec((CONV_TILE, CONV_CH), lambda i: (i, 0)),
            pl.BlockSpec((CONV_HALO, CONV_CH), lambda i: (jnp.minimum((i + 1) * per, n_halo - 1), 0)),
            pl.BlockSpec((CONV_TILE, CONV_CH), lambda i: (i, 1)),
            pl.BlockSpec((CONV_HALO, CONV_CH), lambda i: (jnp.minimum((i + 1) * per, n_halo - 1), 1)),
            pl.BlockSpec((CONV_TILE, IN_COLS), lambda i: (i, 0)),
            pl.BlockSpec((CONV_HALO, CONV_CH), lambda i: (0, 0)),
            vec, vec,
        ],
        out_specs=[
            pl.BlockSpec((CONV_TILE, IN_COLS), lambda i: (i, 0)),
            pl.BlockSpec((CONV_HALO, CONV_CH), lambda i: (0, 0)),
            pl.BlockSpec((8, CONV_CH), lambda i: (0, 0)),
        ],
        out_shape=[
            jax.ShapeDtypeStruct(du.shape, du.dtype),
            jax.ShapeDtypeStruct((CONV_HALO, CONV_CH), F32),
            jax.ShapeDtypeStruct((8, CONV_CH), F32),
        ],
        scratch_shapes=[pltpu.VMEM((SUBLANES, CONV_BUF, CONV_CH), F32), pltpu.VMEM((SUBLANES, CONV_BUF, CONV_CH), F32)],
        input_output_aliases={6: 0}, compiler_params=_cp(),
    )(u, u, y, y, dmixed, dmixed, du, conv_w, ln_g.reshape(1, CONV_CH), ln_b.reshape(1, CONV_CH))


def adamw(w, g, m, v, *, name):
    r, c = w.shape
    tr = r
    for cand in (512, 256, 128, 64, 32, 16, 8):
        if r % cand == 0 and r > cand:
            tr = cand
            break

    def body(w_ref, g_ref, m_ref, v_ref, d_ref, nm_ref, nv_ref):
        d_ref[...], nm_ref[...], nv_ref[...] = _adamw_math(w_ref[...], g_ref[...], m_ref[...], v_ref[...])

    spec = pl.BlockSpec((tr, c), lambda i: (i, 0))
    shape = jax.ShapeDtypeStruct((r, c), F32)
    return pl.pallas_call(
        body, name=name, grid=(r // tr,), in_specs=[spec] * 4, out_specs=[spec] * 3,
        out_shape=[shape] * 3, compiler_params=_cp(),
    )(w, g, m, v)


def _position():
    return lax.axis_index("x"), lax.axis_index("y"), lax.axis_index("c")


def all_gather_many(shards, *, name):
    n = len(shards)

    def body(*refs):
        x_refs, out_refs, token_ref = refs[:n], refs[n:2 * n], refs[2 * n]
        send_sems, recv_sems, local_sems = refs[2 * n + 1:]
        x, y, c = _position()
        me, sibling = (x, y, c), (x, y, 1 - c)
        chips = [(1 - x, y), (x, 1 - y), (1 - x, 1 - y)]
        token_ref[...] = jnp.zeros_like(token_ref)

        def rows(t, px, py, pc):
            return out_refs[t].at[4 * px + 2 * py + pc]

        def copy(t, k, block, to, src=None):
            return pltpu.make_async_remote_copy(
                src_ref=rows(t, *block) if src is None else src, dst_ref=rows(t, *block),
                send_sem=send_sems.at[7 * t + k], recv_sem=recv_sems.at[7 * t + k], device_id=to, device_id_type=MESH)

        mine = [pltpu.make_async_copy(x_refs[t], rows(t, *me), local_sems.at[t]) for t in range(n)]
        for cp in mine:
            cp.start()
        first = []
        for t in range(n):
            first.append(copy(t, 0, me, sibling, src=x_refs[t]))
            first += [copy(t, 1 + j, me, (*chip, c), src=x_refs[t]) for j, chip in enumerate(chips)]
        for cp in first:
            cp.start()
        passed = []
        for t in range(n):
            for j, chip in enumerate(chips):
                copy(t, 1 + j, (*chip, c), me).wait_recv()
                passed.append(copy(t, 4 + j, (*chip, c), sibling))
                passed[-1].start()
        for t in range(n):
            copy(t, 0, sibling, me).wait_recv()
            for j, chip in enumerate(chips):
                copy(t, 4 + j, (*chip, 1 - c), me).wait_recv()
        for cp in first + passed:
            cp.wait_send()
        for cp in mine:
            cp.wait()

    hbm = pl.BlockSpec(memory_space=pltpu.HBM)
    out = pl.pallas_call(
        body, name=name,
        out_shape=[jax.ShapeDtypeStruct((N_DEV,) + s.shape, s.dtype) for s in shards] + [jax.ShapeDtypeStruct((8, LANES), F32)],
        in_specs=[hbm] * n, out_specs=[hbm] * n + [pl.BlockSpec(memory_space=pltpu.VMEM)],
        scratch_shapes=[pltpu.SemaphoreType.DMA((7 * n,)), pltpu.SemaphoreType.DMA((7 * n,)), pltpu.SemaphoreType.DMA((n,))],
        compiler_params=_cp(),
    )(*shards)
    return out[:n], out[n]


_HBM = pl.BlockSpec(memory_space=pltpu.HBM)
_SEM = pl.BlockSpec(memory_space=pltpu.SEMAPHORE)
_EFFECT = pltpu.SideEffectType.DATAFLOW_SIDE_EFFECTING


def _split_copies(src_refs, land_refs, send_sems, recv_sems, plan, n_copies):
    copies = []
    for t, (src_ref, land_ref) in enumerate(zip(src_refs, land_refs)):
        for k in range(n_copies):
            s, d, to = plan(src_ref, land_ref, k)
            copies.append(pltpu.make_async_remote_copy(
                src_ref=s, dst_ref=d, send_sem=send_sems.at[n_copies * t + k], recv_sem=recv_sems.at[n_copies * t + k],
                device_id=to, device_id_type=MESH))
    return copies


def split_start(srcs, lands, plan, n_copies, *, name):
    n = len(srcs)

    def body(*refs):
        src_refs, land_refs, send_sems, recv_sems, token = refs[:n], refs[n:2 * n], refs[2 * n], refs[2 * n + 1], refs[-1]
        for cp in _split_copies(src_refs, land_refs, send_sems, recv_sems, plan, n_copies):
            cp.start()
        token[...] = jnp.zeros_like(token)

    both = list(srcs) + list(lands)
    out = pl.pallas_call(
        body, name=name,
        out_shape=(pltpu.SemaphoreType.DMA((n_copies * n,)), pltpu.SemaphoreType.DMA((n_copies * n,)),
                   *[pltpu.HBM(a.shape, a.dtype) for a in both], jax.ShapeDtypeStruct((8, LANES), F32)),
        in_specs=(_HBM,) * (2 * n), out_specs=(_SEM, _SEM) + (_HBM,) * (2 * n) + (pl.BlockSpec(memory_space=pltpu.VMEM),),
        input_output_aliases={i: 2 + i for i in range(2 * n)},
        compiler_params=pltpu.CompilerParams(has_side_effects=_EFFECT),
    )(*[pltpu.with_memory_space_constraint(a, pltpu.HBM) for a in both])
    return out[0], out[1], list(out[2:2 + n]), list(out[2 + n:2 + 2 * n]), out[-1]


def split_wait(started, after, plan, n_copies, *, name):
    send_sems, recv_sems, srcs, lands, _ = started
    n = len(srcs)

    def body(*refs):
        src_refs, land_refs, send_sems, recv_sems = refs[:n], refs[n:2 * n], refs[2 * n], refs[2 * n + 1]
        for cp in _split_copies(src_refs, land_refs, send_sems, recv_sems, plan, n_copies):
            cp.wait_send()
            cp.wait_recv()

    both = list(srcs) + list(lands)
    out = pl.pallas_call(
        body, name=name,
        out_shape=tuple(pltpu.HBM(a.shape, a.dtype) for a in both),
        in_specs=(_HBM,) * (2 * n) + (_SEM, _SEM, pl.BlockSpec(memory_space=pl.ANY)), out_specs=(_HBM,) * (2 * n),
        input_output_aliases={i: i for i in range(2 * n)},
        compiler_params=pltpu.CompilerParams(has_side_effects=_EFFECT),
    )(*both, send_sems, recv_sems, after)
    return list(out[:n]), list(out[n:])


def _other_chips(x, y):
    return [(1 - x, y), (x, 1 - y), (1 - x, 1 - y)]


def _remote(src, dst, send_sem, recv_sem, to):
    return pltpu.make_async_remote_copy(src_ref=src, dst_ref=dst, send_sem=send_sem, recv_sem=recv_sem,
                                        device_id=to, device_id_type=MESH)


def gather_start(groups, *, name):
    counts = [len(shards) for shards, _ in groups]
    flat = [a for shards, _ in groups for a in shards] + [a for _, lands in groups for a in lands]
    n_all, n_groups = sum(counts), len(groups)

    def body(*refs):
        s_refs, l_refs = refs[:n_all], refs[n_all:2 * n_all]
        sems = refs[2 * n_all:2 * n_all + 3 * n_groups]
        x, y, c = _position()
        me = 4 * x + 2 * y + c
        at = 0
        for gi, n in enumerate(counts):
            send, recv_sibling, recv_ici = sems[3 * gi:3 * gi + 3]
            for t in range(n):
                src, dst = s_refs[at + t], l_refs[at + t].at[me]
                _remote(src, dst, send.at[4 * t], recv_sibling.at[t], (x, y, 1 - c)).start()
                for j, chip in enumerate(_other_chips(x, y)):
                    _remote(src, dst, send.at[4 * t + 1 + j], recv_ici.at[3 * t + j], (*chip, c)).start()
            at += n
        refs[-1][...] = jnp.zeros_like(refs[-1])

    sem_shapes = [pltpu.SemaphoreType.DMA((k * n,)) for n in counts for k in (4, 1, 3)]
    out = pl.pallas_call(
        body, name=name,
        out_shape=(*sem_shapes, *[pltpu.HBM(a.shape, a.dtype) for a in flat], jax.ShapeDtypeStruct((8, LANES), F32)),
        in_specs=(_HBM,) * (2 * n_all),
        out_specs=(_SEM,) * (3 * n_groups) + (_HBM,) * (2 * n_all) + (pl.BlockSpec(memory_space=pltpu.VMEM),),
        input_output_aliases={i: 3 * n_groups + i for i in range(2 * n_all)},
        compiler_params=pltpu.CompilerParams(has_side_effects=_EFFECT),
    )(*[pltpu.with_memory_space_constraint(a, pltpu.HBM) for a in flat])
    thru = out[3 * n_groups:-1]
    states, at = [], 0
    for gi, n in enumerate(counts):
        states.append(dict(shards=list(thru[at:at + n]), lands=list(thru[n_all + at:n_all + at + n]),
                           send=out[3 * gi], recv_sibling=out[3 * gi + 1], recv_ici=out[3 * gi + 2]))
        at += n
    return states, out[-1]


def gather_forward(states, after, *, name):
    counts = [len(s["lands"]) for s in states]
    flat = [a for s in states for a in s["lands"]]
    n_all, n_groups = sum(counts), len(states)

    def body(*refs):
        l_refs = refs[:n_all]
        recv_ici = refs[n_all:n_all + n_groups]
        fwd = refs[n_all + n_groups + 1:n_all + n_groups + 1 + 2 * n_groups]
        x, y, c = _position()
        at = 0
        for gi, n in enumerate(counts):
            fwd_send, fwd_recv = fwd[2 * gi], fwd[2 * gi + 1]
            for t in range(n):
                for j, (px, py) in enumerate(_other_chips(x, y)):
                    block = l_refs[at + t].at[4 * px + 2 * py + c]
                    _remote(block, block, fwd_send.at[3 * t + j], recv_ici[gi].at[3 * t + j], (px, py, c)).wait_recv()
                    _remote(block, block, fwd_send.at[3 * t + j], fwd_recv.at[3 * t + j], (x, y, 1 - c)).start()
            at += n
        refs[-1][...] = jnp.zeros_like(refs[-1])

    sem_shapes = [pltpu.SemaphoreType.DMA((3 * n,)) for n in counts for _ in range(2)]
    out = pl.pallas_call(
        body, name=name,
        out_shape=(*sem_shapes, *[pltpu.HBM(a.shape, a.dtype) for a in flat], jax.ShapeDtypeStruct((8, LANES), F32)),
        in_specs=(_HBM,) * n_all + (_SEM,) * n_groups + (pl.BlockSpec(memory_space=pl.ANY),),
        out_specs=(_SEM,) * (2 * n_groups) + (_HBM,) * n_all + (pl.BlockSpec(memory_space=pltpu.VMEM),),
        input_output_aliases={i: 2 * n_groups + i for i in range(n_all)},
        compiler_params=pltpu.CompilerParams(has_side_effects=_EFFECT),
    )(*flat, *[s["recv_ici"] for s in states], after)
    at = 0
    for gi, (s, n) in enumerate(zip(states, counts)):
        s.update(fwd_send=out[2 * gi], fwd_recv=out[2 * gi + 1], lands=list(out[2 * n_groups + at:2 * n_groups + at + n]))
        at += n
    return out[-1]


def gather_finish(state, after, *, name):
    n = len(state["lands"])

    def body(*refs):
        s_refs, l_refs = refs[:n], refs[n:2 * n]
        send, recv_sibling, fwd_send, fwd_recv = refs[2 * n:2 * n + 4]
        x, y, c = _position()
        me = 4 * x + 2 * y + c
        for t in range(n):
            own = l_refs[t].at[me]
            _remote(s_refs[t], own, send.at[4 * t], recv_sibling.at[t], (x, y, 1 - c)).wait_send()
            _remote(s_refs[t], l_refs[t].at[4 * x + 2 * y + 1 - c], send.at[4 * t], recv_sibling.at[t], (x, y, 1 - c)).wait_recv()
            for j, (px, py) in enumerate(_other_chips(x, y)):
                _remote(s_refs[t], own, send.at[4 * t + 1 + j], recv_sibling.at[t], (px, py, c)).wait_send()
                mine, theirs = l_refs[t].at[4 * px + 2 * py + c], l_refs[t].at[4 * px + 2 * py + 1 - c]
                _remote(mine, mine, fwd_send.at[3 * t + j], fwd_recv.at[3 * t + j], (x, y, 1 - c)).wait_send()
                _remote(theirs, theirs, fwd_send.at[3 * t + j], fwd_recv.at[3 * t + j], (x, y, 1 - c)).wait_recv()

    both = state["shards"] + state["lands"]
    out = pl.pallas_call(
        body, name=name,
        out_shape=tuple(pltpu.HBM(a.shape, a.dtype) for a in both),
        in_specs=(_HBM,) * (2 * n) + (_SEM,) * 4 + (pl.BlockSpec(memory_space=pl.ANY),), out_specs=(_HBM,) * (2 * n),
        input_output_aliases={i: i for i in range(2 * n)},
        compiler_params=pltpu.CompilerParams(has_side_effects=_EFFECT),
    )(*both, state["send"], state["recv_sibling"], state["fwd_send"], state["fwd_recv"], after)
    return list(out[n:])


def _all_peers_plan(src_ref, land_ref, k):
    x, y, c = _position()
    bits = k + 1
    peer = ((1 - x) if bits & 4 else x, (1 - y) if bits & 2 else y, (1 - c) if bits & 1 else c)
    return src_ref, land_ref.at[4 * x + 2 * y + c], peer


def _sibling_plan(src_ref, land_ref, k):
    x, y, c = _position()
    return src_ref.at[2 * k + (1 - c)], land_ref.at[k], (x, y, 1 - c)


def _chips_plan(src_ref, land_ref, j):
    x, y, c = _position()
    px, py = _other_chips(x, y)[j]
    return src_ref.at[j], land_ref.at[j], (px, py, c)


SUM_STEPS = 2


def sum_for_chips(parts, from_sibling, ck_idx, *, name):
    n = len(parts)

    def body(ck_ref, *refs):
        del ck_ref
        for t in range(n):
            refs[2 * n + t][...] = (refs[t][...] + refs[n + t][...]).astype(BF16)

    def blk(a):
        return (None, a.shape[1] // SUM_STEPS, a.shape[2])

    return pl.pallas_call(
        body, name=name,
        grid_spec=pltpu.PrefetchScalarGridSpec(
            num_scalar_prefetch=1, grid=(3, SUM_STEPS),
            in_specs=[pl.BlockSpec(blk(a), lambda j, i, ck: (2 * ck[1 + j] + ck[0], i, 0)) for a in parts]
            + [pl.BlockSpec(blk(a), lambda j, i, ck: (ck[1 + j], i, 0)) for a in from_sibling],
            out_specs=[pl.BlockSpec(blk(a), lambda j, i, ck: (j, i, 0)) for a in from_sibling]),
        out_shape=[jax.ShapeDtypeStruct((3,) + a.shape[1:], BF16) for a in from_sibling], compiler_params=_cp(),
    )(ck_idx, *parts, *from_sibling)


def sum_final(parts, from_sibling, from_chips, kc_idx, *, name):
    n = len(parts)

    def body(kc_ref, *refs):
        del kc_ref
        for t in range(n):
            p, s, a, b, d = (refs[j * n + t] for j in range(5))
            refs[5 * n + t][...] = (((p[...] + s[...]) + a[...].astype(F32)) + b[...].astype(F32)) + d[...].astype(F32)

    def blk(a):
        return (None, a.shape[1] // SUM_STEPS, a.shape[2])

    def chip_specs(j):
        return [pl.BlockSpec(blk(a), lambda i, kc: (j, i, 0)) for a in from_chips]

    return pl.pallas_call(
        body, name=name,
        grid_spec=pltpu.PrefetchScalarGridSpec(
            num_scalar_prefetch=1, grid=(SUM_STEPS,),
            in_specs=[pl.BlockSpec(blk(a), lambda i, kc: (2 * kc[0] + kc[1], i, 0)) for a in parts]
            + [pl.BlockSpec(blk(a), lambda i, kc: (kc[0], i, 0)) for a in from_sibling]
            + chip_specs(0) + chip_specs(1) + chip_specs(2),
            out_specs=[pl.BlockSpec(blk(a)[1:], lambda i, kc: (i, 0)) for a in parts]),
        out_shape=[jax.ShapeDtypeStruct(a.shape[1:], F32) for a in parts], compiler_params=_cp(),
    )(kc_idx, *parts, *from_sibling, *from_chips, *from_chips, *from_chips)


BIG = (
    ("w_in", IN_COLS, True), ("w_out", D_MODEL, False), ("wq_x", D_MODEL, False), ("wkv_x", 2 * D_MODEL, True),
    ("wo_x", D_MODEL, False), ("w_gate_up", 2 * D_FF, True), ("w_down", D_FF, False),
)

SMALL = ("norm_mix_g", "q_norm_g", "k_norm_g", "sinks", "conv_b", "conv_ln_g", "conv_ln_b",
         "norm_x_g", "norm_mem_g", "xq_norm_g", "xk_norm_g", "norm_ffn_g")


ADAMW_STEPS = 8


def adamw_many(ws, gs, ms, vs, *, name):
    n = len(ws)

    def body(*refs):
        for t in range(n):
            w, g, m, v = (refs[j * n + t] for j in range(4))
            d_ref, nm_ref, nv_ref = (refs[(4 + j) * n + t] for j in range(3))
            d_ref[...], nm_ref[...], nv_ref[...] = _adamw_math(w[...], g[...], m[...], v[...])

    specs = [pl.BlockSpec((a.shape[0] // ADAMW_STEPS, a.shape[1]), lambda i: (i, 0)) for a in ws]
    shapes = [jax.ShapeDtypeStruct(a.shape, F32) for a in ws]
    out = pl.pallas_call(
        body, name=name, grid=(ADAMW_STEPS,), in_specs=specs * 4, out_specs=specs * 3, out_shape=shapes * 3,
        compiler_params=_cp(),
    )(*ws, *gs, *ms, *vs)
    return out[:n], out[n:2 * n], out[2 * n:]


def _adamw_math(w, g, m, v):
    m2 = ADAM_B1 * m + (1.0 - ADAM_B1) * g
    v2 = ADAM_B2 * v + (1.0 - ADAM_B2) * jnp.square(g)
    m_hat = m2 / (1.0 - ADAM_B1 ** ADAM_STEP)
    v_hat = v2 / (1.0 - ADAM_B2 ** ADAM_STEP)
    return -ADAM_LR * (m_hat / (jnp.sqrt(v_hat) + ADAM_EPS) + ADAM_WD * w), m2, v2


def _small_rows(per_layer_shape):
    return 1 if len(per_layer_shape) == 1 else per_layer_shape[0]


def pack_small(parts, shapes):
    blocks = []
    for per_layer, sh in zip(parts, shapes):
        for g in per_layer:
            g = g.reshape(_small_rows(sh), sh[-1])
            blocks.append(jnp.pad(g, ((0, 0), (0, D_MODEL - sh[-1]))))
    rows = sum(b.shape[0] for b in blocks)
    blocks.append(jnp.zeros((-rows % 8, D_MODEL), F32))
    return jnp.concatenate(blocks, axis=0)


def update_small(gathered, shapes, weights, moments_m, moments_v, n_update):
    n_all = len(shapes)

    def body(*refs):
        g_ref = refs[0]
        w_refs, m_refs, v_refs = (refs[1 + j * n_update:1 + (j + 1) * n_update] for j in range(3))
        out = refs[1 + 3 * n_update:]
        grad_refs = out[:n_all]
        d_refs, nm_refs, nv_refs = (out[n_all + j * n_update:n_all + (j + 1) * n_update] for j in range(3))
        at = 0
        for p, sh in enumerate(shapes):
            rows, lanes = _small_rows(sh), sh[-1]
            for l in range(DEPTH):
                g = g_ref[0, at:at + rows, 0:lanes]
                for k in range(1, N_DEV):
                    g = g + g_ref[k, at:at + rows, 0:lanes]
                at += rows
                here = (slice(l, l + 1),) + (slice(None),) * (len(sh) - 1) if len(sh) == 1 else (l,)
                grad_refs[p][here] = g
                if p < n_update:
                    d, m2, v2 = _adamw_math(w_refs[p][here], g, m_refs[p][here], v_refs[p][here])
                    d_refs[p][here] = d
                    nm_refs[p][here] = m2
                    nv_refs[p][here] = v2

    full = [jax.ShapeDtypeStruct((DEPTH,) + tuple(sh), F32) for sh in shapes]
    out = pl.pallas_call(
        body, name="update_small", out_shape=full + full[:n_update] * 3, compiler_params=_cp(),
    )(gathered, *weights, *moments_m, *moments_v)
    return (out[:n_all], out[n_all:n_all + n_update], out[n_all + n_update:n_all + 2 * n_update],
            out[n_all + 2 * n_update:])


WEIGHT_GROUPS = {"in": ("w_in",), "mid": ("w_out", "wq_x", "wkv_x", "wo_x"), "ffn": ("w_gate_up", "w_down")}


def _layer_fwd(x0, mem, weights_of, s, reached, target=None):
    w = dict(weights_of("in", x0))
    h0, u = norm_proj(x0, s["norm_mix_g"], w["w_in"])
    mixed = swa_fwd(u, s["q_norm_g"], s["k_norm_g"], s["sinks"])
    reached("attn", mixed)
    mixed, conv_y = conv_fwd(u, mixed, s["conv_w"], s["conv_b"], s["conv_ln_g"], s["conv_ln_b"])
    w.update(weights_of("mid", conv_y))
    memn, kv = kv_fwd(mem, s["norm_mem_g"], w["wkv_x"])
    x1, h1, qx, o, x2, h2 = mid_fwd(mixed, x0, w["w_out"], s["norm_x_g"], w["wq_x"], kv, s["xq_norm_g"], s["xk_norm_g"],
                                    w["wo_x"], s["norm_ffn_g"])
    reached("mid", x2)
    w.update(weights_of("ffn", x2))
    gu, a, *out = ffn_fwd(h2, x2, w["w_gate_up"], w["w_down"], target)
    saved = dict(x0=x0, h0=h0, u=u, conv_y=conv_y, mixed=mixed, x1=x1, h1=h1, qx=qx, memn=memn, kv=kv, o=o, x2=x2, h2=h2,
                 gu=gu, a=a)
    return out, saved, w


def _ordered_after(a, token):
    return a if token is None else a + token[0, 0]


def _layer_bwd(dx3, mem, w, s, sv, token, stage_done):
    gs = {}
    dgu, dx2, dg = ffn_bwd(dx3, sv["gu"], sv["x2"], _ordered_after(s["norm_ffn_g"], token), w["w_down"], w["w_gate_up"])
    gs["norm_ffn_g"] = dg
    gb = {"w_down": mm_tn(sv["a"], dx3, name="mm_dw_down")}
    gb["w_gate_up"] = mm_tn(dgu, sv["h2"], tk=dgu.shape[0], name="mm_dw_gate_up")
    token = stage_done("ffn", gb, gb["w_gate_up"])

    gb = {}
    dq, dx1, dmixed, dkv, dqg, dkg, dg = mid_bwd(dx2, sv["qx"], sv["kv"], s["xq_norm_g"], s["xk_norm_g"], sv["x1"],
                                                 _ordered_after(s["norm_x_g"], token), w["wo_x"], w["wq_x"], w["w_out"])
    gs["xq_norm_g"], gs["xk_norm_g"], gs["norm_x_g"] = dqg, dkg, dg
    gb["wo_x"] = mm_tn(sv["o"], dx2, name="mm_dwo")
    gb["wq_x"] = mm_tn(sv["h1"], dq, name="mm_dwq")
    gb["wkv_x"], gs["norm_mem_g"] = kv_bwd(dkv, sv["memn"], mem, w["wkv_x"])
    gb["w_out"] = mm_tn(sv["mixed"], dx1, name="mm_dw_out")
    token = stage_done("mid", gb, gb["w_out"])

    du, dqg, dkg, dsinks = swa_bwd(sv["u"], dmixed, _ordered_after(s["q_norm_g"], token), s["k_norm_g"], s["sinks"])
    gs["q_norm_g"], gs["k_norm_g"], gs["sinks"] = dqg[0, :HEAD_DIM], dkg[0, :HEAD_DIM], dsinks[0, :N_Q_HEADS]
    token = stage_done("attn", {}, dqg)
    du, dconv_w, dvec = conv_bwd(sv["u"], sv["conv_y"], dmixed, du, s["conv_w"], _ordered_after(s["conv_ln_g"], token),
                                 s["conv_ln_b"])
    gs["conv_w"] = dconv_w[:CONV_K]
    gs["conv_b"], gs["conv_ln_g"], gs["conv_ln_b"] = dvec[0], dvec[1], dvec[2]
    dw_in = mm_tn(du, sv["h0"], tk=2048, name="mm_dw_in")
    token = stage_done("in", {"w_in": dw_in}, dw_in)
    dx0, dg = in_bwd(du, w["w_in"], sv["x0"], _ordered_after(s["norm_mix_g"], token), dx1)
    gs["norm_mix_g"] = dg
    token = stage_done("mix", {}, dx0)
    return dx0, gs, token


def _local_step(x, mem, target, weights_of, reached, smalls, stage_done):
    saved, weights = [], []
    out = [x]
    for l in range(DEPTH):
        out, sv, w = _layer_fwd(out[0], mem, functools.partial(weights_of, l), smalls[l], functools.partial(reached, l),
                                target if l == DEPTH - 1 else None)
        saved.append(sv)
        weights.append(w)
    dx, loss_part = out
    gss, token = [None] * DEPTH, None
    for l in reversed(range(DEPTH)):
        dx, gss[l], token = _layer_bwd(dx, mem, weights[l], smalls[l], saved[l], token,
                                       functools.partial(stage_done, l))
    return loss_part[0, 0], dx, gss


def kernel(x, mem, norm_mix_g, w_in, q_norm_g, k_norm_g, sinks, conv_w, conv_b, conv_ln_g, conv_ln_b, w_out, norm_x_g, norm_mem_g, wq_x, wkv_x, xq_norm_g, xk_norm_g, wo_x, norm_ffn_g, w_gate_up, w_down, loss_target, m_norm_mix_g, m_w_in, m_q_norm_g, m_k_norm_g, m_sinks, m_conv_w, m_conv_b, m_conv_ln_g, m_conv_ln_b, m_w_out, m_norm_x_g, m_norm_mem_g, m_wq_x, m_wkv_x, m_xq_norm_g, m_xk_norm_g, m_wo_x, m_norm_ffn_g, m_w_gate_up, m_w_down, v_norm_mix_g, v_w_in, v_q_norm_g, v_k_norm_g, v_sinks, v_conv_w, v_conv_b, v_conv_ln_g, v_conv_ln_b, v_w_out, v_norm_x_g, v_norm_mem_g, v_wq_x, v_wkv_x, v_xq_norm_g, v_xk_norm_g, v_wo_x, v_norm_ffn_g, v_w_gate_up, v_w_down):
    P = dict(norm_mix_g=norm_mix_g, w_in=w_in, q_norm_g=q_norm_g, k_norm_g=k_norm_g, sinks=sinks, conv_w=conv_w, conv_b=conv_b,
             conv_ln_g=conv_ln_g, conv_ln_b=conv_ln_b, w_out=w_out, norm_x_g=norm_x_g, norm_mem_g=norm_mem_g, wq_x=wq_x,
             wkv_x=wkv_x, xq_norm_g=xq_norm_g, xk_norm_g=xk_norm_g, wo_x=wo_x, norm_ffn_g=norm_ffn_g, w_gate_up=w_gate_up,
             w_down=w_down)
    M = dict(norm_mix_g=m_norm_mix_g, w_in=m_w_in, q_norm_g=m_q_norm_g, k_norm_g=m_k_norm_g, sinks=m_sinks, conv_w=m_conv_w,
             conv_b=m_conv_b, conv_ln_g=m_conv_ln_g, conv_ln_b=m_conv_ln_b, w_out=m_w_out, norm_x_g=m_norm_x_g,
             norm_mem_g=m_norm_mem_g, wq_x=m_wq_x, wkv_x=m_wkv_x, xq_norm_g=m_xq_norm_g, xk_norm_g=m_xk_norm_g, wo_x=m_wo_x,
             norm_ffn_g=m_norm_ffn_g, w_gate_up=m_w_gate_up, w_down=m_w_down)
    V = dict(norm_mix_g=v_norm_mix_g, w_in=v_w_in, q_norm_g=v_q_norm_g, k_norm_g=v_k_norm_g, sinks=v_sinks, conv_w=v_conv_w,
             conv_b=v_conv_b, conv_ln_g=v_conv_ln_g, conv_ln_b=v_conv_ln_b, w_out=v_w_out, norm_x_g=v_norm_x_g,
             norm_mem_g=v_norm_mem_g, wq_x=v_wq_x, wkv_x=v_wkv_x, xq_norm_g=v_xq_norm_g, xk_norm_g=v_xk_norm_g, wo_x=v_wo_x,
             norm_ffn_g=v_norm_ffn_g, w_gate_up=v_w_gate_up, w_down=v_w_down)
    order = ["norm_mix_g", "w_in", "q_norm_g", "k_norm_g", "sinks", "conv_w", "conv_b", "conv_ln_g", "conv_ln_b", "w_out",
             "norm_x_g", "norm_mem_g", "wq_x", "wkv_x", "xq_norm_g", "xk_norm_g", "wo_x", "norm_ffn_g", "w_gate_up", "w_down"]
    xi, yi, ci = _position()
    dev = 4 * xi + 2 * yi + ci
    x2d, mem2d, tgt2d = x[0], mem[0], loss_target[0]

    def travelling(name, l, transposed):
        a = P[name][l]
        return (a.T if transposed else a).astype(BF16)

    rows_of = {n: rows for n, rows, _ in BIG}
    transposed_of = {n: tr for n, _, tr in BIG}

    def whole(names, gathered):
        return {n: g.reshape(rows_of[n], D_MODEL) for n, g in zip(names, gathered)}

    cw = jnp.pad(conv_w.reshape(DEPTH * CONV_K, CONV_CH // N_DEV), ((0, 2), (0, LANES - CONV_CH // N_DEV)))
    (w_in0, cw_all), token0 = all_gather_many([travelling("w_in", 0, True), cw], name="ag_w_in0_conv_w")
    travel_order = [(0, "mid"), (0, "ffn"), (1, "in"), (1, "mid"), (1, "ffn")]
    travel_groups = []
    for l, group in travel_order:
        shards = [_ordered_after(travelling(n, l, transposed_of[n]), token0.astype(BF16)) for n in WEIGHT_GROUPS[group]]
        lands = [lax.dynamic_update_slice(lax.empty((N_DEV,) + s.shape, BF16), s[None], (dev, 0, 0)) for s in shards]
        travel_groups.append((shards, lands))
    travel_states, travel_token = gather_start(travel_groups, name="ag_weights_start")
    travelling_state = dict(zip(travel_order, travel_states))
    forward_at = {(0, "attn"): [(0, "mid")], (0, "mid"): [(0, "ffn"), (1, "in")], (1, "attn"): [(1, "mid"), (1, "ffn")]}

    def reached(l, stage, marker):
        keys = forward_at.get((l, stage))
        if keys:
            gather_forward([travelling_state[k] for k in keys], marker,
                           name="ag_weights_forward_" + "_".join(f"{g}{ll}" for ll, g in keys))

    def weights_of(l, group, marker):
        if (l, group) == (0, "in"):
            return whole(WEIGHT_GROUPS[group], [w_in0])
        gathered = gather_finish(travelling_state[(l, group)], marker, name=f"ag_weights_finish_{group}{l}")
        return whole(WEIGHT_GROUPS[group], gathered)

    cw_full = cw_all[:, :DEPTH * CONV_K, :CONV_CH // N_DEV].reshape(N_DEV, DEPTH, CONV_K, CONV_CH // N_DEV)
    cw_full = jnp.transpose(cw_full, (1, 2, 0, 3)).reshape(DEPTH, CONV_K, CONV_CH)
    smalls = []
    for l in range(DEPTH):
        sl = {n: P[n][l] if n == "sinks" else P[n][l:l + 1] for n in SMALL}
        sl["conv_w"] = jnp.pad(cw_full[l], ((0, CONV_HALO - CONV_K), (0, 0)))
        smalls.append(sl)
    smalls[0]["norm_mix_g"] = _ordered_after(smalls[0]["norm_mix_g"], travel_token)

    ck_idx = jnp.stack([ci] + [2 * px + py for px, py in _other_chips(xi, yi)]).astype(jnp.int32)
    kc_idx = jnp.stack([2 * xi + yi, ci]).astype(jnp.int32)
    got, flight, reduced = {}, {}, {}

    def as_parts(gb):
        keys = sorted(gb)
        return keys, [gb[k].reshape(N_DEV, rows_of[k[1]] // N_DEV, D_MODEL) for k in keys]

    def lands_like(parts, blocks, dtype):
        return [lax.empty((blocks,) + p.shape[1:], dtype) for p in parts]

    def to_sibling(group, gb):
        keys, parts = as_parts(gb)
        flight[group] = (keys, split_start(parts, lands_like(parts, 4, F32), _sibling_plan, 4,
                                           name=f"rs_sibling_{group}_start"))
        return flight[group][1][4]

    def to_chips(group, marker):
        keys, started = flight[group]
        parts, from_sibling = split_wait(started, marker, _sibling_plan, 4, name=f"rs_sibling_{group}_wait")
        chip_sums = sum_for_chips(parts, from_sibling, ck_idx, name=f"rs_sum_for_chips_{group}")
        started = split_start(chip_sums, lands_like(parts, 3, BF16), _chips_plan, 3, name=f"rs_chips_{group}_start")
        flight[group] = (keys, parts, from_sibling, started)
        return started[4]

    def finish(group, marker):
        keys, parts, from_sibling, started = flight[group]
        _, from_chips = split_wait(started, marker, _chips_plan, 3, name=f"rs_chips_{group}_wait")
        reduced.update(zip(keys, sum_final(parts, from_sibling, from_chips, kc_idx, name=f"rs_sum_final_{group}")))

    def stage_done(l, stage, gb, marker):
        gb = {(l, n): g for n, g in gb.items()}
        if l == 1:
            got.update(gb)
            return to_sibling("l1", got) if stage == "mix" else None
        if stage == "ffn":
            return to_chips("l1", marker) + to_sibling("ffn", gb)
        if stage == "mid":
            return to_chips("ffn", marker) + to_sibling("mid", gb)
        if stage == "attn":
            return to_chips("mid", marker)
        if stage == "in":
            return to_sibling("in", gb)
        to_chips("in", marker)
        for group in ("l1", "ffn", "mid"):
            finish(group, marker)
        return None

    loss_part, grad_x, gss = _local_step(x2d, mem2d, tgt2d, weights_of, reached, smalls, stage_done)
    loss = lax.psum(loss_part, ("x", "y", "c"))

    small_names = SMALL + ("conv_w",)
    small_shapes = [(CONV_K, CONV_CH) if n == "conv_w" else P[n].shape[1:] for n in small_names]
    small_parts = pack_small([[gss[l][n] for l in range(DEPTH)] for n in small_names], small_shapes)
    small_land = lax.dynamic_update_slice(lax.empty((N_DEV,) + small_parts.shape, F32), small_parts[None], (dev, 0, 0))
    small_flight = split_start([small_parts], [small_land], _all_peers_plan, N_DEV - 1, name="ag_small_grads_start")

    grads, delta, new_m, new_v = {}, {}, {}, {}

    def update(names):
        two_d = lambda n, a: a.reshape(P[n].shape[0] * P[n].shape[1], P[n].shape[2])
        for n in names:
            grads[n] = jnp.stack([reduced[(l, n)].T if transposed_of[n] else reduced[(l, n)] for l in range(DEPTH)])
        d_, m_, v_ = adamw_many(*[[two_d(n, src[n]) for n in names] for src in (P, grads, M, V)],
                                name="adamw_" + "_".join(names))
        for i, n in enumerate(names):
            delta[n], new_m[n], new_v[n] = (a[i].reshape(P[n].shape) for a in (d_, m_, v_))

    update([n for n, _, _ in BIG if n != "w_in"])
    finish("in", delta["w_down"])
    update(["w_in"])
    small_all = split_wait(small_flight, delta["w_in"], _all_peers_plan, N_DEV - 1, name="ag_small_grads_wait")[1][0]
    g_, d_, m_, v_ = update_small(small_all, small_shapes, [P[n] for n in SMALL], [M[n] for n in SMALL],
                                  [V[n] for n in SMALL], len(SMALL))
    for i, n in enumerate(SMALL):
        grads[n], delta[n], new_m[n], new_v[n] = g_[i], d_[i], m_[i], v_[i]
    cols = CONV_CH // N_DEV
    grads["conv_w"] = lax.dynamic_slice_in_dim(g_[-1], dev * cols, cols, axis=2)
    flat = lambda a: a.reshape(DEPTH * CONV_K, cols)
    d_, m_, v_ = adamw(flat(conv_w), flat(grads["conv_w"]), flat(m_conv_w), flat(v_conv_w), name="adamw_conv_w")
    delta["conv_w"], new_m["conv_w"], new_v["conv_w"] = (a.reshape(conv_w.shape) for a in (d_, m_, v_))

    return (loss, grad_x[None], *[grads[n] for n in order], *[delta[n] for n in order],
            *[new_m[n] for n in order], *[new_v[n] for n in order])
```

```python
import functools

import jax
import jax.numpy as jnp
import numpy as np
from jax import lax
from jax.experimental import pallas as pl
from jax.experimental.pallas import tpu as pltpu

F32 = jnp.float32
BF16 = jnp.bfloat16

D_MODEL = 1024
HEAD_DIM = 64
N_Q_HEADS = 8
N_KV_HEADS = 2
GROUP = N_Q_HEADS // N_KV_HEADS
ATTN_WIDTH = N_Q_HEADS * HEAD_DIM
KV_WIDTH = N_KV_HEADS * HEAD_DIM
QKV_WIDTH = ATTN_WIDTH + 2 * KV_WIDTH
CONV_CH = 512
IN_COLS = QKV_WIDTH + 2 * CONV_CH
CONV_K = 31
CONV_HALO = 32
BLOCK = 128
N_X_HEADS = 4
X_HEAD_DIM = 256
D_FF = 2816
EPS = 1e-6
NEG = -1e30
DEPTH = 2
N_DEV = 8

ADAM_LR = 0.001
ADAM_B1 = 0.9
ADAM_B2 = 0.999
ADAM_EPS = 1e-08
ADAM_WD = 0.01
ADAM_STEP = 10

V7X_VMEM_LIMIT = 56 * 1024 * 1024
LANES = 128

MESH = pl.DeviceIdType.MESH


def _cp(**kw):
    return pltpu.CompilerParams(vmem_limit_bytes=V7X_VMEM_LIMIT, **kw)


def _dot(a, b, dims):
    return lax.dot_general(a.astype(BF16), b.astype(BF16), (dims, ((), ())), preferred_element_type=F32)


def _dot_nn(a, b):
    return _dot(a, b, ((1,), (0,)))


def _dot_nt(a, b):
    return _dot(a, b, ((1,), (1,)))


def _dot_tn(a, b):
    return _dot(a, b, ((0,), (0,)))


def _sigmoid(x):
    return jax.nn.sigmoid(x)


def _rms(x):
    r = lax.rsqrt(jnp.mean(x * x, axis=-1, keepdims=True) + EPS)
    return x * r, r


def _rms_bwd(dy, xhat, r, g):
    dxh = dy * g
    return r * (dxh - xhat * jnp.mean(dxh * xhat, axis=-1, keepdims=True))


KV_STEPS = 2


def kv_fwd(mem, g, wt_kv):
    n_mem, d = mem.shape
    rows = wt_kv.shape[0] // KV_STEPS

    def body(mem_ref, g_ref, wt_ref, memn_ref, kv_ref):
        @pl.when(pl.program_id(0) == 0)
        def _():
            memn_ref[...] = (_rms(mem_ref[...])[0] * g_ref[...]).astype(memn_ref.dtype)

        kv_ref[...] = _dot_nt(memn_ref[...], wt_ref[...])

    whole = pl.BlockSpec((n_mem, d), lambda j: (0, 0))
    return pl.pallas_call(
        body, name="kv_fwd", grid=(KV_STEPS,),
        in_specs=[whole, pl.BlockSpec((1, d), lambda j: (0, 0)), pl.BlockSpec((rows, d), lambda j: (j, 0))],
        out_specs=[whole, pl.BlockSpec((n_mem, rows), lambda j: (0, j))],
        out_shape=[jax.ShapeDtypeStruct((n_mem, d), BF16), jax.ShapeDtypeStruct((n_mem, wt_kv.shape[0]), F32)],
        compiler_params=_cp(),
    )(mem, g.reshape(1, d), wt_kv)


def kv_bwd(dkv, memn, mem, wt_kv):
    n_mem, d = mem.shape
    rows = wt_kv.shape[0] // KV_STEPS

    def body(dkv_ref, memn_ref, mem_ref, wt_ref, dwt_ref, dg_ref, dmemn_s):
        j = pl.program_id(0)
        dwt_ref[...] = _dot_tn(dkv_ref[...], memn_ref[...])
        part = _dot_nn(dkv_ref[...], wt_ref[...])

        @pl.when(j == 0)
        def _():
            dmemn_s[...] = part

        @pl.when(j > 0)
        def _():
            dmemn_s[...] += part

        @pl.when(j == KV_STEPS - 1)
        def _():
            dg_ref[...] = jnp.sum(dmemn_s[...] * _rms(mem_ref[...])[0], axis=0, keepdims=True)

    whole = pl.BlockSpec((n_mem, d), lambda j: (0, 0))
    return pl.pallas_call(
        body, name="kv_bwd", grid=(KV_STEPS,),
        in_specs=[pl.BlockSpec((n_mem, rows), lambda j: (0, j)), whole, whole, pl.BlockSpec((rows, d), lambda j: (j, 0))],
        out_specs=[pl.BlockSpec((rows, d), lambda j: (j, 0)), pl.BlockSpec((1, d), lambda j: (0, 0))],
        out_shape=[jax.ShapeDtypeStruct(wt_kv.shape, F32), jax.ShapeDtypeStruct((1, d), F32)],
        scratch_shapes=[pltpu.VMEM((n_mem, d), F32)], compiler_params=_cp(),
    )(dkv, memn, mem, wt_kv)


def _tile(n, cap):
    if n <= cap:
        return n
    best = None
    for t in range(LANES, cap + 1, LANES):
        if n % t == 0:
            best = t
    assert best is not None, (n, cap)
    return best


def mm_tn(a, b, *, name, ta_cap=1536, tb_cap=1024, tk=1024):
    m, ka = a.shape
    nb = b.shape[1]
    assert b.shape[0] == m
    tk = min(tk, m)
    ta = _tile(ka, ta_cap)
    tb = _tile(nb, tb_cap)

    def body(a_ref, b_ref, o_ref):
        @pl.when(pl.program_id(2) == 0)
        def _():
            o_ref[...] = jnp.zeros_like(o_ref)

        o_ref[...] += _dot_tn(a_ref[...], b_ref[...])

    return pl.pallas_call(
        body, name=name, grid=(ka // ta, nb // tb, m // tk),
        in_specs=[pl.BlockSpec((tk, ta), lambda i, j, kk: (kk, i)), pl.BlockSpec((tk, tb), lambda i, j, kk: (kk, j))],
        out_specs=pl.BlockSpec((ta, tb), lambda i, j, kk: (i, j)),
        out_shape=jax.ShapeDtypeStruct((ka, nb), F32), compiler_params=_cp(),
    )(a, b)


def _whole(shape):
    return pl.BlockSpec(shape, lambda i: (0,) * len(shape), pipeline_mode=pl.Buffered(1))


def _rows(tm, n):
    return pl.BlockSpec((tm, n), lambda i: (i, 0))


def _vec(n):
    return pl.BlockSpec((1, n), lambda i: (0, 0))


def _chunks(n, cap=1408):
    size = _tile(n, cap)
    return [(s, size) for s in range(0, n, size)]


def _zero_at_first_step(*refs):
    @pl.when(pl.program_id(0) == 0)
    def _():
        for r in refs:
            r[...] = jnp.zeros_like(r)


def norm_proj(x, g, wt, *, tm=1024):
    m, d = x.shape
    n = wt.shape[0]
    tm = min(tm, m)

    def body(x_ref, g_ref, wt_ref, h_ref, u_ref):
        h = (_rms(x_ref[...])[0] * g_ref[...]).astype(BF16)
        h_ref[...] = h
        for s, sz in _chunks(n):
            u_ref[:, s:s + sz] = _dot_nt(h, wt_ref[s:s + sz, :])

    return pl.pallas_call(
        body, name="norm_proj", grid=(m // tm,),
        in_specs=[_rows(tm, d), _vec(d), _whole((n, d))],
        out_specs=[_rows(tm, d), _rows(tm, n)],
        out_shape=[jax.ShapeDtypeStruct((m, d), BF16), jax.ShapeDtypeStruct((m, n), F32)],
        compiler_params=_cp(),
    )(x, g.reshape(1, d), wt)


def _xattn_heads(q_ref, kv_ref, qg_v, kg_v, d):
    normed = []
    for h in range(N_X_HEADS):
        cols = slice(h * X_HEAD_DIM, (h + 1) * X_HEAD_DIM)
        qh, rq = _rms(q_ref[:, cols])
        normed.append((qh, rq, (qh * qg_v).astype(BF16), (_rms(kv_ref[:, cols])[0] * kg_v).astype(BF16),
                       kv_ref[:, d + h * X_HEAD_DIM:d + (h + 1) * X_HEAD_DIM].astype(BF16)))
    scores = [_dot_nt(qn, kn) * (X_HEAD_DIM ** -0.5) for _, _, qn, kn, _ in normed]
    out = []
    for (qh, rq, qn, kn, v), s in zip(normed, scores):
        e = jnp.exp(s - jnp.max(s, axis=-1, keepdims=True))
        out.append((qh, rq, qn, kn, v, e / jnp.sum(e, axis=-1, keepdims=True)))
    return out


def mid_fwd(mixed, x0, w_out, g_x, wq, kv, xqg, xkg, wo, g_f, *, tm=512):
    m, d = x0.shape
    n_mem = kv.shape[0]

    def body(mixed_ref, x0_ref, w_out_ref, g_x_ref, wq_ref, kv_ref, xqg_ref, xkg_ref, wo_ref, g_f_ref,
             x1_ref, h1_ref, qx_ref, o_ref, x2_ref, h2_ref):
        x1 = x0_ref[...] + _dot_nn(mixed_ref[...], w_out_ref[...])
        x1_ref[...] = x1
        h1 = (_rms(x1)[0] * g_x_ref[...]).astype(BF16)
        h1_ref[...] = h1
        qx_ref[...] = _dot_nn(h1, wq_ref[...])
        for h, (_, _, _, _, v, p) in enumerate(_xattn_heads(qx_ref, kv_ref, xqg_ref[...], xkg_ref[...], d)):
            o_ref[:, h * X_HEAD_DIM:(h + 1) * X_HEAD_DIM] = _dot_nn(p, v).astype(o_ref.dtype)
        x2 = x1 + _dot_nn(o_ref[...], wo_ref[...])
        x2_ref[...] = x2
        h2_ref[...] = (_rms(x2)[0] * g_f_ref[...]).astype(BF16)

    sq = _whole((d, d))
    f32_rows, bf_rows = jax.ShapeDtypeStruct((m, d), F32), jax.ShapeDtypeStruct((m, d), BF16)
    return pl.pallas_call(
        body, name="mid_fwd", grid=(m // tm,),
        in_specs=[_rows(tm, d), _rows(tm, d), sq, _vec(d), sq, _whole((n_mem, 2 * d)), _vec(X_HEAD_DIM), _vec(X_HEAD_DIM),
                  sq, _vec(d)],
        out_specs=[_rows(tm, d)] * 6,
        out_shape=[f32_rows, bf_rows, f32_rows, bf_rows, f32_rows, bf_rows],
        compiler_params=_cp(),
    )(mixed, x0, w_out, g_x.reshape(1, d), wq, kv, xqg.reshape(1, X_HEAD_DIM), xkg.reshape(1, X_HEAD_DIM), wo,
      g_f.reshape(1, d))


def ffn_fwd(h2, x2, wt_gu, w_down, target=None, *, tm=256):
    m, d = x2.shape
    f = w_down.shape[0]
    with_loss = target is not None

    def body(*refs):
        if with_loss:
            h2_ref, x2_ref, wt_gu_ref, w_down_ref, t_ref, gu_ref, a_ref, dy_ref, l_ref = refs
        else:
            h2_ref, x2_ref, wt_gu_ref, w_down_ref, gu_ref, a_ref, x3_ref = refs
        h = h2_ref[...]
        for s, sz in _chunks(2 * f):
            gu_ref[:, s:s + sz] = _dot_nt(h, wt_gu_ref[s:s + sz, :])
        for s, sz in _chunks(f):
            g = gu_ref[:, s:s + sz]
            a_ref[:, s:s + sz] = (g * _sigmoid(g) * gu_ref[:, f + s:f + s + sz]).astype(a_ref.dtype)
        x3 = x2_ref[...] + _dot_nn(a_ref[...], w_down_ref[...])
        if not with_loss:
            x3_ref[...] = x3
            return
        err = x3 - t_ref[...]
        dy_ref[...] = err * (1.0 / d)
        _zero_at_first_step(l_ref)
        part = jnp.sum(jnp.sum(err * err, axis=-1, keepdims=True), axis=0, keepdims=True)
        l_ref[...] += jnp.broadcast_to(part * (0.5 / d), l_ref.shape)

    last = [_rows(tm, d), pl.BlockSpec((1, LANES), lambda i: (0, 0))] if with_loss else [_rows(tm, d)]
    last_shape = [jax.ShapeDtypeStruct((m, d), F32)] + ([jax.ShapeDtypeStruct((1, LANES), F32)] if with_loss else [])
    return pl.pallas_call(
        body, name="ffn_fwd_loss" if with_loss else "ffn_fwd", grid=(m // tm,),
        in_specs=[_rows(tm, d), _rows(tm, d), _whole((2 * f, d)), _whole((f, d))] + ([_rows(tm, d)] if with_loss else []),
        out_specs=[_rows(tm, 2 * f), _rows(tm, f)] + last,
        out_shape=[jax.ShapeDtypeStruct((m, 2 * f), F32), jax.ShapeDtypeStruct((m, f), BF16)] + last_shape,
        compiler_params=_cp(),
    )(*([h2, x2, wt_gu, w_down] + ([target] if with_loss else [])))


def ffn_bwd(dx3, gu, x2, g_f, w_down, wt_gu, *, tm=256):
    m, d = x2.shape
    f = w_down.shape[0]

    def body(dx3_ref, gu_ref, x2_ref, g_ref, w_down_ref, wt_gu_ref, dgu_ref, dx2_ref, dg_ref):
        _zero_at_first_step(dg_ref)
        dx3 = dx3_ref[...]
        dx3_b = dx3.astype(BF16)
        for s, sz in _chunks(f):
            da = _dot_nt(dx3_b, w_down_ref[s:s + sz, :])
            g = gu_ref[:, s:s + sz]
            u = gu_ref[:, f + s:f + s + sz]
            sg = _sigmoid(g)
            dgu_ref[:, s:s + sz] = (da * u * (sg * (1.0 + g * (1.0 - sg)))).astype(dgu_ref.dtype)
            dgu_ref[:, f + s:f + s + sz] = (da * (g * sg)).astype(dgu_ref.dtype)
        dh2 = _dot_nn(dgu_ref[...], wt_gu_ref[...])
        xh, r = _rms(x2_ref[...])
        dg_ref[...] += jnp.sum(dh2 * xh, axis=0, keepdims=True)
        dx2_ref[...] = dx3 + _rms_bwd(dh2, xh, r, g_ref[...])

    return pl.pallas_call(
        body, name="ffn_bwd", grid=(m // tm,),
        in_specs=[_rows(tm, d), _rows(tm, 2 * f), _rows(tm, d), _vec(d), _whole((f, d)), _whole((2 * f, d))],
        out_specs=[_rows(tm, 2 * f), _rows(tm, d), _vec(d)],
        out_shape=[jax.ShapeDtypeStruct((m, 2 * f), BF16), jax.ShapeDtypeStruct((m, d), F32),
                   jax.ShapeDtypeStruct((1, d), F32)],
        compiler_params=_cp(),
    )(dx3, gu, x2, g_f.reshape(1, d), w_down, wt_gu)


def mid_bwd(dx2, qx, kv, xqg, xkg, x1, g_x, wo, wq, w_out, *, tm=512):
    m, d = x1.shape
    n_mem = kv.shape[0]
    nt = m // tm

    def body(dx2_ref, qx_ref, kv_ref, xqg_ref, xkg_ref, x1_ref, g_x_ref, wo_ref, wq_ref, w_out_ref,
             dq_ref, dx1_ref, dmixed_ref, dkv_ref, dqg_ref, dkg_ref, dg_ref):
        i = pl.program_id(0)
        _zero_at_first_step(dkv_ref, dqg_ref, dkg_ref, dg_ref)
        qg_v, kg_v = xqg_ref[...], xkg_ref[...]
        dx2 = dx2_ref[...]
        do = _dot_nt(dx2, wo_ref[...])
        dqg_acc = jnp.zeros((1, X_HEAD_DIM), F32)
        heads = _xattn_heads(qx_ref, kv_ref, qg_v, kg_v, d)
        head_cols = [slice(h * X_HEAD_DIM, (h + 1) * X_HEAD_DIM) for h in range(N_X_HEADS)]
        do_h = [do[:, cols].astype(BF16) for cols in head_cols]
        dps = [_dot_nt(do_h[h], heads[h][4]) for h in range(N_X_HEADS)]
        dss = []
        for (_, _, _, _, _, p), dp in zip(heads, dps):
            dss.append((p.astype(BF16), (p * (dp - jnp.sum(p * dp, axis=-1, keepdims=True))).astype(BF16)))
        for h, ((qh, rq, qn, kn, _, _), (p, ds)) in enumerate(zip(heads, dss)):
            cols = head_cols[h]
            vcols = slice(d + h * X_HEAD_DIM, d + (h + 1) * X_HEAD_DIM)
            dkv_ref[:, vcols] += _dot_tn(p, do_h[h])
            dqn = _dot_nn(ds, kn) * (X_HEAD_DIM ** -0.5)
            dkv_ref[:, cols] += _dot_tn(ds, qn) * (X_HEAD_DIM ** -0.5)
            dqg_acc = dqg_acc + jnp.sum(dqn * qh, axis=0, keepdims=True)
            dq_ref[:, cols] = _rms_bwd(dqn, qh, rq, qg_v).astype(dq_ref.dtype)
        dqg_ref[...] += dqg_acc
        dh1 = _dot_nt(dq_ref[...], wq_ref[...])
        xh, r = _rms(x1_ref[...])
        dg_ref[...] += jnp.sum(dh1 * xh, axis=0, keepdims=True)
        dx1 = dx2 + _rms_bwd(dh1, xh, r, g_x_ref[...])
        dx1_ref[...] = dx1
        dmixed_ref[...] = _dot_nt(dx1, w_out_ref[...])

        @pl.when(i == nt - 1)
        def _():
            dkg_acc = jnp.zeros((1, X_HEAD_DIM), F32)
            for h in range(N_X_HEADS):
                cols = slice(h * X_HEAD_DIM, (h + 1) * X_HEAD_DIM)
                kh, rk = _rms(kv_ref[:, cols])
                dkn = dkv_ref[:, cols]
                dkg_acc = dkg_acc + jnp.sum(dkn * kh, axis=0, keepdims=True)
                dkv_ref[:, cols] = _rms_bwd(dkn, kh, rk, kg_v)
            dkg_ref[...] = dkg_acc

    sq = _whole((d, d))
    full = pl.BlockSpec((n_mem, 2 * d), lambda i: (0, 0))
    return pl.pallas_call(
        body, name="mid_bwd", grid=(nt,),
        in_specs=[_rows(tm, d), _rows(tm, d), _whole((n_mem, 2 * d)), _vec(X_HEAD_DIM), _vec(X_HEAD_DIM), _rows(tm, d),
                  _vec(d), sq, sq, sq],
        out_specs=[_rows(tm, d), _rows(tm, d), _rows(tm, d), full, _vec(X_HEAD_DIM), _vec(X_HEAD_DIM), _vec(d)],
        out_shape=[jax.ShapeDtypeStruct((m, d), BF16), jax.ShapeDtypeStruct((m, d), F32), jax.ShapeDtypeStruct((m, d), F32),
                   jax.ShapeDtypeStruct((n_mem, 2 * d), F32), jax.ShapeDtypeStruct((1, X_HEAD_DIM), F32),
                   jax.ShapeDtypeStruct((1, X_HEAD_DIM), F32), jax.ShapeDtypeStruct((1, d), F32)],
        compiler_params=_cp(),
    )(dx2, qx, kv, xqg.reshape(1, X_HEAD_DIM), xkg.reshape(1, X_HEAD_DIM), x1, g_x.reshape(1, d), wo, wq, w_out)


def in_bwd(du, wt_in, x0, g_mix, dx1, *, tm=1024):
    m, d = x0.shape
    n = wt_in.shape[0]
    tm = min(tm, m)

    def body(du_ref, wt_ref, x0_ref, g_ref, dx1_ref, dx0_ref, dg_ref):
        _zero_at_first_step(dg_ref)
        dh0 = _dot_nn(du_ref[...], wt_ref[...])
        xh, r = _rms(x0_ref[...])
        dg_ref[...] += jnp.sum(dh0 * xh, axis=0, keepdims=True)
        dx0_ref[...] = dx1_ref[...] + _rms_bwd(dh0, xh, r, g_ref[...])

    return pl.pallas_call(
        body, name="in_bwd", grid=(m // tm,),
        in_specs=[_rows(tm, n), _whole((n, d)), _rows(tm, d), _vec(d), _rows(tm, d)],
        out_specs=[_rows(tm, d), _vec(d)],
        out_shape=[jax.ShapeDtypeStruct((m, d), F32), jax.ShapeDtypeStruct((1, d), F32)],
        compiler_params=_cp(),
    )(du, wt_in, x0, g_mix.reshape(1, d), dx1)


SWA_TILE = 512
SWA_SUB = SWA_TILE // BLOCK
SWA_KEYS = SWA_TILE + BLOCK
PAIR = 2 * HEAD_DIM
KCOL = ATTN_WIDTH
VCOL = ATTN_WIDTH + KV_WIDTH


def _swa_constants():
    r = np.arange(2 * BLOCK)[:, None]
    j = np.arange(4 * BLOCK)[None, :]
    dist = (r % BLOCK) + BLOCK - (j % (2 * BLOCK))
    valid = (dist >= 0) & (dist < BLOCK)
    first_valid = valid & ((j % (2 * BLOCK)) >= BLOCK)
    bias, bias_first = [], []
    for kv in range(N_KV_HEADS):
        head = kv * GROUP + 2 * (r // BLOCK) + j // (2 * BLOCK)
        b = -(2.0 ** -(head + 1.0)) * dist
        bias.append(np.where(valid, b, NEG))
        bias_first.append(np.where(first_valid, b, NEG))
    lane = np.arange(LANES)
    seg = (lane[:, None] // HEAD_DIM == lane[None, :] // HEAD_DIM) / HEAD_DIM
    row = np.arange(4 * BLOCK)[:, None]
    ones = (row // (2 * BLOCK)) == (lane[None, :] // HEAD_DIM)
    return (jnp.asarray(np.stack(bias), F32), jnp.asarray(np.stack(bias_first), F32), jnp.asarray(seg, BF16),
            jnp.asarray(ones, BF16))


def _segmean(x, seg_ref):
    hi = x.astype(BF16)
    lo = (x - hi.astype(F32)).astype(BF16)
    return _dot_nn(hi, seg_ref[...]) + _dot_nn(lo, seg_ref[...])


def _two_heads(x, kv):
    lane = lax.broadcasted_iota(jnp.int32, (1, LANES), 1)
    mine = (lane < HEAD_DIM) if kv == 0 else (lane >= HEAD_DIM)
    base = jnp.where(mine, x, 0.0)
    other = pltpu.roll(base, HEAD_DIM, 1)
    return jnp.concatenate([base, other] if kv == 0 else [other, base], axis=0)


def _from_two_heads(y, kv):
    rows = y.shape[0] // 2
    lane = lax.broadcasted_iota(jnp.int32, (1, LANES), 1)
    top, bot = y[:rows], y[rows:]
    if kv == 0:
        return jnp.where(lane < HEAD_DIM, top + pltpu.roll(bot, HEAD_DIM, 1), 0.0)
    return jnp.where(lane >= HEAD_DIM, pltpu.roll(top, HEAD_DIM, 1) + bot, 0.0)


def _pair_rows(ref, rows, kv):
    c = kv * 2 * PAIR
    return jnp.concatenate([ref[rows, c:c + PAIR], ref[rows, c + PAIR:c + 2 * PAIR]], axis=0)


def _head_cols(fn, kv):
    return [jnp.concatenate([fn(kv * GROUP + half), fn(kv * GROUP + 2 + half)], axis=0) for half in range(2)]


def _swa_prologue(cur_ref, prev_ref, qg_ref, kg_ref, seg_ref, qg_s, kn_s, v_s):
    qg_s[...] = (cur_ref[:, 0:ATTN_WIDTH] * qg_ref[...]).astype(BF16)
    k = jnp.concatenate([prev_ref[:, KCOL:KCOL + KV_WIDTH], cur_ref[:, KCOL:KCOL + KV_WIDTH]], axis=0)
    kn_s[...] = k * lax.rsqrt(_segmean(k * k, seg_ref) + EPS) * kg_ref[...]
    v_s[0:BLOCK, :] = prev_ref[:, VCOL:VCOL + KV_WIDTH]
    v_s[BLOCK:SWA_KEYS, :] = cur_ref[:, VCOL:VCOL + KV_WIDTH]


def _swa_products(qg_s, kn_s, rows, keys, kv):
    q2 = _pair_rows(qg_s, rows, kv)
    k2 = _two_heads(kn_s[keys, :], kv)
    return q2, k2, _dot_nt(q2, k2)


def _swa_scores(cur_ref, sinks_ref, qg_s, kn_s, bias, rows, keys, kv):
    q2, k2, t = _swa_products(qg_s, kn_s, rows, keys, kv)
    return q2, k2, t, _swa_softmax(cur_ref, sinks_ref, t, bias, rows, kv)


def _swa_cols(cur_ref, sinks_ref, rows, kv):
    def rq(h):
        x = cur_ref[rows, h * HEAD_DIM:(h + 1) * HEAD_DIM]
        return lax.rsqrt(jnp.mean(x * x, axis=-1, keepdims=True) + EPS)

    scale = _head_cols(lambda h: rq(h) * (HEAD_DIM ** -0.5), kv)
    sink = _head_cols(lambda h: jnp.full((BLOCK, 1), sinks_ref[h], F32), kv)
    return list(zip(scale, sink))


def _swa_softmax(cur_ref, sinks_ref, t, bias, rows, kv):
    halves = []
    for half, (scale, sink) in enumerate(_swa_cols(cur_ref, sinks_ref, rows, kv)):
        cols = slice(half * 2 * BLOCK, (half + 1) * 2 * BLOCK)
        s = t[:, cols] * scale + bias[:, cols]
        mx = jnp.maximum(jnp.max(s, axis=-1, keepdims=True), sink)
        halves.append((scale, jnp.exp(s - mx), jnp.exp(sink - mx), mx))
    return halves


def _swa_probs_from_lse(cur_ref, sinks_ref, lse_ref, t, bias, rows, kv):
    halves = []
    for half, (scale, sink) in enumerate(_swa_cols(cur_ref, sinks_ref, rows, kv)):
        cols = slice(half * 2 * BLOCK, (half + 1) * 2 * BLOCK)
        lse = _head_cols(lambda h: lse_ref[rows, h * HEAD_DIM:h * HEAD_DIM + 1], kv)[half]
        halves.append((scale, jnp.exp(t[:, cols] * scale + bias[:, cols] - lse), jnp.exp(sink - lse)))
    return halves


def swa_fwd(u, qg, kg, sinks):
    t_rows = u.shape[0]
    nt = t_rows // SWA_TILE
    bias_c, bias_first_c, seg_c, ones_c = _swa_constants()

    def body(sinks_ref, cur_ref, prev_ref, qg_ref, kg_ref, seg_ref, bias_ref, biasf_ref, ones_ref, o_ref, lse_ref,
             qg_s, kn_s, v_s):
        i = pl.program_id(0)
        _swa_prologue(cur_ref, prev_ref, qg_ref, kg_ref, seg_ref, qg_s, kn_s, v_s)
        lane = lax.broadcasted_iota(jnp.int32, (1, LANES), 1)
        work = [(b, kv, slice(b * BLOCK, (b + 1) * BLOCK), slice(b * BLOCK, (b + 2) * BLOCK))
                for b in range(SWA_SUB) for kv in range(N_KV_HEADS)]
        products = [_swa_products(qg_s, kn_s, rows, keys, kv)[2] for _, kv, rows, keys in work]
        scored = []
        for (b, kv, rows, _), t in zip(work, products):
            bias = jnp.where(i == 0, biasf_ref[kv], bias_ref[kv]) if b == 0 else bias_ref[kv]
            halves = _swa_softmax(cur_ref, sinks_ref, t, bias, rows, kv)
            scored.append((jnp.concatenate([halves[0][1], halves[1][1]], axis=1).astype(BF16), halves[0][2], halves[1][2],
                           halves[0][3], halves[1][3]))
        for (b, kv, rows, keys), (e, es0, es1, mx0, mx1) in zip(work, scored):
            v2 = jnp.concatenate([_two_heads(v_s[keys, :], kv).astype(BF16), ones_ref[...]], axis=1)
            ox = _dot_nn(e, v2)
            den = ox[:, LANES:] + jnp.where(lane < HEAD_DIM, es0, es1)
            out = (ox[:, :LANES] / den).astype(o_ref.dtype)
            c = kv * 2 * PAIR
            o_ref[rows, c:c + PAIR] = out[:BLOCK]
            o_ref[rows, c + PAIR:c + 2 * PAIR] = out[BLOCK:]
            lse = jnp.where(lane < HEAD_DIM, mx0, mx1) + jnp.log(den)
            lse_ref[rows, c:c + PAIR] = lse[:BLOCK]
            lse_ref[rows, c + PAIR:c + 2 * PAIR] = lse[BLOCK:]

    const3 = pl.BlockSpec((N_KV_HEADS, 2 * BLOCK, 4 * BLOCK), lambda i: (0, 0, 0))
    return pl.pallas_call(
        body, name="swa_fwd", grid=(nt,),
        in_specs=[
            pl.BlockSpec(memory_space=pltpu.SMEM),
            pl.BlockSpec((SWA_TILE, QKV_WIDTH), lambda i: (i, 0)),
            pl.BlockSpec((BLOCK, QKV_WIDTH), lambda i: (jnp.maximum(i * SWA_SUB - 1, 0), 0)),
            pl.BlockSpec((1, ATTN_WIDTH), lambda i: (0, 0)), pl.BlockSpec((1, KV_WIDTH), lambda i: (0, 0)),
            pl.BlockSpec((LANES, LANES), lambda i: (0, 0)), const3, const3,
            pl.BlockSpec((4 * BLOCK, LANES), lambda i: (0, 0)),
        ],
        out_specs=[pl.BlockSpec((SWA_TILE, ATTN_WIDTH), lambda i: (i, 0))] * 2,
        out_shape=[jax.ShapeDtypeStruct((t_rows, 2 * ATTN_WIDTH), BF16), jax.ShapeDtypeStruct((t_rows, ATTN_WIDTH), F32)],
        scratch_shapes=[pltpu.VMEM((SWA_TILE, ATTN_WIDTH), BF16), pltpu.VMEM((SWA_KEYS, KV_WIDTH), F32),
                        pltpu.VMEM((SWA_KEYS, KV_WIDTH), F32)],
        compiler_params=_cp(),
    )(sinks, u, u, jnp.tile(qg, N_Q_HEADS).reshape(1, ATTN_WIDTH), jnp.tile(kg, N_KV_HEADS).reshape(1, KV_WIDTH),
      seg_c, bias_c, bias_first_c, ones_c)


def swa_bwd(u, lse, dmixed, qg, kg, sinks):
    t_rows = u.shape[0]
    nt = t_rows // SWA_TILE
    bias_c, bias_first_c, seg_c, _ = _swa_constants()

    def body(sinks_ref, cur_ref, prev_ref, lse_ref, do_ref, qg_ref, kg_ref, seg_ref, bias_ref, biasf_ref,
             du_ref, dqg_ref, dkg_ref, dsk_ref, qg_s, kn_s, v_s, acck_s, accv_s, carryk_s, carryv_s):
        step = pl.program_id(0)
        i = nt - 1 - step

        @pl.when(step == 0)
        def _():
            for r in (carryk_s, carryv_s, dqg_ref, dkg_ref, dsk_ref):
                r[...] = jnp.zeros_like(r)

        _swa_prologue(cur_ref, prev_ref, qg_ref, kg_ref, seg_ref, qg_s, kn_s, v_s)
        for acc, carry in ((acck_s, carryk_s), (accv_s, carryv_s)):
            acc[0:SWA_TILE, :] = jnp.zeros((SWA_TILE, KV_WIDTH), F32)
            acc[SWA_TILE:SWA_KEYS, :] = carry[...]

        lane = lax.broadcasted_iota(jnp.int32, (1, LANES), 1)
        g_pair = qg_ref[:, 0:PAIR]
        dqg_acc = jnp.zeros((1, PAIR), F32)
        dsk_acc = jnp.zeros((1, LANES), F32)
        work = [(b, kv, slice(b * BLOCK, (b + 1) * BLOCK), slice(b * BLOCK, (b + 2) * BLOCK))
                for b in range(SWA_SUB) for kv in range(N_KV_HEADS)]
        products = []
        for _, kv, rows, keys in work:
            q2, k2, t = _swa_products(qg_s, kn_s, rows, keys, kv)
            do2 = _pair_rows(do_ref, rows, kv).astype(BF16)
            products.append((q2, k2, t, do2, _dot_nt(do2, _two_heads(v_s[keys, :], kv))))
        exps = []
        for (b, kv, rows, _), (_, _, t, _, _) in zip(work, products):
            bias = jnp.where(i == 0, biasf_ref[kv], bias_ref[kv]) if b == 0 else bias_ref[kv]
            exps.append(_swa_probs_from_lse(cur_ref, sinks_ref, lse_ref, t, bias, rows, kv))
        softmaxed = []
        for (b, kv, rows, _), (_, _, t, _, dp), halves in zip(work, products, exps):
            p_parts, dt_parts, coef = [], [], []
            for half, (scale, p, p_sink) in enumerate(halves):
                cols = slice(half * 2 * BLOCK, (half + 1) * 2 * BLOCK)
                dp_h = dp[:, cols]
                delta = jnp.sum(p * dp_h, axis=-1, keepdims=True)
                ds = p * (dp_h - delta)
                dsink = -p_sink * delta
                for pair in range(2):
                    part = jnp.sum(dsink[pair * BLOCK:(pair + 1) * BLOCK], axis=0, keepdims=True)
                    dsk_acc = dsk_acc + jnp.where(lane == kv * GROUP + 2 * pair + half, part, 0.0)
                dscale = jnp.sum(ds * t[:, cols], axis=-1, keepdims=True)
                coef.append(-dscale * scale * scale * scale)
                p_parts.append(p.astype(BF16))
                dt_parts.append((ds * scale).astype(BF16))
            softmaxed.append((jnp.concatenate(p_parts, axis=1), jnp.concatenate(dt_parts, axis=1),
                              jnp.where(lane < HEAD_DIM, coef[0], coef[1])))
        for (_, kv, rows, keys), (q2, k2, _, do2, _), (p2, dt, coef) in zip(work, products, softmaxed):
            dqg2 = _dot_nn(dt, k2)
            q_raw = _pair_rows(cur_ref, rows, kv)
            dq = dqg2 * g_pair + coef * q_raw
            dqg_acc = dqg_acc + jnp.sum(dqg2 * q_raw, axis=0, keepdims=True)
            c = kv * 2 * PAIR
            du_ref[rows, c:c + PAIR] = dq[:BLOCK].astype(du_ref.dtype)
            du_ref[rows, c + PAIR:c + 2 * PAIR] = dq[BLOCK:].astype(du_ref.dtype)
        to_keys = [(_from_two_heads(_dot_tn(dt, q2), kv), _from_two_heads(_dot_tn(p2, do2), kv))
                   for (_, kv, _, _), (q2, _, _, do2, _), (p2, dt, _) in zip(work, products, softmaxed)]
        for (_, _, _, keys), (dk, dv) in zip(work, to_keys):
            acck_s[keys, :] += dk
            accv_s[keys, :] += dv
        dqg_ref[...] += dqg_acc + pltpu.roll(dqg_acc, HEAD_DIM, 1)
        dsk_ref[...] += dsk_acc

        own = slice(BLOCK, SWA_KEYS)
        k = cur_ref[:, KCOL:KCOL + KV_WIDTH]
        rk = lax.rsqrt(_segmean(k * k, seg_ref) + EPS)
        kh = k * rk
        dkn = acck_s[own, :]
        dkh = dkn * kg_ref[...]
        du_ref[:, KCOL:KCOL + KV_WIDTH] = (rk * (dkh - kh * _segmean(dkh * kh, seg_ref))).astype(du_ref.dtype)
        du_ref[:, VCOL:VCOL + KV_WIDTH] = accv_s[own, :].astype(du_ref.dtype)
        dkg_part = jnp.sum(dkn * kh, axis=0, keepdims=True)
        dkg_ref[...] += dkg_part + pltpu.roll(dkg_part, HEAD_DIM, 1)
        carryk_s[...] = acck_s[0:BLOCK, :]
        carryv_s[...] = accv_s[0:BLOCK, :]

    const3 = pl.BlockSpec((N_KV_HEADS, 2 * BLOCK, 4 * BLOCK), lambda s: (0, 0, 0))
    vec = pl.BlockSpec((1, LANES), lambda s: (0, 0))
    return pl.pallas_call(
        body, name="swa_bwd", grid=(nt,),
        in_specs=[
            pl.BlockSpec(memory_space=pltpu.SMEM),
            pl.BlockSpec((SWA_TILE, QKV_WIDTH), lambda s: (nt - 1 - s, 0)),
            pl.BlockSpec((BLOCK, QKV_WIDTH), lambda s: (jnp.maximum((nt - 1 - s) * SWA_SUB - 1, 0), 0)),
            pl.BlockSpec((SWA_TILE, ATTN_WIDTH), lambda s: (nt - 1 - s, 0)),
            pl.BlockSpec((SWA_TILE, ATTN_WIDTH), lambda s: (nt - 1 - s, 0)),
            pl.BlockSpec((1, ATTN_WIDTH), lambda s: (0, 0)), vec,
            pl.BlockSpec((LANES, LANES), lambda s: (0, 0)), const3, const3,
        ],
        out_specs=[pl.BlockSpec((SWA_TILE, QKV_WIDTH), lambda s: (nt - 1 - s, 0)), vec, vec, vec],
        out_shape=[jax.ShapeDtypeStruct((t_rows, IN_COLS), BF16)] + [jax.ShapeDtypeStruct((1, LANES), F32)] * 3,
        scratch_shapes=[pltpu.VMEM((SWA_TILE, ATTN_WIDTH), BF16)] + [pltpu.VMEM((SWA_KEYS, KV_WIDTH), F32)] * 4
        + [pltpu.VMEM((BLOCK, KV_WIDTH), F32)] * 2,
        compiler_params=_cp(),
    )(sinks, u, u, lse, dmixed, jnp.tile(qg, N_Q_HEADS).reshape(1, ATTN_WIDTH), jnp.tile(kg, N_KV_HEADS).reshape(1, KV_WIDTH),
      seg_c, bias_c, bias_first_c)


CONV_TILE = 512
CONV_CHUNK = 64
VAL0 = QKV_WIDTH
GATE0 = QKV_WIDTH + CONV_CH


def _glu(ref):
    return ref[:, VAL0:GATE0] * _sigmoid(ref[:, GATE0:GATE0 + CONV_CH])


SUBLANES = 8
CONV_BUF = CONV_HALO + CONV_TILE + SUBLANES
CONV_EXT = CONV_HALO + CONV_TILE


def _fill_shifted(sh_ref):
    for r in range(1, SUBLANES):
        sh_ref[r, 0:CONV_EXT, :] = sh_ref[0, pl.ds(r, CONV_EXT), :]


def _shifted(sh_ref, start, offset, n):
    return sh_ref[offset % SUBLANES, pl.ds(start + offset - offset % SUBLANES, n), :]


def _layernorm_stats(y):
    mu = jnp.mean(y, axis=-1, keepdims=True)
    yc = y - mu
    rstd = lax.rsqrt(jnp.mean(yc * yc, axis=-1, keepdims=True) + EPS)
    return yc * rstd, rstd


def conv_fwd(u, mixed, conv_w, conv_b, ln_g, ln_b):
    t = u.shape[0]
    nt = t // CONV_TILE
    per = CONV_TILE // CONV_HALO

    def body(cur_ref, prev_ref, mixed_ref, w_ref, b_ref, g_ref, b2_ref, o_ref, y_ref, gl_ref):
        del mixed_ref
        i = pl.program_id(0)
        gl_ref[0, 0:CONV_HALO, :] = jnp.where(i > 0, _glu(prev_ref), 0.0)
        gl_ref[0, CONV_HALO:CONV_EXT, :] = _glu(cur_ref)
        gl_ref[0, CONV_EXT:CONV_BUF, :] = jnp.zeros((SUBLANES, CONV_CH), F32)
        _fill_shifted(gl_ref)
        for c0 in range(0, CONV_TILE, CONV_CHUNK):
            acc = jnp.broadcast_to(b_ref[...], (CONV_CHUNK, CONV_CH))
            for k in range(CONV_K):
                acc = acc + w_ref[k:k + 1, :] * _shifted(gl_ref, c0, 2 + k, CONV_CHUNK)
            y_ref[c0:c0 + CONV_CHUNK, :] = acc
        yh, _ = _layernorm_stats(y_ref[...])
        yln = yh * g_ref[...] + b2_ref[...]
        o_ref[...] = (yln * _sigmoid(yln)).astype(o_ref.dtype)

    vec = pl.BlockSpec((1, CONV_CH), lambda i: (0, 0))
    return pl.pallas_call(
        body, name="conv_fwd", grid=(nt,),
        in_specs=[
            pl.BlockSpec((CONV_TILE, IN_COLS), lambda i: (i, 0)),
            pl.BlockSpec((CONV_HALO, IN_COLS), lambda i: (jnp.maximum(i * per - 1, 0), 0)),
            pl.BlockSpec(memory_space=pl.ANY),
            pl.BlockSpec((CONV_HALO, CONV_CH), lambda i: (0, 0)),
            vec, vec, vec,
        ],
        out_specs=[pl.BlockSpec((CONV_TILE, CONV_CH), lambda i: (i, 1)), pl.BlockSpec((CONV_TILE, CONV_CH), lambda i: (i, 0))],
        out_shape=[jax.ShapeDtypeStruct(mixed.shape, mixed.dtype), jax.ShapeDtypeStruct((t, CONV_CH), F32)],
        scratch_shapes=[pltpu.VMEM((SUBLANES, CONV_BUF, CONV_CH), F32)],
        input_output_aliases={2: 0}, compiler_params=_cp(),
    )(u, u, mixed, conv_w, conv_b.reshape(1, CONV_CH), ln_g.reshape(1, CONV_CH), ln_b.reshape(1, CONV_CH))


def conv_bwd(u, y, dmixed, du, conv_w, ln_g, ln_b):
    t = u.shape[0]
    nt = t // CONV_TILE
    per = CONV_TILE // CONV_HALO

    def body(cur_ref, prev_ref, y_ref, yn_ref, do_ref, don_ref, du_in_ref, w_ref, g_ref, b2_ref,
             du_ref, dw_ref, dvec_ref, gl_ref, dy_ref):
        i = pl.program_id(0)
        last = i == nt - 1
        _zero_at_first_step(dw_ref, dvec_ref)

        gl_ref[0, 0:CONV_HALO, :] = jnp.where(i > 0, _glu(prev_ref), 0.0)
        gl_ref[0, CONV_HALO:CONV_EXT, :] = _glu(cur_ref)
        gl_ref[0, CONV_EXT:CONV_BUF, :] = jnp.zeros((SUBLANES, CONV_CH), F32)
        _fill_shifted(gl_ref)

        yh, rstd = _layernorm_stats(jnp.concatenate([y_ref[...], yn_ref[...]], axis=0))
        g = g_ref[...]
        yln = yh * g + b2_ref[...]
        sg = _sigmoid(yln)
        dout = jnp.concatenate([do_ref[...], jnp.where(last, 0.0, don_ref[...])], axis=0)
        dyln = dout * (sg * (1.0 + yln * (1.0 - sg)))
        dyh = dyln * g
        dy = rstd * (dyh - jnp.mean(dyh, axis=-1, keepdims=True) - yh * jnp.mean(dyh * yh, axis=-1, keepdims=True))
        dy_ref[0, 0:CONV_EXT, :] = dy
        dy_ref[0, CONV_EXT:CONV_BUF, :] = jnp.zeros((SUBLANES, CONV_CH), F32)
        _fill_shifted(dy_ref)

        own = slice(0, CONV_TILE)
        dvec_ref[0:1, :] += jnp.sum(dy[own], axis=0, keepdims=True)
        dvec_ref[1:2, :] += jnp.sum(dyln[own] * yh[own], axis=0, keepdims=True)
        dvec_ref[2:3, :] += jnp.sum(dyln[own], axis=0, keepdims=True)
        for k in range(CONV_K):
            dw_ref[k:k + 1, :] += jnp.sum(dy[own] * _shifted(gl_ref, 0, 2 + k, CONV_TILE), axis=0, keepdims=True)

        for c0 in range(0, CONV_TILE, CONV_CHUNK):
            acc = jnp.zeros((CONV_CHUNK, CONV_CH), F32)
            for k in range(CONV_K):
                acc = acc + w_ref[k:k + 1, :] * _shifted(dy_ref, c0, CONV_K - 1 - k, CONV_CHUNK)
            rows = slice(c0, c0 + CONV_CHUNK)
            val = cur_ref[rows, VAL0:GATE0]
            sgate = _sigmoid(cur_ref[rows, GATE0:GATE0 + CONV_CH])
            du_ref[rows, VAL0:GATE0] = (acc * sgate).astype(du_ref.dtype)
            du_ref[rows, GATE0:GATE0 + CONV_CH] = (acc * val * sgate * (1.0 - sgate)).astype(du_ref.dtype)
        du_ref[:, 0:QKV_WIDTH] = du_in_ref[:, 0:QKV_WIDTH]

    vec = pl.BlockSpec((1, CONV_CH), lambda i: (0, 0))
    n_halo = t // CONV_HALO
    return pl.pallas_call(
        body, name="conv_bwd", grid=(nt,),
        in_specs=[
            pl.BlockSpec((CONV_TILE, IN_COLS), lambda i: (i, 0)),
            pl.BlockSpec((CONV_HALO, IN_COLS), lambda i: (jnp.maximum(i * per - 1, 0), 0)),
            pl.BlockSpec((CONV_TILE, CONV_CH), lambda i: (i, 0)),
            pl.BlockSpec((CONV_HALO, CONV_CH), lambda i: (jnp.minimum((i + 1) * per, n_halo - 1), 0)),
            pl.BlockSpec((CONV_TILE, CONV_CH), lambda i: (i, 1)),
            pl.BlockSpec((CONV_HALO, CONV_CH), lambda i: (jnp.minimum((i + 1) * per, n_halo - 1), 1)),
            pl.BlockSpec((CONV_TILE, IN_COLS), lambda i: (i, 0)),
            pl.BlockSpec((CONV_HALO, CONV_CH), lambda i: (0, 0)),
            vec, vec,
        ],
        out_specs=[
            pl.BlockSpec((CONV_TILE, IN_COLS), lambda i: (i, 0)),
            pl.BlockSpec((CONV_HALO, CONV_CH), lambda i: (0, 0)),
            pl.BlockSpec((8, CONV_CH), lambda i: (0, 0)),
        ],
        out_shape=[
            jax.ShapeDtypeStruct(du.shape, du.dtype),
            jax.ShapeDtypeStruct((CONV_HALO, CONV_CH), F32),
            jax.ShapeDtypeStruct((8, CONV_CH), F32),
        ],
        scratch_shapes=[pltpu.VMEM((SUBLANES, CONV_BUF, CONV_CH), F32), pltpu.VMEM((SUBLANES, CONV_BUF, CONV_CH), F32)],
        input_output_aliases={6: 0}, compiler_params=_cp(),
    )(u, u, y, y, dmixed, dmixed, du, conv_w, ln_g.reshape(1, CONV_CH), ln_b.reshape(1, CONV_CH))


def adamw(w, g, m, v, *, name):
    r, c = w.shape
    tr = r
    for cand in (512, 256, 128, 64, 32, 16, 8):
        if r % cand == 0 and r > cand:
            tr = cand
            break

    def body(w_ref, g_ref, m_ref, v_ref, d_ref, nm_ref, nv_ref):
        d_ref[...], nm_ref[...], nv_ref[...] = _adamw_math(w_ref[...], g_ref[...], m_ref[...], v_ref[...])

    spec = pl.BlockSpec((tr, c), lambda i: (i, 0))
    shape = jax.ShapeDtypeStruct((r, c), F32)
    return pl.pallas_call(
        body, name=name, grid=(r // tr,), in_specs=[spec] * 4, out_specs=[spec] * 3,
        out_shape=[shape] * 3, compiler_params=_cp(),
    )(w, g, m, v)


def _position():
    return lax.axis_index("x"), lax.axis_index("y"), lax.axis_index("c")


def all_gather_many(shards, *, name):
    n = len(shards)

    def body(*refs):
        x_refs, out_refs, token_ref = refs[:n], refs[n:2 * n], refs[2 * n]
        send_sems, recv_sems, local_sems = refs[2 * n + 1:]
        x, y, c = _position()
        me, sibling = (x, y, c), (x, y, 1 - c)
        chips = [(1 - x, y), (x, 1 - y), (1 - x, 1 - y)]
        token_ref[...] = jnp.zeros_like(token_ref)

        def rows(t, px, py, pc):
            return out_refs[t].at[4 * px + 2 * py + pc]

        def copy(t, k, block, to, src=None):
            return pltpu.make_async_remote_copy(
                src_ref=rows(t, *block) if src is None else src, dst_ref=rows(t, *block),
                send_sem=send_sems.at[7 * t + k], recv_sem=recv_sems.at[7 * t + k], device_id=to, device_id_type=MESH)

        mine = [pltpu.make_async_copy(x_refs[t], rows(t, *me), local_sems.at[t]) for t in range(n)]
        for cp in mine:
            cp.start()
        first = []
        for t in range(n):
            first.append(copy(t, 0, me, sibling, src=x_refs[t]))
            first += [copy(t, 1 + j, me, (*chip, c), src=x_refs[t]) for j, chip in enumerate(chips)]
        for cp in first:
            cp.start()
        passed = []
        for t in range(n):
            for j, chip in enumerate(chips):
                copy(t, 1 + j, (*chip, c), me).wait_recv()
                passed.append(copy(t, 4 + j, (*chip, c), sibling))
                passed[-1].start()
        for t in range(n):
            copy(t, 0, sibling, me).wait_recv()
            for j, chip in enumerate(chips):
                copy(t, 4 + j, (*chip, 1 - c), me).wait_recv()
        for cp in first + passed:
            cp.wait_send()
        for cp in mine:
            cp.wait()

    hbm = pl.BlockSpec(memory_space=pltpu.HBM)
    out = pl.pallas_call(
        body, name=name,
        out_shape=[jax.ShapeDtypeStruct((N_DEV,) + s.shape, s.dtype) for s in shards] + [jax.ShapeDtypeStruct((8, LANES), F32)],
        in_specs=[hbm] * n, out_specs=[hbm] * n + [pl.BlockSpec(memory_space=pltpu.VMEM)],
        scratch_shapes=[pltpu.SemaphoreType.DMA((7 * n,)), pltpu.SemaphoreType.DMA((7 * n,)), pltpu.SemaphoreType.DMA((n,))],
        compiler_params=_cp(),
    )(*shards)
    return out[:n], out[n]


_HBM = pl.BlockSpec(memory_space=pltpu.HBM)
_SEM = pl.BlockSpec(memory_space=pltpu.SEMAPHORE)
_EFFECT = pltpu.SideEffectType.DATAFLOW_SIDE_EFFECTING


def _split_copies(src_refs, land_refs, send_sems, recv_sems, plan, n_copies):
    copies = []
    for t, (src_ref, land_ref) in enumerate(zip(src_refs, land_refs)):
        for k in range(n_copies):
            s, d, to = plan(src_ref, land_ref, k)
            copies.append(pltpu.make_async_remote_copy(
                src_ref=s, dst_ref=d, send_sem=send_sems.at[n_copies * t + k], recv_sem=recv_sems.at[n_copies * t + k],
                device_id=to, device_id_type=MESH))
    return copies


def split_start(srcs, lands, plan, n_copies, *, name):
    n = len(srcs)

    def body(*refs):
        src_refs, land_refs, send_sems, recv_sems, token = refs[:n], refs[n:2 * n], refs[2 * n], refs[2 * n + 1], refs[-1]
        for cp in _split_copies(src_refs, land_refs, send_sems, recv_sems, plan, n_copies):
            cp.start()
        token[...] = jnp.zeros_like(token)

    both = list(srcs) + list(lands)
    out = pl.pallas_call(
        body, name=name,
        out_shape=(pltpu.SemaphoreType.DMA((n_copies * n,)), pltpu.SemaphoreType.DMA((n_copies * n,)),
                   *[pltpu.HBM(a.shape, a.dtype) for a in both], jax.ShapeDtypeStruct((8, LANES), F32)),
        in_specs=(_HBM,) * (2 * n), out_specs=(_SEM, _SEM) + (_HBM,) * (2 * n) + (pl.BlockSpec(memory_space=pltpu.VMEM),),
        input_output_aliases={i: 2 + i for i in range(2 * n)},
        compiler_params=pltpu.CompilerParams(has_side_effects=_EFFECT),
    )(*[pltpu.with_memory_space_constraint(a, pltpu.HBM) for a in both])
    return out[0], out[1], list(out[2:2 + n]), list(out[2 + n:2 + 2 * n]), out[-1]


def split_wait(started, after, plan, n_copies, *, name):
    send_sems, recv_sems, srcs, lands, _ = started
    n = len(srcs)

    def body(*refs):
        src_refs, land_refs, send_sems, recv_sems = refs[:n], refs[n:2 * n], refs[2 * n], refs[2 * n + 1]
        for cp in _split_copies(src_refs, land_refs, send_sems, recv_sems, plan, n_copies):
            cp.wait_send()
            cp.wait_recv()

    both = list(srcs) + list(lands)
    out = pl.pallas_call(
        body, name=name,
        out_shape=tuple(pltpu.HBM(a.shape, a.dtype) for a in both),
        in_specs=(_HBM,) * (2 * n) + (_SEM, _SEM, pl.BlockSpec(memory_space=pl.ANY)), out_specs=(_HBM,) * (2 * n),
        input_output_aliases={i: i for i in range(2 * n)},
        compiler_params=pltpu.CompilerParams(has_side_effects=_EFFECT),
    )(*both, send_sems, recv_sems, after)
    return list(out[:n]), list(out[n:])


def _other_chips(x, y):
    return [(1 - x, y), (x, 1 - y), (1 - x, 1 - y)]


def _remote(src, dst, send_sem, recv_sem, to):
    return pltpu.make_async_remote_copy(src_ref=src, dst_ref=dst, send_sem=send_sem, recv_sem=recv_sem,
                                        device_id=to, device_id_type=MESH)


def gather_start(groups, *, name):
    counts = [len(shards) for shards, _ in groups]
    flat = [a for shards, _ in groups for a in shards] + [a for _, lands in groups for a in lands]
    n_all, n_groups = sum(counts), len(groups)

    def body(*refs):
        s_refs, l_refs = refs[:n_all], refs[n_all:2 * n_all]
        sems = refs[2 * n_all:2 * n_all + 3 * n_groups]
        x, y, c = _position()
        me = 4 * x + 2 * y + c
        at = 0
        for gi, n in enumerate(counts):
            send, recv_sibling, recv_ici = sems[3 * gi:3 * gi + 3]
            for t in range(n):
                src, dst = s_refs[at + t], l_refs[at + t].at[me]
                _remote(src, dst, send.at[4 * t], recv_sibling.at[t], (x, y, 1 - c)).start()
                for j, chip in enumerate(_other_chips(x, y)):
                    _remote(src, dst, send.at[4 * t + 1 + j], recv_ici.at[3 * t + j], (*chip, c)).start()
            at += n
        refs[-1][...] = jnp.zeros_like(refs[-1])

    sem_shapes = [pltpu.SemaphoreType.DMA((k * n,)) for n in counts for k in (4, 1, 3)]
    out = pl.pallas_call(
        body, name=name,
        out_shape=(*sem_shapes, *[pltpu.HBM(a.shape, a.dtype) for a in flat], jax.ShapeDtypeStruct((8, LANES), F32)),
        in_specs=(_HBM,) * (2 * n_all),
        out_specs=(_SEM,) * (3 * n_groups) + (_HBM,) * (2 * n_all) + (pl.BlockSpec(memory_space=pltpu.VMEM),),
        input_output_aliases={i: 3 * n_groups + i for i in range(2 * n_all)},
        compiler_params=pltpu.CompilerParams(has_side_effects=_EFFECT),
    )(*[pltpu.with_memory_space_constraint(a, pltpu.HBM) for a in flat])
    thru = out[3 * n_groups:-1]
    states, at = [], 0
    for gi, n in enumerate(counts):
        states.append(dict(shards=list(thru[at:at + n]), lands=list(thru[n_all + at:n_all + at + n]),
                           send=out[3 * gi], recv_sibling=out[3 * gi + 1], recv_ici=out[3 * gi + 2]))
        at += n
    return states, out[-1]


def gather_forward(states, after, *, name):
    counts = [len(s["lands"]) for s in states]
    flat = [a for s in states for a in s["lands"]]
    n_all, n_groups = sum(counts), len(states)

    def body(*refs):
        l_refs = refs[:n_all]
        recv_ici = refs[n_all:n_all + n_groups]
        fwd = refs[n_all + n_groups + 1:n_all + n_groups + 1 + 2 * n_groups]
        x, y, c = _position()
        at = 0
        for gi, n in enumerate(counts):
            fwd_send, fwd_recv = fwd[2 * gi], fwd[2 * gi + 1]
            for t in range(n):
                for j, (px, py) in enumerate(_other_chips(x, y)):
                    block = l_refs[at + t].at[4 * px + 2 * py + c]
                    _remote(block, block, fwd_send.at[3 * t + j], recv_ici[gi].at[3 * t + j], (px, py, c)).wait_recv()
                    _remote(block, block, fwd_send.at[3 * t + j], fwd_recv.at[3 * t + j], (x, y, 1 - c)).start()
            at += n
        refs[-1][...] = jnp.zeros_like(refs[-1])

    sem_shapes = [pltpu.SemaphoreType.DMA((3 * n,)) for n in counts for _ in range(2)]
    out = pl.pallas_call(
        body, name=name,
        out_shape=(*sem_shapes, *[pltpu.HBM(a.shape, a.dtype) for a in flat], jax.ShapeDtypeStruct((8, LANES), F32)),
        in_specs=(_HBM,) * n_all + (_SEM,) * n_groups + (pl.BlockSpec(memory_space=pl.ANY),),
        out_specs=(_SEM,) * (2 * n_groups) + (_HBM,) * n_all + (pl.BlockSpec(memory_space=pltpu.VMEM),),
        input_output_aliases={i: 2 * n_groups + i for i in range(n_all)},
        compiler_params=pltpu.CompilerParams(has_side_effects=_EFFECT),
    )(*flat, *[s["recv_ici"] for s in states], after)
    at = 0
    for gi, (s, n) in enumerate(zip(states, counts)):
        s.update(fwd_send=out[2 * gi], fwd_recv=out[2 * gi + 1], lands=list(out[2 * n_groups + at:2 * n_groups + at + n]))
        at += n
    return out[-1]


def gather_finish(state, after, *, name):
    n = len(state["lands"])

    def body(*refs):
        s_refs, l_refs = refs[:n], refs[n:2 * n]
        send, recv_sibling, fwd_send, fwd_recv = refs[2 * n:2 * n + 4]
        x, y, c = _position()
        me = 4 * x + 2 * y + c
        for t in range(n):
            own = l_refs[t].at[me]
            _remote(s_refs[t], own, send.at[4 * t], recv_sibling.at[t], (x, y, 1 - c)).wait_send()
            _remote(s_refs[t], l_refs[t].at[4 * x + 2 * y + 1 - c], send.at[4 * t], recv_sibling.at[t], (x, y, 1 - c)).wait_recv()
            for j, (px, py) in enumerate(_other_chips(x, y)):
                _remote(s_refs[t], own, send.at[4 * t + 1 + j], recv_sibling.at[t], (px, py, c)).wait_send()
                mine, theirs = l_refs[t].at[4 * px + 2 * py + c], l_refs[t].at[4 * px + 2 * py + 1 - c]
                _remote(mine, mine, fwd_send.at[3 * t + j], fwd_recv.at[3 * t + j], (x, y, 1 - c)).wait_send()
                _remote(theirs, theirs, fwd_send.at[3 * t + j], fwd_recv.at[3 * t + j], (x, y, 1 - c)).wait_recv()

    both = state["shards"] + state["lands"]
    out = pl.pallas_call(
        body, name=name,
        out_shape=tuple(pltpu.HBM(a.shape, a.dtype) for a in both),
        in_specs=(_HBM,) * (2 * n) + (_SEM,) * 4 + (pl.BlockSpec(memory_space=pl.ANY),), out_specs=(_HBM,) * (2 * n),
        input_output_aliases={i: i for i in range(2 * n)},
        compiler_params=pltpu.CompilerParams(has_side_effects=_EFFECT),
    )(*both, state["send"], state["recv_sibling"], state["fwd_send"], state["fwd_recv"], after)
    return list(out[n:])


def _all_peers_plan(src_ref, land_ref, k):
    x, y, c = _position()
    bits = k + 1
    peer = ((1 - x) if bits & 4 else x, (1 - y) if bits & 2 else y, (1 - c) if bits & 1 else c)
    return src_ref, land_ref.at[4 * x + 2 * y + c], peer


def _sibling_plan(src_ref, land_ref, k):
    x, y, c = _position()
    return src_ref.at[2 * k + (1 - c)], land_ref.at[k], (x, y, 1 - c)


def _chips_plan(src_ref, land_ref, j):
    x, y, c = _position()
    px, py = _other_chips(x, y)[j]
    return src_ref.at[j], land_ref.at[j], (px, py, c)


SUM_STEPS = 2


def sum_for_chips(parts, from_sibling, ck_idx, *, name):
    n = len(parts)

    def body(ck_ref, *refs):
        del ck_ref
        for t in range(n):
            refs[2 * n + t][...] = (refs[t][...] + refs[n + t][...]).astype(BF16)

    def blk(a):
        return (None, a.shape[1] // SUM_STEPS, a.shape[2])

    return pl.pallas_call(
        body, name=name,
        grid_spec=pltpu.PrefetchScalarGridSpec(
            num_scalar_prefetch=1, grid=(3, SUM_STEPS),
            in_specs=[pl.BlockSpec(blk(a), lambda j, i, ck: (2 * ck[1 + j] + ck[0], i, 0)) for a in parts]
            + [pl.BlockSpec(blk(a), lambda j, i, ck: (ck[1 + j], i, 0)) for a in from_sibling],
            out_specs=[pl.BlockSpec(blk(a), lambda j, i, ck: (j, i, 0)) for a in from_sibling]),
        out_shape=[jax.ShapeDtypeStruct((3,) + a.shape[1:], BF16) for a in from_sibling], compiler_params=_cp(),
    )(ck_idx, *parts, *from_sibling)


def sum_final(parts, from_sibling, from_chips, kc_idx, *, name):
    n = len(parts)

    def body(kc_ref, *refs):
        del kc_ref
        for t in range(n):
            p, s, a, b, d = (refs[j * n + t] for j in range(5))
            refs[5 * n + t][...] = (((p[...] + s[...]) + a[...].astype(F32)) + b[...].astype(F32)) + d[...].astype(F32)

    def blk(a):
        return (None, a.shape[1] // SUM_STEPS, a.shape[2])

    def chip_specs(j):
        return [pl.BlockSpec(blk(a), lambda i, kc: (j, i, 0)) for a in from_chips]

    return pl.pallas_call(
        body, name=name,
        grid_spec=pltpu.PrefetchScalarGridSpec(
            num_scalar_prefetch=1, grid=(SUM_STEPS,),
            in_specs=[pl.BlockSpec(blk(a), lambda i, kc: (2 * kc[0] + kc[1], i, 0)) for a in parts]
            + [pl.BlockSpec(blk(a), lambda i, kc: (kc[0], i, 0)) for a in from_sibling]
            + chip_specs(0) + chip_specs(1) + chip_specs(2),
            out_specs=[pl.BlockSpec(blk(a)[1:], lambda i, kc: (i, 0)) for a in parts]),
        out_shape=[jax.ShapeDtypeStruct(a.shape[1:], F32) for a in parts], compiler_params=_cp(),
    )(kc_idx, *parts, *from_sibling, *from_chips, *from_chips, *from_chips)


BIG = (
    ("w_in", IN_COLS, True), ("w_out", D_MODEL, False), ("wq_x", D_MODEL, False), ("wkv_x", 2 * D_MODEL, True),
    ("wo_x", D_MODEL, False), ("w_gate_up", 2 * D_FF, True), ("w_down", D_FF, False),
)

SMALL = ("norm_mix_g", "q_norm_g", "k_norm_g", "sinks", "conv_b", "conv_ln_g", "conv_ln_b",
         "norm_x_g", "norm_mem_g", "xq_norm_g", "xk_norm_g", "norm_ffn_g")


ADAMW_STEPS = 8


def adamw_many(ws, gs, ms, vs, *, name):
    n = len(ws)

    def body(*refs):
        for t in range(n):
            w, g, m, v = (refs[j * n + t] for j in range(4))
            d_ref, nm_ref, nv_ref = (refs[(4 + j) * n + t] for j in range(3))
            d_ref[...], nm_ref[...], nv_ref[...] = _adamw_math(w[...], g[...], m[...], v[...])

    specs = [pl.BlockSpec((a.shape[0] // ADAMW_STEPS, a.shape[1]), lambda i: (i, 0)) for a in ws]
    shapes = [jax.ShapeDtypeStruct(a.shape, F32) for a in ws]
    out = pl.pallas_call(
        body, name=name, grid=(ADAMW_STEPS,), in_specs=specs * 4, out_specs=specs * 3, out_shape=shapes * 3,
        compiler_params=_cp(),
    )(*ws, *gs, *ms, *vs)
    return out[:n], out[n:2 * n], out[2 * n:]


def _adamw_math(w, g, m, v):
    m2 = ADAM_B1 * m + (1.0 - ADAM_B1) * g
    v2 = ADAM_B2 * v + (1.0 - ADAM_B2) * jnp.square(g)
    m_hat = m2 / (1.0 - ADAM_B1 ** ADAM_STEP)
    v_hat = v2 / (1.0 - ADAM_B2 ** ADAM_STEP)
    return -ADAM_LR * (m_hat / (jnp.sqrt(v_hat) + ADAM_EPS) + ADAM_WD * w), m2, v2


def _small_rows(per_layer_shape):
    return 1 if len(per_layer_shape) == 1 else per_layer_shape[0]


def pack_small(parts, shapes):
    blocks = []
    for per_layer, sh in zip(parts, shapes):
        for g in per_layer:
            g = g.reshape(_small_rows(sh), sh[-1])
            blocks.append(jnp.pad(g, ((0, 0), (0, D_MODEL - sh[-1]))))
    rows = sum(b.shape[0] for b in blocks)
    blocks.append(jnp.zeros((-rows % 8, D_MODEL), F32))
    return jnp.concatenate(blocks, axis=0)


def update_small(gathered, shapes, weights, moments_m, moments_v, n_update):
    n_all = len(shapes)

    def body(*refs):
        g_ref = refs[0]
        w_refs, m_refs, v_refs = (refs[1 + j * n_update:1 + (j + 1) * n_update] for j in range(3))
        out = refs[1 + 3 * n_update:]
        grad_refs = out[:n_all]
        d_refs, nm_refs, nv_refs = (out[n_all + j * n_update:n_all + (j + 1) * n_update] for j in range(3))
        at = 0
        for p, sh in enumerate(shapes):
            rows, lanes = _small_rows(sh), sh[-1]
            for l in range(DEPTH):
                g = g_ref[0, at:at + rows, 0:lanes]
                for k in range(1, N_DEV):
                    g = g + g_ref[k, at:at + rows, 0:lanes]
                at += rows
                here = (slice(l, l + 1),) + (slice(None),) * (len(sh) - 1) if len(sh) == 1 else (l,)
                grad_refs[p][here] = g
                if p < n_update:
                    d, m2, v2 = _adamw_math(w_refs[p][here], g, m_refs[p][here], v_refs[p][here])
                    d_refs[p][here] = d
                    nm_refs[p][here] = m2
                    nv_refs[p][here] = v2

    full = [jax.ShapeDtypeStruct((DEPTH,) + tuple(sh), F32) for sh in shapes]
    out = pl.pallas_call(
        body, name="update_small", out_shape=full + full[:n_update] * 3, compiler_params=_cp(),
    )(gathered, *weights, *moments_m, *moments_v)
    return (out[:n_all], out[n_all:n_all + n_update], out[n_all + n_update:n_all + 2 * n_update],
            out[n_all + 2 * n_update:])


WEIGHT_GROUPS = {"in": ("w_in",), "mid": ("w_out", "wq_x", "wkv_x", "wo_x"), "ffn": ("w_gate_up", "w_down")}


def _layer_fwd(x0, mem, weights_of, s, reached, target=None):
    w = dict(weights_of("in", x0))
    h0, u = norm_proj(x0, s["norm_mix_g"], w["w_in"])
    mixed, lse = swa_fwd(u, s["q_norm_g"], s["k_norm_g"], s["sinks"])
    reached("attn", mixed)
    mixed, conv_y = conv_fwd(u, mixed, s["conv_w"], s["conv_b"], s["conv_ln_g"], s["conv_ln_b"])
    w.update(weights_of("mid", conv_y))
    memn, kv = kv_fwd(mem, s["norm_mem_g"], w["wkv_x"])
    x1, h1, qx, o, x2, h2 = mid_fwd(mixed, x0, w["w_out"], s["norm_x_g"], w["wq_x"], kv, s["xq_norm_g"], s["xk_norm_g"],
                                    w["wo_x"], s["norm_ffn_g"])
    reached("mid", x2)
    w.update(weights_of("ffn", x2))
    gu, a, *out = ffn_fwd(h2, x2, w["w_gate_up"], w["w_down"], target)
    saved = dict(x0=x0, h0=h0, u=u, lse=lse, conv_y=conv_y, mixed=mixed, x1=x1, h1=h1, qx=qx, memn=memn, kv=kv, o=o, x2=x2, h2=h2,
                 gu=gu, a=a)
    return out, saved, w


def _ordered_after(a, token):
    return a if token is None else a + token[0, 0]


def _layer_bwd(dx3, mem, w, s, sv, token, stage_done):
    gs = {}
    dgu, dx2, dg = ffn_bwd(dx3, sv["gu"], sv["x2"], _ordered_after(s["norm_ffn_g"], token), w["w_down"], w["w_gate_up"])
    gs["norm_ffn_g"] = dg
    gb = {"w_down": mm_tn(sv["a"], dx3, name="mm_dw_down")}
    gb["w_gate_up"] = mm_tn(dgu, sv["h2"], tk=dgu.shape[0], name="mm_dw_gate_up")
    token = stage_done("ffn", gb, gb["w_gate_up"])

    gb = {}
    dq, dx1, dmixed, dkv, dqg, dkg, dg = mid_bwd(dx2, sv["qx"], sv["kv"], s["xq_norm_g"], s["xk_norm_g"], sv["x1"],
                                                 _ordered_after(s["norm_x_g"], token), w["wo_x"], w["wq_x"], w["w_out"])
    gs["xq_norm_g"], gs["xk_norm_g"], gs["norm_x_g"] = dqg, dkg, dg
    gb["wo_x"] = mm_tn(sv["o"], dx2, name="mm_dwo")
    gb["wq_x"] = mm_tn(sv["h1"], dq, name="mm_dwq")
    gb["wkv_x"], gs["norm_mem_g"] = kv_bwd(dkv, sv["memn"], mem, w["wkv_x"])
    gb["w_out"] = mm_tn(sv["mixed"], dx1, name="mm_dw_out")
    token = stage_done("mid", gb, gb["w_out"])

    du, dqg, dkg, dsinks = swa_bwd(sv["u"], sv["lse"], dmixed, _ordered_after(s["q_norm_g"], token), s["k_norm_g"],
                                   s["sinks"])
    gs["q_norm_g"], gs["k_norm_g"], gs["sinks"] = dqg[0, :HEAD_DIM], dkg[0, :HEAD_DIM], dsinks[0, :N_Q_HEADS]
    token = stage_done("attn", {}, dqg)
    du, dconv_w, dvec = conv_bwd(sv["u"], sv["conv_y"], dmixed, du, s["conv_w"], _ordered_after(s["conv_ln_g"], token),
                                 s["conv_ln_b"])
    gs["conv_w"] = dconv_w[:CONV_K]
    gs["conv_b"], gs["conv_ln_g"], gs["conv_ln_b"] = dvec[0], dvec[1], dvec[2]
    dw_in = mm_tn(du, sv["h0"], tk=2048, name="mm_dw_in")
    token = stage_done("in", {"w_in": dw_in}, dw_in)
    dx0, dg = in_bwd(du, w["w_in"], sv["x0"], _ordered_after(s["norm_mix_g"], token), dx1)
    gs["norm_mix_g"] = dg
    token = stage_done("mix", {}, dx0)
    return dx0, gs, token


def _local_step(x, mem, target, weights_of, reached, smalls, stage_done):
    saved, weights = [], []
    out = [x]
    for l in range(DEPTH):
        out, sv, w = _layer_fwd(out[0], mem, functools.partial(weights_of, l), smalls[l], functools.partial(reached, l),
                                target if l == DEPTH - 1 else None)
        saved.append(sv)
        weights.append(w)
    dx, loss_part = out
    gss, token = [None] * DEPTH, None
    for l in reversed(range(DEPTH)):
        dx, gss[l], token = _layer_bwd(dx, mem, weights[l], smalls[l], saved[l], token,
                                       functools.partial(stage_done, l))
    return loss_part[0, 0], dx, gss


def kernel(x, mem, norm_mix_g, w_in, q_norm_g, k_norm_g, sinks, conv_w, conv_b, conv_ln_g, conv_ln_b, w_out, norm_x_g, norm_mem_g, wq_x, wkv_x, xq_norm_g, xk_norm_g, wo_x, norm_ffn_g, w_gate_up, w_down, loss_target, m_norm_mix_g, m_w_in, m_q_norm_g, m_k_norm_g, m_sinks, m_conv_w, m_conv_b, m_conv_ln_g, m_conv_ln_b, m_w_out, m_norm_x_g, m_norm_mem_g, m_wq_x, m_wkv_x, m_xq_norm_g, m_xk_norm_g, m_wo_x, m_norm_ffn_g, m_w_gate_up, m_w_down, v_norm_mix_g, v_w_in, v_q_norm_g, v_k_norm_g, v_sinks, v_conv_w, v_conv_b, v_conv_ln_g, v_conv_ln_b, v_w_out, v_norm_x_g, v_norm_mem_g, v_wq_x, v_wkv_x, v_xq_norm_g, v_xk_norm_g, v_wo_x, v_norm_ffn_g, v_w_gate_up, v_w_down):
    P = dict(norm_mix_g=norm_mix_g, w_in=w_in, q_norm_g=q_norm_g, k_norm_g=k_norm_g, sinks=sinks, conv_w=conv_w, conv_b=conv_b,
             conv_ln_g=conv_ln_g, conv_ln_b=conv_ln_b, w_out=w_out, norm_x_g=norm_x_g, norm_mem_g=norm_mem_g, wq_x=wq_x,
             wkv_x=wkv_x, xq_norm_g=xq_norm_g, xk_norm_g=xk_norm_g, wo_x=wo_x, norm_ffn_g=norm_ffn_g, w_gate_up=w_gate_up,
             w_down=w_down)
    M = dict(norm_mix_g=m_norm_mix_g, w_in=m_w_in, q_norm_g=m_q_norm_g, k_norm_g=m_k_norm_g, sinks=m_sinks, conv_w=m_conv_w,
             conv_b=m_conv_b, conv_ln_g=m_conv_ln_g, conv_ln_b=m_conv_ln_b, w_out=m_w_out, norm_x_g=m_norm_x_g,
             norm_mem_g=m_norm_mem_g, wq_x=m_wq_x, wkv_x=m_wkv_x, xq_norm_g=m_xq_norm_g, xk_norm_g=m_xk_norm_g, wo_x=m_wo_x,
             norm_ffn_g=m_norm_ffn_g, w_gate_up=m_w_gate_up, w_down=m_w_down)
    V = dict(norm_mix_g=v_norm_mix_g, w_in=v_w_in, q_norm_g=v_q_norm_g, k_norm_g=v_k_norm_g, sinks=v_sinks, conv_w=v_conv_w,
             conv_b=v_conv_b, conv_ln_g=v_conv_ln_g, conv_ln_b=v_conv_ln_b, w_out=v_w_out, norm_x_g=v_norm_x_g,
             norm_mem_g=v_norm_mem_g, wq_x=v_wq_x, wkv_x=v_wkv_x, xq_norm_g=v_xq_norm_g, xk_norm_g=v_xk_norm_g, wo_x=v_wo_x,
             norm_ffn_g=v_norm_ffn_g, w_gate_up=v_w_gate_up, w_down=v_w_down)
    order = ["norm_mix_g", "w_in", "q_norm_g", "k_norm_g", "sinks", "conv_w", "conv_b", "conv_ln_g", "conv_ln_b", "w_out",
             "norm_x_g", "norm_mem_g", "wq_x", "wkv_x", "xq_norm_g", "xk_norm_g", "wo_x", "norm_ffn_g", "w_gate_up", "w_down"]
    xi, yi, ci = _position()
    dev = 4 * xi + 2 * yi + ci
    x2d, mem2d, tgt2d = x[0], mem[0], loss_target[0]

    def travelling(name, l, transposed):
        a = P[name][l]
        return (a.T if transposed else a).astype(BF16)

    rows_of = {n: rows for n, rows, _ in BIG}
    transposed_of = {n: tr for n, _, tr in BIG}

    def whole(names, gathered):
        return {n: g.reshape(rows_of[n], D_MODEL) for n, g in zip(names, gathered)}

    cw = jnp.pad(conv_w.reshape(DEPTH * CONV_K, CONV_CH // N_DEV), ((0, 2), (0, LANES - CONV_CH // N_DEV)))
    (w_in0, cw_all), token0 = all_gather_many([travelling("w_in", 0, True), cw], name="ag_w_in0_conv_w")
    travel_order = [(0, "mid"), (0, "ffn"), (1, "in"), (1, "mid"), (1, "ffn")]
    travel_groups = []
    for l, group in travel_order:
        shards = [_ordered_after(travelling(n, l, transposed_of[n]), token0.astype(BF16)) for n in WEIGHT_GROUPS[group]]
        lands = [lax.dynamic_update_slice(lax.empty((N_DEV,) + s.shape, BF16), s[None], (dev, 0, 0)) for s in shards]
        travel_groups.append((shards, lands))
    travel_states, travel_token = gather_start(travel_groups, name="ag_weights_start")
    travelling_state = dict(zip(travel_order, travel_states))
    forward_at = {(0, "attn"): [(0, "mid")], (0, "mid"): [(0, "ffn"), (1, "in")], (1, "attn"): [(1, "mid"), (1, "ffn")]}

    def reached(l, stage, marker):
        keys = forward_at.get((l, stage))
        if keys:
            gather_forward([travelling_state[k] for k in keys], marker,
                           name="ag_weights_forward_" + "_".join(f"{g}{ll}" for ll, g in keys))

    def weights_of(l, group, marker):
        if (l, group) == (0, "in"):
            return whole(WEIGHT_GROUPS[group], [w_in0])
        gathered = gather_finish(travelling_state[(l, group)], marker, name=f"ag_weights_finish_{group}{l}")
        return whole(WEIGHT_GROUPS[group], gathered)

    cw_full = cw_all[:, :DEPTH * CONV_K, :CONV_CH // N_DEV].reshape(N_DEV, DEPTH, CONV_K, CONV_CH // N_DEV)
    cw_full = jnp.transpose(cw_full, (1, 2, 0, 3)).reshape(DEPTH, CONV_K, CONV_CH)
    smalls = []
    for l in range(DEPTH):
        sl = {n: P[n][l] if n == "sinks" else P[n][l:l + 1] for n in SMALL}
        sl["conv_w"] = jnp.pad(cw_full[l], ((0, CONV_HALO - CONV_K), (0, 0)))
        smalls.append(sl)
    smalls[0]["norm_mix_g"] = _ordered_after(smalls[0]["norm_mix_g"], travel_token)

    ck_idx = jnp.stack([ci] + [2 * px + py for px, py in _other_chips(xi, yi)]).astype(jnp.int32)
    kc_idx = jnp.stack([2 * xi + yi, ci]).astype(jnp.int32)
    got, flight, reduced = {}, {}, {}

    def as_parts(gb):
        keys = sorted(gb)
        return keys, [gb[k].reshape(N_DEV, rows_of[k[1]] // N_DEV, D_MODEL) for k in keys]

    def lands_like(parts, blocks, dtype):
        return [lax.empty((blocks,) + p.shape[1:], dtype) for p in parts]

    def to_sibling(group, gb):
        keys, parts = as_parts(gb)
        flight[group] = (keys, split_start(parts, lands_like(parts, 4, F32), _sibling_plan, 4,
                                           name=f"rs_sibling_{group}_start"))
        return flight[group][1][4]

    def to_chips(group, marker):
        keys, started = flight[group]
        parts, from_sibling = split_wait(started, marker, _sibling_plan, 4, name=f"rs_sibling_{group}_wait")
        chip_sums = sum_for_chips(parts, from_sibling, ck_idx, name=f"rs_sum_for_chips_{group}")
        started = split_start(chip_sums, lands_like(parts, 3, BF16), _chips_plan, 3, name=f"rs_chips_{group}_start")
        flight[group] = (keys, parts, from_sibling, started)
        return started[4]

    def finish(group, marker):
        keys, parts, from_sibling, started = flight[group]
        _, from_chips = split_wait(started, marker, _chips_plan, 3, name=f"rs_chips_{group}_wait")
        reduced.update(zip(keys, sum_final(parts, from_sibling, from_chips, kc_idx, name=f"rs_sum_final_{group}")))

    def stage_done(l, stage, gb, marker):
        gb = {(l, n): g for n, g in gb.items()}
        if l == 1:
            got.update(gb)
            return to_sibling("l1", got) if stage == "mix" else None
        if stage == "ffn":
            return to_chips("l1", marker) + to_sibling("ffn", gb)
        if stage == "mid":
            return to_chips("ffn", marker) + to_sibling("mid", gb)
        if stage == "attn":
            return to_chips("mid", marker)
        if stage == "in":
            return to_sibling("in", gb)
        to_chips("in", marker)
        for group in ("l1", "ffn", "mid"):
            finish(group, marker)
        return None

    loss_part, grad_x, gss = _local_step(x2d, mem2d, tgt2d, weights_of, reached, smalls, stage_done)
    loss = lax.psum(loss_part, ("x", "y", "c"))

    small_names = SMALL + ("conv_w",)
    small_shapes = [(CONV_K, CONV_CH) if n == "conv_w" else P[n].shape[1:] for n in small_names]
    small_parts = pack_small([[gss[l][n] for l in range(DEPTH)] for n in small_names], small_shapes)
    small_land = lax.dynamic_update_slice(lax.empty((N_DEV,) + small_parts.shape, F32), small_parts[None], (dev, 0, 0))
    small_flight = split_start([small_parts], [small_land], _all_peers_plan, N_DEV - 1, name="ag_small_grads_start")

    grads, delta, new_m, new_v = {}, {}, {}, {}

    def update(names):
        two_d = lambda n, a: a.reshape(P[n].shape[0] * P[n].shape[1], P[n].shape[2])
        for n in names:
            grads[n] = jnp.stack([reduced[(l, n)].T if transposed_of[n] else reduced[(l, n)] for l in range(DEPTH)])
        d_, m_, v_ = adamw_many(*[[two_d(n, src[n]) for n in names] for src in (P, grads, M, V)],
                                name="adamw_" + "_".join(names))
        for i, n in enumerate(names):
            delta[n], new_m[n], new_v[n] = (a[i].reshape(P[n].shape) for a in (d_, m_, v_))

    update([n for n, _, _ in BIG if n != "w_in"])
    finish("in", delta["w_down"])
    update(["w_in"])
    small_all = split_wait(small_flight, delta["w_in"], _all_peers_plan, N_DEV - 1, name="ag_small_grads_wait")[1][0]
    g_, d_, m_, v_ = update_small(small_all, small_shapes, [P[n] for n in SMALL], [M[n] for n in SMALL],
                                  [V[n] for n in SMALL], len(SMALL))
    for i, n in enumerate(SMALL):
        grads[n], delta[n], new_m[n], new_v[n] = g_[i], d_[i], m_[i], v_[i]
    cols = CONV_CH // N_DEV
    grads["conv_w"] = lax.dynamic_slice_in_dim(g_[-1], dev * cols, cols, axis=2)
    flat = lambda a: a.reshape(DEPTH * CONV_K, cols)
    d_, m_, v_ = adamw(flat(conv_w), flat(grads["conv_w"]), flat(m_conv_w), flat(v_conv_w), name="adamw_conv_w")
    delta["conv_w"], new_m["conv_w"], new_v["conv_w"] = (a.reshape(conv_w.shape) for a in (d_, m_, v_))

    return (loss, grad_x[None], *[grads[n] for n in order], *[delta[n] for n in order],
            *[new_m[n] for n in order], *[new_v[n] for n in order])
```
